```python
import math
import jax, jax.numpy as jnp
from jax import lax
import numpy as np

D_MODEL = 1024
BATCH = 2
SEQ = 8192
DEPTH = 1

D_MIX = D_MODEL
GLA_HEADS = 4
GLA_DV = D_MIX // 2 // GLA_HEADS
GLA_DK = GLA_DV // 2
GLA_RANK = 16
GLA_GATE_TAU = 16.0
GLA_CHUNK = 64
SWA_HEADS = 8
SWA_KV_HEADS = 2
SWA_HEAD_DIM = D_MIX // 2 // SWA_HEADS
SWA_WINDOW = 128
N_EXPERTS = 32
TOP_K = 4
D_FF = D_MODEL
SWIGLU_LIMIT = 7.0
SWIGLU_ALPHA = 1.702
MOE_BLOCK = 128
NORM_EPS = 1e-5

SPLIT_SIZES = (
    GLA_HEADS * GLA_DK,
    GLA_HEADS * GLA_DK,
    GLA_HEADS * GLA_DV,
    GLA_HEADS * GLA_DV,
    GLA_RANK,
    SWA_HEADS * SWA_HEAD_DIM,
    SWA_KV_HEADS * SWA_HEAD_DIM,
    SWA_KV_HEADS * SWA_HEAD_DIM,
)
D_IN = sum(SPLIT_SIZES)
D_CAT = GLA_HEADS * GLA_DV + SWA_HEADS * SWA_HEAD_DIM

kernel_name = "hymba_gla_swa_sink_moe"


def rmsnorm(x, g):
    xf = x.astype(jnp.float32)
    y = xf * lax.rsqrt(jnp.mean(xf * xf, axis=-1, keepdims=True) + NORM_EPS)
    return (y * g.astype(jnp.float32)).astype(x.dtype)


def alibi_slopes(n):
    return jnp.exp2(-8.0 * jnp.arange(1, n + 1, dtype=jnp.float32) / n)


def gla_chunked(q, k, v, log_a):
    B, L, H, dk = q.shape
    dv = v.shape[-1]
    C = GLA_CHUNK
    n = L // C

    def to_chunks(t):
        return jnp.moveaxis(t.astype(jnp.float32).reshape(B, n, C, H, t.shape[-1]), 1, 0)

    qc, kc, vc, gc = (to_chunks(t) for t in (q * (dk ** -0.5), k, v, log_a))
    causal = jnp.tril(jnp.ones((C, C), dtype=bool))

    def step(S, inp):
        qi, ki, vi, gi = inp
        b = jnp.cumsum(gi, axis=1)
        b_last = b[:, -1]
        q_e = qi * jnp.exp(b)
        k_e = ki * jnp.exp(-b)
        A = jnp.where(causal, jnp.einsum('bihd,bjhd->bhij', q_e, k_e), 0.0)
        o = (jnp.einsum('bhij,bjhv->bihv', A, vi)
             + jnp.einsum('bihd,bhdv->bihv', q_e, S))
        k_tail = ki * jnp.exp(b_last[:, None] - b)
        S = jnp.exp(b_last)[..., None] * S + jnp.einsum('bjhd,bjhv->bhdv', k_tail, vi)
        return S, o

    S0 = jnp.zeros((B, H, dk, dv), jnp.float32)
    _, o = lax.scan(step, S0, (qc, kc, vc, gc))
    return jnp.moveaxis(o, 0, 1).reshape(B, L, H, dv).astype(v.dtype)


def swa_with_sinks(q, k, v, sinks, slopes):
    B, L, Hq, dh = q.shape
    Hkv = k.shape[2]
    G = Hq // Hkv
    W = SWA_WINDOW
    nb = L // W
    qb = q.reshape(B, nb, W, Hkv, G, dh)

    def band(t):
        tb = t.reshape(B, nb, W, Hkv, dh)
        prev = jnp.concatenate([jnp.zeros_like(tb[:, :1]), tb[:, :-1]], axis=1)
        return jnp.concatenate([prev, tb], axis=2)

    kb, vb = band(k), band(v)
    s = jnp.einsum('bnqkgd,bnskd->bnkgqs', qb, kb).astype(jnp.float32) * (dh ** -0.5)
    rel = jnp.arange(W)[:, None] + W - jnp.arange(2 * W)[None, :]
    in_window = (rel >= 0) & (rel < W)
    exists = (jnp.arange(nb)[:, None] > 0) | (jnp.arange(2 * W)[None, :] >= W)
    valid = in_window[None] & exists[:, None, :]
    bias = -slopes.reshape(Hkv, G)[:, :, None, None] * rel.astype(jnp.float32)
    s = jnp.where(valid[None, :, None, None], s + bias, -jnp.inf)
    sink = jnp.broadcast_to(sinks.astype(jnp.float32).reshape(1, 1, Hkv, G, 1, 1),
                            s.shape[:-1] + (1,))
    p = jax.nn.softmax(jnp.concatenate([s, sink], axis=-1), axis=-1)[..., :-1]
    o = jnp.einsum('bnkgqs,bnskd->bnqkgd', p.astype(v.dtype), vb)
    return o.reshape(B, L, Hq * dh)


def moe(h, w_router, b_router, w_glu, b_glu, w_lin, b_lin, w_down, b_down):
    T, D = h.shape
    TK = T * TOP_K
    logits = (h @ w_router + b_router).astype(jnp.float32)
    top_val, top_idx = lax.top_k(logits, TOP_K)
    gate = jax.nn.softmax(top_val, axis=-1)

    flat_e = top_idx.reshape(-1)
    flat_tok = jnp.repeat(jnp.arange(T, dtype=jnp.int32), TOP_K)
    flat_w = gate.reshape(-1)
    order = jnp.argsort(flat_e)
    e_sorted = flat_e[order]
    counts = jnp.bincount(flat_e, length=N_EXPERTS)
    starts = jnp.cumsum(counts) - counts
    padded = (counts + MOE_BLOCK - 1) // MOE_BLOCK * MOE_BLOCK
    pends = jnp.cumsum(padded)
    pstarts = pends - padded
    dest = pstarts[e_sorted] + (jnp.arange(TK) - starts[e_sorted])

    n_blocks = -(-TK // MOE_BLOCK) + N_EXPERTS
    P = n_blocks * MOE_BLOCK
    tok_buf = jnp.zeros((P,), jnp.int32).at[dest].set(flat_tok[order])
    w_buf = jnp.zeros((P,), jnp.float32).at[dest].set(flat_w[order])
    block_expert = jnp.minimum(
        jnp.searchsorted(pends, jnp.arange(n_blocks) * MOE_BLOCK, side='right'), N_EXPERTS - 1)
    xb = h[tok_buf].reshape(n_blocks, MOE_BLOCK, D)

    def expert_block(args):
        xblk, e = args
        glu = jnp.minimum(xblk @ w_glu[e] + b_glu[e], SWIGLU_LIMIT)
        lin = jnp.clip(xblk @ w_lin[e] + b_lin[e], -SWIGLU_LIMIT, SWIGLU_LIMIT)
        hid = glu * jax.nn.sigmoid(SWIGLU_ALPHA * glu) * (lin + 1.0)
        return hid @ w_down[e] + b_down[e]

    y = lax.map(expert_block, (xb, block_expert)).reshape(P, D)
    return jnp.zeros_like(h).at[tok_buf].add(y * w_buf[:, None].astype(h.dtype))


def setup_inputs(seed: int = 0) -> dict:
    key = jax.random.key(seed)
    ks = jax.random.split(key, 20)
    f32 = jnp.float32

    def nrm(k, shape, scale):
        return jax.random.normal(k, shape, f32) * scale

    Ly = DEPTH
    return {
        "x": nrm(ks[0], (BATCH, SEQ, D_MODEL), 1.0),
        "norm1_g": 1.0 + nrm(ks[1], (Ly, D_MODEL), 0.02),
        "w_in": nrm(ks[2], (Ly, D_MODEL, D_IN), D_MODEL ** -0.5),
        "w_alpha_up": nrm(ks[3], (Ly, GLA_RANK, GLA_HEADS * GLA_DK), GLA_RANK ** -0.5),
        "b_alpha": nrm(ks[4], (Ly, GLA_HEADS * GLA_DK), 0.1) + 2.0,
        "gla_norm_g": 1.0 + nrm(ks[5], (Ly, GLA_DV), 0.02),
        "swa_sinks": nrm(ks[6], (Ly, SWA_HEADS), 0.5),
        "swa_norm_g": 1.0 + nrm(ks[7], (Ly, SWA_HEAD_DIM), 0.02),
        "w_out": nrm(ks[8], (Ly, D_CAT, D_MODEL), D_CAT ** -0.5),
        "norm2_g": 1.0 + nrm(ks[9], (Ly, D_MODEL), 0.02),
        "w_router": nrm(ks[10], (Ly, D_MODEL, N_EXPERTS), D_MODEL ** -0.5),
        "b_router": nrm(ks[11], (Ly, N_EXPERTS), 0.01),
        "w_glu": nrm(ks[12], (Ly, N_EXPERTS, D_MODEL, D_FF), D_MODEL ** -0.5),
        "b_glu": nrm(ks[13], (Ly, N_EXPERTS, D_FF), 0.01),
        "w_lin": nrm(ks[14], (Ly, N_EXPERTS, D_MODEL, D_FF), D_MODEL ** -0.5),
        "b_lin": nrm(ks[15], (Ly, N_EXPERTS, D_FF), 0.01),
        "w_down": nrm(ks[16], (Ly, N_EXPERTS, D_FF, D_MODEL), D_FF ** -0.5),
        "b_down": nrm(ks[17], (Ly, N_EXPERTS, D_MODEL), 0.01),
        "final_g": 1.0 + nrm(ks[18], (D_MODEL,), 0.02),
    }


def reference(x, norm1_g, w_in, w_alpha_up, b_alpha, gla_norm_g, swa_sinks, swa_norm_g,
              w_out, norm2_g, w_router, b_router, w_glu, b_glu, w_lin, b_lin,
              w_down, b_down, final_g):
    B, L, D = x.shape
    offsets = [int(o) for o in np.cumsum(SPLIT_SIZES)[:-1]]
    slopes = alibi_slopes(SWA_HEADS)
    h = x
    for l in range(DEPTH):
        u = rmsnorm(h, norm1_g[l])
        proj = u @ w_in[l]
        q_a, k_a, v_a, r_a, z_a, q_b, k_b, v_b = jnp.split(proj, offsets, axis=-1)
        log_a = jax.nn.log_sigmoid(
            (z_a @ w_alpha_up[l] + b_alpha[l]).astype(jnp.float32)) / GLA_GATE_TAU
        o_a = gla_chunked(q_a.reshape(B, L, GLA_HEADS, GLA_DK),
                          k_a.reshape(B, L, GLA_HEADS, GLA_DK),
                          v_a.reshape(B, L, GLA_HEADS, GLA_DV),
                          log_a.reshape(B, L, GLA_HEADS, GLA_DK))
        o_a = rmsnorm(o_a, gla_norm_g[l]).reshape(B, L, GLA_HEADS * GLA_DV) * jax.nn.silu(r_a)
        o_b = swa_with_sinks(q_b.reshape(B, L, SWA_HEADS, SWA_HEAD_DIM),
                             k_b.reshape(B, L, SWA_KV_HEADS, SWA_HEAD_DIM),
                             v_b.reshape(B, L, SWA_KV_HEADS, SWA_HEAD_DIM),
                             swa_sinks[l], slopes)
        o_b = rmsnorm(o_b.reshape(B, L, SWA_HEADS, SWA_HEAD_DIM),
                      swa_norm_g[l]).reshape(B, L, SWA_HEADS * SWA_HEAD_DIM)
        h = h + jnp.concatenate([o_a, o_b], axis=-1) @ w_out[l]
        hn = rmsnorm(h, norm2_g[l]).reshape(B * L, D)
        h = h + moe(hn, w_router[l], b_router[l], w_glu[l], b_glu[l], w_lin[l], b_lin[l],
                    w_down[l], b_down[l]).reshape(B, L, D)
    return rmsnorm(h, final_g)
```

```python
import functools

import numpy as np
import jax
import jax.numpy as jnp
from jax import lax
from jax.experimental import pallas as pl
from jax.experimental.pallas import tpu as pltpu

F32 = jnp.float32
BF16 = jnp.bfloat16
HIGHEST = lax.Precision.HIGHEST

D_MODEL = 1024
GLA_HEADS = 4
GLA_DK = 64
GLA_DV = 128
GLA_RANK = 16
GLA_GATE_TAU = 16.0
GLA_CHUNK = 64
SWA_HEADS = 8
SWA_KV_HEADS = 2
SWA_HEAD_DIM = 64
SWA_WINDOW = 128
N_EXPERTS = 32
TOP_K = 4
D_FF = 1024
SWIGLU_LIMIT = 7.0
SWIGLU_ALPHA = 1.702
NORM_EPS = 1e-5

LANES = 128
VMEM_LIMIT = 56 * 1024 * 1024

TM_PROJ = 512
TL_GLA = 256
TM_ROUTE = 256
TM_EXPERT = 256
TM_COMBINE = 256
DISPATCH_PAIRS = 1024

NEG_BIG = -1e30

C_QA, C_KA, C_VA, C_RA, C_QB, C_KB, C_KBS, C_VB, C_VBS, C_Z, C_END = (
    0, 512, 1024, 1536, 2048, 2560, 2688, 2816, 2944, 3072, 3200)


def _rms(x, g):
    return x * lax.rsqrt(jnp.mean(x * x, axis=-1, keepdims=True) + NORM_EPS) * g


def _in_proj_kernel(x_ref, g_ref, w_ref, wup_ref, ba_ref,
                    qa_ref, ka_ref, va_ref, ra_ref, la_ref, qb_ref, kb_ref, kbs_ref, vb_ref, vbs_ref):
    u = _rms(x_ref[...], g_ref[...]).astype(BF16)

    def proj(c0, c1):
        return jnp.dot(u, w_ref[:, c0:c1], preferred_element_type=F32)

    qa_ref[...] = proj(C_QA, C_KA).astype(BF16)
    ka_ref[...] = proj(C_KA, C_VA).astype(BF16)
    va_ref[...] = proj(C_VA, C_RA).astype(BF16)
    ra_ref[...] = proj(C_RA, C_QB).astype(BF16)
    qb_ref[...] = proj(C_QB, C_KB).astype(BF16)
    kb_ref[...] = proj(C_KB, C_KBS).astype(BF16)
    kbs_ref[...] = proj(C_KBS, C_VB).astype(BF16)
    vb_ref[...] = proj(C_VB, C_VBS).astype(BF16)
    vbs_ref[...] = proj(C_VBS, C_Z).astype(BF16)
    z = proj(C_Z, C_END)[:, :GLA_RANK]
    y = jnp.dot(z, wup_ref[...], preferred_element_type=F32, precision=HIGHEST) + ba_ref[...]
    log_sig = jnp.minimum(y, 0.0) - jnp.log1p(jnp.exp(-jnp.abs(y)))
    la_ref[...] = log_sig * (1.0 / GLA_GATE_TAU)


def _in_proj(x2, g1, w_cat, wup_p, ba_p):
    T = x2.shape[0]
    tm = TM_PROJ
    row = lambda w: pl.BlockSpec((tm, w), lambda i: (i, 0))
    full = lambda a: pl.BlockSpec(a.shape, lambda i: (0,) * a.ndim)
    outs = [(512, BF16), (512, BF16), (512, BF16), (512, BF16), (512, F32),
            (512, BF16), (128, BF16), (128, BF16), (128, BF16), (128, BF16)]
    return pl.pallas_call(
        _in_proj_kernel,
        grid=(T // tm,),
        in_specs=[row(D_MODEL), full(g1), full(w_cat), full(wup_p), full(ba_p)],
        out_specs=[row(w) for w, _ in outs],
        out_shape=[jax.ShapeDtypeStruct((T, w), dt) for w, dt in outs],
        compiler_params=pltpu.CompilerParams(
            dimension_semantics=("arbitrary",), vmem_limit_bytes=VMEM_LIMIT),
        name="in_proj",
    )(x2, g1, w_cat, wup_p, ba_p)


def _gla_kernel(q_ref, k_ref, v_ref, r_ref, la_ref, g_ref, o_ref, st_ref):
    @pl.when(pl.program_id(1) == 0)
    def _():
        st_ref[...] = jnp.zeros_like(st_ref)

    tl = TL_GLA
    c = GLA_CHUNK
    ri = lax.broadcasted_iota(jnp.int32, (tl, tl), 0)
    ci = lax.broadcasted_iota(jnp.int32, (tl, tl), 1)
    cum_mat = jnp.where((ci <= ri) & ((ri // c) == (ci // c)), 1.0, 0.0).astype(F32)
    b_all = jnp.dot(cum_mat, la_ref[...], preferred_element_type=F32, precision=HIGHEST)
    causal = (lax.broadcasted_iota(jnp.int32, (c, c), 0)
              >= lax.broadcasted_iota(jnp.int32, (c, c), 1))
    g = g_ref[...]
    for ch in range(tl // c):
        rows = slice(ch * c, (ch + 1) * c)
        b = b_all[rows]
        b_last = b[c - 1:c]
        qf = q_ref[rows, :].astype(F32)
        kf = k_ref[rows, :].astype(F32)
        q_e = (qf * jnp.exp(b) * (GLA_DK ** -0.5)).astype(BF16)
        k_e = (kf * jnp.exp(-b)).astype(BF16)
        k_t = (kf * jnp.exp(b_last - b)).astype(BF16)
        decay = jnp.exp(b_last)
        for h in range(GLA_HEADS):
            ls = slice(h * LANES, (h + 1) * LANES)
            qh, kh, kth = q_e[:, ls], k_e[:, ls], k_t[:, ls]
            vh = v_ref[rows, ls]
            a = pl.dot(qh, kh, trans_b=True)
            a = jnp.where(causal, a, 0.0).astype(BF16)
            st = st_ref[h]
            o = (jnp.dot(a, vh, preferred_element_type=F32)
                 + pl.dot(qh, st.astype(BF16), trans_b=True))
            st_ref[h] = st * decay[:, ls] + pl.dot(vh, kth, trans_a=True)
            rh = r_ref[rows, ls].astype(F32)
            o = _rms(o, g) * (rh * jax.nn.sigmoid(rh))
            o_ref[rows, ls] = o.astype(BF16)


def _gla(qa, ka, va, ra, la, g, batch, seq):
    tl = TL_GLA
    nl = seq // tl
    row = lambda: pl.BlockSpec((tl, 512), lambda b, i: (b * nl + i, 0))
    return pl.pallas_call(
        _gla_kernel,
        grid=(batch, nl),
        in_specs=[row(), row(), row(), row(), row(),
                  pl.BlockSpec((1, GLA_DV), lambda b, i: (0, 0))],
        out_specs=row(),
        out_shape=jax.ShapeDtypeStruct((batch * seq, 512), BF16),
        scratch_shapes=[pltpu.VMEM((GLA_HEADS, GLA_DV, LANES), F32)],
        compiler_params=pltpu.CompilerParams(
            dimension_semantics=("arbitrary", "arbitrary"), vmem_limit_bytes=VMEM_LIMIT),
        name="gla",
    )(qa, ka, va, ra, la, g)


def _swa_kernel(slopes, sink_ref, q_ref, kc_ref, kp_ref, ksc_ref, ksp_ref,
                vc_ref, vp_ref, vsc_ref, vsp_ref, g_ref, o_ref):
    w = SWA_WINDOW
    n = pl.program_id(1)
    cat = lambda p, cu: jnp.concatenate([p[...], cu[...]], axis=0)
    k, ks = cat(kp_ref, kc_ref), cat(ksp_ref, ksc_ref)
    v, vs = cat(vp_ref, vc_ref), cat(vsp_ref, vsc_ref)
    lane_lo = lax.broadcasted_iota(jnp.int32, (2 * w, LANES), 1) < SWA_HEAD_DIM
    zero = jnp.zeros_like(k)
    k_low = [jnp.where(lane_lo, k, zero), jnp.where(lane_lo, ks, zero)]
    k_high = [jnp.where(lane_lo, zero, ks), jnp.where(lane_lo, zero, k)]
    v_low = [jnp.where(lane_lo, v, zero), jnp.where(lane_lo, vs, zero)]
    v_high = [jnp.where(lane_lo, zero, vs), jnp.where(lane_lo, zero, v)]

    qi = lax.broadcasted_iota(jnp.int32, (w, 2 * w), 0)
    kj = lax.broadcasted_iota(jnp.int32, (w, 2 * w), 1)
    rel = qi + w - kj
    valid = (rel >= 0) & (rel < w) & ((n > 0) | (kj >= w))
    relf = rel.astype(F32)
    hi_half = lax.broadcasted_iota(jnp.int32, (LANES, LANES), 0) // SWA_HEAD_DIM
    hj_half = lax.broadcasted_iota(jnp.int32, (LANES, LANES), 1) // SWA_HEAD_DIM
    mean_mat = jnp.where(hi_half == hj_half, 1.0 / SWA_HEAD_DIM, 0.0).astype(F32)
    g = g_ref[...]

    def probs(s, head):
        s = s * (SWA_HEAD_DIM ** -0.5) - slopes[head] * relf
        s = jnp.where(valid, s, NEG_BIG)
        sink = sink_ref[head]
        m = jnp.maximum(jnp.max(s, axis=-1, keepdims=True), sink)
        e = jnp.exp(s - m)
        den = jnp.sum(e, axis=-1, keepdims=True) + jnp.exp(sink - m)
        return (e / den).astype(BF16)

    for pair in range(SWA_HEADS // 2):
        j = (2 * pair) // (SWA_HEADS // SWA_KV_HEADS)
        qp = q_ref[:, pair * LANES:(pair + 1) * LANES]
        p0 = probs(pl.dot(qp, k_low[j], trans_b=True), 2 * pair)
        p1 = probs(pl.dot(qp, k_high[j], trans_b=True), 2 * pair + 1)
        o = (jnp.dot(p0, v_low[j], preferred_element_type=F32)
             + jnp.dot(p1, v_high[j], preferred_element_type=F32))
        ms = jnp.dot(o * o, mean_mat, preferred_element_type=F32, precision=HIGHEST)
        o = o * lax.rsqrt(ms + NORM_EPS) * g
        o_ref[:, pair * LANES:(pair + 1) * LANES] = o.astype(BF16)


def _swa(qb, kb, kbs, vb, vbs, sinks, g2, batch, seq):
    w = SWA_WINDOW
    nb = seq // w
    slopes = [float(2.0 ** (-8.0 * (i + 1) / SWA_HEADS)) for i in range(SWA_HEADS)]
    cur = lambda width: pl.BlockSpec((w, width), lambda b, n: (b * nb + n, 0))
    prev = lambda width: pl.BlockSpec((w, width), lambda b, n: (b * nb + jnp.maximum(n - 1, 0), 0))
    return pl.pallas_call(
        functools.partial(_swa_kernel, slopes),
        grid=(batch, nb),
        in_specs=[pl.BlockSpec(memory_space=pltpu.SMEM),
                  cur(512), cur(128), prev(128), cur(128), prev(128),
                  cur(128), prev(128), cur(128), prev(128),
                  pl.BlockSpec((1, LANES), lambda b, n: (0, 0))],
        out_specs=cur(512),
        out_shape=jax.ShapeDtypeStruct((batch * seq, 512), BF16),
        compiler_params=pltpu.CompilerParams(
            dimension_semantics=("arbitrary", "arbitrary"), vmem_limit_bytes=VMEM_LIMIT),
        name="swa",
    )(sinks, qb, kb, kb, kbs, kbs, vb, vb, vbs, vbs, g2)


def _out_route_kernel(oa_ref, ob_ref, x_ref, woa_ref, wob_ref, g_ref, wr_ref, br_ref,
                      h1_ref, hn_ref, route_ref, cnt_ref):
    @pl.when(pl.program_id(0) == 0)
    def _():
        cnt_ref[...] = jnp.zeros_like(cnt_ref)

    tm = TM_ROUTE
    h1 = (x_ref[...]
          + jnp.dot(oa_ref[...], woa_ref[...], preferred_element_type=F32)
          + jnp.dot(ob_ref[...], wob_ref[...], preferred_element_type=F32))
    h1_ref[...] = h1
    hn = _rms(h1, g_ref[...])
    hn_ref[...] = hn
    logits = jnp.dot(hn, wr_ref[...], preferred_element_type=F32, precision=HIGHEST) + br_ref[...]

    lane = lax.broadcasted_iota(jnp.int32, (tm, LANES), 1)
    work = logits
    vals, sels = [], []
    for _ in range(TOP_K):
        m = jnp.max(work, axis=-1, keepdims=True)
        idx = jnp.min(jnp.where(work == m, lane, LANES), axis=-1, keepdims=True)
        sel = lane == idx
        vals.append(m)
        sels.append(sel)
        work = jnp.where(sel, -3e38, work)
    exps = [jnp.exp(v - vals[0]) for v in vals]
    den = exps[0] + exps[1] + exps[2] + exps[3]

    multihot = jnp.where(sels[0] | sels[1] | sels[2] | sels[3], 1.0, 0.0)
    ri = lax.broadcasted_iota(jnp.int32, (tm, tm), 0)
    ci = lax.broadcasted_iota(jnp.int32, (tm, tm), 1)
    strict = jnp.where(ci < ri, 1.0, 0.0).astype(BF16)
    before = jnp.dot(strict, multihot.astype(BF16), preferred_element_type=F32) + cnt_ref[...]
    lanef = lane.astype(F32)
    route = jnp.zeros((tm, LANES), F32)
    for k in range(TOP_K):
        idx_f = jnp.sum(jnp.where(sels[k], lanef, 0.0), axis=-1, keepdims=True)
        rank_f = jnp.sum(jnp.where(sels[k], before, 0.0), axis=-1, keepdims=True)
        route = jnp.where(lane == k, idx_f, route)
        route = jnp.where(lane == TOP_K + k, exps[k] / den, route)
        route = jnp.where(lane == 2 * TOP_K + k, rank_f, route)
    route_ref[...] = route
    cnt_ref[...] += jnp.sum(multihot, axis=0, keepdims=True)


def _out_route(oa, ob, x2, woa, wob, g2, wr_p, br_p):
    T = x2.shape[0]
    tm = TM_ROUTE
    row = lambda w: pl.BlockSpec((tm, w), lambda i: (i, 0))
    full = lambda a: pl.BlockSpec(a.shape, lambda i: (0,) * a.ndim)
    return pl.pallas_call(
        _out_route_kernel,
        grid=(T // tm,),
        in_specs=[row(512), row(512), row(D_MODEL), full(woa), full(wob), full(g2),
                  full(wr_p), full(br_p)],
        out_specs=[row(D_MODEL), row(D_MODEL), row(LANES),
                   pl.BlockSpec((1, LANES), lambda i: (0, 0))],
        out_shape=[jax.ShapeDtypeStruct((T, D_MODEL), F32),
                   jax.ShapeDtypeStruct((T, D_MODEL), F32),
                   jax.ShapeDtypeStruct((T, LANES), F32),
                   jax.ShapeDtypeStruct((1, LANES), F32)],
        compiler_params=pltpu.CompilerParams(
            dimension_semantics=("arbitrary",), vmem_limit_bytes=VMEM_LIMIT),
        name="out_route",
    )(oa, ob, x2, woa, wob, g2, wr_p, br_p)


def _row_copy(src, s, dst, d, sem):
    return pltpu.make_async_copy(src.at[pl.ds(s, 1)], dst.at[pl.ds(d, 1)], sem)


def _dispatch_kernel(dest_ref, pad_start_ref, pad_cnt_ref, nused_ref, hn_ref, xs_ref,
                     zero_ref, sems, zsem):
    i = pl.program_id(0)
    n = pl.num_programs(0)
    npairs = DISPATCH_PAIRS
    tm = TM_EXPERT
    slot = i % 2

    def issue(p, carry):
        pair = i * npairs + p
        _row_copy(hn_ref, pair // TOP_K, xs_ref, dest_ref[pair], sems.at[slot]).start()
        return carry

    lax.fori_loop(0, npairs, issue, 0)

    def wait_rows(s):
        pltpu.make_async_copy(hn_ref.at[pl.ds(0, npairs)], xs_ref.at[pl.ds(0, npairs)],
                              sems.at[s]).wait()

    @pl.when(i > 0)
    def _():
        wait_rows(1 - slot)

    @pl.when(i == n - 1)
    def _():
        wait_rows(slot)
        zero_ref[...] = jnp.zeros_like(zero_ref)
        for e in range(N_EXPERTS):
            start, cnt = pad_start_ref[e], pad_cnt_ref[e]

            def zissue(r, carry):
                _row_copy(zero_ref, 0, xs_ref, start + r, zsem).start()
                return carry

            def zwait(r, carry):
                _row_copy(zero_ref, 0, xs_ref, start + r, zsem).wait()
                return carry

            lax.fori_loop(0, cnt, zissue, 0)
            lax.fori_loop(0, cnt, zwait, 0)

        def tail_copy(t):
            return pltpu.make_async_copy(zero_ref, xs_ref.at[pl.ds(t * tm, tm)], zsem)

        def tissue(t, carry):
            tail_copy(t).start()
            return carry

        def twait(t, carry):
            tail_copy(t).wait()
            return carry

        n_tiles = xs_ref.shape[0] // tm
        lax.fori_loop(nused_ref[0], n_tiles, tissue, 0)
        lax.fori_loop(nused_ref[0], n_tiles, twait, 0)


def _dispatch(dest, pad_start, pad_cnt, n_used, hn, n_rows):
    T = hn.shape[0]
    any_spec = pl.BlockSpec(memory_space=pl.ANY)
    return pl.pallas_call(
        _dispatch_kernel,
        grid_spec=pltpu.PrefetchScalarGridSpec(
            num_scalar_prefetch=4,
            grid=(T * TOP_K // DISPATCH_PAIRS,),
            in_specs=[any_spec],
            out_specs=any_spec,
            scratch_shapes=[pltpu.VMEM((TM_EXPERT, D_MODEL), F32),
                            pltpu.SemaphoreType.DMA((2,)),
                            pltpu.SemaphoreType.DMA(())]),
        out_shape=jax.ShapeDtypeStruct((n_rows, D_MODEL), F32),
        compiler_params=pltpu.CompilerParams(
            dimension_semantics=("arbitrary",), has_side_effects=True),
        name="dispatch",
    )(dest, pad_start, pad_cnt, n_used, hn)


def _experts_kernel(tile_ref, texp_ref, nused_ref, x_ref, wg_ref, bg_ref, wl_ref, bl_ref,
                    wd_ref, bd_ref, y_ref, wg_bf, wl_bf, wd_bf):
    i = pl.program_id(0)

    @pl.when(i < nused_ref[0])
    def _():
        new_expert = (i == 0) | (texp_ref[i] != texp_ref[jnp.maximum(i - 1, 0)])

        @pl.when(new_expert)
        def _():
            wg_bf[...] = wg_ref[0].astype(BF16)
            wl_bf[...] = wl_ref[0].astype(BF16)
            wd_bf[...] = wd_ref[0].astype(BF16)

        x = x_ref[...].astype(BF16)
        glu = jnp.minimum(jnp.dot(x, wg_bf[...], preferred_element_type=F32) + bg_ref[0],
                          SWIGLU_LIMIT)
        lin = jnp.clip(jnp.dot(x, wl_bf[...], preferred_element_type=F32) + bl_ref[0],
                       -SWIGLU_LIMIT, SWIGLU_LIMIT)
        hid = glu * jax.nn.sigmoid(SWIGLU_ALPHA * glu) * (lin + 1.0)
        y_ref[...] = jnp.dot(hid.astype(BF16), wd_bf[...], preferred_element_type=F32) + bd_ref[0]

    @pl.when(i >= nused_ref[0])
    def _():
        y_ref[...] = jnp.zeros_like(y_ref)


def _experts(tile_idx, tile_expert, n_used, xs, w_glu, b_glu, w_lin, b_lin, w_down, b_down):
    tm = TM_EXPERT
    n_tiles = xs.shape[0] // tm
    row = pl.BlockSpec((tm, D_MODEL), lambda i, t, e, n: (t[i], 0))
    wspec = lambda: pl.BlockSpec((1, D_MODEL, D_FF), lambda i, t, e, n: (e[i], 0, 0))
    bspec = lambda: pl.BlockSpec((1, 1, D_FF), lambda i, t, e, n: (e[i], 0, 0))
    return pl.pallas_call(
        _experts_kernel,
        grid_spec=pltpu.PrefetchScalarGridSpec(
            num_scalar_prefetch=3,
            grid=(n_tiles,),
            in_specs=[row, wspec(), bspec(), wspec(), bspec(), wspec(), bspec()],
            out_specs=pl.BlockSpec((tm, D_MODEL), lambda i, t, e, n: (i, 0)),
            scratch_shapes=[pltpu.VMEM((D_MODEL, D_FF), BF16)] * 3),
        out_shape=jax.ShapeDtypeStruct(xs.shape, F32),
        compiler_params=pltpu.CompilerParams(
            dimension_semantics=("arbitrary",), vmem_limit_bytes=VMEM_LIMIT),
        name="experts",
    )(tile_idx, tile_expert, n_used, xs, w_glu, b_glu.reshape(N_EXPERTS, 1, D_FF),
      w_lin, b_lin.reshape(N_EXPERTS, 1, D_FF), w_down, b_down.reshape(N_EXPERTS, 1, D_MODEL))


def _combine_kernel(dest_ref, ys_ref, h1_ref, route_ref, g_ref, o_ref, buf, sems):
    i = pl.program_id(0)
    n = pl.num_programs(0)
    tm = TM_COMBINE

    def issue(tile, slot):
        def body(r, carry):
            for k in range(TOP_K):
                d = dest_ref[(tile * tm + r) * TOP_K + k]
                pltpu.make_async_copy(ys_ref.at[pl.ds(d, 1)], buf.at[slot, k, pl.ds(r, 1)],
                                      sems.at[slot]).start()
            return carry
        lax.fori_loop(0, tm, body, 0)

    @pl.when(i == 0)
    def _():
        issue(0, 0)

    @pl.when(i + 1 < n)
    def _():
        issue(i + 1, (i + 1) % 2)

    slot = i % 2
    for k in range(TOP_K):
        pltpu.make_async_copy(ys_ref.at[pl.ds(0, tm)], buf.at[slot, k], sems.at[slot]).wait()
    acc = h1_ref[...]
    for k in range(TOP_K):
        acc = acc + route_ref[:, TOP_K + k:TOP_K + k + 1] * buf[slot, k]
    o_ref[...] = _rms(acc, g_ref[...])


def _combine(dest, ys, h1, route, fg):
    T = h1.shape[0]
    tm = TM_COMBINE
    return pl.pallas_call(
        _combine_kernel,
        grid_spec=pltpu.PrefetchScalarGridSpec(
            num_scalar_prefetch=1,
            grid=(T // tm,),
            in_specs=[pl.BlockSpec(memory_space=pl.ANY),
                      pl.BlockSpec((tm, D_MODEL), lambda i, d: (i, 0)),
                      pl.BlockSpec((tm, LANES), lambda i, d: (i, 0)),
                      pl.BlockSpec((1, D_MODEL), lambda i, d: (0, 0))],
            out_specs=pl.BlockSpec((tm, D_MODEL), lambda i, d: (i, 0)),
            scratch_shapes=[pltpu.VMEM((2, TOP_K, tm, D_MODEL), F32),
                            pltpu.SemaphoreType.DMA((2,))]),
        out_shape=jax.ShapeDtypeStruct((T, D_MODEL), F32),
        compiler_params=pltpu.CompilerParams(
            dimension_semantics=("arbitrary",), vmem_limit_bytes=VMEM_LIMIT),
        name="combine",
    )(dest, ys, h1, route, fg)


def _head_slots(w, heads, dim):
    lead = w.shape[:-1]
    w = w.reshape(lead + (heads, dim))
    w = jnp.pad(w, [(0, 0)] * len(lead) + [(0, 0), (0, LANES - dim)])
    return w.reshape(lead + (heads * LANES,))


def _swap_halves(w):
    h = w.shape[-1] // 2
    return jnp.concatenate([w[..., h:], w[..., :h]], axis=-1)


def _layer(x2, batch, seq, norm1_g, w_in, w_alpha_up, b_alpha, gla_norm_g, swa_sinks, swa_norm_g,
           w_out, norm2_g, w_router, b_router, w_glu, b_glu, w_lin, b_lin, w_down, b_down):
    T = x2.shape[0]
    kb_w, vb_w = w_in[:, 2064:2192], w_in[:, 2192:2320]
    w_cat = jnp.concatenate([
        _head_slots(w_in[:, 0:256], GLA_HEADS, GLA_DK),
        _head_slots(w_in[:, 256:512], GLA_HEADS, GLA_DK),
        w_in[:, 512:1024], w_in[:, 1024:1536], w_in[:, 1552:2064],
        kb_w, _swap_halves(kb_w), vb_w, _swap_halves(vb_w),
        jnp.pad(w_in[:, 1536:1552], [(0, 0), (0, LANES - GLA_RANK)]),
    ], axis=1).astype(BF16)
    wup_p = _head_slots(w_alpha_up, GLA_HEADS, GLA_DK)
    ba_p = _head_slots(b_alpha, GLA_HEADS, GLA_DK).reshape(1, -1)

    qa, ka, va, ra, la, qb, kb, kbs, vb, vbs = _in_proj(
        x2, norm1_g.reshape(1, -1), w_cat, wup_p, ba_p)
    oa = _gla(qa, ka, va, ra, la, gla_norm_g.reshape(1, -1), batch, seq)
    ob = _swa(qb, kb, kbs, vb, vbs, swa_sinks, jnp.tile(swa_norm_g, 2).reshape(1, -1), batch, seq)

    wo = w_out.astype(BF16)
    wr_p = jnp.pad(w_router, [(0, 0), (0, LANES - N_EXPERTS)])
    br_p = jnp.pad(b_router, [(0, LANES - N_EXPERTS)], constant_values=NEG_BIG).reshape(1, -1)
    h1, hn, route, cnt = _out_route(oa, ob, x2, wo[:512], wo[512:], norm2_g.reshape(1, -1),
                                    wr_p, br_p)

    tm = TM_EXPERT
    n_tiles = T * TOP_K // tm + N_EXPERTS
    counts = cnt[0, :N_EXPERTS].astype(jnp.int32)
    padded = (counts + tm - 1) // tm * tm
    pends = jnp.cumsum(padded)
    pstarts = pends - padded
    top_idx = route[:, 0:TOP_K].astype(jnp.int32)
    rank = route[:, 2 * TOP_K:3 * TOP_K].astype(jnp.int32)
    seg_start = jnp.sum(jnp.where(top_idx[..., None] == jnp.arange(N_EXPERTS), pstarts, 0), axis=-1)
    dest = (seg_start + rank).reshape(-1)
    n_used = pends[-1] // tm
    tile_idx = jnp.minimum(jnp.arange(n_tiles, dtype=jnp.int32), n_used - 1)
    tile_expert = jnp.minimum(
        jnp.searchsorted(pends, tile_idx * tm, side="right"), N_EXPERTS - 1).astype(jnp.int32)

    n_used = n_used.reshape(1)
    xs = _dispatch(dest, pstarts + counts, padded - counts, n_used, hn, n_tiles * tm)
    ys = _experts(tile_idx, tile_expert, n_used, xs,
                  w_glu, b_glu, w_lin, b_lin, w_down, b_down)
    return dest, ys, h1, route


def kernel(x, norm1_g, w_in, w_alpha_up, b_alpha, gla_norm_g, swa_sinks, swa_norm_g, w_out,
           norm2_g, w_router, b_router, w_glu, b_glu, w_lin, b_lin, w_down, b_down, final_g):
    batch, seq, d = x.shape
    assert norm1_g.shape[0] == 1, "single-layer problem"
    x2 = x.reshape(batch * seq, d)
    dest, ys, h1, route = _layer(
        x2, batch, seq, norm1_g[0], w_in[0], w_alpha_up[0], b_alpha[0], gla_norm_g[0],
        swa_sinks[0], swa_norm_g[0], w_out[0], norm2_g[0], w_router[0], b_router[0],
        w_glu[0], b_glu[0], w_lin[0], b_lin[0], w_down[0], b_down[0])
    out = _combine(dest, ys, h1, route, final_g.reshape(1, -1))
    return out.reshape(batch, seq, d)
```

```python
import functools

import numpy as np
import jax
import jax.numpy as jnp
from jax import lax
from jax.experimental import pallas as pl
from jax.experimental.pallas import tpu as pltpu

F32 = jnp.float32
BF16 = jnp.bfloat16
HIGHEST = lax.Precision.HIGHEST

D_MODEL = 1024
GLA_HEADS = 4
GLA_DK = 64
GLA_DV = 128
GLA_RANK = 16
GLA_GATE_TAU = 16.0
GLA_CHUNK = 64
SWA_HEADS = 8
SWA_KV_HEADS = 2
SWA_HEAD_DIM = 64
SWA_WINDOW = 128
N_EXPERTS = 32
TOP_K = 4
D_FF = 1024
SWIGLU_LIMIT = 7.0
SWIGLU_ALPHA = 1.702
NORM_EPS = 1e-5

LANES = 128
VMEM_LIMIT = 56 * 1024 * 1024

TM_PROJ = 512
TL_GLA = 256
TM_ROUTE = 256
TM_EXPERT = 256
TM_COMBINE = 256
TM_DISPATCH = 256
DISPATCH_UNROLL = 8

NEG_BIG = -1e30

C_QA, C_KA, C_VA, C_RA, C_QB, C_KB, C_KBS, C_VB, C_VBS, C_Z, C_END = (
    0, 512, 1024, 1536, 2048, 2560, 2688, 2816, 2944, 3072, 3200)


def _rms(x, g):
    return x * lax.rsqrt(jnp.mean(x * x, axis=-1, keepdims=True) + NORM_EPS) * g


def _in_proj_kernel(x_ref, g_ref, w_ref, wup_ref, ba_ref,
                    qa_ref, ka_ref, va_ref, ra_ref, la_ref, qb_ref, kb_ref, kbs_ref, vb_ref, vbs_ref):
    u = _rms(x_ref[...], g_ref[...]).astype(BF16)

    def proj(c0, c1):
        return jnp.dot(u, w_ref[:, c0:c1], preferred_element_type=F32)

    qa_ref[...] = proj(C_QA, C_KA).astype(BF16)
    ka_ref[...] = proj(C_KA, C_VA).astype(BF16)
    va_ref[...] = proj(C_VA, C_RA).astype(BF16)
    ra_ref[...] = proj(C_RA, C_QB).astype(BF16)
    qb_ref[...] = proj(C_QB, C_KB).astype(BF16)
    kb_ref[...] = proj(C_KB, C_KBS).astype(BF16)
    kbs_ref[...] = proj(C_KBS, C_VB).astype(BF16)
    vb_ref[...] = proj(C_VB, C_VBS).astype(BF16)
    vbs_ref[...] = proj(C_VBS, C_Z).astype(BF16)
    z = proj(C_Z, C_END)[:, :GLA_RANK]
    y = jnp.dot(z, wup_ref[...], preferred_element_type=F32, precision=HIGHEST) + ba_ref[...]
    log_sig = jnp.minimum(y, 0.0) - jnp.log1p(jnp.exp(-jnp.abs(y)))
    la_ref[...] = log_sig * (1.0 / GLA_GATE_TAU)


def _in_proj(x2, g1, w_cat, wup_p, ba_p):
    T = x2.shape[0]
    tm = TM_PROJ
    row = lambda w: pl.BlockSpec((tm, w), lambda i: (i, 0))
    full = lambda a: pl.BlockSpec(a.shape, lambda i: (0,) * a.ndim)
    outs = [(512, BF16), (512, BF16), (512, BF16), (512, BF16), (512, F32),
            (512, BF16), (128, BF16), (128, BF16), (128, BF16), (128, BF16)]
    return pl.pallas_call(
        _in_proj_kernel,
        grid=(T // tm,),
        in_specs=[row(D_MODEL), full(g1), full(w_cat), full(wup_p), full(ba_p)],
        out_specs=[row(w) for w, _ in outs],
        out_shape=[jax.ShapeDtypeStruct((T, w), dt) for w, dt in outs],
        compiler_params=pltpu.CompilerParams(
            dimension_semantics=("arbitrary",), vmem_limit_bytes=VMEM_LIMIT),
        name="in_proj",
    )(x2, g1, w_cat, wup_p, ba_p)


def _gla_kernel(q_ref, k_ref, v_ref, r_ref, la_ref, g_ref, o_ref, st_ref):
    @pl.when(pl.program_id(1) == 0)
    def _():
        st_ref[...] = jnp.zeros_like(st_ref)

    tl = TL_GLA
    c = GLA_CHUNK
    ri = lax.broadcasted_iota(jnp.int32, (tl, tl), 0)
    ci = lax.broadcasted_iota(jnp.int32, (tl, tl), 1)
    cum_mat = jnp.where((ci <= ri) & ((ri // c) == (ci // c)), 1.0, 0.0).astype(F32)
    b_all = jnp.dot(cum_mat, la_ref[...], preferred_element_type=F32, precision=HIGHEST)
    causal = (lax.broadcasted_iota(jnp.int32, (c, c), 0)
              >= lax.broadcasted_iota(jnp.int32, (c, c), 1))
    g = g_ref[...]
    for ch in range(tl // c):
        rows = slice(ch * c, (ch + 1) * c)
        b = b_all[rows]
        b_last = b[c - 1:c]
        qf = q_ref[rows, :].astype(F32)
        kf = k_ref[rows, :].astype(F32)
        q_e = (qf * jnp.exp(b) * (GLA_DK ** -0.5)).astype(BF16)
        k_e = (kf * jnp.exp(-b)).astype(BF16)
        k_t = (kf * jnp.exp(b_last - b)).astype(BF16)
        decay = jnp.exp(b_last)
        for h in range(GLA_HEADS):
            ls = slice(h * LANES, (h + 1) * LANES)
            qh, kh, kth = q_e[:, ls], k_e[:, ls], k_t[:, ls]
            vh = v_ref[rows, ls]
            a = pl.dot(qh, kh, trans_b=True)
            a = jnp.where(causal, a, 0.0).astype(BF16)
            st = st_ref[h]
            o = (jnp.dot(a, vh, preferred_element_type=F32)
                 + pl.dot(qh, st.astype(BF16), trans_b=True))
            st_ref[h] = st * decay[:, ls] + pl.dot(vh, kth, trans_a=True)
            rh = r_ref[rows, ls].astype(F32)
            o = _rms(o, g) * (rh * jax.nn.sigmoid(rh))
            o_ref[rows, ls] = o.astype(BF16)


def _gla(qa, ka, va, ra, la, g, batch, seq):
    tl = TL_GLA
    nl = seq // tl
    row = lambda: pl.BlockSpec((tl, 512), lambda b, i: (b * nl + i, 0))
    return pl.pallas_call(
        _gla_kernel,
        grid=(batch, nl),
        in_specs=[row(), row(), row(), row(), row(),
                  pl.BlockSpec((1, GLA_DV), lambda b, i: (0, 0))],
        out_specs=row(),
        out_shape=jax.ShapeDtypeStruct((batch * seq, 512), BF16),
        scratch_shapes=[pltpu.VMEM((GLA_HEADS, GLA_DV, LANES), F32)],
        compiler_params=pltpu.CompilerParams(
            dimension_semantics=("arbitrary", "arbitrary"), vmem_limit_bytes=VMEM_LIMIT),
        name="gla",
    )(qa, ka, va, ra, la, g)


def _swa_kernel(slopes, sink_ref, q_ref, kc_ref, kp_ref, ksc_ref, ksp_ref,
                vc_ref, vp_ref, vsc_ref, vsp_ref, g_ref, o_ref):
    w = SWA_WINDOW
    n = pl.program_id(1)
    cat = lambda p, cu: jnp.concatenate([p[...], cu[...]], axis=0)
    k, ks = cat(kp_ref, kc_ref), cat(ksp_ref, ksc_ref)
    v, vs = cat(vp_ref, vc_ref), cat(vsp_ref, vsc_ref)
    lane_lo = lax.broadcasted_iota(jnp.int32, (2 * w, LANES), 1) < SWA_HEAD_DIM
    zero = jnp.zeros_like(k)
    k_low = [jnp.where(lane_lo, k, zero), jnp.where(lane_lo, ks, zero)]
    k_high = [jnp.where(lane_lo, zero, ks), jnp.where(lane_lo, zero, k)]
    v_low = [jnp.where(lane_lo, v, zero), jnp.where(lane_lo, vs, zero)]
    v_high = [jnp.where(lane_lo, zero, vs), jnp.where(lane_lo, zero, v)]

    qi = lax.broadcasted_iota(jnp.int32, (w, 2 * w), 0)
    kj = lax.broadcasted_iota(jnp.int32, (w, 2 * w), 1)
    rel = qi + w - kj
    valid = (rel >= 0) & (rel < w) & ((n > 0) | (kj >= w))
    relf = rel.astype(F32)
    hi_half = lax.broadcasted_iota(jnp.int32, (LANES, LANES), 0) // SWA_HEAD_DIM
    hj_half = lax.broadcasted_iota(jnp.int32, (LANES, LANES), 1) // SWA_HEAD_DIM
    mean_mat = jnp.where(hi_half == hj_half, 1.0 / SWA_HEAD_DIM, 0.0).astype(F32)
    g = g_ref[...]

    def probs(s, head):
        s = s * (SWA_HEAD_DIM ** -0.5) - slopes[head] * relf
        s = jnp.where(valid, s, NEG_BIG)
        sink = sink_ref[head]
        m = jnp.maximum(jnp.max(s, axis=-1, keepdims=True), sink)
        e = jnp.exp(s - m)
        den = jnp.sum(e, axis=-1, keepdims=True) + jnp.exp(sink - m)
        return (e / den).astype(BF16)

    for pair in range(SWA_HEADS // 2):
        j = (2 * pair) // (SWA_HEADS // SWA_KV_HEADS)
        qp = q_ref[:, pair * LANES:(pair + 1) * LANES]
        p0 = probs(pl.dot(qp, k_low[j], trans_b=True), 2 * pair)
        p1 = probs(pl.dot(qp, k_high[j], trans_b=True), 2 * pair + 1)
        o = (jnp.dot(p0, v_low[j], preferred_element_type=F32)
             + jnp.dot(p1, v_high[j], preferred_element_type=F32))
        ms = jnp.dot(o * o, mean_mat, preferred_element_type=F32, precision=HIGHEST)
        o = o * lax.rsqrt(ms + NORM_EPS) * g
        o_ref[:, pair * LANES:(pair + 1) * LANES] = o.astype(BF16)


def _swa(qb, kb, kbs, vb, vbs, sinks, g2, batch, seq):
    w = SWA_WINDOW
    nb = seq // w
    slopes = [float(2.0 ** (-8.0 * (i + 1) / SWA_HEADS)) for i in range(SWA_HEADS)]
    cur = lambda width: pl.BlockSpec((w, width), lambda b, n: (b * nb + n, 0))
    prev = lambda width: pl.BlockSpec((w, width), lambda b, n: (b * nb + jnp.maximum(n - 1, 0), 0))
    return pl.pallas_call(
        functools.partial(_swa_kernel, slopes),
        grid=(batch, nb),
        in_specs=[pl.BlockSpec(memory_space=pltpu.SMEM),
                  cur(512), cur(128), prev(128), cur(128), prev(128),
                  cur(128), prev(128), cur(128), prev(128),
                  pl.BlockSpec((1, LANES), lambda b, n: (0, 0))],
        out_specs=cur(512),
        out_shape=jax.ShapeDtypeStruct((batch * seq, 512), BF16),
        compiler_params=pltpu.CompilerParams(
            dimension_semantics=("arbitrary", "arbitrary"), vmem_limit_bytes=VMEM_LIMIT),
        name="swa",
    )(sinks, qb, kb, kb, kbs, kbs, vb, vb, vbs, vbs, g2)


def _out_route_kernel(oa_ref, ob_ref, x_ref, woa_ref, wob_ref, g_ref, wr_ref, br_ref,
                      h1_ref, hn_ref, route_ref, cnt_ref):
    @pl.when(pl.program_id(0) == 0)
    def _():
        cnt_ref[...] = jnp.zeros_like(cnt_ref)

    tm = TM_ROUTE
    h1 = (x_ref[...]
          + jnp.dot(oa_ref[...], woa_ref[...], preferred_element_type=F32)
          + jnp.dot(ob_ref[...], wob_ref[...], preferred_element_type=F32))
    h1_ref[...] = h1
    hn = _rms(h1, g_ref[...])
    hn_ref[...] = hn
    logits = jnp.dot(hn, wr_ref[...], preferred_element_type=F32, precision=HIGHEST) + br_ref[...]

    lane = lax.broadcasted_iota(jnp.int32, (tm, LANES), 1)
    work = logits
    vals, sels = [], []
    for _ in range(TOP_K):
        m = jnp.max(work, axis=-1, keepdims=True)
        idx = jnp.min(jnp.where(work == m, lane, LANES), axis=-1, keepdims=True)
        sel = lane == idx
        vals.append(m)
        sels.append(sel)
        work = jnp.where(sel, -3e38, work)
    exps = [jnp.exp(v - vals[0]) for v in vals]
    den = exps[0] + exps[1] + exps[2] + exps[3]

    multihot = jnp.where(sels[0] | sels[1] | sels[2] | sels[3], 1.0, 0.0)
    ri = lax.broadcasted_iota(jnp.int32, (tm, tm), 0)
    ci = lax.broadcasted_iota(jnp.int32, (tm, tm), 1)
    strict = jnp.where(ci < ri, 1.0, 0.0).astype(BF16)
    before = jnp.dot(strict, multihot.astype(BF16), preferred_element_type=F32) + cnt_ref[...]
    lanef = lane.astype(F32)
    route = jnp.zeros((tm, LANES), F32)
    for k in range(TOP_K):
        idx_f = jnp.sum(jnp.where(sels[k], lanef, 0.0), axis=-1, keepdims=True)
        rank_f = jnp.sum(jnp.where(sels[k], before, 0.0), axis=-1, keepdims=True)
        route = jnp.where(lane == k, idx_f, route)
        route = jnp.where(lane == TOP_K + k, exps[k] / den, route)
        route = jnp.where(lane == 2 * TOP_K + k, rank_f, route)
    route_ref[...] = route
    cnt_ref[...] += jnp.sum(multihot, axis=0, keepdims=True)


def _out_route(oa, ob, x2, woa, wob, g2, wr_p, br_p):
    T = x2.shape[0]
    tm = TM_ROUTE
    row = lambda w: pl.BlockSpec((tm, w), lambda i: (i, 0))
    full = lambda a: pl.BlockSpec(a.shape, lambda i: (0,) * a.ndim)
    return pl.pallas_call(
        _out_route_kernel,
        grid=(T // tm,),
        in_specs=[row(512), row(512), row(D_MODEL), full(woa), full(wob), full(g2),
                  full(wr_p), full(br_p)],
        out_specs=[row(D_MODEL), row(D_MODEL), row(LANES),
                   pl.BlockSpec((1, LANES), lambda i: (0, 0))],
        out_shape=[jax.ShapeDtypeStruct((T, D_MODEL), F32),
                   jax.ShapeDtypeStruct((T, D_MODEL), F32),
                   jax.ShapeDtypeStruct((T, LANES), F32),
                   jax.ShapeDtypeStruct((1, LANES), F32)],
        compiler_params=pltpu.CompilerParams(
            dimension_semantics=("arbitrary",), vmem_limit_bytes=VMEM_LIMIT),
        name="out_route",
    )(oa, ob, x2, woa, wob, g2, wr_p, br_p)


def _row_copy(src, s, dst, d, sem):
    return pltpu.make_async_copy(src.at[pl.ds(s, 1)], dst.at[pl.ds(d, 1)], sem)


def _dispatch_kernel(dest_ref, pad_start_ref, pad_cnt_ref, nused_ref, hn_ref, xs_ref,
                     zero_ref, sem, zsem):
    i = pl.program_id(0)
    n = pl.num_programs(0)
    tm = TM_EXPERT
    tok_per_iter = DISPATCH_UNROLL // TOP_K

    def issue(g, carry):
        for u in range(DISPATCH_UNROLL):
            r = g * tok_per_iter + u // TOP_K
            d = dest_ref[(i * TM_DISPATCH + r) * TOP_K + u % TOP_K]
            _row_copy(hn_ref, r, xs_ref, d, sem).start()
        return carry

    lax.fori_loop(0, TM_DISPATCH // tok_per_iter, issue, 0)
    for _ in range(TOP_K):
        pltpu.make_async_copy(hn_ref, xs_ref.at[pl.ds(0, TM_DISPATCH)], sem).wait()

    @pl.when(i == n - 1)
    def _():
        zero_ref[...] = jnp.zeros_like(zero_ref)
        for e in range(N_EXPERTS):
            start, cnt = pad_start_ref[e], pad_cnt_ref[e]

            def zissue(r, carry):
                _row_copy(zero_ref, 0, xs_ref, start + r, zsem).start()
                return carry

            def zwait(r, carry):
                _row_copy(zero_ref, 0, xs_ref, start + r, zsem).wait()
                return carry

            lax.fori_loop(0, cnt, zissue, 0)
            lax.fori_loop(0, cnt, zwait, 0)

        def tail_copy(t):
            return pltpu.make_async_copy(zero_ref, xs_ref.at[pl.ds(t * tm, tm)], zsem)

        def tissue(t, carry):
            tail_copy(t).start()
            return carry

        def twait(t, carry):
            tail_copy(t).wait()
            return carry

        n_tiles = xs_ref.shape[0] // tm
        lax.fori_loop(nused_ref[0], n_tiles, tissue, 0)
        lax.fori_loop(nused_ref[0], n_tiles, twait, 0)


def _dispatch(dest, pad_start, pad_cnt, n_used, hn, n_rows):
    T = hn.shape[0]
    any_spec = pl.BlockSpec(memory_space=pl.ANY)
    return pl.pallas_call(
        _dispatch_kernel,
        grid_spec=pltpu.PrefetchScalarGridSpec(
            num_scalar_prefetch=4,
            grid=(T // TM_DISPATCH,),
            in_specs=[pl.BlockSpec((TM_DISPATCH, D_MODEL), lambda i, *_: (i, 0))],
            out_specs=any_spec,
            scratch_shapes=[pltpu.VMEM((TM_EXPERT, D_MODEL), F32),
                            pltpu.SemaphoreType.DMA(()),
                            pltpu.SemaphoreType.DMA(())]),
        out_shape=jax.ShapeDtypeStruct((n_rows, D_MODEL), F32),
        compiler_params=pltpu.CompilerParams(
            dimension_semantics=("arbitrary",), has_side_effects=True,
            vmem_limit_bytes=VMEM_LIMIT),
        name="dispatch",
    )(dest, pad_start, pad_cnt, n_used, hn)


def _experts_kernel(tile_ref, texp_ref, nused_ref, x_ref, wg_ref, bg_ref, wl_ref, bl_ref,
                    wd_ref, bd_ref, y_ref, wg_bf, wl_bf, wd_bf):
    i = pl.program_id(0)

    @pl.when(i < nused_ref[0])
    def _():
        new_expert = (i == 0) | (texp_ref[i] != texp_ref[jnp.maximum(i - 1, 0)])

        @pl.when(new_expert)
        def _():
            wg_bf[...] = wg_ref[0].astype(BF16)
            wl_bf[...] = wl_ref[0].astype(BF16)
            wd_bf[...] = wd_ref[0].astype(BF16)

        x = x_ref[...].astype(BF16)
        glu = jnp.minimum(jnp.dot(x, wg_bf[...], preferred_element_type=F32) + bg_ref[0],
                          SWIGLU_LIMIT)
        lin = jnp.clip(jnp.dot(x, wl_bf[...], preferred_element_type=F32) + bl_ref[0],
                       -SWIGLU_LIMIT, SWIGLU_LIMIT)
        hid = glu * jax.nn.sigmoid(SWIGLU_ALPHA * glu) * (lin + 1.0)
        y_ref[...] = jnp.dot(hid.astype(BF16), wd_bf[...], preferred_element_type=F32) + bd_ref[0]

    @pl.when(i >= nused_ref[0])
    def _():
        y_ref[...] = jnp.zeros_like(y_ref)


def _experts(tile_idx, tile_expert, n_used, xs, w_glu, b_glu, w_lin, b_lin, w_down, b_down):
    tm = TM_EXPERT
    n_tiles = xs.shape[0] // tm
    row = pl.BlockSpec((tm, D_MODEL), lambda i, t, e, n: (t[i], 0))
    wspec = lambda: pl.BlockSpec((1, D_MODEL, D_FF), lambda i, t, e, n: (e[i], 0, 0))
    bspec = lambda: pl.BlockSpec((1, 1, D_FF), lambda i, t, e, n: (e[i], 0, 0))
    return pl.pallas_call(
        _experts_kernel,
        grid_spec=pltpu.PrefetchScalarGridSpec(
            num_scalar_prefetch=3,
            grid=(n_tiles,),
            in_specs=[row, wspec(), bspec(), wspec(), bspec(), wspec(), bspec()],
            out_specs=pl.BlockSpec((tm, D_MODEL), lambda i, t, e, n: (i, 0)),
            scratch_shapes=[pltpu.VMEM((D_MODEL, D_FF), BF16)] * 3),
        out_shape=jax.ShapeDtypeStruct(xs.shape, F32),
        compiler_params=pltpu.CompilerParams(
            dimension_semantics=("arbitrary",), vmem_limit_bytes=VMEM_LIMIT),
        name="experts",
    )(tile_idx, tile_expert, n_used, xs, w_glu, b_glu.reshape(N_EXPERTS, 1, D_FF),
      w_lin, b_lin.reshape(N_EXPERTS, 1, D_FF), w_down, b_down.reshape(N_EXPERTS, 1, D_MODEL))


def _combine_kernel(dest_ref, ys_ref, h1_ref, route_ref, g_ref, o_ref, buf, sems):
    i = pl.program_id(0)
    n = pl.num_programs(0)
    tm = TM_COMBINE

    def issue(tile, slot):
        def body(r, carry):
            for k in range(TOP_K):
                d = dest_ref[(tile * tm + r) * TOP_K + k]
                pltpu.make_async_copy(ys_ref.at[pl.ds(d, 1)], buf.at[slot, k, pl.ds(r, 1)],
                                      sems.at[slot]).start()
            return carry
        lax.fori_loop(0, tm, body, 0)

    @pl.when(i == 0)
    def _():
        issue(0, 0)

    @pl.when(i + 1 < n)
    def _():
        issue(i + 1, (i + 1) % 2)

    slot = i % 2
    for k in range(TOP_K):
        pltpu.make_async_copy(ys_ref.at[pl.ds(0, tm)], buf.at[slot, k], sems.at[slot]).wait()
    acc = h1_ref[...]
    for k in range(TOP_K):
        acc = acc + route_ref[:, TOP_K + k:TOP_K + k + 1] * buf[slot, k]
    o_ref[...] = _rms(acc, g_ref[...])


def _combine(dest, ys, h1, route, fg):
    T = h1.shape[0]
    tm = TM_COMBINE
    return pl.pallas_call(
        _combine_kernel,
        grid_spec=pltpu.PrefetchScalarGridSpec(
            num_scalar_prefetch=1,
            grid=(T // tm,),
            in_specs=[pl.BlockSpec(memory_space=pl.ANY),
                      pl.BlockSpec((tm, D_MODEL), lambda i, d: (i, 0)),
                      pl.BlockSpec((tm, LANES), lambda i, d: (i, 0)),
                      pl.BlockSpec((1, D_MODEL), lambda i, d: (0, 0))],
            out_specs=pl.BlockSpec((tm, D_MODEL), lambda i, d: (i, 0)),
            scratch_shapes=[pltpu.VMEM((2, TOP_K, tm, D_MODEL), F32),
                            pltpu.SemaphoreType.DMA((2,))]),
        out_shape=jax.ShapeDtypeStruct((T, D_MODEL), F32),
        compiler_params=pltpu.CompilerParams(
            dimension_semantics=("arbitrary",), vmem_limit_bytes=VMEM_LIMIT),
        name="combine",
    )(dest, ys, h1, route, fg)


def _head_slots(w, heads, dim):
    lead = w.shape[:-1]
    w = w.reshape(lead + (heads, dim))
    w = jnp.pad(w, [(0, 0)] * len(lead) + [(0, 0), (0, LANES - dim)])
    return w.reshape(lead + (heads * LANES,))


def _swap_halves(w):
    h = w.shape[-1] // 2
    return jnp.concatenate([w[..., h:], w[..., :h]], axis=-1)


def _layer(x2, batch, seq, norm1_g, w_in, w_alpha_up, b_alpha, gla_norm_g, swa_sinks, swa_norm_g,
           w_out, norm2_g, w_router, b_router, w_glu, b_glu, w_lin, b_lin, w_down, b_down):
    T = x2.shape[0]
    kb_w, vb_w = w_in[:, 2064:2192], w_in[:, 2192:2320]
    w_cat = jnp.concatenate([
        _head_slots(w_in[:, 0:256], GLA_HEADS, GLA_DK),
        _head_slots(w_in[:, 256:512], GLA_HEADS, GLA_DK),
        w_in[:, 512:1024], w_in[:, 1024:1536], w_in[:, 1552:2064],
        kb_w, _swap_halves(kb_w), vb_w, _swap_halves(vb_w),
        jnp.pad(w_in[:, 1536:1552], [(0, 0), (0, LANES - GLA_RANK)]),
    ], axis=1).astype(BF16)
    wup_p = _head_slots(w_alpha_up, GLA_HEADS, GLA_DK)
    ba_p = _head_slots(b_alpha, GLA_HEADS, GLA_DK).reshape(1, -1)

    qa, ka, va, ra, la, qb, kb, kbs, vb, vbs = _in_proj(
        x2, norm1_g.reshape(1, -1), w_cat, wup_p, ba_p)
    oa = _gla(qa, ka, va, ra, la, gla_norm_g.reshape(1, -1), batch, seq)
    ob = _swa(qb, kb, kbs, vb, vbs, swa_sinks, jnp.tile(swa_norm_g, 2).reshape(1, -1), batch, seq)

    wo = w_out.astype(BF16)
    wr_p = jnp.pad(w_router, [(0, 0), (0, LANES - N_EXPERTS)])
    br_p = jnp.pad(b_router, [(0, LANES - N_EXPERTS)], constant_values=NEG_BIG).reshape(1, -1)
    h1, hn, route, cnt = _out_route(oa, ob, x2, wo[:512], wo[512:], norm2_g.reshape(1, -1),
                                    wr_p, br_p)

    tm = TM_EXPERT
    n_tiles = T * TOP_K // tm + N_EXPERTS
    counts = cnt[0, :N_EXPERTS].astype(jnp.int32)
    padded = (counts + tm - 1) // tm * tm
    pends = jnp.cumsum(padded)
    pstarts = pends - padded
    top_idx = route[:, 0:TOP_K].astype(jnp.int32)
    rank = route[:, 2 * TOP_K:3 * TOP_K].astype(jnp.int32)
    seg_start = jnp.sum(jnp.where(top_idx[..., None] == jnp.arange(N_EXPERTS), pstarts, 0), axis=-1)
    dest = (seg_start + rank).reshape(-1)
    n_used = pends[-1] // tm
    tile_idx = jnp.minimum(jnp.arange(n_tiles, dtype=jnp.int32), n_used - 1)
    tile_expert = jnp.minimum(
        jnp.sum((pends[None, :] <= (tile_idx * tm)[:, None]).astype(jnp.int32), axis=1),
        N_EXPERTS - 1)

    n_used = n_used.reshape(1)
    xs = _dispatch(dest, pstarts + counts, padded - counts, n_used, hn, n_tiles * tm)
    ys = _experts(tile_idx, tile_expert, n_used, xs,
                  w_glu, b_glu, w_lin, b_lin, w_down, b_down)
    return dest, ys, h1, route


def kernel(x, norm1_g, w_in, w_alpha_up, b_alpha, gla_norm_g, swa_sinks, swa_norm_g, w_out,
           norm2_g, w_router, b_router, w_glu, b_glu, w_lin, b_lin, w_down, b_down, final_g):
    batch, seq, d = x.shape
    assert norm1_g.shape[0] == 1, "single-layer problem"
    x2 = x.reshape(batch * seq, d)
    dest, ys, h1, route = _layer(
        x2, batch, seq, norm1_g[0], w_in[0], w_alpha_up[0], b_alpha[0], gla_norm_g[0],
        swa_sinks[0], swa_norm_g[0], w_out[0], norm2_g[0], w_router[0], b_router[0],
        w_glu[0], b_glu[0], w_lin[0], b_lin[0], w_down[0], b_down[0])
    out = _combine(dest, ys, h1, route, final_g.reshape(1, -1))
    return out.reshape(batch, seq, d)
```

```python
import numpy as np
import jax
import jax.numpy as jnp
from jax import lax
from jax.experimental import pallas as pl
from jax.experimental.pallas import tpu as pltpu

F32 = jnp.float32
BF16 = jnp.bfloat16
HIGHEST = lax.Precision.HIGHEST

D_MODEL = 1024
GLA_HEADS = 4
GLA_DK = 64
GLA_DV = 128
GLA_RANK = 16
GLA_GATE_TAU = 16.0
GLA_CHUNK = 64
SWA_HEADS = 8
SWA_KV_HEADS = 2
SWA_HEAD_DIM = 64
SWA_WINDOW = 128
N_EXPERTS = 32
TOP_K = 4
D_FF = 1024
SWIGLU_LIMIT = 7.0
SWIGLU_ALPHA = 1.702
NORM_EPS = 1e-5

LANES = 128
VMEM_LIMIT = 56 * 1024 * 1024

TM_PROJ = 512
TL_GLA = 256
TM_ROUTE = 256
TM_EXPERT = 256
TM_COMBINE = 256
TM_DISPATCH = 256
DISPATCH_UNROLL = 8

NEG_BIG = -1e30
ROUTE_ROWS = 16

C_QA, C_KA, C_VA, C_RA, C_QB, C_KB, C_KBS, C_VB, C_VBS, C_Z, C_END = (
    0, 256, 512, 1024, 1536, 2048, 2176, 2304, 2432, 2560, 2688)
Z_PIECES = 5


def _split3(x):
    hi = x.astype(BF16).astype(F32)
    r = x - hi
    mid = r.astype(BF16).astype(F32)
    lo = (r - mid).astype(BF16).astype(F32)
    return hi, mid, lo


def _rms(x, g):
    return x * lax.rsqrt(jnp.mean(x * x, axis=-1, keepdims=True) + NORM_EPS) * g


def _in_proj_kernel(x_ref, g_ref, w_ref, wup_ref, ba_ref,
                    qa_ref, ka_ref, va_ref, ra_ref, la_ref, qb_ref, kb_ref, kbs_ref, vb_ref, vbs_ref):
    u = _rms(x_ref[...], g_ref[...]).astype(BF16)

    def proj(c0, c1):
        return jnp.dot(u, w_ref[:, c0:c1], preferred_element_type=F32)

    qa_ref[...] = proj(C_QA, C_KA).astype(BF16)
    ka_ref[...] = proj(C_KA, C_VA).astype(BF16)
    va_ref[...] = proj(C_VA, C_RA).astype(BF16)
    ra_ref[...] = proj(C_RA, C_QB).astype(BF16)
    qb_ref[...] = proj(C_QB, C_KB).astype(BF16)
    kb_ref[...] = proj(C_KB, C_KBS).astype(BF16)
    kbs_ref[...] = proj(C_KBS, C_VB).astype(BF16)
    vb_ref[...] = proj(C_VB, C_VBS).astype(BF16)
    vbs_ref[...] = proj(C_VBS, C_Z).astype(BF16)
    z = proj(C_Z, C_END)
    hi, mid, lo = _split3(z)
    piece = lax.broadcasted_iota(jnp.int32, z.shape, 1) // GLA_RANK
    zc = jnp.where((piece == 0) | (piece == 3), hi, jnp.where(piece == 2, lo, mid)).astype(BF16)
    y = jnp.dot(zc, wup_ref[...], preferred_element_type=F32) + ba_ref[...]
    log_sig = jnp.minimum(y, 0.0) - jnp.log1p(jnp.exp(-jnp.abs(y)))
    la_ref[...] = log_sig * (1.0 / GLA_GATE_TAU)


def _in_proj(x2, g1, w_cat, wup_p, ba_p):
    T = x2.shape[0]
    tm = TM_PROJ
    row = lambda w: pl.BlockSpec((tm, w), lambda i: (i, 0))
    full = lambda a: pl.BlockSpec(a.shape, lambda i: (0,) * a.ndim)
    outs = [(256, BF16), (256, BF16), (512, BF16), (512, BF16), (256, F32),
            (512, BF16), (128, BF16), (128, BF16), (128, BF16), (128, BF16)]
    return pl.pallas_call(
        _in_proj_kernel,
        grid=(T // tm,),
        in_specs=[row(D_MODEL), full(g1), full(w_cat), full(wup_p), full(ba_p)],
        out_specs=[row(w) for w, _ in outs],
        out_shape=[jax.ShapeDtypeStruct((T, w), dt) for w, dt in outs],
        compiler_params=pltpu.CompilerParams(
            dimension_semantics=("arbitrary",), vmem_limit_bytes=VMEM_LIMIT),
        name="in_proj",
    )(x2, g1, w_cat, wup_p, ba_p)


def _gla_kernel(q_ref, k_ref, v_ref, r_ref, la_ref, g_ref, o_ref, st_ref):
    @pl.when(pl.program_id(1) == 0)
    def _():
        st_ref[...] = jnp.zeros_like(st_ref)

    tl = TL_GLA
    c = GLA_CHUNK
    kw = GLA_HEADS * GLA_DK
    ri = lax.broadcasted_iota(jnp.int32, (tl, tl), 0)
    ci = lax.broadcasted_iota(jnp.int32, (tl, tl), 1)
    cum_mat = jnp.where((ci <= ri) & ((ri // c) == (ci // c)), 1.0, 0.0).astype(BF16)
    pieces = jnp.concatenate([p.astype(BF16) for p in _split3(la_ref[...])], axis=1)
    b3 = jnp.dot(cum_mat, pieces, preferred_element_type=F32)
    b_all = b3[:, :kw] + b3[:, kw:2 * kw] + b3[:, 2 * kw:]
    causal = (lax.broadcasted_iota(jnp.int32, (c, c), 0)
              >= lax.broadcasted_iota(jnp.int32, (c, c), 1))
    low_half = lax.broadcasted_iota(jnp.int32, (c, LANES), 1) < GLA_DK
    g = g_ref[...]
    for ch in range(tl // c):
        rows = slice(ch * c, (ch + 1) * c)
        b = b_all[rows]
        b_last = b[c - 1:c]
        qf = q_ref[rows, :].astype(F32)
        kf = k_ref[rows, :].astype(F32)
        q_e = (qf * jnp.exp(b) * (GLA_DK ** -0.5)).astype(BF16)
        k_e = (kf * jnp.exp(-b)).astype(BF16)
        k_t = (kf * jnp.exp(b_last - b)).astype(BF16)
        decay = jnp.exp(b_last)
        for h in range(GLA_HEADS):
            ps = slice((h // 2) * LANES, (h // 2 + 1) * LANES)
            mine = low_half if h % 2 == 0 else ~low_half
            qp, kp = q_e[:, ps], k_e[:, ps]
            qh = jnp.where(mine, qp, jnp.zeros_like(qp))
            kth = jnp.where(mine, k_t[:, ps], jnp.zeros_like(qp))
            vs = slice(h * GLA_DV, (h + 1) * GLA_DV)
            vh = v_ref[rows, vs]
            a = pl.dot(qh, kp, trans_b=True)
            a = jnp.where(causal, a, 0.0).astype(BF16)
            st = st_ref[h]
            o = (jnp.dot(a, vh, preferred_element_type=F32)
                 + pl.dot(qh, st.astype(BF16), trans_b=True))
            st_ref[h] = st * decay[:, ps] + pl.dot(vh, kth, trans_a=True)
            rh = r_ref[rows, vs].astype(F32)
            o = _rms(o, g) * (rh * jax.nn.sigmoid(rh))
            o_ref[rows, vs] = o.astype(BF16)


def _gla(qa, ka, va, ra, la, g, batch, seq):
    tl = TL_GLA
    nl = seq // tl
    row = lambda w: pl.BlockSpec((tl, w), lambda b, i: (b * nl + i, 0))
    return pl.pallas_call(
        _gla_kernel,
        grid=(batch, nl),
        in_specs=[row(256), row(256), row(512), row(512), row(256),
                  pl.BlockSpec((1, GLA_DV), lambda b, i: (0, 0))],
        out_specs=row(512),
        out_shape=jax.ShapeDtypeStruct((batch * seq, 512), BF16),
        scratch_shapes=[pltpu.VMEM((GLA_HEADS, GLA_DV, LANES), F32)],
        compiler_params=pltpu.CompilerParams(
            dimension_semantics=("arbitrary", "arbitrary"), vmem_limit_bytes=VMEM_LIMIT),
        name="gla",
    )(qa, ka, va, ra, la, g)


def _swa_kernel(sink_ref, bias_ref, q_ref, kc_ref, kp_ref, ksc_ref, ksp_ref,
                vc_ref, vp_ref, vsc_ref, vsp_ref, g_ref, o_ref):
    w = SWA_WINDOW
    cat = lambda p, cu: jnp.concatenate([p[...], cu[...]], axis=0)
    scale = jnp.asarray(SWA_HEAD_DIM ** -0.5, BF16)
    k = cat(kp_ref, kc_ref) * scale
    ks = cat(ksp_ref, ksc_ref) * scale
    v, vs = cat(vp_ref, vc_ref), cat(vsp_ref, vsc_ref)
    lane_lo = lax.broadcasted_iota(jnp.int32, (2 * w, LANES), 1) < SWA_HEAD_DIM
    zero = jnp.zeros_like(k)
    k_low = [jnp.where(lane_lo, k, zero), jnp.where(lane_lo, ks, zero)]
    k_high = [jnp.where(lane_lo, zero, ks), jnp.where(lane_lo, zero, k)]
    v_low = [jnp.where(lane_lo, v, zero), jnp.where(lane_lo, vs, zero)]
    v_high = [jnp.where(lane_lo, zero, vs), jnp.where(lane_lo, zero, v)]

    hi_half = lax.broadcasted_iota(jnp.int32, (LANES, LANES), 0) // SWA_HEAD_DIM
    hj_half = lax.broadcasted_iota(jnp.int32, (LANES, LANES), 1) // SWA_HEAD_DIM
    mean_mat = jnp.where(hi_half == hj_half, 1.0 / SWA_HEAD_DIM, 0.0).astype(F32)
    g = g_ref[...]

    def probs(s, head):
        s = s + bias_ref[0, head]
        sink = sink_ref[head]
        m = jnp.maximum(jnp.max(s, axis=-1, keepdims=True), sink)
        e = jnp.exp(s - m)
        den = jnp.sum(e, axis=-1, keepdims=True) + jnp.exp(sink - m)
        return (e * (1.0 / den)).astype(BF16)

    for pair in range(SWA_HEADS // 2):
        j = (2 * pair) // (SWA_HEADS // SWA_KV_HEADS)
        qp = q_ref[:, pair * LANES:(pair + 1) * LANES]
        p0 = probs(pl.dot(qp, k_low[j], trans_b=True), 2 * pair)
        p1 = probs(pl.dot(qp, k_high[j], trans_b=True), 2 * pair + 1)
        o = (jnp.dot(p0, v_low[j], preferred_element_type=F32)
             + jnp.dot(p1, v_high[j], preferred_element_type=F32))
        ms = jnp.dot(o * o, mean_mat, preferred_element_type=F32, precision=HIGHEST)
        o = o * lax.rsqrt(ms + NORM_EPS) * g
        o_ref[:, pair * LANES:(pair + 1) * LANES] = o.astype(BF16)


def _swa_bias():
    w = SWA_WINDOW
    slopes = 2.0 ** (-8.0 * np.arange(1, SWA_HEADS + 1, dtype=np.float64) / SWA_HEADS)
    rel = np.arange(w)[:, None] + w - np.arange(2 * w)[None, :]
    in_window = (rel >= 0) & (rel < w)
    exists = np.stack([np.arange(2 * w) >= w, np.ones(2 * w, bool)])
    valid = in_window[None] & exists[:, None, :]
    bias = -slopes[None, :, None, None] * rel[None, None].astype(np.float64)
    return np.where(valid[:, None], bias, NEG_BIG).astype(np.float32)


def _swa(qb, kb, kbs, vb, vbs, sinks, g2, batch, seq):
    w = SWA_WINDOW
    nb = seq // w
    cur = lambda width: pl.BlockSpec((w, width), lambda b, n: (b * nb + n, 0))
    prev = lambda width: pl.BlockSpec((w, width), lambda b, n: (b * nb + jnp.maximum(n - 1, 0), 0))
    return pl.pallas_call(
        _swa_kernel,
        grid=(batch, nb),
        in_specs=[pl.BlockSpec(memory_space=pltpu.SMEM),
                  pl.BlockSpec((1, SWA_HEADS, w, 2 * w), lambda b, n: (jnp.minimum(n, 1), 0, 0, 0)),
                  cur(512), cur(128), prev(128), cur(128), prev(128),
                  cur(128), prev(128), cur(128), prev(128),
                  pl.BlockSpec((1, LANES), lambda b, n: (0, 0))],
        out_specs=cur(512),
        out_shape=jax.ShapeDtypeStruct((batch * seq, 512), BF16),
        compiler_params=pltpu.CompilerParams(
            dimension_semantics=("arbitrary", "arbitrary"), vmem_limit_bytes=VMEM_LIMIT),
        name="swa",
    )(sinks, jnp.asarray(_swa_bias()), qb, kb, kb, kbs, kbs, vb, vb, vbs, vbs, g2)


def _out_route_kernel(oa_ref, ob_ref, x_ref, woa_ref, wob_ref, g_ref, wrh_ref, wrl_ref, br_ref,
                      h1_ref, hn_ref, route_ref, cnt_ref):
    @pl.when(pl.program_id(0) == 0)
    def _():
        cnt_ref[...] = jnp.zeros_like(cnt_ref)

    tm = TM_ROUTE
    h1 = (x_ref[...]
          + jnp.dot(oa_ref[...], woa_ref[...], preferred_element_type=F32)
          + jnp.dot(ob_ref[...], wob_ref[...], preferred_element_type=F32))
    h1_ref[...] = h1
    hn = _rms(h1, g_ref[...])
    hn_ref[...] = hn
    hn_hi = hn.astype(BF16)
    hn_lo = (hn - hn_hi.astype(F32)).astype(BF16)
    logits = (pl.dot(wrh_ref[...], hn_hi, trans_b=True)
              + pl.dot(wrh_ref[...], hn_lo, trans_b=True)
              + pl.dot(wrl_ref[...], hn_hi, trans_b=True)) + br_ref[...]

    eid = lax.broadcasted_iota(jnp.int32, (N_EXPERTS, tm), 0)
    work = logits
    vals, idxs, sels = [], [], []
    for _ in range(TOP_K):
        m = jnp.max(work, axis=0, keepdims=True)
        idx = jnp.min(jnp.where(work == m, eid, N_EXPERTS), axis=0, keepdims=True)
        sel = eid == idx
        vals.append(m)
        idxs.append(idx)
        sels.append(sel)
        work = jnp.where(sel, -3e38, work)
    exps = [jnp.exp(v - vals[0]) for v in vals]
    inv_den = 1.0 / (exps[0] + exps[1] + exps[2] + exps[3])

    multihot = jnp.where(sels[0] | sels[1] | sels[2] | sels[3], 1.0, 0.0)
    ri = lax.broadcasted_iota(jnp.int32, (tm, tm), 0)
    ci = lax.broadcasted_iota(jnp.int32, (tm, tm), 1)
    earlier = jnp.where(ri < ci, 1.0, 0.0).astype(BF16)
    before = (jnp.dot(multihot.astype(BF16), earlier, preferred_element_type=F32)
              + cnt_ref[:, 0:1])
    ranks = [jnp.sum(jnp.where(s, before, 0.0), axis=0, keepdims=True) for s in sels]
    route_ref[...] = jnp.concatenate(
        [i.astype(F32) for i in idxs] + [e * inv_den for e in exps] + ranks
        + [jnp.zeros((ROUTE_ROWS - 3 * TOP_K, tm), F32)], axis=0)
    cnt_ref[...] += jnp.sum(multihot, axis=1, keepdims=True)


def _out_route(oa, ob, x2, woa, wob, g2, wr_hi, wr_lo, br):
    T = x2.shape[0]
    tm = TM_ROUTE
    row = lambda w: pl.BlockSpec((tm, w), lambda i: (i, 0))
    full = lambda a: pl.BlockSpec(a.shape, lambda i: (0,) * a.ndim)
    return pl.pallas_call(
        _out_route_kernel,
        grid=(T // tm,),
        in_specs=[row(512), row(512), row(D_MODEL), full(woa), full(wob), full(g2),
                  full(wr_hi), full(wr_lo), full(br)],
        out_specs=[row(D_MODEL), row(D_MODEL),
                   pl.BlockSpec((ROUTE_ROWS, tm), lambda i: (0, i)),
                   pl.BlockSpec((N_EXPERTS, LANES), lambda i: (0, 0))],
        out_shape=[jax.ShapeDtypeStruct((T, D_MODEL), F32),
                   jax.ShapeDtypeStruct((T, D_MODEL), F32),
                   jax.ShapeDtypeStruct((ROUTE_ROWS, T), F32),
                   jax.ShapeDtypeStruct((N_EXPERTS, LANES), F32)],
        compiler_params=pltpu.CompilerParams(
            dimension_semantics=("arbitrary",), vmem_limit_bytes=VMEM_LIMIT),
        name="out_route",
    )(oa, ob, x2, woa, wob, g2, wr_hi, wr_lo, br)


def _row_copy(src, s, dst, d, sem):
    return pltpu.make_async_copy(src.at[pl.ds(s, 1)], dst.at[pl.ds(d, 1)], sem)


def _dispatch_kernel(dest_ref, pad_start_ref, pad_cnt_ref, nused_ref, hn_ref, xs_ref,
                     zero_ref, sem, zsem):
    i = pl.program_id(0)
    n = pl.num_programs(0)
    tm = TM_EXPERT
    tok_per_iter = DISPATCH_UNROLL // TOP_K

    def issue(g, carry):
        for u in range(DISPATCH_UNROLL):
            r = g * tok_per_iter + u // TOP_K
            d = dest_ref[(i * TM_DISPATCH + r) * TOP_K + u % TOP_K]
            _row_copy(hn_ref, r, xs_ref, d, sem).start()
        return carry

    lax.fori_loop(0, TM_DISPATCH // tok_per_iter, issue, 0)
    for _ in range(TOP_K):
        pltpu.make_async_copy(hn_ref, xs_ref.at[pl.ds(0, TM_DISPATCH)], sem).wait()

    @pl.when(i == n - 1)
    def _():
        zero_ref[...] = jnp.zeros_like(zero_ref)
        for e in range(N_EXPERTS):
            start, cnt = pad_start_ref[e], pad_cnt_ref[e]

            def zissue(r, carry):
                _row_copy(zero_ref, 0, xs_ref, start + r, zsem).start()
                return carry

            def zwait(r, carry):
                _row_copy(zero_ref, 0, xs_ref, start + r, zsem).wait()
                return carry

            lax.fori_loop(0, cnt, zissue, 0)
            lax.fori_loop(0, cnt, zwait, 0)

        def tail_copy(t):
            return pltpu.make_async_copy(zero_ref, xs_ref.at[pl.ds(t * tm, tm)], zsem)

        def tissue(t, carry):
            tail_copy(t).start()
            return carry

        def twait(t, carry):
            tail_copy(t).wait()
            return carry

        n_tiles = xs_ref.shape[0] // tm
        lax.fori_loop(nused_ref[0], n_tiles, tissue, 0)
        lax.fori_loop(nused_ref[0], n_tiles, twait, 0)


def _dispatch(dest, pad_start, pad_cnt, n_used, hn, n_rows):
    T = hn.shape[0]
    any_spec = pl.BlockSpec(memory_space=pl.ANY)
    return pl.pallas_call(
        _dispatch_kernel,
        grid_spec=pltpu.PrefetchScalarGridSpec(
            num_scalar_prefetch=4,
            grid=(T // TM_DISPATCH,),
            in_specs=[pl.BlockSpec((TM_DISPATCH, D_MODEL), lambda i, *_: (i, 0))],
            out_specs=any_spec,
            scratch_shapes=[pltpu.VMEM((TM_EXPERT, D_MODEL), F32),
                            pltpu.SemaphoreType.DMA(()),
                            pltpu.SemaphoreType.DMA(())]),
        out_shape=jax.ShapeDtypeStruct((n_rows, D_MODEL), F32),
        compiler_params=pltpu.CompilerParams(
            dimension_semantics=("arbitrary",), has_side_effects=True,
            vmem_limit_bytes=VMEM_LIMIT),
        name="dispatch",
    )(dest, pad_start, pad_cnt, n_used, hn)


def _experts_kernel(tile_ref, texp_ref, nused_ref, x_ref, wg_ref, bg_ref, wl_ref, bl_ref,
                    wd_ref, bd_ref, y_ref, wg_bf, wl_bf, wd_bf):
    i = pl.program_id(0)

    @pl.when(i < nused_ref[0])
    def _():
        new_expert = (i == 0) | (texp_ref[i] != texp_ref[jnp.maximum(i - 1, 0)])

        @pl.when(new_expert)
        def _():
            wg_bf[...] = wg_ref[0].astype(BF16)
            wl_bf[...] = wl_ref[0].astype(BF16)
            wd_bf[...] = wd_ref[0].astype(BF16)

        x = x_ref[...].astype(BF16)
        glu = jnp.minimum(jnp.dot(x, wg_bf[...], preferred_element_type=F32) + bg_ref[0],
                          SWIGLU_LIMIT)
        lin = jnp.clip(jnp.dot(x, wl_bf[...], preferred_element_type=F32) + bl_ref[0],
                       -SWIGLU_LIMIT, SWIGLU_LIMIT)
        hid = glu * jax.nn.sigmoid(SWIGLU_ALPHA * glu) * (lin + 1.0)
        y_ref[...] = jnp.dot(hid.astype(BF16), wd_bf[...], preferred_element_type=F32) + bd_ref[0]

    @pl.when(i >= nused_ref[0])
    def _():
        y_ref[...] = jnp.zeros_like(y_ref)


def _experts(tile_idx, tile_expert, n_used, xs, w_glu, b_glu, w_lin, b_lin, w_down, b_down):
    tm = TM_EXPERT
    n_tiles = xs.shape[0] // tm
    row = pl.BlockSpec((tm, D_MODEL), lambda i, t, e, n: (t[i], 0))
    wspec = lambda: pl.BlockSpec((1, D_MODEL, D_FF), lambda i, t, e, n: (e[i], 0, 0))
    bspec = lambda: pl.BlockSpec((1, 1, D_FF), lambda i, t, e, n: (e[i], 0, 0))
    return pl.pallas_call(
        _experts_kernel,
        grid_spec=pltpu.PrefetchScalarGridSpec(
            num_scalar_prefetch=3,
            grid=(n_tiles,),
            in_specs=[row, wspec(), bspec(), wspec(), bspec(), wspec(), bspec()],
            out_specs=pl.BlockSpec((tm, D_MODEL), lambda i, t, e, n: (i, 0)),
            scratch_shapes=[pltpu.VMEM((D_MODEL, D_FF), BF16)] * 3),
        out_shape=jax.ShapeDtypeStruct(xs.shape, F32),
        compiler_params=pltpu.CompilerParams(
            dimension_semantics=("arbitrary",), vmem_limit_bytes=VMEM_LIMIT),
        name="experts",
    )(tile_idx, tile_expert, n_used, xs, w_glu, b_glu.reshape(N_EXPERTS, 1, D_FF),
      w_lin, b_lin.reshape(N_EXPERTS, 1, D_FF), w_down, b_down.reshape(N_EXPERTS, 1, D_MODEL))


def _combine_kernel(dest_ref, ys_ref, h1_ref, gate_ref, g_ref, o_ref, buf, sems):
    i = pl.program_id(0)
    n = pl.num_programs(0)
    tm = TM_COMBINE

    def issue(tile, slot):
        def body(r, carry):
            for k in range(TOP_K):
                d = dest_ref[(tile * tm + r) * TOP_K + k]
                pltpu.make_async_copy(ys_ref.at[pl.ds(d, 1)], buf.at[slot, k, pl.ds(r, 1)],
                                      sems.at[slot]).start()
            return carry
        lax.fori_loop(0, tm, body, 0)

    @pl.when(i == 0)
    def _():
        issue(0, 0)

    @pl.when(i + 1 < n)
    def _():
        issue(i + 1, (i + 1) % 2)

    slot = i % 2
    for k in range(TOP_K):
        pltpu.make_async_copy(ys_ref.at[pl.ds(0, tm)], buf.at[slot, k], sems.at[slot]).wait()
    acc = h1_ref[...]
    for k in range(TOP_K):
        acc = acc + gate_ref[:, k:k + 1] * buf[slot, k]
    o_ref[...] = _rms(acc, g_ref[...])


def _combine(dest, ys, h1, gate, fg):
    T = h1.shape[0]
    tm = TM_COMBINE
    return pl.pallas_call(
        _combine_kernel,
        grid_spec=pltpu.PrefetchScalarGridSpec(
            num_scalar_prefetch=1,
            grid=(T // tm,),
            in_specs=[pl.BlockSpec(memory_space=pl.ANY),
                      pl.BlockSpec((tm, D_MODEL), lambda i, d: (i, 0)),
                      pl.BlockSpec((tm, TOP_K), lambda i, d: (i, 0)),
                      pl.BlockSpec((1, D_MODEL), lambda i, d: (0, 0))],
            out_specs=pl.BlockSpec((tm, D_MODEL), lambda i, d: (i, 0)),
            scratch_shapes=[pltpu.VMEM((2, TOP_K, tm, D_MODEL), F32),
                            pltpu.SemaphoreType.DMA((2,))]),
        out_shape=jax.ShapeDtypeStruct((T, D_MODEL), F32),
        compiler_params=pltpu.CompilerParams(
            dimension_semantics=("arbitrary",), vmem_limit_bytes=VMEM_LIMIT),
        name="combine",
    )(dest, ys, h1, gate, fg)


def _swap_halves(w):
    h = w.shape[-1] // 2
    return jnp.concatenate([w[..., h:], w[..., :h]], axis=-1)


def _layer(x2, batch, seq, norm1_g, w_in, w_alpha_up, b_alpha, gla_norm_g, swa_sinks, swa_norm_g,
           w_out, norm2_g, w_router, b_router, w_glu, b_glu, w_lin, b_lin, w_down, b_down):
    T = x2.shape[0]
    kb_w, vb_w = w_in[:, 2064:2192], w_in[:, 2192:2320]
    w_z = w_in[:, 1536:1552]
    w_cat = jnp.concatenate([
        w_in[:, 0:1536], w_in[:, 1552:2064],
        kb_w, _swap_halves(kb_w), vb_w, _swap_halves(vb_w),
        jnp.pad(jnp.tile(w_z, (1, Z_PIECES)), [(0, 0), (0, LANES - Z_PIECES * GLA_RANK)]),
    ], axis=1).astype(BF16)
    wup_hi = w_alpha_up.astype(BF16)
    wup_lo = (w_alpha_up - wup_hi.astype(F32)).astype(BF16)
    wup_cat = jnp.pad(jnp.concatenate([wup_hi, wup_hi, wup_hi, wup_lo, wup_lo], axis=0),
                      [(0, LANES - Z_PIECES * GLA_RANK), (0, 0)])

    qa, ka, va, ra, la, qb, kb, kbs, vb, vbs = _in_proj(
        x2, norm1_g.reshape(1, -1), w_cat, wup_cat, b_alpha.reshape(1, -1))
    oa = _gla(qa, ka, va, ra, la, gla_norm_g.reshape(1, -1), batch, seq)
    ob = _swa(qb, kb, kbs, vb, vbs, swa_sinks, jnp.tile(swa_norm_g, 2).reshape(1, -1), batch, seq)

    wo = w_out.astype(BF16)
    wr_t = w_router.T
    wr_hi = wr_t.astype(BF16)
    wr_lo = (wr_t - wr_hi.astype(F32)).astype(BF16)
    h1, hn, route, cnt = _out_route(oa, ob, x2, wo[:512], wo[512:], norm2_g.reshape(1, -1),
                                    wr_hi, wr_lo, b_router.reshape(-1, 1))

    tm = TM_EXPERT
    n_tiles = T * TOP_K // tm + N_EXPERTS
    counts = cnt[:, 0].astype(jnp.int32)
    padded = (counts + tm - 1) // tm * tm
    pends = jnp.cumsum(padded)
    pstarts = pends - padded
    top_idx = route[0:TOP_K].T.astype(jnp.int32)
    gate = route[TOP_K:2 * TOP_K].T
    rank = route[2 * TOP_K:3 * TOP_K].T.astype(jnp.int32)
    seg_start = jnp.sum(jnp.where(top_idx[..., None] == jnp.arange(N_EXPERTS), pstarts, 0), axis=-1)
    dest = (seg_start + rank).reshape(-1)
    n_used = pends[-1] // tm
    tile_idx = jnp.minimum(jnp.arange(n_tiles, dtype=jnp.int32), n_used - 1)
    tile_expert = jnp.minimum(
        jnp.sum((pends[None, :] <= (tile_idx * tm)[:, None]).astype(jnp.int32), axis=1),
        N_EXPERTS - 1)

    n_used = n_used.reshape(1)
    xs = _dispatch(dest, pstarts + counts, padded - counts, n_used, hn, n_tiles * tm)
    ys = _experts(tile_idx, tile_expert, n_used, xs,
                  w_glu, b_glu, w_lin, b_lin, w_down, b_down)
    return dest, ys, h1, gate


def kernel(x, norm1_g, w_in, w_alpha_up, b_alpha, gla_norm_g, swa_sinks, swa_norm_g, w_out,
           norm2_g, w_router, b_router, w_glu, b_glu, w_lin, b_lin, w_down, b_down, final_g):
    batch, seq, d = x.shape
    assert norm1_g.shape[0] == 1, "single-layer problem"
    x2 = x.reshape(batch * seq, d)
    dest, ys, h1, gate = _layer(
        x2, batch, seq, norm1_g[0], w_in[0], w_alpha_up[0], b_alpha[0], gla_norm_g[0],
        swa_sinks[0], swa_norm_g[0], w_out[0], norm2_g[0], w_router[0], b_router[0],
        w_glu[0], b_glu[0], w_lin[0], b_lin[0], w_down[0], b_down[0])
    out = _combine(dest, ys, h1, gate, final_g.reshape(1, -1))
    return out.reshape(batch, seq, d)
```

```python
import numpy as np
import jax
import jax.numpy as jnp
from jax import lax
from jax.experimental import pallas as pl
from jax.experimental.pallas import tpu as pltpu

F32 = jnp.float32
BF16 = jnp.bfloat16
HIGHEST = lax.Precision.HIGHEST

D_MODEL = 1024
GLA_HEADS = 4
GLA_DK = 64
GLA_DV = 128
GLA_RANK = 16
GLA_GATE_TAU = 16.0
GLA_CHUNK = 64
SWA_HEADS = 8
SWA_KV_HEADS = 2
SWA_HEAD_DIM = 64
SWA_WINDOW = 128
N_EXPERTS = 32
TOP_K = 4
D_FF = 1024
SWIGLU_LIMIT = 7.0
SWIGLU_ALPHA = 1.702
NORM_EPS = 1e-5

LANES = 128
VMEM_LIMIT = 56 * 1024 * 1024

TM_PROJ = 512
TL_GLA = 256
TM_ROUTE = 512
TM_EXPERT = 256
TM_COMBINE = 256
SWA_ROWS = 128
TM_DISPATCH = 256
DISPATCH_UNROLL = 8

NEG_BIG = -1e30
ROUTE_ROWS = 16

C_QA, C_KA, C_VA, C_RA, C_QB, C_KB, C_KBS, C_VB, C_VBS, C_Z, C_END = (
    0, 256, 512, 1024, 1536, 2048, 2176, 2304, 2432, 2560, 2688)
Z_PIECES = 5


def _split3(x):
    hi = x.astype(BF16).astype(F32)
    r = x - hi
    mid = r.astype(BF16).astype(F32)
    lo = (r - mid).astype(BF16).astype(F32)
    return hi, mid, lo


def _rms(x, g):
    return x * lax.rsqrt(jnp.mean(x * x, axis=-1, keepdims=True) + NORM_EPS) * g


def _in_proj_kernel(x_ref, g_ref, w_ref, wup_ref, ba_ref,
                    qa_ref, ka_ref, va_ref, ra_ref, la_ref, qb_ref, kb_ref, kbs_ref, vb_ref, vbs_ref):
    u = _rms(x_ref[...], g_ref[...]).astype(BF16)

    def proj(c0, c1):
        return jnp.dot(u, w_ref[:, c0:c1], preferred_element_type=F32)

    qa_ref[...] = proj(C_QA, C_KA).astype(BF16)
    ka_ref[...] = proj(C_KA, C_VA).astype(BF16)
    va_ref[...] = proj(C_VA, C_RA).astype(BF16)
    ra_ref[...] = proj(C_RA, C_QB).astype(BF16)
    qb_ref[...] = proj(C_QB, C_KB).astype(BF16)
    kb_ref[...] = proj(C_KB, C_KBS).astype(BF16)
    kbs_ref[...] = proj(C_KBS, C_VB).astype(BF16)
    vb_ref[...] = proj(C_VB, C_VBS).astype(BF16)
    vbs_ref[...] = proj(C_VBS, C_Z).astype(BF16)
    z = proj(C_Z, C_END)
    hi, mid, lo = _split3(z)
    piece = lax.broadcasted_iota(jnp.int32, z.shape, 1) // GLA_RANK
    zc = jnp.where((piece == 0) | (piece == 3), hi, jnp.where(piece == 2, lo, mid)).astype(BF16)
    y = jnp.dot(zc, wup_ref[...], preferred_element_type=F32) + ba_ref[...]
    log_sig = jnp.minimum(y, 0.0) - jnp.log1p(jnp.exp(-jnp.abs(y)))
    la_ref[...] = log_sig * (1.0 / GLA_GATE_TAU)


def _in_proj(x2, g1, w_cat, wup_p, ba_p):
    T = x2.shape[0]
    tm = TM_PROJ
    row = lambda w: pl.BlockSpec((tm, w), lambda i: (i, 0))
    full = lambda a: pl.BlockSpec(a.shape, lambda i: (0,) * a.ndim)
    outs = [(256, BF16), (256, BF16), (512, BF16), (512, BF16), (256, F32),
            (512, BF16), (128, BF16), (128, BF16), (128, BF16), (128, BF16)]
    return pl.pallas_call(
        _in_proj_kernel,
        grid=(T // tm,),
        in_specs=[row(D_MODEL), full(g1), full(w_cat), full(wup_p), full(ba_p)],
        out_specs=[row(w) for w, _ in outs],
        out_shape=[jax.ShapeDtypeStruct((T, w), dt) for w, dt in outs],
        compiler_params=pltpu.CompilerParams(
            dimension_semantics=("arbitrary",), vmem_limit_bytes=VMEM_LIMIT),
        name="in_proj",
    )(x2, g1, w_cat, wup_p, ba_p)


def _gla_kernel(q_ref, k_ref, v_ref, r_ref, la_ref, g_ref, cum_ref, o_ref, st_ref):
    @pl.when(pl.program_id(1) == 0)
    def _():
        st_ref[...] = jnp.zeros_like(st_ref)

    tl = TL_GLA
    c = GLA_CHUNK
    kw = GLA_HEADS * GLA_DK
    pieces = jnp.concatenate([p.astype(BF16) for p in _split3(la_ref[...])], axis=1)
    b3 = jnp.dot(cum_ref[...], pieces, preferred_element_type=F32)
    b_all = b3[:, :kw] + b3[:, kw:2 * kw] + b3[:, 2 * kw:]
    causal = (lax.broadcasted_iota(jnp.int32, (c, c), 0)
              >= lax.broadcasted_iota(jnp.int32, (c, c), 1))
    low_half = lax.broadcasted_iota(jnp.int32, (c, LANES), 1) < GLA_DK
    g = g_ref[...]
    for ch in range(tl // c):
        rows = slice(ch * c, (ch + 1) * c)
        b = b_all[rows]
        b_last = b[c - 1:c]
        qf = q_ref[rows, :].astype(F32)
        kf = k_ref[rows, :].astype(F32)
        q_e = (qf * jnp.exp(b) * (GLA_DK ** -0.5)).astype(BF16)
        k_e = (kf * jnp.exp(-b)).astype(BF16)
        k_t = (kf * jnp.exp(b_last - b)).astype(BF16)
        decay = jnp.exp(b_last)
        for h in range(GLA_HEADS):
            ps = slice((h // 2) * LANES, (h // 2 + 1) * LANES)
            mine = low_half if h % 2 == 0 else ~low_half
            qp, kp = q_e[:, ps], k_e[:, ps]
            qh = jnp.where(mine, qp, jnp.zeros_like(qp))
            kth = jnp.where(mine, k_t[:, ps], jnp.zeros_like(qp))
            vs = slice(h * GLA_DV, (h + 1) * GLA_DV)
            vh = v_ref[rows, vs]
            a = pl.dot(qh, kp, trans_b=True)
            a = jnp.where(causal, a, 0.0).astype(BF16)
            st = st_ref[h]
            o = (jnp.dot(a, vh, preferred_element_type=F32)
                 + pl.dot(qh, st.astype(BF16), trans_b=True))
            st_ref[h] = st * decay[:, ps] + pl.dot(vh, kth, trans_a=True)
            rh = r_ref[rows, vs].astype(F32)
            o = _rms(o, g) * (rh * jax.nn.sigmoid(rh))
            o_ref[rows, vs] = o.astype(BF16)


def _gla(qa, ka, va, ra, la, g, batch, seq):
    tl = TL_GLA
    nl = seq // tl
    row = lambda w: pl.BlockSpec((tl, w), lambda b, i: (b * nl + i, 0))
    r = np.arange(tl)
    cum = jnp.asarray((r[None, :] <= r[:, None])
                      & (r[None, :] // GLA_CHUNK == r[:, None] // GLA_CHUNK), BF16)
    return pl.pallas_call(
        _gla_kernel,
        grid=(batch, nl),
        in_specs=[row(256), row(256), row(512), row(512), row(256),
                  pl.BlockSpec((1, GLA_DV), lambda b, i: (0, 0)),
                  pl.BlockSpec((tl, tl), lambda b, i: (0, 0))],
        out_specs=row(512),
        out_shape=jax.ShapeDtypeStruct((batch * seq, 512), BF16),
        scratch_shapes=[pltpu.VMEM((GLA_HEADS, GLA_DV, LANES), F32)],
        compiler_params=pltpu.CompilerParams(
            dimension_semantics=("arbitrary", "arbitrary"), vmem_limit_bytes=VMEM_LIMIT),
        name="gla",
    )(qa, ka, va, ra, la, g, cum)


def _swa_kernel(sink_ref, bias_ref, q_ref, kc_ref, kp_ref, ksc_ref, ksp_ref,
                vc_ref, vp_ref, vsc_ref, vsp_ref, g_ref, o_ref):
    w = SWA_WINDOW
    cat = lambda p, cu: jnp.concatenate([p[...], cu[...]], axis=0)
    scale = jnp.asarray(SWA_HEAD_DIM ** -0.5, BF16)
    k = cat(kp_ref, kc_ref) * scale
    ks = cat(ksp_ref, ksc_ref) * scale
    v, vs = cat(vp_ref, vc_ref), cat(vsp_ref, vsc_ref)
    lane_lo = lax.broadcasted_iota(jnp.int32, (2 * w, LANES), 1) < SWA_HEAD_DIM
    zero = jnp.zeros_like(k)
    k_low = [jnp.where(lane_lo, k, zero), jnp.where(lane_lo, ks, zero)]
    k_high = [jnp.where(lane_lo, zero, ks), jnp.where(lane_lo, zero, k)]
    v_low = [jnp.where(lane_lo, v, zero), jnp.where(lane_lo, vs, zero)]
    v_high = [jnp.where(lane_lo, zero, vs), jnp.where(lane_lo, zero, v)]

    hi_half = lax.broadcasted_iota(jnp.int32, (LANES, LANES), 0) // SWA_HEAD_DIM
    hj_half = lax.broadcasted_iota(jnp.int32, (LANES, LANES), 1) // SWA_HEAD_DIM
    mean_mat = jnp.where(hi_half == hj_half, 1.0 / SWA_HEAD_DIM, 0.0).astype(F32)
    g = g_ref[...]

    def probs(s, head, rows):
        s = s + bias_ref[0, head, rows, :]
        sink = sink_ref[head]
        m = jnp.maximum(jnp.max(s, axis=-1, keepdims=True), sink)
        e = jnp.exp(s - m)
        den = jnp.sum(e, axis=-1, keepdims=True) + jnp.exp(sink - m)
        return (e * (1.0 / den)).astype(BF16)

    for pair in range(SWA_HEADS // 2):
        j = (2 * pair) // (SWA_HEADS // SWA_KV_HEADS)
        cols = slice(pair * LANES, (pair + 1) * LANES)
        for rb in range(w // SWA_ROWS):
            rows = slice(rb * SWA_ROWS, (rb + 1) * SWA_ROWS)
            qp = q_ref[rows, cols]
            p0 = probs(pl.dot(qp, k_low[j], trans_b=True), 2 * pair, rows)
            p1 = probs(pl.dot(qp, k_high[j], trans_b=True), 2 * pair + 1, rows)
            o = (jnp.dot(p0, v_low[j], preferred_element_type=F32)
                 + jnp.dot(p1, v_high[j], preferred_element_type=F32))
            ms = jnp.dot(o * o, mean_mat, preferred_element_type=F32, precision=HIGHEST)
            o = o * lax.rsqrt(ms + NORM_EPS) * g
            o_ref[rows, cols] = o.astype(BF16)


def _swa_bias():
    w = SWA_WINDOW
    slopes = 2.0 ** (-8.0 * np.arange(1, SWA_HEADS + 1, dtype=np.float64) / SWA_HEADS)
    rel = np.arange(w)[:, None] + w - np.arange(2 * w)[None, :]
    in_window = (rel >= 0) & (rel < w)
    exists = np.stack([np.arange(2 * w) >= w, np.ones(2 * w, bool)])
    valid = in_window[None] & exists[:, None, :]
    bias = -slopes[None, :, None, None] * rel[None, None].astype(np.float64)
    return np.where(valid[:, None], bias, NEG_BIG).astype(np.float32)


def _swa(qb, kb, kbs, vb, vbs, sinks, g2, batch, seq):
    w = SWA_WINDOW
    nb = seq // w
    cur = lambda width: pl.BlockSpec((w, width), lambda b, n: (b * nb + n, 0))
    prev = lambda width: pl.BlockSpec((w, width), lambda b, n: (b * nb + jnp.maximum(n - 1, 0), 0))
    return pl.pallas_call(
        _swa_kernel,
        grid=(batch, nb),
        in_specs=[pl.BlockSpec(memory_space=pltpu.SMEM),
                  pl.BlockSpec((1, SWA_HEADS, w, 2 * w), lambda b, n: (jnp.minimum(n, 1), 0, 0, 0)),
                  cur(512), cur(128), prev(128), cur(128), prev(128),
                  cur(128), prev(128), cur(128), prev(128),
                  pl.BlockSpec((1, LANES), lambda b, n: (0, 0))],
        out_specs=cur(512),
        out_shape=jax.ShapeDtypeStruct((batch * seq, 512), BF16),
        compiler_params=pltpu.CompilerParams(
            dimension_semantics=("arbitrary", "arbitrary"), vmem_limit_bytes=VMEM_LIMIT),
        name="swa",
    )(sinks, jnp.asarray(_swa_bias()), qb, kb, kb, kbs, kbs, vb, vb, vbs, vbs, g2)


def _out_route_kernel(oa_ref, ob_ref, x_ref, woa_ref, wob_ref, g_ref, wrh_ref, wrl_ref, br_ref,
                      earlier_ref, h1_ref, hn_ref, route_ref, cnt_ref):
    @pl.when(pl.program_id(0) == 0)
    def _():
        cnt_ref[...] = jnp.zeros_like(cnt_ref)

    tm = TM_ROUTE
    h1 = (x_ref[...]
          + jnp.dot(oa_ref[...], woa_ref[...], preferred_element_type=F32)
          + jnp.dot(ob_ref[...], wob_ref[...], preferred_element_type=F32))
    h1_ref[...] = h1
    hn = _rms(h1, g_ref[...])
    hn_ref[...] = hn
    hn_hi = hn.astype(BF16)
    hn_lo = (hn - hn_hi.astype(F32)).astype(BF16)
    logits = (pl.dot(wrh_ref[...], hn_hi, trans_b=True)
              + pl.dot(wrh_ref[...], hn_lo, trans_b=True)
              + pl.dot(wrl_ref[...], hn_hi, trans_b=True)) + br_ref[...]

    eid = lax.broadcasted_iota(jnp.int32, (N_EXPERTS, tm), 0)
    work = logits
    vals, idxs, sels = [], [], []
    for _ in range(TOP_K):
        m = jnp.max(work, axis=0, keepdims=True)
        idx = jnp.min(jnp.where(work == m, eid, N_EXPERTS), axis=0, keepdims=True)
        sel = eid == idx
        vals.append(m)
        idxs.append(idx)
        sels.append(sel)
        work = jnp.where(sel, -3e38, work)
    exps = [jnp.exp(v - vals[0]) for v in vals]
    inv_den = 1.0 / (exps[0] + exps[1] + exps[2] + exps[3])

    multihot = jnp.where(sels[0] | sels[1] | sels[2] | sels[3], 1.0, 0.0)
    before = (jnp.dot(multihot.astype(BF16), earlier_ref[...], preferred_element_type=F32)
              + cnt_ref[:, 0:1])
    ranks = [jnp.sum(jnp.where(s, before, 0.0), axis=0, keepdims=True) for s in sels]
    route_ref[...] = jnp.concatenate(
        [i.astype(F32) for i in idxs] + [e * inv_den for e in exps] + ranks
        + [jnp.zeros((ROUTE_ROWS - 3 * TOP_K, tm), F32)], axis=0)
    cnt_ref[...] += jnp.sum(multihot, axis=1, keepdims=True)


def _out_route(oa, ob, x2, woa, wob, g2, wr_hi, wr_lo, br):
    T = x2.shape[0]
    tm = TM_ROUTE
    row = lambda w: pl.BlockSpec((tm, w), lambda i: (i, 0))
    full = lambda a: pl.BlockSpec(a.shape, lambda i: (0,) * a.ndim)
    earlier = jnp.asarray(np.triu(np.ones((tm, tm), np.float32), k=1), BF16)
    return pl.pallas_call(
        _out_route_kernel,
        grid=(T // tm,),
        in_specs=[row(512), row(512), row(D_MODEL), full(woa), full(wob), full(g2),
                  full(wr_hi), full(wr_lo), full(br), full(earlier)],
        out_specs=[row(D_MODEL), row(D_MODEL),
                   pl.BlockSpec((ROUTE_ROWS, tm), lambda i: (0, i)),
                   pl.BlockSpec((N_EXPERTS, LANES), lambda i: (0, 0))],
        out_shape=[jax.ShapeDtypeStruct((T, D_MODEL), F32),
                   jax.ShapeDtypeStruct((T, D_MODEL), F32),
                   jax.ShapeDtypeStruct((ROUTE_ROWS, T), F32),
                   jax.ShapeDtypeStruct((N_EXPERTS, LANES), F32)],
        compiler_params=pltpu.CompilerParams(
            dimension_semantics=("arbitrary",), vmem_limit_bytes=VMEM_LIMIT),
        name="out_route",
    )(oa, ob, x2, woa, wob, g2, wr_hi, wr_lo, br, earlier)


def _row_copy(src, s, dst, d, sem):
    return pltpu.make_async_copy(src.at[pl.ds(s, 1)], dst.at[pl.ds(d, 1)], sem)


def _dispatch_kernel(dest_ref, pad_start_ref, pad_cnt_ref, nused_ref, hn_ref, xs_ref,
                     zero_ref, sem, zsem):
    i = pl.program_id(0)
    n = pl.num_programs(0)
    tm = TM_EXPERT
    tok_per_iter = DISPATCH_UNROLL // TOP_K

    def issue(g, carry):
        for u in range(DISPATCH_UNROLL):
            r = g * tok_per_iter + u // TOP_K
            d = dest_ref[(i * TM_DISPATCH + r) * TOP_K + u % TOP_K]
            _row_copy(hn_ref, r, xs_ref, d, sem).start()
        return carry

    lax.fori_loop(0, TM_DISPATCH // tok_per_iter, issue, 0)
    for _ in range(TOP_K):
        pltpu.make_async_copy(hn_ref, xs_ref.at[pl.ds(0, TM_DISPATCH)], sem).wait()

    @pl.when(i == n - 1)
    def _():
        zero_ref[...] = jnp.zeros_like(zero_ref)
        for e in range(N_EXPERTS):
            start, cnt = pad_start_ref[e], pad_cnt_ref[e]

            def zissue(r, carry):
                _row_copy(zero_ref, 0, xs_ref, start + r, zsem).start()
                return carry

            def zwait(r, carry):
                _row_copy(zero_ref, 0, xs_ref, start + r, zsem).wait()
                return carry

            lax.fori_loop(0, cnt, zissue, 0)
            lax.fori_loop(0, cnt, zwait, 0)

        def tail_copy(t):
            return pltpu.make_async_copy(zero_ref, xs_ref.at[pl.ds(t * tm, tm)], zsem)

        def tissue(t, carry):
            tail_copy(t).start()
            return carry

        def twait(t, carry):
            tail_copy(t).wait()
            return carry

        n_tiles = xs_ref.shape[0] // tm
        lax.fori_loop(nused_ref[0], n_tiles, tissue, 0)
        lax.fori_loop(nused_ref[0], n_tiles, twait, 0)


def _dispatch(dest, pad_start, pad_cnt, n_used, hn, n_rows):
    T = hn.shape[0]
    any_spec = pl.BlockSpec(memory_space=pl.ANY)
    return pl.pallas_call(
        _dispatch_kernel,
        grid_spec=pltpu.PrefetchScalarGridSpec(
            num_scalar_prefetch=4,
            grid=(T // TM_DISPATCH,),
            in_specs=[pl.BlockSpec((TM_DISPATCH, D_MODEL), lambda i, *_: (i, 0))],
            out_specs=any_spec,
            scratch_shapes=[pltpu.VMEM((TM_EXPERT, D_MODEL), F32),
                            pltpu.SemaphoreType.DMA(()),
                            pltpu.SemaphoreType.DMA(())]),
        out_shape=jax.ShapeDtypeStruct((n_rows, D_MODEL), F32),
        compiler_params=pltpu.CompilerParams(
            dimension_semantics=("arbitrary",), has_side_effects=True,
            vmem_limit_bytes=VMEM_LIMIT),
        name="dispatch",
    )(dest, pad_start, pad_cnt, n_used, hn)


def _experts_kernel(tile0_ref, ntile_ref, xs_ref, wg_ref, bg_ref, wl_ref, bl_ref, wd_ref, bd_ref,
                    ys_ref, wg_bf, wl_bf, wd_bf, xbuf, ybuf, xsem, ysem):
    e = pl.program_id(0)
    tm = TM_EXPERT
    t0, nt = tile0_ref[e], ntile_ref[e]

    def x_load(tile, slot):
        return pltpu.make_async_copy(xs_ref.at[pl.ds(tile * tm, tm)], xbuf.at[slot], xsem.at[slot])

    def y_store(tile, slot):
        return pltpu.make_async_copy(ybuf.at[slot], ys_ref.at[pl.ds(tile * tm, tm)], ysem.at[slot])

    @pl.when((e == 0) & (nt > 0))
    def _():
        x_load(t0, 0).start()

    @pl.when(nt > 0)
    def _():
        wg_bf[...] = wg_ref[0].astype(BF16)
        wl_bf[...] = wl_ref[0].astype(BF16)
        wd_bf[...] = wd_ref[0].astype(BF16)

    def tile_body(j, carry):
        slot = j % 2

        @pl.when(j + 1 < nt)
        def _():
            x_load(t0 + j + 1, 1 - slot).start()

        x_load(t0 + j, slot).wait()

        @pl.when(j >= 2)
        def _():
            y_store(t0 + j - 2, slot).wait()

        x = xbuf[slot].astype(BF16)
        glu = jnp.minimum(jnp.dot(x, wg_bf[...], preferred_element_type=F32) + bg_ref[0],
                          SWIGLU_LIMIT)
        lin = jnp.clip(jnp.dot(x, wl_bf[...], preferred_element_type=F32) + bl_ref[0],
                       -SWIGLU_LIMIT, SWIGLU_LIMIT)
        hid = glu * jax.nn.sigmoid(SWIGLU_ALPHA * glu) * (lin + 1.0)
        ybuf[slot] = jnp.dot(hid.astype(BF16), wd_bf[...], preferred_element_type=F32) + bd_ref[0]
        y_store(t0 + j, slot).start()
        return carry

    lax.fori_loop(0, nt, tile_body, 0)

    @pl.when(e + 1 < pl.num_programs(0))
    def _():
        @pl.when(ntile_ref[e + 1] > 0)
        def _():
            x_load(tile0_ref[e + 1], 0).start()

    @pl.when(nt >= 2)
    def _():
        y_store(t0 + nt - 2, nt % 2).wait()

    @pl.when(nt >= 1)
    def _():
        y_store(t0 + nt - 1, (nt - 1) % 2).wait()

    @pl.when(e == pl.num_programs(0) - 1)
    def _():
        ybuf[0] = jnp.zeros((tm, D_MODEL), F32)
        n_tiles = ys_ref.shape[0] // tm

        def tail_issue(t, carry):
            y_store(t, 0).start()
            return carry

        def tail_wait(t, carry):
            y_store(t, 0).wait()
            return carry

        lax.fori_loop(t0 + nt, n_tiles, tail_issue, 0)
        lax.fori_loop(t0 + nt, n_tiles, tail_wait, 0)


def _experts(tile0, ntile, xs, w_glu, b_glu, w_lin, b_lin, w_down, b_down):
    tm = TM_EXPERT
    any_spec = pl.BlockSpec(memory_space=pl.ANY)
    wspec = lambda: pl.BlockSpec((1, D_MODEL, D_FF), lambda e, *_: (e, 0, 0))
    bspec = lambda: pl.BlockSpec((1, 1, D_FF), lambda e, *_: (e, 0, 0))
    return pl.pallas_call(
        _experts_kernel,
        grid_spec=pltpu.PrefetchScalarGridSpec(
            num_scalar_prefetch=2,
            grid=(N_EXPERTS,),
            in_specs=[any_spec, wspec(), bspec(), wspec(), bspec(), wspec(), bspec()],
            out_specs=any_spec,
            scratch_shapes=[pltpu.VMEM((D_MODEL, D_FF), BF16)] * 3
            + [pltpu.VMEM((2, tm, D_MODEL), F32), pltpu.VMEM((2, tm, D_MODEL), F32),
               pltpu.SemaphoreType.DMA((2,)), pltpu.SemaphoreType.DMA((2,))]),
        out_shape=jax.ShapeDtypeStruct(xs.shape, F32),
        compiler_params=pltpu.CompilerParams(
            dimension_semantics=("arbitrary",), vmem_limit_bytes=VMEM_LIMIT,
            has_side_effects=True),
        name="experts",
    )(tile0, ntile, xs, w_glu, b_glu.reshape(N_EXPERTS, 1, D_FF),
      w_lin, b_lin.reshape(N_EXPERTS, 1, D_FF), w_down, b_down.reshape(N_EXPERTS, 1, D_MODEL))


def _combine_kernel(dest_ref, ys_ref, h1_ref, gate_ref, g_ref, o_ref, buf, sems):
    i = pl.program_id(0)
    n = pl.num_programs(0)
    tm = TM_COMBINE

    def issue(tile, slot):
        def body(r, carry):
            for k in range(TOP_K):
                d = dest_ref[(tile * tm + r) * TOP_K + k]
                pltpu.make_async_copy(ys_ref.at[pl.ds(d, 1)], buf.at[slot, k, pl.ds(r, 1)],
                                      sems.at[slot]).start()
            return carry
        lax.fori_loop(0, tm, body, 0)

    @pl.when(i == 0)
    def _():
        issue(0, 0)

    @pl.when(i + 1 < n)
    def _():
        issue(i + 1, (i + 1) % 2)

    slot = i % 2
    for k in range(TOP_K):
        pltpu.make_async_copy(ys_ref.at[pl.ds(0, tm)], buf.at[slot, k], sems.at[slot]).wait()
    acc = h1_ref[...]
    for k in range(TOP_K):
        acc = acc + gate_ref[:, k:k + 1] * buf[slot, k]
    o_ref[...] = _rms(acc, g_ref[...])


def _combine(dest, ys, h1, gate, fg):
    T = h1.shape[0]
    tm = TM_COMBINE
    return pl.pallas_call(
        _combine_kernel,
        grid_spec=pltpu.PrefetchScalarGridSpec(
            num_scalar_prefetch=1,
            grid=(T // tm,),
            in_specs=[pl.BlockSpec(memory_space=pl.ANY),
                      pl.BlockSpec((tm, D_MODEL), lambda i, d: (i, 0)),
                      pl.BlockSpec((tm, TOP_K), lambda i, d: (i, 0)),
                      pl.BlockSpec((1, D_MODEL), lambda i, d: (0, 0))],
            out_specs=pl.BlockSpec((tm, D_MODEL), lambda i, d: (i, 0)),
            scratch_shapes=[pltpu.VMEM((2, TOP_K, tm, D_MODEL), F32),
                            pltpu.SemaphoreType.DMA((2,))]),
        out_shape=jax.ShapeDtypeStruct((T, D_MODEL), F32),
        compiler_params=pltpu.CompilerParams(
            dimension_semantics=("arbitrary",), vmem_limit_bytes=VMEM_LIMIT),
        name="combine",
    )(dest, ys, h1, gate, fg)


def _swap_halves(w):
    h = w.shape[-1] // 2
    return jnp.concatenate([w[..., h:], w[..., :h]], axis=-1)


def _layer(x2, batch, seq, norm1_g, w_in, w_alpha_up, b_alpha, gla_norm_g, swa_sinks, swa_norm_g,
           w_out, norm2_g, w_router, b_router, w_glu, b_glu, w_lin, b_lin, w_down, b_down):
    T = x2.shape[0]
    kb_w, vb_w = w_in[:, 2064:2192], w_in[:, 2192:2320]
    w_z = w_in[:, 1536:1552]
    w_cat = jnp.concatenate([
        w_in[:, 0:1536], w_in[:, 1552:2064],
        kb_w, _swap_halves(kb_w), vb_w, _swap_halves(vb_w),
        jnp.pad(jnp.tile(w_z, (1, Z_PIECES)), [(0, 0), (0, LANES - Z_PIECES * GLA_RANK)]),
    ], axis=1).astype(BF16)
    wup_hi = w_alpha_up.astype(BF16)
    wup_lo = (w_alpha_up - wup_hi.astype(F32)).astype(BF16)
    wup_cat = jnp.pad(jnp.concatenate([wup_hi, wup_hi, wup_hi, wup_lo, wup_lo], axis=0),
                      [(0, LANES - Z_PIECES * GLA_RANK), (0, 0)])

    qa, ka, va, ra, la, qb, kb, kbs, vb, vbs = _in_proj(
        x2, norm1_g.reshape(1, -1), w_cat, wup_cat, b_alpha.reshape(1, -1))
    oa = _gla(qa, ka, va, ra, la, gla_norm_g.reshape(1, -1), batch, seq)
    ob = _swa(qb, kb, kbs, vb, vbs, swa_sinks, jnp.tile(swa_norm_g, 2).reshape(1, -1), batch, seq)

    wo = w_out.astype(BF16)
    wr_t = w_router.T
    wr_hi = wr_t.astype(BF16)
    wr_lo = (wr_t - wr_hi.astype(F32)).astype(BF16)
    h1, hn, route, cnt = _out_route(oa, ob, x2, wo[:512], wo[512:], norm2_g.reshape(1, -1),
                                    wr_hi, wr_lo, b_router.reshape(-1, 1))

    tm = TM_EXPERT
    n_tiles = T * TOP_K // tm + N_EXPERTS
    counts = cnt[:, 0].astype(jnp.int32)
    padded = (counts + tm - 1) // tm * tm
    pends = jnp.cumsum(padded)
    pstarts = pends - padded
    top_idx = route[0:TOP_K].T.astype(jnp.int32)
    gate = route[TOP_K:2 * TOP_K].T
    rank = route[2 * TOP_K:3 * TOP_K].T.astype(jnp.int32)
    seg_start = jnp.sum(jnp.where(top_idx[..., None] == jnp.arange(N_EXPERTS), pstarts, 0), axis=-1)
    dest = (seg_start + rank).reshape(-1)
    n_used = (pends[-1] // tm).reshape(1)
    xs = _dispatch(dest, pstarts + counts, padded - counts, n_used, hn, n_tiles * tm)
    ys = _experts(pstarts // tm, padded // tm, xs, w_glu, b_glu, w_lin, b_lin, w_down, b_down)
    return dest, ys, h1, gate


def kernel(x, norm1_g, w_in, w_alpha_up, b_alpha, gla_norm_g, swa_sinks, swa_norm_g, w_out,
           norm2_g, w_router, b_router, w_glu, b_glu, w_lin, b_lin, w_down, b_down, final_g):
    batch, seq, d = x.shape
    assert norm1_g.shape[0] == 1, "single-layer problem"
    x2 = x.reshape(batch * seq, d)
    dest, ys, h1, gate = _layer(
        x2, batch, seq, norm1_g[0], w_in[0], w_alpha_up[0], b_alpha[0], gla_norm_g[0],
        swa_sinks[0], swa_norm_g[0], w_out[0], norm2_g[0], w_router[0], b_router[0],
        w_glu[0], b_glu[0], w_lin[0], b_lin[0], w_down[0], b_down[0])
    out = _combine(dest, ys, h1, gate, final_g.reshape(1, -1))
    return out.reshape(batch, seq, d)
```

```python
import numpy as np
import jax
import jax.numpy as jnp
from jax import lax
from jax.experimental import pallas as pl
from jax.experimental.pallas import tpu as pltpu

F32 = jnp.float32
BF16 = jnp.bfloat16
HIGHEST = lax.Precision.HIGHEST

D_MODEL = 1024
GLA_HEADS = 4
GLA_DK = 64
GLA_DV = 128
GLA_RANK = 16
GLA_GATE_TAU = 16.0
GLA_CHUNK = 64
SWA_HEADS = 8
SWA_KV_HEADS = 2
SWA_HEAD_DIM = 64
SWA_WINDOW = 128
N_EXPERTS = 32
TOP_K = 4
D_FF = 1024
SWIGLU_LIMIT = 7.0
SWIGLU_ALPHA = 1.702
NORM_EPS = 1e-5

LANES = 128
VMEM_LIMIT = 56 * 1024 * 1024

TM_PROJ = 512
TL_GLA = 256
TM_ROUTE = 512
TM_EXPERT = 512
TM_COMBINE = 256
SWA_ROWS = 128
TM_DISPATCH = 256

NEG_BIG = -1e30
ROUTE_ROWS = 16

C_QA, C_KA, C_VA, C_RA, C_QB, C_KB, C_KBS, C_VB, C_VBS, C_Z, C_END = (
    0, 256, 512, 1024, 1536, 2048, 2176, 2304, 2432, 2560, 2688)
Z_PIECES = 5


def _split3(x):
    hi = x.astype(BF16).astype(F32)
    r = x - hi
    mid = r.astype(BF16).astype(F32)
    lo = (r - mid).astype(BF16).astype(F32)
    return hi, mid, lo


def _rms(x, g):
    return x * lax.rsqrt(jnp.mean(x * x, axis=-1, keepdims=True) + NORM_EPS) * g


def _in_proj_kernel(x_ref, g_ref, w_ref, wup_ref, ba_ref,
                    qa_ref, ka_ref, va_ref, ra_ref, la_ref, qb_ref, kb_ref, kbs_ref, vb_ref, vbs_ref):
    u = _rms(x_ref[...], g_ref[...]).astype(BF16)

    def proj(c0, c1):
        return jnp.dot(u, w_ref[:, c0:c1], preferred_element_type=F32)

    qa_ref[...] = proj(C_QA, C_KA).astype(BF16)
    ka_ref[...] = proj(C_KA, C_VA).astype(BF16)
    va_ref[...] = proj(C_VA, C_RA).astype(BF16)
    ra_ref[...] = proj(C_RA, C_QB).astype(BF16)
    qb_ref[...] = proj(C_QB, C_KB).astype(BF16)
    kb_ref[...] = proj(C_KB, C_KBS).astype(BF16)
    kbs_ref[...] = proj(C_KBS, C_VB).astype(BF16)
    vb_ref[...] = proj(C_VB, C_VBS).astype(BF16)
    vbs_ref[...] = proj(C_VBS, C_Z).astype(BF16)
    z = proj(C_Z, C_END)
    hi, mid, lo = _split3(z)
    piece = lax.broadcasted_iota(jnp.int32, z.shape, 1) // GLA_RANK
    zc = jnp.where((piece == 0) | (piece == 3), hi, jnp.where(piece == 2, lo, mid)).astype(BF16)
    y = jnp.dot(zc, wup_ref[...], preferred_element_type=F32) + ba_ref[...]
    log_sig = jnp.minimum(y, 0.0) - jnp.log1p(jnp.exp(-jnp.abs(y)))
    la_ref[...] = log_sig * (1.0 / GLA_GATE_TAU)


def _in_proj(x2, g1, w_cat, wup_p, ba_p):
    T = x2.shape[0]
    tm = TM_PROJ
    row = lambda w: pl.BlockSpec((tm, w), lambda i: (i, 0))
    full = lambda a: pl.BlockSpec(a.shape, lambda i: (0,) * a.ndim)
    outs = [(256, BF16), (256, BF16), (512, BF16), (512, BF16), (256, F32),
            (512, BF16), (128, BF16), (128, BF16), (128, BF16), (128, BF16)]
    return pl.pallas_call(
        _in_proj_kernel,
        grid=(T // tm,),
        in_specs=[row(D_MODEL), full(g1), full(w_cat), full(wup_p), full(ba_p)],
        out_specs=[row(w) for w, _ in outs],
        out_shape=[jax.ShapeDtypeStruct((T, w), dt) for w, dt in outs],
        compiler_params=pltpu.CompilerParams(
            dimension_semantics=("arbitrary",), vmem_limit_bytes=VMEM_LIMIT),
        name="in_proj",
    )(x2, g1, w_cat, wup_p, ba_p)


def _gla_kernel(q_ref, k_ref, v_ref, r_ref, la_ref, g_ref, cum_ref, o_ref, st_ref):
    @pl.when(pl.program_id(1) == 0)
    def _():
        st_ref[...] = jnp.zeros_like(st_ref)

    tl = TL_GLA
    c = GLA_CHUNK
    kw = GLA_HEADS * GLA_DK
    pieces = jnp.concatenate([p.astype(BF16) for p in _split3(la_ref[...])], axis=1)
    b3 = jnp.dot(cum_ref[...], pieces, preferred_element_type=F32)
    b_all = b3[:, :kw] + b3[:, kw:2 * kw] + b3[:, 2 * kw:]
    causal = (lax.broadcasted_iota(jnp.int32, (c, c), 0)
              >= lax.broadcasted_iota(jnp.int32, (c, c), 1))
    low_half = lax.broadcasted_iota(jnp.int32, (c, LANES), 1) < GLA_DK
    g = g_ref[...]
    for ch in range(tl // c):
        rows = slice(ch * c, (ch + 1) * c)
        b = b_all[rows]
        b_last = b[c - 1:c]
        qf = q_ref[rows, :].astype(F32)
        kf = k_ref[rows, :].astype(F32)
        q_e = (qf * jnp.exp(b) * (GLA_DK ** -0.5)).astype(BF16)
        k_e = (kf * jnp.exp(-b)).astype(BF16)
        k_t = (kf * jnp.exp(b_last - b)).astype(BF16)
        decay = jnp.exp(b_last)
        for h in range(GLA_HEADS):
            ps = slice((h // 2) * LANES, (h // 2 + 1) * LANES)
            mine = low_half if h % 2 == 0 else ~low_half
            qp, kp = q_e[:, ps], k_e[:, ps]
            qh = jnp.where(mine, qp, jnp.zeros_like(qp))
            kth = jnp.where(mine, k_t[:, ps], jnp.zeros_like(qp))
            vs = slice(h * GLA_DV, (h + 1) * GLA_DV)
            vh = v_ref[rows, vs]
            a = pl.dot(qh, kp, trans_b=True)
            a = jnp.where(causal, a, 0.0).astype(BF16)
            st = st_ref[h]
            o = (jnp.dot(a, vh, preferred_element_type=F32)
                 + pl.dot(qh, st.astype(BF16), trans_b=True))
            st_ref[h] = st * decay[:, ps] + pl.dot(vh, kth, trans_a=True)
            rh = r_ref[rows, vs].astype(F32)
            o = _rms(o, g) * (rh * jax.nn.sigmoid(rh))
            o_ref[rows, vs] = o.astype(BF16)


def _gla(qa, ka, va, ra, la, g, batch, seq):
    tl = TL_GLA
    nl = seq // tl
    row = lambda w: pl.BlockSpec((tl, w), lambda b, i: (b * nl + i, 0))
    r = np.arange(tl)
    cum = jnp.asarray((r[None, :] <= r[:, None])
                      & (r[None, :] // GLA_CHUNK == r[:, None] // GLA_CHUNK), BF16)
    return pl.pallas_call(
        _gla_kernel,
        grid=(batch, nl),
        in_specs=[row(256), row(256), row(512), row(512), row(256),
                  pl.BlockSpec((1, GLA_DV), lambda b, i: (0, 0)),
                  pl.BlockSpec((tl, tl), lambda b, i: (0, 0))],
        out_specs=row(512),
        out_shape=jax.ShapeDtypeStruct((batch * seq, 512), BF16),
        scratch_shapes=[pltpu.VMEM((GLA_HEADS, GLA_DV, LANES), F32)],
        compiler_params=pltpu.CompilerParams(
            dimension_semantics=("arbitrary", "arbitrary"), vmem_limit_bytes=VMEM_LIMIT),
        name="gla",
    )(qa, ka, va, ra, la, g, cum)


def _swa_kernel(sink_ref, bias_ref, q_ref, kc_ref, kp_ref, ksc_ref, ksp_ref,
                vc_ref, vp_ref, vsc_ref, vsp_ref, g_ref, o_ref):
    w = SWA_WINDOW
    cat = lambda p, cu: jnp.concatenate([p[...], cu[...]], axis=0)
    scale = jnp.asarray(SWA_HEAD_DIM ** -0.5, BF16)
    k = cat(kp_ref, kc_ref) * scale
    ks = cat(ksp_ref, ksc_ref) * scale
    v, vs = cat(vp_ref, vc_ref), cat(vsp_ref, vsc_ref)
    lane_lo = lax.broadcasted_iota(jnp.int32, (2 * w, LANES), 1) < SWA_HEAD_DIM
    zero = jnp.zeros_like(k)
    k_low = [jnp.where(lane_lo, k, zero), jnp.where(lane_lo, ks, zero)]
    k_high = [jnp.where(lane_lo, zero, ks), jnp.where(lane_lo, zero, k)]
    v_low = [jnp.where(lane_lo, v, zero), jnp.where(lane_lo, vs, zero)]
    v_high = [jnp.where(lane_lo, zero, vs), jnp.where(lane_lo, zero, v)]

    hi_half = lax.broadcasted_iota(jnp.int32, (LANES, LANES), 0) // SWA_HEAD_DIM
    hj_half = lax.broadcasted_iota(jnp.int32, (LANES, LANES), 1) // SWA_HEAD_DIM
    mean_mat = jnp.where(hi_half == hj_half, 1.0 / SWA_HEAD_DIM, 0.0).astype(F32)
    g = g_ref[...]

    def probs(s, head, rows):
        s = s + bias_ref[0, head, rows, :]
        sink = sink_ref[head]
        m = jnp.maximum(jnp.max(s, axis=-1, keepdims=True), sink)
        e = jnp.exp(s - m)
        den = jnp.sum(e, axis=-1, keepdims=True) + jnp.exp(sink - m)
        return (e * (1.0 / den)).astype(BF16)

    for pair in range(SWA_HEADS // 2):
        j = (2 * pair) // (SWA_HEADS // SWA_KV_HEADS)
        cols = slice(pair * LANES, (pair + 1) * LANES)
        for rb in range(w // SWA_ROWS):
            rows = slice(rb * SWA_ROWS, (rb + 1) * SWA_ROWS)
            qp = q_ref[rows, cols]
            p0 = probs(pl.dot(qp, k_low[j], trans_b=True), 2 * pair, rows)
            p1 = probs(pl.dot(qp, k_high[j], trans_b=True), 2 * pair + 1, rows)
            o = (jnp.dot(p0, v_low[j], preferred_element_type=F32)
                 + jnp.dot(p1, v_high[j], preferred_element_type=F32))
            ms = jnp.dot(o * o, mean_mat, preferred_element_type=F32, precision=HIGHEST)
            o = o * lax.rsqrt(ms + NORM_EPS) * g
            o_ref[rows, cols] = o.astype(BF16)


def _swa_bias():
    w = SWA_WINDOW
    slopes = 2.0 ** (-8.0 * np.arange(1, SWA_HEADS + 1, dtype=np.float64) / SWA_HEADS)
    rel = np.arange(w)[:, None] + w - np.arange(2 * w)[None, :]
    in_window = (rel >= 0) & (rel < w)
    exists = np.stack([np.arange(2 * w) >= w, np.ones(2 * w, bool)])
    valid = in_window[None] & exists[:, None, :]
    bias = -slopes[None, :, None, None] * rel[None, None].astype(np.float64)
    return np.where(valid[:, None], bias, NEG_BIG).astype(np.float32)


def _swa(qb, kb, kbs, vb, vbs, sinks, g2, batch, seq):
    w = SWA_WINDOW
    nb = seq // w
    cur = lambda width: pl.BlockSpec((w, width), lambda b, n: (b * nb + n, 0))
    prev = lambda width: pl.BlockSpec((w, width), lambda b, n: (b * nb + jnp.maximum(n - 1, 0), 0))
    return pl.pallas_call(
        _swa_kernel,
        grid=(batch, nb),
        in_specs=[pl.BlockSpec(memory_space=pltpu.SMEM),
                  pl.BlockSpec((1, SWA_HEADS, w, 2 * w), lambda b, n: (jnp.minimum(n, 1), 0, 0, 0)),
                  cur(512), cur(128), prev(128), cur(128), prev(128),
                  cur(128), prev(128), cur(128), prev(128),
                  pl.BlockSpec((1, LANES), lambda b, n: (0, 0))],
        out_specs=cur(512),
        out_shape=jax.ShapeDtypeStruct((batch * seq, 512), BF16),
        compiler_params=pltpu.CompilerParams(
            dimension_semantics=("arbitrary", "arbitrary"), vmem_limit_bytes=VMEM_LIMIT),
        name="swa",
    )(sinks, jnp.asarray(_swa_bias()), qb, kb, kb, kbs, kbs, vb, vb, vbs, vbs, g2)


def _out_route_kernel(oa_ref, ob_ref, x_ref, woa_ref, wob_ref, g_ref, wrh_ref, wrl_ref, br_ref,
                      earlier_ref, h1_ref, hn_ref, route_ref, cnt_ref):
    @pl.when(pl.program_id(0) == 0)
    def _():
        cnt_ref[...] = jnp.zeros_like(cnt_ref)

    tm = TM_ROUTE
    h1 = (x_ref[...]
          + jnp.dot(oa_ref[...], woa_ref[...], preferred_element_type=F32)
          + jnp.dot(ob_ref[...], wob_ref[...], preferred_element_type=F32))
    h1_ref[...] = h1
    hn = _rms(h1, g_ref[...])
    hn_ref[...] = hn
    hn_hi = hn.astype(BF16)
    hn_lo = (hn - hn_hi.astype(F32)).astype(BF16)
    logits = (pl.dot(wrh_ref[...], hn_hi, trans_b=True)
              + pl.dot(wrh_ref[...], hn_lo, trans_b=True)
              + pl.dot(wrl_ref[...], hn_hi, trans_b=True)) + br_ref[...]

    eid = lax.broadcasted_iota(jnp.int32, (N_EXPERTS, tm), 0)
    work = logits
    vals, idxs, sels = [], [], []
    for _ in range(TOP_K):
        m = jnp.max(work, axis=0, keepdims=True)
        idx = jnp.min(jnp.where(work == m, eid, N_EXPERTS), axis=0, keepdims=True)
        sel = eid == idx
        vals.append(m)
        idxs.append(idx)
        sels.append(sel)
        work = jnp.where(sel, -3e38, work)
    exps = [jnp.exp(v - vals[0]) for v in vals]
    inv_den = 1.0 / (exps[0] + exps[1] + exps[2] + exps[3])

    multihot = jnp.where(sels[0] | sels[1] | sels[2] | sels[3], 1.0, 0.0)
    before = (jnp.dot(multihot.astype(BF16), earlier_ref[...], preferred_element_type=F32)
              + cnt_ref[:, 0:1])
    ranks = [jnp.sum(jnp.where(s, before, 0.0), axis=0, keepdims=True) for s in sels]
    route_ref[...] = jnp.concatenate(
        [i.astype(F32) for i in idxs] + [e * inv_den for e in exps] + ranks
        + [jnp.zeros((ROUTE_ROWS - 3 * TOP_K, tm), F32)], axis=0)
    cnt_ref[...] += jnp.sum(multihot, axis=1, keepdims=True)


def _out_route(oa, ob, x2, woa, wob, g2, wr_hi, wr_lo, br):
    T = x2.shape[0]
    tm = TM_ROUTE
    row = lambda w: pl.BlockSpec((tm, w), lambda i: (i, 0))
    full = lambda a: pl.BlockSpec(a.shape, lambda i: (0,) * a.ndim)
    earlier = jnp.asarray(np.triu(np.ones((tm, tm), np.float32), k=1), BF16)
    return pl.pallas_call(
        _out_route_kernel,
        grid=(T // tm,),
        in_specs=[row(512), row(512), row(D_MODEL), full(woa), full(wob), full(g2),
                  full(wr_hi), full(wr_lo), full(br), full(earlier)],
        out_specs=[row(D_MODEL), row(D_MODEL),
                   pl.BlockSpec((ROUTE_ROWS, tm), lambda i: (0, i)),
                   pl.BlockSpec((N_EXPERTS, LANES), lambda i: (0, 0))],
        out_shape=[jax.ShapeDtypeStruct((T, D_MODEL), F32),
                   jax.ShapeDtypeStruct((T, D_MODEL), F32),
                   jax.ShapeDtypeStruct((ROUTE_ROWS, T), F32),
                   jax.ShapeDtypeStruct((N_EXPERTS, LANES), F32)],
        compiler_params=pltpu.CompilerParams(
            dimension_semantics=("arbitrary",), vmem_limit_bytes=VMEM_LIMIT),
        name="out_route",
    )(oa, ob, x2, woa, wob, g2, wr_hi, wr_lo, br, earlier)


def _row_copy(src, s, dst, d, sem):
    return pltpu.make_async_copy(src.at[pl.ds(s, 1)], dst.at[pl.ds(d, 1)], sem)


def _dispatch_kernel(dest_ref, pad_start_ref, pad_cnt_ref, nused_ref, hn_ref, xs_ref,
                     zero_ref, sem, zsem):
    i = pl.program_id(0)
    n = pl.num_programs(0)
    tm = TM_EXPERT
    base = i * (TM_DISPATCH * TOP_K)
    for p in range(TM_DISPATCH * TOP_K):
        _row_copy(hn_ref, p // TOP_K, xs_ref, dest_ref[base + p], sem).start()
    for _ in range(TOP_K):
        pltpu.make_async_copy(hn_ref, xs_ref.at[pl.ds(0, TM_DISPATCH)], sem).wait()

    @pl.when(i == n - 1)
    def _():
        zero_ref[...] = jnp.zeros_like(zero_ref)
        for e in range(N_EXPERTS):
            start, cnt = pad_start_ref[e], pad_cnt_ref[e]

            def zissue(r, carry):
                _row_copy(zero_ref, 0, xs_ref, start + r, zsem).start()
                return carry

            def zwait(r, carry):
                _row_copy(zero_ref, 0, xs_ref, start + r, zsem).wait()
                return carry

            lax.fori_loop(0, cnt, zissue, 0)
            lax.fori_loop(0, cnt, zwait, 0)

        def tail_copy(t):
            return pltpu.make_async_copy(zero_ref, xs_ref.at[pl.ds(t * tm, tm)], zsem)

        def tissue(t, carry):
            tail_copy(t).start()
            return carry

        def twait(t, carry):
            tail_copy(t).wait()
            return carry

        n_tiles = xs_ref.shape[0] // tm
        lax.fori_loop(nused_ref[0], n_tiles, tissue, 0)
        lax.fori_loop(nused_ref[0], n_tiles, twait, 0)


def _dispatch(dest, pad_start, pad_cnt, n_used, hn, n_rows):
    T = hn.shape[0]
    any_spec = pl.BlockSpec(memory_space=pl.ANY)
    return pl.pallas_call(
        _dispatch_kernel,
        grid_spec=pltpu.PrefetchScalarGridSpec(
            num_scalar_prefetch=4,
            grid=(T // TM_DISPATCH,),
            in_specs=[pl.BlockSpec((TM_DISPATCH, D_MODEL), lambda i, *_: (i, 0))],
            out_specs=any_spec,
            scratch_shapes=[pltpu.VMEM((TM_EXPERT, D_MODEL), F32),
                            pltpu.SemaphoreType.DMA(()),
                            pltpu.SemaphoreType.DMA(())]),
        out_shape=jax.ShapeDtypeStruct((n_rows, D_MODEL), F32),
        compiler_params=pltpu.CompilerParams(
            dimension_semantics=("arbitrary",), has_side_effects=True,
            vmem_limit_bytes=VMEM_LIMIT),
        name="dispatch",
    )(dest, pad_start, pad_cnt, n_used, hn)


def _experts_kernel(tile0_ref, ntile_ref, xs_ref, wg_ref, bg_ref, wl_ref, bl_ref, wd_ref, bd_ref,
                    ys_ref, wg_bf, wl_bf, wd_bf, xbuf, ybuf, xsem, ysem):
    e = pl.program_id(0)
    tm = TM_EXPERT
    t0, nt = tile0_ref[e], ntile_ref[e]

    def x_load(tile, slot):
        return pltpu.make_async_copy(xs_ref.at[pl.ds(tile * tm, tm)], xbuf.at[slot], xsem.at[slot])

    def y_store(tile, slot):
        return pltpu.make_async_copy(ybuf.at[slot], ys_ref.at[pl.ds(tile * tm, tm)], ysem.at[slot])

    @pl.when((e == 0) & (nt > 0))
    def _():
        x_load(t0, 0).start()

    @pl.when(nt > 0)
    def _():
        wg_bf[...] = wg_ref[0].astype(BF16)
        wl_bf[...] = wl_ref[0].astype(BF16)
        wd_bf[...] = wd_ref[0].astype(BF16)

    def tile_body(j, carry):
        slot = j % 2

        @pl.when(j + 1 < nt)
        def _():
            x_load(t0 + j + 1, 1 - slot).start()

        x_load(t0 + j, slot).wait()

        @pl.when(j >= 2)
        def _():
            y_store(t0 + j - 2, slot).wait()

        x = xbuf[slot].astype(BF16)
        glu = jnp.minimum(jnp.dot(x, wg_bf[...], preferred_element_type=F32) + bg_ref[0],
                          SWIGLU_LIMIT)
        lin = jnp.clip(jnp.dot(x, wl_bf[...], preferred_element_type=F32) + bl_ref[0],
                       -SWIGLU_LIMIT, SWIGLU_LIMIT)
        hid = glu * jax.nn.sigmoid(SWIGLU_ALPHA * glu) * (lin + 1.0)
        ybuf[slot] = jnp.dot(hid.astype(BF16), wd_bf[...], preferred_element_type=F32) + bd_ref[0]
        y_store(t0 + j, slot).start()
        return carry

    lax.fori_loop(0, nt, tile_body, 0)

    @pl.when(e + 1 < pl.num_programs(0))
    def _():
        @pl.when(ntile_ref[e + 1] > 0)
        def _():
            x_load(tile0_ref[e + 1], 0).start()

    @pl.when(nt >= 2)
    def _():
        y_store(t0 + nt - 2, nt % 2).wait()

    @pl.when(nt >= 1)
    def _():
        y_store(t0 + nt - 1, (nt - 1) % 2).wait()

    @pl.when(e == pl.num_programs(0) - 1)
    def _():
        ybuf[0] = jnp.zeros((tm, D_MODEL), F32)
        n_tiles = ys_ref.shape[0] // tm

        def tail_issue(t, carry):
            y_store(t, 0).start()
            return carry

        def tail_wait(t, carry):
            y_store(t, 0).wait()
            return carry

        lax.fori_loop(t0 + nt, n_tiles, tail_issue, 0)
        lax.fori_loop(t0 + nt, n_tiles, tail_wait, 0)


def _experts(tile0, ntile, xs, w_glu, b_glu, w_lin, b_lin, w_down, b_down):
    tm = TM_EXPERT
    any_spec = pl.BlockSpec(memory_space=pl.ANY)
    wspec = lambda: pl.BlockSpec((1, D_MODEL, D_FF), lambda e, *_: (e, 0, 0))
    bspec = lambda: pl.BlockSpec((1, 1, D_FF), lambda e, *_: (e, 0, 0))
    return pl.pallas_call(
        _experts_kernel,
        grid_spec=pltpu.PrefetchScalarGridSpec(
            num_scalar_prefetch=2,
            grid=(N_EXPERTS,),
            in_specs=[any_spec, wspec(), bspec(), wspec(), bspec(), wspec(), bspec()],
            out_specs=any_spec,
            scratch_shapes=[pltpu.VMEM((D_MODEL, D_FF), BF16)] * 3
            + [pltpu.VMEM((2, tm, D_MODEL), F32), pltpu.VMEM((2, tm, D_MODEL), F32),
               pltpu.SemaphoreType.DMA((2,)), pltpu.SemaphoreType.DMA((2,))]),
        out_shape=jax.ShapeDtypeStruct(xs.shape, F32),
        compiler_params=pltpu.CompilerParams(
            dimension_semantics=("arbitrary",), vmem_limit_bytes=VMEM_LIMIT,
            has_side_effects=True),
        name="experts",
    )(tile0, ntile, xs, w_glu, b_glu.reshape(N_EXPERTS, 1, D_FF),
      w_lin, b_lin.reshape(N_EXPERTS, 1, D_FF), w_down, b_down.reshape(N_EXPERTS, 1, D_MODEL))


def _combine_kernel(dest_ref, ys_ref, h1_ref, gate_ref, g_ref, o_ref, buf, sems):
    s = pl.program_id(0)
    n = pl.num_programs(0) - 1
    tm = TM_COMBINE

    def issue(slot):
        base = s * (tm * TOP_K)
        for p in range(tm * TOP_K):
            r, k = divmod(p, TOP_K)
            pltpu.make_async_copy(ys_ref.at[pl.ds(dest_ref[base + p], 1)],
                                  buf.at[slot, k, pl.ds(r, 1)], sems.at[slot]).start()

    for parity in range(2):
        @pl.when((s < n) & (s % 2 == parity))
        def _():
            issue(parity)

    @pl.when(s > 0)
    def _():
        slot = (s - 1) % 2
        for k in range(TOP_K):
            pltpu.make_async_copy(ys_ref.at[pl.ds(0, tm)], buf.at[slot, k], sems.at[slot]).wait()
        acc = h1_ref[...]
        for k in range(TOP_K):
            acc = acc + gate_ref[:, k:k + 1] * buf[slot, k]
        o_ref[...] = _rms(acc, g_ref[...])


def _combine(dest, ys, h1, gate, fg):
    T = h1.shape[0]
    tm = TM_COMBINE
    return pl.pallas_call(
        _combine_kernel,
        grid_spec=pltpu.PrefetchScalarGridSpec(
            num_scalar_prefetch=1,
            grid=(T // tm + 1,),
            in_specs=[pl.BlockSpec(memory_space=pl.ANY),
                      pl.BlockSpec((tm, D_MODEL), lambda s, d: (jnp.maximum(s - 1, 0), 0)),
                      pl.BlockSpec((tm, TOP_K), lambda s, d: (jnp.maximum(s - 1, 0), 0)),
                      pl.BlockSpec((1, D_MODEL), lambda s, d: (0, 0))],
            out_specs=pl.BlockSpec((tm, D_MODEL), lambda s, d: (jnp.maximum(s - 1, 0), 0)),
            scratch_shapes=[pltpu.VMEM((2, TOP_K, tm, D_MODEL), F32),
                            pltpu.SemaphoreType.DMA((2,))]),
        out_shape=jax.ShapeDtypeStruct((T, D_MODEL), F32),
        compiler_params=pltpu.CompilerParams(
            dimension_semantics=("arbitrary",), vmem_limit_bytes=VMEM_LIMIT),
        name="combine",
    )(dest, ys, h1, gate, fg)


def _swap_halves(w):
    h = w.shape[-1] // 2
    return jnp.concatenate([w[..., h:], w[..., :h]], axis=-1)


def _layer(x2, batch, seq, norm1_g, w_in, w_alpha_up, b_alpha, gla_norm_g, swa_sinks, swa_norm_g,
           w_out, norm2_g, w_router, b_router, w_glu, b_glu, w_lin, b_lin, w_down, b_down):
    T = x2.shape[0]
    kb_w, vb_w = w_in[:, 2064:2192], w_in[:, 2192:2320]
    w_z = w_in[:, 1536:1552]
    w_cat = jnp.concatenate([
        w_in[:, 0:1536], w_in[:, 1552:2064],
        kb_w, _swap_halves(kb_w), vb_w, _swap_halves(vb_w),
        jnp.pad(jnp.tile(w_z, (1, Z_PIECES)), [(0, 0), (0, LANES - Z_PIECES * GLA_RANK)]),
    ], axis=1).astype(BF16)
    wup_hi = w_alpha_up.astype(BF16)
    wup_lo = (w_alpha_up - wup_hi.astype(F32)).astype(BF16)
    wup_cat = jnp.pad(jnp.concatenate([wup_hi, wup_hi, wup_hi, wup_lo, wup_lo], axis=0),
                      [(0, LANES - Z_PIECES * GLA_RANK), (0, 0)])

    qa, ka, va, ra, la, qb, kb, kbs, vb, vbs = _in_proj(
        x2, norm1_g.reshape(1, -1), w_cat, wup_cat, b_alpha.reshape(1, -1))
    oa = _gla(qa, ka, va, ra, la, gla_norm_g.reshape(1, -1), batch, seq)
    ob = _swa(qb, kb, kbs, vb, vbs, swa_sinks, jnp.tile(swa_norm_g, 2).reshape(1, -1), batch, seq)

    wo = w_out.astype(BF16)
    wr_t = w_router.T
    wr_hi = wr_t.astype(BF16)
    wr_lo = (wr_t - wr_hi.astype(F32)).astype(BF16)
    h1, hn, route, cnt = _out_route(oa, ob, x2, wo[:512], wo[512:], norm2_g.reshape(1, -1),
                                    wr_hi, wr_lo, b_router.reshape(-1, 1))

    tm = TM_EXPERT
    n_tiles = T * TOP_K // tm + N_EXPERTS
    counts = cnt[:, 0].astype(jnp.int32)
    padded = (counts + tm - 1) // tm * tm
    pends = jnp.cumsum(padded)
    pstarts = pends - padded
    top_idx = route[0:TOP_K].T.astype(jnp.int32)
    gate = route[TOP_K:2 * TOP_K].T
    rank = route[2 * TOP_K:3 * TOP_K].T.astype(jnp.int32)
    seg_start = jnp.sum(jnp.where(top_idx[..., None] == jnp.arange(N_EXPERTS), pstarts, 0), axis=-1)
    dest = (seg_start + rank).reshape(-1)
    n_used = (pends[-1] // tm).reshape(1)
    xs = _dispatch(dest, pstarts + counts, padded - counts, n_used, hn, n_tiles * tm)
    ys = _experts(pstarts // tm, padded // tm, xs, w_glu, b_glu, w_lin, b_lin, w_down, b_down)
    return dest, ys, h1, gate


def kernel(x, norm1_g, w_in, w_alpha_up, b_alpha, gla_norm_g, swa_sinks, swa_norm_g, w_out,
           norm2_g, w_router, b_router, w_glu, b_glu, w_lin, b_lin, w_down, b_down, final_g):
    batch, seq, d = x.shape
    assert norm1_g.shape[0] == 1, "single-layer problem"
    x2 = x.reshape(batch * seq, d)
    dest, ys, h1, gate = _layer(
        x2, batch, seq, norm1_g[0], w_in[0], w_alpha_up[0], b_alpha[0], gla_norm_g[0],
        swa_sinks[0], swa_norm_g[0], w_out[0], norm2_g[0], w_router[0], b_router[0],
        w_glu[0], b_glu[0], w_lin[0], b_lin[0], w_down[0], b_down[0])
    out = _combine(dest, ys, h1, gate, final_g.reshape(1, -1))
    return out.reshape(batch, seq, d)
```

```python
import numpy as np
import jax
import jax.numpy as jnp
from jax import lax
from jax.experimental import pallas as pl
from jax.experimental.pallas import tpu as pltpu

F32 = jnp.float32
BF16 = jnp.bfloat16
HIGHEST = lax.Precision.HIGHEST

D_MODEL = 1024
GLA_HEADS = 4
GLA_DK = 64
GLA_DV = 128
GLA_RANK = 16
GLA_GATE_TAU = 16.0
GLA_CHUNK = 64
SWA_HEADS = 8
SWA_KV_HEADS = 2
SWA_HEAD_DIM = 64
SWA_WINDOW = 128
N_EXPERTS = 32
TOP_K = 4
D_FF = 1024
SWIGLU_LIMIT = 7.0
SWIGLU_ALPHA = 1.702
NORM_EPS = 1e-5

LANES = 128
VMEM_LIMIT = 56 * 1024 * 1024

TM_PROJ = 512
TL_GLA = 256
TM_ROUTE = 512
TM_EXPERT = 512
TM_COMBINE = 256
SWA_ROWS = 128
TM_DISPATCH = 256

NEG_BIG = -1e30
ROUTE_ROWS = 16

C_QA, C_KA, C_VA, C_RA, C_QB, C_KB, C_KBS, C_VB, C_VBS, C_Z, C_END = (
    0, 256, 512, 1024, 1536, 2048, 2176, 2304, 2432, 2560, 2688)
Z_PIECES = 5


def _split3(x):
    hi = x.astype(BF16).astype(F32)
    r = x - hi
    mid = r.astype(BF16).astype(F32)
    lo = (r - mid).astype(BF16).astype(F32)
    return hi, mid, lo


def _rms(x, g):
    return x * lax.rsqrt(jnp.mean(x * x, axis=-1, keepdims=True) + NORM_EPS) * g


def _in_proj_kernel(x_ref, g_ref, w_ref, wup_ref, ba_ref,
                    qa_ref, ka_ref, va_ref, ra_ref, la_ref, qb_ref, kb_ref, kbs_ref, vb_ref, vbs_ref):
    u = _rms(x_ref[...], g_ref[...]).astype(BF16)

    def proj(c0, c1):
        return jnp.dot(u, w_ref[:, c0:c1], preferred_element_type=F32)

    qa_ref[...] = proj(C_QA, C_KA).astype(BF16)
    ka_ref[...] = proj(C_KA, C_VA).astype(BF16)
    va_ref[...] = proj(C_VA, C_RA).astype(BF16)
    ra_ref[...] = proj(C_RA, C_QB).astype(BF16)
    qb_ref[...] = proj(C_QB, C_KB).astype(BF16)
    kb_ref[...] = proj(C_KB, C_KBS).astype(BF16)
    kbs_ref[...] = proj(C_KBS, C_VB).astype(BF16)
    vb_ref[...] = proj(C_VB, C_VBS).astype(BF16)
    vbs_ref[...] = proj(C_VBS, C_Z).astype(BF16)
    z = proj(C_Z, C_END)
    hi, mid, lo = _split3(z)
    piece = lax.broadcasted_iota(jnp.int32, z.shape, 1) // GLA_RANK
    zc = jnp.where((piece == 0) | (piece == 3), hi, jnp.where(piece == 2, lo, mid)).astype(BF16)
    y = jnp.dot(zc, wup_ref[...], preferred_element_type=F32) + ba_ref[...]
    log_sig = jnp.minimum(y, 0.0) - jnp.log1p(jnp.exp(-jnp.abs(y)))
    la_ref[...] = log_sig * (1.0 / GLA_GATE_TAU)


def _in_proj(x2, g1, w_cat, wup_p, ba_p):
    T = x2.shape[0]
    tm = TM_PROJ
    row = lambda w: pl.BlockSpec((tm, w), lambda i: (i, 0))
    full = lambda a: pl.BlockSpec(a.shape, lambda i: (0,) * a.ndim)
    outs = [(256, BF16), (256, BF16), (512, BF16), (512, BF16), (256, F32),
            (512, BF16), (128, BF16), (128, BF16), (128, BF16), (128, BF16)]
    return pl.pallas_call(
        _in_proj_kernel,
        grid=(T // tm,),
        in_specs=[row(D_MODEL), full(g1), full(w_cat), full(wup_p), full(ba_p)],
        out_specs=[row(w) for w, _ in outs],
        out_shape=[jax.ShapeDtypeStruct((T, w), dt) for w, dt in outs],
        compiler_params=pltpu.CompilerParams(
            dimension_semantics=("arbitrary",), vmem_limit_bytes=VMEM_LIMIT),
        name="in_proj",
    )(x2, g1, w_cat, wup_p, ba_p)


def _gla_kernel(q_ref, k_ref, v_ref, r_ref, la_ref, g_ref, cum_ref, o_ref, st_ref):
    @pl.when(pl.program_id(1) == 0)
    def _():
        st_ref[...] = jnp.zeros_like(st_ref)

    tl = TL_GLA
    c = GLA_CHUNK
    kw = GLA_HEADS * GLA_DK
    pieces = jnp.concatenate([p.astype(BF16) for p in _split3(la_ref[...])], axis=1)
    b3 = jnp.dot(cum_ref[...], pieces, preferred_element_type=F32)
    b_all = b3[:, :kw] + b3[:, kw:2 * kw] + b3[:, 2 * kw:]
    causal = (lax.broadcasted_iota(jnp.int32, (c, c), 0)
              >= lax.broadcasted_iota(jnp.int32, (c, c), 1))
    low_half = lax.broadcasted_iota(jnp.int32, (c, LANES), 1) < GLA_DK
    g = g_ref[...]
    for ch in range(tl // c):
        rows = slice(ch * c, (ch + 1) * c)
        b = b_all[rows]
        b_last = b[c - 1:c]
        qf = q_ref[rows, :].astype(F32)
        kf = k_ref[rows, :].astype(F32)
        q_e = (qf * jnp.exp(b) * (GLA_DK ** -0.5)).astype(BF16)
        k_e = (kf * jnp.exp(-b)).astype(BF16)
        k_t = (kf * jnp.exp(b_last - b)).astype(BF16)
        decay = jnp.exp(b_last)
        for h in range(GLA_HEADS):
            ps = slice((h // 2) * LANES, (h // 2 + 1) * LANES)
            mine = low_half if h % 2 == 0 else ~low_half
            qp, kp = q_e[:, ps], k_e[:, ps]
            qh = jnp.where(mine, qp, jnp.zeros_like(qp))
            kth = jnp.where(mine, k_t[:, ps], jnp.zeros_like(qp))
            vs = slice(h * GLA_DV, (h + 1) * GLA_DV)
            vh = v_ref[rows, vs]
            a = pl.dot(qh, kp, trans_b=True)
            a = jnp.where(causal, a, 0.0).astype(BF16)
            st = st_ref[h]
            o = (jnp.dot(a, vh, preferred_element_type=F32)
                 + pl.dot(qh, st.astype(BF16), trans_b=True))
            st_ref[h] = st * decay[:, ps] + pl.dot(vh, kth, trans_a=True)
            rh = r_ref[rows, vs].astype(F32)
            o = _rms(o, g) * (rh * jax.nn.sigmoid(rh))
            o_ref[rows, vs] = o.astype(BF16)


def _gla(qa, ka, va, ra, la, g, batch, seq):
    tl = TL_GLA
    nl = seq // tl
    row = lambda w: pl.BlockSpec((tl, w), lambda b, i: (b * nl + i, 0))
    r = np.arange(tl)
    cum = jnp.asarray((r[None, :] <= r[:, None])
                      & (r[None, :] // GLA_CHUNK == r[:, None] // GLA_CHUNK), BF16)
    return pl.pallas_call(
        _gla_kernel,
        grid=(batch, nl),
        in_specs=[row(256), row(256), row(512), row(512), row(256),
                  pl.BlockSpec((1, GLA_DV), lambda b, i: (0, 0)),
                  pl.BlockSpec((tl, tl), lambda b, i: (0, 0))],
        out_specs=row(512),
        out_shape=jax.ShapeDtypeStruct((batch * seq, 512), BF16),
        scratch_shapes=[pltpu.VMEM((GLA_HEADS, GLA_DV, LANES), F32)],
        compiler_params=pltpu.CompilerParams(
            dimension_semantics=("arbitrary", "arbitrary"), vmem_limit_bytes=VMEM_LIMIT),
        name="gla",
    )(qa, ka, va, ra, la, g, cum)


def _swa_kernel(sink_ref, bias_ref, q_ref, kc_ref, kp_ref, ksc_ref, ksp_ref,
                vc_ref, vp_ref, vsc_ref, vsp_ref, g_ref, o_ref):
    w = SWA_WINDOW
    cat = lambda p, cu: jnp.concatenate([p[...], cu[...]], axis=0)
    scale = jnp.asarray(SWA_HEAD_DIM ** -0.5, BF16)
    k = cat(kp_ref, kc_ref) * scale
    ks = cat(ksp_ref, ksc_ref) * scale
    v, vs = cat(vp_ref, vc_ref), cat(vsp_ref, vsc_ref)
    lane_lo = lax.broadcasted_iota(jnp.int32, (2 * w, LANES), 1) < SWA_HEAD_DIM
    zero = jnp.zeros_like(k)
    k_low = [jnp.where(lane_lo, k, zero), jnp.where(lane_lo, ks, zero)]
    k_high = [jnp.where(lane_lo, zero, ks), jnp.where(lane_lo, zero, k)]
    v_low = [jnp.where(lane_lo, v, zero), jnp.where(lane_lo, vs, zero)]
    v_high = [jnp.where(lane_lo, zero, vs), jnp.where(lane_lo, zero, v)]

    hi_half = lax.broadcasted_iota(jnp.int32, (LANES, LANES), 0) // SWA_HEAD_DIM
    hj_half = lax.broadcasted_iota(jnp.int32, (LANES, LANES), 1) // SWA_HEAD_DIM
    mean_mat = jnp.where(hi_half == hj_half, 1.0 / SWA_HEAD_DIM, 0.0).astype(F32)
    g = g_ref[...]

    def probs(s, head, rows):
        s = s + bias_ref[0, head, rows, :]
        sink = sink_ref[head]
        m = jnp.maximum(jnp.max(s, axis=-1, keepdims=True), sink)
        e = jnp.exp(s - m)
        den = jnp.sum(e, axis=-1, keepdims=True) + jnp.exp(sink - m)
        return (e * (1.0 / den)).astype(BF16)

    for pair in range(SWA_HEADS // 2):
        j = (2 * pair) // (SWA_HEADS // SWA_KV_HEADS)
        cols = slice(pair * LANES, (pair + 1) * LANES)
        for rb in range(w // SWA_ROWS):
            rows = slice(rb * SWA_ROWS, (rb + 1) * SWA_ROWS)
            qp = q_ref[rows, cols]
            p0 = probs(pl.dot(qp, k_low[j], trans_b=True), 2 * pair, rows)
            p1 = probs(pl.dot(qp, k_high[j], trans_b=True), 2 * pair + 1, rows)
            o = (jnp.dot(p0, v_low[j], preferred_element_type=F32)
                 + jnp.dot(p1, v_high[j], preferred_element_type=F32))
            ms = jnp.dot(o * o, mean_mat, preferred_element_type=F32, precision=HIGHEST)
            o = o * lax.rsqrt(ms + NORM_EPS) * g
            o_ref[rows, cols] = o.astype(BF16)


def _swa_bias():
    w = SWA_WINDOW
    slopes = 2.0 ** (-8.0 * np.arange(1, SWA_HEADS + 1, dtype=np.float64) / SWA_HEADS)
    rel = np.arange(w)[:, None] + w - np.arange(2 * w)[None, :]
    in_window = (rel >= 0) & (rel < w)
    exists = np.stack([np.arange(2 * w) >= w, np.ones(2 * w, bool)])
    valid = in_window[None] & exists[:, None, :]
    bias = -slopes[None, :, None, None] * rel[None, None].astype(np.float64)
    return np.where(valid[:, None], bias, NEG_BIG).astype(np.float32)


def _swa(qb, kb, kbs, vb, vbs, sinks, g2, batch, seq):
    w = SWA_WINDOW
    nb = seq // w
    cur = lambda width: pl.BlockSpec((w, width), lambda b, n: (b * nb + n, 0))
    prev = lambda width: pl.BlockSpec((w, width), lambda b, n: (b * nb + jnp.maximum(n - 1, 0), 0))
    return pl.pallas_call(
        _swa_kernel,
        grid=(batch, nb),
        in_specs=[pl.BlockSpec(memory_space=pltpu.SMEM),
                  pl.BlockSpec((1, SWA_HEADS, w, 2 * w), lambda b, n: (jnp.minimum(n, 1), 0, 0, 0)),
                  cur(512), cur(128), prev(128), cur(128), prev(128),
                  cur(128), prev(128), cur(128), prev(128),
                  pl.BlockSpec((1, LANES), lambda b, n: (0, 0))],
        out_specs=cur(512),
        out_shape=jax.ShapeDtypeStruct((batch * seq, 512), BF16),
        compiler_params=pltpu.CompilerParams(
            dimension_semantics=("arbitrary", "arbitrary"), vmem_limit_bytes=VMEM_LIMIT),
        name="swa",
    )(sinks, jnp.asarray(_swa_bias()), qb, kb, kb, kbs, kbs, vb, vb, vbs, vbs, g2)


def _out_route_kernel(oa_ref, ob_ref, x_ref, woa_ref, wob_ref, g_ref, wrh_ref, wrl_ref, br_ref,
                      earlier_ref, h1_ref, hn_ref, route_ref, cnt_ref):
    @pl.when(pl.program_id(0) == 0)
    def _():
        cnt_ref[...] = jnp.zeros_like(cnt_ref)

    tm = TM_ROUTE
    h1 = (x_ref[...]
          + jnp.dot(oa_ref[...], woa_ref[...], preferred_element_type=F32)
          + jnp.dot(ob_ref[...], wob_ref[...], preferred_element_type=F32))
    h1_ref[...] = h1
    hn = _rms(h1, g_ref[...])
    hn_ref[...] = hn
    hn_hi = hn.astype(BF16)
    hn_lo = (hn - hn_hi.astype(F32)).astype(BF16)
    logits = (pl.dot(wrh_ref[...], hn_hi, trans_b=True)
              + pl.dot(wrh_ref[...], hn_lo, trans_b=True)
              + pl.dot(wrl_ref[...], hn_hi, trans_b=True)) + br_ref[...]

    eid = lax.broadcasted_iota(jnp.int32, (N_EXPERTS, tm), 0)
    work = logits
    vals, idxs, sels = [], [], []
    for _ in range(TOP_K):
        m = jnp.max(work, axis=0, keepdims=True)
        idx = jnp.min(jnp.where(work == m, eid, N_EXPERTS), axis=0, keepdims=True)
        sel = eid == idx
        vals.append(m)
        idxs.append(idx)
        sels.append(sel)
        work = jnp.where(sel, -3e38, work)
    exps = [jnp.exp(v - vals[0]) for v in vals]
    inv_den = 1.0 / (exps[0] + exps[1] + exps[2] + exps[3])

    multihot = jnp.where(sels[0] | sels[1] | sels[2] | sels[3], 1.0, 0.0)
    before = (jnp.dot(multihot.astype(BF16), earlier_ref[...], preferred_element_type=F32)
              + cnt_ref[:, 0:1])
    ranks = [jnp.sum(jnp.where(s, before, 0.0), axis=0, keepdims=True) for s in sels]
    route_ref[...] = jnp.concatenate(
        [i.astype(F32) for i in idxs] + [e * inv_den for e in exps] + ranks
        + [jnp.zeros((ROUTE_ROWS - 3 * TOP_K, tm), F32)], axis=0)
    cnt_ref[...] += jnp.sum(multihot, axis=1, keepdims=True)


def _out_route(oa, ob, x2, woa, wob, g2, wr_hi, wr_lo, br):
    T = x2.shape[0]
    tm = TM_ROUTE
    row = lambda w: pl.BlockSpec((tm, w), lambda i: (i, 0))
    full = lambda a: pl.BlockSpec(a.shape, lambda i: (0,) * a.ndim)
    earlier = jnp.asarray(np.triu(np.ones((tm, tm), np.float32), k=1), BF16)
    return pl.pallas_call(
        _out_route_kernel,
        grid=(T // tm,),
        in_specs=[row(512), row(512), row(D_MODEL), full(woa), full(wob), full(g2),
                  full(wr_hi), full(wr_lo), full(br), full(earlier)],
        out_specs=[row(D_MODEL), row(D_MODEL),
                   pl.BlockSpec((ROUTE_ROWS, tm), lambda i: (0, i)),
                   pl.BlockSpec((N_EXPERTS, LANES), lambda i: (0, 0))],
        out_shape=[jax.ShapeDtypeStruct((T, D_MODEL), F32),
                   jax.ShapeDtypeStruct((T, D_MODEL), F32),
                   jax.ShapeDtypeStruct((ROUTE_ROWS, T), F32),
                   jax.ShapeDtypeStruct((N_EXPERTS, LANES), F32)],
        compiler_params=pltpu.CompilerParams(
            dimension_semantics=("arbitrary",), vmem_limit_bytes=VMEM_LIMIT),
        name="out_route",
    )(oa, ob, x2, woa, wob, g2, wr_hi, wr_lo, br, earlier)


def _row_copy(src, s, dst, d, sem):
    return pltpu.make_async_copy(src.at[pl.ds(s, 1)], dst.at[pl.ds(d, 1)], sem)


def _dispatch_kernel(dest_ref, pad_start_ref, pad_cnt_ref, nused_ref, hn_ref, xs_ref,
                     zero_ref, sem, zsem):
    i = pl.program_id(0)
    n = pl.num_programs(0)
    tm = TM_EXPERT
    base = i * (TM_DISPATCH * TOP_K)
    for p in range(TM_DISPATCH * TOP_K):
        _row_copy(hn_ref, p // TOP_K, xs_ref, dest_ref[base + p], sem).start(priority=p % 2)
    for _ in range(TOP_K):
        pltpu.make_async_copy(hn_ref, xs_ref.at[pl.ds(0, TM_DISPATCH)], sem).wait()

    @pl.when(i == n - 1)
    def _():
        zero_ref[...] = jnp.zeros_like(zero_ref)
        sub = 8
        for e in range(N_EXPERTS):
            start, cnt = pad_start_ref[e], pad_cnt_ref[e]
            head = jnp.minimum((-start) & (sub - 1), cnt)
            body0 = start + head
            nbody = (cnt - head) // sub

            def head_copy(r):
                return _row_copy(zero_ref, 0, xs_ref, start + r, zsem)

            def body_copy(c):
                at = pl.multiple_of(body0 + c * sub, sub)
                return pltpu.make_async_copy(zero_ref.at[pl.ds(0, sub)],
                                             xs_ref.at[pl.ds(at, sub)], zsem)

            for copy, count in ((head_copy, head), (body_copy, nbody)):
                def zissue(r, carry, copy=copy):
                    copy(r).start()
                    return carry

                def zwait(r, carry, copy=copy):
                    copy(r).wait()
                    return carry

                lax.fori_loop(0, count, zissue, 0)
                lax.fori_loop(0, count, zwait, 0)

        def tail_copy(t):
            return pltpu.make_async_copy(zero_ref, xs_ref.at[pl.ds(t * tm, tm)], zsem)

        def tissue(t, carry):
            tail_copy(t).start()
            return carry

        def twait(t, carry):
            tail_copy(t).wait()
            return carry

        n_tiles = xs_ref.shape[0] // tm
        lax.fori_loop(nused_ref[0], n_tiles, tissue, 0)
        lax.fori_loop(nused_ref[0], n_tiles, twait, 0)


def _dispatch(dest, pad_start, pad_cnt, n_used, hn, n_rows):
    T = hn.shape[0]
    any_spec = pl.BlockSpec(memory_space=pl.ANY)
    return pl.pallas_call(
        _dispatch_kernel,
        grid_spec=pltpu.PrefetchScalarGridSpec(
            num_scalar_prefetch=4,
            grid=(T // TM_DISPATCH,),
            in_specs=[pl.BlockSpec((TM_DISPATCH, D_MODEL), lambda i, *_: (i, 0))],
            out_specs=any_spec,
            scratch_shapes=[pltpu.VMEM((TM_EXPERT, D_MODEL), F32),
                            pltpu.SemaphoreType.DMA(()),
                            pltpu.SemaphoreType.DMA(())]),
        out_shape=jax.ShapeDtypeStruct((n_rows, D_MODEL), F32),
        compiler_params=pltpu.CompilerParams(
            dimension_semantics=("arbitrary",), has_side_effects=True,
            vmem_limit_bytes=VMEM_LIMIT),
        name="dispatch",
    )(dest, pad_start, pad_cnt, n_used, hn)


def _experts_kernel(tile0_ref, ntile_ref, xs_ref, wg_ref, bg_ref, wl_ref, bl_ref, wd_ref, bd_ref,
                    ys_ref, wg_bf, wl_bf, wd_bf, xbuf, ybuf, xsem, ysem):
    e = pl.program_id(0)
    tm = TM_EXPERT
    t0, nt = tile0_ref[e], ntile_ref[e]

    def x_load(tile, slot):
        return pltpu.make_async_copy(xs_ref.at[pl.ds(tile * tm, tm)], xbuf.at[slot], xsem.at[slot])

    def y_store(tile, slot):
        return pltpu.make_async_copy(ybuf.at[slot], ys_ref.at[pl.ds(tile * tm, tm)], ysem.at[slot])

    @pl.when((e == 0) & (nt > 0))
    def _():
        x_load(t0, 0).start()

    @pl.when(nt > 0)
    def _():
        wg_bf[...] = wg_ref[0].astype(BF16)
        wl_bf[...] = wl_ref[0].astype(BF16)
        wd_bf[...] = wd_ref[0].astype(BF16)

    def tile_body(j, carry):
        slot = j % 2

        @pl.when(j + 1 < nt)
        def _():
            x_load(t0 + j + 1, 1 - slot).start()

        x_load(t0 + j, slot).wait()

        @pl.when(j >= 2)
        def _():
            y_store(t0 + j - 2, slot).wait()

        x = xbuf[slot].astype(BF16)
        glu = jnp.minimum(jnp.dot(x, wg_bf[...], preferred_element_type=F32) + bg_ref[0],
                          SWIGLU_LIMIT)
        lin = jnp.clip(jnp.dot(x, wl_bf[...], preferred_element_type=F32) + bl_ref[0],
                       -SWIGLU_LIMIT, SWIGLU_LIMIT)
        hid = glu * jax.nn.sigmoid(SWIGLU_ALPHA * glu) * (lin + 1.0)
        ybuf[slot] = jnp.dot(hid.astype(BF16), wd_bf[...], preferred_element_type=F32) + bd_ref[0]
        y_store(t0 + j, slot).start()
        return carry

    lax.fori_loop(0, nt, tile_body, 0)

    @pl.when(e + 1 < pl.num_programs(0))
    def _():
        @pl.when(ntile_ref[e + 1] > 0)
        def _():
            x_load(tile0_ref[e + 1], 0).start()

    @pl.when(nt >= 2)
    def _():
        y_store(t0 + nt - 2, nt % 2).wait()

    @pl.when(nt >= 1)
    def _():
        y_store(t0 + nt - 1, (nt - 1) % 2).wait()

    @pl.when(e == pl.num_programs(0) - 1)
    def _():
        ybuf[0] = jnp.zeros((tm, D_MODEL), F32)
        n_tiles = ys_ref.shape[0] // tm

        def tail_issue(t, carry):
            y_store(t, 0).start()
            return carry

        def tail_wait(t, carry):
            y_store(t, 0).wait()
            return carry

        lax.fori_loop(t0 + nt, n_tiles, tail_issue, 0)
        lax.fori_loop(t0 + nt, n_tiles, tail_wait, 0)


def _experts(tile0, ntile, xs, w_glu, b_glu, w_lin, b_lin, w_down, b_down):
    tm = TM_EXPERT
    any_spec = pl.BlockSpec(memory_space=pl.ANY)
    wspec = lambda: pl.BlockSpec((1, D_MODEL, D_FF), lambda e, *_: (e, 0, 0))
    bspec = lambda: pl.BlockSpec((1, 1, D_FF), lambda e, *_: (e, 0, 0))
    return pl.pallas_call(
        _experts_kernel,
        grid_spec=pltpu.PrefetchScalarGridSpec(
            num_scalar_prefetch=2,
            grid=(N_EXPERTS,),
            in_specs=[any_spec, wspec(), bspec(), wspec(), bspec(), wspec(), bspec()],
            out_specs=any_spec,
            scratch_shapes=[pltpu.VMEM((D_MODEL, D_FF), BF16)] * 3
            + [pltpu.VMEM((2, tm, D_MODEL), F32), pltpu.VMEM((2, tm, D_MODEL), F32),
               pltpu.SemaphoreType.DMA((2,)), pltpu.SemaphoreType.DMA((2,))]),
        out_shape=jax.ShapeDtypeStruct(xs.shape, F32),
        compiler_params=pltpu.CompilerParams(
            dimension_semantics=("arbitrary",), vmem_limit_bytes=VMEM_LIMIT,
            has_side_effects=True),
        name="experts",
    )(tile0, ntile, xs, w_glu, b_glu.reshape(N_EXPERTS, 1, D_FF),
      w_lin, b_lin.reshape(N_EXPERTS, 1, D_FF), w_down, b_down.reshape(N_EXPERTS, 1, D_MODEL))


def _combine_kernel(dest_ref, ys_ref, h1_ref, gate_ref, g_ref, o_ref, buf, sems):
    s = pl.program_id(0)
    n = pl.num_programs(0) - 1
    tm = TM_COMBINE

    def issue(slot):
        base = s * (tm * TOP_K)
        for p in range(tm * TOP_K):
            r, k = divmod(p, TOP_K)
            pltpu.make_async_copy(ys_ref.at[pl.ds(dest_ref[base + p], 1)],
                                  buf.at[slot, k, pl.ds(r, 1)],
                                  sems.at[slot]).start(priority=p % 2)

    for parity in range(2):
        @pl.when((s < n) & (s % 2 == parity))
        def _():
            issue(parity)

    @pl.when(s > 0)
    def _():
        slot = (s - 1) % 2
        for k in range(TOP_K):
            pltpu.make_async_copy(ys_ref.at[pl.ds(0, tm)], buf.at[slot, k], sems.at[slot]).wait()
        acc = h1_ref[...]
        for k in range(TOP_K):
            acc = acc + gate_ref[:, k:k + 1] * buf[slot, k]
        o_ref[...] = _rms(acc, g_ref[...])


def _combine(dest, ys, h1, gate, fg):
    T = h1.shape[0]
    tm = TM_COMBINE
    return pl.pallas_call(
        _combine_kernel,
        grid_spec=pltpu.PrefetchScalarGridSpec(
            num_scalar_prefetch=1,
            grid=(T // tm + 1,),
            in_specs=[pl.BlockSpec(memory_space=pl.ANY),
                      pl.BlockSpec((tm, D_MODEL), lambda s, d: (jnp.maximum(s - 1, 0), 0)),
                      pl.BlockSpec((tm, TOP_K), lambda s, d: (jnp.maximum(s - 1, 0), 0)),
                      pl.BlockSpec((1, D_MODEL), lambda s, d: (0, 0))],
            out_specs=pl.BlockSpec((tm, D_MODEL), lambda s, d: (jnp.maximum(s - 1, 0), 0)),
            scratch_shapes=[pltpu.VMEM((2, TOP_K, tm, D_MODEL), F32),
                            pltpu.SemaphoreType.DMA((2,))]),
        out_shape=jax.ShapeDtypeStruct((T, D_MODEL), F32),
        compiler_params=pltpu.CompilerParams(
            dimension_semantics=("arbitrary",), vmem_limit_bytes=VMEM_LIMIT),
        name="combine",
    )(dest, ys, h1, gate, fg)


def _swap_halves(w):
    h = w.shape[-1] // 2
    return jnp.concatenate([w[..., h:], w[..., :h]], axis=-1)


def _layer(x2, batch, seq, norm1_g, w_in, w_alpha_up, b_alpha, gla_norm_g, swa_sinks, swa_norm_g,
           w_out, norm2_g, w_router, b_router, w_glu, b_glu, w_lin, b_lin, w_down, b_down):
    T = x2.shape[0]
    kb_w, vb_w = w_in[:, 2064:2192], w_in[:, 2192:2320]
    w_z = w_in[:, 1536:1552]
    w_cat = jnp.concatenate([
        w_in[:, 0:1536], w_in[:, 1552:2064],
        kb_w, _swap_halves(kb_w), vb_w, _swap_halves(vb_w),
        jnp.pad(jnp.tile(w_z, (1, Z_PIECES)), [(0, 0), (0, LANES - Z_PIECES * GLA_RANK)]),
    ], axis=1).astype(BF16)
    wup_hi = w_alpha_up.astype(BF16)
    wup_lo = (w_alpha_up - wup_hi.astype(F32)).astype(BF16)
    wup_cat = jnp.pad(jnp.concatenate([wup_hi, wup_hi, wup_hi, wup_lo, wup_lo], axis=0),
                      [(0, LANES - Z_PIECES * GLA_RANK), (0, 0)])

    qa, ka, va, ra, la, qb, kb, kbs, vb, vbs = _in_proj(
        x2, norm1_g.reshape(1, -1), w_cat, wup_cat, b_alpha.reshape(1, -1))
    oa = _gla(qa, ka, va, ra, la, gla_norm_g.reshape(1, -1), batch, seq)
    ob = _swa(qb, kb, kbs, vb, vbs, swa_sinks, jnp.tile(swa_norm_g, 2).reshape(1, -1), batch, seq)

    wo = w_out.astype(BF16)
    wr_t = w_router.T
    wr_hi = wr_t.astype(BF16)
    wr_lo = (wr_t - wr_hi.astype(F32)).astype(BF16)
    h1, hn, route, cnt = _out_route(oa, ob, x2, wo[:512], wo[512:], norm2_g.reshape(1, -1),
                                    wr_hi, wr_lo, b_router.reshape(-1, 1))

    tm = TM_EXPERT
    n_tiles = T * TOP_K // tm + N_EXPERTS
    counts = cnt[:, 0].astype(jnp.int32)
    padded = (counts + tm - 1) // tm * tm
    pends = jnp.cumsum(padded)
    pstarts = pends - padded
    top_idx = route[0:TOP_K].T.astype(jnp.int32)
    gate = route[TOP_K:2 * TOP_K].T
    rank = route[2 * TOP_K:3 * TOP_K].T.astype(jnp.int32)
    seg_start = jnp.sum(jnp.where(top_idx[..., None] == jnp.arange(N_EXPERTS), pstarts, 0), axis=-1)
    dest = (seg_start + rank).reshape(-1)
    n_used = (pends[-1] // tm).reshape(1)
    xs = _dispatch(dest, pstarts + counts, padded - counts, n_used, hn, n_tiles * tm)
    ys = _experts(pstarts // tm, padded // tm, xs, w_glu, b_glu, w_lin, b_lin, w_down, b_down)
    return dest, ys, h1, gate


def kernel(x, norm1_g, w_in, w_alpha_up, b_alpha, gla_norm_g, swa_sinks, swa_norm_g, w_out,
           norm2_g, w_router, b_router, w_glu, b_glu, w_lin, b_lin, w_down, b_down, final_g):
    batch, seq, d = x.shape
    assert norm1_g.shape[0] == 1, "single-layer problem"
    x2 = x.reshape(batch * seq, d)
    dest, ys, h1, gate = _layer(
        x2, batch, seq, norm1_g[0], w_in[0], w_alpha_up[0], b_alpha[0], gla_norm_g[0],
        swa_sinks[0], swa_norm_g[0], w_out[0], norm2_g[0], w_router[0], b_router[0],
        w_glu[0], b_glu[0], w_lin[0], b_lin[0], w_down[0], b_down[0])
    out = _combine(dest, ys, h1, gate, final_g.reshape(1, -1))
    return out.reshape(batch, seq, d)
```

```python
import numpy as np
import jax
import jax.numpy as jnp
from jax import lax
from jax.experimental import pallas as pl
from jax.experimental.pallas import tpu as pltpu

F32 = jnp.float32
BF16 = jnp.bfloat16
HIGHEST = lax.Precision.HIGHEST

D_MODEL = 1024
GLA_HEADS = 4
GLA_DK = 64
GLA_DV = 128
GLA_RANK = 16
GLA_GATE_TAU = 16.0
GLA_CHUNK = 64
SWA_HEADS = 8
SWA_KV_HEADS = 2
SWA_HEAD_DIM = 64
SWA_WINDOW = 128
N_EXPERTS = 32
TOP_K = 4
D_FF = 1024
SWIGLU_LIMIT = 7.0
SWIGLU_ALPHA = 1.702
NORM_EPS = 1e-5

LANES = 128
VMEM_LIMIT = 56 * 1024 * 1024

TM_PROJ = 512
TL_GLA = 256
TM_ROUTE = 512
TM_EXPERT = 512
TM_COMBINE = 256
SWA_ROWS = 128
TM_DISPATCH = 256
TILE_DMA_PRIORITY = 1

NEG_BIG = -1e30
ROUTE_ROWS = 16

C_QA, C_KA, C_VA, C_RA, C_QB, C_KB, C_KBS, C_VB, C_VBS, C_Z, C_END = (
    0, 256, 512, 1024, 1536, 2048, 2176, 2304, 2432, 2560, 2688)
Z_PIECES = 5


def _split3(x):
    hi = x.astype(BF16).astype(F32)
    r = x - hi
    mid = r.astype(BF16).astype(F32)
    lo = (r - mid).astype(BF16).astype(F32)
    return hi, mid, lo


def _rms(x, g):
    return x * lax.rsqrt(jnp.mean(x * x, axis=-1, keepdims=True) + NORM_EPS) * g


def _in_proj_kernel(x_ref, g_ref, w_ref, wup_ref, ba_ref,
                    qa_ref, ka_ref, va_ref, ra_ref, la_ref, qb_ref, kb_ref, kbs_ref, vb_ref, vbs_ref):
    u = _rms(x_ref[...], g_ref[...]).astype(BF16)

    def proj(c0, c1):
        return jnp.dot(u, w_ref[:, c0:c1], preferred_element_type=F32)

    qa_ref[...] = proj(C_QA, C_KA).astype(BF16)
    ka_ref[...] = proj(C_KA, C_VA).astype(BF16)
    va_ref[...] = proj(C_VA, C_RA).astype(BF16)
    ra_ref[...] = proj(C_RA, C_QB).astype(BF16)
    qb_ref[...] = proj(C_QB, C_KB).astype(BF16)
    kb_ref[...] = proj(C_KB, C_KBS).astype(BF16)
    kbs_ref[...] = proj(C_KBS, C_VB).astype(BF16)
    vb_ref[...] = proj(C_VB, C_VBS).astype(BF16)
    vbs_ref[...] = proj(C_VBS, C_Z).astype(BF16)
    z = proj(C_Z, C_END)
    hi, mid, lo = _split3(z)
    piece = lax.broadcasted_iota(jnp.int32, z.shape, 1) // GLA_RANK
    zc = jnp.where((piece == 0) | (piece == 3), hi, jnp.where(piece == 2, lo, mid)).astype(BF16)
    y = jnp.dot(zc, wup_ref[...], preferred_element_type=F32) + ba_ref[...]
    log_sig = jnp.minimum(y, 0.0) - jnp.log1p(jnp.exp(-jnp.abs(y)))
    la_ref[...] = log_sig * (1.0 / GLA_GATE_TAU)


def _in_proj(x2, g1, w_cat, wup_p, ba_p):
    T = x2.shape[0]
    tm = TM_PROJ
    row = lambda w: pl.BlockSpec((tm, w), lambda i: (i, 0))
    full = lambda a: pl.BlockSpec(a.shape, lambda i: (0,) * a.ndim)
    outs = [(256, BF16), (256, BF16), (512, BF16), (512, BF16), (256, F32),
            (512, BF16), (128, BF16), (128, BF16), (128, BF16), (128, BF16)]
    return pl.pallas_call(
        _in_proj_kernel,
        grid=(T // tm,),
        in_specs=[row(D_MODEL), full(g1), full(w_cat), full(wup_p), full(ba_p)],
        out_specs=[row(w) for w, _ in outs],
        out_shape=[jax.ShapeDtypeStruct((T, w), dt) for w, dt in outs],
        compiler_params=pltpu.CompilerParams(
            dimension_semantics=("arbitrary",), vmem_limit_bytes=VMEM_LIMIT),
        name="in_proj",
    )(x2, g1, w_cat, wup_p, ba_p)


def _gla_kernel(q_ref, k_ref, v_ref, r_ref, la_ref, g_ref, cum_ref, o_ref, st_ref):
    @pl.when(pl.program_id(1) == 0)
    def _():
        st_ref[...] = jnp.zeros_like(st_ref)

    tl = TL_GLA
    c = GLA_CHUNK
    kw = GLA_HEADS * GLA_DK
    pieces = jnp.concatenate([p.astype(BF16) for p in _split3(la_ref[...])], axis=1)
    b3 = jnp.dot(cum_ref[...], pieces, preferred_element_type=F32)
    b_all = b3[:, :kw] + b3[:, kw:2 * kw] + b3[:, 2 * kw:]
    causal = (lax.broadcasted_iota(jnp.int32, (c, c), 0)
              >= lax.broadcasted_iota(jnp.int32, (c, c), 1))
    low_half = lax.broadcasted_iota(jnp.int32, (c, LANES), 1) < GLA_DK
    g = g_ref[...]
    for ch in range(tl // c):
        rows = slice(ch * c, (ch + 1) * c)
        b = b_all[rows]
        b_last = b[c - 1:c]
        qf = q_ref[rows, :].astype(F32)
        kf = k_ref[rows, :].astype(F32)
        q_e = (qf * jnp.exp(b) * (GLA_DK ** -0.5)).astype(BF16)
        k_e = (kf * jnp.exp(-b)).astype(BF16)
        k_t = (kf * jnp.exp(b_last - b)).astype(BF16)
        decay = jnp.exp(b_last)
        for h in range(GLA_HEADS):
            ps = slice((h // 2) * LANES, (h // 2 + 1) * LANES)
            mine = low_half if h % 2 == 0 else ~low_half
            qp, kp = q_e[:, ps], k_e[:, ps]
            qh = jnp.where(mine, qp, jnp.zeros_like(qp))
            kth = jnp.where(mine, k_t[:, ps], jnp.zeros_like(qp))
            vs = slice(h * GLA_DV, (h + 1) * GLA_DV)
            vh = v_ref[rows, vs]
            a = pl.dot(qh, kp, trans_b=True)
            a = jnp.where(causal, a, 0.0).astype(BF16)
            st = st_ref[h]
            o = (jnp.dot(a, vh, preferred_element_type=F32)
                 + pl.dot(qh, st.astype(BF16), trans_b=True))
            st_ref[h] = st * decay[:, ps] + pl.dot(vh, kth, trans_a=True)
            rh = r_ref[rows, vs].astype(F32)
            o = _rms(o, g) * (rh * jax.nn.sigmoid(rh))
            o_ref[rows, vs] = o.astype(BF16)


def _gla(qa, ka, va, ra, la, g, batch, seq):
    tl = TL_GLA
    nl = seq // tl
    row = lambda w: pl.BlockSpec((tl, w), lambda b, i: (b * nl + i, 0))
    r = np.arange(tl)
    cum = jnp.asarray((r[None, :] <= r[:, None])
                      & (r[None, :] // GLA_CHUNK == r[:, None] // GLA_CHUNK), BF16)
    return pl.pallas_call(
        _gla_kernel,
        grid=(batch, nl),
        in_specs=[row(256), row(256), row(512), row(512), row(256),
                  pl.BlockSpec((1, GLA_DV), lambda b, i: (0, 0)),
                  pl.BlockSpec((tl, tl), lambda b, i: (0, 0))],
        out_specs=row(512),
        out_shape=jax.ShapeDtypeStruct((batch * seq, 512), BF16),
        scratch_shapes=[pltpu.VMEM((GLA_HEADS, GLA_DV, LANES), F32)],
        compiler_params=pltpu.CompilerParams(
            dimension_semantics=("arbitrary", "arbitrary"), vmem_limit_bytes=VMEM_LIMIT),
        name="gla",
    )(qa, ka, va, ra, la, g, cum)


def _swa_kernel(sink_ref, bias_ref, q_ref, kc_ref, kp_ref, ksc_ref, ksp_ref,
                vc_ref, vp_ref, vsc_ref, vsp_ref, g_ref, o_ref):
    w = SWA_WINDOW
    cat = lambda p, cu: jnp.concatenate([p[...], cu[...]], axis=0)
    scale = jnp.asarray(SWA_HEAD_DIM ** -0.5, BF16)
    k = cat(kp_ref, kc_ref) * scale
    ks = cat(ksp_ref, ksc_ref) * scale
    v, vs = cat(vp_ref, vc_ref), cat(vsp_ref, vsc_ref)
    lane_lo = lax.broadcasted_iota(jnp.int32, (2 * w, LANES), 1) < SWA_HEAD_DIM
    zero = jnp.zeros_like(k)
    k_low = [jnp.where(lane_lo, k, zero), jnp.where(lane_lo, ks, zero)]
    k_high = [jnp.where(lane_lo, zero, ks), jnp.where(lane_lo, zero, k)]
    v_low = [jnp.where(lane_lo, v, zero), jnp.where(lane_lo, vs, zero)]
    v_high = [jnp.where(lane_lo, zero, vs), jnp.where(lane_lo, zero, v)]

    hi_half = lax.broadcasted_iota(jnp.int32, (LANES, LANES), 0) // SWA_HEAD_DIM
    hj_half = lax.broadcasted_iota(jnp.int32, (LANES, LANES), 1) // SWA_HEAD_DIM
    mean_mat = jnp.where(hi_half == hj_half, 1.0 / SWA_HEAD_DIM, 0.0).astype(F32)
    g = g_ref[...]

    def probs(s, head, rows):
        s = s + bias_ref[0, head, rows, :]
        sink = sink_ref[head]
        m = jnp.maximum(jnp.max(s, axis=-1, keepdims=True), sink)
        e = jnp.exp(s - m)
        den = jnp.sum(e, axis=-1, keepdims=True) + jnp.exp(sink - m)
        return (e * (1.0 / den)).astype(BF16)

    for pair in range(SWA_HEADS // 2):
        j = (2 * pair) // (SWA_HEADS // SWA_KV_HEADS)
        cols = slice(pair * LANES, (pair + 1) * LANES)
        for rb in range(w // SWA_ROWS):
            rows = slice(rb * SWA_ROWS, (rb + 1) * SWA_ROWS)
            qp = q_ref[rows, cols]
            p0 = probs(pl.dot(qp, k_low[j], trans_b=True), 2 * pair, rows)
            p1 = probs(pl.dot(qp, k_high[j], trans_b=True), 2 * pair + 1, rows)
            o = (jnp.dot(p0, v_low[j], preferred_element_type=F32)
                 + jnp.dot(p1, v_high[j], preferred_element_type=F32))
            ms = jnp.dot(o * o, mean_mat, preferred_element_type=F32, precision=HIGHEST)
            o = o * lax.rsqrt(ms + NORM_EPS) * g
            o_ref[rows, cols] = o.astype(BF16)


def _swa_bias():
    w = SWA_WINDOW
    slopes = 2.0 ** (-8.0 * np.arange(1, SWA_HEADS + 1, dtype=np.float64) / SWA_HEADS)
    rel = np.arange(w)[:, None] + w - np.arange(2 * w)[None, :]
    in_window = (rel >= 0) & (rel < w)
    exists = np.stack([np.arange(2 * w) >= w, np.ones(2 * w, bool)])
    valid = in_window[None] & exists[:, None, :]
    bias = -slopes[None, :, None, None] * rel[None, None].astype(np.float64)
    return np.where(valid[:, None], bias, NEG_BIG).astype(np.float32)


def _swa(qb, kb, kbs, vb, vbs, sinks, g2, batch, seq):
    w = SWA_WINDOW
    nb = seq // w
    cur = lambda width: pl.BlockSpec((w, width), lambda b, n: (b * nb + n, 0))
    prev = lambda width: pl.BlockSpec((w, width), lambda b, n: (b * nb + jnp.maximum(n - 1, 0), 0))
    return pl.pallas_call(
        _swa_kernel,
        grid=(batch, nb),
        in_specs=[pl.BlockSpec(memory_space=pltpu.SMEM),
                  pl.BlockSpec((1, SWA_HEADS, w, 2 * w), lambda b, n: (jnp.minimum(n, 1), 0, 0, 0)),
                  cur(512), cur(128), prev(128), cur(128), prev(128),
                  cur(128), prev(128), cur(128), prev(128),
                  pl.BlockSpec((1, LANES), lambda b, n: (0, 0))],
        out_specs=cur(512),
        out_shape=jax.ShapeDtypeStruct((batch * seq, 512), BF16),
        compiler_params=pltpu.CompilerParams(
            dimension_semantics=("arbitrary", "arbitrary"), vmem_limit_bytes=VMEM_LIMIT),
        name="swa",
    )(sinks, jnp.asarray(_swa_bias()), qb, kb, kb, kbs, kbs, vb, vb, vbs, vbs, g2)


def _out_route_kernel(oa_ref, ob_ref, x_ref, woa_ref, wob_ref, g_ref, wrh_ref, wrl_ref, br_ref,
                      earlier_ref, h1_ref, hn_ref, route_ref, cnt_ref):
    @pl.when(pl.program_id(0) == 0)
    def _():
        cnt_ref[...] = jnp.zeros_like(cnt_ref)

    tm = TM_ROUTE
    h1 = (x_ref[...]
          + jnp.dot(oa_ref[...], woa_ref[...], preferred_element_type=F32)
          + jnp.dot(ob_ref[...], wob_ref[...], preferred_element_type=F32))
    h1_ref[...] = h1
    hn = _rms(h1, g_ref[...])
    hn_ref[...] = hn
    hn_hi = hn.astype(BF16)
    hn_lo = (hn - hn_hi.astype(F32)).astype(BF16)
    logits = (pl.dot(wrh_ref[...], hn_hi, trans_b=True)
              + pl.dot(wrh_ref[...], hn_lo, trans_b=True)
              + pl.dot(wrl_ref[...], hn_hi, trans_b=True)) + br_ref[...]

    eid = lax.broadcasted_iota(jnp.int32, (N_EXPERTS, tm), 0)
    work = logits
    vals, idxs, sels = [], [], []
    for _ in range(TOP_K):
        m = jnp.max(work, axis=0, keepdims=True)
        idx = jnp.min(jnp.where(work == m, eid, N_EXPERTS), axis=0, keepdims=True)
        sel = eid == idx
        vals.append(m)
        idxs.append(idx)
        sels.append(sel)
        work = jnp.where(sel, -3e38, work)
    exps = [jnp.exp(v - vals[0]) for v in vals]
    inv_den = 1.0 / (exps[0] + exps[1] + exps[2] + exps[3])

    multihot = jnp.where(sels[0] | sels[1] | sels[2] | sels[3], 1.0, 0.0)
    before = (jnp.dot(multihot.astype(BF16), earlier_ref[...], preferred_element_type=F32)
              + cnt_ref[:, 0:1])
    ranks = [jnp.sum(jnp.where(s, before, 0.0), axis=0, keepdims=True) for s in sels]
    route_ref[...] = jnp.concatenate(
        [i.astype(F32) for i in idxs] + [e * inv_den for e in exps] + ranks
        + [jnp.zeros((ROUTE_ROWS - 3 * TOP_K, tm), F32)], axis=0)
    cnt_ref[...] += jnp.sum(multihot, axis=1, keepdims=True)


def _out_route(oa, ob, x2, woa, wob, g2, wr_hi, wr_lo, br):
    T = x2.shape[0]
    tm = TM_ROUTE
    row = lambda w: pl.BlockSpec((tm, w), lambda i: (i, 0))
    full = lambda a: pl.BlockSpec(a.shape, lambda i: (0,) * a.ndim)
    earlier = jnp.asarray(np.triu(np.ones((tm, tm), np.float32), k=1), BF16)
    return pl.pallas_call(
        _out_route_kernel,
        grid=(T // tm,),
        in_specs=[row(512), row(512), row(D_MODEL), full(woa), full(wob), full(g2),
                  full(wr_hi), full(wr_lo), full(br), full(earlier)],
        out_specs=[row(D_MODEL), row(D_MODEL),
                   pl.BlockSpec((ROUTE_ROWS, tm), lambda i: (0, i)),
                   pl.BlockSpec((N_EXPERTS, LANES), lambda i: (0, 0))],
        out_shape=[jax.ShapeDtypeStruct((T, D_MODEL), F32),
                   jax.ShapeDtypeStruct((T, D_MODEL), F32),
                   jax.ShapeDtypeStruct((ROUTE_ROWS, T), F32),
                   jax.ShapeDtypeStruct((N_EXPERTS, LANES), F32)],
        compiler_params=pltpu.CompilerParams(
            dimension_semantics=("arbitrary",), vmem_limit_bytes=VMEM_LIMIT),
        name="out_route",
    )(oa, ob, x2, woa, wob, g2, wr_hi, wr_lo, br, earlier)


def _row_copy(src, s, dst, d, sem):
    return pltpu.make_async_copy(src.at[pl.ds(s, 1)], dst.at[pl.ds(d, 1)], sem)


def _dispatch_kernel(dest_ref, pad_start_ref, pad_cnt_ref, nused_ref, hn_ref, xs_ref,
                     zero_ref, sem, zsem):
    i = pl.program_id(0)
    n = pl.num_programs(0)
    tm = TM_EXPERT
    base = i * (TM_DISPATCH * TOP_K)
    for p in range(TM_DISPATCH * TOP_K):
        _row_copy(hn_ref, p // TOP_K, xs_ref, dest_ref[base + p], sem).start(priority=p % 2)
    for _ in range(TOP_K):
        pltpu.make_async_copy(hn_ref, xs_ref.at[pl.ds(0, TM_DISPATCH)], sem).wait()

    @pl.when(i == n - 1)
    def _():
        zero_ref[...] = jnp.zeros_like(zero_ref)
        sub = 8
        for e in range(N_EXPERTS):
            start, cnt = pad_start_ref[e], pad_cnt_ref[e]
            head = jnp.minimum((-start) & (sub - 1), cnt)
            body0 = start + head
            nbody = (cnt - head) // sub

            def head_copy(r):
                return _row_copy(zero_ref, 0, xs_ref, start + r, zsem)

            def body_copy(c):
                at = pl.multiple_of(body0 + c * sub, sub)
                return pltpu.make_async_copy(zero_ref.at[pl.ds(0, sub)],
                                             xs_ref.at[pl.ds(at, sub)], zsem)

            for copy, count in ((head_copy, head), (body_copy, nbody)):
                def zissue(r, carry, copy=copy):
                    copy(r).start()
                    return carry

                def zwait(r, carry, copy=copy):
                    copy(r).wait()
                    return carry

                lax.fori_loop(0, count, zissue, 0)
                lax.fori_loop(0, count, zwait, 0)

        def tail_copy(t):
            return pltpu.make_async_copy(zero_ref, xs_ref.at[pl.ds(t * tm, tm)], zsem)

        def tissue(t, carry):
            tail_copy(t).start()
            return carry

        def twait(t, carry):
            tail_copy(t).wait()
            return carry

        n_tiles = xs_ref.shape[0] // tm
        lax.fori_loop(nused_ref[0], n_tiles, tissue, 0)
        lax.fori_loop(nused_ref[0], n_tiles, twait, 0)


def _dispatch(dest, pad_start, pad_cnt, n_used, hn, n_rows):
    T = hn.shape[0]
    any_spec = pl.BlockSpec(memory_space=pl.ANY)
    return pl.pallas_call(
        _dispatch_kernel,
        grid_spec=pltpu.PrefetchScalarGridSpec(
            num_scalar_prefetch=4,
            grid=(T // TM_DISPATCH,),
            in_specs=[pl.BlockSpec((TM_DISPATCH, D_MODEL), lambda i, *_: (i, 0))],
            out_specs=any_spec,
            scratch_shapes=[pltpu.VMEM((TM_EXPERT, D_MODEL), F32),
                            pltpu.SemaphoreType.DMA(()),
                            pltpu.SemaphoreType.DMA(())]),
        out_shape=jax.ShapeDtypeStruct((n_rows, D_MODEL), F32),
        compiler_params=pltpu.CompilerParams(
            dimension_semantics=("arbitrary",), has_side_effects=True,
            vmem_limit_bytes=VMEM_LIMIT),
        name="dispatch",
    )(dest, pad_start, pad_cnt, n_used, hn)


def _experts_kernel(tile0_ref, ntile_ref, xs_ref, wg_ref, bg_ref, wl_ref, bl_ref, wd_ref, bd_ref,
                    ys_ref, wg_bf, wl_bf, wd_bf, xbuf, ybuf, xsem, ysem):
    e = pl.program_id(0)
    tm = TM_EXPERT
    t0, nt = tile0_ref[e], ntile_ref[e]

    def x_load(tile, slot):
        return pltpu.make_async_copy(xs_ref.at[pl.ds(tile * tm, tm)], xbuf.at[slot], xsem.at[slot])

    def y_store(tile, slot):
        return pltpu.make_async_copy(ybuf.at[slot], ys_ref.at[pl.ds(tile * tm, tm)], ysem.at[slot])

    @pl.when((e == 0) & (nt > 0))
    def _():
        x_load(t0, 0).start(priority=TILE_DMA_PRIORITY)

    @pl.when(nt > 0)
    def _():
        wg_bf[...] = wg_ref[0].astype(BF16)
        wl_bf[...] = wl_ref[0].astype(BF16)
        wd_bf[...] = wd_ref[0].astype(BF16)

    def tile_body(j, carry):
        slot = j % 2

        @pl.when(j + 1 < nt)
        def _():
            x_load(t0 + j + 1, 1 - slot).start(priority=TILE_DMA_PRIORITY)

        x_load(t0 + j, slot).wait()

        @pl.when(j >= 2)
        def _():
            y_store(t0 + j - 2, slot).wait()

        x = xbuf[slot].astype(BF16)
        glu = jnp.minimum(jnp.dot(x, wg_bf[...], preferred_element_type=F32) + bg_ref[0],
                          SWIGLU_LIMIT)
        lin = jnp.clip(jnp.dot(x, wl_bf[...], preferred_element_type=F32) + bl_ref[0],
                       -SWIGLU_LIMIT, SWIGLU_LIMIT)
        hid = glu * jax.nn.sigmoid(SWIGLU_ALPHA * glu) * (lin + 1.0)
        ybuf[slot] = jnp.dot(hid.astype(BF16), wd_bf[...], preferred_element_type=F32) + bd_ref[0]
        y_store(t0 + j, slot).start(priority=TILE_DMA_PRIORITY)
        return carry

    lax.fori_loop(0, nt, tile_body, 0)

    @pl.when(e + 1 < pl.num_programs(0))
    def _():
        @pl.when(ntile_ref[e + 1] > 0)
        def _():
            x_load(tile0_ref[e + 1], 0).start(priority=TILE_DMA_PRIORITY)

    @pl.when(nt >= 2)
    def _():
        y_store(t0 + nt - 2, nt % 2).wait()

    @pl.when(nt >= 1)
    def _():
        y_store(t0 + nt - 1, (nt - 1) % 2).wait()

    @pl.when(e == pl.num_programs(0) - 1)
    def _():
        ybuf[0] = jnp.zeros((tm, D_MODEL), F32)
        n_tiles = ys_ref.shape[0] // tm

        def tail_issue(t, carry):
            y_store(t, 0).start()
            return carry

        def tail_wait(t, carry):
            y_store(t, 0).wait()
            return carry

        lax.fori_loop(t0 + nt, n_tiles, tail_issue, 0)
        lax.fori_loop(t0 + nt, n_tiles, tail_wait, 0)


def _experts(tile0, ntile, xs, w_glu, b_glu, w_lin, b_lin, w_down, b_down):
    tm = TM_EXPERT
    any_spec = pl.BlockSpec(memory_space=pl.ANY)
    wspec = lambda: pl.BlockSpec((1, D_MODEL, D_FF), lambda e, *_: (e, 0, 0))
    bspec = lambda: pl.BlockSpec((1, 1, D_FF), lambda e, *_: (e, 0, 0))
    return pl.pallas_call(
        _experts_kernel,
        grid_spec=pltpu.PrefetchScalarGridSpec(
            num_scalar_prefetch=2,
            grid=(N_EXPERTS,),
            in_specs=[any_spec, wspec(), bspec(), wspec(), bspec(), wspec(), bspec()],
            out_specs=any_spec,
            scratch_shapes=[pltpu.VMEM((D_MODEL, D_FF), BF16)] * 3
            + [pltpu.VMEM((2, tm, D_MODEL), F32), pltpu.VMEM((2, tm, D_MODEL), F32),
               pltpu.SemaphoreType.DMA((2,)), pltpu.SemaphoreType.DMA((2,))]),
        out_shape=jax.ShapeDtypeStruct(xs.shape, F32),
        compiler_params=pltpu.CompilerParams(
            dimension_semantics=("arbitrary",), vmem_limit_bytes=VMEM_LIMIT,
            has_side_effects=True),
        name="experts",
    )(tile0, ntile, xs, w_glu, b_glu.reshape(N_EXPERTS, 1, D_FF),
      w_lin, b_lin.reshape(N_EXPERTS, 1, D_FF), w_down, b_down.reshape(N_EXPERTS, 1, D_MODEL))


def _combine_kernel(dest_ref, ys_ref, h1_ref, gate_ref, g_ref, o_ref, buf, sems):
    s = pl.program_id(0)
    n = pl.num_programs(0) - 1
    tm = TM_COMBINE

    def issue(slot):
        base = s * (tm * TOP_K)
        for p in range(tm * TOP_K):
            r, k = divmod(p, TOP_K)
            pltpu.make_async_copy(ys_ref.at[pl.ds(dest_ref[base + p], 1)],
                                  buf.at[slot, k, pl.ds(r, 1)],
                                  sems.at[slot]).start(priority=p % 2)

    for parity in range(2):
        @pl.when((s < n) & (s % 2 == parity))
        def _():
            issue(parity)

    @pl.when(s > 0)
    def _():
        slot = (s - 1) % 2
        for k in range(TOP_K):
            pltpu.make_async_copy(ys_ref.at[pl.ds(0, tm)], buf.at[slot, k], sems.at[slot]).wait()
        acc = h1_ref[...]
        for k in range(TOP_K):
            acc = acc + gate_ref[:, k:k + 1] * buf[slot, k]
        o_ref[...] = _rms(acc, g_ref[...])


def _combine(dest, ys, h1, gate, fg):
    T = h1.shape[0]
    tm = TM_COMBINE
    return pl.pallas_call(
        _combine_kernel,
        grid_spec=pltpu.PrefetchScalarGridSpec(
            num_scalar_prefetch=1,
            grid=(T // tm + 1,),
            in_specs=[pl.BlockSpec(memory_space=pl.ANY),
                      pl.BlockSpec((tm, D_MODEL), lambda s, d: (jnp.maximum(s - 1, 0), 0)),
                      pl.BlockSpec((tm, TOP_K), lambda s, d: (jnp.maximum(s - 1, 0), 0)),
                      pl.BlockSpec((1, D_MODEL), lambda s, d: (0, 0))],
            out_specs=pl.BlockSpec((tm, D_MODEL), lambda s, d: (jnp.maximum(s - 1, 0), 0)),
            scratch_shapes=[pltpu.VMEM((2, TOP_K, tm, D_MODEL), F32),
                            pltpu.SemaphoreType.DMA((2,))]),
        out_shape=jax.ShapeDtypeStruct((T, D_MODEL), F32),
        compiler_params=pltpu.CompilerParams(
            dimension_semantics=("arbitrary",), vmem_limit_bytes=VMEM_LIMIT),
        name="combine",
    )(dest, ys, h1, gate, fg)


def _swap_halves(w):
    h = w.shape[-1] // 2
    return jnp.concatenate([w[..., h:], w[..., :h]], axis=-1)


def _layer(x2, batch, seq, norm1_g, w_in, w_alpha_up, b_alpha, gla_norm_g, swa_sinks, swa_norm_g,
           w_out, norm2_g, w_router, b_router, w_glu, b_glu, w_lin, b_lin, w_down, b_down):
    T = x2.shape[0]
    kb_w, vb_w = w_in[:, 2064:2192], w_in[:, 2192:2320]
    w_z = w_in[:, 1536:1552]
    w_cat = jnp.concatenate([
        w_in[:, 0:1536], w_in[:, 1552:2064],
        kb_w, _swap_halves(kb_w), vb_w, _swap_halves(vb_w),
        jnp.pad(jnp.tile(w_z, (1, Z_PIECES)), [(0, 0), (0, LANES - Z_PIECES * GLA_RANK)]),
    ], axis=1).astype(BF16)
    wup_hi = w_alpha_up.astype(BF16)
    wup_lo = (w_alpha_up - wup_hi.astype(F32)).astype(BF16)
    wup_cat = jnp.pad(jnp.concatenate([wup_hi, wup_hi, wup_hi, wup_lo, wup_lo], axis=0),
                      [(0, LANES - Z_PIECES * GLA_RANK), (0, 0)])

    qa, ka, va, ra, la, qb, kb, kbs, vb, vbs = _in_proj(
        x2, norm1_g.reshape(1, -1), w_cat, wup_cat, b_alpha.reshape(1, -1))
    oa = _gla(qa, ka, va, ra, la, gla_norm_g.reshape(1, -1), batch, seq)
    ob = _swa(qb, kb, kbs, vb, vbs, swa_sinks, jnp.tile(swa_norm_g, 2).reshape(1, -1), batch, seq)

    wo = w_out.astype(BF16)
    wr_t = w_router.T
    wr_hi = wr_t.astype(BF16)
    wr_lo = (wr_t - wr_hi.astype(F32)).astype(BF16)
    h1, hn, route, cnt = _out_route(oa, ob, x2, wo[:512], wo[512:], norm2_g.reshape(1, -1),
                                    wr_hi, wr_lo, b_router.reshape(-1, 1))

    tm = TM_EXPERT
    n_tiles = T * TOP_K // tm + N_EXPERTS
    counts = cnt[:, 0].astype(jnp.int32)
    padded = (counts + tm - 1) // tm * tm
    pends = jnp.cumsum(padded)
    pstarts = pends - padded
    top_idx = route[0:TOP_K].T.astype(jnp.int32)
    gate = route[TOP_K:2 * TOP_K].T
    rank = route[2 * TOP_K:3 * TOP_K].T.astype(jnp.int32)
    seg_start = jnp.sum(jnp.where(top_idx[..., None] == jnp.arange(N_EXPERTS), pstarts, 0), axis=-1)
    dest = (seg_start + rank).reshape(-1)
    n_used = (pends[-1] // tm).reshape(1)
    xs = _dispatch(dest, pstarts + counts, padded - counts, n_used, hn, n_tiles * tm)
    ys = _experts(pstarts // tm, padded // tm, xs, w_glu, b_glu, w_lin, b_lin, w_down, b_down)
    return dest, ys, h1, gate


def kernel(x, norm1_g, w_in, w_alpha_up, b_alpha, gla_norm_g, swa_sinks, swa_norm_g, w_out,
           norm2_g, w_router, b_router, w_glu, b_glu, w_lin, b_lin, w_down, b_down, final_g):
    batch, seq, d = x.shape
    assert norm1_g.shape[0] == 1, "single-layer problem"
    x2 = x.reshape(batch * seq, d)
    dest, ys, h1, gate = _layer(
        x2, batch, seq, norm1_g[0], w_in[0], w_alpha_up[0], b_alpha[0], gla_norm_g[0],
        swa_sinks[0], swa_norm_g[0], w_out[0], norm2_g[0], w_router[0], b_router[0],
        w_glu[0], b_glu[0], w_lin[0], b_lin[0], w_down[0], b_down[0])
    out = _combine(dest, ys, h1, gate, final_g.reshape(1, -1))
    return out.reshape(batch, seq, d)
```

```python
import numpy as np
import jax
import jax.numpy as jnp
from jax import lax
from jax.experimental import pallas as pl
from jax.experimental.pallas import tpu as pltpu

F32 = jnp.float32
BF16 = jnp.bfloat16
HIGHEST = lax.Precision.HIGHEST

D_MODEL = 1024
GLA_HEADS = 4
GLA_DK = 64
GLA_DV = 128
GLA_RANK = 16
GLA_GATE_TAU = 16.0
GLA_CHUNK = 64
SWA_HEADS = 8
SWA_KV_HEADS = 2
SWA_HEAD_DIM = 64
SWA_WINDOW = 128
N_EXPERTS = 32
TOP_K = 4
D_FF = 1024
SWIGLU_LIMIT = 7.0
SWIGLU_ALPHA = 1.702
NORM_EPS = 1e-5

LANES = 128
VMEM_LIMIT = 56 * 1024 * 1024

TM_PROJ = 512
TL_GLA = 1024
GLA_CUM_ROWS = 256
TM_ROUTE = 512
TM_EXPERT = 512
TM_COMBINE = 256
SWA_SUB = 4
TM_DISPATCH = 256

NEG_BIG = -1e30
ROUTE_ROWS = 16

C_QA, C_KA, C_VA, C_RA, C_QB, C_KB, C_KBS, C_VB, C_VBS, C_Z, C_END = (
    0, 256, 512, 1024, 1536, 2048, 2176, 2304, 2432, 2560, 2688)
Z_PIECES = 5


def _split3(x):
    hi = x.astype(BF16).astype(F32)
    r = x - hi
    mid = r.astype(BF16).astype(F32)
    lo = (r - mid).astype(BF16).astype(F32)
    return hi, mid, lo


def _rms(x, g):
    return x * lax.rsqrt(jnp.mean(x * x, axis=-1, keepdims=True) + NORM_EPS) * g


def _in_proj_kernel(x_ref, g_ref, w_ref, wup_ref, ba_ref,
                    qk_ref, vr_ref, la_ref, qb_ref, kv_ref):
    u = _rms(x_ref[...], g_ref[...]).astype(BF16)

    def proj(c0, c1):
        return jnp.dot(u, w_ref[:, c0:c1], preferred_element_type=F32)

    qk_ref[...] = proj(C_QA, C_VA).astype(BF16)
    vr_ref[:, :C_RA - C_VA] = proj(C_VA, C_RA).astype(BF16)
    vr_ref[:, C_RA - C_VA:] = proj(C_RA, C_QB).astype(BF16)
    qb_ref[...] = proj(C_QB, C_KB).astype(BF16)
    kv_ref[...] = proj(C_KB, C_Z).astype(BF16)
    z = proj(C_Z, C_END)
    hi, mid, lo = _split3(z)
    piece = lax.broadcasted_iota(jnp.int32, z.shape, 1) // GLA_RANK
    zc = jnp.where((piece == 0) | (piece == 3), hi, jnp.where(piece == 2, lo, mid)).astype(BF16)
    y = jnp.dot(zc, wup_ref[...], preferred_element_type=F32) + ba_ref[...]
    log_sig = jnp.minimum(y, 0.0) - jnp.log1p(jnp.exp(-jnp.abs(y)))
    la_ref[...] = log_sig * (1.0 / GLA_GATE_TAU)


def _in_proj(x2, g1, w_cat, wup_p, ba_p):
    T = x2.shape[0]
    tm = TM_PROJ
    row = lambda w: pl.BlockSpec((tm, w), lambda i: (i, 0))
    full = lambda a: pl.BlockSpec(a.shape, lambda i: (0,) * a.ndim)
    outs = [(512, BF16), (1024, BF16), (256, F32), (512, BF16), (512, BF16)]
    return pl.pallas_call(
        _in_proj_kernel,
        grid=(T // tm,),
        in_specs=[row(D_MODEL), full(g1), full(w_cat), full(wup_p), full(ba_p)],
        out_specs=[row(w) for w, _ in outs],
        out_shape=[jax.ShapeDtypeStruct((T, w), dt) for w, dt in outs],
        compiler_params=pltpu.CompilerParams(
            dimension_semantics=("arbitrary",), vmem_limit_bytes=VMEM_LIMIT),
        name="in_proj",
    )(x2, g1, w_cat, wup_p, ba_p)


def _gla_kernel(qk_ref, vr_ref, la_ref, g_ref, cum_ref, o_ref, st_ref):
    @pl.when(pl.program_id(1) == 0)
    def _():
        st_ref[...] = jnp.zeros_like(st_ref)

    tl = TL_GLA
    c = GLA_CHUNK
    kw = GLA_HEADS * GLA_DK
    vw = GLA_HEADS * GLA_DV
    causal = (lax.broadcasted_iota(jnp.int32, (c, c), 0)
              >= lax.broadcasted_iota(jnp.int32, (c, c), 1))
    low_half = lax.broadcasted_iota(jnp.int32, (c, LANES), 1) < GLA_DK
    g = g_ref[...]
    b_groups = []
    for grp in range(tl // GLA_CUM_ROWS):
        la = la_ref[grp * GLA_CUM_ROWS:(grp + 1) * GLA_CUM_ROWS, :]
        pieces = jnp.concatenate([p.astype(BF16) for p in _split3(la)], axis=1)
        b3 = jnp.dot(cum_ref[...], pieces, preferred_element_type=F32)
        b_groups.append(b3[:, :kw] + b3[:, kw:2 * kw] + b3[:, 2 * kw:])
    for ch in range(tl // c):
        rows = slice(ch * c, (ch + 1) * c)
        in_grp = (ch * c) % GLA_CUM_ROWS
        b = b_groups[(ch * c) // GLA_CUM_ROWS][in_grp:in_grp + c]
        b_last = b[c - 1:c]
        qf = qk_ref[rows, :kw].astype(F32)
        kf = qk_ref[rows, kw:].astype(F32)
        q_e = (qf * jnp.exp(b) * (GLA_DK ** -0.5)).astype(BF16)
        k_e = (kf * jnp.exp(-b)).astype(BF16)
        k_t = (kf * jnp.exp(b_last - b)).astype(BF16)
        decay = jnp.exp(b_last)
        for h in range(GLA_HEADS):
            ps = slice((h // 2) * LANES, (h // 2 + 1) * LANES)
            mine = low_half if h % 2 == 0 else ~low_half
            qp, kp = q_e[:, ps], k_e[:, ps]
            qh = jnp.where(mine, qp, jnp.zeros_like(qp))
            kth = jnp.where(mine, k_t[:, ps], jnp.zeros_like(qp))
            vs = slice(h * GLA_DV, (h + 1) * GLA_DV)
            vh = vr_ref[rows, vs]
            a = pl.dot(qh, kp, trans_b=True)
            a = jnp.where(causal, a, 0.0).astype(BF16)
            st = st_ref[h]
            o = (jnp.dot(a, vh, preferred_element_type=F32)
                 + pl.dot(qh, st.astype(BF16), trans_b=True))
            st_ref[h] = st * decay[:, ps] + pl.dot(vh, kth, trans_a=True)
            rh = vr_ref[rows, slice(vw + h * GLA_DV, vw + (h + 1) * GLA_DV)].astype(F32)
            o = _rms(o, g) * (rh * jax.nn.sigmoid(rh))
            o_ref[rows, vs] = o.astype(BF16)


def _gla(qk, vr, la, g, batch, seq):
    tl = TL_GLA
    nl = seq // tl
    row = lambda w: pl.BlockSpec((tl, w), lambda b, i: (b * nl + i, 0))
    r = np.arange(GLA_CUM_ROWS)
    cum = jnp.asarray((r[None, :] <= r[:, None])
                      & (r[None, :] // GLA_CHUNK == r[:, None] // GLA_CHUNK), BF16)
    return pl.pallas_call(
        _gla_kernel,
        grid=(batch, nl),
        in_specs=[row(512), row(1024), row(256),
                  pl.BlockSpec((1, GLA_DV), lambda b, i: (0, 0)),
                  pl.BlockSpec((GLA_CUM_ROWS, GLA_CUM_ROWS), lambda b, i: (0, 0))],
        out_specs=row(512),
        out_shape=jax.ShapeDtypeStruct((batch * seq, 512), BF16),
        scratch_shapes=[pltpu.VMEM((GLA_HEADS, GLA_DV, LANES), F32)],
        compiler_params=pltpu.CompilerParams(
            dimension_semantics=("arbitrary", "arbitrary"), vmem_limit_bytes=VMEM_LIMIT),
        name="gla",
    )(qk, vr, la, g, cum)


def _swa_kernel(sink_ref, bias0_ref, bias_ref, q_ref, kvc_ref, kvp_ref, g_ref, o_ref):
    w = SWA_WINDOW
    scale = jnp.asarray(SWA_HEAD_DIM ** -0.5, BF16)
    lane_lo = lax.broadcasted_iota(jnp.int32, (2 * w, LANES), 1) < SWA_HEAD_DIM
    hi_half = lax.broadcasted_iota(jnp.int32, (LANES, LANES), 0) // SWA_HEAD_DIM
    hj_half = lax.broadcasted_iota(jnp.int32, (LANES, LANES), 1) // SWA_HEAD_DIM
    mean_mat = jnp.where(hi_half == hj_half, 1.0 / SWA_HEAD_DIM, 0.0).astype(F32)
    g = g_ref[...]

    for sb in range(SWA_SUB):
        rows = slice(sb * w, (sb + 1) * w)
        if sb == 0:
            kv = jnp.concatenate([kvp_ref[...], kvc_ref[0:w, :]], axis=0)
            bias = bias0_ref
        else:
            kv = kvc_ref[(sb - 1) * w:(sb + 1) * w, :]
            bias = bias_ref
        k = kv[:, 0:LANES] * scale
        ks = kv[:, LANES:2 * LANES] * scale
        v, vs = kv[:, 2 * LANES:3 * LANES], kv[:, 3 * LANES:]
        zero = jnp.zeros_like(k)
        k_low = [jnp.where(lane_lo, k, zero), jnp.where(lane_lo, ks, zero)]
        k_high = [jnp.where(lane_lo, zero, ks), jnp.where(lane_lo, zero, k)]
        v_low = [jnp.where(lane_lo, v, zero), jnp.where(lane_lo, vs, zero)]
        v_high = [jnp.where(lane_lo, zero, vs), jnp.where(lane_lo, zero, v)]

        def probs(s, head):
            s = s + bias[0, head]
            sink = sink_ref[head]
            m = jnp.maximum(jnp.max(s, axis=-1, keepdims=True), sink)
            e = jnp.exp(s - m)
            den = jnp.sum(e, axis=-1, keepdims=True) + jnp.exp(sink - m)
            return (e * (1.0 / den)).astype(BF16)

        for pair in range(SWA_HEADS // 2):
            j = (2 * pair) // (SWA_HEADS // SWA_KV_HEADS)
            cols = slice(pair * LANES, (pair + 1) * LANES)
            qp = q_ref[rows, cols]
            p0 = probs(pl.dot(qp, k_low[j], trans_b=True), 2 * pair)
            p1 = probs(pl.dot(qp, k_high[j], trans_b=True), 2 * pair + 1)
            o = (jnp.dot(p0, v_low[j], preferred_element_type=F32)
                 + jnp.dot(p1, v_high[j], preferred_element_type=F32))
            ms = jnp.dot(o * o, mean_mat, preferred_element_type=F32, precision=HIGHEST)
            o = o * lax.rsqrt(ms + NORM_EPS) * g
            o_ref[rows, cols] = o.astype(BF16)


def _swa_bias():
    w = SWA_WINDOW
    slopes = 2.0 ** (-8.0 * np.arange(1, SWA_HEADS + 1, dtype=np.float64) / SWA_HEADS)
    rel = np.arange(w)[:, None] + w - np.arange(2 * w)[None, :]
    in_window = (rel >= 0) & (rel < w)
    exists = np.stack([np.arange(2 * w) >= w, np.ones(2 * w, bool)])
    valid = in_window[None] & exists[:, None, :]
    bias = -slopes[None, :, None, None] * rel[None, None].astype(np.float64)
    return np.where(valid[:, None], bias, NEG_BIG).astype(np.float32)


def _swa(qb, kv, sinks, g2, batch, seq):
    w = SWA_WINDOW
    rows = SWA_SUB * w
    nb = seq // rows
    cur = lambda: pl.BlockSpec((rows, 512), lambda b, n: (b * nb + n, 0))
    prev = pl.BlockSpec(
        (w, 512), lambda b, n: (jnp.maximum((b * nb + n) * SWA_SUB - 1, b * nb * SWA_SUB), 0))
    table = lambda index: pl.BlockSpec((1, SWA_HEADS, w, 2 * w), index)
    bias = jnp.asarray(_swa_bias())
    return pl.pallas_call(
        _swa_kernel,
        grid=(batch, nb),
        in_specs=[pl.BlockSpec(memory_space=pltpu.SMEM),
                  table(lambda b, n: (jnp.minimum(n, 1), 0, 0, 0)),
                  table(lambda b, n: (1, 0, 0, 0)),
                  cur(), cur(), prev,
                  pl.BlockSpec((1, LANES), lambda b, n: (0, 0))],
        out_specs=cur(),
        out_shape=jax.ShapeDtypeStruct((batch * seq, 512), BF16),
        compiler_params=pltpu.CompilerParams(
            dimension_semantics=("arbitrary", "arbitrary"), vmem_limit_bytes=VMEM_LIMIT),
        name="swa",
    )(sinks, bias, bias, qb, kv, kv, g2)


def _out_route_kernel(oa_ref, ob_ref, x_ref, woa_ref, wob_ref, g_ref, wrh_ref, wrl_ref, br_ref,
                      earlier_ref, h1_ref, hn_ref, route_ref, cnt_ref):
    @pl.when(pl.program_id(0) == 0)
    def _():
        cnt_ref[...] = jnp.zeros_like(cnt_ref)

    tm = TM_ROUTE
    h1 = (x_ref[...]
          + jnp.dot(oa_ref[...], woa_ref[...], preferred_element_type=F32)
          + jnp.dot(ob_ref[...], wob_ref[...], preferred_element_type=F32))
    h1_ref[...] = h1
    hn = _rms(h1, g_ref[...])
    hn_ref[...] = hn
    hn_hi = hn.astype(BF16)
    hn_lo = (hn - hn_hi.astype(F32)).astype(BF16)
    logits = (pl.dot(wrh_ref[...], hn_hi, trans_b=True)
              + pl.dot(wrh_ref[...], hn_lo, trans_b=True)
              + pl.dot(wrl_ref[...], hn_hi, trans_b=True)) + br_ref[...]

    eid = lax.broadcasted_iota(jnp.int32, (N_EXPERTS, tm), 0)
    work = logits
    vals, idxs, sels = [], [], []
    for _ in range(TOP_K):
        m = jnp.max(work, axis=0, keepdims=True)
        idx = jnp.min(jnp.where(work == m, eid, N_EXPERTS), axis=0, keepdims=True)
        sel = eid == idx
        vals.append(m)
        idxs.append(idx)
        sels.append(sel)
        work = jnp.where(sel, -3e38, work)
    exps = [jnp.exp(v - vals[0]) for v in vals]
    inv_den = 1.0 / (exps[0] + exps[1] + exps[2] + exps[3])

    multihot = jnp.where(sels[0] | sels[1] | sels[2] | sels[3], 1.0, 0.0)
    before = (jnp.dot(multihot.astype(BF16), earlier_ref[...], preferred_element_type=F32)
              + cnt_ref[:, 0:1])
    ranks = [jnp.sum(jnp.where(s, before, 0.0), axis=0, keepdims=True) for s in sels]
    route_ref[...] = jnp.concatenate(
        [i.astype(F32) for i in idxs] + [e * inv_den for e in exps] + ranks
        + [jnp.zeros((ROUTE_ROWS - 3 * TOP_K, tm), F32)], axis=0)
    cnt_ref[...] += jnp.sum(multihot, axis=1, keepdims=True)


def _out_route(oa, ob, x2, woa, wob, g2, wr_hi, wr_lo, br):
    T = x2.shape[0]
    tm = TM_ROUTE
    row = lambda w: pl.BlockSpec((tm, w), lambda i: (i, 0))
    full = lambda a: pl.BlockSpec(a.shape, lambda i: (0,) * a.ndim)
    earlier = jnp.asarray(np.triu(np.ones((tm, tm), np.float32), k=1), BF16)
    return pl.pallas_call(
        _out_route_kernel,
        grid=(T // tm,),
        in_specs=[row(512), row(512), row(D_MODEL), full(woa), full(wob), full(g2),
                  full(wr_hi), full(wr_lo), full(br), full(earlier)],
        out_specs=[row(D_MODEL), row(D_MODEL),
                   pl.BlockSpec((ROUTE_ROWS, tm), lambda i: (0, i)),
                   pl.BlockSpec((N_EXPERTS, LANES), lambda i: (0, 0))],
        out_shape=[jax.ShapeDtypeStruct((T, D_MODEL), F32),
                   jax.ShapeDtypeStruct((T, D_MODEL), F32),
                   jax.ShapeDtypeStruct((ROUTE_ROWS, T), F32),
                   jax.ShapeDtypeStruct((N_EXPERTS, LANES), F32)],
        compiler_params=pltpu.CompilerParams(
            dimension_semantics=("arbitrary",), vmem_limit_bytes=VMEM_LIMIT),
        name="out_route",
    )(oa, ob, x2, woa, wob, g2, wr_hi, wr_lo, br, earlier)


def _row_copy(src, s, dst, d, sem):
    return pltpu.make_async_copy(src.at[pl.ds(s, 1)], dst.at[pl.ds(d, 1)], sem)


def _dispatch_kernel(dest_ref, pad_start_ref, pad_cnt_ref, nused_ref, hn_ref, xs_ref,
                     zero_ref, sem, zsem):
    i = pl.program_id(0)
    n = pl.num_programs(0)
    tm = TM_EXPERT
    base = i * (TM_DISPATCH * TOP_K)
    for p in range(TM_DISPATCH * TOP_K):
        _row_copy(hn_ref, p // TOP_K, xs_ref, dest_ref[base + p], sem).start(priority=p % 2)
    for _ in range(TOP_K):
        pltpu.make_async_copy(hn_ref, xs_ref.at[pl.ds(0, TM_DISPATCH)], sem).wait()

    @pl.when(i == n - 1)
    def _():
        zero_ref[...] = jnp.zeros_like(zero_ref)
        sub = 8
        for e in range(N_EXPERTS):
            start, cnt = pad_start_ref[e], pad_cnt_ref[e]
            head = jnp.minimum((-start) & (sub - 1), cnt)
            body0 = start + head
            nbody = (cnt - head) // sub

            def head_copy(r):
                return _row_copy(zero_ref, 0, xs_ref, start + r, zsem)

            def body_copy(c):
                at = pl.multiple_of(body0 + c * sub, sub)
                return pltpu.make_async_copy(zero_ref.at[pl.ds(0, sub)],
                                             xs_ref.at[pl.ds(at, sub)], zsem)

            for copy, count in ((head_copy, head), (body_copy, nbody)):
                def zissue(r, carry, copy=copy):
                    copy(r).start()
                    return carry

                def zwait(r, carry, copy=copy):
                    copy(r).wait()
                    return carry

                lax.fori_loop(0, count, zissue, 0)
                lax.fori_loop(0, count, zwait, 0)

        def tail_copy(t):
            return pltpu.make_async_copy(zero_ref, xs_ref.at[pl.ds(t * tm, tm)], zsem)

        def tissue(t, carry):
            tail_copy(t).start()
            return carry

        def twait(t, carry):
            tail_copy(t).wait()
            return carry

        n_tiles = xs_ref.shape[0] // tm
        lax.fori_loop(nused_ref[0], n_tiles, tissue, 0)
        lax.fori_loop(nused_ref[0], n_tiles, twait, 0)


def _dispatch(dest, pad_start, pad_cnt, n_used, hn, n_rows):
    T = hn.shape[0]
    any_spec = pl.BlockSpec(memory_space=pl.ANY)
    return pl.pallas_call(
        _dispatch_kernel,
        grid_spec=pltpu.PrefetchScalarGridSpec(
            num_scalar_prefetch=4,
            grid=(T // TM_DISPATCH,),
            in_specs=[pl.BlockSpec((TM_DISPATCH, D_MODEL), lambda i, *_: (i, 0))],
            out_specs=any_spec,
            scratch_shapes=[pltpu.VMEM((TM_EXPERT, D_MODEL), F32),
                            pltpu.SemaphoreType.DMA(()),
                            pltpu.SemaphoreType.DMA(())]),
        out_shape=jax.ShapeDtypeStruct((n_rows, D_MODEL), F32),
        compiler_params=pltpu.CompilerParams(
            dimension_semantics=("arbitrary",), has_side_effects=True,
            vmem_limit_bytes=VMEM_LIMIT),
        name="dispatch",
    )(dest, pad_start, pad_cnt, n_used, hn)


def _experts_kernel(tile0_ref, ntile_ref, xs_ref, wg_ref, bg_ref, wl_ref, bl_ref, wd_ref, bd_ref,
                    ys_ref, wg_bf, wl_bf, wd_bf, xbuf, ybuf, xsem, ysem):
    e = pl.program_id(0)
    tm = TM_EXPERT
    t0, nt = tile0_ref[e], ntile_ref[e]

    def x_load(tile, slot):
        return pltpu.make_async_copy(xs_ref.at[pl.ds(tile * tm, tm)], xbuf.at[slot], xsem.at[slot])

    def y_store(tile, slot):
        return pltpu.make_async_copy(ybuf.at[slot], ys_ref.at[pl.ds(tile * tm, tm)], ysem.at[slot])

    @pl.when((e == 0) & (nt > 0))
    def _():
        x_load(t0, 0).start()

    @pl.when(nt > 0)
    def _():
        wg_bf[...] = wg_ref[0].astype(BF16)
        wl_bf[...] = wl_ref[0].astype(BF16)
        wd_bf[...] = wd_ref[0].astype(BF16)

    def tile_body(j, carry):
        slot = j % 2

        @pl.when(j + 1 < nt)
        def _():
            x_load(t0 + j + 1, 1 - slot).start()

        x_load(t0 + j, slot).wait()

        @pl.when(j >= 2)
        def _():
            y_store(t0 + j - 2, slot).wait()

        x = xbuf[slot].astype(BF16)
        glu = jnp.minimum(jnp.dot(x, wg_bf[...], preferred_element_type=F32) + bg_ref[0],
                          SWIGLU_LIMIT)
        lin = jnp.clip(jnp.dot(x, wl_bf[...], preferred_element_type=F32) + bl_ref[0],
                       -SWIGLU_LIMIT, SWIGLU_LIMIT)
        hid = glu * jax.nn.sigmoid(SWIGLU_ALPHA * glu) * (lin + 1.0)
        ybuf[slot] = jnp.dot(hid.astype(BF16), wd_bf[...], preferred_element_type=F32) + bd_ref[0]
        y_store(t0 + j, slot).start()
        return carry

    lax.fori_loop(0, nt, tile_body, 0)

    @pl.when(e + 1 < pl.num_programs(0))
    def _():
        @pl.when(ntile_ref[e + 1] > 0)
        def _():
            x_load(tile0_ref[e + 1], 0).start()

    @pl.when(nt >= 2)
    def _():
        y_store(t0 + nt - 2, nt % 2).wait()

    @pl.when(nt >= 1)
    def _():
        y_store(t0 + nt - 1, (nt - 1) % 2).wait()

    @pl.when(e == pl.num_programs(0) - 1)
    def _():
        ybuf[0] = jnp.zeros((tm, D_MODEL), F32)
        n_tiles = ys_ref.shape[0] // tm

        def tail_issue(t, carry):
            y_store(t, 0).start()
            return carry

        def tail_wait(t, carry):
            y_store(t, 0).wait()
            return carry

        lax.fori_loop(t0 + nt, n_tiles, tail_issue, 0)
        lax.fori_loop(t0 + nt, n_tiles, tail_wait, 0)


def _experts(tile0, ntile, xs, w_glu, b_glu, w_lin, b_lin, w_down, b_down):
    tm = TM_EXPERT
    any_spec = pl.BlockSpec(memory_space=pl.ANY)
    wspec = lambda: pl.BlockSpec((1, D_MODEL, D_FF), lambda e, *_: (e, 0, 0))
    bspec = lambda: pl.BlockSpec((1, 1, D_FF), lambda e, *_: (e, 0, 0))
    return pl.pallas_call(
        _experts_kernel,
        grid_spec=pltpu.PrefetchScalarGridSpec(
            num_scalar_prefetch=2,
            grid=(N_EXPERTS,),
            in_specs=[any_spec, wspec(), bspec(), wspec(), bspec(), wspec(), bspec()],
            out_specs=any_spec,
            scratch_shapes=[pltpu.VMEM((D_MODEL, D_FF), BF16)] * 3
            + [pltpu.VMEM((2, tm, D_MODEL), F32), pltpu.VMEM((2, tm, D_MODEL), F32),
               pltpu.SemaphoreType.DMA((2,)), pltpu.SemaphoreType.DMA((2,))]),
        out_shape=jax.ShapeDtypeStruct(xs.shape, F32),
        compiler_params=pltpu.CompilerParams(
            dimension_semantics=("arbitrary",), vmem_limit_bytes=VMEM_LIMIT,
            has_side_effects=True),
        name="experts",
    )(tile0, ntile, xs, w_glu, b_glu.reshape(N_EXPERTS, 1, D_FF),
      w_lin, b_lin.reshape(N_EXPERTS, 1, D_FF), w_down, b_down.reshape(N_EXPERTS, 1, D_MODEL))


def _combine_kernel(dest_ref, ys_ref, h1_ref, gate_ref, g_ref, o_ref, buf, sems):
    s = pl.program_id(0)
    n = pl.num_programs(0) - 1
    tm = TM_COMBINE

    def issue(slot):
        base = s * (tm * TOP_K)
        for p in range(tm * TOP_K):
            r, k = divmod(p, TOP_K)
            pltpu.make_async_copy(ys_ref.at[pl.ds(dest_ref[base + p], 1)],
                                  buf.at[slot, k, pl.ds(r, 1)],
                                  sems.at[slot]).start(priority=p % 2)

    for parity in range(2):
        @pl.when((s < n) & (s % 2 == parity))
        def _():
            issue(parity)

    @pl.when(s > 0)
    def _():
        slot = (s - 1) % 2
        for k in range(TOP_K):
            pltpu.make_async_copy(ys_ref.at[pl.ds(0, tm)], buf.at[slot, k], sems.at[slot]).wait()
        acc = h1_ref[...]
        for k in range(TOP_K):
            acc = acc + gate_ref[:, k:k + 1] * buf[slot, k]
        o_ref[...] = _rms(acc, g_ref[...])


def _combine(dest, ys, h1, gate, fg):
    T = h1.shape[0]
    tm = TM_COMBINE
    return pl.pallas_call(
        _combine_kernel,
        grid_spec=pltpu.PrefetchScalarGridSpec(
            num_scalar_prefetch=1,
            grid=(T // tm + 1,),
            in_specs=[pl.BlockSpec(memory_space=pl.ANY),
                      pl.BlockSpec((tm, D_MODEL), lambda s, d: (jnp.maximum(s - 1, 0), 0)),
                      pl.BlockSpec((tm, TOP_K), lambda s, d: (jnp.maximum(s - 1, 0), 0)),
                      pl.BlockSpec((1, D_MODEL), lambda s, d: (0, 0))],
            out_specs=pl.BlockSpec((tm, D_MODEL), lambda s, d: (jnp.maximum(s - 1, 0), 0)),
            scratch_shapes=[pltpu.VMEM((2, TOP_K, tm, D_MODEL), F32),
                            pltpu.SemaphoreType.DMA((2,))]),
        out_shape=jax.ShapeDtypeStruct((T, D_MODEL), F32),
        compiler_params=pltpu.CompilerParams(
            dimension_semantics=("arbitrary",), vmem_limit_bytes=VMEM_LIMIT),
        name="combine",
    )(dest, ys, h1, gate, fg)


def _swap_halves(w):
    h = w.shape[-1] // 2
    return jnp.concatenate([w[..., h:], w[..., :h]], axis=-1)


def _layer(x2, batch, seq, norm1_g, w_in, w_alpha_up, b_alpha, gla_norm_g, swa_sinks, swa_norm_g,
           w_out, norm2_g, w_router, b_router, w_glu, b_glu, w_lin, b_lin, w_down, b_down):
    T = x2.shape[0]
    kb_w, vb_w = w_in[:, 2064:2192], w_in[:, 2192:2320]
    w_z = w_in[:, 1536:1552]
    w_cat = jnp.concatenate([
        w_in[:, 0:1536], w_in[:, 1552:2064],
        kb_w, _swap_halves(kb_w), vb_w, _swap_halves(vb_w),
        jnp.pad(jnp.tile(w_z, (1, Z_PIECES)), [(0, 0), (0, LANES - Z_PIECES * GLA_RANK)]),
    ], axis=1).astype(BF16)
    wup_hi = w_alpha_up.astype(BF16)
    wup_lo = (w_alpha_up - wup_hi.astype(F32)).astype(BF16)
    wup_cat = jnp.pad(jnp.concatenate([wup_hi, wup_hi, wup_hi, wup_lo, wup_lo], axis=0),
                      [(0, LANES - Z_PIECES * GLA_RANK), (0, 0)])

    qk, vr, la, qb, kv = _in_proj(
        x2, norm1_g.reshape(1, -1), w_cat, wup_cat, b_alpha.reshape(1, -1))
    oa = _gla(qk, vr, la, gla_norm_g.reshape(1, -1), batch, seq)
    ob = _swa(qb, kv, swa_sinks, jnp.tile(swa_norm_g, 2).reshape(1, -1), batch, seq)

    wo = w_out.astype(BF16)
    wr_t = w_router.T
    wr_hi = wr_t.astype(BF16)
    wr_lo = (wr_t - wr_hi.astype(F32)).astype(BF16)
    h1, hn, route, cnt = _out_route(oa, ob, x2, wo[:512], wo[512:], norm2_g.reshape(1, -1),
                                    wr_hi, wr_lo, b_router.reshape(-1, 1))

    tm = TM_EXPERT
    n_tiles = T * TOP_K // tm + N_EXPERTS
    counts = cnt[:, 0].astype(jnp.int32)
    padded = (counts + tm - 1) // tm * tm
    pends = jnp.cumsum(padded)
    pstarts = pends - padded
    top_idx = route[0:TOP_K].T.astype(jnp.int32)
    gate = route[TOP_K:2 * TOP_K].T
    rank = route[2 * TOP_K:3 * TOP_K].T.astype(jnp.int32)
    seg_start = jnp.sum(jnp.where(top_idx[..., None] == jnp.arange(N_EXPERTS), pstarts, 0), axis=-1)
    dest = (seg_start + rank).reshape(-1)
    n_used = (pends[-1] // tm).reshape(1)
    xs = _dispatch(dest, pstarts + counts, padded - counts, n_used, hn, n_tiles * tm)
    ys = _experts(pstarts // tm, padded // tm, xs, w_glu, b_glu, w_lin, b_lin, w_down, b_down)
    return dest, ys, h1, gate


def kernel(x, norm1_g, w_in, w_alpha_up, b_alpha, gla_norm_g, swa_sinks, swa_norm_g, w_out,
           norm2_g, w_router, b_router, w_glu, b_glu, w_lin, b_lin, w_down, b_down, final_g):
    batch, seq, d = x.shape
    assert norm1_g.shape[0] == 1, "single-layer problem"
    x2 = x.reshape(batch * seq, d)
    dest, ys, h1, gate = _layer(
        x2, batch, seq, norm1_g[0], w_in[0], w_alpha_up[0], b_alpha[0], gla_norm_g[0],
        swa_sinks[0], swa_norm_g[0], w_out[0], norm2_g[0], w_router[0], b_router[0],
        w_glu[0], b_glu[0], w_lin[0], b_lin[0], w_down[0], b_down[0])
    out = _combine(dest, ys, h1, gate, final_g.reshape(1, -1))
    return out.reshape(batch, seq, d)
```

```python
import numpy as np
import jax
import jax.numpy as jnp
from jax import lax
from jax.experimental import pallas as pl
from jax.experimental.pallas import tpu as pltpu

F32 = jnp.float32
BF16 = jnp.bfloat16
HIGHEST = lax.Precision.HIGHEST

D_MODEL = 1024
GLA_HEADS = 4
GLA_DK = 64
GLA_DV = 128
GLA_RANK = 16
GLA_GATE_TAU = 16.0
GLA_CHUNK = 64
SWA_HEADS = 8
SWA_KV_HEADS = 2
SWA_HEAD_DIM = 64
SWA_WINDOW = 128
N_EXPERTS = 32
TOP_K = 4
D_FF = 1024
SWIGLU_LIMIT = 7.0
SWIGLU_ALPHA = 1.702
NORM_EPS = 1e-5

LANES = 128
SLAB_SUBLANES = D_MODEL // LANES
VMEM_LIMIT = 56 * 1024 * 1024

TM_PROJ = 512
TL_GLA = 1024
GLA_CUM_ROWS = 256
TM_ROUTE = 512
TM_EXPERT = 512
TM_COMBINE = 256
SWA_SUB = 4
TM_DISPATCH = 256

NEG_BIG = -1e30
ROUTE_ROWS = 16

C_QA, C_KA, C_VA, C_RA, C_QB, C_KB, C_KBS, C_VB, C_VBS, C_Z, C_END = (
    0, 256, 512, 1024, 1536, 2048, 2176, 2304, 2432, 2560, 2688)
Z_PIECES = 5


def _split3(x):
    hi = x.astype(BF16).astype(F32)
    r = x - hi
    mid = r.astype(BF16).astype(F32)
    lo = (r - mid).astype(BF16).astype(F32)
    return hi, mid, lo


def _rms(x, g):
    return x * lax.rsqrt(jnp.mean(x * x, axis=-1, keepdims=True) + NORM_EPS) * g


def _in_proj_kernel(x_ref, g_ref, w_ref, wup_ref, ba_ref,
                    qk_ref, vr_ref, la_ref, qb_ref, kv_ref):
    u = _rms(x_ref[...], g_ref[...]).astype(BF16)

    def proj(c0, c1):
        return jnp.dot(u, w_ref[:, c0:c1], preferred_element_type=F32)

    qk_ref[...] = proj(C_QA, C_VA).astype(BF16)
    vr_ref[:, :C_RA - C_VA] = proj(C_VA, C_RA).astype(BF16)
    vr_ref[:, C_RA - C_VA:] = proj(C_RA, C_QB).astype(BF16)
    qb_ref[...] = proj(C_QB, C_KB).astype(BF16)
    kv_ref[...] = proj(C_KB, C_Z).astype(BF16)
    z = proj(C_Z, C_END)
    hi, mid, lo = _split3(z)
    piece = lax.broadcasted_iota(jnp.int32, z.shape, 1) // GLA_RANK
    zc = jnp.where((piece == 0) | (piece == 3), hi, jnp.where(piece == 2, lo, mid)).astype(BF16)
    y = jnp.dot(zc, wup_ref[...], preferred_element_type=F32) + ba_ref[...]
    log_sig = jnp.minimum(y, 0.0) - jnp.log1p(jnp.exp(-jnp.abs(y)))
    la_ref[...] = log_sig * (1.0 / GLA_GATE_TAU)


def _in_proj(x2, g1, w_cat, wup_p, ba_p):
    T = x2.shape[0]
    tm = TM_PROJ
    row = lambda w: pl.BlockSpec((tm, w), lambda i: (i, 0))
    full = lambda a: pl.BlockSpec(a.shape, lambda i: (0,) * a.ndim)
    outs = [(512, BF16), (1024, BF16), (256, F32), (512, BF16), (512, BF16)]
    return pl.pallas_call(
        _in_proj_kernel,
        grid=(T // tm,),
        in_specs=[row(D_MODEL), full(g1), full(w_cat), full(wup_p), full(ba_p)],
        out_specs=[row(w) for w, _ in outs],
        out_shape=[jax.ShapeDtypeStruct((T, w), dt) for w, dt in outs],
        compiler_params=pltpu.CompilerParams(
            dimension_semantics=("arbitrary",), vmem_limit_bytes=VMEM_LIMIT),
        name="in_proj",
    )(x2, g1, w_cat, wup_p, ba_p)


def _gla_kernel(qk_ref, vr_ref, la_ref, g_ref, cum_ref, o_ref, st_ref):
    @pl.when(pl.program_id(1) == 0)
    def _():
        st_ref[...] = jnp.zeros_like(st_ref)

    tl = TL_GLA
    c = GLA_CHUNK
    kw = GLA_HEADS * GLA_DK
    vw = GLA_HEADS * GLA_DV
    causal = (lax.broadcasted_iota(jnp.int32, (c, c), 0)
              >= lax.broadcasted_iota(jnp.int32, (c, c), 1))
    low_half = lax.broadcasted_iota(jnp.int32, (c, LANES), 1) < GLA_DK
    g = g_ref[...]
    b_groups = []
    for grp in range(tl // GLA_CUM_ROWS):
        la = la_ref[grp * GLA_CUM_ROWS:(grp + 1) * GLA_CUM_ROWS, :]
        pieces = jnp.concatenate([p.astype(BF16) for p in _split3(la)], axis=1)
        b3 = jnp.dot(cum_ref[...], pieces, preferred_element_type=F32)
        b_groups.append(b3[:, :kw] + b3[:, kw:2 * kw] + b3[:, 2 * kw:])
    for ch in range(tl // c):
        rows = slice(ch * c, (ch + 1) * c)
        in_grp = (ch * c) % GLA_CUM_ROWS
        b = b_groups[(ch * c) // GLA_CUM_ROWS][in_grp:in_grp + c]
        b_last = b[c - 1:c]
        qf = qk_ref[rows, :kw].astype(F32)
        kf = qk_ref[rows, kw:].astype(F32)
        q_e = (qf * jnp.exp(b) * (GLA_DK ** -0.5)).astype(BF16)
        k_e = (kf * jnp.exp(-b)).astype(BF16)
        k_t = (kf * jnp.exp(b_last - b)).astype(BF16)
        decay = jnp.exp(b_last)
        for h in range(GLA_HEADS):
            ps = slice((h // 2) * LANES, (h // 2 + 1) * LANES)
            mine = low_half if h % 2 == 0 else ~low_half
            qp, kp = q_e[:, ps], k_e[:, ps]
            qh = jnp.where(mine, qp, jnp.zeros_like(qp))
            kth = jnp.where(mine, k_t[:, ps], jnp.zeros_like(qp))
            vs = slice(h * GLA_DV, (h + 1) * GLA_DV)
            vh = vr_ref[rows, vs]
            a = pl.dot(qh, kp, trans_b=True)
            a = jnp.where(causal, a, 0.0).astype(BF16)
            st = st_ref[h]
            o = (jnp.dot(a, vh, preferred_element_type=F32)
                 + pl.dot(qh, st.astype(BF16), trans_b=True))
            st_ref[h] = st * decay[:, ps] + pl.dot(vh, kth, trans_a=True)
            rh = vr_ref[rows, slice(vw + h * GLA_DV, vw + (h + 1) * GLA_DV)].astype(F32)
            o = _rms(o, g) * (rh * jax.nn.sigmoid(rh))
            o_ref[rows, vs] = o.astype(BF16)


def _gla(qk, vr, la, g, batch, seq):
    tl = TL_GLA
    nl = seq // tl
    row = lambda w: pl.BlockSpec((tl, w), lambda b, i: (b * nl + i, 0))
    r = np.arange(GLA_CUM_ROWS)
    cum = jnp.asarray((r[None, :] <= r[:, None])
                      & (r[None, :] // GLA_CHUNK == r[:, None] // GLA_CHUNK), BF16)
    return pl.pallas_call(
        _gla_kernel,
        grid=(batch, nl),
        in_specs=[row(512), row(1024), row(256),
                  pl.BlockSpec((1, GLA_DV), lambda b, i: (0, 0)),
                  pl.BlockSpec((GLA_CUM_ROWS, GLA_CUM_ROWS), lambda b, i: (0, 0))],
        out_specs=row(512),
        out_shape=jax.ShapeDtypeStruct((batch * seq, 512), BF16),
        scratch_shapes=[pltpu.VMEM((GLA_HEADS, GLA_DV, LANES), F32)],
        compiler_params=pltpu.CompilerParams(
            dimension_semantics=("arbitrary", "arbitrary"), vmem_limit_bytes=VMEM_LIMIT),
        name="gla",
    )(qk, vr, la, g, cum)


def _swa_kernel(sink_ref, bias0_ref, bias_ref, q_ref, kvc_ref, kvp_ref, g_ref, o_ref):
    w = SWA_WINDOW
    scale = jnp.asarray(SWA_HEAD_DIM ** -0.5, BF16)
    lane_lo = lax.broadcasted_iota(jnp.int32, (2 * w, LANES), 1) < SWA_HEAD_DIM
    hi_half = lax.broadcasted_iota(jnp.int32, (LANES, LANES), 0) // SWA_HEAD_DIM
    hj_half = lax.broadcasted_iota(jnp.int32, (LANES, LANES), 1) // SWA_HEAD_DIM
    mean_mat = jnp.where(hi_half == hj_half, 1.0 / SWA_HEAD_DIM, 0.0).astype(F32)
    g = g_ref[...]

    for sb in range(SWA_SUB):
        rows = slice(sb * w, (sb + 1) * w)
        if sb == 0:
            kv = jnp.concatenate([kvp_ref[...], kvc_ref[0:w, :]], axis=0)
            bias = bias0_ref
        else:
            kv = kvc_ref[(sb - 1) * w:(sb + 1) * w, :]
            bias = bias_ref
        k = kv[:, 0:LANES] * scale
        ks = kv[:, LANES:2 * LANES] * scale
        v, vs = kv[:, 2 * LANES:3 * LANES], kv[:, 3 * LANES:]
        zero = jnp.zeros_like(k)
        k_low = [jnp.where(lane_lo, k, zero), jnp.where(lane_lo, ks, zero)]
        k_high = [jnp.where(lane_lo, zero, ks), jnp.where(lane_lo, zero, k)]
        v_low = [jnp.where(lane_lo, v, zero), jnp.where(lane_lo, vs, zero)]
        v_high = [jnp.where(lane_lo, zero, vs), jnp.where(lane_lo, zero, v)]

        def probs(s, head):
            s = s + bias[0, head]
            sink = sink_ref[head]
            m = jnp.maximum(jnp.max(s, axis=-1, keepdims=True), sink)
            e = jnp.exp(s - m)
            den = jnp.sum(e, axis=-1, keepdims=True) + jnp.exp(sink - m)
            return (e * (1.0 / den)).astype(BF16)

        for pair in range(SWA_HEADS // 2):
            j = (2 * pair) // (SWA_HEADS // SWA_KV_HEADS)
            cols = slice(pair * LANES, (pair + 1) * LANES)
            qp = q_ref[rows, cols]
            p0 = probs(pl.dot(qp, k_low[j], trans_b=True), 2 * pair)
            p1 = probs(pl.dot(qp, k_high[j], trans_b=True), 2 * pair + 1)
            o = (jnp.dot(p0, v_low[j], preferred_element_type=F32)
                 + jnp.dot(p1, v_high[j], preferred_element_type=F32))
            ms = jnp.dot(o * o, mean_mat, preferred_element_type=F32, precision=HIGHEST)
            o = o * lax.rsqrt(ms + NORM_EPS) * g
            o_ref[rows, cols] = o.astype(BF16)


def _swa_bias():
    w = SWA_WINDOW
    slopes = 2.0 ** (-8.0 * np.arange(1, SWA_HEADS + 1, dtype=np.float64) / SWA_HEADS)
    rel = np.arange(w)[:, None] + w - np.arange(2 * w)[None, :]
    in_window = (rel >= 0) & (rel < w)
    exists = np.stack([np.arange(2 * w) >= w, np.ones(2 * w, bool)])
    valid = in_window[None] & exists[:, None, :]
    bias = -slopes[None, :, None, None] * rel[None, None].astype(np.float64)
    return np.where(valid[:, None], bias, NEG_BIG).astype(np.float32)


def _swa(qb, kv, sinks, g2, batch, seq):
    w = SWA_WINDOW
    rows = SWA_SUB * w
    nb = seq // rows
    cur = lambda: pl.BlockSpec((rows, 512), lambda b, n: (b * nb + n, 0))
    prev = pl.BlockSpec(
        (w, 512), lambda b, n: (jnp.maximum((b * nb + n) * SWA_SUB - 1, b * nb * SWA_SUB), 0))
    table = lambda index: pl.BlockSpec((1, SWA_HEADS, w, 2 * w), index)
    bias = jnp.asarray(_swa_bias())
    return pl.pallas_call(
        _swa_kernel,
        grid=(batch, nb),
        in_specs=[pl.BlockSpec(memory_space=pltpu.SMEM),
                  table(lambda b, n: (jnp.minimum(n, 1), 0, 0, 0)),
                  table(lambda b, n: (1, 0, 0, 0)),
                  cur(), cur(), prev,
                  pl.BlockSpec((1, LANES), lambda b, n: (0, 0))],
        out_specs=cur(),
        out_shape=jax.ShapeDtypeStruct((batch * seq, 512), BF16),
        compiler_params=pltpu.CompilerParams(
            dimension_semantics=("arbitrary", "arbitrary"), vmem_limit_bytes=VMEM_LIMIT),
        name="swa",
    )(sinks, bias, bias, qb, kv, kv, g2)


def _out_route_kernel(oa_ref, ob_ref, x_ref, woa_ref, wob_ref, g_ref, wrh_ref, wrl_ref, br_ref,
                      earlier_ref, h1_ref, hn3_ref, route_ref, cnt_ref, hbuf, hsem):
    i = pl.program_id(0)
    tm = TM_ROUTE

    @pl.when(i == 0)
    def _():
        cnt_ref[...] = jnp.zeros_like(cnt_ref)

    def hn_store(step, slot):
        return [pltpu.make_async_copy(hbuf.at[slot, :, pl.ds(c * LANES, LANES)],
                                      hn3_ref.at[pl.ds(step * tm, tm), c, :], hsem.at[slot])
                for c in range(SLAB_SUBLANES)]

    slot = i % 2

    @pl.when(i >= 2)
    def _():
        for cp in hn_store(i - 2, slot):
            cp.wait()

    h1 = (x_ref[...]
          + jnp.dot(oa_ref[...], woa_ref[...], preferred_element_type=F32)
          + jnp.dot(ob_ref[...], wob_ref[...], preferred_element_type=F32))
    h1_ref[...] = h1
    hn = _rms(h1, g_ref[...])
    hbuf[slot] = hn
    for cp in hn_store(i, slot):
        cp.start()

    @pl.when(i == pl.num_programs(0) - 1)
    def _():
        @pl.when(i >= 1)
        def _():
            for cp in hn_store(i - 1, 1 - slot):
                cp.wait()
        for cp in hn_store(i, slot):
            cp.wait()

    hn_hi = hn.astype(BF16)
    hn_lo = (hn - hn_hi.astype(F32)).astype(BF16)
    logits = (pl.dot(wrh_ref[...], hn_hi, trans_b=True)
              + pl.dot(wrh_ref[...], hn_lo, trans_b=True)
              + pl.dot(wrl_ref[...], hn_hi, trans_b=True)) + br_ref[...]

    eid = lax.broadcasted_iota(jnp.int32, (N_EXPERTS, tm), 0)
    work = logits
    vals, idxs, sels = [], [], []
    for _ in range(TOP_K):
        m = jnp.max(work, axis=0, keepdims=True)
        idx = jnp.min(jnp.where(work == m, eid, N_EXPERTS), axis=0, keepdims=True)
        sel = eid == idx
        vals.append(m)
        idxs.append(idx)
        sels.append(sel)
        work = jnp.where(sel, -3e38, work)
    exps = [jnp.exp(v - vals[0]) for v in vals]
    inv_den = 1.0 / (exps[0] + exps[1] + exps[2] + exps[3])

    multihot = jnp.where(sels[0] | sels[1] | sels[2] | sels[3], 1.0, 0.0)
    before = (jnp.dot(multihot.astype(BF16), earlier_ref[...], preferred_element_type=F32)
              + cnt_ref[:, 0:1])
    ranks = [jnp.sum(jnp.where(s, before, 0.0), axis=0, keepdims=True) for s in sels]
    route_ref[...] = jnp.concatenate(
        [i.astype(F32) for i in idxs] + [e * inv_den for e in exps] + ranks
        + [jnp.zeros((ROUTE_ROWS - 3 * TOP_K, tm), F32)], axis=0)
    cnt_ref[...] += jnp.sum(multihot, axis=1, keepdims=True)


def _out_route(oa, ob, x2, woa, wob, g2, wr_hi, wr_lo, br):
    T = x2.shape[0]
    tm = TM_ROUTE
    row = lambda w: pl.BlockSpec((tm, w), lambda i: (i, 0))
    full = lambda a: pl.BlockSpec(a.shape, lambda i: (0,) * a.ndim)
    earlier = jnp.asarray(np.triu(np.ones((tm, tm), np.float32), k=1), BF16)
    return pl.pallas_call(
        _out_route_kernel,
        grid=(T // tm,),
        in_specs=[row(512), row(512), row(D_MODEL), full(woa), full(wob), full(g2),
                  full(wr_hi), full(wr_lo), full(br), full(earlier)],
        out_specs=[row(D_MODEL), pl.BlockSpec(memory_space=pl.ANY),
                   pl.BlockSpec((ROUTE_ROWS, tm), lambda i: (0, i)),
                   pl.BlockSpec((N_EXPERTS, LANES), lambda i: (0, 0))],
        out_shape=[jax.ShapeDtypeStruct((T, D_MODEL), F32),
                   jax.ShapeDtypeStruct((T, SLAB_SUBLANES, LANES), F32),
                   jax.ShapeDtypeStruct((ROUTE_ROWS, T), F32),
                   jax.ShapeDtypeStruct((N_EXPERTS, LANES), F32)],
        scratch_shapes=[pltpu.VMEM((2, tm, D_MODEL), F32), pltpu.SemaphoreType.DMA((2,))],
        compiler_params=pltpu.CompilerParams(
            dimension_semantics=("arbitrary",), vmem_limit_bytes=VMEM_LIMIT,
            has_side_effects=True),
        name="out_route",
    )(oa, ob, x2, woa, wob, g2, wr_hi, wr_lo, br, earlier)


def _row_copy(src, s, dst, d, sem):
    return pltpu.make_async_copy(src.at[pl.ds(s, 1)], dst.at[pl.ds(d, 1)], sem)


def _dispatch_kernel(dest_ref, pad_start_ref, pad_cnt_ref, nused_ref, hn_ref, xs_ref,
                     zero_ref, sem, zsem):
    i = pl.program_id(0)
    n = pl.num_programs(0)
    tm = TM_EXPERT
    base = i * (TM_DISPATCH * TOP_K)
    for p in range(TM_DISPATCH * TOP_K):
        _row_copy(hn_ref, p // TOP_K, xs_ref, dest_ref[base + p], sem).start(priority=p % 2)
    for _ in range(TOP_K):
        pltpu.make_async_copy(hn_ref, xs_ref.at[pl.ds(0, TM_DISPATCH)], sem).wait()

    @pl.when(i == n - 1)
    def _():
        zero_ref[...] = jnp.zeros_like(zero_ref)
        sub = 8
        for e in range(N_EXPERTS):
            start, cnt = pad_start_ref[e], pad_cnt_ref[e]
            head = cnt & (sub - 1)
            body0 = start + head
            nbody = cnt // sub

            def head_copy(r):
                return _row_copy(zero_ref, 0, xs_ref, start + r, zsem)

            def body_copy(c):
                return pltpu.make_async_copy(zero_ref.at[pl.ds(0, sub)],
                                             xs_ref.at[pl.ds(body0 + c * sub, sub)], zsem)

            for copy, count in ((head_copy, head), (body_copy, nbody)):
                def zissue(r, carry, copy=copy):
                    copy(r).start()
                    return carry

                def zwait(r, carry, copy=copy):
                    copy(r).wait()
                    return carry

                lax.fori_loop(0, count, zissue, 0)
                lax.fori_loop(0, count, zwait, 0)

        def tail_copy(t):
            return pltpu.make_async_copy(zero_ref, xs_ref.at[pl.ds(t * tm, tm)], zsem)

        def tissue(t, carry):
            tail_copy(t).start()
            return carry

        def twait(t, carry):
            tail_copy(t).wait()
            return carry

        n_tiles = xs_ref.shape[0] // tm
        lax.fori_loop(nused_ref[0], n_tiles, tissue, 0)
        lax.fori_loop(nused_ref[0], n_tiles, twait, 0)


def _dispatch(dest, pad_start, pad_cnt, n_used, hn, n_rows):
    T = hn.shape[0]
    any_spec = pl.BlockSpec(memory_space=pl.ANY)
    return pl.pallas_call(
        _dispatch_kernel,
        grid_spec=pltpu.PrefetchScalarGridSpec(
            num_scalar_prefetch=4,
            grid=(T // TM_DISPATCH,),
            in_specs=[pl.BlockSpec((TM_DISPATCH, SLAB_SUBLANES, LANES), lambda i, *_: (i, 0, 0))],
            out_specs=any_spec,
            scratch_shapes=[pltpu.VMEM((TM_EXPERT, SLAB_SUBLANES, LANES), F32),
                            pltpu.SemaphoreType.DMA(()),
                            pltpu.SemaphoreType.DMA(())]),
        out_shape=jax.ShapeDtypeStruct((n_rows, SLAB_SUBLANES, LANES), F32),
        compiler_params=pltpu.CompilerParams(
            dimension_semantics=("arbitrary",), has_side_effects=True,
            vmem_limit_bytes=VMEM_LIMIT),
        name="dispatch",
    )(dest, pad_start, pad_cnt, n_used, hn)


def _experts_kernel(tile0_ref, ntile_ref, xs_ref, wg_ref, bg_ref, wl_ref, bl_ref, wd_ref, bd_ref,
                    ys_ref, wg_bf, wl_bf, wd_bf, xbuf, ybuf, xsem, ysem):
    e = pl.program_id(0)
    tm = TM_EXPERT
    t0, nt = tile0_ref[e], ntile_ref[e]

    class x_load:
        def __init__(self, tile, slot):
            self.copies = [
                pltpu.make_async_copy(xs_ref.at[pl.ds(tile * tm, tm), c, :],
                                      xbuf.at[slot, :, pl.ds(c * LANES, LANES)], xsem.at[slot])
                for c in range(SLAB_SUBLANES)]

        def start(self):
            for cp in self.copies:
                cp.start()

        def wait(self):
            for cp in self.copies:
                cp.wait()

    def y_store(tile, slot):
        return pltpu.make_async_copy(ybuf.at[slot], ys_ref.at[pl.ds(tile * tm, tm)], ysem.at[slot])

    @pl.when((e == 0) & (nt > 0))
    def _():
        x_load(t0, 0).start()

    @pl.when(nt > 0)
    def _():
        wg_bf[...] = wg_ref[0].astype(BF16)
        wl_bf[...] = wl_ref[0].astype(BF16)
        wd_bf[...] = wd_ref[0].astype(BF16)

    def tile_body(j, carry):
        slot = j % 2

        @pl.when(j + 1 < nt)
        def _():
            x_load(t0 + j + 1, 1 - slot).start()

        x_load(t0 + j, slot).wait()

        @pl.when(j >= 2)
        def _():
            y_store(t0 + j - 2, slot).wait()

        x = xbuf[slot].astype(BF16)
        glu = jnp.minimum(jnp.dot(x, wg_bf[...], preferred_element_type=F32) + bg_ref[0],
                          SWIGLU_LIMIT)
        lin = jnp.clip(jnp.dot(x, wl_bf[...], preferred_element_type=F32) + bl_ref[0],
                       -SWIGLU_LIMIT, SWIGLU_LIMIT)
        hid = glu * jax.nn.sigmoid(SWIGLU_ALPHA * glu) * (lin + 1.0)
        ybuf[slot] = jnp.dot(hid.astype(BF16), wd_bf[...], preferred_element_type=F32) + bd_ref[0]
        y_store(t0 + j, slot).start()
        return carry

    lax.fori_loop(0, nt, tile_body, 0)

    @pl.when(e + 1 < pl.num_programs(0))
    def _():
        @pl.when(ntile_ref[e + 1] > 0)
        def _():
            x_load(tile0_ref[e + 1], 0).start()

    @pl.when(nt >= 2)
    def _():
        y_store(t0 + nt - 2, nt % 2).wait()

    @pl.when(nt >= 1)
    def _():
        y_store(t0 + nt - 1, (nt - 1) % 2).wait()

    @pl.when(e == pl.num_programs(0) - 1)
    def _():
        ybuf[0] = jnp.zeros((tm, D_MODEL), F32)
        n_tiles = ys_ref.shape[0] // tm

        def tail_issue(t, carry):
            y_store(t, 0).start()
            return carry

        def tail_wait(t, carry):
            y_store(t, 0).wait()
            return carry

        lax.fori_loop(t0 + nt, n_tiles, tail_issue, 0)
        lax.fori_loop(t0 + nt, n_tiles, tail_wait, 0)


def _experts(tile0, ntile, xs, w_glu, b_glu, w_lin, b_lin, w_down, b_down):
    tm = TM_EXPERT
    any_spec = pl.BlockSpec(memory_space=pl.ANY)
    wspec = lambda: pl.BlockSpec((1, D_MODEL, D_FF), lambda e, *_: (e, 0, 0))
    bspec = lambda: pl.BlockSpec((1, 1, D_FF), lambda e, *_: (e, 0, 0))
    return pl.pallas_call(
        _experts_kernel,
        grid_spec=pltpu.PrefetchScalarGridSpec(
            num_scalar_prefetch=2,
            grid=(N_EXPERTS,),
            in_specs=[any_spec, wspec(), bspec(), wspec(), bspec(), wspec(), bspec()],
            out_specs=any_spec,
            scratch_shapes=[pltpu.VMEM((D_MODEL, D_FF), BF16)] * 3
            + [pltpu.VMEM((2, tm, D_MODEL), F32), pltpu.VMEM((2, tm, D_MODEL), F32),
               pltpu.SemaphoreType.DMA((2,)), pltpu.SemaphoreType.DMA((2,))]),
        out_shape=jax.ShapeDtypeStruct((xs.shape[0], D_MODEL), F32),
        compiler_params=pltpu.CompilerParams(
            dimension_semantics=("arbitrary",), vmem_limit_bytes=VMEM_LIMIT,
            has_side_effects=True),
        name="experts",
    )(tile0, ntile, xs, w_glu, b_glu.reshape(N_EXPERTS, 1, D_FF),
      w_lin, b_lin.reshape(N_EXPERTS, 1, D_FF), w_down, b_down.reshape(N_EXPERTS, 1, D_MODEL))


def _combine_kernel(dest_ref, ys_ref, h1_ref, gate_ref, g_ref, o_ref, buf, sems):
    s = pl.program_id(0)
    n = pl.num_programs(0) - 1
    tm = TM_COMBINE

    def issue(slot):
        base = s * (tm * TOP_K)
        for p in range(tm * TOP_K):
            r, k = divmod(p, TOP_K)
            pltpu.make_async_copy(ys_ref.at[pl.ds(dest_ref[base + p], 1)],
                                  buf.at[slot, k, pl.ds(r, 1)],
                                  sems.at[slot]).start(priority=p % 2)

    for parity in range(2):
        @pl.when((s < n) & (s % 2 == parity))
        def _():
            issue(parity)

    @pl.when(s > 0)
    def _():
        slot = (s - 1) % 2
        for k in range(TOP_K):
            pltpu.make_async_copy(ys_ref.at[pl.ds(0, tm)], buf.at[slot, k], sems.at[slot]).wait()
        acc = h1_ref[...]
        for k in range(TOP_K):
            acc = acc + gate_ref[:, k:k + 1] * buf[slot, k]
        o_ref[...] = _rms(acc, g_ref[...])


def _combine(dest, ys, h1, gate, fg):
    T = h1.shape[0]
    tm = TM_COMBINE
    return pl.pallas_call(
        _combine_kernel,
        grid_spec=pltpu.PrefetchScalarGridSpec(
            num_scalar_prefetch=1,
            grid=(T // tm + 1,),
            in_specs=[pl.BlockSpec(memory_space=pl.ANY),
                      pl.BlockSpec((tm, D_MODEL), lambda s, d: (jnp.maximum(s - 1, 0), 0)),
                      pl.BlockSpec((tm, TOP_K), lambda s, d: (jnp.maximum(s - 1, 0), 0)),
                      pl.BlockSpec((1, D_MODEL), lambda s, d: (0, 0))],
            out_specs=pl.BlockSpec((tm, D_MODEL), lambda s, d: (jnp.maximum(s - 1, 0), 0)),
            scratch_shapes=[pltpu.VMEM((2, TOP_K, tm, D_MODEL), F32),
                            pltpu.SemaphoreType.DMA((2,))]),
        out_shape=jax.ShapeDtypeStruct((T, D_MODEL), F32),
        compiler_params=pltpu.CompilerParams(
            dimension_semantics=("arbitrary",), vmem_limit_bytes=VMEM_LIMIT),
        name="combine",
    )(dest, ys, h1, gate, fg)


def _swap_halves(w):
    h = w.shape[-1] // 2
    return jnp.concatenate([w[..., h:], w[..., :h]], axis=-1)


def _layer(x2, batch, seq, norm1_g, w_in, w_alpha_up, b_alpha, gla_norm_g, swa_sinks, swa_norm_g,
           w_out, norm2_g, w_router, b_router, w_glu, b_glu, w_lin, b_lin, w_down, b_down):
    T = x2.shape[0]
    kb_w, vb_w = w_in[:, 2064:2192], w_in[:, 2192:2320]
    w_z = w_in[:, 1536:1552]
    w_cat = jnp.concatenate([
        w_in[:, 0:1536], w_in[:, 1552:2064],
        kb_w, _swap_halves(kb_w), vb_w, _swap_halves(vb_w),
        jnp.pad(jnp.tile(w_z, (1, Z_PIECES)), [(0, 0), (0, LANES - Z_PIECES * GLA_RANK)]),
    ], axis=1).astype(BF16)
    wup_hi = w_alpha_up.astype(BF16)
    wup_lo = (w_alpha_up - wup_hi.astype(F32)).astype(BF16)
    wup_cat = jnp.pad(jnp.concatenate([wup_hi, wup_hi, wup_hi, wup_lo, wup_lo], axis=0),
                      [(0, LANES - Z_PIECES * GLA_RANK), (0, 0)])

    qk, vr, la, qb, kv = _in_proj(
        x2, norm1_g.reshape(1, -1), w_cat, wup_cat, b_alpha.reshape(1, -1))
    oa = _gla(qk, vr, la, gla_norm_g.reshape(1, -1), batch, seq)
    ob = _swa(qb, kv, swa_sinks, jnp.tile(swa_norm_g, 2).reshape(1, -1), batch, seq)

    wo = w_out.astype(BF16)
    wr_t = w_router.T
    wr_hi = wr_t.astype(BF16)
    wr_lo = (wr_t - wr_hi.astype(F32)).astype(BF16)
    h1, hn, route, cnt = _out_route(oa, ob, x2, wo[:512], wo[512:], norm2_g.reshape(1, -1),
                                    wr_hi, wr_lo, b_router.reshape(-1, 1))

    tm = TM_EXPERT
    n_tiles = T * TOP_K // tm + N_EXPERTS
    counts = cnt[:, 0].astype(jnp.int32)
    padded = (counts + tm - 1) // tm * tm
    pends = jnp.cumsum(padded)
    pstarts = pends - padded
    top_idx = route[0:TOP_K].T.astype(jnp.int32)
    gate = route[TOP_K:2 * TOP_K].T
    rank = route[2 * TOP_K:3 * TOP_K].T.astype(jnp.int32)
    seg_start = jnp.sum(jnp.where(top_idx[..., None] == jnp.arange(N_EXPERTS), pstarts, 0), axis=-1)
    dest = (seg_start + rank).reshape(-1)
    n_used = (pends[-1] // tm).reshape(1)
    xs = _dispatch(dest, pstarts + counts, padded - counts, n_used, hn, n_tiles * tm)
    ys = _experts(pstarts // tm, padded // tm, xs, w_glu, b_glu, w_lin, b_lin, w_down, b_down)
    return dest, ys, h1, gate


def kernel(x, norm1_g, w_in, w_alpha_up, b_alpha, gla_norm_g, swa_sinks, swa_norm_g, w_out,
           norm2_g, w_router, b_router, w_glu, b_glu, w_lin, b_lin, w_down, b_down, final_g):
    batch, seq, d = x.shape
    assert norm1_g.shape[0] == 1, "single-layer problem"
    x2 = x.reshape(batch * seq, d)
    dest, ys, h1, gate = _layer(
        x2, batch, seq, norm1_g[0], w_in[0], w_alpha_up[0], b_alpha[0], gla_norm_g[0],
        swa_sinks[0], swa_norm_g[0], w_out[0], norm2_g[0], w_router[0], b_router[0],
        w_glu[0], b_glu[0], w_lin[0], b_lin[0], w_down[0], b_down[0])
    out = _combine(dest, ys, h1, gate, final_g.reshape(1, -1))
    return out.reshape(batch, seq, d)
```

```python
import numpy as np
import jax
import jax.numpy as jnp
from jax import lax
from jax.experimental import pallas as pl
from jax.experimental.pallas import tpu as pltpu

F32 = jnp.float32
BF16 = jnp.bfloat16
HIGHEST = lax.Precision.HIGHEST

D_MODEL = 1024
GLA_HEADS = 4
GLA_DK = 64
GLA_DV = 128
GLA_RANK = 16
GLA_GATE_TAU = 16.0
GLA_CHUNK = 64
SWA_HEADS = 8
SWA_KV_HEADS = 2
SWA_HEAD_DIM = 64
SWA_WINDOW = 128
N_EXPERTS = 32
TOP_K = 4
D_FF = 1024
SWIGLU_LIMIT = 7.0
SWIGLU_ALPHA = 1.702
NORM_EPS = 1e-5

LANES = 128
SLAB_SUBLANES = D_MODEL // LANES
VMEM_LIMIT = 56 * 1024 * 1024

TM_PROJ = 512
TL_GLA = 1024
GLA_CUM_ROWS = 256
TM_ROUTE = 512
TM_EXPERT = 512
TM_COMBINE = 256
SWA_SUB = 4
TM_DISPATCH = 256

NEG_BIG = -1e30
ROUTE_ROWS = 16

C_QA, C_KA, C_VA, C_RA, C_QB, C_KB, C_KBS, C_VB, C_VBS, C_Z, C_END = (
    0, 256, 512, 1024, 1536, 2048, 2176, 2304, 2432, 2560, 2688)
Z_PIECES = 5


def _split3(x):
    hi = x.astype(BF16).astype(F32)
    r = x - hi
    mid = r.astype(BF16).astype(F32)
    lo = (r - mid).astype(BF16).astype(F32)
    return hi, mid, lo


def _rms(x, g):
    return x * lax.rsqrt(jnp.mean(x * x, axis=-1, keepdims=True) + NORM_EPS) * g


def _in_proj_kernel(x_ref, g_ref, w_ref, wup_ref, ba_ref,
                    qk_ref, vr_ref, la_ref, qb_ref, kv_ref):
    u = _rms(x_ref[...], g_ref[...]).astype(BF16)

    def proj(c0, c1):
        return jnp.dot(u, w_ref[:, c0:c1], preferred_element_type=F32)

    qk_ref[...] = proj(C_QA, C_VA).astype(BF16)
    vr_ref[:, :C_RA - C_VA] = proj(C_VA, C_RA).astype(BF16)
    vr_ref[:, C_RA - C_VA:] = proj(C_RA, C_QB).astype(BF16)
    qb_ref[...] = proj(C_QB, C_KB).astype(BF16)
    kv_ref[...] = proj(C_KB, C_Z).astype(BF16)
    z = proj(C_Z, C_END)
    hi, mid, lo = _split3(z)
    piece = lax.broadcasted_iota(jnp.int32, z.shape, 1) // GLA_RANK
    zc = jnp.where((piece == 0) | (piece == 3), hi, jnp.where(piece == 2, lo, mid)).astype(BF16)
    y = jnp.dot(zc, wup_ref[...], preferred_element_type=F32) + ba_ref[...]
    log_sig = jnp.minimum(y, 0.0) - jnp.log1p(jnp.exp(-jnp.abs(y)))
    la_ref[...] = log_sig * (1.0 / GLA_GATE_TAU)


def _in_proj(x2, g1, w_cat, wup_p, ba_p):
    T = x2.shape[0]
    tm = TM_PROJ
    row = lambda w: pl.BlockSpec((tm, w), lambda i: (i, 0))
    full = lambda a: pl.BlockSpec(a.shape, lambda i: (0,) * a.ndim)
    outs = [(512, BF16), (1024, BF16), (256, F32), (512, BF16), (512, BF16)]
    return pl.pallas_call(
        _in_proj_kernel,
        grid=(T // tm,),
        in_specs=[row(D_MODEL), full(g1), full(w_cat), full(wup_p), full(ba_p)],
        out_specs=[row(w) for w, _ in outs],
        out_shape=[jax.ShapeDtypeStruct((T, w), dt) for w, dt in outs],
        compiler_params=pltpu.CompilerParams(
            dimension_semantics=("arbitrary",), vmem_limit_bytes=VMEM_LIMIT),
        name="in_proj",
    )(x2, g1, w_cat, wup_p, ba_p)


def _gla_kernel(qk_ref, vr_ref, la_ref, g_ref, cum_ref, o_ref, st_ref):
    @pl.when(pl.program_id(1) == 0)
    def _():
        st_ref[...] = jnp.zeros_like(st_ref)

    tl = TL_GLA
    c = GLA_CHUNK
    kw = GLA_HEADS * GLA_DK
    vw = GLA_HEADS * GLA_DV
    causal = (lax.broadcasted_iota(jnp.int32, (c, c), 0)
              >= lax.broadcasted_iota(jnp.int32, (c, c), 1))
    low_half = lax.broadcasted_iota(jnp.int32, (c, LANES), 1) < GLA_DK
    g = g_ref[...]
    b_groups = []
    for grp in range(tl // GLA_CUM_ROWS):
        la = la_ref[grp * GLA_CUM_ROWS:(grp + 1) * GLA_CUM_ROWS, :]
        pieces = jnp.concatenate([p.astype(BF16) for p in _split3(la)], axis=1)
        b3 = jnp.dot(cum_ref[...], pieces, preferred_element_type=F32)
        b_groups.append(b3[:, :kw] + b3[:, kw:2 * kw] + b3[:, 2 * kw:])
    for ch in range(tl // c):
        rows = slice(ch * c, (ch + 1) * c)
        in_grp = (ch * c) % GLA_CUM_ROWS
        b = b_groups[(ch * c) // GLA_CUM_ROWS][in_grp:in_grp + c]
        b_last = b[c - 1:c]
        qf = qk_ref[rows, :kw].astype(F32)
        kf = qk_ref[rows, kw:].astype(F32)
        q_e = (qf * jnp.exp(b) * (GLA_DK ** -0.5)).astype(BF16)
        k_e = (kf * jnp.exp(-b)).astype(BF16)
        k_t = (kf * jnp.exp(b_last - b)).astype(BF16)
        decay = jnp.exp(b_last)
        for h in range(GLA_HEADS):
            ps = slice((h // 2) * LANES, (h // 2 + 1) * LANES)
            mine = low_half if h % 2 == 0 else ~low_half
            qp, kp = q_e[:, ps], k_e[:, ps]
            qh = jnp.where(mine, qp, jnp.zeros_like(qp))
            kth = jnp.where(mine, k_t[:, ps], jnp.zeros_like(qp))
            vs = slice(h * GLA_DV, (h + 1) * GLA_DV)
            vh = vr_ref[rows, vs]
            a = pl.dot(qh, kp, trans_b=True)
            a = jnp.where(causal, a, 0.0).astype(BF16)
            st = st_ref[h]
            o = (jnp.dot(a, vh, preferred_element_type=F32)
                 + pl.dot(qh, st.astype(BF16), trans_b=True))
            st_ref[h] = st * decay[:, ps] + pl.dot(vh, kth, trans_a=True)
            rh = vr_ref[rows, slice(vw + h * GLA_DV, vw + (h + 1) * GLA_DV)].astype(F32)
            o = _rms(o, g) * (rh * jax.nn.sigmoid(rh))
            o_ref[rows, vs] = o.astype(BF16)


def _gla(qk, vr, la, g, batch, seq):
    tl = TL_GLA
    nl = seq // tl
    row = lambda w: pl.BlockSpec((tl, w), lambda b, i: (b * nl + i, 0))
    r = np.arange(GLA_CUM_ROWS)
    cum = jnp.asarray((r[None, :] <= r[:, None])
                      & (r[None, :] // GLA_CHUNK == r[:, None] // GLA_CHUNK), BF16)
    return pl.pallas_call(
        _gla_kernel,
        grid=(batch, nl),
        in_specs=[row(512), row(1024), row(256),
                  pl.BlockSpec((1, GLA_DV), lambda b, i: (0, 0)),
                  pl.BlockSpec((GLA_CUM_ROWS, GLA_CUM_ROWS), lambda b, i: (0, 0))],
        out_specs=row(512),
        out_shape=jax.ShapeDtypeStruct((batch * seq, 512), BF16),
        scratch_shapes=[pltpu.VMEM((GLA_HEADS, GLA_DV, LANES), F32)],
        compiler_params=pltpu.CompilerParams(
            dimension_semantics=("arbitrary", "arbitrary"), vmem_limit_bytes=VMEM_LIMIT),
        name="gla",
    )(qk, vr, la, g, cum)


def _swa_kernel(sink_ref, bias0_ref, bias_ref, q_ref, kvc_ref, kvp_ref, g_ref, o_ref):
    w = SWA_WINDOW
    scale = jnp.asarray(SWA_HEAD_DIM ** -0.5, BF16)
    lane_lo = lax.broadcasted_iota(jnp.int32, (2 * w, LANES), 1) < SWA_HEAD_DIM
    hi_half = lax.broadcasted_iota(jnp.int32, (LANES, LANES), 0) // SWA_HEAD_DIM
    hj_half = lax.broadcasted_iota(jnp.int32, (LANES, LANES), 1) // SWA_HEAD_DIM
    mean_mat = jnp.where(hi_half == hj_half, 1.0 / SWA_HEAD_DIM, 0.0).astype(F32)
    g = g_ref[...]

    for sb in range(SWA_SUB):
        rows = slice(sb * w, (sb + 1) * w)
        if sb == 0:
            kv = jnp.concatenate([kvp_ref[...], kvc_ref[0:w, :]], axis=0)
            bias = bias0_ref
        else:
            kv = kvc_ref[(sb - 1) * w:(sb + 1) * w, :]
            bias = bias_ref
        k = kv[:, 0:LANES] * scale
        ks = kv[:, LANES:2 * LANES] * scale
        v, vs = kv[:, 2 * LANES:3 * LANES], kv[:, 3 * LANES:]
        zero = jnp.zeros_like(k)
        k_low = [jnp.where(lane_lo, k, zero), jnp.where(lane_lo, ks, zero)]
        k_high = [jnp.where(lane_lo, zero, ks), jnp.where(lane_lo, zero, k)]
        v_low = [jnp.where(lane_lo, v, zero), jnp.where(lane_lo, vs, zero)]
        v_high = [jnp.where(lane_lo, zero, vs), jnp.where(lane_lo, zero, v)]

        def probs(s, head):
            s = s + bias[0, head]
            sink = sink_ref[head]
            m = jnp.maximum(jnp.max(s, axis=-1, keepdims=True), sink)
            e = jnp.exp(s - m)
            den = jnp.sum(e, axis=-1, keepdims=True) + jnp.exp(sink - m)
            return (e * (1.0 / den)).astype(BF16)

        for pair in range(SWA_HEADS // 2):
            j = (2 * pair) // (SWA_HEADS // SWA_KV_HEADS)
            cols = slice(pair * LANES, (pair + 1) * LANES)
            qp = q_ref[rows, cols]
            p0 = probs(pl.dot(qp, k_low[j], trans_b=True), 2 * pair)
            p1 = probs(pl.dot(qp, k_high[j], trans_b=True), 2 * pair + 1)
            o = (jnp.dot(p0, v_low[j], preferred_element_type=F32)
                 + jnp.dot(p1, v_high[j], preferred_element_type=F32))
            ms = jnp.dot(o * o, mean_mat, preferred_element_type=F32, precision=HIGHEST)
            o = o * lax.rsqrt(ms + NORM_EPS) * g
            o_ref[rows, cols] = o.astype(BF16)


def _swa_bias():
    w = SWA_WINDOW
    slopes = 2.0 ** (-8.0 * np.arange(1, SWA_HEADS + 1, dtype=np.float64) / SWA_HEADS)
    rel = np.arange(w)[:, None] + w - np.arange(2 * w)[None, :]
    in_window = (rel >= 0) & (rel < w)
    exists = np.stack([np.arange(2 * w) >= w, np.ones(2 * w, bool)])
    valid = in_window[None] & exists[:, None, :]
    bias = -slopes[None, :, None, None] * rel[None, None].astype(np.float64)
    return np.where(valid[:, None], bias, NEG_BIG).astype(np.float32)


def _swa(qb, kv, sinks, g2, batch, seq):
    w = SWA_WINDOW
    rows = SWA_SUB * w
    nb = seq // rows
    cur = lambda: pl.BlockSpec((rows, 512), lambda b, n: (b * nb + n, 0))
    prev = pl.BlockSpec(
        (w, 512), lambda b, n: (jnp.maximum((b * nb + n) * SWA_SUB - 1, b * nb * SWA_SUB), 0))
    table = lambda index: pl.BlockSpec((1, SWA_HEADS, w, 2 * w), index)
    bias = jnp.asarray(_swa_bias())
    return pl.pallas_call(
        _swa_kernel,
        grid=(batch, nb),
        in_specs=[pl.BlockSpec(memory_space=pltpu.SMEM),
                  table(lambda b, n: (jnp.minimum(n, 1), 0, 0, 0)),
                  table(lambda b, n: (1, 0, 0, 0)),
                  cur(), cur(), prev,
                  pl.BlockSpec((1, LANES), lambda b, n: (0, 0))],
        out_specs=cur(),
        out_shape=jax.ShapeDtypeStruct((batch * seq, 512), BF16),
        compiler_params=pltpu.CompilerParams(
            dimension_semantics=("arbitrary", "arbitrary"), vmem_limit_bytes=VMEM_LIMIT),
        name="swa",
    )(sinks, bias, bias, qb, kv, kv, g2)


def _out_route_kernel(oa_ref, ob_ref, x_ref, woa_ref, wob_ref, g_ref, wrh_ref, wrl_ref, br_ref,
                      earlier_ref, h1_ref, hn3_ref, route_ref, cnt_ref, hbuf, hsem):
    i = pl.program_id(0)
    tm = TM_ROUTE

    @pl.when(i == 0)
    def _():
        cnt_ref[...] = jnp.zeros_like(cnt_ref)

    def hn_store(step, slot):
        return [pltpu.make_async_copy(hbuf.at[slot, :, pl.ds(c * LANES, LANES)],
                                      hn3_ref.at[pl.ds(step * tm, tm), c, :], hsem.at[slot])
                for c in range(SLAB_SUBLANES)]

    slot = i % 2

    @pl.when(i >= 2)
    def _():
        for cp in hn_store(i - 2, slot):
            cp.wait()

    h1 = (x_ref[...]
          + jnp.dot(oa_ref[...], woa_ref[...], preferred_element_type=F32)
          + jnp.dot(ob_ref[...], wob_ref[...], preferred_element_type=F32))
    h1_ref[...] = h1
    hn = _rms(h1, g_ref[...])
    hbuf[slot] = hn
    for cp in hn_store(i, slot):
        cp.start()

    @pl.when(i == pl.num_programs(0) - 1)
    def _():
        @pl.when(i >= 1)
        def _():
            for cp in hn_store(i - 1, 1 - slot):
                cp.wait()
        for cp in hn_store(i, slot):
            cp.wait()

    hn_hi = hn.astype(BF16)
    hn_lo = (hn - hn_hi.astype(F32)).astype(BF16)
    logits = (pl.dot(wrh_ref[...], hn_hi, trans_b=True)
              + pl.dot(wrh_ref[...], hn_lo, trans_b=True)
              + pl.dot(wrl_ref[...], hn_hi, trans_b=True)) + br_ref[...]

    eid = lax.broadcasted_iota(jnp.int32, (N_EXPERTS, tm), 0)
    work = logits
    vals, idxs, sels = [], [], []
    for _ in range(TOP_K):
        m = jnp.max(work, axis=0, keepdims=True)
        idx = jnp.min(jnp.where(work == m, eid, N_EXPERTS), axis=0, keepdims=True)
        sel = eid == idx
        vals.append(m)
        idxs.append(idx)
        sels.append(sel)
        work = jnp.where(sel, -3e38, work)
    exps = [jnp.exp(v - vals[0]) for v in vals]
    inv_den = 1.0 / (exps[0] + exps[1] + exps[2] + exps[3])

    multihot = jnp.where(sels[0] | sels[1] | sels[2] | sels[3], 1.0, 0.0)
    before = (jnp.dot(multihot.astype(BF16), earlier_ref[...], preferred_element_type=F32)
              + cnt_ref[:, 0:1])
    ranks = [jnp.sum(jnp.where(s, before, 0.0), axis=0, keepdims=True) for s in sels]
    route_ref[...] = jnp.concatenate(
        [i.astype(F32) for i in idxs] + [e * inv_den for e in exps] + ranks
        + [jnp.zeros((ROUTE_ROWS - 3 * TOP_K, tm), F32)], axis=0)
    cnt_ref[...] += jnp.sum(multihot, axis=1, keepdims=True)


def _out_route(oa, ob, x2, woa, wob, g2, wr_hi, wr_lo, br):
    T = x2.shape[0]
    tm = TM_ROUTE
    row = lambda w: pl.BlockSpec((tm, w), lambda i: (i, 0))
    full = lambda a: pl.BlockSpec(a.shape, lambda i: (0,) * a.ndim)
    earlier = jnp.asarray(np.triu(np.ones((tm, tm), np.float32), k=1), BF16)
    return pl.pallas_call(
        _out_route_kernel,
        grid=(T // tm,),
        in_specs=[row(512), row(512), row(D_MODEL), full(woa), full(wob), full(g2),
                  full(wr_hi), full(wr_lo), full(br), full(earlier)],
        out_specs=[row(D_MODEL), pl.BlockSpec(memory_space=pl.ANY),
                   pl.BlockSpec((ROUTE_ROWS, tm), lambda i: (0, i)),
                   pl.BlockSpec((N_EXPERTS, LANES), lambda i: (0, 0))],
        out_shape=[jax.ShapeDtypeStruct((T, D_MODEL), F32),
                   jax.ShapeDtypeStruct((T, SLAB_SUBLANES, LANES), F32),
                   jax.ShapeDtypeStruct((ROUTE_ROWS, T), F32),
                   jax.ShapeDtypeStruct((N_EXPERTS, LANES), F32)],
        scratch_shapes=[pltpu.VMEM((2, tm, D_MODEL), F32), pltpu.SemaphoreType.DMA((2,))],
        compiler_params=pltpu.CompilerParams(
            dimension_semantics=("arbitrary",), vmem_limit_bytes=VMEM_LIMIT,
            has_side_effects=True),
        name="out_route",
    )(oa, ob, x2, woa, wob, g2, wr_hi, wr_lo, br, earlier)


def _row_copy(src, s, dst, d, sem):
    return pltpu.make_async_copy(src.at[pl.ds(s, 1)], dst.at[pl.ds(d, 1)], sem)


def _dispatch_kernel(dest_ref, pad_start_ref, pad_cnt_ref, nused_ref, hn_ref, xs_ref,
                     zero_ref, sem, zsem):
    i = pl.program_id(0)
    n = pl.num_programs(0)
    tm = TM_EXPERT
    base = i * (TM_DISPATCH * TOP_K)
    for p in range(TM_DISPATCH * TOP_K):
        _row_copy(hn_ref, p // TOP_K, xs_ref, dest_ref[base + p], sem).start(priority=p % 2)
    for _ in range(TOP_K):
        pltpu.make_async_copy(hn_ref, xs_ref.at[pl.ds(0, TM_DISPATCH)], sem).wait()

    @pl.when(i == n - 1)
    def _():
        zero_ref[...] = jnp.zeros_like(zero_ref)
        sub = 8
        for e in range(N_EXPERTS):
            start, cnt = pad_start_ref[e], pad_cnt_ref[e]
            head = cnt & (sub - 1)
            body0 = start + head
            nbody = cnt // sub

            def head_copy(r):
                return _row_copy(zero_ref, 0, xs_ref, start + r, zsem)

            def body_copy(c):
                return pltpu.make_async_copy(zero_ref.at[pl.ds(0, sub)],
                                             xs_ref.at[pl.ds(body0 + c * sub, sub)], zsem)

            for copy, count in ((head_copy, head), (body_copy, nbody)):
                def zissue(r, carry, copy=copy):
                    copy(r).start()
                    return carry

                def zwait(r, carry, copy=copy):
                    copy(r).wait()
                    return carry

                lax.fori_loop(0, count, zissue, 0)
                lax.fori_loop(0, count, zwait, 0)

        def tail_copy(t):
            return pltpu.make_async_copy(zero_ref, xs_ref.at[pl.ds(t * tm, tm)], zsem)

        def tissue(t, carry):
            tail_copy(t).start()
            return carry

        def twait(t, carry):
            tail_copy(t).wait()
            return carry

        n_tiles = xs_ref.shape[0] // tm
        lax.fori_loop(nused_ref[0], n_tiles, tissue, 0)
        lax.fori_loop(nused_ref[0], n_tiles, twait, 0)


def _dispatch(dest, pad_start, pad_cnt, n_used, hn, n_rows):
    T = hn.shape[0]
    any_spec = pl.BlockSpec(memory_space=pl.ANY)
    return pl.pallas_call(
        _dispatch_kernel,
        grid_spec=pltpu.PrefetchScalarGridSpec(
            num_scalar_prefetch=4,
            grid=(T // TM_DISPATCH,),
            in_specs=[pl.BlockSpec((TM_DISPATCH, SLAB_SUBLANES, LANES), lambda i, *_: (i, 0, 0))],
            out_specs=any_spec,
            scratch_shapes=[pltpu.VMEM((TM_EXPERT, SLAB_SUBLANES, LANES), F32),
                            pltpu.SemaphoreType.DMA(()),
                            pltpu.SemaphoreType.DMA(())]),
        out_shape=jax.ShapeDtypeStruct((n_rows, SLAB_SUBLANES, LANES), F32),
        compiler_params=pltpu.CompilerParams(
            dimension_semantics=("arbitrary",), has_side_effects=True,
            vmem_limit_bytes=VMEM_LIMIT),
        name="dispatch",
    )(dest, pad_start, pad_cnt, n_used, hn)


def _experts_kernel(tile0_ref, ntile_ref, xs_ref, wg_ref, bg_ref, wl_ref, bl_ref, wd_ref, bd_ref,
                    ys_ref, wg_bf, wl_bf, wd_bf, xbuf, ybuf, xsem, ysem):
    e = pl.program_id(0)
    tm = TM_EXPERT
    t0, nt = tile0_ref[e], ntile_ref[e]

    class x_load:
        def __init__(self, tile, slot):
            self.copies = [
                pltpu.make_async_copy(xs_ref.at[pl.ds(tile * tm, tm), c, :],
                                      xbuf.at[slot, :, pl.ds(c * LANES, LANES)], xsem.at[slot])
                for c in range(SLAB_SUBLANES)]

        def start(self):
            for cp in self.copies:
                cp.start()

        def wait(self):
            for cp in self.copies:
                cp.wait()

    def y_store(tile, slot):
        return pltpu.make_async_copy(ybuf.at[slot], ys_ref.at[pl.ds(tile * tm, tm)], ysem.at[slot])

    @pl.when((e == 0) & (nt > 0))
    def _():
        x_load(t0, 0).start()

    @pl.when(nt > 0)
    def _():
        wg_bf[...] = wg_ref[0].astype(BF16)
        wl_bf[...] = wl_ref[0].astype(BF16)
        wd_bf[...] = wd_ref[0].astype(BF16)

    def do_tile(j, slot):
        @pl.when(j + 1 < nt)
        def _():
            x_load(t0 + j + 1, 1 - slot).start()

        x_load(t0 + j, slot).wait()

        @pl.when(j >= 2)
        def _():
            y_store(t0 + j - 2, slot).wait()

        x = xbuf[slot].astype(BF16)
        glu = jnp.minimum(jnp.dot(x, wg_bf[...], preferred_element_type=F32) + bg_ref[0],
                          SWIGLU_LIMIT)
        lin = jnp.clip(jnp.dot(x, wl_bf[...], preferred_element_type=F32) + bl_ref[0],
                       -SWIGLU_LIMIT, SWIGLU_LIMIT)
        hid = glu * jax.nn.sigmoid(SWIGLU_ALPHA * glu) * (lin + 1.0)
        ybuf[slot] = jnp.dot(hid.astype(BF16), wd_bf[...], preferred_element_type=F32) + bd_ref[0]
        y_store(t0 + j, slot).start()

    def pair_body(p, carry):
        do_tile(2 * p, 0)

        @pl.when(2 * p + 1 < nt)
        def _():
            do_tile(2 * p + 1, 1)

        return carry

    lax.fori_loop(0, (nt + 1) // 2, pair_body, 0)

    @pl.when(e + 1 < pl.num_programs(0))
    def _():
        @pl.when(ntile_ref[e + 1] > 0)
        def _():
            x_load(tile0_ref[e + 1], 0).start()

    @pl.when(nt >= 2)
    def _():
        y_store(t0 + nt - 2, nt % 2).wait()

    @pl.when(nt >= 1)
    def _():
        y_store(t0 + nt - 1, (nt - 1) % 2).wait()

    @pl.when(e == pl.num_programs(0) - 1)
    def _():
        ybuf[0] = jnp.zeros((tm, D_MODEL), F32)
        n_tiles = ys_ref.shape[0] // tm

        def tail_issue(t, carry):
            y_store(t, 0).start()
            return carry

        def tail_wait(t, carry):
            y_store(t, 0).wait()
            return carry

        lax.fori_loop(t0 + nt, n_tiles, tail_issue, 0)
        lax.fori_loop(t0 + nt, n_tiles, tail_wait, 0)


def _experts(tile0, ntile, xs, w_glu, b_glu, w_lin, b_lin, w_down, b_down):
    tm = TM_EXPERT
    any_spec = pl.BlockSpec(memory_space=pl.ANY)
    wspec = lambda: pl.BlockSpec((1, D_MODEL, D_FF), lambda e, *_: (e, 0, 0))
    bspec = lambda: pl.BlockSpec((1, 1, D_FF), lambda e, *_: (e, 0, 0))
    return pl.pallas_call(
        _experts_kernel,
        grid_spec=pltpu.PrefetchScalarGridSpec(
            num_scalar_prefetch=2,
            grid=(N_EXPERTS,),
            in_specs=[any_spec, wspec(), bspec(), wspec(), bspec(), wspec(), bspec()],
            out_specs=any_spec,
            scratch_shapes=[pltpu.VMEM((D_MODEL, D_FF), BF16)] * 3
            + [pltpu.VMEM((2, tm, D_MODEL), F32), pltpu.VMEM((2, tm, D_MODEL), F32),
               pltpu.SemaphoreType.DMA((2,)), pltpu.SemaphoreType.DMA((2,))]),
        out_shape=jax.ShapeDtypeStruct((xs.shape[0], D_MODEL), F32),
        compiler_params=pltpu.CompilerParams(
            dimension_semantics=("arbitrary",), vmem_limit_bytes=VMEM_LIMIT,
            has_side_effects=True),
        name="experts",
    )(tile0, ntile, xs, w_glu, b_glu.reshape(N_EXPERTS, 1, D_FF),
      w_lin, b_lin.reshape(N_EXPERTS, 1, D_FF), w_down, b_down.reshape(N_EXPERTS, 1, D_MODEL))


def _combine_kernel(dest_ref, ys_ref, h1_ref, gate_ref, g_ref, o_ref, buf, sems):
    s = pl.program_id(0)
    n = pl.num_programs(0) - 1
    tm = TM_COMBINE

    def issue(slot):
        base = s * (tm * TOP_K)
        for p in range(tm * TOP_K):
            r, k = divmod(p, TOP_K)
            pltpu.make_async_copy(ys_ref.at[pl.ds(dest_ref[base + p], 1)],
                                  buf.at[slot, k, pl.ds(r, 1)],
                                  sems.at[slot]).start(priority=p % 2)

    for parity in range(2):
        @pl.when((s < n) & (s % 2 == parity))
        def _():
            issue(parity)

    @pl.when(s > 0)
    def _():
        slot = (s - 1) % 2
        for k in range(TOP_K):
            pltpu.make_async_copy(ys_ref.at[pl.ds(0, tm)], buf.at[slot, k], sems.at[slot]).wait()
        acc = h1_ref[...]
        for k in range(TOP_K):
            acc = acc + gate_ref[:, k:k + 1] * buf[slot, k]
        o_ref[...] = _rms(acc, g_ref[...])


def _combine(dest, ys, h1, gate, fg):
    T = h1.shape[0]
    tm = TM_COMBINE
    return pl.pallas_call(
        _combine_kernel,
        grid_spec=pltpu.PrefetchScalarGridSpec(
            num_scalar_prefetch=1,
            grid=(T // tm + 1,),
            in_specs=[pl.BlockSpec(memory_space=pl.ANY),
                      pl.BlockSpec((tm, D_MODEL), lambda s, d: (jnp.maximum(s - 1, 0), 0)),
                      pl.BlockSpec((tm, TOP_K), lambda s, d: (jnp.maximum(s - 1, 0), 0)),
                      pl.BlockSpec((1, D_MODEL), lambda s, d: (0, 0))],
            out_specs=pl.BlockSpec((tm, D_MODEL), lambda s, d: (jnp.maximum(s - 1, 0), 0)),
            scratch_shapes=[pltpu.VMEM((2, TOP_K, tm, D_MODEL), F32),
                            pltpu.SemaphoreType.DMA((2,))]),
        out_shape=jax.ShapeDtypeStruct((T, D_MODEL), F32),
        compiler_params=pltpu.CompilerParams(
            dimension_semantics=("arbitrary",), vmem_limit_bytes=VMEM_LIMIT),
        name="combine",
    )(dest, ys, h1, gate, fg)


def _swap_halves(w):
    h = w.shape[-1] // 2
    return jnp.concatenate([w[..., h:], w[..., :h]], axis=-1)


def _layer(x2, batch, seq, norm1_g, w_in, w_alpha_up, b_alpha, gla_norm_g, swa_sinks, swa_norm_g,
           w_out, norm2_g, w_router, b_router, w_glu, b_glu, w_lin, b_lin, w_down, b_down):
    T = x2.shape[0]
    kb_w, vb_w = w_in[:, 2064:2192], w_in[:, 2192:2320]
    w_z = w_in[:, 1536:1552]
    w_cat = jnp.concatenate([
        w_in[:, 0:1536], w_in[:, 1552:2064],
        kb_w, _swap_halves(kb_w), vb_w, _swap_halves(vb_w),
        jnp.pad(jnp.tile(w_z, (1, Z_PIECES)), [(0, 0), (0, LANES - Z_PIECES * GLA_RANK)]),
    ], axis=1).astype(BF16)
    wup_hi = w_alpha_up.astype(BF16)
    wup_lo = (w_alpha_up - wup_hi.astype(F32)).astype(BF16)
    wup_cat = jnp.pad(jnp.concatenate([wup_hi, wup_hi, wup_hi, wup_lo, wup_lo], axis=0),
                      [(0, LANES - Z_PIECES * GLA_RANK), (0, 0)])

    qk, vr, la, qb, kv = _in_proj(
        x2, norm1_g.reshape(1, -1), w_cat, wup_cat, b_alpha.reshape(1, -1))
    oa = _gla(qk, vr, la, gla_norm_g.reshape(1, -1), batch, seq)
    ob = _swa(qb, kv, swa_sinks, jnp.tile(swa_norm_g, 2).reshape(1, -1), batch, seq)

    wo = w_out.astype(BF16)
    wr_t = w_router.T
    wr_hi = wr_t.astype(BF16)
    wr_lo = (wr_t - wr_hi.astype(F32)).astype(BF16)
    h1, hn, route, cnt = _out_route(oa, ob, x2, wo[:512], wo[512:], norm2_g.reshape(1, -1),
                                    wr_hi, wr_lo, b_router.reshape(-1, 1))

    tm = TM_EXPERT
    n_tiles = T * TOP_K // tm + N_EXPERTS
    counts = cnt[:, 0].astype(jnp.int32)
    padded = (counts + tm - 1) // tm * tm
    pends = jnp.cumsum(padded)
    pstarts = pends - padded
    top_idx = route[0:TOP_K].T.astype(jnp.int32)
    gate = route[TOP_K:2 * TOP_K].T
    rank = route[2 * TOP_K:3 * TOP_K].T.astype(jnp.int32)
    seg_start = jnp.sum(jnp.where(top_idx[..., None] == jnp.arange(N_EXPERTS), pstarts, 0), axis=-1)
    dest = (seg_start + rank).reshape(-1)
    n_used = (pends[-1] // tm).reshape(1)
    xs = _dispatch(dest, pstarts + counts, padded - counts, n_used, hn, n_tiles * tm)
    ys = _experts(pstarts // tm, padded // tm, xs, w_glu, b_glu, w_lin, b_lin, w_down, b_down)
    return dest, ys, h1, gate


def kernel(x, norm1_g, w_in, w_alpha_up, b_alpha, gla_norm_g, swa_sinks, swa_norm_g, w_out,
           norm2_g, w_router, b_router, w_glu, b_glu, w_lin, b_lin, w_down, b_down, final_g):
    batch, seq, d = x.shape
    assert norm1_g.shape[0] == 1, "single-layer problem"
    x2 = x.reshape(batch * seq, d)
    dest, ys, h1, gate = _layer(
        x2, batch, seq, norm1_g[0], w_in[0], w_alpha_up[0], b_alpha[0], gla_norm_g[0],
        swa_sinks[0], swa_norm_g[0], w_out[0], norm2_g[0], w_router[0], b_router[0],
        w_glu[0], b_glu[0], w_lin[0], b_lin[0], w_down[0], b_down[0])
    out = _combine(dest, ys, h1, gate, final_g.reshape(1, -1))
    return out.reshape(batch, seq, d)
```

```python
import numpy as np
import jax
import jax.numpy as jnp
from jax import lax
from jax.experimental import pallas as pl
from jax.experimental.pallas import tpu as pltpu

F32 = jnp.float32
BF16 = jnp.bfloat16

D_MODEL = 1024
GLA_HEADS = 4
GLA_DK = 64
GLA_DV = 128
GLA_RANK = 16
GLA_GATE_TAU = 16.0
GLA_CHUNK = 64
SWA_HEADS = 8
SWA_KV_HEADS = 2
SWA_HEAD_DIM = 64
SWA_WINDOW = 128
N_EXPERTS = 32
TOP_K = 4
D_FF = 1024
SWIGLU_LIMIT = 7.0
SWIGLU_ALPHA = 1.702
NORM_EPS = 1e-5

LANES = 128
SLAB_SUBLANES = D_MODEL // LANES
VMEM_LIMIT = 56 * 1024 * 1024

TM_PROJ = 512
TL_GLA = 1024
GLA_CUM_ROWS = 256
TM_ROUTE = 512
TM_EXPERT = 512
TM_COMBINE = 256
SWA_SUB = 4
TM_DISPATCH = 256

NEG_BIG = -1e30
ROUTE_ROWS = 16

C_QA, C_KA, C_VA, C_RA, C_QB, C_KB, C_KBS, C_VB, C_VBS, C_Z, C_END = (
    0, 256, 512, 1024, 1536, 2048, 2176, 2304, 2432, 2560, 2688)
Z_PIECES = 5


def _split3(x):
    hi = x.astype(BF16).astype(F32)
    r = x - hi
    mid = r.astype(BF16).astype(F32)
    lo = (r - mid).astype(BF16).astype(F32)
    return hi, mid, lo


def _rms(x, g):
    return x * lax.rsqrt(jnp.mean(x * x, axis=-1, keepdims=True) + NORM_EPS) * g


def _in_proj_kernel(x_ref, g_ref, w_ref, wup_ref, ba_ref,
                    qk_ref, vr_ref, la_ref, qb_ref, kv_ref):
    u = _rms(x_ref[...], g_ref[...]).astype(BF16)

    def proj(c0, c1):
        return jnp.dot(u, w_ref[:, c0:c1], preferred_element_type=F32)

    qk_ref[...] = proj(C_QA, C_VA).astype(BF16)
    vr_ref[:, :C_RA - C_VA] = proj(C_VA, C_RA).astype(BF16)
    vr_ref[:, C_RA - C_VA:] = proj(C_RA, C_QB).astype(BF16)
    qb_ref[...] = proj(C_QB, C_KB).astype(BF16)
    kv_ref[...] = proj(C_KB, C_Z).astype(BF16)
    z = proj(C_Z, C_END)
    hi, mid, lo = _split3(z)
    piece = lax.broadcasted_iota(jnp.int32, z.shape, 1) // GLA_RANK
    zc = jnp.where((piece == 0) | (piece == 3), hi, jnp.where(piece == 2, lo, mid)).astype(BF16)
    y = jnp.dot(zc, wup_ref[...], preferred_element_type=F32) + ba_ref[...]
    log_sig = jnp.minimum(y, 0.0) - jnp.log1p(jnp.exp(-jnp.abs(y)))
    la_ref[...] = log_sig * (1.0 / GLA_GATE_TAU)


def _in_proj(x2, g1, w_cat, wup_p, ba_p):
    T = x2.shape[0]
    tm = TM_PROJ
    row = lambda w: pl.BlockSpec((tm, w), lambda i: (i, 0))
    full = lambda a: pl.BlockSpec(a.shape, lambda i: (0,) * a.ndim)
    outs = [(512, BF16), (1024, BF16), (256, F32), (512, BF16), (512, BF16)]
    return pl.pallas_call(
        _in_proj_kernel,
        grid=(T // tm,),
        in_specs=[row(D_MODEL), full(g1), full(w_cat), full(wup_p), full(ba_p)],
        out_specs=[row(w) for w, _ in outs],
        out_shape=[jax.ShapeDtypeStruct((T, w), dt) for w, dt in outs],
        compiler_params=pltpu.CompilerParams(
            dimension_semantics=("arbitrary",), vmem_limit_bytes=VMEM_LIMIT),
        name="in_proj",
    )(x2, g1, w_cat, wup_p, ba_p)


def _gla_kernel(qk_ref, vr_ref, la_ref, g_ref, cum_ref, o_ref, st_ref):
    @pl.when(pl.program_id(1) == 0)
    def _():
        st_ref[...] = jnp.zeros_like(st_ref)

    tl = TL_GLA
    c = GLA_CHUNK
    kw = GLA_HEADS * GLA_DK
    vw = GLA_HEADS * GLA_DV
    causal = (lax.broadcasted_iota(jnp.int32, (c, c), 0)
              >= lax.broadcasted_iota(jnp.int32, (c, c), 1))
    low_half = lax.broadcasted_iota(jnp.int32, (c, LANES), 1) < GLA_DK
    g = g_ref[...]
    b_groups = []
    for grp in range(tl // GLA_CUM_ROWS):
        la = la_ref[grp * GLA_CUM_ROWS:(grp + 1) * GLA_CUM_ROWS, :]
        pieces = jnp.concatenate([p.astype(BF16) for p in _split3(la)], axis=1)
        b3 = jnp.dot(cum_ref[...], pieces, preferred_element_type=F32)
        b_groups.append(b3[:, :kw] + b3[:, kw:2 * kw] + b3[:, 2 * kw:])
    for ch in range(tl // c):
        rows = slice(ch * c, (ch + 1) * c)
        in_grp = (ch * c) % GLA_CUM_ROWS
        b = b_groups[(ch * c) // GLA_CUM_ROWS][in_grp:in_grp + c]
        b_last = b[c - 1:c]
        qf = qk_ref[rows, :kw].astype(F32)
        kf = qk_ref[rows, kw:].astype(F32)
        q_e = (qf * jnp.exp(b) * (GLA_DK ** -0.5)).astype(BF16)
        k_e = (kf * jnp.exp(-b)).astype(BF16)
        k_t = (kf * jnp.exp(b_last - b)).astype(BF16)
        decay = jnp.exp(b_last)
        for h in range(GLA_HEADS):
            ps = slice((h // 2) * LANES, (h // 2 + 1) * LANES)
            mine = low_half if h % 2 == 0 else ~low_half
            qp, kp = q_e[:, ps], k_e[:, ps]
            qh = jnp.where(mine, qp, jnp.zeros_like(qp))
            kth = jnp.where(mine, k_t[:, ps], jnp.zeros_like(qp))
            vs = slice(h * GLA_DV, (h + 1) * GLA_DV)
            vh = vr_ref[rows, vs]
            a = pl.dot(qh, kp, trans_b=True)
            a = jnp.where(causal, a, 0.0).astype(BF16)
            st = st_ref[h]
            o = (jnp.dot(a, vh, preferred_element_type=F32)
                 + pl.dot(qh, st.astype(BF16), trans_b=True))
            st_ref[h] = st * decay[:, ps] + pl.dot(vh, kth, trans_a=True)
            rh = vr_ref[rows, slice(vw + h * GLA_DV, vw + (h + 1) * GLA_DV)].astype(F32)
            o = _rms(o, g) * (rh * jax.nn.sigmoid(rh))
            o_ref[rows, vs] = o.astype(BF16)


def _gla(qk, vr, la, g, batch, seq):
    tl = TL_GLA
    nl = seq // tl
    row = lambda w: pl.BlockSpec((tl, w), lambda b, i: (b * nl + i, 0))
    r = np.arange(GLA_CUM_ROWS)
    cum = jnp.asarray((r[None, :] <= r[:, None])
                      & (r[None, :] // GLA_CHUNK == r[:, None] // GLA_CHUNK), BF16)
    return pl.pallas_call(
        _gla_kernel,
        grid=(batch, nl),
        in_specs=[row(512), row(1024), row(256),
                  pl.BlockSpec((1, GLA_DV), lambda b, i: (0, 0)),
                  pl.BlockSpec((GLA_CUM_ROWS, GLA_CUM_ROWS), lambda b, i: (0, 0))],
        out_specs=row(512),
        out_shape=jax.ShapeDtypeStruct((batch * seq, 512), BF16),
        scratch_shapes=[pltpu.VMEM((GLA_HEADS, GLA_DV, LANES), F32)],
        compiler_params=pltpu.CompilerParams(
            dimension_semantics=("arbitrary", "arbitrary"), vmem_limit_bytes=VMEM_LIMIT),
        name="gla",
    )(qk, vr, la, g, cum)


def _swa_kernel(sink_ref, bias0_ref, bias_ref, q_ref, kvc_ref, kvp_ref, g_ref, o_ref):
    w = SWA_WINDOW
    scale = jnp.asarray(SWA_HEAD_DIM ** -0.5, BF16)
    kv_lane = lax.broadcasted_iota(jnp.int32, (2 * w, LANES), 1)
    lane_lo = kv_lane < SWA_HEAD_DIM
    out_lo = lax.broadcasted_iota(jnp.int32, (w, LANES), 1) < SWA_HEAD_DIM
    ones_hi = jnp.where(kv_lane == SWA_HEAD_DIM, 1.0, 0.0).astype(BF16)
    ones_lo = jnp.where(kv_lane == 0, 1.0, 0.0).astype(BF16)
    g = g_ref[...]

    for sb in range(SWA_SUB):
        rows = slice(sb * w, (sb + 1) * w)
        if sb == 0:
            kv = jnp.concatenate([kvp_ref[...], kvc_ref[0:w, :]], axis=0)
            bias = bias0_ref
        else:
            kv = kvc_ref[(sb - 1) * w:(sb + 1) * w, :]
            bias = bias_ref
        k = kv[:, 0:LANES] * scale
        ks = kv[:, LANES:2 * LANES] * scale
        v, vs = kv[:, 2 * LANES:3 * LANES], kv[:, 3 * LANES:]
        zero = jnp.zeros_like(k)
        k_low = [jnp.where(lane_lo, k, zero), jnp.where(lane_lo, ks, zero)]
        k_high = [jnp.where(lane_lo, zero, ks), jnp.where(lane_lo, zero, k)]
        v_low = [jnp.where(lane_lo, v, zero) + ones_hi, jnp.where(lane_lo, vs, zero) + ones_hi]
        v_high = [jnp.where(lane_lo, zero, vs) + ones_lo, jnp.where(lane_lo, zero, v) + ones_lo]

        def weights(s, head):
            s = s + bias[0, head]
            sink = sink_ref[head]
            m = jnp.maximum(jnp.max(s, axis=-1, keepdims=True), sink)
            return jnp.exp(s - m).astype(BF16), jnp.exp(sink - m)

        for pair in range(SWA_HEADS // 2):
            j = (2 * pair) // (SWA_HEADS // SWA_KV_HEADS)
            cols = slice(pair * LANES, (pair + 1) * LANES)
            qp = q_ref[rows, cols]
            e0, sink0 = weights(pl.dot(qp, k_low[j], trans_b=True), 2 * pair)
            e1, sink1 = weights(pl.dot(qp, k_high[j], trans_b=True), 2 * pair + 1)
            oa = jnp.dot(e0, v_low[j], preferred_element_type=F32)
            ob = jnp.dot(e1, v_high[j], preferred_element_type=F32)
            inv0 = 1.0 / (oa[:, SWA_HEAD_DIM:SWA_HEAD_DIM + 1] + sink0)
            inv1 = 1.0 / (ob[:, 0:1] + sink1)
            o = jnp.where(out_lo, oa * inv0, ob * inv1)
            sq = o * o
            ms_lo = jnp.sum(jnp.where(out_lo, sq, 0.0), axis=-1, keepdims=True)
            ms_hi = jnp.sum(jnp.where(out_lo, 0.0, sq), axis=-1, keepdims=True)
            ms = jnp.where(out_lo, ms_lo, ms_hi) * (1.0 / SWA_HEAD_DIM)
            o = o * lax.rsqrt(ms + NORM_EPS) * g
            o_ref[rows, cols] = o.astype(BF16)


def _swa_bias():
    w = SWA_WINDOW
    slopes = 2.0 ** (-8.0 * np.arange(1, SWA_HEADS + 1, dtype=np.float64) / SWA_HEADS)
    rel = np.arange(w)[:, None] + w - np.arange(2 * w)[None, :]
    in_window = (rel >= 0) & (rel < w)
    exists = np.stack([np.arange(2 * w) >= w, np.ones(2 * w, bool)])
    valid = in_window[None] & exists[:, None, :]
    bias = -slopes[None, :, None, None] * rel[None, None].astype(np.float64)
    return np.where(valid[:, None], bias, NEG_BIG).astype(np.float32)


def _swa(qb, kv, sinks, g2, batch, seq):
    w = SWA_WINDOW
    rows = SWA_SUB * w
    nb = seq // rows
    cur = lambda: pl.BlockSpec((rows, 512), lambda b, n: (b * nb + n, 0))
    prev = pl.BlockSpec(
        (w, 512), lambda b, n: (jnp.maximum((b * nb + n) * SWA_SUB - 1, b * nb * SWA_SUB), 0))
    table = lambda index: pl.BlockSpec((1, SWA_HEADS, w, 2 * w), index)
    bias = jnp.asarray(_swa_bias())
    return pl.pallas_call(
        _swa_kernel,
        grid=(batch, nb),
        in_specs=[pl.BlockSpec(memory_space=pltpu.SMEM),
                  table(lambda b, n: (jnp.minimum(n, 1), 0, 0, 0)),
                  table(lambda b, n: (1, 0, 0, 0)),
                  cur(), cur(), prev,
                  pl.BlockSpec((1, LANES), lambda b, n: (0, 0))],
        out_specs=cur(),
        out_shape=jax.ShapeDtypeStruct((batch * seq, 512), BF16),
        compiler_params=pltpu.CompilerParams(
            dimension_semantics=("arbitrary", "arbitrary"), vmem_limit_bytes=VMEM_LIMIT),
        name="swa",
    )(sinks, bias, bias, qb, kv, kv, g2)


def _out_route_kernel(oa_ref, ob_ref, x_ref, woa_ref, wob_ref, g_ref, wrh_ref, wrl_ref, br_ref,
                      earlier_ref, h1_ref, hn3_ref, route_ref, cnt_ref, hbuf, hsem):
    i = pl.program_id(0)
    tm = TM_ROUTE

    @pl.when(i == 0)
    def _():
        cnt_ref[...] = jnp.zeros_like(cnt_ref)

    def hn_store(step, slot):
        return [pltpu.make_async_copy(hbuf.at[slot, :, pl.ds(c * LANES, LANES)],
                                      hn3_ref.at[pl.ds(step * tm, tm), c, :], hsem.at[slot])
                for c in range(SLAB_SUBLANES)]

    slot = i % 2

    @pl.when(i >= 2)
    def _():
        for cp in hn_store(i - 2, slot):
            cp.wait()

    h1 = (x_ref[...]
          + jnp.dot(oa_ref[...], woa_ref[...], preferred_element_type=F32)
          + jnp.dot(ob_ref[...], wob_ref[...], preferred_element_type=F32))
    h1_ref[...] = h1
    hn = _rms(h1, g_ref[...])
    hbuf[slot] = hn
    for cp in hn_store(i, slot):
        cp.start()

    @pl.when(i == pl.num_programs(0) - 1)
    def _():
        @pl.when(i >= 1)
        def _():
            for cp in hn_store(i - 1, 1 - slot):
                cp.wait()
        for cp in hn_store(i, slot):
            cp.wait()

    hn_hi = hn.astype(BF16)
    hn_lo = (hn - hn_hi.astype(F32)).astype(BF16)
    logits = (pl.dot(wrh_ref[...], hn_hi, trans_b=True)
              + pl.dot(wrh_ref[...], hn_lo, trans_b=True)
              + pl.dot(wrl_ref[...], hn_hi, trans_b=True)) + br_ref[...]

    eid = lax.broadcasted_iota(jnp.int32, (N_EXPERTS, tm), 0)
    work = logits
    vals, idxs, sels = [], [], []
    for _ in range(TOP_K):
        m = jnp.max(work, axis=0, keepdims=True)
        idx = jnp.min(jnp.where(work == m, eid, N_EXPERTS), axis=0, keepdims=True)
        sel = eid == idx
        vals.append(m)
        idxs.append(idx)
        sels.append(sel)
        work = jnp.where(sel, -3e38, work)
    exps = [jnp.exp(v - vals[0]) for v in vals]
    inv_den = 1.0 / (exps[0] + exps[1] + exps[2] + exps[3])

    multihot = jnp.where(sels[0] | sels[1] | sels[2] | sels[3], 1.0, 0.0)
    before = (jnp.dot(multihot.astype(BF16), earlier_ref[...], preferred_element_type=F32)
              + cnt_ref[:, 0:1])
    ranks = [jnp.sum(jnp.where(s, before, 0.0), axis=0, keepdims=True) for s in sels]
    route_ref[...] = jnp.concatenate(
        [i.astype(F32) for i in idxs] + [e * inv_den for e in exps] + ranks
        + [jnp.zeros((ROUTE_ROWS - 3 * TOP_K, tm), F32)], axis=0)
    cnt_ref[...] += jnp.sum(multihot, axis=1, keepdims=True)


def _out_route(oa, ob, x2, woa, wob, g2, wr_hi, wr_lo, br):
    T = x2.shape[0]
    tm = TM_ROUTE
    row = lambda w: pl.BlockSpec((tm, w), lambda i: (i, 0))
    full = lambda a: pl.BlockSpec(a.shape, lambda i: (0,) * a.ndim)
    earlier = jnp.asarray(np.triu(np.ones((tm, tm), np.float32), k=1), BF16)
    return pl.pallas_call(
        _out_route_kernel,
        grid=(T // tm,),
        in_specs=[row(512), row(512), row(D_MODEL), full(woa), full(wob), full(g2),
                  full(wr_hi), full(wr_lo), full(br), full(earlier)],
        out_specs=[row(D_MODEL), pl.BlockSpec(memory_space=pl.ANY),
                   pl.BlockSpec((ROUTE_ROWS, tm), lambda i: (0, i)),
                   pl.BlockSpec((N_EXPERTS, LANES), lambda i: (0, 0))],
        out_shape=[jax.ShapeDtypeStruct((T, D_MODEL), F32),
                   jax.ShapeDtypeStruct((T, SLAB_SUBLANES, LANES), F32),
                   jax.ShapeDtypeStruct((ROUTE_ROWS, T), F32),
                   jax.ShapeDtypeStruct((N_EXPERTS, LANES), F32)],
        scratch_shapes=[pltpu.VMEM((2, tm, D_MODEL), F32), pltpu.SemaphoreType.DMA((2,))],
        compiler_params=pltpu.CompilerParams(
            dimension_semantics=("arbitrary",), vmem_limit_bytes=VMEM_LIMIT,
            has_side_effects=True),
        name="out_route",
    )(oa, ob, x2, woa, wob, g2, wr_hi, wr_lo, br, earlier)


def _row_copy(src, s, dst, d, sem):
    return pltpu.make_async_copy(src.at[pl.ds(s, 1)], dst.at[pl.ds(d, 1)], sem)


def _dispatch_kernel(dest_ref, pad_start_ref, pad_cnt_ref, nused_ref, hn_ref, xs_ref,
                     zero_ref, sem, zsem):
    i = pl.program_id(0)
    n = pl.num_programs(0)
    tm = TM_EXPERT
    base = i * (TM_DISPATCH * TOP_K)
    for p in range(TM_DISPATCH * TOP_K):
        _row_copy(hn_ref, p // TOP_K, xs_ref, dest_ref[base + p], sem).start(priority=p % 2)
    for _ in range(TOP_K):
        pltpu.make_async_copy(hn_ref, xs_ref.at[pl.ds(0, TM_DISPATCH)], sem).wait()

    @pl.when(i == n - 1)
    def _():
        zero_ref[...] = jnp.zeros_like(zero_ref)
        sub = 8
        for e in range(N_EXPERTS):
            start, cnt = pad_start_ref[e], pad_cnt_ref[e]
            head = cnt & (sub - 1)
            body0 = start + head
            nbody = cnt // sub

            def head_copy(r):
                return _row_copy(zero_ref, 0, xs_ref, start + r, zsem)

            def body_copy(c):
                return pltpu.make_async_copy(zero_ref.at[pl.ds(0, sub)],
                                             xs_ref.at[pl.ds(body0 + c * sub, sub)], zsem)

            for copy, count in ((head_copy, head), (body_copy, nbody)):
                def zissue(r, carry, copy=copy):
                    copy(r).start()
                    return carry

                def zwait(r, carry, copy=copy):
                    copy(r).wait()
                    return carry

                lax.fori_loop(0, count, zissue, 0)
                lax.fori_loop(0, count, zwait, 0)

        def tail_copy(t):
            return pltpu.make_async_copy(zero_ref, xs_ref.at[pl.ds(t * tm, tm)], zsem)

        def tissue(t, carry):
            tail_copy(t).start()
            return carry

        def twait(t, carry):
            tail_copy(t).wait()
            return carry

        n_tiles = xs_ref.shape[0] // tm
        lax.fori_loop(nused_ref[0], n_tiles, tissue, 0)
        lax.fori_loop(nused_ref[0], n_tiles, twait, 0)


def _dispatch(dest, pad_start, pad_cnt, n_used, hn, n_rows):
    T = hn.shape[0]
    any_spec = pl.BlockSpec(memory_space=pl.ANY)
    return pl.pallas_call(
        _dispatch_kernel,
        grid_spec=pltpu.PrefetchScalarGridSpec(
            num_scalar_prefetch=4,
            grid=(T // TM_DISPATCH,),
            in_specs=[pl.BlockSpec((TM_DISPATCH, SLAB_SUBLANES, LANES), lambda i, *_: (i, 0, 0))],
            out_specs=any_spec,
            scratch_shapes=[pltpu.VMEM((TM_EXPERT, SLAB_SUBLANES, LANES), F32),
                            pltpu.SemaphoreType.DMA(()),
                            pltpu.SemaphoreType.DMA(())]),
        out_shape=jax.ShapeDtypeStruct((n_rows, SLAB_SUBLANES, LANES), F32),
        compiler_params=pltpu.CompilerParams(
            dimension_semantics=("arbitrary",), has_side_effects=True,
            vmem_limit_bytes=VMEM_LIMIT),
        name="dispatch",
    )(dest, pad_start, pad_cnt, n_used, hn)


def _experts_kernel(tile0_ref, ntile_ref, xs_ref, wg_ref, bg_ref, wl_ref, bl_ref, wd_ref, bd_ref,
                    ys_ref, wg_bf, wl_bf, wd_bf, xbuf, ybuf, xsem, ysem):
    e = pl.program_id(0)
    tm = TM_EXPERT
    t0, nt = tile0_ref[e], ntile_ref[e]

    class x_load:
        def __init__(self, tile, slot):
            self.copies = [
                pltpu.make_async_copy(xs_ref.at[pl.ds(tile * tm, tm), c, :],
                                      xbuf.at[slot, :, pl.ds(c * LANES, LANES)], xsem.at[slot])
                for c in range(SLAB_SUBLANES)]

        def start(self):
            for cp in self.copies:
                cp.start()

        def wait(self):
            for cp in self.copies:
                cp.wait()

    def y_store(tile, slot):
        return pltpu.make_async_copy(ybuf.at[slot], ys_ref.at[pl.ds(tile * tm, tm)], ysem.at[slot])

    @pl.when((e == 0) & (nt > 0))
    def _():
        x_load(t0, 0).start()

    @pl.when(nt > 0)
    def _():
        wg_bf[...] = wg_ref[0].astype(BF16)
        wl_bf[...] = wl_ref[0].astype(BF16)
        wd_bf[...] = wd_ref[0].astype(BF16)

    def tile_body(j, carry):
        slot = j % 2

        @pl.when(j + 1 < nt)
        def _():
            x_load(t0 + j + 1, 1 - slot).start()

        x_load(t0 + j, slot).wait()

        @pl.when(j >= 2)
        def _():
            y_store(t0 + j - 2, slot).wait()

        x = xbuf[slot].astype(BF16)
        glu = jnp.minimum(jnp.dot(x, wg_bf[...], preferred_element_type=F32) + bg_ref[0],
                          SWIGLU_LIMIT)
        lin = jnp.clip(jnp.dot(x, wl_bf[...], preferred_element_type=F32) + bl_ref[0],
                       -SWIGLU_LIMIT, SWIGLU_LIMIT)
        hid = glu * jax.nn.sigmoid(SWIGLU_ALPHA * glu) * (lin + 1.0)
        ybuf[slot] = jnp.dot(hid.astype(BF16), wd_bf[...], preferred_element_type=F32) + bd_ref[0]
        y_store(t0 + j, slot).start()
        return carry

    lax.fori_loop(0, nt, tile_body, 0)

    @pl.when(e + 1 < pl.num_programs(0))
    def _():
        @pl.when(ntile_ref[e + 1] > 0)
        def _():
            x_load(tile0_ref[e + 1], 0).start()

    @pl.when(nt >= 2)
    def _():
        y_store(t0 + nt - 2, nt % 2).wait()

    @pl.when(nt >= 1)
    def _():
        y_store(t0 + nt - 1, (nt - 1) % 2).wait()

    @pl.when(e == pl.num_programs(0) - 1)
    def _():
        ybuf[0] = jnp.zeros((tm, D_MODEL), F32)
        n_tiles = ys_ref.shape[0] // tm

        def tail_issue(t, carry):
            y_store(t, 0).start()
            return carry

        def tail_wait(t, carry):
            y_store(t, 0).wait()
            return carry

        lax.fori_loop(t0 + nt, n_tiles, tail_issue, 0)
        lax.fori_loop(t0 + nt, n_tiles, tail_wait, 0)


def _experts(tile0, ntile, xs, w_glu, b_glu, w_lin, b_lin, w_down, b_down):
    tm = TM_EXPERT
    any_spec = pl.BlockSpec(memory_space=pl.ANY)
    wspec = lambda: pl.BlockSpec((1, D_MODEL, D_FF), lambda e, *_: (e, 0, 0))
    bspec = lambda: pl.BlockSpec((1, 1, D_FF), lambda e, *_: (e, 0, 0))
    return pl.pallas_call(
        _experts_kernel,
        grid_spec=pltpu.PrefetchScalarGridSpec(
            num_scalar_prefetch=2,
            grid=(N_EXPERTS,),
            in_specs=[any_spec, wspec(), bspec(), wspec(), bspec(), wspec(), bspec()],
            out_specs=any_spec,
            scratch_shapes=[pltpu.VMEM((D_MODEL, D_FF), BF16)] * 3
            + [pltpu.VMEM((2, tm, D_MODEL), F32), pltpu.VMEM((2, tm, D_MODEL), F32),
               pltpu.SemaphoreType.DMA((2,)), pltpu.SemaphoreType.DMA((2,))]),
        out_shape=jax.ShapeDtypeStruct((xs.shape[0], D_MODEL), F32),
        compiler_params=pltpu.CompilerParams(
            dimension_semantics=("arbitrary",), vmem_limit_bytes=VMEM_LIMIT,
            has_side_effects=True),
        name="experts",
    )(tile0, ntile, xs, w_glu, b_glu.reshape(N_EXPERTS, 1, D_FF),
      w_lin, b_lin.reshape(N_EXPERTS, 1, D_FF), w_down, b_down.reshape(N_EXPERTS, 1, D_MODEL))


def _combine_kernel(dest_ref, ys_ref, h1_ref, gate_ref, g_ref, o_ref, buf, sems):
    s = pl.program_id(0)
    n = pl.num_programs(0) - 1
    tm = TM_COMBINE

    def issue(slot):
        base = s * (tm * TOP_K)
        for p in range(tm * TOP_K):
            r, k = divmod(p, TOP_K)
            pltpu.make_async_copy(ys_ref.at[pl.ds(dest_ref[base + p], 1)],
                                  buf.at[slot, k, pl.ds(r, 1)],
                                  sems.at[slot]).start(priority=p % 2)

    for parity in range(2):
        @pl.when((s < n) & (s % 2 == parity))
        def _():
            issue(parity)

    @pl.when(s > 0)
    def _():
        slot = (s - 1) % 2
        for k in range(TOP_K):
            pltpu.make_async_copy(ys_ref.at[pl.ds(0, tm)], buf.at[slot, k], sems.at[slot]).wait()
        acc = h1_ref[...]
        for k in range(TOP_K):
            acc = acc + gate_ref[:, k:k + 1] * buf[slot, k]
        o_ref[...] = _rms(acc, g_ref[...])


def _combine(dest, ys, h1, gate, fg):
    T = h1.shape[0]
    tm = TM_COMBINE
    return pl.pallas_call(
        _combine_kernel,
        grid_spec=pltpu.PrefetchScalarGridSpec(
            num_scalar_prefetch=1,
            grid=(T // tm + 1,),
            in_specs=[pl.BlockSpec(memory_space=pl.ANY),
                      pl.BlockSpec((tm, D_MODEL), lambda s, d: (jnp.maximum(s - 1, 0), 0)),
                      pl.BlockSpec((tm, TOP_K), lambda s, d: (jnp.maximum(s - 1, 0), 0)),
                      pl.BlockSpec((1, D_MODEL), lambda s, d: (0, 0))],
            out_specs=pl.BlockSpec((tm, D_MODEL), lambda s, d: (jnp.maximum(s - 1, 0), 0)),
            scratch_shapes=[pltpu.VMEM((2, TOP_K, tm, D_MODEL), F32),
                            pltpu.SemaphoreType.DMA((2,))]),
        out_shape=jax.ShapeDtypeStruct((T, D_MODEL), F32),
        compiler_params=pltpu.CompilerParams(
            dimension_semantics=("arbitrary",), vmem_limit_bytes=VMEM_LIMIT),
        name="combine",
    )(dest, ys, h1, gate, fg)


def _swap_halves(w):
    h = w.shape[-1] // 2
    return jnp.concatenate([w[..., h:], w[..., :h]], axis=-1)


def _layer(x2, batch, seq, norm1_g, w_in, w_alpha_up, b_alpha, gla_norm_g, swa_sinks, swa_norm_g,
           w_out, norm2_g, w_router, b_router, w_glu, b_glu, w_lin, b_lin, w_down, b_down):
    T = x2.shape[0]
    kb_w, vb_w = w_in[:, 2064:2192], w_in[:, 2192:2320]
    w_z = w_in[:, 1536:1552]
    w_cat = jnp.concatenate([
        w_in[:, 0:1536], w_in[:, 1552:2064],
        kb_w, _swap_halves(kb_w), vb_w, _swap_halves(vb_w),
        jnp.pad(jnp.tile(w_z, (1, Z_PIECES)), [(0, 0), (0, LANES - Z_PIECES * GLA_RANK)]),
    ], axis=1).astype(BF16)
    wup_hi = w_alpha_up.astype(BF16)
    wup_lo = (w_alpha_up - wup_hi.astype(F32)).astype(BF16)
    wup_cat = jnp.pad(jnp.concatenate([wup_hi, wup_hi, wup_hi, wup_lo, wup_lo], axis=0),
                      [(0, LANES - Z_PIECES * GLA_RANK), (0, 0)])

    qk, vr, la, qb, kv = _in_proj(
        x2, norm1_g.reshape(1, -1), w_cat, wup_cat, b_alpha.reshape(1, -1))
    oa = _gla(qk, vr, la, gla_norm_g.reshape(1, -1), batch, seq)
    ob = _swa(qb, kv, swa_sinks, jnp.tile(swa_norm_g, 2).reshape(1, -1), batch, seq)

    wo = w_out.astype(BF16)
    wr_t = w_router.T
    wr_hi = wr_t.astype(BF16)
    wr_lo = (wr_t - wr_hi.astype(F32)).astype(BF16)
    h1, hn, route, cnt = _out_route(oa, ob, x2, wo[:512], wo[512:], norm2_g.reshape(1, -1),
                                    wr_hi, wr_lo, b_router.reshape(-1, 1))

    tm = TM_EXPERT
    n_tiles = T * TOP_K // tm + N_EXPERTS
    counts = cnt[:, 0].astype(jnp.int32)
    padded = (counts + tm - 1) // tm * tm
    pends = jnp.cumsum(padded)
    pstarts = pends - padded
    top_idx = route[0:TOP_K].T.astype(jnp.int32)
    gate = route[TOP_K:2 * TOP_K].T
    rank = route[2 * TOP_K:3 * TOP_K].T.astype(jnp.int32)
    seg_start = jnp.sum(jnp.where(top_idx[..., None] == jnp.arange(N_EXPERTS), pstarts, 0), axis=-1)
    dest = (seg_start + rank).reshape(-1)
    n_used = (pends[-1] // tm).reshape(1)
    xs = _dispatch(dest, pstarts + counts, padded - counts, n_used, hn, n_tiles * tm)
    ys = _experts(pstarts // tm, padded // tm, xs, w_glu, b_glu, w_lin, b_lin, w_down, b_down)
    return dest, ys, h1, gate


def kernel(x, norm1_g, w_in, w_alpha_up, b_alpha, gla_norm_g, swa_sinks, swa_norm_g, w_out,
           norm2_g, w_router, b_router, w_glu, b_glu, w_lin, b_lin, w_down, b_down, final_g):
    batch, seq, d = x.shape
    assert norm1_g.shape[0] == 1, "single-layer problem"
    x2 = x.reshape(batch * seq, d)
    dest, ys, h1, gate = _layer(
        x2, batch, seq, norm1_g[0], w_in[0], w_alpha_up[0], b_alpha[0], gla_norm_g[0],
        swa_sinks[0], swa_norm_g[0], w_out[0], norm2_g[0], w_router[0], b_router[0],
        w_glu[0], b_glu[0], w_lin[0], b_lin[0], w_down[0], b_down[0])
    out = _combine(dest, ys, h1, gate, final_g.reshape(1, -1))
    return out.reshape(batch, seq, d)
```

```python
import numpy as np
import jax
import jax.numpy as jnp
from jax import lax
from jax.experimental import pallas as pl
from jax.experimental.pallas import tpu as pltpu

F32 = jnp.float32
BF16 = jnp.bfloat16

D_MODEL = 1024
GLA_HEADS = 4
GLA_DK = 64
GLA_DV = 128
GLA_RANK = 16
GLA_GATE_TAU = 16.0
GLA_CHUNK = 64
SWA_HEADS = 8
SWA_KV_HEADS = 2
SWA_HEAD_DIM = 64
SWA_WINDOW = 128
N_EXPERTS = 32
TOP_K = 4
D_FF = 1024
SWIGLU_LIMIT = 7.0
SWIGLU_ALPHA = 1.702
NORM_EPS = 1e-5

LANES = 128
SLAB_SUBLANES = D_MODEL // LANES
VMEM_LIMIT = 56 * 1024 * 1024

TM_PROJ = 512
TL_GLA = 1024
GLA_CUM_ROWS = 256
TM_ROUTE = 512
TM_EXPERT = 512
TM_COMBINE = 256
SWA_SUB = 4
TM_DISPATCH = 256

NEG_BIG = -1e30
ROUTE_ROWS = 16

C_QA, C_KA, C_VA, C_RA, C_QB, C_KB, C_KBS, C_VB, C_VBS, C_Z, C_END = (
    0, 256, 512, 1024, 1536, 2048, 2176, 2304, 2432, 2560, 2688)
Z_PIECES = 5


def _split3(x):
    hi = x.astype(BF16).astype(F32)
    r = x - hi
    mid = r.astype(BF16).astype(F32)
    lo = (r - mid).astype(BF16).astype(F32)
    return hi, mid, lo


class _SlabTileCopy:
    def __init__(self, flat, slab, sem, to_slab):
        self.copies = []
        for c in range(SLAB_SUBLANES):
            pair = (flat.at[:, pl.ds(c * LANES, LANES)], slab.at[:, c, :])
            src, dst = pair if to_slab else pair[::-1]
            self.copies.append(pltpu.make_async_copy(src, dst, sem))

    def start(self):
        for cp in self.copies:
            cp.start()

    def wait(self):
        for cp in self.copies:
            cp.wait()


def _rms(x, g):
    return x * lax.rsqrt(jnp.mean(x * x, axis=-1, keepdims=True) + NORM_EPS) * g


def _in_proj_kernel(x_ref, g_ref, w_ref, wup_ref, ba_ref,
                    qk_ref, vr_ref, la_ref, qb_ref, kv_ref):
    u = _rms(x_ref[...], g_ref[...]).astype(BF16)

    def proj(c0, c1):
        return jnp.dot(u, w_ref[:, c0:c1], preferred_element_type=F32)

    qk_ref[...] = proj(C_QA, C_VA).astype(BF16)
    vr_ref[:, :C_RA - C_VA] = proj(C_VA, C_RA).astype(BF16)
    vr_ref[:, C_RA - C_VA:] = proj(C_RA, C_QB).astype(BF16)
    qb_ref[...] = proj(C_QB, C_KB).astype(BF16)
    kv_ref[...] = proj(C_KB, C_Z).astype(BF16)
    z = proj(C_Z, C_END)
    hi, mid, lo = _split3(z)
    piece = lax.broadcasted_iota(jnp.int32, z.shape, 1) // GLA_RANK
    zc = jnp.where((piece == 0) | (piece == 3), hi, jnp.where(piece == 2, lo, mid)).astype(BF16)
    y = jnp.dot(zc, wup_ref[...], preferred_element_type=F32) + ba_ref[...]
    log_sig = jnp.minimum(y, 0.0) - jnp.log1p(jnp.exp(-jnp.abs(y)))
    la_ref[...] = log_sig * (1.0 / GLA_GATE_TAU)


def _in_proj(x2, g1, w_cat, wup_p, ba_p):
    T = x2.shape[0]
    tm = TM_PROJ
    row = lambda w: pl.BlockSpec((tm, w), lambda i: (i, 0))
    full = lambda a: pl.BlockSpec(a.shape, lambda i: (0,) * a.ndim)
    outs = [(512, BF16), (1024, BF16), (256, F32), (512, BF16), (512, BF16)]
    return pl.pallas_call(
        _in_proj_kernel,
        grid=(T // tm,),
        in_specs=[row(D_MODEL), full(g1), full(w_cat), full(wup_p), full(ba_p)],
        out_specs=[row(w) for w, _ in outs],
        out_shape=[jax.ShapeDtypeStruct((T, w), dt) for w, dt in outs],
        compiler_params=pltpu.CompilerParams(
            dimension_semantics=("arbitrary",), vmem_limit_bytes=VMEM_LIMIT),
        name="in_proj",
    )(x2, g1, w_cat, wup_p, ba_p)


def _gla_kernel(qk_ref, vr_ref, la_ref, g_ref, cum_ref, o_ref, st_ref):
    @pl.when(pl.program_id(1) == 0)
    def _():
        st_ref[...] = jnp.zeros_like(st_ref)

    tl = TL_GLA
    c = GLA_CHUNK
    kw = GLA_HEADS * GLA_DK
    vw = GLA_HEADS * GLA_DV
    causal = (lax.broadcasted_iota(jnp.int32, (c, c), 0)
              >= lax.broadcasted_iota(jnp.int32, (c, c), 1))
    low_half = lax.broadcasted_iota(jnp.int32, (c, LANES), 1) < GLA_DK
    g = g_ref[...]
    b_groups = []
    for grp in range(tl // GLA_CUM_ROWS):
        la = la_ref[grp * GLA_CUM_ROWS:(grp + 1) * GLA_CUM_ROWS, :]
        pieces = jnp.concatenate([p.astype(BF16) for p in _split3(la)], axis=1)
        b3 = jnp.dot(cum_ref[...], pieces, preferred_element_type=F32)
        b_groups.append(b3[:, :kw] + b3[:, kw:2 * kw] + b3[:, 2 * kw:])
    for ch in range(tl // c):
        rows = slice(ch * c, (ch + 1) * c)
        in_grp = (ch * c) % GLA_CUM_ROWS
        b = b_groups[(ch * c) // GLA_CUM_ROWS][in_grp:in_grp + c]
        b_last = b[c - 1:c]
        qf = qk_ref[rows, :kw].astype(F32)
        kf = qk_ref[rows, kw:].astype(F32)
        q_e = (qf * jnp.exp(b) * (GLA_DK ** -0.5)).astype(BF16)
        k_e = (kf * jnp.exp(-b)).astype(BF16)
        k_t = (kf * jnp.exp(b_last - b)).astype(BF16)
        decay = jnp.exp(b_last)
        for h in range(GLA_HEADS):
            ps = slice((h // 2) * LANES, (h // 2 + 1) * LANES)
            mine = low_half if h % 2 == 0 else ~low_half
            qp, kp = q_e[:, ps], k_e[:, ps]
            qh = jnp.where(mine, qp, jnp.zeros_like(qp))
            kth = jnp.where(mine, k_t[:, ps], jnp.zeros_like(qp))
            vs = slice(h * GLA_DV, (h + 1) * GLA_DV)
            vh = vr_ref[rows, vs]
            a = pl.dot(qh, kp, trans_b=True)
            a = jnp.where(causal, a, 0.0).astype(BF16)
            st = st_ref[h]
            o = (jnp.dot(a, vh, preferred_element_type=F32)
                 + pl.dot(qh, st.astype(BF16), trans_b=True))
            st_ref[h] = st * decay[:, ps] + pl.dot(vh, kth, trans_a=True)
            rh = vr_ref[rows, slice(vw + h * GLA_DV, vw + (h + 1) * GLA_DV)].astype(F32)
            o = _rms(o, g) * (rh * jax.nn.sigmoid(rh))
            o_ref[rows, vs] = o.astype(BF16)


def _gla(qk, vr, la, g, batch, seq):
    tl = TL_GLA
    nl = seq // tl
    row = lambda w: pl.BlockSpec((tl, w), lambda b, i: (b * nl + i, 0))
    r = np.arange(GLA_CUM_ROWS)
    cum = jnp.asarray((r[None, :] <= r[:, None])
                      & (r[None, :] // GLA_CHUNK == r[:, None] // GLA_CHUNK), BF16)
    return pl.pallas_call(
        _gla_kernel,
        grid=(batch, nl),
        in_specs=[row(512), row(1024), row(256),
                  pl.BlockSpec((1, GLA_DV), lambda b, i: (0, 0)),
                  pl.BlockSpec((GLA_CUM_ROWS, GLA_CUM_ROWS), lambda b, i: (0, 0))],
        out_specs=row(512),
        out_shape=jax.ShapeDtypeStruct((batch * seq, 512), BF16),
        scratch_shapes=[pltpu.VMEM((GLA_HEADS, GLA_DV, LANES), F32)],
        compiler_params=pltpu.CompilerParams(
            dimension_semantics=("arbitrary", "arbitrary"), vmem_limit_bytes=VMEM_LIMIT),
        name="gla",
    )(qk, vr, la, g, cum)


def _swa_kernel(sink_ref, bias0_ref, bias_ref, q_ref, kvc_ref, kvp_ref, g_ref, o_ref):
    w = SWA_WINDOW
    scale = jnp.asarray(SWA_HEAD_DIM ** -0.5, BF16)
    kv_lane = lax.broadcasted_iota(jnp.int32, (2 * w, LANES), 1)
    lane_lo = kv_lane < SWA_HEAD_DIM
    out_lo = lax.broadcasted_iota(jnp.int32, (w, LANES), 1) < SWA_HEAD_DIM
    ones_hi = jnp.where(kv_lane == SWA_HEAD_DIM, 1.0, 0.0).astype(BF16)
    ones_lo = jnp.where(kv_lane == 0, 1.0, 0.0).astype(BF16)
    g = g_ref[...]

    for sb in range(SWA_SUB):
        rows = slice(sb * w, (sb + 1) * w)
        if sb == 0:
            kv = jnp.concatenate([kvp_ref[...], kvc_ref[0:w, :]], axis=0)
            bias = bias0_ref
        else:
            kv = kvc_ref[(sb - 1) * w:(sb + 1) * w, :]
            bias = bias_ref
        k = kv[:, 0:LANES] * scale
        ks = kv[:, LANES:2 * LANES] * scale
        v, vs = kv[:, 2 * LANES:3 * LANES], kv[:, 3 * LANES:]
        zero = jnp.zeros_like(k)
        k_low = [jnp.where(lane_lo, k, zero), jnp.where(lane_lo, ks, zero)]
        k_high = [jnp.where(lane_lo, zero, ks), jnp.where(lane_lo, zero, k)]
        v_low = [jnp.where(lane_lo, v, zero) + ones_hi, jnp.where(lane_lo, vs, zero) + ones_hi]
        v_high = [jnp.where(lane_lo, zero, vs) + ones_lo, jnp.where(lane_lo, zero, v) + ones_lo]

        def weights(s, head):
            s = s + bias[0, head]
            sink = sink_ref[head]
            m = jnp.maximum(jnp.max(s, axis=-1, keepdims=True), sink)
            return jnp.exp(s - m).astype(BF16), jnp.exp(sink - m)

        for pair in range(SWA_HEADS // 2):
            j = (2 * pair) // (SWA_HEADS // SWA_KV_HEADS)
            cols = slice(pair * LANES, (pair + 1) * LANES)
            qp = q_ref[rows, cols]
            e0, sink0 = weights(pl.dot(qp, k_low[j], trans_b=True), 2 * pair)
            e1, sink1 = weights(pl.dot(qp, k_high[j], trans_b=True), 2 * pair + 1)
            oa = jnp.dot(e0, v_low[j], preferred_element_type=F32)
            ob = jnp.dot(e1, v_high[j], preferred_element_type=F32)
            inv0 = 1.0 / (oa[:, SWA_HEAD_DIM:SWA_HEAD_DIM + 1] + sink0)
            inv1 = 1.0 / (ob[:, 0:1] + sink1)
            o = jnp.where(out_lo, oa * inv0, ob * inv1)
            sq = o * o
            ms_lo = jnp.sum(jnp.where(out_lo, sq, 0.0), axis=-1, keepdims=True)
            ms_hi = jnp.sum(jnp.where(out_lo, 0.0, sq), axis=-1, keepdims=True)
            ms = jnp.where(out_lo, ms_lo, ms_hi) * (1.0 / SWA_HEAD_DIM)
            o = o * lax.rsqrt(ms + NORM_EPS) * g
            o_ref[rows, cols] = o.astype(BF16)


def _swa_bias():
    w = SWA_WINDOW
    slopes = 2.0 ** (-8.0 * np.arange(1, SWA_HEADS + 1, dtype=np.float64) / SWA_HEADS)
    rel = np.arange(w)[:, None] + w - np.arange(2 * w)[None, :]
    in_window = (rel >= 0) & (rel < w)
    exists = np.stack([np.arange(2 * w) >= w, np.ones(2 * w, bool)])
    valid = in_window[None] & exists[:, None, :]
    bias = -slopes[None, :, None, None] * rel[None, None].astype(np.float64)
    return np.where(valid[:, None], bias, NEG_BIG).astype(np.float32)


def _swa(qb, kv, sinks, g2, batch, seq):
    w = SWA_WINDOW
    rows = SWA_SUB * w
    nb = seq // rows
    cur = lambda: pl.BlockSpec((rows, 512), lambda b, n: (b * nb + n, 0))
    prev = pl.BlockSpec(
        (w, 512), lambda b, n: (jnp.maximum((b * nb + n) * SWA_SUB - 1, b * nb * SWA_SUB), 0))
    table = lambda index: pl.BlockSpec((1, SWA_HEADS, w, 2 * w), index)
    bias = jnp.asarray(_swa_bias())
    return pl.pallas_call(
        _swa_kernel,
        grid=(batch, nb),
        in_specs=[pl.BlockSpec(memory_space=pltpu.SMEM),
                  table(lambda b, n: (jnp.minimum(n, 1), 0, 0, 0)),
                  table(lambda b, n: (1, 0, 0, 0)),
                  cur(), cur(), prev,
                  pl.BlockSpec((1, LANES), lambda b, n: (0, 0))],
        out_specs=cur(),
        out_shape=jax.ShapeDtypeStruct((batch * seq, 512), BF16),
        compiler_params=pltpu.CompilerParams(
            dimension_semantics=("arbitrary", "arbitrary"), vmem_limit_bytes=VMEM_LIMIT),
        name="swa",
    )(sinks, bias, bias, qb, kv, kv, g2)


def _out_route_kernel(oa_ref, ob_ref, x_ref, woa_ref, wob_ref, g_ref, wrh_ref, wrl_ref, br_ref,
                      earlier_ref, h13_ref, hn3_ref, route_ref, cnt_ref, hbuf, hsem):
    i = pl.program_id(0)
    tm = TM_ROUTE

    @pl.when(i == 0)
    def _():
        cnt_ref[...] = jnp.zeros_like(cnt_ref)

    def stores(step, slot):
        rows = pl.ds(step * tm, tm)
        return [_SlabTileCopy(hbuf.at[slot, which], out.at[rows], hsem.at[slot], to_slab=True)
                for which, out in enumerate((h13_ref, hn3_ref))]

    slot = i % 2

    @pl.when(i >= 2)
    def _():
        for st in stores(i - 2, slot):
            st.wait()

    h1 = (x_ref[...]
          + jnp.dot(oa_ref[...], woa_ref[...], preferred_element_type=F32)
          + jnp.dot(ob_ref[...], wob_ref[...], preferred_element_type=F32))
    hn = _rms(h1, g_ref[...])
    hbuf[slot, 0] = h1
    hbuf[slot, 1] = hn
    for st in stores(i, slot):
        st.start()

    @pl.when(i == pl.num_programs(0) - 1)
    def _():
        @pl.when(i >= 1)
        def _():
            for st in stores(i - 1, 1 - slot):
                st.wait()
        for st in stores(i, slot):
            st.wait()

    hn_hi = hn.astype(BF16)
    hn_lo = (hn - hn_hi.astype(F32)).astype(BF16)
    logits = (pl.dot(wrh_ref[...], hn_hi, trans_b=True)
              + pl.dot(wrh_ref[...], hn_lo, trans_b=True)
              + pl.dot(wrl_ref[...], hn_hi, trans_b=True)) + br_ref[...]

    eid = lax.broadcasted_iota(jnp.int32, (N_EXPERTS, tm), 0)
    work = logits
    vals, idxs, sels = [], [], []
    for _ in range(TOP_K):
        m = jnp.max(work, axis=0, keepdims=True)
        idx = jnp.min(jnp.where(work == m, eid, N_EXPERTS), axis=0, keepdims=True)
        sel = eid == idx
        vals.append(m)
        idxs.append(idx)
        sels.append(sel)
        work = jnp.where(sel, -3e38, work)
    exps = [jnp.exp(v - vals[0]) for v in vals]
    inv_den = 1.0 / (exps[0] + exps[1] + exps[2] + exps[3])

    multihot = jnp.where(sels[0] | sels[1] | sels[2] | sels[3], 1.0, 0.0)
    before = (jnp.dot(multihot.astype(BF16), earlier_ref[...], preferred_element_type=F32)
              + cnt_ref[:, 0:1])
    ranks = [jnp.sum(jnp.where(s, before, 0.0), axis=0, keepdims=True) for s in sels]
    route_ref[...] = jnp.concatenate(
        [i.astype(F32) for i in idxs] + [e * inv_den for e in exps] + ranks
        + [jnp.zeros((ROUTE_ROWS - 3 * TOP_K, tm), F32)], axis=0)
    cnt_ref[...] += jnp.sum(multihot, axis=1, keepdims=True)


def _out_route(oa, ob, x2, woa, wob, g2, wr_hi, wr_lo, br):
    T = x2.shape[0]
    tm = TM_ROUTE
    row = lambda w: pl.BlockSpec((tm, w), lambda i: (i, 0))
    full = lambda a: pl.BlockSpec(a.shape, lambda i: (0,) * a.ndim)
    earlier = jnp.asarray(np.triu(np.ones((tm, tm), np.float32), k=1), BF16)
    return pl.pallas_call(
        _out_route_kernel,
        grid=(T // tm,),
        in_specs=[row(512), row(512), row(D_MODEL), full(woa), full(wob), full(g2),
                  full(wr_hi), full(wr_lo), full(br), full(earlier)],
        out_specs=[pl.BlockSpec(memory_space=pl.ANY), pl.BlockSpec(memory_space=pl.ANY),
                   pl.BlockSpec((ROUTE_ROWS, tm), lambda i: (0, i)),
                   pl.BlockSpec((N_EXPERTS, LANES), lambda i: (0, 0))],
        out_shape=[jax.ShapeDtypeStruct((T, SLAB_SUBLANES, LANES), F32),
                   jax.ShapeDtypeStruct((T, SLAB_SUBLANES, LANES), F32),
                   jax.ShapeDtypeStruct((ROUTE_ROWS, T), F32),
                   jax.ShapeDtypeStruct((N_EXPERTS, LANES), F32)],
        scratch_shapes=[pltpu.VMEM((2, 2, tm, D_MODEL), F32), pltpu.SemaphoreType.DMA((2,))],
        compiler_params=pltpu.CompilerParams(
            dimension_semantics=("arbitrary",), vmem_limit_bytes=VMEM_LIMIT,
            has_side_effects=True),
        name="out_route",
    )(oa, ob, x2, woa, wob, g2, wr_hi, wr_lo, br, earlier)


def _row_copy(src, s, dst, d, sem):
    return pltpu.make_async_copy(src.at[pl.ds(s, 1)], dst.at[pl.ds(d, 1)], sem)


def _dispatch_kernel(dest_ref, pad_start_ref, pad_cnt_ref, nused_ref, hn_ref, xs_ref,
                     zero_ref, sem, zsem):
    i = pl.program_id(0)
    n = pl.num_programs(0)
    tm = TM_EXPERT
    base = i * (TM_DISPATCH * TOP_K)
    for p in range(TM_DISPATCH * TOP_K):
        _row_copy(hn_ref, p // TOP_K, xs_ref, dest_ref[base + p], sem).start(priority=p % 2)
    for _ in range(TOP_K):
        pltpu.make_async_copy(hn_ref, xs_ref.at[pl.ds(0, TM_DISPATCH)], sem).wait()

    @pl.when(i == n - 1)
    def _():
        zero_ref[...] = jnp.zeros_like(zero_ref)
        sub = 8
        for e in range(N_EXPERTS):
            start, cnt = pad_start_ref[e], pad_cnt_ref[e]
            head = cnt & (sub - 1)
            body0 = start + head
            nbody = cnt // sub

            def head_copy(r):
                return _row_copy(zero_ref, 0, xs_ref, start + r, zsem)

            def body_copy(c):
                return pltpu.make_async_copy(zero_ref.at[pl.ds(0, sub)],
                                             xs_ref.at[pl.ds(body0 + c * sub, sub)], zsem)

            for copy, count in ((head_copy, head), (body_copy, nbody)):
                def zissue(r, carry, copy=copy):
                    copy(r).start()
                    return carry

                def zwait(r, carry, copy=copy):
                    copy(r).wait()
                    return carry

                lax.fori_loop(0, count, zissue, 0)
                lax.fori_loop(0, count, zwait, 0)

        def tail_copy(t):
            return pltpu.make_async_copy(zero_ref, xs_ref.at[pl.ds(t * tm, tm)], zsem)

        def tissue(t, carry):
            tail_copy(t).start()
            return carry

        def twait(t, carry):
            tail_copy(t).wait()
            return carry

        n_tiles = xs_ref.shape[0] // tm
        lax.fori_loop(nused_ref[0], n_tiles, tissue, 0)
        lax.fori_loop(nused_ref[0], n_tiles, twait, 0)


def _dispatch(dest, pad_start, pad_cnt, n_used, hn, n_rows):
    T = hn.shape[0]
    any_spec = pl.BlockSpec(memory_space=pl.ANY)
    return pl.pallas_call(
        _dispatch_kernel,
        grid_spec=pltpu.PrefetchScalarGridSpec(
            num_scalar_prefetch=4,
            grid=(T // TM_DISPATCH,),
            in_specs=[pl.BlockSpec((TM_DISPATCH, SLAB_SUBLANES, LANES), lambda i, *_: (i, 0, 0))],
            out_specs=any_spec,
            scratch_shapes=[pltpu.VMEM((TM_EXPERT, SLAB_SUBLANES, LANES), F32),
                            pltpu.SemaphoreType.DMA(()),
                            pltpu.SemaphoreType.DMA(())]),
        out_shape=jax.ShapeDtypeStruct((n_rows, SLAB_SUBLANES, LANES), F32),
        compiler_params=pltpu.CompilerParams(
            dimension_semantics=("arbitrary",), has_side_effects=True,
            vmem_limit_bytes=VMEM_LIMIT),
        name="dispatch",
    )(dest, pad_start, pad_cnt, n_used, hn)


def _experts_kernel(tile0_ref, ntile_ref, xs_ref, wg_ref, bg_ref, wl_ref, bl_ref, wd_ref, bd_ref,
                    ys_ref, wg_bf, wl_bf, wd_bf, xbuf, ybuf, xsem, ysem):
    e = pl.program_id(0)
    tm = TM_EXPERT
    t0, nt = tile0_ref[e], ntile_ref[e]

    def x_load(tile, slot):
        return _SlabTileCopy(xbuf.at[slot], xs_ref.at[pl.ds(tile * tm, tm)], xsem.at[slot],
                             to_slab=False)

    def y_store(tile, slot):
        return _SlabTileCopy(ybuf.at[slot], ys_ref.at[pl.ds(tile * tm, tm)], ysem.at[slot],
                             to_slab=True)

    @pl.when((e == 0) & (nt > 0))
    def _():
        x_load(t0, 0).start()

    @pl.when(nt > 0)
    def _():
        wg_bf[...] = wg_ref[0].astype(BF16)
        wl_bf[...] = wl_ref[0].astype(BF16)
        wd_bf[...] = wd_ref[0].astype(BF16)

    def tile_body(j, carry):
        slot = j % 2

        @pl.when(j + 1 < nt)
        def _():
            x_load(t0 + j + 1, 1 - slot).start()

        x_load(t0 + j, slot).wait()

        @pl.when(j >= 2)
        def _():
            y_store(t0 + j - 2, slot).wait()

        x = xbuf[slot].astype(BF16)
        glu = jnp.minimum(jnp.dot(x, wg_bf[...], preferred_element_type=F32) + bg_ref[0],
                          SWIGLU_LIMIT)
        lin = jnp.clip(jnp.dot(x, wl_bf[...], preferred_element_type=F32) + bl_ref[0],
                       -SWIGLU_LIMIT, SWIGLU_LIMIT)
        hid = glu * jax.nn.sigmoid(SWIGLU_ALPHA * glu) * (lin + 1.0)
        ybuf[slot] = jnp.dot(hid.astype(BF16), wd_bf[...], preferred_element_type=F32) + bd_ref[0]
        y_store(t0 + j, slot).start()
        return carry

    lax.fori_loop(0, nt, tile_body, 0)

    @pl.when(e + 1 < pl.num_programs(0))
    def _():
        @pl.when(ntile_ref[e + 1] > 0)
        def _():
            x_load(tile0_ref[e + 1], 0).start()

    @pl.when(nt >= 2)
    def _():
        y_store(t0 + nt - 2, nt % 2).wait()

    @pl.when(nt >= 1)
    def _():
        y_store(t0 + nt - 1, (nt - 1) % 2).wait()

    @pl.when(e == pl.num_programs(0) - 1)
    def _():
        ybuf[0] = jnp.zeros((tm, D_MODEL), F32)
        n_tiles = ys_ref.shape[0] // tm

        def tail_issue(t, carry):
            y_store(t, 0).start()
            return carry

        def tail_wait(t, carry):
            y_store(t, 0).wait()
            return carry

        lax.fori_loop(t0 + nt, n_tiles, tail_issue, 0)
        lax.fori_loop(t0 + nt, n_tiles, tail_wait, 0)


def _experts(tile0, ntile, xs, w_glu, b_glu, w_lin, b_lin, w_down, b_down):
    tm = TM_EXPERT
    any_spec = pl.BlockSpec(memory_space=pl.ANY)
    wspec = lambda: pl.BlockSpec((1, D_MODEL, D_FF), lambda e, *_: (e, 0, 0))
    bspec = lambda: pl.BlockSpec((1, 1, D_FF), lambda e, *_: (e, 0, 0))
    return pl.pallas_call(
        _experts_kernel,
        grid_spec=pltpu.PrefetchScalarGridSpec(
            num_scalar_prefetch=2,
            grid=(N_EXPERTS,),
            in_specs=[any_spec, wspec(), bspec(), wspec(), bspec(), wspec(), bspec()],
            out_specs=any_spec,
            scratch_shapes=[pltpu.VMEM((D_MODEL, D_FF), BF16)] * 3
            + [pltpu.VMEM((2, tm, D_MODEL), F32), pltpu.VMEM((2, tm, D_MODEL), F32),
               pltpu.SemaphoreType.DMA((2,)), pltpu.SemaphoreType.DMA((2,))]),
        out_shape=jax.ShapeDtypeStruct(xs.shape, F32),
        compiler_params=pltpu.CompilerParams(
            dimension_semantics=("arbitrary",), vmem_limit_bytes=VMEM_LIMIT,
            has_side_effects=True),
        name="experts",
    )(tile0, ntile, xs, w_glu, b_glu.reshape(N_EXPERTS, 1, D_FF),
      w_lin, b_lin.reshape(N_EXPERTS, 1, D_FF), w_down, b_down.reshape(N_EXPERTS, 1, D_MODEL))


def _combine_kernel(dest_ref, ys_ref, h1_ref, gate_ref, g_ref, out_ref, buf, obuf, sems, osems):
    s = pl.program_id(0)
    n = pl.num_programs(0) - 1
    tm = TM_COMBINE

    def issue(slot):
        base = s * (tm * TOP_K)
        for p in range(tm * TOP_K):
            r, k = divmod(p, TOP_K)
            _row_copy(ys_ref, dest_ref[base + p], buf.at[slot, k], r,
                      sems.at[slot]).start(priority=p % 2)

    def out_store(tile, slot):
        return _SlabTileCopy(out_ref.at[pl.ds(tile * tm, tm)], obuf.at[slot], osems.at[slot],
                             to_slab=False)

    for parity in range(2):
        @pl.when((s < n) & (s % 2 == parity))
        def _():
            issue(parity)

    @pl.when(s > 0)
    def _():
        tile = s - 1
        slot = tile % 2

        @pl.when(tile >= 2)
        def _():
            out_store(tile - 2, slot).wait()

        for k in range(TOP_K):
            pltpu.make_async_copy(ys_ref.at[pl.ds(0, tm)], buf.at[slot, k], sems.at[slot]).wait()
        acc = h1_ref[...]
        for k in range(TOP_K):
            acc = acc + gate_ref[:, k:k + 1, :] * buf[slot, k]
        ss = jnp.sum(jnp.sum(acc * acc, axis=2, keepdims=True), axis=1, keepdims=True)
        obuf[slot] = acc * lax.rsqrt(ss * (1.0 / D_MODEL) + NORM_EPS) * g_ref[...]
        out_store(tile, slot).start()

    @pl.when(s == n)
    def _():
        @pl.when(n >= 2)
        def _():
            out_store(n - 2, n % 2).wait()
        out_store(n - 1, (n - 1) % 2).wait()


def _combine(dest, ys3, h13, gate, fg):
    T = h13.shape[0]
    tm = TM_COMBINE
    slab = (SLAB_SUBLANES, LANES)
    prev_tile = lambda s, d: (jnp.maximum(s - 1, 0), 0, 0)
    gate_b = jnp.broadcast_to(gate[:, :, None], gate.shape + (LANES,))
    return pl.pallas_call(
        _combine_kernel,
        grid_spec=pltpu.PrefetchScalarGridSpec(
            num_scalar_prefetch=1,
            grid=(T // tm + 1,),
            in_specs=[pl.BlockSpec(memory_space=pl.ANY),
                      pl.BlockSpec((tm,) + slab, prev_tile),
                      pl.BlockSpec((tm, TOP_K, LANES), prev_tile),
                      pl.BlockSpec((1,) + slab, lambda s, d: (0, 0, 0))],
            out_specs=pl.BlockSpec(memory_space=pl.ANY),
            scratch_shapes=[pltpu.VMEM((2, TOP_K, tm) + slab, F32),
                            pltpu.VMEM((2, tm) + slab, F32),
                            pltpu.SemaphoreType.DMA((2,)),
                            pltpu.SemaphoreType.DMA((2,))]),
        out_shape=jax.ShapeDtypeStruct((T, D_MODEL), F32),
        compiler_params=pltpu.CompilerParams(
            dimension_semantics=("arbitrary",), vmem_limit_bytes=VMEM_LIMIT,
            has_side_effects=True),
        name="combine",
    )(dest, ys3, h13, gate_b, fg.reshape((1,) + slab))


def _swap_halves(w):
    h = w.shape[-1] // 2
    return jnp.concatenate([w[..., h:], w[..., :h]], axis=-1)


def _layer(x2, batch, seq, norm1_g, w_in, w_alpha_up, b_alpha, gla_norm_g, swa_sinks, swa_norm_g,
           w_out, norm2_g, w_router, b_router, w_glu, b_glu, w_lin, b_lin, w_down, b_down):
    T = x2.shape[0]
    kb_w, vb_w = w_in[:, 2064:2192], w_in[:, 2192:2320]
    w_z = w_in[:, 1536:1552]
    w_cat = jnp.concatenate([
        w_in[:, 0:1536], w_in[:, 1552:2064],
        kb_w, _swap_halves(kb_w), vb_w, _swap_halves(vb_w),
        jnp.pad(jnp.tile(w_z, (1, Z_PIECES)), [(0, 0), (0, LANES - Z_PIECES * GLA_RANK)]),
    ], axis=1).astype(BF16)
    wup_hi = w_alpha_up.astype(BF16)
    wup_lo = (w_alpha_up - wup_hi.astype(F32)).astype(BF16)
    wup_cat = jnp.pad(jnp.concatenate([wup_hi, wup_hi, wup_hi, wup_lo, wup_lo], axis=0),
                      [(0, LANES - Z_PIECES * GLA_RANK), (0, 0)])

    qk, vr, la, qb, kv = _in_proj(
        x2, norm1_g.reshape(1, -1), w_cat, wup_cat, b_alpha.reshape(1, -1))
    oa = _gla(qk, vr, la, gla_norm_g.reshape(1, -1), batch, seq)
    ob = _swa(qb, kv, swa_sinks, jnp.tile(swa_norm_g, 2).reshape(1, -1), batch, seq)

    wo = w_out.astype(BF16)
    wr_t = w_router.T
    wr_hi = wr_t.astype(BF16)
    wr_lo = (wr_t - wr_hi.astype(F32)).astype(BF16)
    h1, hn, route, cnt = _out_route(oa, ob, x2, wo[:512], wo[512:], norm2_g.reshape(1, -1),
                                    wr_hi, wr_lo, b_router.reshape(-1, 1))

    tm = TM_EXPERT
    n_tiles = T * TOP_K // tm + N_EXPERTS
    counts = cnt[:, 0].astype(jnp.int32)
    padded = (counts + tm - 1) // tm * tm
    pends = jnp.cumsum(padded)
    pstarts = pends - padded
    top_idx = route[0:TOP_K].T.astype(jnp.int32)
    gate = route[TOP_K:2 * TOP_K].T
    rank = route[2 * TOP_K:3 * TOP_K].T.astype(jnp.int32)
    seg_start = jnp.sum(jnp.where(top_idx[..., None] == jnp.arange(N_EXPERTS), pstarts, 0), axis=-1)
    dest = (seg_start + rank).reshape(-1)
    n_used = (pends[-1] // tm).reshape(1)
    xs = _dispatch(dest, pstarts + counts, padded - counts, n_used, hn, n_tiles * tm)
    ys = _experts(pstarts // tm, padded // tm, xs, w_glu, b_glu, w_lin, b_lin, w_down, b_down)
    return dest, ys, h1, gate


def kernel(x, norm1_g, w_in, w_alpha_up, b_alpha, gla_norm_g, swa_sinks, swa_norm_g, w_out,
           norm2_g, w_router, b_router, w_glu, b_glu, w_lin, b_lin, w_down, b_down, final_g):
    batch, seq, d = x.shape
    assert norm1_g.shape[0] == 1, "single-layer problem"
    x2 = x.reshape(batch * seq, d)
    dest, ys, h1, gate = _layer(
        x2, batch, seq, norm1_g[0], w_in[0], w_alpha_up[0], b_alpha[0], gla_norm_g[0],
        swa_sinks[0], swa_norm_g[0], w_out[0], norm2_g[0], w_router[0], b_router[0],
        w_glu[0], b_glu[0], w_lin[0], b_lin[0], w_down[0], b_down[0])
    out = _combine(dest, ys, h1, gate, final_g.reshape(1, -1))
    return out.reshape(batch, seq, d)
```

```python
import numpy as np
import jax
import jax.numpy as jnp
from jax import lax
from jax.experimental import pallas as pl
from jax.experimental.pallas import tpu as pltpu

F32 = jnp.float32
BF16 = jnp.bfloat16

D_MODEL = 1024
GLA_HEADS = 4
GLA_DK = 64
GLA_DV = 128
GLA_RANK = 16
GLA_GATE_TAU = 16.0
GLA_CHUNK = 64
SWA_HEADS = 8
SWA_KV_HEADS = 2
SWA_HEAD_DIM = 64
SWA_WINDOW = 128
N_EXPERTS = 32
TOP_K = 4
D_FF = 1024
SWIGLU_LIMIT = 7.0
SWIGLU_ALPHA = 1.702
NORM_EPS = 1e-5

LANES = 128
SLAB_SUBLANES = D_MODEL // LANES
VMEM_LIMIT = 56 * 1024 * 1024

TM_PROJ = 512
TL_GLA = 1024
GLA_CUM_ROWS = 256
TM_ROUTE = 512
TM_EXPERT = 512
TM_COMBINE = 256
SWA_SUB = 4
TM_DISPATCH = 256

NEG_BIG = -1e30
ROUTE_ROWS = 16

C_QA, C_KA, C_VA, C_RA, C_QB, C_KB, C_KBS, C_VB, C_VBS, C_Z, C_END = (
    0, 256, 512, 1024, 1536, 2048, 2176, 2304, 2432, 2560, 2688)
Z_PIECES = 5


def _split3(x):
    hi = x.astype(BF16).astype(F32)
    r = x - hi
    mid = r.astype(BF16).astype(F32)
    lo = (r - mid).astype(BF16).astype(F32)
    return hi, mid, lo


class _SlabTileCopy:
    def __init__(self, flat, slab, sem, to_slab):
        self.copies = []
        for c in range(SLAB_SUBLANES):
            pair = (flat.at[:, pl.ds(c * LANES, LANES)], slab.at[:, c, :])
            src, dst = pair if to_slab else pair[::-1]
            self.copies.append(pltpu.make_async_copy(src, dst, sem))

    def start(self):
        for cp in self.copies:
            cp.start()

    def wait(self):
        for cp in self.copies:
            cp.wait()


def _rms(x, g):
    return x * lax.rsqrt(jnp.mean(x * x, axis=-1, keepdims=True) + NORM_EPS) * g


def _in_proj_kernel(x_ref, g_ref, w_ref, wup_ref, ba_ref,
                    qk_ref, vr_ref, la_ref, qb_ref, kv_ref):
    u = _rms(x_ref[...], g_ref[...]).astype(BF16)

    def proj(c0, c1):
        return jnp.dot(u, w_ref[:, c0:c1], preferred_element_type=F32)

    qk_ref[...] = proj(C_QA, C_VA).astype(BF16)
    vr_ref[:, :C_RA - C_VA] = proj(C_VA, C_RA).astype(BF16)
    vr_ref[:, C_RA - C_VA:] = proj(C_RA, C_QB).astype(BF16)
    qb_ref[...] = proj(C_QB, C_KB).astype(BF16)
    kv_ref[...] = proj(C_KB, C_Z).astype(BF16)
    z = proj(C_Z, C_END)
    hi, mid, lo = _split3(z)
    piece = lax.broadcasted_iota(jnp.int32, z.shape, 1) // GLA_RANK
    zc = jnp.where((piece == 0) | (piece == 3), hi, jnp.where(piece == 2, lo, mid)).astype(BF16)
    y = jnp.dot(zc, wup_ref[...], preferred_element_type=F32) + ba_ref[...]
    log_sig = jnp.minimum(y, 0.0) - jnp.log1p(jnp.exp(-jnp.abs(y)))
    la_ref[...] = log_sig * (1.0 / GLA_GATE_TAU)


def _in_proj(x2, g1, w_cat, wup_p, ba_p):
    T = x2.shape[0]
    tm = TM_PROJ
    row = lambda w: pl.BlockSpec((tm, w), lambda i: (i, 0))
    full = lambda a: pl.BlockSpec(a.shape, lambda i: (0,) * a.ndim)
    outs = [(512, BF16), (1024, BF16), (256, F32), (512, BF16), (512, BF16)]
    return pl.pallas_call(
        _in_proj_kernel,
        grid=(T // tm,),
        in_specs=[row(D_MODEL), full(g1), full(w_cat), full(wup_p), full(ba_p)],
        out_specs=[row(w) for w, _ in outs],
        out_shape=[jax.ShapeDtypeStruct((T, w), dt) for w, dt in outs],
        compiler_params=pltpu.CompilerParams(
            dimension_semantics=("arbitrary",), vmem_limit_bytes=VMEM_LIMIT),
        name="in_proj",
    )(x2, g1, w_cat, wup_p, ba_p)


def _gla_kernel(qk_ref, vr_ref, la_ref, g_ref, cum_ref, o_ref, st_ref):
    @pl.when(pl.program_id(1) == 0)
    def _():
        st_ref[...] = jnp.zeros_like(st_ref)

    tl = TL_GLA
    c = GLA_CHUNK
    kw = GLA_HEADS * GLA_DK
    vw = GLA_HEADS * GLA_DV
    causal = (lax.broadcasted_iota(jnp.int32, (c, c), 0)
              >= lax.broadcasted_iota(jnp.int32, (c, c), 1))
    low_half = lax.broadcasted_iota(jnp.int32, (c, LANES), 1) < GLA_DK
    g = g_ref[...]
    b_groups = []
    for grp in range(tl // GLA_CUM_ROWS):
        la = la_ref[grp * GLA_CUM_ROWS:(grp + 1) * GLA_CUM_ROWS, :]
        pieces = jnp.concatenate([p.astype(BF16) for p in _split3(la)], axis=1)
        b3 = jnp.dot(cum_ref[...], pieces, preferred_element_type=F32)
        b_groups.append(b3[:, :kw] + b3[:, kw:2 * kw] + b3[:, 2 * kw:])
    for ch in range(tl // c):
        rows = slice(ch * c, (ch + 1) * c)
        in_grp = (ch * c) % GLA_CUM_ROWS
        b = b_groups[(ch * c) // GLA_CUM_ROWS][in_grp:in_grp + c]
        b_last = b[c - 1:c]
        qf = qk_ref[rows, :kw].astype(F32)
        kf = qk_ref[rows, kw:].astype(F32)
        q_e = (qf * jnp.exp(b) * (GLA_DK ** -0.5)).astype(BF16)
        k_e = (kf * jnp.exp(-b)).astype(BF16)
        k_t = (kf * jnp.exp(b_last - b)).astype(BF16)
        decay = jnp.exp(b_last)
        for h in range(GLA_HEADS):
            ps = slice((h // 2) * LANES, (h // 2 + 1) * LANES)
            mine = low_half if h % 2 == 0 else ~low_half
            qp, kp = q_e[:, ps], k_e[:, ps]
            qh = jnp.where(mine, qp, jnp.zeros_like(qp))
            kth = jnp.where(mine, k_t[:, ps], jnp.zeros_like(qp))
            vs = slice(h * GLA_DV, (h + 1) * GLA_DV)
            vh = vr_ref[rows, vs]
            a = pl.dot(qh, kp, trans_b=True)
            a = jnp.where(causal, a, 0.0).astype(BF16)
            st = st_ref[h]
            o = (jnp.dot(a, vh, preferred_element_type=F32)
                 + pl.dot(qh, st.astype(BF16), trans_b=True))
            st_ref[h] = st * decay[:, ps] + pl.dot(vh, kth, trans_a=True)
            rh = vr_ref[rows, slice(vw + h * GLA_DV, vw + (h + 1) * GLA_DV)].astype(F32)
            o = _rms(o, g) * (rh * jax.nn.sigmoid(rh))
            o_ref[rows, vs] = o.astype(BF16)


def _gla(qk, vr, la, g, batch, seq):
    tl = TL_GLA
    nl = seq // tl
    row = lambda w: pl.BlockSpec((tl, w), lambda b, i: (b * nl + i, 0))
    r = np.arange(GLA_CUM_ROWS)
    cum = jnp.asarray((r[None, :] <= r[:, None])
                      & (r[None, :] // GLA_CHUNK == r[:, None] // GLA_CHUNK), BF16)
    return pl.pallas_call(
        _gla_kernel,
        grid=(batch, nl),
        in_specs=[row(512), row(1024), row(256),
                  pl.BlockSpec((1, GLA_DV), lambda b, i: (0, 0)),
                  pl.BlockSpec((GLA_CUM_ROWS, GLA_CUM_ROWS), lambda b, i: (0, 0))],
        out_specs=row(512),
        out_shape=jax.ShapeDtypeStruct((batch * seq, 512), BF16),
        scratch_shapes=[pltpu.VMEM((GLA_HEADS, GLA_DV, LANES), F32)],
        compiler_params=pltpu.CompilerParams(
            dimension_semantics=("arbitrary", "arbitrary"), vmem_limit_bytes=VMEM_LIMIT),
        name="gla",
    )(qk, vr, la, g, cum)


def _swa_kernel(sink_ref, bias0_ref, bias_ref, q_ref, kvc_ref, kvp_ref, g_ref, o_ref):
    w = SWA_WINDOW
    scale = jnp.asarray(SWA_HEAD_DIM ** -0.5, BF16)
    kv_lane = lax.broadcasted_iota(jnp.int32, (2 * w, LANES), 1)
    lane_lo = kv_lane < SWA_HEAD_DIM
    out_lo = lax.broadcasted_iota(jnp.int32, (w, LANES), 1) < SWA_HEAD_DIM
    ones_hi = jnp.where(kv_lane == SWA_HEAD_DIM, 1.0, 0.0).astype(BF16)
    ones_lo = jnp.where(kv_lane == 0, 1.0, 0.0).astype(BF16)
    g = g_ref[...]

    for sb in range(SWA_SUB):
        rows = slice(sb * w, (sb + 1) * w)
        if sb == 0:
            kv = jnp.concatenate([kvp_ref[...], kvc_ref[0:w, :]], axis=0)
            bias = bias0_ref
        else:
            kv = kvc_ref[(sb - 1) * w:(sb + 1) * w, :]
            bias = bias_ref
        k = kv[:, 0:LANES] * scale
        ks = kv[:, LANES:2 * LANES] * scale
        v, vs = kv[:, 2 * LANES:3 * LANES], kv[:, 3 * LANES:]
        zero = jnp.zeros_like(k)
        k_low = [jnp.where(lane_lo, k, zero), jnp.where(lane_lo, ks, zero)]
        k_high = [jnp.where(lane_lo, zero, ks), jnp.where(lane_lo, zero, k)]
        v_low = [jnp.where(lane_lo, v, zero) + ones_hi, jnp.where(lane_lo, vs, zero) + ones_hi]
        v_high = [jnp.where(lane_lo, zero, vs) + ones_lo, jnp.where(lane_lo, zero, v) + ones_lo]

        def weights(s, head):
            s = s + bias[0, head]
            sink = sink_ref[head]
            m = jnp.maximum(jnp.max(s, axis=-1, keepdims=True), sink)
            return jnp.exp(s - m).astype(BF16), jnp.exp(sink - m)

        for pair in range(SWA_HEADS // 2):
            j = (2 * pair) // (SWA_HEADS // SWA_KV_HEADS)
            cols = slice(pair * LANES, (pair + 1) * LANES)
            qp = q_ref[rows, cols]
            e0, sink0 = weights(pl.dot(qp, k_low[j], trans_b=True), 2 * pair)
            e1, sink1 = weights(pl.dot(qp, k_high[j], trans_b=True), 2 * pair + 1)
            oa = jnp.dot(e0, v_low[j], preferred_element_type=F32)
            ob = jnp.dot(e1, v_high[j], preferred_element_type=F32)
            inv0 = 1.0 / (oa[:, SWA_HEAD_DIM:SWA_HEAD_DIM + 1] + sink0)
            inv1 = 1.0 / (ob[:, 0:1] + sink1)
            o = jnp.where(out_lo, oa * inv0, ob * inv1)
            sq = o * o
            ms_lo = jnp.sum(jnp.where(out_lo, sq, 0.0), axis=-1, keepdims=True)
            ms_hi = jnp.sum(jnp.where(out_lo, 0.0, sq), axis=-1, keepdims=True)
            ms = jnp.where(out_lo, ms_lo, ms_hi) * (1.0 / SWA_HEAD_DIM)
            o = o * lax.rsqrt(ms + NORM_EPS) * g
            o_ref[rows, cols] = o.astype(BF16)


def _swa_bias():
    w = SWA_WINDOW
    slopes = 2.0 ** (-8.0 * np.arange(1, SWA_HEADS + 1, dtype=np.float64) / SWA_HEADS)
    rel = np.arange(w)[:, None] + w - np.arange(2 * w)[None, :]
    in_window = (rel >= 0) & (rel < w)
    exists = np.stack([np.arange(2 * w) >= w, np.ones(2 * w, bool)])
    valid = in_window[None] & exists[:, None, :]
    bias = -slopes[None, :, None, None] * rel[None, None].astype(np.float64)
    return np.where(valid[:, None], bias, NEG_BIG).astype(np.float32)


def _swa(qb, kv, sinks, g2, batch, seq):
    w = SWA_WINDOW
    rows = SWA_SUB * w
    nb = seq // rows
    cur = lambda: pl.BlockSpec((rows, 512), lambda b, n: (b * nb + n, 0))
    prev = pl.BlockSpec(
        (w, 512), lambda b, n: (jnp.maximum((b * nb + n) * SWA_SUB - 1, b * nb * SWA_SUB), 0))
    table = lambda index: pl.BlockSpec((1, SWA_HEADS, w, 2 * w), index)
    bias = jnp.asarray(_swa_bias())
    return pl.pallas_call(
        _swa_kernel,
        grid=(batch, nb),
        in_specs=[pl.BlockSpec(memory_space=pltpu.SMEM),
                  table(lambda b, n: (jnp.minimum(n, 1), 0, 0, 0)),
                  table(lambda b, n: (1, 0, 0, 0)),
                  cur(), cur(), prev,
                  pl.BlockSpec((1, LANES), lambda b, n: (0, 0))],
        out_specs=cur(),
        out_shape=jax.ShapeDtypeStruct((batch * seq, 512), BF16),
        compiler_params=pltpu.CompilerParams(
            dimension_semantics=("arbitrary", "arbitrary"), vmem_limit_bytes=VMEM_LIMIT),
        name="swa",
    )(sinks, bias, bias, qb, kv, kv, g2)


def _out_route_kernel(oa_ref, ob_ref, x_ref, woa_ref, wob_ref, g_ref, wrh_ref, wrl_ref, br_ref,
                      earlier_ref, h13_ref, hn3_ref, route_ref, cnt_ref, hbuf, hsem):
    i = pl.program_id(0)
    tm = TM_ROUTE

    @pl.when(i == 0)
    def _():
        cnt_ref[...] = jnp.zeros_like(cnt_ref)

    def stores(step, slot):
        rows = pl.ds(step * tm, tm)
        return [_SlabTileCopy(hbuf.at[slot, which], out.at[rows], hsem.at[slot], to_slab=True)
                for which, out in enumerate((h13_ref, hn3_ref))]

    slot = i % 2

    @pl.when(i >= 2)
    def _():
        for st in stores(i - 2, slot):
            st.wait()

    h1 = (x_ref[...]
          + jnp.dot(oa_ref[...], woa_ref[...], preferred_element_type=F32)
          + jnp.dot(ob_ref[...], wob_ref[...], preferred_element_type=F32))
    hn = _rms(h1, g_ref[...])
    hbuf[slot, 0] = h1
    hbuf[slot, 1] = hn
    for st in stores(i, slot):
        st.start()

    @pl.when(i == pl.num_programs(0) - 1)
    def _():
        @pl.when(i >= 1)
        def _():
            for st in stores(i - 1, 1 - slot):
                st.wait()
        for st in stores(i, slot):
            st.wait()

    hn_hi = hn.astype(BF16)
    hn_lo = (hn - hn_hi.astype(F32)).astype(BF16)
    logits = (pl.dot(wrh_ref[...], hn_hi, trans_b=True)
              + pl.dot(wrh_ref[...], hn_lo, trans_b=True)
              + pl.dot(wrl_ref[...], hn_hi, trans_b=True)) + br_ref[...]

    eid = lax.broadcasted_iota(jnp.int32, (N_EXPERTS, tm), 0)
    work = logits
    vals, idxs, sels = [], [], []
    for _ in range(TOP_K):
        m = jnp.max(work, axis=0, keepdims=True)
        idx = jnp.min(jnp.where(work == m, eid, N_EXPERTS), axis=0, keepdims=True)
        sel = eid == idx
        vals.append(m)
        idxs.append(idx)
        sels.append(sel)
        work = jnp.where(sel, -3e38, work)
    exps = [jnp.exp(v - vals[0]) for v in vals]
    inv_den = 1.0 / (exps[0] + exps[1] + exps[2] + exps[3])

    multihot = jnp.where(sels[0] | sels[1] | sels[2] | sels[3], 1.0, 0.0)
    before = (jnp.dot(multihot.astype(BF16), earlier_ref[...], preferred_element_type=F32)
              + cnt_ref[:, 0:1])
    ranks = [jnp.sum(jnp.where(s, before, 0.0), axis=0, keepdims=True) for s in sels]
    route_ref[...] = jnp.concatenate(
        [i.astype(F32) for i in idxs] + [e * inv_den for e in exps] + ranks
        + [jnp.zeros((ROUTE_ROWS - 3 * TOP_K, tm), F32)], axis=0)
    cnt_ref[...] += jnp.sum(multihot, axis=1, keepdims=True)


def _out_route(oa, ob, x2, woa, wob, g2, wr_hi, wr_lo, br):
    T = x2.shape[0]
    tm = TM_ROUTE
    row = lambda w: pl.BlockSpec((tm, w), lambda i: (i, 0))
    full = lambda a: pl.BlockSpec(a.shape, lambda i: (0,) * a.ndim)
    earlier = jnp.asarray(np.triu(np.ones((tm, tm), np.float32), k=1), BF16)
    return pl.pallas_call(
        _out_route_kernel,
        grid=(T // tm,),
        in_specs=[row(512), row(512), row(D_MODEL), full(woa), full(wob), full(g2),
                  full(wr_hi), full(wr_lo), full(br), full(earlier)],
        out_specs=[pl.BlockSpec(memory_space=pl.ANY), pl.BlockSpec(memory_space=pl.ANY),
                   pl.BlockSpec((ROUTE_ROWS, tm), lambda i: (0, i)),
                   pl.BlockSpec((N_EXPERTS, LANES), lambda i: (0, 0))],
        out_shape=[jax.ShapeDtypeStruct((T, SLAB_SUBLANES, LANES), F32),
                   jax.ShapeDtypeStruct((T, SLAB_SUBLANES, LANES), F32),
                   jax.ShapeDtypeStruct((ROUTE_ROWS, T), F32),
                   jax.ShapeDtypeStruct((N_EXPERTS, LANES), F32)],
        scratch_shapes=[pltpu.VMEM((2, 2, tm, D_MODEL), F32), pltpu.SemaphoreType.DMA((2,))],
        compiler_params=pltpu.CompilerParams(
            dimension_semantics=("arbitrary",), vmem_limit_bytes=VMEM_LIMIT,
            has_side_effects=True),
        name="out_route",
    )(oa, ob, x2, woa, wob, g2, wr_hi, wr_lo, br, earlier)


def _row_copy(src, s, dst, d, sem):
    return pltpu.make_async_copy(src.at[pl.ds(s, 1)], dst.at[pl.ds(d, 1)], sem)


def _dispatch_kernel(dest_ref, pad_start_ref, pad_cnt_ref, nused_ref, hn_ref, gate_ref,
                     xs_ref, gate_lanes_ref, zero_ref, sem, zsem):
    i = pl.program_id(0)
    n = pl.num_programs(0)
    tm = TM_EXPERT
    gate_lanes_ref[...] = jnp.broadcast_to(gate_ref[...], gate_lanes_ref.shape)
    base = i * (TM_DISPATCH * TOP_K)
    for p in range(TM_DISPATCH * TOP_K):
        _row_copy(hn_ref, p // TOP_K, xs_ref, dest_ref[base + p], sem).start(priority=p % 2)
    for _ in range(TOP_K):
        pltpu.make_async_copy(hn_ref, xs_ref.at[pl.ds(0, TM_DISPATCH)], sem).wait()

    @pl.when(i == n - 1)
    def _():
        zero_ref[...] = jnp.zeros_like(zero_ref)
        sub = 8
        for e in range(N_EXPERTS):
            start, cnt = pad_start_ref[e], pad_cnt_ref[e]
            head = cnt & (sub - 1)
            body0 = start + head
            nbody = cnt // sub

            def head_copy(r):
                return _row_copy(zero_ref, 0, xs_ref, start + r, zsem)

            def body_copy(c):
                return pltpu.make_async_copy(zero_ref.at[pl.ds(0, sub)],
                                             xs_ref.at[pl.ds(body0 + c * sub, sub)], zsem)

            for copy, count in ((head_copy, head), (body_copy, nbody)):
                def zissue(r, carry, copy=copy):
                    copy(r).start()
                    return carry

                def zwait(r, carry, copy=copy):
                    copy(r).wait()
                    return carry

                lax.fori_loop(0, count, zissue, 0)
                lax.fori_loop(0, count, zwait, 0)

        def tail_copy(t):
            return pltpu.make_async_copy(zero_ref, xs_ref.at[pl.ds(t * tm, tm)], zsem)

        def tissue(t, carry):
            tail_copy(t).start()
            return carry

        def twait(t, carry):
            tail_copy(t).wait()
            return carry

        n_tiles = xs_ref.shape[0] // tm
        lax.fori_loop(nused_ref[0], n_tiles, tissue, 0)
        lax.fori_loop(nused_ref[0], n_tiles, twait, 0)


def _dispatch(dest, pad_start, pad_cnt, n_used, hn, gate, n_rows):
    T = hn.shape[0]
    tile = lambda *shape: pl.BlockSpec((TM_DISPATCH,) + shape, lambda i, *_: (i, 0, 0))
    return pl.pallas_call(
        _dispatch_kernel,
        grid_spec=pltpu.PrefetchScalarGridSpec(
            num_scalar_prefetch=4,
            grid=(T // TM_DISPATCH,),
            in_specs=[tile(SLAB_SUBLANES, LANES), tile(TOP_K, 1)],
            out_specs=[pl.BlockSpec(memory_space=pl.ANY), tile(TOP_K, LANES)],
            scratch_shapes=[pltpu.VMEM((TM_EXPERT, SLAB_SUBLANES, LANES), F32),
                            pltpu.SemaphoreType.DMA(()),
                            pltpu.SemaphoreType.DMA(())]),
        out_shape=[jax.ShapeDtypeStruct((n_rows, SLAB_SUBLANES, LANES), F32),
                   jax.ShapeDtypeStruct((T, TOP_K, LANES), F32)],
        compiler_params=pltpu.CompilerParams(
            dimension_semantics=("arbitrary",), has_side_effects=True,
            vmem_limit_bytes=VMEM_LIMIT),
        name="dispatch",
    )(dest, pad_start, pad_cnt, n_used, hn, gate[:, :, None])


def _experts_kernel(tile0_ref, ntile_ref, xs_ref, wg_ref, bg_ref, wl_ref, bl_ref, wd_ref, bd_ref,
                    ys_ref, wg_bf, wl_bf, wd_bf, xbuf, ybuf, xsem, ysem):
    e = pl.program_id(0)
    tm = TM_EXPERT
    t0, nt = tile0_ref[e], ntile_ref[e]

    def x_load(tile, slot):
        return _SlabTileCopy(xbuf.at[slot], xs_ref.at[pl.ds(tile * tm, tm)], xsem.at[slot],
                             to_slab=False)

    def y_store(tile, slot):
        return _SlabTileCopy(ybuf.at[slot], ys_ref.at[pl.ds(tile * tm, tm)], ysem.at[slot],
                             to_slab=True)

    @pl.when((e == 0) & (nt > 0))
    def _():
        x_load(t0, 0).start()

    @pl.when(nt > 0)
    def _():
        wg_bf[...] = wg_ref[0].astype(BF16)
        wl_bf[...] = wl_ref[0].astype(BF16)
        wd_bf[...] = wd_ref[0].astype(BF16)

    def tile_body(j, carry):
        slot = j % 2

        @pl.when(j + 1 < nt)
        def _():
            x_load(t0 + j + 1, 1 - slot).start()

        x_load(t0 + j, slot).wait()

        @pl.when(j >= 2)
        def _():
            y_store(t0 + j - 2, slot).wait()

        x = xbuf[slot].astype(BF16)
        glu = jnp.minimum(jnp.dot(x, wg_bf[...], preferred_element_type=F32) + bg_ref[0],
                          SWIGLU_LIMIT)
        lin = jnp.clip(jnp.dot(x, wl_bf[...], preferred_element_type=F32) + bl_ref[0],
                       -SWIGLU_LIMIT, SWIGLU_LIMIT)
        hid = glu * jax.nn.sigmoid(SWIGLU_ALPHA * glu) * (lin + 1.0)
        ybuf[slot] = jnp.dot(hid.astype(BF16), wd_bf[...], preferred_element_type=F32) + bd_ref[0]
        y_store(t0 + j, slot).start()
        return carry

    lax.fori_loop(0, nt, tile_body, 0)

    @pl.when(e + 1 < pl.num_programs(0))
    def _():
        @pl.when(ntile_ref[e + 1] > 0)
        def _():
            x_load(tile0_ref[e + 1], 0).start()

    @pl.when(nt >= 2)
    def _():
        y_store(t0 + nt - 2, nt % 2).wait()

    @pl.when(nt >= 1)
    def _():
        y_store(t0 + nt - 1, (nt - 1) % 2).wait()

    @pl.when(e == pl.num_programs(0) - 1)
    def _():
        ybuf[0] = jnp.zeros((tm, D_MODEL), F32)
        n_tiles = ys_ref.shape[0] // tm

        def tail_issue(t, carry):
            y_store(t, 0).start()
            return carry

        def tail_wait(t, carry):
            y_store(t, 0).wait()
            return carry

        lax.fori_loop(t0 + nt, n_tiles, tail_issue, 0)
        lax.fori_loop(t0 + nt, n_tiles, tail_wait, 0)


def _experts(tile0, ntile, xs, w_glu, b_glu, w_lin, b_lin, w_down, b_down):
    tm = TM_EXPERT
    any_spec = pl.BlockSpec(memory_space=pl.ANY)
    wspec = lambda: pl.BlockSpec((1, D_MODEL, D_FF), lambda e, *_: (e, 0, 0))
    bspec = lambda: pl.BlockSpec((1, 1, D_FF), lambda e, *_: (e, 0, 0))
    return pl.pallas_call(
        _experts_kernel,
        grid_spec=pltpu.PrefetchScalarGridSpec(
            num_scalar_prefetch=2,
            grid=(N_EXPERTS,),
            in_specs=[any_spec, wspec(), bspec(), wspec(), bspec(), wspec(), bspec()],
            out_specs=any_spec,
            scratch_shapes=[pltpu.VMEM((D_MODEL, D_FF), BF16)] * 3
            + [pltpu.VMEM((2, tm, D_MODEL), F32), pltpu.VMEM((2, tm, D_MODEL), F32),
               pltpu.SemaphoreType.DMA((2,)), pltpu.SemaphoreType.DMA((2,))]),
        out_shape=jax.ShapeDtypeStruct(xs.shape, F32),
        compiler_params=pltpu.CompilerParams(
            dimension_semantics=("arbitrary",), vmem_limit_bytes=VMEM_LIMIT,
            has_side_effects=True),
        name="experts",
    )(tile0, ntile, xs, w_glu, b_glu.reshape(N_EXPERTS, 1, D_FF),
      w_lin, b_lin.reshape(N_EXPERTS, 1, D_FF), w_down, b_down.reshape(N_EXPERTS, 1, D_MODEL))


def _combine_kernel(dest_ref, ys_ref, h1_ref, gate_ref, g_ref, out_ref, buf, obuf, sems, osems):
    s = pl.program_id(0)
    n = pl.num_programs(0) - 1
    tm = TM_COMBINE

    def issue(slot):
        base = s * (tm * TOP_K)
        for p in range(tm * TOP_K):
            r, k = divmod(p, TOP_K)
            _row_copy(ys_ref, dest_ref[base + p], buf.at[slot, k], r,
                      sems.at[slot]).start(priority=p % 2)

    def out_store(tile, slot):
        return _SlabTileCopy(out_ref.at[pl.ds(tile * tm, tm)], obuf.at[slot], osems.at[slot],
                             to_slab=False)

    for parity in range(2):
        @pl.when((s < n) & (s % 2 == parity))
        def _():
            issue(parity)

    @pl.when(s > 0)
    def _():
        tile = s - 1
        slot = tile % 2

        @pl.when(tile >= 2)
        def _():
            out_store(tile - 2, slot).wait()

        for k in range(TOP_K):
            pltpu.make_async_copy(ys_ref.at[pl.ds(0, tm)], buf.at[slot, k], sems.at[slot]).wait()
        acc = h1_ref[...]
        for k in range(TOP_K):
            acc = acc + gate_ref[:, k:k + 1, :] * buf[slot, k]
        ss = jnp.sum(jnp.sum(acc * acc, axis=2, keepdims=True), axis=1, keepdims=True)
        obuf[slot] = acc * lax.rsqrt(ss * (1.0 / D_MODEL) + NORM_EPS) * g_ref[...]
        out_store(tile, slot).start()

    @pl.when(s == n)
    def _():
        @pl.when(n >= 2)
        def _():
            out_store(n - 2, n % 2).wait()
        out_store(n - 1, (n - 1) % 2).wait()


def _combine(dest, ys3, h13, gate, fg):
    T = h13.shape[0]
    tm = TM_COMBINE
    slab = (SLAB_SUBLANES, LANES)
    prev_tile = lambda s, d: (jnp.maximum(s - 1, 0), 0, 0)
    return pl.pallas_call(
        _combine_kernel,
        grid_spec=pltpu.PrefetchScalarGridSpec(
            num_scalar_prefetch=1,
            grid=(T // tm + 1,),
            in_specs=[pl.BlockSpec(memory_space=pl.ANY),
                      pl.BlockSpec((tm,) + slab, prev_tile),
                      pl.BlockSpec((tm, TOP_K, LANES), prev_tile),
                      pl.BlockSpec((1,) + slab, lambda s, d: (0, 0, 0))],
            out_specs=pl.BlockSpec(memory_space=pl.ANY),
            scratch_shapes=[pltpu.VMEM((2, TOP_K, tm) + slab, F32),
                            pltpu.VMEM((2, tm) + slab, F32),
                            pltpu.SemaphoreType.DMA((2,)),
                            pltpu.SemaphoreType.DMA((2,))]),
        out_shape=jax.ShapeDtypeStruct((T, D_MODEL), F32),
        compiler_params=pltpu.CompilerParams(
            dimension_semantics=("arbitrary",), vmem_limit_bytes=VMEM_LIMIT,
            has_side_effects=True),
        name="combine",
    )(dest, ys3, h13, gate, fg.reshape((1,) + slab))


def _swap_halves(w):
    h = w.shape[-1] // 2
    return jnp.concatenate([w[..., h:], w[..., :h]], axis=-1)


def _layer(x2, batch, seq, norm1_g, w_in, w_alpha_up, b_alpha, gla_norm_g, swa_sinks, swa_norm_g,
           w_out, norm2_g, w_router, b_router, w_glu, b_glu, w_lin, b_lin, w_down, b_down):
    T = x2.shape[0]
    kb_w, vb_w = w_in[:, 2064:2192], w_in[:, 2192:2320]
    w_z = w_in[:, 1536:1552]
    w_cat = jnp.concatenate([
        w_in[:, 0:1536], w_in[:, 1552:2064],
        kb_w, _swap_halves(kb_w), vb_w, _swap_halves(vb_w),
        jnp.pad(jnp.tile(w_z, (1, Z_PIECES)), [(0, 0), (0, LANES - Z_PIECES * GLA_RANK)]),
    ], axis=1).astype(BF16)
    wup_hi = w_alpha_up.astype(BF16)
    wup_lo = (w_alpha_up - wup_hi.astype(F32)).astype(BF16)
    wup_cat = jnp.pad(jnp.concatenate([wup_hi, wup_hi, wup_hi, wup_lo, wup_lo], axis=0),
                      [(0, LANES - Z_PIECES * GLA_RANK), (0, 0)])

    qk, vr, la, qb, kv = _in_proj(
        x2, norm1_g.reshape(1, -1), w_cat, wup_cat, b_alpha.reshape(1, -1))
    oa = _gla(qk, vr, la, gla_norm_g.reshape(1, -1), batch, seq)
    ob = _swa(qb, kv, swa_sinks, jnp.tile(swa_norm_g, 2).reshape(1, -1), batch, seq)

    wo = w_out.astype(BF16)
    wr_t = w_router.T
    wr_hi = wr_t.astype(BF16)
    wr_lo = (wr_t - wr_hi.astype(F32)).astype(BF16)
    h1, hn, route, cnt = _out_route(oa, ob, x2, wo[:512], wo[512:], norm2_g.reshape(1, -1),
                                    wr_hi, wr_lo, b_router.reshape(-1, 1))

    tm = TM_EXPERT
    n_tiles = T * TOP_K // tm + N_EXPERTS
    counts = cnt[:, 0].astype(jnp.int32)
    padded = (counts + tm - 1) // tm * tm
    pends = jnp.cumsum(padded)
    pstarts = pends - padded
    top_idx = route[0:TOP_K].T.astype(jnp.int32)
    gate = route[TOP_K:2 * TOP_K].T
    rank = route[2 * TOP_K:3 * TOP_K].T.astype(jnp.int32)
    seg_start = jnp.sum(jnp.where(top_idx[..., None] == jnp.arange(N_EXPERTS), pstarts, 0), axis=-1)
    dest = (seg_start + rank).reshape(-1)
    n_used = (pends[-1] // tm).reshape(1)
    xs, gate = _dispatch(dest, pstarts + counts, padded - counts, n_used, hn, gate, n_tiles * tm)
    ys = _experts(pstarts // tm, padded // tm, xs, w_glu, b_glu, w_lin, b_lin, w_down, b_down)
    return dest, ys, h1, gate


def kernel(x, norm1_g, w_in, w_alpha_up, b_alpha, gla_norm_g, swa_sinks, swa_norm_g, w_out,
           norm2_g, w_router, b_router, w_glu, b_glu, w_lin, b_lin, w_down, b_down, final_g):
    batch, seq, d = x.shape
    assert norm1_g.shape[0] == 1, "single-layer problem"
    x2 = x.reshape(batch * seq, d)
    dest, ys, h1, gate = _layer(
        x2, batch, seq, norm1_g[0], w_in[0], w_alpha_up[0], b_alpha[0], gla_norm_g[0],
        swa_sinks[0], swa_norm_g[0], w_out[0], norm2_g[0], w_router[0], b_router[0],
        w_glu[0], b_glu[0], w_lin[0], b_lin[0], w_down[0], b_down[0])
    out = _combine(dest, ys, h1, gate, final_g.reshape(1, -1))
    return out.reshape(batch, seq, d)
```

```python
import numpy as np
import jax
import jax.numpy as jnp
from jax import lax
from jax.experimental import pallas as pl
from jax.experimental.pallas import tpu as pltpu

F32 = jnp.float32
BF16 = jnp.bfloat16

D_MODEL = 1024
GLA_HEADS = 4
GLA_DK = 64
GLA_DV = 128
GLA_RANK = 16
GLA_GATE_TAU = 16.0
GLA_CHUNK = 64
SWA_HEADS = 8
SWA_KV_HEADS = 2
SWA_HEAD_DIM = 64
SWA_WINDOW = 128
N_EXPERTS = 32
TOP_K = 4
D_FF = 1024
SWIGLU_LIMIT = 7.0
SWIGLU_ALPHA = 1.702
NORM_EPS = 1e-5

LANES = 128
SLAB_SUBLANES = D_MODEL // LANES
VMEM_LIMIT = 56 * 1024 * 1024

TM_PROJ = 512
TL_GLA = 1024
GLA_CUM_ROWS = 256
TM_ROUTE = 512
TM_EXPERT = 512
TM_COMBINE = 256
SWA_SUB = 4
TM_DISPATCH = 256

NEG_BIG = -1e30
ROUTE_ROWS = 16

C_QA, C_KA, C_VA, C_RA, C_QB, C_KB, C_KBS, C_VB, C_VBS, C_Z, C_END = (
    0, 256, 512, 1024, 1536, 2048, 2176, 2304, 2432, 2560, 2688)
Z_PIECES = 5


def _split3(x):
    hi = x.astype(BF16).astype(F32)
    r = x - hi
    mid = r.astype(BF16).astype(F32)
    lo = (r - mid).astype(BF16).astype(F32)
    return hi, mid, lo


class _SlabTileCopy:
    def __init__(self, flat, slab, sem, to_slab):
        self.copies = []
        for c in range(SLAB_SUBLANES):
            pair = (flat.at[:, pl.ds(c * LANES, LANES)], slab.at[:, c, :])
            src, dst = pair if to_slab else pair[::-1]
            self.copies.append(pltpu.make_async_copy(src, dst, sem))

    def start(self):
        for cp in self.copies:
            cp.start()

    def wait(self):
        for cp in self.copies:
            cp.wait()


def _rms(x, g):
    return x * lax.rsqrt(jnp.mean(x * x, axis=-1, keepdims=True) + NORM_EPS) * g


def _in_proj_kernel(x_ref, g_ref, w_ref, wup_ref, ba_ref,
                    qk_ref, vr_ref, la_ref, qb_ref, kv_ref):
    u = _rms(x_ref[...], g_ref[...]).astype(BF16)

    def proj(c0, c1):
        return jnp.dot(u, w_ref[:, c0:c1], preferred_element_type=F32)

    qk_ref[...] = proj(C_QA, C_VA).astype(BF16)
    vr_ref[:, :C_RA - C_VA] = proj(C_VA, C_RA).astype(BF16)
    vr_ref[:, C_RA - C_VA:] = proj(C_RA, C_QB).astype(BF16)
    qb_ref[...] = proj(C_QB, C_KB).astype(BF16)
    kv_ref[...] = proj(C_KB, C_Z).astype(BF16)
    z = proj(C_Z, C_END)
    hi, mid, lo = _split3(z)
    piece = lax.broadcasted_iota(jnp.int32, z.shape, 1) // GLA_RANK
    zc = jnp.where((piece == 0) | (piece == 3), hi, jnp.where(piece == 2, lo, mid)).astype(BF16)
    y = jnp.dot(zc, wup_ref[...], preferred_element_type=F32) + ba_ref[...]
    log_sig = jnp.minimum(y, 0.0) - jnp.log1p(jnp.exp(-jnp.abs(y)))
    la_ref[...] = log_sig * (1.0 / GLA_GATE_TAU)


def _in_proj(x2, g1, w_cat, wup_p, ba_p):
    T = x2.shape[0]
    tm = TM_PROJ
    row = lambda w: pl.BlockSpec((tm, w), lambda i: (i, 0))
    full = lambda a: pl.BlockSpec(a.shape, lambda i: (0,) * a.ndim)
    outs = [(512, BF16), (1024, BF16), (256, F32), (512, BF16), (512, BF16)]
    return pl.pallas_call(
        _in_proj_kernel,
        grid=(T // tm,),
        in_specs=[row(D_MODEL), full(g1), full(w_cat), full(wup_p), full(ba_p)],
        out_specs=[row(w) for w, _ in outs],
        out_shape=[jax.ShapeDtypeStruct((T, w), dt) for w, dt in outs],
        compiler_params=pltpu.CompilerParams(
            dimension_semantics=("arbitrary",), vmem_limit_bytes=VMEM_LIMIT),
        name="in_proj",
    )(x2, g1, w_cat, wup_p, ba_p)


def _gla_kernel(qk_ref, vr_ref, la_ref, g_ref, cum_ref, o_ref, st_ref):
    @pl.when(pl.program_id(1) == 0)
    def _():
        st_ref[...] = jnp.zeros_like(st_ref)

    tl = TL_GLA
    c = GLA_CHUNK
    kw = GLA_HEADS * GLA_DK
    vw = GLA_HEADS * GLA_DV
    causal = (lax.broadcasted_iota(jnp.int32, (c, c), 0)
              >= lax.broadcasted_iota(jnp.int32, (c, c), 1))
    low_half = lax.broadcasted_iota(jnp.int32, (c, LANES), 1) < GLA_DK
    g = g_ref[...]
    b_groups = []
    for grp in range(tl // GLA_CUM_ROWS):
        la = la_ref[grp * GLA_CUM_ROWS:(grp + 1) * GLA_CUM_ROWS, :]
        pieces = jnp.concatenate([p.astype(BF16) for p in _split3(la)], axis=1)
        b3 = jnp.dot(cum_ref[...], pieces, preferred_element_type=F32)
        b_groups.append(b3[:, :kw] + b3[:, kw:2 * kw] + b3[:, 2 * kw:])
    for ch in range(tl // c):
        rows = slice(ch * c, (ch + 1) * c)
        in_grp = (ch * c) % GLA_CUM_ROWS
        b = b_groups[(ch * c) // GLA_CUM_ROWS][in_grp:in_grp + c]
        b_last = b[c - 1:c]
        qf = qk_ref[rows, :kw].astype(F32)
        kf = qk_ref[rows, kw:].astype(F32)
        q_e = (qf * jnp.exp(b) * (GLA_DK ** -0.5)).astype(BF16)
        k_e = (kf * jnp.exp(-b)).astype(BF16)
        k_t = (kf * jnp.exp(b_last - b)).astype(BF16)
        decay = jnp.exp(b_last)
        for h in range(GLA_HEADS):
            ps = slice((h // 2) * LANES, (h // 2 + 1) * LANES)
            mine = low_half if h % 2 == 0 else ~low_half
            qp, kp = q_e[:, ps], k_e[:, ps]
            qh = jnp.where(mine, qp, jnp.zeros_like(qp))
            kth = jnp.where(mine, k_t[:, ps], jnp.zeros_like(qp))
            vs = slice(h * GLA_DV, (h + 1) * GLA_DV)
            vh = vr_ref[rows, vs]
            a = pl.dot(qh, kp, trans_b=True)
            a = jnp.where(causal, a, 0.0).astype(BF16)
            st = st_ref[h]
            o = (jnp.dot(a, vh, preferred_element_type=F32)
                 + pl.dot(qh, st.astype(BF16), trans_b=True))
            st_ref[h] = st * decay[:, ps] + pl.dot(vh, kth, trans_a=True)
            rh = vr_ref[rows, slice(vw + h * GLA_DV, vw + (h + 1) * GLA_DV)].astype(F32)
            o = _rms(o, g) * (rh * jax.nn.sigmoid(rh))
            o_ref[rows, vs] = o.astype(BF16)


def _gla(qk, vr, la, g, batch, seq):
    tl = TL_GLA
    nl = seq // tl
    row = lambda w: pl.BlockSpec((tl, w), lambda b, i: (b * nl + i, 0))
    r = np.arange(GLA_CUM_ROWS)
    cum = jnp.asarray((r[None, :] <= r[:, None])
                      & (r[None, :] // GLA_CHUNK == r[:, None] // GLA_CHUNK), BF16)
    return pl.pallas_call(
        _gla_kernel,
        grid=(batch, nl),
        in_specs=[row(512), row(1024), row(256),
                  pl.BlockSpec((1, GLA_DV), lambda b, i: (0, 0)),
                  pl.BlockSpec((GLA_CUM_ROWS, GLA_CUM_ROWS), lambda b, i: (0, 0))],
        out_specs=row(512),
        out_shape=jax.ShapeDtypeStruct((batch * seq, 512), BF16),
        scratch_shapes=[pltpu.VMEM((GLA_HEADS, GLA_DV, LANES), F32)],
        compiler_params=pltpu.CompilerParams(
            dimension_semantics=("arbitrary", "arbitrary"), vmem_limit_bytes=VMEM_LIMIT),
        name="gla",
    )(qk, vr, la, g, cum)


def _swa_kernel(sink_ref, bias0_ref, bias_ref, q_ref, kvc_ref, kvp_ref, g_ref, o_ref):
    w = SWA_WINDOW
    scale = jnp.asarray(SWA_HEAD_DIM ** -0.5, BF16)
    kv_lane = lax.broadcasted_iota(jnp.int32, (2 * w, LANES), 1)
    lane_lo = kv_lane < SWA_HEAD_DIM
    out_lo = lax.broadcasted_iota(jnp.int32, (w, LANES), 1) < SWA_HEAD_DIM
    ones_hi = jnp.where(kv_lane == SWA_HEAD_DIM, 1.0, 0.0).astype(BF16)
    ones_lo = jnp.where(kv_lane == 0, 1.0, 0.0).astype(BF16)
    g = g_ref[...]

    for sb in range(SWA_SUB):
        rows = slice(sb * w, (sb + 1) * w)
        if sb == 0:
            kv = jnp.concatenate([kvp_ref[...], kvc_ref[0:w, :]], axis=0)
            bias = bias0_ref
        else:
            kv = kvc_ref[(sb - 1) * w:(sb + 1) * w, :]
            bias = bias_ref
        k = kv[:, 0:LANES] * scale
        ks = kv[:, LANES:2 * LANES] * scale
        v, vs = kv[:, 2 * LANES:3 * LANES], kv[:, 3 * LANES:]
        zero = jnp.zeros_like(k)
        k_low = [jnp.where(lane_lo, k, zero), jnp.where(lane_lo, ks, zero)]
        k_high = [jnp.where(lane_lo, zero, ks), jnp.where(lane_lo, zero, k)]
        v_low = [jnp.where(lane_lo, v, zero) + ones_hi, jnp.where(lane_lo, vs, zero) + ones_hi]
        v_high = [jnp.where(lane_lo, zero, vs) + ones_lo, jnp.where(lane_lo, zero, v) + ones_lo]

        def weights(s, head):
            s = s + bias[0, head]
            sink = sink_ref[head]
            m = jnp.maximum(jnp.max(s, axis=-1, keepdims=True), sink)
            return jnp.exp(s - m).astype(BF16), jnp.exp(sink - m)

        for pair in range(SWA_HEADS // 2):
            j = (2 * pair) // (SWA_HEADS // SWA_KV_HEADS)
            cols = slice(pair * LANES, (pair + 1) * LANES)
            qp = q_ref[rows, cols]
            e0, sink0 = weights(pl.dot(qp, k_low[j], trans_b=True), 2 * pair)
            e1, sink1 = weights(pl.dot(qp, k_high[j], trans_b=True), 2 * pair + 1)
            oa = jnp.dot(e0, v_low[j], preferred_element_type=F32)
            ob = jnp.dot(e1, v_high[j], preferred_element_type=F32)
            inv0 = 1.0 / (oa[:, SWA_HEAD_DIM:SWA_HEAD_DIM + 1] + sink0)
            inv1 = 1.0 / (ob[:, 0:1] + sink1)
            o = jnp.where(out_lo, oa * inv0, ob * inv1)
            sq = o * o
            ms_lo = jnp.sum(jnp.where(out_lo, sq, 0.0), axis=-1, keepdims=True)
            ms_hi = jnp.sum(jnp.where(out_lo, 0.0, sq), axis=-1, keepdims=True)
            ms = jnp.where(out_lo, ms_lo, ms_hi) * (1.0 / SWA_HEAD_DIM)
            o = o * lax.rsqrt(ms + NORM_EPS) * g
            o_ref[rows, cols] = o.astype(BF16)


def _swa_bias():
    w = SWA_WINDOW
    slopes = 2.0 ** (-8.0 * np.arange(1, SWA_HEADS + 1, dtype=np.float64) / SWA_HEADS)
    rel = np.arange(w)[:, None] + w - np.arange(2 * w)[None, :]
    in_window = (rel >= 0) & (rel < w)
    exists = np.stack([np.arange(2 * w) >= w, np.ones(2 * w, bool)])
    valid = in_window[None] & exists[:, None, :]
    bias = -slopes[None, :, None, None] * rel[None, None].astype(np.float64)
    return np.where(valid[:, None], bias, NEG_BIG).astype(np.float32)


def _swa(qb, kv, sinks, g2, batch, seq):
    w = SWA_WINDOW
    rows = SWA_SUB * w
    nb = seq // rows
    cur = lambda: pl.BlockSpec((rows, 512), lambda b, n: (b * nb + n, 0))
    prev = pl.BlockSpec(
        (w, 512), lambda b, n: (jnp.maximum((b * nb + n) * SWA_SUB - 1, b * nb * SWA_SUB), 0))
    table = lambda index: pl.BlockSpec((1, SWA_HEADS, w, 2 * w), index)
    bias = jnp.asarray(_swa_bias())
    return pl.pallas_call(
        _swa_kernel,
        grid=(batch, nb),
        in_specs=[pl.BlockSpec(memory_space=pltpu.SMEM),
                  table(lambda b, n: (jnp.minimum(n, 1), 0, 0, 0)),
                  table(lambda b, n: (1, 0, 0, 0)),
                  cur(), cur(), prev,
                  pl.BlockSpec((1, LANES), lambda b, n: (0, 0))],
        out_specs=cur(),
        out_shape=jax.ShapeDtypeStruct((batch * seq, 512), BF16),
        compiler_params=pltpu.CompilerParams(
            dimension_semantics=("arbitrary", "arbitrary"), vmem_limit_bytes=VMEM_LIMIT),
        name="swa",
    )(sinks, bias, bias, qb, kv, kv, g2)


def _out_route_kernel(oa_ref, ob_ref, x_ref, woa_ref, wob_ref, g_ref, wrh_ref, wrl_ref, br_ref,
                      earlier_ref, h13_ref, hn3_ref, route_ref, cnt_ref, hbuf, hsem):
    i = pl.program_id(0)
    tm = TM_ROUTE

    @pl.when(i == 0)
    def _():
        cnt_ref[...] = jnp.zeros_like(cnt_ref)

    def stores(step, slot):
        rows = pl.ds(step * tm, tm)
        return [_SlabTileCopy(hbuf.at[slot, which], out.at[rows], hsem.at[slot], to_slab=True)
                for which, out in enumerate((h13_ref, hn3_ref))]

    slot = i % 2

    @pl.when(i >= 2)
    def _():
        for st in stores(i - 2, slot):
            st.wait()

    h1 = (x_ref[...]
          + jnp.dot(oa_ref[...], woa_ref[...], preferred_element_type=F32)
          + jnp.dot(ob_ref[...], wob_ref[...], preferred_element_type=F32))
    hn = _rms(h1, g_ref[...])
    hbuf[slot, 0] = h1
    hbuf[slot, 1] = hn
    for st in stores(i, slot):
        st.start()

    @pl.when(i == pl.num_programs(0) - 1)
    def _():
        @pl.when(i >= 1)
        def _():
            for st in stores(i - 1, 1 - slot):
                st.wait()
        for st in stores(i, slot):
            st.wait()

    hn_hi = hn.astype(BF16)
    hn_lo = (hn - hn_hi.astype(F32)).astype(BF16)
    logits = (pl.dot(wrh_ref[...], hn_hi, trans_b=True)
              + pl.dot(wrh_ref[...], hn_lo, trans_b=True)
              + pl.dot(wrl_ref[...], hn_hi, trans_b=True)) + br_ref[...]

    eid = lax.broadcasted_iota(jnp.int32, (N_EXPERTS, tm), 0)
    work = logits
    vals, idxs, sels = [], [], []
    for _ in range(TOP_K):
        m = jnp.max(work, axis=0, keepdims=True)
        idx = jnp.min(jnp.where(work == m, eid, N_EXPERTS), axis=0, keepdims=True)
        sel = eid == idx
        vals.append(m)
        idxs.append(idx)
        sels.append(sel)
        work = jnp.where(sel, -3e38, work)
    exps = [jnp.exp(v - vals[0]) for v in vals]
    inv_den = 1.0 / (exps[0] + exps[1] + exps[2] + exps[3])

    multihot = jnp.where(sels[0] | sels[1] | sels[2] | sels[3], 1.0, 0.0)
    before = (jnp.dot(multihot.astype(BF16), earlier_ref[...], preferred_element_type=F32)
              + cnt_ref[:, 0:1])
    ranks = [jnp.sum(jnp.where(s, before, 0.0), axis=0, keepdims=True) for s in sels]
    route_ref[...] = jnp.concatenate(
        [i.astype(F32) for i in idxs] + [e * inv_den for e in exps] + ranks
        + [jnp.zeros((ROUTE_ROWS - 3 * TOP_K, tm), F32)], axis=0)
    cnt_ref[...] += jnp.sum(multihot, axis=1, keepdims=True)


def _out_route(oa, ob, x2, woa, wob, g2, wr_hi, wr_lo, br):
    T = x2.shape[0]
    tm = TM_ROUTE
    row = lambda w: pl.BlockSpec((tm, w), lambda i: (i, 0))
    full = lambda a: pl.BlockSpec(a.shape, lambda i: (0,) * a.ndim)
    earlier = jnp.asarray(np.triu(np.ones((tm, tm), np.float32), k=1), BF16)
    return pl.pallas_call(
        _out_route_kernel,
        grid=(T // tm,),
        in_specs=[row(512), row(512), row(D_MODEL), full(woa), full(wob), full(g2),
                  full(wr_hi), full(wr_lo), full(br), full(earlier)],
        out_specs=[pl.BlockSpec(memory_space=pl.ANY), pl.BlockSpec(memory_space=pl.ANY),
                   pl.BlockSpec((ROUTE_ROWS, tm), lambda i: (0, i)),
                   pl.BlockSpec((N_EXPERTS, LANES), lambda i: (0, 0))],
        out_shape=[jax.ShapeDtypeStruct((T, SLAB_SUBLANES, LANES), F32),
                   jax.ShapeDtypeStruct((T, SLAB_SUBLANES, LANES), F32),
                   jax.ShapeDtypeStruct((ROUTE_ROWS, T), F32),
                   jax.ShapeDtypeStruct((N_EXPERTS, LANES), F32)],
        scratch_shapes=[pltpu.VMEM((2, 2, tm, D_MODEL), F32), pltpu.SemaphoreType.DMA((2,))],
        compiler_params=pltpu.CompilerParams(
            dimension_semantics=("arbitrary",), vmem_limit_bytes=VMEM_LIMIT,
            has_side_effects=True),
        name="out_route",
    )(oa, ob, x2, woa, wob, g2, wr_hi, wr_lo, br, earlier)


def _row_copy(src, s, dst, d, sem):
    return pltpu.make_async_copy(src.at[pl.ds(s, 1)], dst.at[pl.ds(d, 1)], sem)


def _dispatch_kernel(dest_ref, pad_start_ref, pad_cnt_ref, nused_ref, hn_ref, xs_ref,
                     zero_ref, sem, zsem):
    i = pl.program_id(0)
    n = pl.num_programs(0)
    tm = TM_EXPERT
    base = i * (TM_DISPATCH * TOP_K)
    for p in range(TM_DISPATCH * TOP_K):
        _row_copy(hn_ref, p // TOP_K, xs_ref, dest_ref[base + p], sem).start(priority=p % 2)
    for _ in range(TOP_K):
        pltpu.make_async_copy(hn_ref, xs_ref.at[pl.ds(0, TM_DISPATCH)], sem).wait()

    @pl.when(i == n - 1)
    def _():
        zero_ref[...] = jnp.zeros_like(zero_ref)
        sub = 8
        for e in range(N_EXPERTS):
            start, cnt = pad_start_ref[e], pad_cnt_ref[e]
            head = cnt & (sub - 1)
            body0 = start + head
            nbody = cnt // sub

            def head_copy(r):
                return _row_copy(zero_ref, 0, xs_ref, start + r, zsem)

            def body_copy(c):
                return pltpu.make_async_copy(zero_ref.at[pl.ds(0, sub)],
                                             xs_ref.at[pl.ds(body0 + c * sub, sub)], zsem)

            for copy, count in ((head_copy, head), (body_copy, nbody)):
                def zissue(r, carry, copy=copy):
                    copy(r).start()
                    return carry

                def zwait(r, carry, copy=copy):
                    copy(r).wait()
                    return carry

                lax.fori_loop(0, count, zissue, 0)
                lax.fori_loop(0, count, zwait, 0)

        def tail_copy(t):
            return pltpu.make_async_copy(zero_ref, xs_ref.at[pl.ds(t * tm, tm)], zsem)

        def tissue(t, carry):
            tail_copy(t).start()
            return carry

        def twait(t, carry):
            tail_copy(t).wait()
            return carry

        n_tiles = xs_ref.shape[0] // tm
        lax.fori_loop(nused_ref[0], n_tiles, tissue, 0)
        lax.fori_loop(nused_ref[0], n_tiles, twait, 0)


def _dispatch(dest, pad_start, pad_cnt, n_used, hn, n_rows):
    T = hn.shape[0]
    any_spec = pl.BlockSpec(memory_space=pl.ANY)
    return pl.pallas_call(
        _dispatch_kernel,
        grid_spec=pltpu.PrefetchScalarGridSpec(
            num_scalar_prefetch=4,
            grid=(T // TM_DISPATCH,),
            in_specs=[pl.BlockSpec((TM_DISPATCH, SLAB_SUBLANES, LANES), lambda i, *_: (i, 0, 0))],
            out_specs=any_spec,
            scratch_shapes=[pltpu.VMEM((TM_EXPERT, SLAB_SUBLANES, LANES), F32),
                            pltpu.SemaphoreType.DMA(()),
                            pltpu.SemaphoreType.DMA(())]),
        out_shape=jax.ShapeDtypeStruct((n_rows, SLAB_SUBLANES, LANES), F32),
        compiler_params=pltpu.CompilerParams(
            dimension_semantics=("arbitrary",), has_side_effects=True,
            vmem_limit_bytes=VMEM_LIMIT),
        name="dispatch",
    )(dest, pad_start, pad_cnt, n_used, hn)


def _experts_kernel(tile0_ref, ntile_ref, xs_ref, wg_ref, bg_ref, wl_ref, bl_ref, wd_ref, bd_ref,
                    ys_ref, wg_bf, wl_bf, wd_bf, xbuf, ybuf, xsem, ysem):
    e = pl.program_id(0)
    tm = TM_EXPERT
    t0, nt = tile0_ref[e], ntile_ref[e]

    def x_load(tile, slot):
        return _SlabTileCopy(xbuf.at[slot], xs_ref.at[pl.ds(tile * tm, tm)], xsem.at[slot],
                             to_slab=False)

    def y_store(tile, slot):
        return _SlabTileCopy(ybuf.at[slot], ys_ref.at[pl.ds(tile * tm, tm)], ysem.at[slot],
                             to_slab=True)

    @pl.when((e == 0) & (nt > 0))
    def _():
        x_load(t0, 0).start()

    @pl.when(nt > 0)
    def _():
        wg_bf[...] = wg_ref[0].astype(BF16)
        wl_bf[...] = wl_ref[0].astype(BF16)
        wd_bf[...] = wd_ref[0].astype(BF16)

    def tile_body(j, carry):
        slot = j % 2

        @pl.when(j + 1 < nt)
        def _():
            x_load(t0 + j + 1, 1 - slot).start()

        x_load(t0 + j, slot).wait()

        @pl.when(j >= 2)
        def _():
            y_store(t0 + j - 2, slot).wait()

        x = xbuf[slot].astype(BF16)
        glu = jnp.minimum(jnp.dot(x, wg_bf[...], preferred_element_type=F32) + bg_ref[0],
                          SWIGLU_LIMIT)
        lin = jnp.clip(jnp.dot(x, wl_bf[...], preferred_element_type=F32) + bl_ref[0],
                       -SWIGLU_LIMIT, SWIGLU_LIMIT)
        hid = glu * jax.nn.sigmoid(SWIGLU_ALPHA * glu) * (lin + 1.0)
        ybuf[slot] = jnp.dot(hid.astype(BF16), wd_bf[...], preferred_element_type=F32) + bd_ref[0]
        y_store(t0 + j, slot).start()
        return carry

    lax.fori_loop(0, nt, tile_body, 0)

    @pl.when(e + 1 < pl.num_programs(0))
    def _():
        @pl.when(ntile_ref[e + 1] > 0)
        def _():
            x_load(tile0_ref[e + 1], 0).start()

    @pl.when(nt >= 2)
    def _():
        y_store(t0 + nt - 2, nt % 2).wait()

    @pl.when(nt >= 1)
    def _():
        y_store(t0 + nt - 1, (nt - 1) % 2).wait()

    @pl.when(e == pl.num_programs(0) - 1)
    def _():
        ybuf[0] = jnp.zeros((tm, D_MODEL), F32)
        n_tiles = ys_ref.shape[0] // tm

        def tail_issue(t, carry):
            y_store(t, 0).start()
            return carry

        def tail_wait(t, carry):
            y_store(t, 0).wait()
            return carry

        lax.fori_loop(t0 + nt, n_tiles, tail_issue, 0)
        lax.fori_loop(t0 + nt, n_tiles, tail_wait, 0)


def _experts(tile0, ntile, xs, w_glu, b_glu, w_lin, b_lin, w_down, b_down):
    tm = TM_EXPERT
    any_spec = pl.BlockSpec(memory_space=pl.ANY)
    wspec = lambda: pl.BlockSpec((1, D_MODEL, D_FF), lambda e, *_: (e, 0, 0))
    bspec = lambda: pl.BlockSpec((1, 1, D_FF), lambda e, *_: (e, 0, 0))
    return pl.pallas_call(
        _experts_kernel,
        grid_spec=pltpu.PrefetchScalarGridSpec(
            num_scalar_prefetch=2,
            grid=(N_EXPERTS,),
            in_specs=[any_spec, wspec(), bspec(), wspec(), bspec(), wspec(), bspec()],
            out_specs=any_spec,
            scratch_shapes=[pltpu.VMEM((D_MODEL, D_FF), BF16)] * 3
            + [pltpu.VMEM((2, tm, D_MODEL), F32), pltpu.VMEM((2, tm, D_MODEL), F32),
               pltpu.SemaphoreType.DMA((2,)), pltpu.SemaphoreType.DMA((2,))]),
        out_shape=jax.ShapeDtypeStruct(xs.shape, F32),
        compiler_params=pltpu.CompilerParams(
            dimension_semantics=("arbitrary",), vmem_limit_bytes=VMEM_LIMIT,
            has_side_effects=True),
        name="experts",
    )(tile0, ntile, xs, w_glu, b_glu.reshape(N_EXPERTS, 1, D_FF),
      w_lin, b_lin.reshape(N_EXPERTS, 1, D_FF), w_down, b_down.reshape(N_EXPERTS, 1, D_MODEL))


def _combine_kernel(dest_ref, gate_ref, ys_ref, h1_ref, g_ref, out_ref, buf, obuf, sems, osems):
    s = pl.program_id(0)
    n = pl.num_programs(0) - 1
    tm = TM_COMBINE

    def issue(slot):
        base = s * (tm * TOP_K)
        for p in range(tm * TOP_K):
            r, k = divmod(p, TOP_K)
            _row_copy(ys_ref, dest_ref[base + p], buf.at[slot, k], r,
                      sems.at[slot]).start(priority=p % 2)

    def out_store(tile, slot):
        return _SlabTileCopy(out_ref.at[pl.ds(tile * tm, tm)], obuf.at[slot], osems.at[slot],
                             to_slab=False)

    for parity in range(2):
        @pl.when((s < n) & (s % 2 == parity))
        def _():
            issue(parity)

    @pl.when(s > 0)
    def _():
        tile = s - 1
        slot = tile % 2

        @pl.when(tile >= 2)
        def _():
            out_store(tile - 2, slot).wait()

        for k in range(TOP_K):
            pltpu.make_async_copy(ys_ref.at[pl.ds(0, tm)], buf.at[slot, k], sems.at[slot]).wait()
        g = g_ref[0]
        base = tile * (tm * TOP_K)
        for r in range(tm):
            acc = h1_ref[r]
            for k in range(TOP_K):
                acc = acc + gate_ref[base + r * TOP_K + k] * buf[slot, k, r]
            ss = jnp.sum(jnp.sum(acc * acc, axis=1, keepdims=True), axis=0, keepdims=True)
            obuf[slot, r] = acc * lax.rsqrt(ss * (1.0 / D_MODEL) + NORM_EPS) * g
        out_store(tile, slot).start()

    @pl.when(s == n)
    def _():
        @pl.when(n >= 2)
        def _():
            out_store(n - 2, n % 2).wait()
        out_store(n - 1, (n - 1) % 2).wait()


def _combine(dest, ys3, h13, gate, fg):
    T = h13.shape[0]
    tm = TM_COMBINE
    slab = (SLAB_SUBLANES, LANES)
    prev_tile = lambda s, *_: (jnp.maximum(s - 1, 0), 0, 0)
    return pl.pallas_call(
        _combine_kernel,
        grid_spec=pltpu.PrefetchScalarGridSpec(
            num_scalar_prefetch=2,
            grid=(T // tm + 1,),
            in_specs=[pl.BlockSpec(memory_space=pl.ANY),
                      pl.BlockSpec((tm,) + slab, prev_tile),
                      pl.BlockSpec((1,) + slab, lambda s, *_: (0, 0, 0))],
            out_specs=pl.BlockSpec(memory_space=pl.ANY),
            scratch_shapes=[pltpu.VMEM((2, TOP_K, tm) + slab, F32),
                            pltpu.VMEM((2, tm) + slab, F32),
                            pltpu.SemaphoreType.DMA((2,)),
                            pltpu.SemaphoreType.DMA((2,))]),
        out_shape=jax.ShapeDtypeStruct((T, D_MODEL), F32),
        compiler_params=pltpu.CompilerParams(
            dimension_semantics=("arbitrary",), vmem_limit_bytes=VMEM_LIMIT,
            has_side_effects=True),
        name="combine",
    )(dest, gate.reshape(-1), ys3, h13, fg.reshape((1,) + slab))


def _swap_halves(w):
    h = w.shape[-1] // 2
    return jnp.concatenate([w[..., h:], w[..., :h]], axis=-1)


def _layer(x2, batch, seq, norm1_g, w_in, w_alpha_up, b_alpha, gla_norm_g, swa_sinks, swa_norm_g,
           w_out, norm2_g, w_router, b_router, w_glu, b_glu, w_lin, b_lin, w_down, b_down):
    T = x2.shape[0]
    kb_w, vb_w = w_in[:, 2064:2192], w_in[:, 2192:2320]
    w_z = w_in[:, 1536:1552]
    w_cat = jnp.concatenate([
        w_in[:, 0:1536], w_in[:, 1552:2064],
        kb_w, _swap_halves(kb_w), vb_w, _swap_halves(vb_w),
        jnp.pad(jnp.tile(w_z, (1, Z_PIECES)), [(0, 0), (0, LANES - Z_PIECES * GLA_RANK)]),
    ], axis=1).astype(BF16)
    wup_hi = w_alpha_up.astype(BF16)
    wup_lo = (w_alpha_up - wup_hi.astype(F32)).astype(BF16)
    wup_cat = jnp.pad(jnp.concatenate([wup_hi, wup_hi, wup_hi, wup_lo, wup_lo], axis=0),
                      [(0, LANES - Z_PIECES * GLA_RANK), (0, 0)])

    qk, vr, la, qb, kv = _in_proj(
        x2, norm1_g.reshape(1, -1), w_cat, wup_cat, b_alpha.reshape(1, -1))
    oa = _gla(qk, vr, la, gla_norm_g.reshape(1, -1), batch, seq)
    ob = _swa(qb, kv, swa_sinks, jnp.tile(swa_norm_g, 2).reshape(1, -1), batch, seq)

    wo = w_out.astype(BF16)
    wr_t = w_router.T
    wr_hi = wr_t.astype(BF16)
    wr_lo = (wr_t - wr_hi.astype(F32)).astype(BF16)
    h1, hn, route, cnt = _out_route(oa, ob, x2, wo[:512], wo[512:], norm2_g.reshape(1, -1),
                                    wr_hi, wr_lo, b_router.reshape(-1, 1))

    tm = TM_EXPERT
    n_tiles = T * TOP_K // tm + N_EXPERTS
    counts = cnt[:, 0].astype(jnp.int32)
    padded = (counts + tm - 1) // tm * tm
    pends = jnp.cumsum(padded)
    pstarts = pends - padded
    top_idx = route[0:TOP_K].T.astype(jnp.int32)
    gate = route[TOP_K:2 * TOP_K].T
    rank = route[2 * TOP_K:3 * TOP_K].T.astype(jnp.int32)
    seg_start = jnp.sum(jnp.where(top_idx[..., None] == jnp.arange(N_EXPERTS), pstarts, 0), axis=-1)
    dest = (seg_start + rank).reshape(-1)
    n_used = (pends[-1] // tm).reshape(1)
    xs = _dispatch(dest, pstarts + counts, padded - counts, n_used, hn, n_tiles * tm)
    ys = _experts(pstarts // tm, padded // tm, xs, w_glu, b_glu, w_lin, b_lin, w_down, b_down)
    return dest, ys, h1, gate


def kernel(x, norm1_g, w_in, w_alpha_up, b_alpha, gla_norm_g, swa_sinks, swa_norm_g, w_out,
           norm2_g, w_router, b_router, w_glu, b_glu, w_lin, b_lin, w_down, b_down, final_g):
    batch, seq, d = x.shape
    assert norm1_g.shape[0] == 1, "single-layer problem"
    x2 = x.reshape(batch * seq, d)
    dest, ys, h1, gate = _layer(
        x2, batch, seq, norm1_g[0], w_in[0], w_alpha_up[0], b_alpha[0], gla_norm_g[0],
        swa_sinks[0], swa_norm_g[0], w_out[0], norm2_g[0], w_router[0], b_router[0],
        w_glu[0], b_glu[0], w_lin[0], b_lin[0], w_down[0], b_down[0])
    out = _combine(dest, ys, h1, gate, final_g.reshape(1, -1))
    return out.reshape(batch, seq, d)
```

```python
import numpy as np
import jax
import jax.numpy as jnp
from jax import lax
from jax.experimental import pallas as pl
from jax.experimental.pallas import tpu as pltpu

F32 = jnp.float32
BF16 = jnp.bfloat16

D_MODEL = 1024
GLA_HEADS = 4
GLA_DK = 64
GLA_DV = 128
GLA_RANK = 16
GLA_GATE_TAU = 16.0
GLA_CHUNK = 64
SWA_HEADS = 8
SWA_KV_HEADS = 2
SWA_HEAD_DIM = 64
SWA_WINDOW = 128
N_EXPERTS = 32
TOP_K = 4
D_FF = 1024
SWIGLU_LIMIT = 7.0
SWIGLU_ALPHA = 1.702
NORM_EPS = 1e-5

LANES = 128
SLAB_SUBLANES = D_MODEL // LANES
VMEM_LIMIT = 56 * 1024 * 1024

TM_PROJ = 512
TL_GLA = 1024
GLA_CUM_ROWS = 256
TM_ROUTE = 512
TM_EXPERT = 512
TM_COMBINE = 256
SWA_SUB = 4
TM_DISPATCH = 256

NEG_BIG = -1e30
ROUTE_ROWS = 16

C_QA, C_KA, C_VA, C_RA, C_QB, C_KB, C_KBS, C_VB, C_VBS, C_Z, C_END = (
    0, 256, 512, 1024, 1536, 2048, 2176, 2304, 2432, 2560, 2688)
Z_PIECES = 5


def _split3(x):
    hi = x.astype(BF16).astype(F32)
    r = x - hi
    mid = r.astype(BF16).astype(F32)
    lo = (r - mid).astype(BF16).astype(F32)
    return hi, mid, lo


class _SlabTileCopy:
    def __init__(self, flat, slab, sem, to_slab):
        self.copies = []
        for c in range(SLAB_SUBLANES):
            pair = (flat.at[:, pl.ds(c * LANES, LANES)], slab.at[:, c, :])
            src, dst = pair if to_slab else pair[::-1]
            self.copies.append(pltpu.make_async_copy(src, dst, sem))

    def start(self):
        for cp in self.copies:
            cp.start()

    def wait(self):
        for cp in self.copies:
            cp.wait()


def _rms(x, g):
    return x * lax.rsqrt(jnp.mean(x * x, axis=-1, keepdims=True) + NORM_EPS) * g


def _in_proj_kernel(x_ref, g_ref, w_ref, wup_ref, ba_ref,
                    qk_ref, vr_ref, la_ref, qb_ref, kv_ref):
    u = _rms(x_ref[...], g_ref[...]).astype(BF16)

    def proj(c0, c1):
        return jnp.dot(u, w_ref[:, c0:c1], preferred_element_type=F32)

    qk_ref[...] = proj(C_QA, C_VA).astype(BF16)
    vr_ref[:, :C_RA - C_VA] = proj(C_VA, C_RA).astype(BF16)
    vr_ref[:, C_RA - C_VA:] = proj(C_RA, C_QB).astype(BF16)
    qb_ref[...] = proj(C_QB, C_KB).astype(BF16)
    kv_ref[...] = proj(C_KB, C_Z).astype(BF16)
    z = proj(C_Z, C_END)
    hi, mid, lo = _split3(z)
    piece = lax.broadcasted_iota(jnp.int32, z.shape, 1) // GLA_RANK
    zc = jnp.where((piece == 0) | (piece == 3), hi, jnp.where(piece == 2, lo, mid)).astype(BF16)
    y = jnp.dot(zc, wup_ref[...], preferred_element_type=F32) + ba_ref[...]
    log_sig = jnp.minimum(y, 0.0) - jnp.log1p(jnp.exp(-jnp.abs(y)))
    la_ref[...] = log_sig * (1.0 / GLA_GATE_TAU)


def _in_proj(x2, g1, w_cat, wup_p, ba_p):
    T = x2.shape[0]
    tm = TM_PROJ
    row = lambda w: pl.BlockSpec((tm, w), lambda i: (i, 0))
    full = lambda a: pl.BlockSpec(a.shape, lambda i: (0,) * a.ndim)
    outs = [(512, BF16), (1024, BF16), (256, F32), (512, BF16), (512, BF16)]
    return pl.pallas_call(
        _in_proj_kernel,
        grid=(T // tm,),
        in_specs=[row(D_MODEL), full(g1), full(w_cat), full(wup_p), full(ba_p)],
        out_specs=[row(w) for w, _ in outs],
        out_shape=[jax.ShapeDtypeStruct((T, w), dt) for w, dt in outs],
        compiler_params=pltpu.CompilerParams(
            dimension_semantics=("arbitrary",), vmem_limit_bytes=VMEM_LIMIT),
        name="in_proj",
    )(x2, g1, w_cat, wup_p, ba_p)


def _gla_kernel(qk_ref, vr_ref, la_ref, g_ref, cum_ref, o_ref, st_ref):
    @pl.when(pl.program_id(1) == 0)
    def _():
        st_ref[...] = jnp.zeros_like(st_ref)

    tl = TL_GLA
    c = GLA_CHUNK
    kw = GLA_HEADS * GLA_DK
    vw = GLA_HEADS * GLA_DV
    causal = (lax.broadcasted_iota(jnp.int32, (c, c), 0)
              >= lax.broadcasted_iota(jnp.int32, (c, c), 1))
    low_half = lax.broadcasted_iota(jnp.int32, (c, LANES), 1) < GLA_DK
    g = g_ref[...]
    b_groups = []
    for grp in range(tl // GLA_CUM_ROWS):
        la = la_ref[grp * GLA_CUM_ROWS:(grp + 1) * GLA_CUM_ROWS, :]
        pieces = jnp.concatenate([p.astype(BF16) for p in _split3(la)], axis=1)
        b3 = jnp.dot(cum_ref[...], pieces, preferred_element_type=F32)
        b_groups.append(b3[:, :kw] + b3[:, kw:2 * kw] + b3[:, 2 * kw:])
    for ch in range(tl // c):
        rows = slice(ch * c, (ch + 1) * c)
        in_grp = (ch * c) % GLA_CUM_ROWS
        b = b_groups[(ch * c) // GLA_CUM_ROWS][in_grp:in_grp + c]
        b_last = b[c - 1:c]
        qf = qk_ref[rows, :kw].astype(F32)
        kf = qk_ref[rows, kw:].astype(F32)
        q_e = (qf * jnp.exp(b) * (GLA_DK ** -0.5)).astype(BF16)
        k_e = (kf * jnp.exp(-b)).astype(BF16)
        k_t = (kf * jnp.exp(b_last - b)).astype(BF16)
        decay = jnp.exp(b_last)
        for h in range(GLA_HEADS):
            ps = slice((h // 2) * LANES, (h // 2 + 1) * LANES)
            mine = low_half if h % 2 == 0 else ~low_half
            qp, kp = q_e[:, ps], k_e[:, ps]
            qh = jnp.where(mine, qp, jnp.zeros_like(qp))
            kth = jnp.where(mine, k_t[:, ps], jnp.zeros_like(qp))
            vs = slice(h * GLA_DV, (h + 1) * GLA_DV)
            vh = vr_ref[rows, vs]
            a = pl.dot(qh, kp, trans_b=True)
            a = jnp.where(causal, a, 0.0).astype(BF16)
            st = st_ref[h]
            o = (jnp.dot(a, vh, preferred_element_type=F32)
                 + pl.dot(qh, st.astype(BF16), trans_b=True))
            st_ref[h] = st * decay[:, ps] + pl.dot(vh, kth, trans_a=True)
            rh = vr_ref[rows, slice(vw + h * GLA_DV, vw + (h + 1) * GLA_DV)].astype(F32)
            o = _rms(o, g) * (rh * jax.nn.sigmoid(rh))
            o_ref[rows, vs] = o.astype(BF16)


def _gla(qk, vr, la, g, batch, seq):
    tl = TL_GLA
    nl = seq // tl
    row = lambda w: pl.BlockSpec((tl, w), lambda b, i: (b * nl + i, 0))
    r = np.arange(GLA_CUM_ROWS)
    cum = jnp.asarray((r[None, :] <= r[:, None])
                      & (r[None, :] // GLA_CHUNK == r[:, None] // GLA_CHUNK), BF16)
    return pl.pallas_call(
        _gla_kernel,
        grid=(batch, nl),
        in_specs=[row(512), row(1024), row(256),
                  pl.BlockSpec((1, GLA_DV), lambda b, i: (0, 0)),
                  pl.BlockSpec((GLA_CUM_ROWS, GLA_CUM_ROWS), lambda b, i: (0, 0))],
        out_specs=row(512),
        out_shape=jax.ShapeDtypeStruct((batch * seq, 512), BF16),
        scratch_shapes=[pltpu.VMEM((GLA_HEADS, GLA_DV, LANES), F32)],
        compiler_params=pltpu.CompilerParams(
            dimension_semantics=("arbitrary", "arbitrary"), vmem_limit_bytes=VMEM_LIMIT),
        name="gla",
    )(qk, vr, la, g, cum)


def _swa_kernel(sink_ref, bias0_ref, bias_ref, q_ref, kvc_ref, kvp_ref, g_ref, o_ref):
    w = SWA_WINDOW
    scale = jnp.asarray(SWA_HEAD_DIM ** -0.5, BF16)
    kv_lane = lax.broadcasted_iota(jnp.int32, (2 * w, LANES), 1)
    lane_lo = kv_lane < SWA_HEAD_DIM
    out_lo = lax.broadcasted_iota(jnp.int32, (w, LANES), 1) < SWA_HEAD_DIM
    ones_hi = jnp.where(kv_lane == SWA_HEAD_DIM, 1.0, 0.0).astype(BF16)
    ones_lo = jnp.where(kv_lane == 0, 1.0, 0.0).astype(BF16)
    g = g_ref[...]

    for sb in range(SWA_SUB):
        rows = slice(sb * w, (sb + 1) * w)
        if sb == 0:
            kv = jnp.concatenate([kvp_ref[...], kvc_ref[0:w, :]], axis=0)
            bias = bias0_ref
        else:
            kv = kvc_ref[(sb - 1) * w:(sb + 1) * w, :]
            bias = bias_ref
        k = kv[:, 0:LANES] * scale
        ks = kv[:, LANES:2 * LANES] * scale
        v, vs = kv[:, 2 * LANES:3 * LANES], kv[:, 3 * LANES:]
        zero = jnp.zeros_like(k)
        k_low = [jnp.where(lane_lo, k, zero), jnp.where(lane_lo, ks, zero)]
        k_high = [jnp.where(lane_lo, zero, ks), jnp.where(lane_lo, zero, k)]
        v_low = [jnp.where(lane_lo, v, zero) + ones_hi, jnp.where(lane_lo, vs, zero) + ones_hi]
        v_high = [jnp.where(lane_lo, zero, vs) + ones_lo, jnp.where(lane_lo, zero, v) + ones_lo]

        def weights(s, head):
            s = s + bias[0, head]
            sink = sink_ref[head]
            m = jnp.maximum(jnp.max(s, axis=-1, keepdims=True), sink)
            return jnp.exp(s - m).astype(BF16), jnp.exp(sink - m)

        for pair in range(SWA_HEADS // 2):
            j = (2 * pair) // (SWA_HEADS // SWA_KV_HEADS)
            cols = slice(pair * LANES, (pair + 1) * LANES)
            qp = q_ref[rows, cols]
            e0, sink0 = weights(pl.dot(qp, k_low[j], trans_b=True), 2 * pair)
            e1, sink1 = weights(pl.dot(qp, k_high[j], trans_b=True), 2 * pair + 1)
            oa = jnp.dot(e0, v_low[j], preferred_element_type=F32)
            ob = jnp.dot(e1, v_high[j], preferred_element_type=F32)
            inv0 = 1.0 / (oa[:, SWA_HEAD_DIM:SWA_HEAD_DIM + 1] + sink0)
            inv1 = 1.0 / (ob[:, 0:1] + sink1)
            o = jnp.where(out_lo, oa * inv0, ob * inv1)
            sq = o * o
            ms_lo = jnp.sum(jnp.where(out_lo, sq, 0.0), axis=-1, keepdims=True)
            ms_hi = jnp.sum(jnp.where(out_lo, 0.0, sq), axis=-1, keepdims=True)
            ms = jnp.where(out_lo, ms_lo, ms_hi) * (1.0 / SWA_HEAD_DIM)
            o = o * lax.rsqrt(ms + NORM_EPS) * g
            o_ref[rows, cols] = o.astype(BF16)


def _swa_bias():
    w = SWA_WINDOW
    slopes = 2.0 ** (-8.0 * np.arange(1, SWA_HEADS + 1, dtype=np.float64) / SWA_HEADS)
    rel = np.arange(w)[:, None] + w - np.arange(2 * w)[None, :]
    in_window = (rel >= 0) & (rel < w)
    exists = np.stack([np.arange(2 * w) >= w, np.ones(2 * w, bool)])
    valid = in_window[None] & exists[:, None, :]
    bias = -slopes[None, :, None, None] * rel[None, None].astype(np.float64)
    return np.where(valid[:, None], bias, NEG_BIG).astype(np.float32)


def _swa(qb, kv, sinks, g2, batch, seq):
    w = SWA_WINDOW
    rows = SWA_SUB * w
    nb = seq // rows
    cur = lambda: pl.BlockSpec((rows, 512), lambda b, n: (b * nb + n, 0))
    prev = pl.BlockSpec(
        (w, 512), lambda b, n: (jnp.maximum((b * nb + n) * SWA_SUB - 1, b * nb * SWA_SUB), 0))
    table = lambda index: pl.BlockSpec((1, SWA_HEADS, w, 2 * w), index)
    bias = jnp.asarray(_swa_bias())
    return pl.pallas_call(
        _swa_kernel,
        grid=(batch, nb),
        in_specs=[pl.BlockSpec(memory_space=pltpu.SMEM),
                  table(lambda b, n: (jnp.minimum(n, 1), 0, 0, 0)),
                  table(lambda b, n: (1, 0, 0, 0)),
                  cur(), cur(), prev,
                  pl.BlockSpec((1, LANES), lambda b, n: (0, 0))],
        out_specs=cur(),
        out_shape=jax.ShapeDtypeStruct((batch * seq, 512), BF16),
        compiler_params=pltpu.CompilerParams(
            dimension_semantics=("arbitrary", "arbitrary"), vmem_limit_bytes=VMEM_LIMIT),
        name="swa",
    )(sinks, bias, bias, qb, kv, kv, g2)


def _out_route_kernel(oa_ref, ob_ref, x_ref, wo_ref, g_ref, wrh_ref, wrl_ref, br_ref,
                      earlier_ref, h13_ref, hn3_ref, route_ref, cnt_ref, hbuf, hsem):
    i = pl.program_id(0)
    tm = TM_ROUTE

    @pl.when(i == 0)
    def _():
        cnt_ref[...] = jnp.zeros_like(cnt_ref)

    def stores(step, slot):
        rows = pl.ds(step * tm, tm)
        return [_SlabTileCopy(hbuf.at[slot, which], out.at[rows], hsem.at[slot], to_slab=True)
                for which, out in enumerate((h13_ref, hn3_ref))]

    slot = i % 2

    @pl.when(i >= 2)
    def _():
        for st in stores(i - 2, slot):
            st.wait()

    h1 = (x_ref[...]
          + jnp.dot(oa_ref[...], wo_ref[:oa_ref.shape[1], :], preferred_element_type=F32)
          + jnp.dot(ob_ref[...], wo_ref[oa_ref.shape[1]:, :], preferred_element_type=F32))
    hn = _rms(h1, g_ref[...])
    hbuf[slot, 0] = h1
    hbuf[slot, 1] = hn
    for st in stores(i, slot):
        st.start()

    @pl.when(i == pl.num_programs(0) - 1)
    def _():
        @pl.when(i >= 1)
        def _():
            for st in stores(i - 1, 1 - slot):
                st.wait()
        for st in stores(i, slot):
            st.wait()

    hn_hi = hn.astype(BF16)
    hn_lo = (hn - hn_hi.astype(F32)).astype(BF16)
    logits = (pl.dot(wrh_ref[...], hn_hi, trans_b=True)
              + pl.dot(wrh_ref[...], hn_lo, trans_b=True)
              + pl.dot(wrl_ref[...], hn_hi, trans_b=True)) + br_ref[...]

    eid = lax.broadcasted_iota(jnp.int32, (N_EXPERTS, tm), 0)
    work = logits
    vals, idxs, sels = [], [], []
    for _ in range(TOP_K):
        m = jnp.max(work, axis=0, keepdims=True)
        idx = jnp.min(jnp.where(work == m, eid, N_EXPERTS), axis=0, keepdims=True)
        sel = eid == idx
        vals.append(m)
        idxs.append(idx)
        sels.append(sel)
        work = jnp.where(sel, -3e38, work)
    exps = [jnp.exp(v - vals[0]) for v in vals]
    inv_den = 1.0 / (exps[0] + exps[1] + exps[2] + exps[3])

    multihot = jnp.where(sels[0] | sels[1] | sels[2] | sels[3], 1.0, 0.0)
    before = (jnp.dot(multihot.astype(BF16), earlier_ref[...], preferred_element_type=F32)
              + cnt_ref[:, 0:1])
    ranks = [jnp.sum(jnp.where(s, before, 0.0), axis=0, keepdims=True) for s in sels]
    route_ref[...] = jnp.concatenate(
        [i.astype(F32) for i in idxs] + [e * inv_den for e in exps] + ranks
        + [jnp.zeros((ROUTE_ROWS - 3 * TOP_K, tm), F32)], axis=0)
    cnt_ref[...] += jnp.sum(multihot, axis=1, keepdims=True)


def _out_route(oa, ob, x2, wo, g2, wr_hi, wr_lo, br):
    T = x2.shape[0]
    tm = TM_ROUTE
    row = lambda w: pl.BlockSpec((tm, w), lambda i: (i, 0))
    full = lambda a: pl.BlockSpec(a.shape, lambda i: (0,) * a.ndim)
    earlier = jnp.asarray(np.triu(np.ones((tm, tm), np.float32), k=1), BF16)
    return pl.pallas_call(
        _out_route_kernel,
        grid=(T // tm,),
        in_specs=[row(512), row(512), row(D_MODEL), full(wo), full(g2),
                  full(wr_hi), full(wr_lo), full(br), full(earlier)],
        out_specs=[pl.BlockSpec(memory_space=pl.ANY), pl.BlockSpec(memory_space=pl.ANY),
                   pl.BlockSpec((ROUTE_ROWS, tm), lambda i: (0, i)),
                   pl.BlockSpec((N_EXPERTS, LANES), lambda i: (0, 0))],
        out_shape=[jax.ShapeDtypeStruct((T, SLAB_SUBLANES, LANES), F32),
                   jax.ShapeDtypeStruct((T, SLAB_SUBLANES, LANES), F32),
                   jax.ShapeDtypeStruct((ROUTE_ROWS, T), F32),
                   jax.ShapeDtypeStruct((N_EXPERTS, LANES), F32)],
        scratch_shapes=[pltpu.VMEM((2, 2, tm, D_MODEL), F32), pltpu.SemaphoreType.DMA((2,))],
        compiler_params=pltpu.CompilerParams(
            dimension_semantics=("arbitrary",), vmem_limit_bytes=VMEM_LIMIT,
            has_side_effects=True),
        name="out_route",
    )(oa, ob, x2, wo, g2, wr_hi, wr_lo, br, earlier)


def _row_copy(src, s, dst, d, sem):
    return pltpu.make_async_copy(src.at[pl.ds(s, 1)], dst.at[pl.ds(d, 1)], sem)


def _dispatch_kernel(dest_ref, pad_start_ref, pad_cnt_ref, nused_ref, hn_ref, xs_ref,
                     zero_ref, sem, zsem):
    i = pl.program_id(0)
    n = pl.num_programs(0)
    tm = TM_EXPERT
    n_tok = dest_ref.shape[0] // TOP_K
    base = i * TM_DISPATCH
    for p in range(TM_DISPATCH * TOP_K):
        r, k = divmod(p, TOP_K)
        _row_copy(hn_ref, r, xs_ref, dest_ref[base + k * n_tok + r], sem).start(priority=p % 2)
    for _ in range(TOP_K):
        pltpu.make_async_copy(hn_ref, xs_ref.at[pl.ds(0, TM_DISPATCH)], sem).wait()

    @pl.when(i == n - 1)
    def _():
        zero_ref[...] = jnp.zeros_like(zero_ref)
        sub = 8
        for e in range(N_EXPERTS):
            start, cnt = pad_start_ref[e], pad_cnt_ref[e]
            head = cnt & (sub - 1)
            body0 = start + head
            nbody = cnt // sub

            def head_copy(r):
                return _row_copy(zero_ref, 0, xs_ref, start + r, zsem)

            def body_copy(c):
                return pltpu.make_async_copy(zero_ref.at[pl.ds(0, sub)],
                                             xs_ref.at[pl.ds(body0 + c * sub, sub)], zsem)

            for copy, count in ((head_copy, head), (body_copy, nbody)):
                def zissue(r, carry, copy=copy):
                    copy(r).start()
                    return carry

                def zwait(r, carry, copy=copy):
                    copy(r).wait()
                    return carry

                lax.fori_loop(0, count, zissue, 0)
                lax.fori_loop(0, count, zwait, 0)

        def tail_copy(t):
            return pltpu.make_async_copy(zero_ref, xs_ref.at[pl.ds(t * tm, tm)], zsem)

        def tissue(t, carry):
            tail_copy(t).start()
            return carry

        def twait(t, carry):
            tail_copy(t).wait()
            return carry

        n_tiles = xs_ref.shape[0] // tm
        lax.fori_loop(nused_ref[0], n_tiles, tissue, 0)
        lax.fori_loop(nused_ref[0], n_tiles, twait, 0)


def _dispatch(dest, pad_start, pad_cnt, n_used, hn, n_rows):
    T = hn.shape[0]
    any_spec = pl.BlockSpec(memory_space=pl.ANY)
    return pl.pallas_call(
        _dispatch_kernel,
        grid_spec=pltpu.PrefetchScalarGridSpec(
            num_scalar_prefetch=4,
            grid=(T // TM_DISPATCH,),
            in_specs=[pl.BlockSpec((TM_DISPATCH, SLAB_SUBLANES, LANES), lambda i, *_: (i, 0, 0))],
            out_specs=any_spec,
            scratch_shapes=[pltpu.VMEM((TM_EXPERT, SLAB_SUBLANES, LANES), F32),
                            pltpu.SemaphoreType.DMA(()),
                            pltpu.SemaphoreType.DMA(())]),
        out_shape=jax.ShapeDtypeStruct((n_rows, SLAB_SUBLANES, LANES), F32),
        compiler_params=pltpu.CompilerParams(
            dimension_semantics=("arbitrary",), has_side_effects=True,
            vmem_limit_bytes=VMEM_LIMIT),
        name="dispatch",
    )(dest, pad_start, pad_cnt, n_used, hn)


def _experts_kernel(tile0_ref, ntile_ref, xs_ref, wg_ref, bg_ref, wl_ref, bl_ref, wd_ref, bd_ref,
                    ys_ref, wg_bf, wl_bf, wd_bf, xbuf, ybuf, xsem, ysem):
    e = pl.program_id(0)
    tm = TM_EXPERT
    t0, nt = tile0_ref[e], ntile_ref[e]

    def x_load(tile, slot):
        return _SlabTileCopy(xbuf.at[slot], xs_ref.at[pl.ds(tile * tm, tm)], xsem.at[slot],
                             to_slab=False)

    def y_store(tile, slot):
        return _SlabTileCopy(ybuf.at[slot], ys_ref.at[pl.ds(tile * tm, tm)], ysem.at[slot],
                             to_slab=True)

    @pl.when((e == 0) & (nt > 0))
    def _():
        x_load(t0, 0).start()

    @pl.when(nt > 0)
    def _():
        wg_bf[...] = wg_ref[0].astype(BF16)
        wl_bf[...] = wl_ref[0].astype(BF16)
        wd_bf[...] = wd_ref[0].astype(BF16)

    def tile_body(j, carry):
        slot = j % 2

        @pl.when(j + 1 < nt)
        def _():
            x_load(t0 + j + 1, 1 - slot).start()

        x_load(t0 + j, slot).wait()

        @pl.when(j >= 2)
        def _():
            y_store(t0 + j - 2, slot).wait()

        x = xbuf[slot].astype(BF16)
        glu = jnp.minimum(jnp.dot(x, wg_bf[...], preferred_element_type=F32) + bg_ref[0],
                          SWIGLU_LIMIT)
        lin = jnp.clip(jnp.dot(x, wl_bf[...], preferred_element_type=F32) + bl_ref[0],
                       -SWIGLU_LIMIT, SWIGLU_LIMIT)
        hid = glu * jax.nn.sigmoid(SWIGLU_ALPHA * glu) * (lin + 1.0)
        ybuf[slot] = jnp.dot(hid.astype(BF16), wd_bf[...], preferred_element_type=F32) + bd_ref[0]
        y_store(t0 + j, slot).start()
        return carry

    lax.fori_loop(0, nt, tile_body, 0)

    @pl.when(e + 1 < pl.num_programs(0))
    def _():
        @pl.when(ntile_ref[e + 1] > 0)
        def _():
            x_load(tile0_ref[e + 1], 0).start()

    @pl.when(nt >= 2)
    def _():
        y_store(t0 + nt - 2, nt % 2).wait()

    @pl.when(nt >= 1)
    def _():
        y_store(t0 + nt - 1, (nt - 1) % 2).wait()

    @pl.when(e == pl.num_programs(0) - 1)
    def _():
        ybuf[0] = jnp.zeros((tm, D_MODEL), F32)
        n_tiles = ys_ref.shape[0] // tm

        def tail_issue(t, carry):
            y_store(t, 0).start()
            return carry

        def tail_wait(t, carry):
            y_store(t, 0).wait()
            return carry

        lax.fori_loop(t0 + nt, n_tiles, tail_issue, 0)
        lax.fori_loop(t0 + nt, n_tiles, tail_wait, 0)


def _experts(tile0, ntile, xs, w_glu, b_glu, w_lin, b_lin, w_down, b_down):
    tm = TM_EXPERT
    any_spec = pl.BlockSpec(memory_space=pl.ANY)
    wspec = lambda: pl.BlockSpec((1, D_MODEL, D_FF), lambda e, *_: (e, 0, 0))
    bspec = lambda: pl.BlockSpec((1, 1, D_FF), lambda e, *_: (e, 0, 0))
    return pl.pallas_call(
        _experts_kernel,
        grid_spec=pltpu.PrefetchScalarGridSpec(
            num_scalar_prefetch=2,
            grid=(N_EXPERTS,),
            in_specs=[any_spec, wspec(), bspec(), wspec(), bspec(), wspec(), bspec()],
            out_specs=any_spec,
            scratch_shapes=[pltpu.VMEM((D_MODEL, D_FF), BF16)] * 3
            + [pltpu.VMEM((2, tm, D_MODEL), F32), pltpu.VMEM((2, tm, D_MODEL), F32),
               pltpu.SemaphoreType.DMA((2,)), pltpu.SemaphoreType.DMA((2,))]),
        out_shape=jax.ShapeDtypeStruct(xs.shape, F32),
        compiler_params=pltpu.CompilerParams(
            dimension_semantics=("arbitrary",), vmem_limit_bytes=VMEM_LIMIT,
            has_side_effects=True),
        name="experts",
    )(tile0, ntile, xs, w_glu, b_glu.reshape(N_EXPERTS, 1, D_FF),
      w_lin, b_lin.reshape(N_EXPERTS, 1, D_FF), w_down, b_down.reshape(N_EXPERTS, 1, D_MODEL))


def _combine_kernel(dest_ref, gate_ref, ys_ref, h1_ref, g_ref, out_ref, buf, obuf, sems, osems):
    s = pl.program_id(0)
    n = pl.num_programs(0) - 1
    tm = TM_COMBINE
    n_tok = dest_ref.shape[0] // TOP_K

    def issue(slot):
        for p in range(tm * TOP_K):
            r, k = divmod(p, TOP_K)
            _row_copy(ys_ref, dest_ref[s * tm + k * n_tok + r], buf.at[slot, k], r,
                      sems.at[slot]).start(priority=p % 2)

    def out_store(tile, slot):
        return _SlabTileCopy(out_ref.at[pl.ds(tile * tm, tm)], obuf.at[slot], osems.at[slot],
                             to_slab=False)

    for parity in range(2):
        @pl.when((s < n) & (s % 2 == parity))
        def _():
            issue(parity)

    @pl.when(s > 0)
    def _():
        tile = s - 1
        slot = tile % 2

        @pl.when(tile >= 2)
        def _():
            out_store(tile - 2, slot).wait()

        for k in range(TOP_K):
            pltpu.make_async_copy(ys_ref.at[pl.ds(0, tm)], buf.at[slot, k], sems.at[slot]).wait()
        g = g_ref[0]
        for r in range(tm):
            acc = h1_ref[r]
            for k in range(TOP_K):
                acc = acc + gate_ref[tile * tm + k * n_tok + r] * buf[slot, k, r]
            ss = jnp.sum(jnp.sum(acc * acc, axis=1, keepdims=True), axis=0, keepdims=True)
            obuf[slot, r] = acc * lax.rsqrt(ss * (1.0 / D_MODEL) + NORM_EPS) * g
        out_store(tile, slot).start()

    @pl.when(s == n)
    def _():
        @pl.when(n >= 2)
        def _():
            out_store(n - 2, n % 2).wait()
        out_store(n - 1, (n - 1) % 2).wait()


def _combine(dest, ys3, h13, gate, fg):
    T = h13.shape[0]
    tm = TM_COMBINE
    slab = (SLAB_SUBLANES, LANES)
    prev_tile = lambda s, *_: (jnp.maximum(s - 1, 0), 0, 0)
    return pl.pallas_call(
        _combine_kernel,
        grid_spec=pltpu.PrefetchScalarGridSpec(
            num_scalar_prefetch=2,
            grid=(T // tm + 1,),
            in_specs=[pl.BlockSpec(memory_space=pl.ANY),
                      pl.BlockSpec((tm,) + slab, prev_tile),
                      pl.BlockSpec((1,) + slab, lambda s, *_: (0, 0, 0))],
            out_specs=pl.BlockSpec(memory_space=pl.ANY),
            scratch_shapes=[pltpu.VMEM((2, TOP_K, tm) + slab, F32),
                            pltpu.VMEM((2, tm) + slab, F32),
                            pltpu.SemaphoreType.DMA((2,)),
                            pltpu.SemaphoreType.DMA((2,))]),
        out_shape=jax.ShapeDtypeStruct((T, D_MODEL), F32),
        compiler_params=pltpu.CompilerParams(
            dimension_semantics=("arbitrary",), vmem_limit_bytes=VMEM_LIMIT,
            has_side_effects=True),
        name="combine",
    )(dest, gate, ys3, h13, fg.reshape((1,) + slab))


def _swap_halves(w):
    h = w.shape[-1] // 2
    return jnp.concatenate([w[..., h:], w[..., :h]], axis=-1)


def _layer(x2, batch, seq, norm1_g, w_in, w_alpha_up, b_alpha, gla_norm_g, swa_sinks, swa_norm_g,
           w_out, norm2_g, w_router, b_router, w_glu, b_glu, w_lin, b_lin, w_down, b_down):
    T = x2.shape[0]
    kb_w, vb_w = w_in[:, 2064:2192], w_in[:, 2192:2320]
    w_z = w_in[:, 1536:1552]
    w_cat = jnp.concatenate([
        w_in[:, 0:1536], w_in[:, 1552:2064],
        kb_w, _swap_halves(kb_w), vb_w, _swap_halves(vb_w),
        jnp.pad(jnp.tile(w_z, (1, Z_PIECES)), [(0, 0), (0, LANES - Z_PIECES * GLA_RANK)]),
    ], axis=1).astype(BF16)
    wup_hi = w_alpha_up.astype(BF16)
    wup_lo = (w_alpha_up - wup_hi.astype(F32)).astype(BF16)
    wup_cat = jnp.pad(jnp.concatenate([wup_hi, wup_hi, wup_hi, wup_lo, wup_lo], axis=0),
                      [(0, LANES - Z_PIECES * GLA_RANK), (0, 0)])

    qk, vr, la, qb, kv = _in_proj(
        x2, norm1_g.reshape(1, -1), w_cat, wup_cat, b_alpha.reshape(1, -1))
    oa = _gla(qk, vr, la, gla_norm_g.reshape(1, -1), batch, seq)
    ob = _swa(qb, kv, swa_sinks, jnp.tile(swa_norm_g, 2).reshape(1, -1), batch, seq)

    wo = w_out.astype(BF16)
    wr_t = w_router.T
    wr_hi = wr_t.astype(BF16)
    wr_lo = (wr_t - wr_hi.astype(F32)).astype(BF16)
    h1, hn, route, cnt = _out_route(oa, ob, x2, wo, norm2_g.reshape(1, -1),
                                    wr_hi, wr_lo, b_router.reshape(-1, 1))

    tm = TM_EXPERT
    n_tiles = T * TOP_K // tm + N_EXPERTS
    counts = cnt[:, 0].astype(jnp.int32)
    padded = (counts + tm - 1) // tm * tm
    pends = jnp.cumsum(padded)
    pstarts = pends - padded
    top_idx = route[0:TOP_K].astype(jnp.int32)
    gate = route[TOP_K:2 * TOP_K].reshape(-1)
    rank = route[2 * TOP_K:3 * TOP_K].astype(jnp.int32)
    experts = jnp.arange(N_EXPERTS)[:, None, None]
    seg_start = jnp.sum(jnp.where(top_idx[None] == experts, pstarts[:, None, None], 0), axis=0)
    dest = (seg_start + rank).reshape(-1)
    n_used = (pends[-1] // tm).reshape(1)
    xs = _dispatch(dest, pstarts + counts, padded - counts, n_used, hn, n_tiles * tm)
    ys = _experts(pstarts // tm, padded // tm, xs, w_glu, b_glu, w_lin, b_lin, w_down, b_down)
    return dest, ys, h1, gate


def kernel(x, norm1_g, w_in, w_alpha_up, b_alpha, gla_norm_g, swa_sinks, swa_norm_g, w_out,
           norm2_g, w_router, b_router, w_glu, b_glu, w_lin, b_lin, w_down, b_down, final_g):
    batch, seq, d = x.shape
    assert norm1_g.shape[0] == 1, "single-layer problem"
    x2 = x.reshape(batch * seq, d)
    dest, ys, h1, gate = _layer(
        x2, batch, seq, norm1_g[0], w_in[0], w_alpha_up[0], b_alpha[0], gla_norm_g[0],
        swa_sinks[0], swa_norm_g[0], w_out[0], norm2_g[0], w_router[0], b_router[0],
        w_glu[0], b_glu[0], w_lin[0], b_lin[0], w_down[0], b_down[0])
    out = _combine(dest, ys, h1, gate, final_g.reshape(1, -1))
    return out.reshape(batch, seq, d)
```

```python
import numpy as np
import jax
import jax.numpy as jnp
from jax import lax
from jax.experimental import pallas as pl
from jax.experimental.pallas import tpu as pltpu

F32 = jnp.float32
BF16 = jnp.bfloat16

D_MODEL = 1024
GLA_HEADS = 4
GLA_DK = 64
GLA_DV = 128
GLA_RANK = 16
GLA_GATE_TAU = 16.0
GLA_CHUNK = 64
SWA_HEADS = 8
SWA_KV_HEADS = 2
SWA_HEAD_DIM = 64
SWA_WINDOW = 128
N_EXPERTS = 32
TOP_K = 4
D_FF = 1024
SWIGLU_LIMIT = 7.0
SWIGLU_ALPHA = 1.702
NORM_EPS = 1e-5

LANES = 128
SLAB_SUBLANES = D_MODEL // LANES
VMEM_LIMIT = 56 * 1024 * 1024

TM_PROJ = 1024
TL_GLA = 1024
GLA_CUM_ROWS = 256
TM_ROUTE = 1024
TM_EXPERT = 512
TM_COMBINE = 256
SWA_SUB = 8
TM_DISPATCH = 256

NEG_BIG = -1e30
ROUTE_ROWS = 16

C_QA, C_KA, C_VA, C_RA, C_QB, C_KB, C_KBS, C_VB, C_VBS, C_Z, C_END = (
    0, 256, 512, 1024, 1536, 2048, 2176, 2304, 2432, 2560, 2688)
Z_PIECES = 5


def _split3(x):
    hi = x.astype(BF16).astype(F32)
    r = x - hi
    mid = r.astype(BF16).astype(F32)
    lo = (r - mid).astype(BF16).astype(F32)
    return hi, mid, lo


class _SlabTileCopy:
    def __init__(self, flat, slab, sem, to_slab):
        self.copies = []
        for c in range(SLAB_SUBLANES):
            pair = (flat.at[:, pl.ds(c * LANES, LANES)], slab.at[:, c, :])
            src, dst = pair if to_slab else pair[::-1]
            self.copies.append(pltpu.make_async_copy(src, dst, sem))

    def start(self):
        for cp in self.copies:
            cp.start()

    def wait(self):
        for cp in self.copies:
            cp.wait()


def _rms(x, g):
    return x * lax.rsqrt(jnp.mean(x * x, axis=-1, keepdims=True) + NORM_EPS) * g


def _in_proj_kernel(x_ref, g_ref, w_ref, wup_ref, ba_ref,
                    qk_ref, vr_ref, la_ref, qb_ref, kv_ref):
    u = _rms(x_ref[...], g_ref[...]).astype(BF16)

    def proj(c0, c1):
        return jnp.dot(u, w_ref[:, c0:c1], preferred_element_type=F32)

    qk_ref[...] = proj(C_QA, C_VA).astype(BF16)
    vr_ref[:, :C_RA - C_VA] = proj(C_VA, C_RA).astype(BF16)
    vr_ref[:, C_RA - C_VA:] = proj(C_RA, C_QB).astype(BF16)
    qb_ref[...] = proj(C_QB, C_KB).astype(BF16)
    kv_ref[...] = proj(C_KB, C_Z).astype(BF16)
    z = proj(C_Z, C_END)
    hi, mid, lo = _split3(z)
    piece = lax.broadcasted_iota(jnp.int32, z.shape, 1) // GLA_RANK
    zc = jnp.where((piece == 0) | (piece == 3), hi, jnp.where(piece == 2, lo, mid)).astype(BF16)
    y = jnp.dot(zc, wup_ref[...], preferred_element_type=F32) + ba_ref[...]
    log_sig = jnp.minimum(y, 0.0) - jnp.log1p(jnp.exp(-jnp.abs(y)))
    la_ref[...] = log_sig * (1.0 / GLA_GATE_TAU)


def _in_proj(x2, g1, w_cat, wup_p, ba_p):
    T = x2.shape[0]
    tm = TM_PROJ
    row = lambda w: pl.BlockSpec((tm, w), lambda i: (i, 0))
    full = lambda a: pl.BlockSpec(a.shape, lambda i: (0,) * a.ndim)
    outs = [(512, BF16), (1024, BF16), (256, F32), (512, BF16), (512, BF16)]
    return pl.pallas_call(
        _in_proj_kernel,
        grid=(T // tm,),
        in_specs=[row(D_MODEL), full(g1), full(w_cat), full(wup_p), full(ba_p)],
        out_specs=[row(w) for w, _ in outs],
        out_shape=[jax.ShapeDtypeStruct((T, w), dt) for w, dt in outs],
        compiler_params=pltpu.CompilerParams(
            dimension_semantics=("arbitrary",), vmem_limit_bytes=VMEM_LIMIT),
        name="in_proj",
    )(x2, g1, w_cat, wup_p, ba_p)


def _gla_kernel(qk_ref, vr_ref, la_ref, g_ref, cum_ref, o_ref, st_ref):
    @pl.when(pl.program_id(1) == 0)
    def _():
        st_ref[...] = jnp.zeros_like(st_ref)

    tl = TL_GLA
    c = GLA_CHUNK
    kw = GLA_HEADS * GLA_DK
    vw = GLA_HEADS * GLA_DV
    causal = (lax.broadcasted_iota(jnp.int32, (c, c), 0)
              >= lax.broadcasted_iota(jnp.int32, (c, c), 1))
    low_half = lax.broadcasted_iota(jnp.int32, (c, LANES), 1) < GLA_DK
    g = g_ref[...]
    b_groups = []
    for grp in range(tl // GLA_CUM_ROWS):
        la = la_ref[grp * GLA_CUM_ROWS:(grp + 1) * GLA_CUM_ROWS, :]
        pieces = jnp.concatenate([p.astype(BF16) for p in _split3(la)], axis=1)
        b3 = jnp.dot(cum_ref[...], pieces, preferred_element_type=F32)
        b_groups.append(b3[:, :kw] + b3[:, kw:2 * kw] + b3[:, 2 * kw:])
    for ch in range(tl // c):
        rows = slice(ch * c, (ch + 1) * c)
        in_grp = (ch * c) % GLA_CUM_ROWS
        b = b_groups[(ch * c) // GLA_CUM_ROWS][in_grp:in_grp + c]
        b_last = b[c - 1:c]
        qf = qk_ref[rows, :kw].astype(F32)
        kf = qk_ref[rows, kw:].astype(F32)
        q_e = (qf * jnp.exp(b) * (GLA_DK ** -0.5)).astype(BF16)
        k_e = (kf * jnp.exp(-b)).astype(BF16)
        k_t = (kf * jnp.exp(b_last - b)).astype(BF16)
        decay = jnp.exp(b_last)
        for h in range(GLA_HEADS):
            ps = slice((h // 2) * LANES, (h // 2 + 1) * LANES)
            mine = low_half if h % 2 == 0 else ~low_half
            qp, kp = q_e[:, ps], k_e[:, ps]
            qh = jnp.where(mine, qp, jnp.zeros_like(qp))
            kth = jnp.where(mine, k_t[:, ps], jnp.zeros_like(qp))
            vs = slice(h * GLA_DV, (h + 1) * GLA_DV)
            vh = vr_ref[rows, vs]
            a = pl.dot(qh, kp, trans_b=True)
            a = jnp.where(causal, a, 0.0).astype(BF16)
            st = st_ref[h]
            o = (jnp.dot(a, vh, preferred_element_type=F32)
                 + pl.dot(qh, st.astype(BF16), trans_b=True))
            st_ref[h] = st * decay[:, ps] + pl.dot(vh, kth, trans_a=True)
            rh = vr_ref[rows, slice(vw + h * GLA_DV, vw + (h + 1) * GLA_DV)].astype(F32)
            o = _rms(o, g) * (rh * jax.nn.sigmoid(rh))
            o_ref[rows, vs] = o.astype(BF16)


def _gla(qk, vr, la, g, batch, seq):
    tl = TL_GLA
    nl = seq // tl
    row = lambda w: pl.BlockSpec((tl, w), lambda b, i: (b * nl + i, 0))
    r = np.arange(GLA_CUM_ROWS)
    cum = jnp.asarray((r[None, :] <= r[:, None])
                      & (r[None, :] // GLA_CHUNK == r[:, None] // GLA_CHUNK), BF16)
    return pl.pallas_call(
        _gla_kernel,
        grid=(batch, nl),
        in_specs=[row(512), row(1024), row(256),
                  pl.BlockSpec((1, GLA_DV), lambda b, i: (0, 0)),
                  pl.BlockSpec((GLA_CUM_ROWS, GLA_CUM_ROWS), lambda b, i: (0, 0))],
        out_specs=row(512),
        out_shape=jax.ShapeDtypeStruct((batch * seq, 512), BF16),
        scratch_shapes=[pltpu.VMEM((GLA_HEADS, GLA_DV, LANES), F32)],
        compiler_params=pltpu.CompilerParams(
            dimension_semantics=("arbitrary", "arbitrary"), vmem_limit_bytes=VMEM_LIMIT),
        name="gla",
    )(qk, vr, la, g, cum)


def _swa_kernel(sink_ref, bias0_ref, bias_ref, q_ref, kvc_ref, kvp_ref, g_ref, o_ref):
    w = SWA_WINDOW
    scale = jnp.asarray(SWA_HEAD_DIM ** -0.5, BF16)
    kv_lane = lax.broadcasted_iota(jnp.int32, (2 * w, LANES), 1)
    lane_lo = kv_lane < SWA_HEAD_DIM
    out_lo = lax.broadcasted_iota(jnp.int32, (w, LANES), 1) < SWA_HEAD_DIM
    ones_hi = jnp.where(kv_lane == SWA_HEAD_DIM, 1.0, 0.0).astype(BF16)
    ones_lo = jnp.where(kv_lane == 0, 1.0, 0.0).astype(BF16)
    g = g_ref[...]

    for sb in range(SWA_SUB):
        rows = slice(sb * w, (sb + 1) * w)
        if sb == 0:
            kv = jnp.concatenate([kvp_ref[...], kvc_ref[0:w, :]], axis=0)
            bias = bias0_ref
        else:
            kv = kvc_ref[(sb - 1) * w:(sb + 1) * w, :]
            bias = bias_ref
        k = kv[:, 0:LANES] * scale
        ks = kv[:, LANES:2 * LANES] * scale
        v, vs = kv[:, 2 * LANES:3 * LANES], kv[:, 3 * LANES:]
        zero = jnp.zeros_like(k)
        k_low = [jnp.where(lane_lo, k, zero), jnp.where(lane_lo, ks, zero)]
        k_high = [jnp.where(lane_lo, zero, ks), jnp.where(lane_lo, zero, k)]
        v_low = [jnp.where(lane_lo, v, zero) + ones_hi, jnp.where(lane_lo, vs, zero) + ones_hi]
        v_high = [jnp.where(lane_lo, zero, vs) + ones_lo, jnp.where(lane_lo, zero, v) + ones_lo]

        def weights(s, head):
            s = s + bias[0, head]
            sink = sink_ref[head]
            m = jnp.maximum(jnp.max(s, axis=-1, keepdims=True), sink)
            return jnp.exp(s - m).astype(BF16), jnp.exp(sink - m)

        for pair in range(SWA_HEADS // 2):
            j = (2 * pair) // (SWA_HEADS // SWA_KV_HEADS)
            cols = slice(pair * LANES, (pair + 1) * LANES)
            qp = q_ref[rows, cols]
            e0, sink0 = weights(pl.dot(qp, k_low[j], trans_b=True), 2 * pair)
            e1, sink1 = weights(pl.dot(qp, k_high[j], trans_b=True), 2 * pair + 1)
            oa = jnp.dot(e0, v_low[j], preferred_element_type=F32)
            ob = jnp.dot(e1, v_high[j], preferred_element_type=F32)
            inv0 = 1.0 / (oa[:, SWA_HEAD_DIM:SWA_HEAD_DIM + 1] + sink0)
            inv1 = 1.0 / (ob[:, 0:1] + sink1)
            o = jnp.where(out_lo, oa * inv0, ob * inv1)
            sq = o * o
            ms_lo = jnp.sum(jnp.where(out_lo, sq, 0.0), axis=-1, keepdims=True)
            ms_hi = jnp.sum(jnp.where(out_lo, 0.0, sq), axis=-1, keepdims=True)
            ms = jnp.where(out_lo, ms_lo, ms_hi) * (1.0 / SWA_HEAD_DIM)
            o = o * lax.rsqrt(ms + NORM_EPS) * g
            o_ref[rows, cols] = o.astype(BF16)


def _swa_bias():
    w = SWA_WINDOW
    slopes = 2.0 ** (-8.0 * np.arange(1, SWA_HEADS + 1, dtype=np.float64) / SWA_HEADS)
    rel = np.arange(w)[:, None] + w - np.arange(2 * w)[None, :]
    in_window = (rel >= 0) & (rel < w)
    exists = np.stack([np.arange(2 * w) >= w, np.ones(2 * w, bool)])
    valid = in_window[None] & exists[:, None, :]
    bias = -slopes[None, :, None, None] * rel[None, None].astype(np.float64)
    return np.where(valid[:, None], bias, NEG_BIG).astype(np.float32)


def _swa(qb, kv, sinks, g2, batch, seq):
    w = SWA_WINDOW
    rows = SWA_SUB * w
    nb = seq // rows
    cur = lambda: pl.BlockSpec((rows, 512), lambda b, n: (b * nb + n, 0))
    prev = pl.BlockSpec(
        (w, 512), lambda b, n: (jnp.maximum((b * nb + n) * SWA_SUB - 1, b * nb * SWA_SUB), 0))
    table = lambda index: pl.BlockSpec((1, SWA_HEADS, w, 2 * w), index)
    bias = jnp.asarray(_swa_bias())
    return pl.pallas_call(
        _swa_kernel,
        grid=(batch, nb),
        in_specs=[pl.BlockSpec(memory_space=pltpu.SMEM),
                  table(lambda b, n: (jnp.minimum(n, 1), 0, 0, 0)),
                  table(lambda b, n: (1, 0, 0, 0)),
                  cur(), cur(), prev,
                  pl.BlockSpec((1, LANES), lambda b, n: (0, 0))],
        out_specs=cur(),
        out_shape=jax.ShapeDtypeStruct((batch * seq, 512), BF16),
        compiler_params=pltpu.CompilerParams(
            dimension_semantics=("arbitrary", "arbitrary"), vmem_limit_bytes=VMEM_LIMIT),
        name="swa",
    )(sinks, bias, bias, qb, kv, kv, g2)


def _out_route_kernel(oa_ref, ob_ref, x_ref, wo_ref, g_ref, wrh_ref, wrl_ref, br_ref,
                      earlier_ref, h13_ref, hn3_ref, route_ref, cnt_ref, hbuf, hsem):
    i = pl.program_id(0)
    tm = TM_ROUTE

    @pl.when(i == 0)
    def _():
        cnt_ref[...] = jnp.zeros_like(cnt_ref)

    def stores(step, slot):
        rows = pl.ds(step * tm, tm)
        return [_SlabTileCopy(hbuf.at[slot, which], out.at[rows], hsem.at[slot], to_slab=True)
                for which, out in enumerate((h13_ref, hn3_ref))]

    slot = i % 2

    @pl.when(i >= 2)
    def _():
        for st in stores(i - 2, slot):
            st.wait()

    h1 = (x_ref[...]
          + jnp.dot(oa_ref[...], wo_ref[:oa_ref.shape[1], :], preferred_element_type=F32)
          + jnp.dot(ob_ref[...], wo_ref[oa_ref.shape[1]:, :], preferred_element_type=F32))
    hn = _rms(h1, g_ref[...])
    hbuf[slot, 0] = h1
    hbuf[slot, 1] = hn
    for st in stores(i, slot):
        st.start()

    @pl.when(i == pl.num_programs(0) - 1)
    def _():
        @pl.when(i >= 1)
        def _():
            for st in stores(i - 1, 1 - slot):
                st.wait()
        for st in stores(i, slot):
            st.wait()

    hn_hi = hn.astype(BF16)
    hn_lo = (hn - hn_hi.astype(F32)).astype(BF16)
    logits = (pl.dot(wrh_ref[...], hn_hi, trans_b=True)
              + pl.dot(wrh_ref[...], hn_lo, trans_b=True)
              + pl.dot(wrl_ref[...], hn_hi, trans_b=True)) + br_ref[...]

    eid = lax.broadcasted_iota(jnp.int32, (N_EXPERTS, tm), 0)
    work = logits
    vals, idxs, sels = [], [], []
    for _ in range(TOP_K):
        m = jnp.max(work, axis=0, keepdims=True)
        idx = jnp.min(jnp.where(work == m, eid, N_EXPERTS), axis=0, keepdims=True)
        sel = eid == idx
        vals.append(m)
        idxs.append(idx)
        sels.append(sel)
        work = jnp.where(sel, -3e38, work)
    exps = [jnp.exp(v - vals[0]) for v in vals]
    inv_den = 1.0 / (exps[0] + exps[1] + exps[2] + exps[3])

    multihot = jnp.where(sels[0] | sels[1] | sels[2] | sels[3], 1.0, 0.0)
    before = (jnp.dot(multihot.astype(BF16), earlier_ref[...], preferred_element_type=F32)
              + cnt_ref[:, 0:1])
    ranks = [jnp.sum(jnp.where(s, before, 0.0), axis=0, keepdims=True) for s in sels]
    route_ref[...] = jnp.concatenate(
        [i.astype(F32) for i in idxs] + [e * inv_den for e in exps] + ranks
        + [jnp.zeros((ROUTE_ROWS - 3 * TOP_K, tm), F32)], axis=0)
    cnt_ref[...] += jnp.sum(multihot, axis=1, keepdims=True)


def _out_route(oa, ob, x2, wo, g2, wr_hi, wr_lo, br):
    T = x2.shape[0]
    tm = TM_ROUTE
    row = lambda w: pl.BlockSpec((tm, w), lambda i: (i, 0))
    full = lambda a: pl.BlockSpec(a.shape, lambda i: (0,) * a.ndim)
    earlier = jnp.asarray(np.triu(np.ones((tm, tm), np.float32), k=1), BF16)
    return pl.pallas_call(
        _out_route_kernel,
        grid=(T // tm,),
        in_specs=[row(512), row(512), row(D_MODEL), full(wo), full(g2),
                  full(wr_hi), full(wr_lo), full(br), full(earlier)],
        out_specs=[pl.BlockSpec(memory_space=pl.ANY), pl.BlockSpec(memory_space=pl.ANY),
                   pl.BlockSpec((ROUTE_ROWS, tm), lambda i: (0, i)),
                   pl.BlockSpec((N_EXPERTS, LANES), lambda i: (0, 0))],
        out_shape=[jax.ShapeDtypeStruct((T, SLAB_SUBLANES, LANES), F32),
                   jax.ShapeDtypeStruct((T, SLAB_SUBLANES, LANES), F32),
                   jax.ShapeDtypeStruct((ROUTE_ROWS, T), F32),
                   jax.ShapeDtypeStruct((N_EXPERTS, LANES), F32)],
        scratch_shapes=[pltpu.VMEM((2, 2, tm, D_MODEL), F32), pltpu.SemaphoreType.DMA((2,))],
        compiler_params=pltpu.CompilerParams(
            dimension_semantics=("arbitrary",), vmem_limit_bytes=VMEM_LIMIT,
            has_side_effects=True),
        name="out_route",
    )(oa, ob, x2, wo, g2, wr_hi, wr_lo, br, earlier)


def _row_copy(src, s, dst, d, sem):
    return pltpu.make_async_copy(src.at[pl.ds(s, 1)], dst.at[pl.ds(d, 1)], sem)


def _dispatch_kernel(dest_ref, pad_start_ref, pad_cnt_ref, nused_ref, hn_ref, xs_ref,
                     zero_ref, sem, zsem):
    i = pl.program_id(0)
    n = pl.num_programs(0)
    tm = TM_EXPERT
    n_tok = dest_ref.shape[0] // TOP_K
    base = i * TM_DISPATCH
    for p in range(TM_DISPATCH * TOP_K):
        r, k = divmod(p, TOP_K)
        _row_copy(hn_ref, r, xs_ref, dest_ref[base + k * n_tok + r], sem).start(priority=p % 2)
    for _ in range(TOP_K):
        pltpu.make_async_copy(hn_ref, xs_ref.at[pl.ds(0, TM_DISPATCH)], sem).wait()

    @pl.when(i == n - 1)
    def _():
        zero_ref[...] = jnp.zeros_like(zero_ref)
        sub = 8
        for e in range(N_EXPERTS):
            start, cnt = pad_start_ref[e], pad_cnt_ref[e]
            head = cnt & (sub - 1)
            body0 = start + head
            nbody = cnt // sub

            def head_copy(r):
                return _row_copy(zero_ref, 0, xs_ref, start + r, zsem)

            def body_copy(c):
                return pltpu.make_async_copy(zero_ref.at[pl.ds(0, sub)],
                                             xs_ref.at[pl.ds(body0 + c * sub, sub)], zsem)

            for copy, count in ((head_copy, head), (body_copy, nbody)):
                def zissue(r, carry, copy=copy):
                    copy(r).start()
                    return carry

                def zwait(r, carry, copy=copy):
                    copy(r).wait()
                    return carry

                lax.fori_loop(0, count, zissue, 0)
                lax.fori_loop(0, count, zwait, 0)

        def tail_copy(t):
            return pltpu.make_async_copy(zero_ref, xs_ref.at[pl.ds(t * tm, tm)], zsem)

        def tissue(t, carry):
            tail_copy(t).start()
            return carry

        def twait(t, carry):
            tail_copy(t).wait()
            return carry

        n_tiles = xs_ref.shape[0] // tm
        lax.fori_loop(nused_ref[0], n_tiles, tissue, 0)
        lax.fori_loop(nused_ref[0], n_tiles, twait, 0)


def _dispatch(dest, pad_start, pad_cnt, n_used, hn, n_rows):
    T = hn.shape[0]
    any_spec = pl.BlockSpec(memory_space=pl.ANY)
    return pl.pallas_call(
        _dispatch_kernel,
        grid_spec=pltpu.PrefetchScalarGridSpec(
            num_scalar_prefetch=4,
            grid=(T // TM_DISPATCH,),
            in_specs=[pl.BlockSpec((TM_DISPATCH, SLAB_SUBLANES, LANES), lambda i, *_: (i, 0, 0))],
            out_specs=any_spec,
            scratch_shapes=[pltpu.VMEM((TM_EXPERT, SLAB_SUBLANES, LANES), F32),
                            pltpu.SemaphoreType.DMA(()),
                            pltpu.SemaphoreType.DMA(())]),
        out_shape=jax.ShapeDtypeStruct((n_rows, SLAB_SUBLANES, LANES), F32),
        compiler_params=pltpu.CompilerParams(
            dimension_semantics=("arbitrary",), has_side_effects=True,
            vmem_limit_bytes=VMEM_LIMIT),
        name="dispatch",
    )(dest, pad_start, pad_cnt, n_used, hn)


def _experts_kernel(tile0_ref, ntile_ref, xs_ref, wg_ref, bg_ref, wl_ref, bl_ref, wd_ref, bd_ref,
                    ys_ref, wg_bf, wl_bf, wd_bf, xbuf, ybuf, xsem, ysem):
    e = pl.program_id(0)
    tm = TM_EXPERT
    t0, nt = tile0_ref[e], ntile_ref[e]

    def x_load(tile, slot):
        return _SlabTileCopy(xbuf.at[slot], xs_ref.at[pl.ds(tile * tm, tm)], xsem.at[slot],
                             to_slab=False)

    def y_store(tile, slot):
        return _SlabTileCopy(ybuf.at[slot], ys_ref.at[pl.ds(tile * tm, tm)], ysem.at[slot],
                             to_slab=True)

    @pl.when((e == 0) & (nt > 0))
    def _():
        x_load(t0, 0).start()

    @pl.when(nt > 0)
    def _():
        wg_bf[...] = wg_ref[0].astype(BF16)
        wl_bf[...] = wl_ref[0].astype(BF16)
        wd_bf[...] = wd_ref[0].astype(BF16)

    def tile_body(j, carry):
        slot = j % 2

        @pl.when(j + 1 < nt)
        def _():
            x_load(t0 + j + 1, 1 - slot).start()

        x_load(t0 + j, slot).wait()

        @pl.when(j >= 2)
        def _():
            y_store(t0 + j - 2, slot).wait()

        x = xbuf[slot].astype(BF16)
        glu = jnp.minimum(jnp.dot(x, wg_bf[...], preferred_element_type=F32) + bg_ref[0],
                          SWIGLU_LIMIT)
        lin = jnp.clip(jnp.dot(x, wl_bf[...], preferred_element_type=F32) + bl_ref[0],
                       -SWIGLU_LIMIT, SWIGLU_LIMIT)
        hid = glu * jax.nn.sigmoid(SWIGLU_ALPHA * glu) * (lin + 1.0)
        ybuf[slot] = jnp.dot(hid.astype(BF16), wd_bf[...], preferred_element_type=F32) + bd_ref[0]
        y_store(t0 + j, slot).start()
        return carry

    lax.fori_loop(0, nt, tile_body, 0)

    @pl.when(e + 1 < pl.num_programs(0))
    def _():
        @pl.when(ntile_ref[e + 1] > 0)
        def _():
            x_load(tile0_ref[e + 1], 0).start()

    @pl.when(nt >= 2)
    def _():
        y_store(t0 + nt - 2, nt % 2).wait()

    @pl.when(nt >= 1)
    def _():
        y_store(t0 + nt - 1, (nt - 1) % 2).wait()

    @pl.when(e == pl.num_programs(0) - 1)
    def _():
        ybuf[0] = jnp.zeros((tm, D_MODEL), F32)
        n_tiles = ys_ref.shape[0] // tm

        def tail_issue(t, carry):
            y_store(t, 0).start()
            return carry

        def tail_wait(t, carry):
            y_store(t, 0).wait()
            return carry

        lax.fori_loop(t0 + nt, n_tiles, tail_issue, 0)
        lax.fori_loop(t0 + nt, n_tiles, tail_wait, 0)


def _experts(tile0, ntile, xs, w_glu, b_glu, w_lin, b_lin, w_down, b_down):
    tm = TM_EXPERT
    any_spec = pl.BlockSpec(memory_space=pl.ANY)
    wspec = lambda: pl.BlockSpec((1, D_MODEL, D_FF), lambda e, *_: (e, 0, 0))
    bspec = lambda: pl.BlockSpec((1, 1, D_FF), lambda e, *_: (e, 0, 0))
    return pl.pallas_call(
        _experts_kernel,
        grid_spec=pltpu.PrefetchScalarGridSpec(
            num_scalar_prefetch=2,
            grid=(N_EXPERTS,),
            in_specs=[any_spec, wspec(), bspec(), wspec(), bspec(), wspec(), bspec()],
            out_specs=any_spec,
            scratch_shapes=[pltpu.VMEM((D_MODEL, D_FF), BF16)] * 3
            + [pltpu.VMEM((2, tm, D_MODEL), F32), pltpu.VMEM((2, tm, D_MODEL), F32),
               pltpu.SemaphoreType.DMA((2,)), pltpu.SemaphoreType.DMA((2,))]),
        out_shape=jax.ShapeDtypeStruct(xs.shape, F32),
        compiler_params=pltpu.CompilerParams(
            dimension_semantics=("arbitrary",), vmem_limit_bytes=VMEM_LIMIT,
            has_side_effects=True),
        name="experts",
    )(tile0, ntile, xs, w_glu, b_glu.reshape(N_EXPERTS, 1, D_FF),
      w_lin, b_lin.reshape(N_EXPERTS, 1, D_FF), w_down, b_down.reshape(N_EXPERTS, 1, D_MODEL))


def _combine_kernel(dest_ref, gate_ref, ys_ref, h1_ref, g_ref, out_ref, buf, obuf, sems, osems):
    s = pl.program_id(0)
    tm = TM_COMBINE
    n = out_ref.shape[0] // tm
    n_tok = dest_ref.shape[0] // TOP_K

    def out_store(tile, slot):
        return _SlabTileCopy(out_ref.at[pl.ds(tile * tm, tm)], obuf.at[slot], osems.at[slot],
                             to_slab=False)

    def issue_row(slot, r):
        for k in range(TOP_K):
            _row_copy(ys_ref, dest_ref[s * tm + k * n_tok + r], buf.at[slot, k], r,
                      sems.at[slot]).start(priority=k % 2)

    def reduce_row(slot, r):
        acc = h1_ref[r]
        for k in range(TOP_K):
            acc = acc + gate_ref[(s - 1) * tm + k * n_tok + r] * buf[slot, k, r]
        ss = jnp.sum(jnp.sum(acc * acc, axis=1, keepdims=True), axis=0, keepdims=True)
        obuf[slot, r] = acc * lax.rsqrt(ss * (1.0 / D_MODEL) + NORM_EPS) * g_ref[0]

    def step_body(gather_slot, reduce_slot):
        for r in range(tm):
            if gather_slot is not None:
                issue_row(gather_slot, r)
            if reduce_slot is not None:
                reduce_row(reduce_slot, r)
        if reduce_slot is not None:
            out_store(s - 1, reduce_slot).start()

    for parity in range(2):
        other = 1 - parity

        @pl.when(s % 2 == parity)
        def _():
            @pl.when(s >= 3)
            def _():
                out_store(s - 3, other).wait()

            @pl.when(s > 0)
            def _():
                for k in range(TOP_K):
                    pltpu.make_async_copy(ys_ref.at[pl.ds(0, tm)], buf.at[other, k],
                                          sems.at[other]).wait()

            @pl.when((s > 0) & (s < n))
            def _():
                step_body(parity, other)

            if parity == 0:
                @pl.when(s == 0)
                def _():
                    step_body(parity, None)

            if n % 2 == parity:
                @pl.when(s == n)
                def _():
                    step_body(None, other)
                    if n >= 2:
                        out_store(n - 2, parity).wait()
                    out_store(n - 1, other).wait()


def _combine(dest, ys3, h13, gate, fg):
    T = h13.shape[0]
    tm = TM_COMBINE
    slab = (SLAB_SUBLANES, LANES)
    prev_tile = lambda s, *_: (jnp.maximum(s - 1, 0), 0, 0)
    return pl.pallas_call(
        _combine_kernel,
        grid_spec=pltpu.PrefetchScalarGridSpec(
            num_scalar_prefetch=2,
            grid=(T // tm + 1,),
            in_specs=[pl.BlockSpec(memory_space=pl.ANY),
                      pl.BlockSpec((tm,) + slab, prev_tile),
                      pl.BlockSpec((1,) + slab, lambda s, *_: (0, 0, 0))],
            out_specs=pl.BlockSpec(memory_space=pl.ANY),
            scratch_shapes=[pltpu.VMEM((2, TOP_K, tm) + slab, F32),
                            pltpu.VMEM((2, tm) + slab, F32),
                            pltpu.SemaphoreType.DMA((2,)),
                            pltpu.SemaphoreType.DMA((2,))]),
        out_shape=jax.ShapeDtypeStruct((T, D_MODEL), F32),
        compiler_params=pltpu.CompilerParams(
            dimension_semantics=("arbitrary",), vmem_limit_bytes=VMEM_LIMIT,
            has_side_effects=True),
        name="combine",
    )(dest, gate, ys3, h13, fg.reshape((1,) + slab))


def _swap_halves(w):
    h = w.shape[-1] // 2
    return jnp.concatenate([w[..., h:], w[..., :h]], axis=-1)


def _layer(x2, batch, seq, norm1_g, w_in, w_alpha_up, b_alpha, gla_norm_g, swa_sinks, swa_norm_g,
           w_out, norm2_g, w_router, b_router, w_glu, b_glu, w_lin, b_lin, w_down, b_down):
    T = x2.shape[0]
    kb_w, vb_w = w_in[:, 2064:2192], w_in[:, 2192:2320]
    w_z = w_in[:, 1536:1552]
    w_cat = jnp.concatenate([
        w_in[:, 0:1536], w_in[:, 1552:2064],
        kb_w, _swap_halves(kb_w), vb_w, _swap_halves(vb_w),
        jnp.pad(jnp.tile(w_z, (1, Z_PIECES)), [(0, 0), (0, LANES - Z_PIECES * GLA_RANK)]),
    ], axis=1).astype(BF16)
    wup_hi = w_alpha_up.astype(BF16)
    wup_lo = (w_alpha_up - wup_hi.astype(F32)).astype(BF16)
    wup_cat = jnp.pad(jnp.concatenate([wup_hi, wup_hi, wup_hi, wup_lo, wup_lo], axis=0),
                      [(0, LANES - Z_PIECES * GLA_RANK), (0, 0)])

    qk, vr, la, qb, kv = _in_proj(
        x2, norm1_g.reshape(1, -1), w_cat, wup_cat, b_alpha.reshape(1, -1))
    oa = _gla(qk, vr, la, gla_norm_g.reshape(1, -1), batch, seq)
    ob = _swa(qb, kv, swa_sinks, jnp.tile(swa_norm_g, 2).reshape(1, -1), batch, seq)

    wo = w_out.astype(BF16)
    wr_t = w_router.T
    wr_hi = wr_t.astype(BF16)
    wr_lo = (wr_t - wr_hi.astype(F32)).astype(BF16)
    h1, hn, route, cnt = _out_route(oa, ob, x2, wo, norm2_g.reshape(1, -1),
                                    wr_hi, wr_lo, b_router.reshape(-1, 1))

    tm = TM_EXPERT
    n_tiles = T * TOP_K // tm + N_EXPERTS
    counts = cnt[:, 0].astype(jnp.int32)
    padded = (counts + tm - 1) // tm * tm
    pends = jnp.cumsum(padded)
    pstarts = pends - padded
    top_idx = route[0:TOP_K].astype(jnp.int32)
    gate = route[TOP_K:2 * TOP_K].reshape(-1)
    rank = route[2 * TOP_K:3 * TOP_K].astype(jnp.int32)
    experts = jnp.arange(N_EXPERTS)[:, None, None]
    seg_start = jnp.sum(jnp.where(top_idx[None] == experts, pstarts[:, None, None], 0), axis=0)
    dest = (seg_start + rank).reshape(-1)
    n_used = (pends[-1] // tm).reshape(1)
    xs = _dispatch(dest, pstarts + counts, padded - counts, n_used, hn, n_tiles * tm)
    ys = _experts(pstarts // tm, padded // tm, xs, w_glu, b_glu, w_lin, b_lin, w_down, b_down)
    return dest, ys, h1, gate


def kernel(x, norm1_g, w_in, w_alpha_up, b_alpha, gla_norm_g, swa_sinks, swa_norm_g, w_out,
           norm2_g, w_router, b_router, w_glu, b_glu, w_lin, b_lin, w_down, b_down, final_g):
    batch, seq, d = x.shape
    assert norm1_g.shape[0] == 1, "single-layer problem"
    x2 = x.reshape(batch * seq, d)
    dest, ys, h1, gate = _layer(
        x2, batch, seq, norm1_g[0], w_in[0], w_alpha_up[0], b_alpha[0], gla_norm_g[0],
        swa_sinks[0], swa_norm_g[0], w_out[0], norm2_g[0], w_router[0], b_router[0],
        w_glu[0], b_glu[0], w_lin[0], b_lin[0], w_down[0], b_down[0])
    out = _combine(dest, ys, h1, gate, final_g.reshape(1, -1))
    return out.reshape(batch, seq, d)
```

```python
import numpy as np
import jax
import jax.numpy as jnp
from jax import lax
from jax.experimental import pallas as pl
from jax.experimental.pallas import tpu as pltpu

F32 = jnp.float32
BF16 = jnp.bfloat16

D_MODEL = 1024
GLA_HEADS = 4
GLA_DK = 64
GLA_DV = 128
GLA_RANK = 16
GLA_GATE_TAU = 16.0
GLA_CHUNK = 64
SWA_HEADS = 8
SWA_KV_HEADS = 2
SWA_HEAD_DIM = 64
SWA_WINDOW = 128
N_EXPERTS = 32
TOP_K = 4
D_FF = 1024
SWIGLU_LIMIT = 7.0
SWIGLU_ALPHA = 1.702
NORM_EPS = 1e-5

LANES = 128
SLAB_SUBLANES = D_MODEL // LANES
VMEM_LIMIT = 56 * 1024 * 1024

TM_PROJ = 1024
TL_GLA = 1024
GLA_CUM_ROWS = 256
TM_ROUTE = 1024
TM_EXPERT = 512
TM_COMBINE = 256
SWA_SUB = 8
TM_DISPATCH = 256

NEG_BIG = -1e30
ROUTE_ROWS = 16

C_QA, C_KA, C_VA, C_RA, C_QB, C_KB, C_KBS, C_VB, C_VBS, C_Z, C_END = (
    0, 256, 512, 1024, 1536, 2048, 2176, 2304, 2432, 2560, 2688)
Z_PIECES = 5


def _split3(x):
    hi = x.astype(BF16).astype(F32)
    r = x - hi
    mid = r.astype(BF16).astype(F32)
    lo = (r - mid).astype(BF16).astype(F32)
    return hi, mid, lo


class _SlabTileCopy:
    def __init__(self, flat, slab, sem, to_slab):
        self.copies = []
        for c in range(SLAB_SUBLANES):
            pair = (flat.at[:, pl.ds(c * LANES, LANES)], slab.at[:, c, :])
            src, dst = pair if to_slab else pair[::-1]
            self.copies.append(pltpu.make_async_copy(src, dst, sem))

    def start(self):
        for cp in self.copies:
            cp.start()

    def wait(self):
        for cp in self.copies:
            cp.wait()


def _rms(x, g):
    return x * lax.rsqrt(jnp.mean(x * x, axis=-1, keepdims=True) + NORM_EPS) * g


def _in_proj_kernel(x_ref, g_ref, w_ref, wup_ref, ba_ref,
                    qk_ref, vr_ref, la_ref, qb_ref, kv_ref):
    u = _rms(x_ref[...], g_ref[...]).astype(BF16)

    def proj(c0, c1):
        return jnp.dot(u, w_ref[:, c0:c1], preferred_element_type=F32)

    qk_ref[...] = proj(C_QA, C_VA).astype(BF16)
    vr_ref[:, :C_RA - C_VA] = proj(C_VA, C_RA).astype(BF16)
    vr_ref[:, C_RA - C_VA:] = proj(C_RA, C_QB).astype(BF16)
    qb_ref[...] = proj(C_QB, C_KB).astype(BF16)
    kv_ref[...] = proj(C_KB, C_Z).astype(BF16)
    z = proj(C_Z, C_END)
    hi, mid, lo = _split3(z)
    piece = lax.broadcasted_iota(jnp.int32, z.shape, 1) // GLA_RANK
    zc = jnp.where((piece == 0) | (piece == 3), hi, jnp.where(piece == 2, lo, mid)).astype(BF16)
    y = jnp.dot(zc, wup_ref[...], preferred_element_type=F32) + ba_ref[...]
    log_sig = jnp.minimum(y, 0.0) - jnp.log1p(jnp.exp(-jnp.abs(y)))
    la_ref[...] = log_sig * (1.0 / GLA_GATE_TAU)


def _in_proj(x2, g1, w_cat, wup_p, ba_p):
    T = x2.shape[0]
    tm = TM_PROJ
    row = lambda w: pl.BlockSpec((tm, w), lambda i: (i, 0))
    full = lambda a: pl.BlockSpec(a.shape, lambda i: (0,) * a.ndim)
    outs = [(512, BF16), (1024, BF16), (256, F32), (512, BF16), (512, BF16)]
    return pl.pallas_call(
        _in_proj_kernel,
        grid=(T // tm,),
        in_specs=[row(D_MODEL), full(g1), full(w_cat), full(wup_p), full(ba_p)],
        out_specs=[row(w) for w, _ in outs],
        out_shape=[jax.ShapeDtypeStruct((T, w), dt) for w, dt in outs],
        compiler_params=pltpu.CompilerParams(
            dimension_semantics=("arbitrary",), vmem_limit_bytes=VMEM_LIMIT),
        name="in_proj",
    )(x2, g1, w_cat, wup_p, ba_p)


def _gla_kernel(qk_ref, vr_ref, la_ref, g_ref, cum_ref, o_ref, st_ref):
    @pl.when(pl.program_id(1) == 0)
    def _():
        st_ref[...] = jnp.zeros_like(st_ref)

    tl = TL_GLA
    c = GLA_CHUNK
    kw = GLA_HEADS * GLA_DK
    vw = GLA_HEADS * GLA_DV
    causal = (lax.broadcasted_iota(jnp.int32, (c, c), 0)
              >= lax.broadcasted_iota(jnp.int32, (c, c), 1))
    low_half = lax.broadcasted_iota(jnp.int32, (c, LANES), 1) < GLA_DK
    g = g_ref[...]
    b_groups = []
    for grp in range(tl // GLA_CUM_ROWS):
        la = la_ref[grp * GLA_CUM_ROWS:(grp + 1) * GLA_CUM_ROWS, :]
        pieces = jnp.concatenate([p.astype(BF16) for p in _split3(la)], axis=1)
        b3 = jnp.dot(cum_ref[...], pieces, preferred_element_type=F32)
        b_groups.append(b3[:, :kw] + b3[:, kw:2 * kw] + b3[:, 2 * kw:])
    for ch in range(tl // c):
        rows = slice(ch * c, (ch + 1) * c)
        in_grp = (ch * c) % GLA_CUM_ROWS
        b = b_groups[(ch * c) // GLA_CUM_ROWS][in_grp:in_grp + c]
        b_last = b[c - 1:c]
        qf = qk_ref[rows, :kw].astype(F32)
        kf = qk_ref[rows, kw:].astype(F32)
        q_e = (qf * jnp.exp(b) * (GLA_DK ** -0.5)).astype(BF16)
        k_e = (kf * jnp.exp(-b)).astype(BF16)
        k_t = (kf * jnp.exp(b_last - b)).astype(BF16)
        decay = jnp.exp(b_last)
        for h in range(GLA_HEADS):
            ps = slice((h // 2) * LANES, (h // 2 + 1) * LANES)
            mine = low_half if h % 2 == 0 else ~low_half
            qp, kp = q_e[:, ps], k_e[:, ps]
            qh = jnp.where(mine, qp, jnp.zeros_like(qp))
            kth = jnp.where(mine, k_t[:, ps], jnp.zeros_like(qp))
            vs = slice(h * GLA_DV, (h + 1) * GLA_DV)
            vh = vr_ref[rows, vs]
            a = pl.dot(qh, kp, trans_b=True)
            a = jnp.where(causal, a, 0.0).astype(BF16)
            st = st_ref[h]
            o = (jnp.dot(a, vh, preferred_element_type=F32)
                 + pl.dot(qh, st.astype(BF16), trans_b=True))
            st_ref[h] = st * decay[:, ps] + pl.dot(vh, kth, trans_a=True)
            rh = vr_ref[rows, slice(vw + h * GLA_DV, vw + (h + 1) * GLA_DV)].astype(F32)
            o = _rms(o, g) * (rh * jax.nn.sigmoid(rh))
            o_ref[rows, vs] = o.astype(BF16)


def _gla(qk, vr, la, g, batch, seq):
    tl = TL_GLA
    nl = seq // tl
    row = lambda w: pl.BlockSpec((tl, w), lambda b, i: (b * nl + i, 0))
    r = np.arange(GLA_CUM_ROWS)
    cum = jnp.asarray((r[None, :] <= r[:, None])
                      & (r[None, :] // GLA_CHUNK == r[:, None] // GLA_CHUNK), BF16)
    return pl.pallas_call(
        _gla_kernel,
        grid=(batch, nl),
        in_specs=[row(512), row(1024), row(256),
                  pl.BlockSpec((1, GLA_DV), lambda b, i: (0, 0)),
                  pl.BlockSpec((GLA_CUM_ROWS, GLA_CUM_ROWS), lambda b, i: (0, 0))],
        out_specs=row(512),
        out_shape=jax.ShapeDtypeStruct((batch * seq, 512), BF16),
        scratch_shapes=[pltpu.VMEM((GLA_HEADS, GLA_DV, LANES), F32)],
        compiler_params=pltpu.CompilerParams(
            dimension_semantics=("arbitrary", "arbitrary"), vmem_limit_bytes=VMEM_LIMIT),
        name="gla",
    )(qk, vr, la, g, cum)


def _swa_kernel(sink_ref, bias0_ref, bias_ref, q_ref, kvc_ref, kvp_ref, g_ref, o_ref):
    w = SWA_WINDOW
    scale = jnp.asarray(SWA_HEAD_DIM ** -0.5, BF16)
    kv_lane = lax.broadcasted_iota(jnp.int32, (2 * w, LANES), 1)
    lane_lo = kv_lane < SWA_HEAD_DIM
    out_lo = lax.broadcasted_iota(jnp.int32, (w, LANES), 1) < SWA_HEAD_DIM
    ones_hi = jnp.where(kv_lane == SWA_HEAD_DIM, 1.0, 0.0).astype(BF16)
    ones_lo = jnp.where(kv_lane == 0, 1.0, 0.0).astype(BF16)
    g = g_ref[...]

    for sb in range(SWA_SUB):
        rows = slice(sb * w, (sb + 1) * w)
        if sb == 0:
            kv = jnp.concatenate([kvp_ref[...], kvc_ref[0:w, :]], axis=0)
            bias = bias0_ref
        else:
            kv = kvc_ref[(sb - 1) * w:(sb + 1) * w, :]
            bias = bias_ref
        k = kv[:, 0:LANES] * scale
        ks = kv[:, LANES:2 * LANES] * scale
        v, vs = kv[:, 2 * LANES:3 * LANES], kv[:, 3 * LANES:]
        zero = jnp.zeros_like(k)
        k_low = [jnp.where(lane_lo, k, zero), jnp.where(lane_lo, ks, zero)]
        k_high = [jnp.where(lane_lo, zero, ks), jnp.where(lane_lo, zero, k)]
        v_low = [jnp.where(lane_lo, v, zero) + ones_hi, jnp.where(lane_lo, vs, zero) + ones_hi]
        v_high = [jnp.where(lane_lo, zero, vs) + ones_lo, jnp.where(lane_lo, zero, v) + ones_lo]

        def weights(s, head):
            s = s + bias[0, head]
            sink = sink_ref[head]
            m = jnp.maximum(jnp.max(s, axis=-1, keepdims=True), sink)
            return jnp.exp(s - m).astype(BF16), jnp.exp(sink - m)

        for pair in range(SWA_HEADS // 2):
            j = (2 * pair) // (SWA_HEADS // SWA_KV_HEADS)
            cols = slice(pair * LANES, (pair + 1) * LANES)
            qp = q_ref[rows, cols]
            e0, sink0 = weights(pl.dot(qp, k_low[j], trans_b=True), 2 * pair)
            e1, sink1 = weights(pl.dot(qp, k_high[j], trans_b=True), 2 * pair + 1)
            oa = jnp.dot(e0, v_low[j], preferred_element_type=F32)
            ob = jnp.dot(e1, v_high[j], preferred_element_type=F32)
            inv0 = 1.0 / (oa[:, SWA_HEAD_DIM:SWA_HEAD_DIM + 1] + sink0)
            inv1 = 1.0 / (ob[:, 0:1] + sink1)
            o = jnp.where(out_lo, oa * inv0, ob * inv1)
            sq = o * o
            ms_lo = jnp.sum(jnp.where(out_lo, sq, 0.0), axis=-1, keepdims=True)
            ms_hi = jnp.sum(jnp.where(out_lo, 0.0, sq), axis=-1, keepdims=True)
            ms = jnp.where(out_lo, ms_lo, ms_hi) * (1.0 / SWA_HEAD_DIM)
            o = o * lax.rsqrt(ms + NORM_EPS) * g
            o_ref[rows, cols] = o.astype(BF16)


def _swa_bias():
    w = SWA_WINDOW
    slopes = 2.0 ** (-8.0 * np.arange(1, SWA_HEADS + 1, dtype=np.float64) / SWA_HEADS)
    rel = np.arange(w)[:, None] + w - np.arange(2 * w)[None, :]
    in_window = (rel >= 0) & (rel < w)
    exists = np.stack([np.arange(2 * w) >= w, np.ones(2 * w, bool)])
    valid = in_window[None] & exists[:, None, :]
    bias = -slopes[None, :, None, None] * rel[None, None].astype(np.float64)
    return np.where(valid[:, None], bias, NEG_BIG).astype(np.float32)


def _swa(qb, kv, sinks, g2, batch, seq):
    w = SWA_WINDOW
    rows = SWA_SUB * w
    nb = seq // rows
    cur = lambda: pl.BlockSpec((rows, 512), lambda b, n: (b * nb + n, 0))
    prev = pl.BlockSpec(
        (w, 512), lambda b, n: (jnp.maximum((b * nb + n) * SWA_SUB - 1, b * nb * SWA_SUB), 0))
    table = lambda index: pl.BlockSpec((1, SWA_HEADS, w, 2 * w), index)
    bias = jnp.asarray(_swa_bias())
    return pl.pallas_call(
        _swa_kernel,
        grid=(batch, nb),
        in_specs=[pl.BlockSpec(memory_space=pltpu.SMEM),
                  table(lambda b, n: (jnp.minimum(n, 1), 0, 0, 0)),
                  table(lambda b, n: (1, 0, 0, 0)),
                  cur(), cur(), prev,
                  pl.BlockSpec((1, LANES), lambda b, n: (0, 0))],
        out_specs=cur(),
        out_shape=jax.ShapeDtypeStruct((batch * seq, 512), BF16),
        compiler_params=pltpu.CompilerParams(
            dimension_semantics=("arbitrary", "arbitrary"), vmem_limit_bytes=VMEM_LIMIT),
        name="swa",
    )(sinks, bias, bias, qb, kv, kv, g2)


def _out_route_kernel(oa_ref, ob_ref, x_ref, wo_ref, g_ref, wrh_ref, wrl_ref, br_ref,
                      earlier_ref, h13_ref, hn3_ref, route_ref, cnt_ref, hbuf, hsem):
    i = pl.program_id(0)
    tm = TM_ROUTE

    @pl.when(i == 0)
    def _():
        cnt_ref[...] = jnp.zeros_like(cnt_ref)

    def stores(step, slot):
        rows = pl.ds(step * tm, tm)
        return [_SlabTileCopy(hbuf.at[slot, which], out.at[rows], hsem.at[slot], to_slab=True)
                for which, out in enumerate((h13_ref, hn3_ref))]

    slot = i % 2

    @pl.when(i >= 2)
    def _():
        for st in stores(i - 2, slot):
            st.wait()

    h1 = (x_ref[...]
          + jnp.dot(oa_ref[...], wo_ref[:oa_ref.shape[1], :], preferred_element_type=F32)
          + jnp.dot(ob_ref[...], wo_ref[oa_ref.shape[1]:, :], preferred_element_type=F32))
    hn = _rms(h1, g_ref[...])
    hbuf[slot, 0] = h1
    hbuf[slot, 1] = hn
    for st in stores(i, slot):
        st.start()

    @pl.when(i == pl.num_programs(0) - 1)
    def _():
        @pl.when(i >= 1)
        def _():
            for st in stores(i - 1, 1 - slot):
                st.wait()
        for st in stores(i, slot):
            st.wait()

    hn_hi = hn.astype(BF16)
    hn_lo = (hn - hn_hi.astype(F32)).astype(BF16)
    logits = (pl.dot(wrh_ref[...], hn_hi, trans_b=True)
              + pl.dot(wrh_ref[...], hn_lo, trans_b=True)
              + pl.dot(wrl_ref[...], hn_hi, trans_b=True)) + br_ref[...]

    eid = lax.broadcasted_iota(jnp.int32, (N_EXPERTS, tm), 0)
    work = logits
    vals, idxs, sels = [], [], []
    for _ in range(TOP_K):
        m = jnp.max(work, axis=0, keepdims=True)
        idx = jnp.min(jnp.where(work == m, eid, N_EXPERTS), axis=0, keepdims=True)
        sel = eid == idx
        vals.append(m)
        idxs.append(idx)
        sels.append(sel)
        work = jnp.where(sel, -3e38, work)
    exps = [jnp.exp(v - vals[0]) for v in vals]
    inv_den = 1.0 / (exps[0] + exps[1] + exps[2] + exps[3])

    multihot = jnp.where(sels[0] | sels[1] | sels[2] | sels[3], 1.0, 0.0)
    before = (jnp.dot(multihot.astype(BF16), earlier_ref[...], preferred_element_type=F32)
              + cnt_ref[:, 0:1])
    ranks = [jnp.sum(jnp.where(s, before, 0.0), axis=0, keepdims=True) for s in sels]
    route_ref[...] = jnp.concatenate(
        [i.astype(F32) for i in idxs] + [e * inv_den for e in exps] + ranks
        + [jnp.zeros((ROUTE_ROWS - 3 * TOP_K, tm), F32)], axis=0)
    cnt_ref[...] += jnp.sum(multihot, axis=1, keepdims=True)


def _out_route(oa, ob, x2, wo, g2, wr_hi, wr_lo, br):
    T = x2.shape[0]
    tm = TM_ROUTE
    row = lambda w: pl.BlockSpec((tm, w), lambda i: (i, 0))
    full = lambda a: pl.BlockSpec(a.shape, lambda i: (0,) * a.ndim)
    earlier = jnp.asarray(np.triu(np.ones((tm, tm), np.float32), k=1), BF16)
    return pl.pallas_call(
        _out_route_kernel,
        grid=(T // tm,),
        in_specs=[row(512), row(512), row(D_MODEL), full(wo), full(g2),
                  full(wr_hi), full(wr_lo), full(br), full(earlier)],
        out_specs=[pl.BlockSpec(memory_space=pl.ANY), pl.BlockSpec(memory_space=pl.ANY),
                   pl.BlockSpec((ROUTE_ROWS, tm), lambda i: (0, i)),
                   pl.BlockSpec((N_EXPERTS, LANES), lambda i: (0, 0))],
        out_shape=[jax.ShapeDtypeStruct((T, SLAB_SUBLANES, LANES), F32),
                   jax.ShapeDtypeStruct((T, SLAB_SUBLANES, LANES), F32),
                   jax.ShapeDtypeStruct((ROUTE_ROWS, T), F32),
                   jax.ShapeDtypeStruct((N_EXPERTS, LANES), F32)],
        scratch_shapes=[pltpu.VMEM((2, 2, tm, D_MODEL), F32), pltpu.SemaphoreType.DMA((2,))],
        compiler_params=pltpu.CompilerParams(
            dimension_semantics=("arbitrary",), vmem_limit_bytes=VMEM_LIMIT,
            has_side_effects=True),
        name="out_route",
    )(oa, ob, x2, wo, g2, wr_hi, wr_lo, br, earlier)


def _row_copy(src, s, dst, d, sem):
    return pltpu.make_async_copy(src.at[pl.ds(s, 1)], dst.at[pl.ds(d, 1)], sem)


def _dispatch_kernel(dest_ref, pad_start_ref, pad_cnt_ref, nused_ref, hn_ref, xs_ref,
                     zero_ref, sem, zsem):
    i = pl.program_id(0)
    n = pl.num_programs(0)
    tm = TM_EXPERT
    n_tok = dest_ref.shape[0] // TOP_K
    base = i * TM_DISPATCH
    for p in range(TM_DISPATCH * TOP_K):
        r, k = divmod(p, TOP_K)
        _row_copy(hn_ref, r, xs_ref, dest_ref[base + k * n_tok + r], sem).start(priority=p % 2)
    for _ in range(TOP_K):
        pltpu.make_async_copy(hn_ref, xs_ref.at[pl.ds(0, TM_DISPATCH)], sem).wait()

    @pl.when(i == n - 1)
    def _():
        zero_ref[...] = jnp.zeros_like(zero_ref)
        sub = 8
        for e in range(N_EXPERTS):
            start, cnt = pad_start_ref[e], pad_cnt_ref[e]
            head = cnt & (sub - 1)
            body0 = start + head
            nbody = cnt // sub

            def head_copy(r):
                return _row_copy(zero_ref, 0, xs_ref, start + r, zsem)

            def body_copy(c):
                return pltpu.make_async_copy(zero_ref.at[pl.ds(0, sub)],
                                             xs_ref.at[pl.ds(body0 + c * sub, sub)], zsem)

            for copy, count in ((head_copy, head), (body_copy, nbody)):
                def zissue(r, carry, copy=copy):
                    copy(r).start()
                    return carry

                def zwait(r, carry, copy=copy):
                    copy(r).wait()
                    return carry

                lax.fori_loop(0, count, zissue, 0)
                lax.fori_loop(0, count, zwait, 0)

        def tail_copy(t):
            return pltpu.make_async_copy(zero_ref, xs_ref.at[pl.ds(t * tm, tm)], zsem)

        def tissue(t, carry):
            tail_copy(t).start()
            return carry

        def twait(t, carry):
            tail_copy(t).wait()
            return carry

        n_tiles = xs_ref.shape[0] // tm
        lax.fori_loop(nused_ref[0], n_tiles, tissue, 0)
        lax.fori_loop(nused_ref[0], n_tiles, twait, 0)


def _dispatch(dest, pad_start, pad_cnt, n_used, hn, n_rows):
    T = hn.shape[0]
    any_spec = pl.BlockSpec(memory_space=pl.ANY)
    return pl.pallas_call(
        _dispatch_kernel,
        grid_spec=pltpu.PrefetchScalarGridSpec(
            num_scalar_prefetch=4,
            grid=(T // TM_DISPATCH,),
            in_specs=[pl.BlockSpec((TM_DISPATCH, SLAB_SUBLANES, LANES), lambda i, *_: (i, 0, 0))],
            out_specs=any_spec,
            scratch_shapes=[pltpu.VMEM((TM_EXPERT, SLAB_SUBLANES, LANES), F32),
                            pltpu.SemaphoreType.DMA(()),
                            pltpu.SemaphoreType.DMA(())]),
        out_shape=jax.ShapeDtypeStruct((n_rows, SLAB_SUBLANES, LANES), F32),
        compiler_params=pltpu.CompilerParams(
            dimension_semantics=("arbitrary",), has_side_effects=True,
            vmem_limit_bytes=VMEM_LIMIT),
        name="dispatch",
    )(dest, pad_start, pad_cnt, n_used, hn)


def _experts_kernel(tile0_ref, ntile_ref, xs_ref, wg_ref, bg_ref, wl_ref, bl_ref, wd_ref, bd_ref,
                    ys_ref, wg_bf, wl_bf, wd_bf, xbuf, ybuf, xsem, ysem):
    e = pl.program_id(0)
    tm = TM_EXPERT
    t0, nt = tile0_ref[e], ntile_ref[e]

    def x_load(tile, slot):
        return _SlabTileCopy(xbuf.at[slot], xs_ref.at[pl.ds(tile * tm, tm)], xsem.at[slot],
                             to_slab=False)

    def y_store(tile, slot):
        return _SlabTileCopy(ybuf.at[slot], ys_ref.at[pl.ds(tile * tm, tm)], ysem.at[slot],
                             to_slab=True)

    @pl.when((e == 0) & (nt > 0))
    def _():
        x_load(t0, 0).start()

    @pl.when(nt > 0)
    def _():
        wg_bf[...] = wg_ref[0].astype(BF16)
        wl_bf[...] = wl_ref[0].astype(BF16)
        wd_bf[...] = wd_ref[0].astype(BF16)

    def tile_body(j, carry):
        slot = j % 2

        @pl.when(j + 1 < nt)
        def _():
            x_load(t0 + j + 1, 1 - slot).start()

        x_load(t0 + j, slot).wait()

        @pl.when(j >= 2)
        def _():
            y_store(t0 + j - 2, slot).wait()

        x = xbuf[slot].astype(BF16)
        glu = jnp.minimum(jnp.dot(x, wg_bf[...], preferred_element_type=F32) + bg_ref[0],
                          SWIGLU_LIMIT)
        lin = jnp.clip(jnp.dot(x, wl_bf[...], preferred_element_type=F32) + bl_ref[0],
                       -SWIGLU_LIMIT, SWIGLU_LIMIT)
        hid = glu * jax.nn.sigmoid(SWIGLU_ALPHA * glu) * (lin + 1.0)
        ybuf[slot] = jnp.dot(hid.astype(BF16), wd_bf[...], preferred_element_type=F32) + bd_ref[0]
        y_store(t0 + j, slot).start()
        return carry

    lax.fori_loop(0, nt, tile_body, 0)

    @pl.when(e + 1 < pl.num_programs(0))
    def _():
        @pl.when(ntile_ref[e + 1] > 0)
        def _():
            x_load(tile0_ref[e + 1], 0).start()

    @pl.when(nt >= 2)
    def _():
        y_store(t0 + nt - 2, nt % 2).wait()

    @pl.when(nt >= 1)
    def _():
        y_store(t0 + nt - 1, (nt - 1) % 2).wait()

    @pl.when(e == pl.num_programs(0) - 1)
    def _():
        ybuf[0] = jnp.zeros((tm, D_MODEL), F32)
        n_tiles = ys_ref.shape[0] // tm

        def tail_issue(t, carry):
            y_store(t, 0).start()
            return carry

        def tail_wait(t, carry):
            y_store(t, 0).wait()
            return carry

        lax.fori_loop(t0 + nt, n_tiles, tail_issue, 0)
        lax.fori_loop(t0 + nt, n_tiles, tail_wait, 0)


def _experts(tile0, ntile, xs, w_glu, b_glu, w_lin, b_lin, w_down, b_down):
    tm = TM_EXPERT
    any_spec = pl.BlockSpec(memory_space=pl.ANY)
    wspec = lambda: pl.BlockSpec((1, D_MODEL, D_FF), lambda e, *_: (e, 0, 0))
    bspec = lambda: pl.BlockSpec((1, 1, D_FF), lambda e, *_: (e, 0, 0))
    return pl.pallas_call(
        _experts_kernel,
        grid_spec=pltpu.PrefetchScalarGridSpec(
            num_scalar_prefetch=2,
            grid=(N_EXPERTS,),
            in_specs=[any_spec, wspec(), bspec(), wspec(), bspec(), wspec(), bspec()],
            out_specs=any_spec,
            scratch_shapes=[pltpu.VMEM((D_MODEL, D_FF), BF16)] * 3
            + [pltpu.VMEM((2, tm, D_MODEL), F32), pltpu.VMEM((2, tm, D_MODEL), F32),
               pltpu.SemaphoreType.DMA((2,)), pltpu.SemaphoreType.DMA((2,))]),
        out_shape=jax.ShapeDtypeStruct(xs.shape, F32),
        compiler_params=pltpu.CompilerParams(
            dimension_semantics=("arbitrary",), vmem_limit_bytes=VMEM_LIMIT,
            has_side_effects=True),
        name="experts",
    )(tile0, ntile, xs, w_glu, b_glu.reshape(N_EXPERTS, 1, D_FF),
      w_lin, b_lin.reshape(N_EXPERTS, 1, D_FF), w_down, b_down.reshape(N_EXPERTS, 1, D_MODEL))


def _combine_kernel(dest_ref, gate_ref, ys_ref, h1_ref, g_ref, out_ref, buf, obuf, sems, osems):
    s = pl.program_id(0)
    tm = TM_COMBINE
    n = out_ref.shape[0] // tm
    n_tok = dest_ref.shape[0] // TOP_K

    def out_store(tile, slot):
        return _SlabTileCopy(out_ref.at[pl.ds(tile * tm, tm)], obuf.at[slot], osems.at[slot],
                             to_slab=False)

    def issue_row(slot, r):
        for k in range(TOP_K):
            _row_copy(ys_ref, dest_ref[s * tm + k * n_tok + r], buf.at[slot, k], r,
                      sems.at[slot]).start(priority=k % 2)

    def reduce_row(slot, r):
        acc = h1_ref[r]
        for k in range(TOP_K):
            acc = acc + gate_ref[(s - 1) * tm + k * n_tok + r] * buf[slot, k, r]
        ss = jnp.sum(jnp.sum(acc * acc, axis=1, keepdims=True), axis=0, keepdims=True)
        obuf[slot, r] = acc * lax.rsqrt(ss * (1.0 / D_MODEL) + NORM_EPS) * g_ref[0]

    def step_body(gather_slot, reduce_slot):
        for r in range(tm):
            if gather_slot is not None:
                issue_row(gather_slot, r)
            if reduce_slot is not None:
                reduce_row(reduce_slot, r)
        if reduce_slot is not None:
            out_store(s - 1, reduce_slot).start()

    for parity in range(2):
        other = 1 - parity

        @pl.when(s % 2 == parity)
        def _():
            @pl.when(s >= 3)
            def _():
                out_store(s - 3, other).wait()

            @pl.when(s > 0)
            def _():
                for k in range(TOP_K):
                    pltpu.make_async_copy(ys_ref.at[pl.ds(0, tm)], buf.at[other, k],
                                          sems.at[other]).wait()

            @pl.when(s < n)
            def _():
                step_body(parity, None)

            @pl.when(s > 0)
            def _():
                step_body(None, other)

            if n % 2 == parity:
                @pl.when(s == n)
                def _():
                    if n >= 2:
                        out_store(n - 2, parity).wait()
                    out_store(n - 1, other).wait()


def _combine(dest, ys3, h13, gate, fg):
    T = h13.shape[0]
    tm = TM_COMBINE
    slab = (SLAB_SUBLANES, LANES)
    prev_tile = lambda s, *_: (jnp.maximum(s - 1, 0), 0, 0)
    return pl.pallas_call(
        _combine_kernel,
        grid_spec=pltpu.PrefetchScalarGridSpec(
            num_scalar_prefetch=2,
            grid=(T // tm + 1,),
            in_specs=[pl.BlockSpec(memory_space=pl.ANY),
                      pl.BlockSpec((tm,) + slab, prev_tile),
                      pl.BlockSpec((1,) + slab, lambda s, *_: (0, 0, 0))],
            out_specs=pl.BlockSpec(memory_space=pl.ANY),
            scratch_shapes=[pltpu.VMEM((2, TOP_K, tm) + slab, F32),
                            pltpu.VMEM((2, tm) + slab, F32),
                            pltpu.SemaphoreType.DMA((2,)),
                            pltpu.SemaphoreType.DMA((2,))]),
        out_shape=jax.ShapeDtypeStruct((T, D_MODEL), F32),
        compiler_params=pltpu.CompilerParams(
            dimension_semantics=("arbitrary",), vmem_limit_bytes=VMEM_LIMIT,
            has_side_effects=True),
        name="combine",
    )(dest, gate, ys3, h13, fg.reshape((1,) + slab))


def _swap_halves(w):
    h = w.shape[-1] // 2
    return jnp.concatenate([w[..., h:], w[..., :h]], axis=-1)


def _layer(x2, batch, seq, norm1_g, w_in, w_alpha_up, b_alpha, gla_norm_g, swa_sinks, swa_norm_g,
           w_out, norm2_g, w_router, b_router, w_glu, b_glu, w_lin, b_lin, w_down, b_down):
    T = x2.shape[0]
    kb_w, vb_w = w_in[:, 2064:2192], w_in[:, 2192:2320]
    w_z = w_in[:, 1536:1552]
    w_cat = jnp.concatenate([
        w_in[:, 0:1536], w_in[:, 1552:2064],
        kb_w, _swap_halves(kb_w), vb_w, _swap_halves(vb_w),
        jnp.pad(jnp.tile(w_z, (1, Z_PIECES)), [(0, 0), (0, LANES - Z_PIECES * GLA_RANK)]),
    ], axis=1).astype(BF16)
    wup_hi = w_alpha_up.astype(BF16)
    wup_lo = (w_alpha_up - wup_hi.astype(F32)).astype(BF16)
    wup_cat = jnp.pad(jnp.concatenate([wup_hi, wup_hi, wup_hi, wup_lo, wup_lo], axis=0),
                      [(0, LANES - Z_PIECES * GLA_RANK), (0, 0)])

    qk, vr, la, qb, kv = _in_proj(
        x2, norm1_g.reshape(1, -1), w_cat, wup_cat, b_alpha.reshape(1, -1))
    oa = _gla(qk, vr, la, gla_norm_g.reshape(1, -1), batch, seq)
    ob = _swa(qb, kv, swa_sinks, jnp.tile(swa_norm_g, 2).reshape(1, -1), batch, seq)

    wo = w_out.astype(BF16)
    wr_t = w_router.T
    wr_hi = wr_t.astype(BF16)
    wr_lo = (wr_t - wr_hi.astype(F32)).astype(BF16)
    h1, hn, route, cnt = _out_route(oa, ob, x2, wo, norm2_g.reshape(1, -1),
                                    wr_hi, wr_lo, b_router.reshape(-1, 1))

    tm = TM_EXPERT
    n_tiles = T * TOP_K // tm + N_EXPERTS
    counts = cnt[:, 0].astype(jnp.int32)
    padded = (counts + tm - 1) // tm * tm
    pends = jnp.cumsum(padded)
    pstarts = pends - padded
    top_idx = route[0:TOP_K].astype(jnp.int32)
    gate = route[TOP_K:2 * TOP_K].reshape(-1)
    rank = route[2 * TOP_K:3 * TOP_K].astype(jnp.int32)
    experts = jnp.arange(N_EXPERTS)[:, None, None]
    seg_start = jnp.sum(jnp.where(top_idx[None] == experts, pstarts[:, None, None], 0), axis=0)
    dest = (seg_start + rank).reshape(-1)
    n_used = (pends[-1] // tm).reshape(1)
    xs = _dispatch(dest, pstarts + counts, padded - counts, n_used, hn, n_tiles * tm)
    ys = _experts(pstarts // tm, padded // tm, xs, w_glu, b_glu, w_lin, b_lin, w_down, b_down)
    return dest, ys, h1, gate


def kernel(x, norm1_g, w_in, w_alpha_up, b_alpha, gla_norm_g, swa_sinks, swa_norm_g, w_out,
           norm2_g, w_router, b_router, w_glu, b_glu, w_lin, b_lin, w_down, b_down, final_g):
    batch, seq, d = x.shape
    assert norm1_g.shape[0] == 1, "single-layer problem"
    x2 = x.reshape(batch * seq, d)
    dest, ys, h1, gate = _layer(
        x2, batch, seq, norm1_g[0], w_in[0], w_alpha_up[0], b_alpha[0], gla_norm_g[0],
        swa_sinks[0], swa_norm_g[0], w_out[0], norm2_g[0], w_router[0], b_router[0],
        w_glu[0], b_glu[0], w_lin[0], b_lin[0], w_down[0], b_down[0])
    out = _combine(dest, ys, h1, gate, final_g.reshape(1, -1))
    return out.reshape(batch, seq, d)
```

```python
import numpy as np
import jax
import jax.numpy as jnp
from jax import lax
from jax.experimental import pallas as pl
from jax.experimental.pallas import tpu as pltpu

F32 = jnp.float32
BF16 = jnp.bfloat16

D_MODEL = 1024
GLA_HEADS = 4
GLA_DK = 64
GLA_DV = 128
GLA_RANK = 16
GLA_GATE_TAU = 16.0
GLA_CHUNK = 64
SWA_HEADS = 8
SWA_KV_HEADS = 2
SWA_HEAD_DIM = 64
SWA_WINDOW = 128
N_EXPERTS = 32
TOP_K = 4
D_FF = 1024
SWIGLU_LIMIT = 7.0
SWIGLU_ALPHA = 1.702
NORM_EPS = 1e-5

LANES = 128
SLAB_SUBLANES = D_MODEL // LANES
VMEM_LIMIT = 56 * 1024 * 1024

TM_PROJ = 1024
TL_GLA = 1024
GLA_CUM_ROWS = 256
TM_ROUTE = 1024
TM_EXPERT = 512
TM_COMBINE = 256
SWA_SUB = 8
TM_DISPATCH = 256

NEG_BIG = -1e30
ROUTE_ROWS = 16

C_QA, C_KA, C_VA, C_RA, C_QB, C_KB, C_KBS, C_VB, C_VBS, C_Z, C_END = (
    0, 256, 512, 1024, 1536, 2048, 2176, 2304, 2432, 2560, 2688)
Z_PIECES = 5


def _split3(x):
    hi = x.astype(BF16).astype(F32)
    r = x - hi
    mid = r.astype(BF16).astype(F32)
    lo = (r - mid).astype(BF16).astype(F32)
    return hi, mid, lo


class _SlabTileCopy:
    def __init__(self, flat, slab, sem, to_slab):
        self.copies = []
        for c in range(SLAB_SUBLANES):
            pair = (flat.at[:, pl.ds(c * LANES, LANES)], slab.at[:, c, :])
            src, dst = pair if to_slab else pair[::-1]
            self.copies.append(pltpu.make_async_copy(src, dst, sem))

    def start(self):
        for cp in self.copies:
            cp.start()

    def wait(self):
        for cp in self.copies:
            cp.wait()


def _rms(x, g):
    return x * lax.rsqrt(jnp.mean(x * x, axis=-1, keepdims=True) + NORM_EPS) * g


def _in_proj_kernel(x_ref, g_ref, w_ref, wup_ref, ba_ref,
                    qk_ref, vr_ref, la_ref, qb_ref, kv_ref):
    u = _rms(x_ref[...], g_ref[...]).astype(BF16)

    def proj(c0, c1):
        return jnp.dot(u, w_ref[:, c0:c1], preferred_element_type=F32)

    qk_ref[...] = proj(C_QA, C_VA).astype(BF16)
    vr_ref[:, :C_RA - C_VA] = proj(C_VA, C_RA).astype(BF16)
    vr_ref[:, C_RA - C_VA:] = proj(C_RA, C_QB).astype(BF16)
    qb_ref[...] = proj(C_QB, C_KB).astype(BF16)
    kv_ref[...] = proj(C_KB, C_Z).astype(BF16)
    z = proj(C_Z, C_END)
    hi, mid, lo = _split3(z)
    piece = lax.broadcasted_iota(jnp.int32, z.shape, 1) // GLA_RANK
    zc = jnp.where((piece == 0) | (piece == 3), hi, jnp.where(piece == 2, lo, mid)).astype(BF16)
    y = jnp.dot(zc, wup_ref[...], preferred_element_type=F32) + ba_ref[...]
    log_sig = jnp.minimum(y, 0.0) - jnp.log1p(jnp.exp(-jnp.abs(y)))
    la_ref[...] = log_sig * (1.0 / GLA_GATE_TAU)


def _in_proj(x2, g1, w_cat, wup_p, ba_p):
    T = x2.shape[0]
    tm = TM_PROJ
    row = lambda w: pl.BlockSpec((tm, w), lambda i: (i, 0))
    full = lambda a: pl.BlockSpec(a.shape, lambda i: (0,) * a.ndim)
    outs = [(512, BF16), (1024, BF16), (256, F32), (512, BF16), (512, BF16)]
    return pl.pallas_call(
        _in_proj_kernel,
        grid=(T // tm,),
        in_specs=[row(D_MODEL), full(g1), full(w_cat), full(wup_p), full(ba_p)],
        out_specs=[row(w) for w, _ in outs],
        out_shape=[jax.ShapeDtypeStruct((T, w), dt) for w, dt in outs],
        compiler_params=pltpu.CompilerParams(
            dimension_semantics=("arbitrary",), vmem_limit_bytes=VMEM_LIMIT),
        name="in_proj",
    )(x2, g1, w_cat, wup_p, ba_p)


def _gla_kernel(qk_ref, vr_ref, la_ref, g_ref, cum_ref, o_ref, st_ref):
    @pl.when(pl.program_id(1) == 0)
    def _():
        st_ref[...] = jnp.zeros_like(st_ref)

    tl = TL_GLA
    c = GLA_CHUNK
    kw = GLA_HEADS * GLA_DK
    vw = GLA_HEADS * GLA_DV
    causal = (lax.broadcasted_iota(jnp.int32, (c, c), 0)
              >= lax.broadcasted_iota(jnp.int32, (c, c), 1))
    low_half = lax.broadcasted_iota(jnp.int32, (c, LANES), 1) < GLA_DK
    g = g_ref[...]
    b_groups = []
    for grp in range(tl // GLA_CUM_ROWS):
        la = la_ref[grp * GLA_CUM_ROWS:(grp + 1) * GLA_CUM_ROWS, :]
        pieces = jnp.concatenate([p.astype(BF16) for p in _split3(la)], axis=1)
        b3 = jnp.dot(cum_ref[...], pieces, preferred_element_type=F32)
        b_groups.append(b3[:, :kw] + b3[:, kw:2 * kw] + b3[:, 2 * kw:])
    for ch in range(tl // c):
        rows = slice(ch * c, (ch + 1) * c)
        in_grp = (ch * c) % GLA_CUM_ROWS
        b = b_groups[(ch * c) // GLA_CUM_ROWS][in_grp:in_grp + c]
        b_last = b[c - 1:c]
        qf = qk_ref[rows, :kw].astype(F32)
        kf = qk_ref[rows, kw:].astype(F32)
        q_e = (qf * jnp.exp(b) * (GLA_DK ** -0.5)).astype(BF16)
        k_e = (kf * jnp.exp(-b)).astype(BF16)
        k_t = (kf * jnp.exp(b_last - b)).astype(BF16)
        decay = jnp.exp(b_last)
        for h in range(GLA_HEADS):
            ps = slice((h // 2) * LANES, (h // 2 + 1) * LANES)
            mine = low_half if h % 2 == 0 else ~low_half
            qp, kp = q_e[:, ps], k_e[:, ps]
            qh = jnp.where(mine, qp, jnp.zeros_like(qp))
            kth = jnp.where(mine, k_t[:, ps], jnp.zeros_like(qp))
            vs = slice(h * GLA_DV, (h + 1) * GLA_DV)
            vh = vr_ref[rows, vs]
            a = pl.dot(qh, kp, trans_b=True)
            a = jnp.where(causal, a, 0.0).astype(BF16)
            st = st_ref[h]
            o = (jnp.dot(a, vh, preferred_element_type=F32)
                 + pl.dot(qh, st.astype(BF16), trans_b=True))
            st_ref[h] = st * decay[:, ps] + pl.dot(vh, kth, trans_a=True)
            rh = vr_ref[rows, slice(vw + h * GLA_DV, vw + (h + 1) * GLA_DV)].astype(F32)
            o = _rms(o, g) * (rh * jax.nn.sigmoid(rh))
            o_ref[rows, vs] = o.astype(BF16)


def _gla(qk, vr, la, g, batch, seq):
    tl = TL_GLA
    nl = seq // tl
    row = lambda w: pl.BlockSpec((tl, w), lambda b, i: (b * nl + i, 0))
    r = np.arange(GLA_CUM_ROWS)
    cum = jnp.asarray((r[None, :] <= r[:, None])
                      & (r[None, :] // GLA_CHUNK == r[:, None] // GLA_CHUNK), BF16)
    return pl.pallas_call(
        _gla_kernel,
        grid=(batch, nl),
        in_specs=[row(512), row(1024), row(256),
                  pl.BlockSpec((1, GLA_DV), lambda b, i: (0, 0)),
                  pl.BlockSpec((GLA_CUM_ROWS, GLA_CUM_ROWS), lambda b, i: (0, 0))],
        out_specs=row(512),
        out_shape=jax.ShapeDtypeStruct((batch * seq, 512), BF16),
        scratch_shapes=[pltpu.VMEM((GLA_HEADS, GLA_DV, LANES), F32)],
        compiler_params=pltpu.CompilerParams(
            dimension_semantics=("arbitrary", "arbitrary"), vmem_limit_bytes=VMEM_LIMIT),
        name="gla",
    )(qk, vr, la, g, cum)


def _swa_kernel(sink_ref, bias0_ref, bias_ref, q_ref, kvc_ref, kvp_ref, g_ref, o_ref):
    w = SWA_WINDOW
    scale = jnp.asarray(SWA_HEAD_DIM ** -0.5, BF16)
    kv_lane = lax.broadcasted_iota(jnp.int32, (2 * w, LANES), 1)
    lane_lo = kv_lane < SWA_HEAD_DIM
    out_lo = lax.broadcasted_iota(jnp.int32, (w, LANES), 1) < SWA_HEAD_DIM
    ones_hi = jnp.where(kv_lane == SWA_HEAD_DIM, 1.0, 0.0).astype(BF16)
    ones_lo = jnp.where(kv_lane == 0, 1.0, 0.0).astype(BF16)
    g = g_ref[...]

    for sb in range(SWA_SUB):
        rows = slice(sb * w, (sb + 1) * w)
        if sb == 0:
            kv = jnp.concatenate([kvp_ref[...], kvc_ref[0:w, :]], axis=0)
            bias = bias0_ref
        else:
            kv = kvc_ref[(sb - 1) * w:(sb + 1) * w, :]
            bias = bias_ref
        k = kv[:, 0:LANES] * scale
        ks = kv[:, LANES:2 * LANES] * scale
        v, vs = kv[:, 2 * LANES:3 * LANES], kv[:, 3 * LANES:]
        zero = jnp.zeros_like(k)
        k_low = [jnp.where(lane_lo, k, zero), jnp.where(lane_lo, ks, zero)]
        k_high = [jnp.where(lane_lo, zero, ks), jnp.where(lane_lo, zero, k)]
        v_low = [jnp.where(lane_lo, v, zero) + ones_hi, jnp.where(lane_lo, vs, zero) + ones_hi]
        v_high = [jnp.where(lane_lo, zero, vs) + ones_lo, jnp.where(lane_lo, zero, v) + ones_lo]

        def weights(s, head):
            s = s + bias[0, head]
            sink = sink_ref[head]
            m = jnp.maximum(jnp.max(s, axis=-1, keepdims=True), sink)
            return jnp.exp(s - m).astype(BF16), jnp.exp(sink - m)

        for pair in range(SWA_HEADS // 2):
            j = (2 * pair) // (SWA_HEADS // SWA_KV_HEADS)
            cols = slice(pair * LANES, (pair + 1) * LANES)
            qp = q_ref[rows, cols]
            e0, sink0 = weights(pl.dot(qp, k_low[j], trans_b=True), 2 * pair)
            e1, sink1 = weights(pl.dot(qp, k_high[j], trans_b=True), 2 * pair + 1)
            oa = jnp.dot(e0, v_low[j], preferred_element_type=F32)
            ob = jnp.dot(e1, v_high[j], preferred_element_type=F32)
            inv0 = 1.0 / (oa[:, SWA_HEAD_DIM:SWA_HEAD_DIM + 1] + sink0)
            inv1 = 1.0 / (ob[:, 0:1] + sink1)
            o = jnp.where(out_lo, oa * inv0, ob * inv1)
            sq = o * o
            ms_lo = jnp.sum(jnp.where(out_lo, sq, 0.0), axis=-1, keepdims=True)
            ms_hi = jnp.sum(jnp.where(out_lo, 0.0, sq), axis=-1, keepdims=True)
            ms = jnp.where(out_lo, ms_lo, ms_hi) * (1.0 / SWA_HEAD_DIM)
            o = o * lax.rsqrt(ms + NORM_EPS) * g
            o_ref[rows, cols] = o.astype(BF16)


def _swa_bias():
    w = SWA_WINDOW
    slopes = 2.0 ** (-8.0 * np.arange(1, SWA_HEADS + 1, dtype=np.float64) / SWA_HEADS)
    rel = np.arange(w)[:, None] + w - np.arange(2 * w)[None, :]
    in_window = (rel >= 0) & (rel < w)
    exists = np.stack([np.arange(2 * w) >= w, np.ones(2 * w, bool)])
    valid = in_window[None] & exists[:, None, :]
    bias = -slopes[None, :, None, None] * rel[None, None].astype(np.float64)
    return np.where(valid[:, None], bias, NEG_BIG).astype(np.float32)


def _swa(qb, kv, sinks, g2, batch, seq):
    w = SWA_WINDOW
    rows = SWA_SUB * w
    nb = seq // rows
    cur = lambda: pl.BlockSpec((rows, 512), lambda b, n: (b * nb + n, 0))
    prev = pl.BlockSpec(
        (w, 512), lambda b, n: (jnp.maximum((b * nb + n) * SWA_SUB - 1, b * nb * SWA_SUB), 0))
    table = lambda index: pl.BlockSpec((1, SWA_HEADS, w, 2 * w), index)
    bias = jnp.asarray(_swa_bias())
    return pl.pallas_call(
        _swa_kernel,
        grid=(batch, nb),
        in_specs=[pl.BlockSpec(memory_space=pltpu.SMEM),
                  table(lambda b, n: (jnp.minimum(n, 1), 0, 0, 0)),
                  table(lambda b, n: (1, 0, 0, 0)),
                  cur(), cur(), prev,
                  pl.BlockSpec((1, LANES), lambda b, n: (0, 0))],
        out_specs=cur(),
        out_shape=jax.ShapeDtypeStruct((batch * seq, 512), BF16),
        compiler_params=pltpu.CompilerParams(
            dimension_semantics=("arbitrary", "arbitrary"), vmem_limit_bytes=VMEM_LIMIT),
        name="swa",
    )(sinks, bias, bias, qb, kv, kv, g2)


def _out_route_kernel(oa_ref, ob_ref, x_ref, wo_ref, g_ref, wrh_ref, wrl_ref, br_ref,
                      earlier_ref, h13_ref, hn3_ref, route_ref, cnt_ref, hbuf, hsem):
    i = pl.program_id(0)
    tm = TM_ROUTE

    @pl.when(i == 0)
    def _():
        cnt_ref[...] = jnp.zeros_like(cnt_ref)

    def stores(step, slot):
        rows = pl.ds(step * tm, tm)
        return [_SlabTileCopy(hbuf.at[slot, which], out.at[rows], hsem.at[slot], to_slab=True)
                for which, out in enumerate((h13_ref, hn3_ref))]

    slot = i % 2

    @pl.when(i >= 2)
    def _():
        for st in stores(i - 2, slot):
            st.wait()

    h1 = (x_ref[...]
          + jnp.dot(oa_ref[...], wo_ref[:oa_ref.shape[1], :], preferred_element_type=F32)
          + jnp.dot(ob_ref[...], wo_ref[oa_ref.shape[1]:, :], preferred_element_type=F32))
    hn = _rms(h1, g_ref[...])
    hbuf[slot, 0] = h1
    hbuf[slot, 1] = hn
    for st in stores(i, slot):
        st.start()

    @pl.when(i == pl.num_programs(0) - 1)
    def _():
        @pl.when(i >= 1)
        def _():
            for st in stores(i - 1, 1 - slot):
                st.wait()
        for st in stores(i, slot):
            st.wait()

    hn_hi = hn.astype(BF16)
    hn_lo = (hn - hn_hi.astype(F32)).astype(BF16)
    logits = (pl.dot(wrh_ref[...], hn_hi, trans_b=True)
              + pl.dot(wrh_ref[...], hn_lo, trans_b=True)
              + pl.dot(wrl_ref[...], hn_hi, trans_b=True)) + br_ref[...]

    eid = lax.broadcasted_iota(jnp.int32, (N_EXPERTS, tm), 0)
    work = logits
    vals, idxs, sels = [], [], []
    for _ in range(TOP_K):
        m = jnp.max(work, axis=0, keepdims=True)
        idx = jnp.min(jnp.where(work == m, eid, N_EXPERTS), axis=0, keepdims=True)
        sel = eid == idx
        vals.append(m)
        idxs.append(idx)
        sels.append(sel)
        work = jnp.where(sel, -3e38, work)
    exps = [jnp.exp(v - vals[0]) for v in vals]
    inv_den = 1.0 / (exps[0] + exps[1] + exps[2] + exps[3])

    multihot = jnp.where(sels[0] | sels[1] | sels[2] | sels[3], 1.0, 0.0)
    before = (jnp.dot(multihot.astype(BF16), earlier_ref[...], preferred_element_type=F32)
              + cnt_ref[:, 0:1])
    ranks = [jnp.sum(jnp.where(s, before, 0.0), axis=0, keepdims=True) for s in sels]
    route_ref[...] = jnp.concatenate(
        [i.astype(F32) for i in idxs] + [e * inv_den for e in exps] + ranks
        + [jnp.zeros((ROUTE_ROWS - 3 * TOP_K, tm), F32)], axis=0)
    cnt_ref[...] += jnp.sum(multihot, axis=1, keepdims=True)


def _out_route(oa, ob, x2, wo, g2, wr_hi, wr_lo, br):
    T = x2.shape[0]
    tm = TM_ROUTE
    row = lambda w: pl.BlockSpec((tm, w), lambda i: (i, 0))
    full = lambda a: pl.BlockSpec(a.shape, lambda i: (0,) * a.ndim)
    earlier = jnp.asarray(np.triu(np.ones((tm, tm), np.float32), k=1), BF16)
    return pl.pallas_call(
        _out_route_kernel,
        grid=(T // tm,),
        in_specs=[row(512), row(512), row(D_MODEL), full(wo), full(g2),
                  full(wr_hi), full(wr_lo), full(br), full(earlier)],
        out_specs=[pl.BlockSpec(memory_space=pl.ANY), pl.BlockSpec(memory_space=pl.ANY),
                   pl.BlockSpec((ROUTE_ROWS, tm), lambda i: (0, i)),
                   pl.BlockSpec((N_EXPERTS, LANES), lambda i: (0, 0))],
        out_shape=[jax.ShapeDtypeStruct((T, SLAB_SUBLANES, LANES), F32),
                   jax.ShapeDtypeStruct((T, SLAB_SUBLANES, LANES), F32),
                   jax.ShapeDtypeStruct((ROUTE_ROWS, T), F32),
                   jax.ShapeDtypeStruct((N_EXPERTS, LANES), F32)],
        scratch_shapes=[pltpu.VMEM((2, 2, tm, D_MODEL), F32), pltpu.SemaphoreType.DMA((2,))],
        compiler_params=pltpu.CompilerParams(
            dimension_semantics=("arbitrary",), vmem_limit_bytes=VMEM_LIMIT,
            has_side_effects=True),
        name="out_route",
    )(oa, ob, x2, wo, g2, wr_hi, wr_lo, br, earlier)


def _row_copy(src, s, dst, d, sem):
    return pltpu.make_async_copy(src.at[pl.ds(s, 1)], dst.at[pl.ds(d, 1)], sem)


def _dispatch_kernel(dest_ref, pad_start_ref, pad_cnt_ref, nused_ref, hn_ref, xs_ref,
                     zero_ref, sem, zsem):
    i = pl.program_id(0)
    n = pl.num_programs(0)
    tm = TM_EXPERT
    n_tok = dest_ref.shape[0] // TOP_K
    base = i * TM_DISPATCH
    for p in range(TM_DISPATCH * TOP_K):
        r, k = divmod(p, TOP_K)
        _row_copy(hn_ref, r, xs_ref, dest_ref[base + k * n_tok + r], sem).start(priority=p % 2)
    for _ in range(TOP_K):
        pltpu.make_async_copy(hn_ref, xs_ref.at[pl.ds(0, TM_DISPATCH)], sem).wait()

    @pl.when(i == n - 1)
    def _():
        zero_ref[...] = jnp.zeros_like(zero_ref)
        sub = 8
        for e in range(N_EXPERTS):
            start, cnt = pad_start_ref[e], pad_cnt_ref[e]
            head = cnt & (sub - 1)
            body0 = start + head
            nbody = cnt // sub

            def head_copy(r):
                return _row_copy(zero_ref, 0, xs_ref, start + r, zsem)

            def body_copy(c):
                return pltpu.make_async_copy(zero_ref.at[pl.ds(0, sub)],
                                             xs_ref.at[pl.ds(body0 + c * sub, sub)], zsem)

            for copy, count in ((head_copy, head), (body_copy, nbody)):
                def zissue(r, carry, copy=copy):
                    copy(r).start()
                    return carry

                def zwait(r, carry, copy=copy):
                    copy(r).wait()
                    return carry

                lax.fori_loop(0, count, zissue, 0)
                lax.fori_loop(0, count, zwait, 0)

        def tail_copy(t):
            return pltpu.make_async_copy(zero_ref, xs_ref.at[pl.ds(t * tm, tm)], zsem)

        def tissue(t, carry):
            tail_copy(t).start()
            return carry

        def twait(t, carry):
            tail_copy(t).wait()
            return carry

        n_tiles = xs_ref.shape[0] // tm
        lax.fori_loop(nused_ref[0], n_tiles, tissue, 0)
        lax.fori_loop(nused_ref[0], n_tiles, twait, 0)


def _dispatch(dest, pad_start, pad_cnt, n_used, hn, n_rows):
    T = hn.shape[0]
    any_spec = pl.BlockSpec(memory_space=pl.ANY)
    return pl.pallas_call(
        _dispatch_kernel,
        grid_spec=pltpu.PrefetchScalarGridSpec(
            num_scalar_prefetch=4,
            grid=(T // TM_DISPATCH,),
            in_specs=[pl.BlockSpec((TM_DISPATCH, SLAB_SUBLANES, LANES), lambda i, *_: (i, 0, 0))],
            out_specs=any_spec,
            scratch_shapes=[pltpu.VMEM((TM_EXPERT, SLAB_SUBLANES, LANES), F32),
                            pltpu.SemaphoreType.DMA(()),
                            pltpu.SemaphoreType.DMA(())]),
        out_shape=jax.ShapeDtypeStruct((n_rows, SLAB_SUBLANES, LANES), F32),
        compiler_params=pltpu.CompilerParams(
            dimension_semantics=("arbitrary",), has_side_effects=True,
            vmem_limit_bytes=VMEM_LIMIT),
        name="dispatch",
    )(dest, pad_start, pad_cnt, n_used, hn)


def _experts_kernel(tile0_ref, ntile_ref, xs_ref, wg_ref, bg_ref, wl_ref, bl_ref, wd_ref, bd_ref,
                    ys_ref, wg_bf, wl_bf, wd_bf, xbuf, ybuf, xsem, ysem):
    e = pl.program_id(0)
    tm = TM_EXPERT
    t0, nt = tile0_ref[e], ntile_ref[e]

    def x_load(tile, slot):
        return _SlabTileCopy(xbuf.at[slot], xs_ref.at[pl.ds(tile * tm, tm)], xsem.at[slot],
                             to_slab=False)

    def y_store(tile, slot):
        return _SlabTileCopy(ybuf.at[slot], ys_ref.at[pl.ds(tile * tm, tm)], ysem.at[slot],
                             to_slab=True)

    @pl.when((e == 0) & (nt > 0))
    def _():
        x_load(t0, 0).start()

    @pl.when(nt > 0)
    def _():
        wg_bf[...] = wg_ref[0].astype(BF16)
        wl_bf[...] = wl_ref[0].astype(BF16)
        wd_bf[...] = wd_ref[0].astype(BF16)

    def tile_body(j, carry):
        slot = j % 2

        @pl.when(j + 1 < nt)
        def _():
            x_load(t0 + j + 1, 1 - slot).start()

        x_load(t0 + j, slot).wait()

        @pl.when(j >= 2)
        def _():
            y_store(t0 + j - 2, slot).wait()

        x = xbuf[slot].astype(BF16)
        glu = jnp.minimum(jnp.dot(x, wg_bf[...], preferred_element_type=F32) + bg_ref[0],
                          SWIGLU_LIMIT)
        lin = jnp.clip(jnp.dot(x, wl_bf[...], preferred_element_type=F32) + bl_ref[0],
                       -SWIGLU_LIMIT, SWIGLU_LIMIT)
        hid = glu * jax.nn.sigmoid(SWIGLU_ALPHA * glu) * (lin + 1.0)
        ybuf[slot] = jnp.dot(hid.astype(BF16), wd_bf[...], preferred_element_type=F32) + bd_ref[0]
        y_store(t0 + j, slot).start()
        return carry

    lax.fori_loop(0, nt, tile_body, 0)

    @pl.when(e + 1 < pl.num_programs(0))
    def _():
        @pl.when(ntile_ref[e + 1] > 0)
        def _():
            x_load(tile0_ref[e + 1], 0).start()

    @pl.when(nt >= 2)
    def _():
        y_store(t0 + nt - 2, nt % 2).wait()

    @pl.when(nt >= 1)
    def _():
        y_store(t0 + nt - 1, (nt - 1) % 2).wait()

    @pl.when(e == pl.num_programs(0) - 1)
    def _():
        ybuf[0] = jnp.zeros((tm, D_MODEL), F32)
        n_tiles = ys_ref.shape[0] // tm

        def tail_issue(t, carry):
            y_store(t, 0).start()
            return carry

        def tail_wait(t, carry):
            y_store(t, 0).wait()
            return carry

        lax.fori_loop(t0 + nt, n_tiles, tail_issue, 0)
        lax.fori_loop(t0 + nt, n_tiles, tail_wait, 0)


def _experts(tile0, ntile, xs, w_glu, b_glu, w_lin, b_lin, w_down, b_down):
    tm = TM_EXPERT
    any_spec = pl.BlockSpec(memory_space=pl.ANY)
    wspec = lambda: pl.BlockSpec((1, D_MODEL, D_FF), lambda e, *_: (e, 0, 0))
    bspec = lambda: pl.BlockSpec((1, 1, D_FF), lambda e, *_: (e, 0, 0))
    return pl.pallas_call(
        _experts_kernel,
        grid_spec=pltpu.PrefetchScalarGridSpec(
            num_scalar_prefetch=2,
            grid=(N_EXPERTS,),
            in_specs=[any_spec, wspec(), bspec(), wspec(), bspec(), wspec(), bspec()],
            out_specs=any_spec,
            scratch_shapes=[pltpu.VMEM((D_MODEL, D_FF), BF16)] * 3
            + [pltpu.VMEM((2, tm, D_MODEL), F32), pltpu.VMEM((2, tm, D_MODEL), F32),
               pltpu.SemaphoreType.DMA((2,)), pltpu.SemaphoreType.DMA((2,))]),
        out_shape=jax.ShapeDtypeStruct(xs.shape, F32),
        compiler_params=pltpu.CompilerParams(
            dimension_semantics=("arbitrary",), vmem_limit_bytes=VMEM_LIMIT,
            has_side_effects=True),
        name="experts",
    )(tile0, ntile, xs, w_glu, b_glu.reshape(N_EXPERTS, 1, D_FF),
      w_lin, b_lin.reshape(N_EXPERTS, 1, D_FF), w_down, b_down.reshape(N_EXPERTS, 1, D_MODEL))


def _combine_kernel(dest_ref, gate_ref, ys_ref, h1_ref, g_ref, out_ref, buf, obuf, sems, osems):
    s = pl.program_id(0)
    tm = TM_COMBINE
    n = out_ref.shape[0] // tm
    n_tok = dest_ref.shape[0] // TOP_K

    def out_store(tile, slot):
        return _SlabTileCopy(out_ref.at[pl.ds(tile * tm, tm)], obuf.at[slot], osems.at[slot],
                             to_slab=False)

    def issue_row(slot, r):
        for k in range(TOP_K):
            _row_copy(ys_ref, dest_ref[s * tm + k * n_tok + r], buf.at[slot, k], r,
                      sems.at[slot]).start(priority=k % 2)

    def reduce_row(slot, r):
        acc = h1_ref[r]
        for k in range(TOP_K):
            acc = acc + gate_ref[(s - 1) * tm + k * n_tok + r] * buf[slot, k, r]
        ss = jnp.sum(jnp.sum(acc * acc, axis=1, keepdims=True), axis=0, keepdims=True)
        obuf[slot, r] = acc * lax.rsqrt(ss * (1.0 / D_MODEL) + NORM_EPS) * g_ref[0]

    def step_body(gather_slot, reduce_slot):
        for r in range(tm):
            if gather_slot is not None:
                issue_row(gather_slot, r)
            if reduce_slot is not None:
                reduce_row(reduce_slot, r)
        if reduce_slot is not None:
            out_store(s - 1, reduce_slot).start()

    for parity in range(2):
        other = 1 - parity

        @pl.when(s % 2 == parity)
        def _():
            @pl.when(s < n)
            def _():
                step_body(parity, None)

            @pl.when(s >= 3)
            def _():
                out_store(s - 3, other).wait()

            @pl.when(s > 0)
            def _():
                for k in range(TOP_K):
                    pltpu.make_async_copy(ys_ref.at[pl.ds(0, tm)], buf.at[other, k],
                                          sems.at[other]).wait()
                step_body(None, other)

            if n % 2 == parity:
                @pl.when(s == n)
                def _():
                    if n >= 2:
                        out_store(n - 2, parity).wait()
                    out_store(n - 1, other).wait()


def _combine(dest, ys3, h13, gate, fg):
    T = h13.shape[0]
    tm = TM_COMBINE
    slab = (SLAB_SUBLANES, LANES)
    prev_tile = lambda s, *_: (jnp.maximum(s - 1, 0), 0, 0)
    return pl.pallas_call(
        _combine_kernel,
        grid_spec=pltpu.PrefetchScalarGridSpec(
            num_scalar_prefetch=2,
            grid=(T // tm + 1,),
            in_specs=[pl.BlockSpec(memory_space=pl.ANY),
                      pl.BlockSpec((tm,) + slab, prev_tile),
                      pl.BlockSpec((1,) + slab, lambda s, *_: (0, 0, 0))],
            out_specs=pl.BlockSpec(memory_space=pl.ANY),
            scratch_shapes=[pltpu.VMEM((2, TOP_K, tm) + slab, F32),
                            pltpu.VMEM((2, tm) + slab, F32),
                            pltpu.SemaphoreType.DMA((2,)),
                            pltpu.SemaphoreType.DMA((2,))]),
        out_shape=jax.ShapeDtypeStruct((T, D_MODEL), F32),
        compiler_params=pltpu.CompilerParams(
            dimension_semantics=("arbitrary",), vmem_limit_bytes=VMEM_LIMIT,
            has_side_effects=True),
        name="combine",
    )(dest, gate, ys3, h13, fg.reshape((1,) + slab))


def _swap_halves(w):
    h = w.shape[-1] // 2
    return jnp.concatenate([w[..., h:], w[..., :h]], axis=-1)


def _layer(x2, batch, seq, norm1_g, w_in, w_alpha_up, b_alpha, gla_norm_g, swa_sinks, swa_norm_g,
           w_out, norm2_g, w_router, b_router, w_glu, b_glu, w_lin, b_lin, w_down, b_down):
    T = x2.shape[0]
    kb_w, vb_w = w_in[:, 2064:2192], w_in[:, 2192:2320]
    w_z = w_in[:, 1536:1552]
    w_cat = jnp.concatenate([
        w_in[:, 0:1536], w_in[:, 1552:2064],
        kb_w, _swap_halves(kb_w), vb_w, _swap_halves(vb_w),
        jnp.pad(jnp.tile(w_z, (1, Z_PIECES)), [(0, 0), (0, LANES - Z_PIECES * GLA_RANK)]),
    ], axis=1).astype(BF16)
    wup_hi = w_alpha_up.astype(BF16)
    wup_lo = (w_alpha_up - wup_hi.astype(F32)).astype(BF16)
    wup_cat = jnp.pad(jnp.concatenate([wup_hi, wup_hi, wup_hi, wup_lo, wup_lo], axis=0),
                      [(0, LANES - Z_PIECES * GLA_RANK), (0, 0)])

    qk, vr, la, qb, kv = _in_proj(
        x2, norm1_g.reshape(1, -1), w_cat, wup_cat, b_alpha.reshape(1, -1))
    oa = _gla(qk, vr, la, gla_norm_g.reshape(1, -1), batch, seq)
    ob = _swa(qb, kv, swa_sinks, jnp.tile(swa_norm_g, 2).reshape(1, -1), batch, seq)

    wo = w_out.astype(BF16)
    wr_t = w_router.T
    wr_hi = wr_t.astype(BF16)
    wr_lo = (wr_t - wr_hi.astype(F32)).astype(BF16)
    h1, hn, route, cnt = _out_route(oa, ob, x2, wo, norm2_g.reshape(1, -1),
                                    wr_hi, wr_lo, b_router.reshape(-1, 1))

    tm = TM_EXPERT
    n_tiles = T * TOP_K // tm + N_EXPERTS
    counts = cnt[:, 0].astype(jnp.int32)
    padded = (counts + tm - 1) // tm * tm
    pends = jnp.cumsum(padded)
    pstarts = pends - padded
    top_idx = route[0:TOP_K].astype(jnp.int32)
    gate = route[TOP_K:2 * TOP_K].reshape(-1)
    rank = route[2 * TOP_K:3 * TOP_K].astype(jnp.int32)
    experts = jnp.arange(N_EXPERTS)[:, None, None]
    seg_start = jnp.sum(jnp.where(top_idx[None] == experts, pstarts[:, None, None], 0), axis=0)
    dest = (seg_start + rank).reshape(-1)
    n_used = (pends[-1] // tm).reshape(1)
    xs = _dispatch(dest, pstarts + counts, padded - counts, n_used, hn, n_tiles * tm)
    ys = _experts(pstarts // tm, padded // tm, xs, w_glu, b_glu, w_lin, b_lin, w_down, b_down)
    return dest, ys, h1, gate


def kernel(x, norm1_g, w_in, w_alpha_up, b_alpha, gla_norm_g, swa_sinks, swa_norm_g, w_out,
           norm2_g, w_router, b_router, w_glu, b_glu, w_lin, b_lin, w_down, b_down, final_g):
    batch, seq, d = x.shape
    assert norm1_g.shape[0] == 1, "single-layer problem"
    x2 = x.reshape(batch * seq, d)
    dest, ys, h1, gate = _layer(
        x2, batch, seq, norm1_g[0], w_in[0], w_alpha_up[0], b_alpha[0], gla_norm_g[0],
        swa_sinks[0], swa_norm_g[0], w_out[0], norm2_g[0], w_router[0], b_router[0],
        w_glu[0], b_glu[0], w_lin[0], b_lin[0], w_down[0], b_down[0])
    out = _combine(dest, ys, h1, gate, final_g.reshape(1, -1))
    return out.reshape(batch, seq, d)
```

```python
import numpy as np
import jax
import jax.numpy as jnp
from jax import lax
from jax.experimental import pallas as pl
from jax.experimental.pallas import tpu as pltpu

F32 = jnp.float32
BF16 = jnp.bfloat16

D_MODEL = 1024
GLA_HEADS = 4
GLA_DK = 64
GLA_DV = 128
GLA_RANK = 16
GLA_GATE_TAU = 16.0
GLA_CHUNK = 64
SWA_HEADS = 8
SWA_KV_HEADS = 2
SWA_HEAD_DIM = 64
SWA_WINDOW = 128
N_EXPERTS = 32
TOP_K = 4
D_FF = 1024
SWIGLU_LIMIT = 7.0
SWIGLU_ALPHA = 1.702
NORM_EPS = 1e-5

LANES = 128
SLAB_SUBLANES = D_MODEL // LANES
VMEM_LIMIT = 56 * 1024 * 1024

TM_PROJ = 1024
TL_GLA = 1024
GLA_CUM_ROWS = 256
TM_ROUTE = 1024
TM_EXPERT = 512
EXPERT_PAD = 256
TM_COMBINE = 256
SWA_SUB = 8
TM_DISPATCH = 256

NEG_BIG = -1e30
ROUTE_ROWS = 16

C_QA, C_KA, C_VA, C_RA, C_QB, C_KB, C_KBS, C_VB, C_VBS, C_Z, C_END = (
    0, 256, 512, 1024, 1536, 2048, 2176, 2304, 2432, 2560, 2688)
Z_PIECES = 5


def _split3(x):
    hi = x.astype(BF16).astype(F32)
    r = x - hi
    mid = r.astype(BF16).astype(F32)
    lo = (r - mid).astype(BF16).astype(F32)
    return hi, mid, lo


class _SlabTileCopy:
    def __init__(self, flat, slab, sem, to_slab):
        self.copies = []
        for c in range(SLAB_SUBLANES):
            pair = (flat.at[:, pl.ds(c * LANES, LANES)], slab.at[:, c, :])
            src, dst = pair if to_slab else pair[::-1]
            self.copies.append(pltpu.make_async_copy(src, dst, sem))

    def start(self):
        for cp in self.copies:
            cp.start()

    def wait(self):
        for cp in self.copies:
            cp.wait()


def _rms(x, g):
    return x * lax.rsqrt(jnp.mean(x * x, axis=-1, keepdims=True) + NORM_EPS) * g


def _in_proj_kernel(x_ref, g_ref, w_ref, wup_ref, ba_ref,
                    qk_ref, vr_ref, la_ref, qb_ref, kv_ref):
    u = _rms(x_ref[...], g_ref[...]).astype(BF16)

    def proj(c0, c1):
        return jnp.dot(u, w_ref[:, c0:c1], preferred_element_type=F32)

    qk_ref[...] = proj(C_QA, C_VA).astype(BF16)
    vr_ref[:, :C_RA - C_VA] = proj(C_VA, C_RA).astype(BF16)
    vr_ref[:, C_RA - C_VA:] = proj(C_RA, C_QB).astype(BF16)
    qb_ref[...] = proj(C_QB, C_KB).astype(BF16)
    kv_ref[...] = proj(C_KB, C_Z).astype(BF16)
    z = proj(C_Z, C_END)
    hi, mid, lo = _split3(z)
    piece = lax.broadcasted_iota(jnp.int32, z.shape, 1) // GLA_RANK
    zc = jnp.where((piece == 0) | (piece == 3), hi, jnp.where(piece == 2, lo, mid)).astype(BF16)
    y = jnp.dot(zc, wup_ref[...], preferred_element_type=F32) + ba_ref[...]
    log_sig = jnp.minimum(y, 0.0) - jnp.log1p(jnp.exp(-jnp.abs(y)))
    la_ref[...] = log_sig * (1.0 / GLA_GATE_TAU)


def _in_proj(x2, g1, w_cat, wup_p, ba_p):
    T = x2.shape[0]
    tm = TM_PROJ
    row = lambda w: pl.BlockSpec((tm, w), lambda i: (i, 0))
    full = lambda a: pl.BlockSpec(a.shape, lambda i: (0,) * a.ndim)
    outs = [(512, BF16), (1024, BF16), (256, F32), (512, BF16), (512, BF16)]
    return pl.pallas_call(
        _in_proj_kernel,
        grid=(T // tm,),
        in_specs=[row(D_MODEL), full(g1), full(w_cat), full(wup_p), full(ba_p)],
        out_specs=[row(w) for w, _ in outs],
        out_shape=[jax.ShapeDtypeStruct((T, w), dt) for w, dt in outs],
        compiler_params=pltpu.CompilerParams(
            dimension_semantics=("arbitrary",), vmem_limit_bytes=VMEM_LIMIT),
        name="in_proj",
    )(x2, g1, w_cat, wup_p, ba_p)


def _gla_kernel(qk_ref, vr_ref, la_ref, g_ref, cum_ref, o_ref, st_ref):
    @pl.when(pl.program_id(1) == 0)
    def _():
        st_ref[...] = jnp.zeros_like(st_ref)

    tl = TL_GLA
    c = GLA_CHUNK
    kw = GLA_HEADS * GLA_DK
    vw = GLA_HEADS * GLA_DV
    causal = (lax.broadcasted_iota(jnp.int32, (c, c), 0)
              >= lax.broadcasted_iota(jnp.int32, (c, c), 1))
    low_half = lax.broadcasted_iota(jnp.int32, (c, LANES), 1) < GLA_DK
    g = g_ref[...]
    b_groups = []
    for grp in range(tl // GLA_CUM_ROWS):
        la = la_ref[grp * GLA_CUM_ROWS:(grp + 1) * GLA_CUM_ROWS, :]
        pieces = jnp.concatenate([p.astype(BF16) for p in _split3(la)], axis=1)
        b3 = jnp.dot(cum_ref[...], pieces, preferred_element_type=F32)
        b_groups.append(b3[:, :kw] + b3[:, kw:2 * kw] + b3[:, 2 * kw:])
    for ch in range(tl // c):
        rows = slice(ch * c, (ch + 1) * c)
        in_grp = (ch * c) % GLA_CUM_ROWS
        b = b_groups[(ch * c) // GLA_CUM_ROWS][in_grp:in_grp + c]
        b_last = b[c - 1:c]
        qf = qk_ref[rows, :kw].astype(F32)
        kf = qk_ref[rows, kw:].astype(F32)
        q_e = (qf * jnp.exp(b) * (GLA_DK ** -0.5)).astype(BF16)
        k_e = (kf * jnp.exp(-b)).astype(BF16)
        k_t = (kf * jnp.exp(b_last - b)).astype(BF16)
        decay = jnp.exp(b_last)
        for h in range(GLA_HEADS):
            ps = slice((h // 2) * LANES, (h // 2 + 1) * LANES)
            mine = low_half if h % 2 == 0 else ~low_half
            qp, kp = q_e[:, ps], k_e[:, ps]
            qh = jnp.where(mine, qp, jnp.zeros_like(qp))
            kth = jnp.where(mine, k_t[:, ps], jnp.zeros_like(qp))
            vs = slice(h * GLA_DV, (h + 1) * GLA_DV)
            vh = vr_ref[rows, vs]
            a = pl.dot(qh, kp, trans_b=True)
            a = jnp.where(causal, a, 0.0).astype(BF16)
            st = st_ref[h]
            o = (jnp.dot(a, vh, preferred_element_type=F32)
                 + pl.dot(qh, st.astype(BF16), trans_b=True))
            st_ref[h] = st * decay[:, ps] + pl.dot(vh, kth, trans_a=True)
            rh = vr_ref[rows, slice(vw + h * GLA_DV, vw + (h + 1) * GLA_DV)].astype(F32)
            o = _rms(o, g) * (rh * jax.nn.sigmoid(rh))
            o_ref[rows, vs] = o.astype(BF16)


def _gla(qk, vr, la, g, batch, seq):
    tl = TL_GLA
    nl = seq // tl
    row = lambda w: pl.BlockSpec((tl, w), lambda b, i: (b * nl + i, 0))
    r = np.arange(GLA_CUM_ROWS)
    cum = jnp.asarray((r[None, :] <= r[:, None])
                      & (r[None, :] // GLA_CHUNK == r[:, None] // GLA_CHUNK), BF16)
    return pl.pallas_call(
        _gla_kernel,
        grid=(batch, nl),
        in_specs=[row(512), row(1024), row(256),
                  pl.BlockSpec((1, GLA_DV), lambda b, i: (0, 0)),
                  pl.BlockSpec((GLA_CUM_ROWS, GLA_CUM_ROWS), lambda b, i: (0, 0))],
        out_specs=row(512),
        out_shape=jax.ShapeDtypeStruct((batch * seq, 512), BF16),
        scratch_shapes=[pltpu.VMEM((GLA_HEADS, GLA_DV, LANES), F32)],
        compiler_params=pltpu.CompilerParams(
            dimension_semantics=("arbitrary", "arbitrary"), vmem_limit_bytes=VMEM_LIMIT),
        name="gla",
    )(qk, vr, la, g, cum)


def _swa_kernel(sink_ref, bias0_ref, bias_ref, q_ref, kvc_ref, kvp_ref, g_ref, o_ref):
    w = SWA_WINDOW
    scale = jnp.asarray(SWA_HEAD_DIM ** -0.5, BF16)
    kv_lane = lax.broadcasted_iota(jnp.int32, (2 * w, LANES), 1)
    lane_lo = kv_lane < SWA_HEAD_DIM
    out_lo = lax.broadcasted_iota(jnp.int32, (w, LANES), 1) < SWA_HEAD_DIM
    ones_hi = jnp.where(kv_lane == SWA_HEAD_DIM, 1.0, 0.0).astype(BF16)
    ones_lo = jnp.where(kv_lane == 0, 1.0, 0.0).astype(BF16)
    g = g_ref[...]

    for sb in range(SWA_SUB):
        rows = slice(sb * w, (sb + 1) * w)
        if sb == 0:
            kv = jnp.concatenate([kvp_ref[...], kvc_ref[0:w, :]], axis=0)
            bias = bias0_ref
        else:
            kv = kvc_ref[(sb - 1) * w:(sb + 1) * w, :]
            bias = bias_ref
        k = kv[:, 0:LANES] * scale
        ks = kv[:, LANES:2 * LANES] * scale
        v, vs = kv[:, 2 * LANES:3 * LANES], kv[:, 3 * LANES:]
        zero = jnp.zeros_like(k)
        k_low = [jnp.where(lane_lo, k, zero), jnp.where(lane_lo, ks, zero)]
        k_high = [jnp.where(lane_lo, zero, ks), jnp.where(lane_lo, zero, k)]
        v_low = [jnp.where(lane_lo, v, zero) + ones_hi, jnp.where(lane_lo, vs, zero) + ones_hi]
        v_high = [jnp.where(lane_lo, zero, vs) + ones_lo, jnp.where(lane_lo, zero, v) + ones_lo]

        def weights(s, head):
            s = s + bias[0, head]
            sink = sink_ref[head]
            m = jnp.maximum(jnp.max(s, axis=-1, keepdims=True), sink)
            return jnp.exp(s - m).astype(BF16), jnp.exp(sink - m)

        for pair in range(SWA_HEADS // 2):
            j = (2 * pair) // (SWA_HEADS // SWA_KV_HEADS)
            cols = slice(pair * LANES, (pair + 1) * LANES)
            qp = q_ref[rows, cols]
            e0, sink0 = weights(pl.dot(qp, k_low[j], trans_b=True), 2 * pair)
            e1, sink1 = weights(pl.dot(qp, k_high[j], trans_b=True), 2 * pair + 1)
            oa = jnp.dot(e0, v_low[j], preferred_element_type=F32)
            ob = jnp.dot(e1, v_high[j], preferred_element_type=F32)
            inv0 = 1.0 / (oa[:, SWA_HEAD_DIM:SWA_HEAD_DIM + 1] + sink0)
            inv1 = 1.0 / (ob[:, 0:1] + sink1)
            o = jnp.where(out_lo, oa * inv0, ob * inv1)
            sq = o * o
            ms_lo = jnp.sum(jnp.where(out_lo, sq, 0.0), axis=-1, keepdims=True)
            ms_hi = jnp.sum(jnp.where(out_lo, 0.0, sq), axis=-1, keepdims=True)
            ms = jnp.where(out_lo, ms_lo, ms_hi) * (1.0 / SWA_HEAD_DIM)
            o = o * lax.rsqrt(ms + NORM_EPS) * g
            o_ref[rows, cols] = o.astype(BF16)


def _swa_bias():
    w = SWA_WINDOW
    slopes = 2.0 ** (-8.0 * np.arange(1, SWA_HEADS + 1, dtype=np.float64) / SWA_HEADS)
    rel = np.arange(w)[:, None] + w - np.arange(2 * w)[None, :]
    in_window = (rel >= 0) & (rel < w)
    exists = np.stack([np.arange(2 * w) >= w, np.ones(2 * w, bool)])
    valid = in_window[None] & exists[:, None, :]
    bias = -slopes[None, :, None, None] * rel[None, None].astype(np.float64)
    return np.where(valid[:, None], bias, NEG_BIG).astype(np.float32)


def _swa(qb, kv, sinks, g2, batch, seq):
    w = SWA_WINDOW
    rows = SWA_SUB * w
    nb = seq // rows
    cur = lambda: pl.BlockSpec((rows, 512), lambda b, n: (b * nb + n, 0))
    prev = pl.BlockSpec(
        (w, 512), lambda b, n: (jnp.maximum((b * nb + n) * SWA_SUB - 1, b * nb * SWA_SUB), 0))
    table = lambda index: pl.BlockSpec((1, SWA_HEADS, w, 2 * w), index)
    bias = jnp.asarray(_swa_bias())
    return pl.pallas_call(
        _swa_kernel,
        grid=(batch, nb),
        in_specs=[pl.BlockSpec(memory_space=pltpu.SMEM),
                  table(lambda b, n: (jnp.minimum(n, 1), 0, 0, 0)),
                  table(lambda b, n: (1, 0, 0, 0)),
                  cur(), cur(), prev,
                  pl.BlockSpec((1, LANES), lambda b, n: (0, 0))],
        out_specs=cur(),
        out_shape=jax.ShapeDtypeStruct((batch * seq, 512), BF16),
        compiler_params=pltpu.CompilerParams(
            dimension_semantics=("arbitrary", "arbitrary"), vmem_limit_bytes=VMEM_LIMIT),
        name="swa",
    )(sinks, bias, bias, qb, kv, kv, g2)


def _out_route_kernel(oa_ref, ob_ref, x_ref, wo_ref, g_ref, wrh_ref, wrl_ref, br_ref,
                      earlier_ref, h13_ref, hn3_ref, route_ref, cnt_ref, hbuf, hsem):
    i = pl.program_id(0)
    tm = TM_ROUTE

    @pl.when(i == 0)
    def _():
        cnt_ref[...] = jnp.zeros_like(cnt_ref)

    def stores(step, slot):
        rows = pl.ds(step * tm, tm)
        return [_SlabTileCopy(hbuf.at[slot, which], out.at[rows], hsem.at[slot], to_slab=True)
                for which, out in enumerate((h13_ref, hn3_ref))]

    slot = i % 2

    @pl.when(i >= 2)
    def _():
        for st in stores(i - 2, slot):
            st.wait()

    h1 = (x_ref[...]
          + jnp.dot(oa_ref[...], wo_ref[:oa_ref.shape[1], :], preferred_element_type=F32)
          + jnp.dot(ob_ref[...], wo_ref[oa_ref.shape[1]:, :], preferred_element_type=F32))
    hn = _rms(h1, g_ref[...])
    hbuf[slot, 0] = h1
    hbuf[slot, 1] = hn
    for st in stores(i, slot):
        st.start()

    @pl.when(i == pl.num_programs(0) - 1)
    def _():
        @pl.when(i >= 1)
        def _():
            for st in stores(i - 1, 1 - slot):
                st.wait()
        for st in stores(i, slot):
            st.wait()

    hn_hi = hn.astype(BF16)
    hn_lo = (hn - hn_hi.astype(F32)).astype(BF16)
    logits = (pl.dot(wrh_ref[...], hn_hi, trans_b=True)
              + pl.dot(wrh_ref[...], hn_lo, trans_b=True)
              + pl.dot(wrl_ref[...], hn_hi, trans_b=True)) + br_ref[...]

    eid = lax.broadcasted_iota(jnp.int32, (N_EXPERTS, tm), 0)
    work = logits
    vals, idxs, sels = [], [], []
    for _ in range(TOP_K):
        m = jnp.max(work, axis=0, keepdims=True)
        idx = jnp.min(jnp.where(work == m, eid, N_EXPERTS), axis=0, keepdims=True)
        sel = eid == idx
        vals.append(m)
        idxs.append(idx)
        sels.append(sel)
        work = jnp.where(sel, -3e38, work)
    exps = [jnp.exp(v - vals[0]) for v in vals]
    inv_den = 1.0 / (exps[0] + exps[1] + exps[2] + exps[3])

    multihot = jnp.where(sels[0] | sels[1] | sels[2] | sels[3], 1.0, 0.0)
    before = (jnp.dot(multihot.astype(BF16), earlier_ref[...], preferred_element_type=F32)
              + cnt_ref[:, 0:1])
    ranks = [jnp.sum(jnp.where(s, before, 0.0), axis=0, keepdims=True) for s in sels]
    route_ref[...] = jnp.concatenate(
        [i.astype(F32) for i in idxs] + [e * inv_den for e in exps] + ranks
        + [jnp.zeros((ROUTE_ROWS - 3 * TOP_K, tm), F32)], axis=0)
    cnt_ref[...] += jnp.sum(multihot, axis=1, keepdims=True)


def _out_route(oa, ob, x2, wo, g2, wr_hi, wr_lo, br):
    T = x2.shape[0]
    tm = TM_ROUTE
    row = lambda w: pl.BlockSpec((tm, w), lambda i: (i, 0))
    full = lambda a: pl.BlockSpec(a.shape, lambda i: (0,) * a.ndim)
    earlier = jnp.asarray(np.triu(np.ones((tm, tm), np.float32), k=1), BF16)
    return pl.pallas_call(
        _out_route_kernel,
        grid=(T // tm,),
        in_specs=[row(512), row(512), row(D_MODEL), full(wo), full(g2),
                  full(wr_hi), full(wr_lo), full(br), full(earlier)],
        out_specs=[pl.BlockSpec(memory_space=pl.ANY), pl.BlockSpec(memory_space=pl.ANY),
                   pl.BlockSpec((ROUTE_ROWS, tm), lambda i: (0, i)),
                   pl.BlockSpec((N_EXPERTS, LANES), lambda i: (0, 0))],
        out_shape=[jax.ShapeDtypeStruct((T, SLAB_SUBLANES, LANES), F32),
                   jax.ShapeDtypeStruct((T, SLAB_SUBLANES, LANES), F32),
                   jax.ShapeDtypeStruct((ROUTE_ROWS, T), F32),
                   jax.ShapeDtypeStruct((N_EXPERTS, LANES), F32)],
        scratch_shapes=[pltpu.VMEM((2, 2, tm, D_MODEL), F32), pltpu.SemaphoreType.DMA((2,))],
        compiler_params=pltpu.CompilerParams(
            dimension_semantics=("arbitrary",), vmem_limit_bytes=VMEM_LIMIT,
            has_side_effects=True),
        name="out_route",
    )(oa, ob, x2, wo, g2, wr_hi, wr_lo, br, earlier)


def _row_copy(src, s, dst, d, sem):
    return pltpu.make_async_copy(src.at[pl.ds(s, 1)], dst.at[pl.ds(d, 1)], sem)


def _dispatch_kernel(dest_ref, pad_start_ref, pad_cnt_ref, nused_ref, hn_ref, xs_ref,
                     zero_ref, sem, zsem):
    i = pl.program_id(0)
    n = pl.num_programs(0)
    tm = EXPERT_PAD
    n_tok = dest_ref.shape[0] // TOP_K
    base = i * TM_DISPATCH
    for p in range(TM_DISPATCH * TOP_K):
        r, k = divmod(p, TOP_K)
        _row_copy(hn_ref, r, xs_ref, dest_ref[base + k * n_tok + r], sem).start(priority=p % 2)
    for _ in range(TOP_K):
        pltpu.make_async_copy(hn_ref, xs_ref.at[pl.ds(0, TM_DISPATCH)], sem).wait()

    @pl.when(i == n - 1)
    def _():
        zero_ref[...] = jnp.zeros_like(zero_ref)
        sub = 8
        for e in range(N_EXPERTS):
            start, cnt = pad_start_ref[e], pad_cnt_ref[e]
            head = cnt & (sub - 1)
            body0 = start + head
            nbody = cnt // sub

            def head_copy(r):
                return _row_copy(zero_ref, 0, xs_ref, start + r, zsem)

            def body_copy(c):
                return pltpu.make_async_copy(zero_ref.at[pl.ds(0, sub)],
                                             xs_ref.at[pl.ds(body0 + c * sub, sub)], zsem)

            for copy, count in ((head_copy, head), (body_copy, nbody)):
                def zissue(r, carry, copy=copy):
                    copy(r).start()
                    return carry

                def zwait(r, carry, copy=copy):
                    copy(r).wait()
                    return carry

                lax.fori_loop(0, count, zissue, 0)
                lax.fori_loop(0, count, zwait, 0)

        def tail_copy(t):
            return pltpu.make_async_copy(zero_ref, xs_ref.at[pl.ds(t * tm, tm)], zsem)

        def tissue(t, carry):
            tail_copy(t).start()
            return carry

        def twait(t, carry):
            tail_copy(t).wait()
            return carry

        n_tiles = xs_ref.shape[0] // tm
        lax.fori_loop(nused_ref[0], n_tiles, tissue, 0)
        lax.fori_loop(nused_ref[0], n_tiles, twait, 0)


def _dispatch(dest, pad_start, pad_cnt, n_used, hn, n_rows):
    T = hn.shape[0]
    any_spec = pl.BlockSpec(memory_space=pl.ANY)
    return pl.pallas_call(
        _dispatch_kernel,
        grid_spec=pltpu.PrefetchScalarGridSpec(
            num_scalar_prefetch=4,
            grid=(T // TM_DISPATCH,),
            in_specs=[pl.BlockSpec((TM_DISPATCH, SLAB_SUBLANES, LANES), lambda i, *_: (i, 0, 0))],
            out_specs=any_spec,
            scratch_shapes=[pltpu.VMEM((EXPERT_PAD, SLAB_SUBLANES, LANES), F32),
                            pltpu.SemaphoreType.DMA(()),
                            pltpu.SemaphoreType.DMA(())]),
        out_shape=jax.ShapeDtypeStruct((n_rows, SLAB_SUBLANES, LANES), F32),
        compiler_params=pltpu.CompilerParams(
            dimension_semantics=("arbitrary",), has_side_effects=True,
            vmem_limit_bytes=VMEM_LIMIT),
        name="dispatch",
    )(dest, pad_start, pad_cnt, n_used, hn)


def _experts_kernel(row0_ref, nfull_ref, tail_ref, xs_ref, wg_ref, bg_ref, wl_ref, bl_ref, wd_ref,
                    bd_ref, ys_ref, wg_bf, wl_bf, wd_bf, xbuf, ybuf, xsem, ysem):
    e = pl.program_id(0)
    tm, half = TM_EXPERT, EXPERT_PAD
    r0, nf, tail = row0_ref[e], nfull_ref[e], tail_ref[e]
    nt = nf + tail

    def x_load(row, slot, rows):
        return _SlabTileCopy(xbuf.at[slot, pl.ds(0, rows)], xs_ref.at[pl.ds(row, rows)],
                             xsem.at[slot], to_slab=False)

    def y_store(row, slot, rows):
        return _SlabTileCopy(ybuf.at[slot, pl.ds(0, rows)], ys_ref.at[pl.ds(row, rows)],
                             ysem.at[slot], to_slab=True)

    def start_first_load(expert):
        @pl.when(nfull_ref[expert] > 0)
        def _():
            x_load(row0_ref[expert], 0, tm).start()

        @pl.when((nfull_ref[expert] == 0) & (tail_ref[expert] > 0))
        def _():
            x_load(row0_ref[expert], 0, half).start()

    def compute(slot, rows):
        x = xbuf[slot, :rows].astype(BF16)
        glu = jnp.minimum(jnp.dot(x, wg_bf[...], preferred_element_type=F32) + bg_ref[0],
                          SWIGLU_LIMIT)
        lin = jnp.clip(jnp.dot(x, wl_bf[...], preferred_element_type=F32) + bl_ref[0],
                       -SWIGLU_LIMIT, SWIGLU_LIMIT)
        hid = glu * jax.nn.sigmoid(SWIGLU_ALPHA * glu) * (lin + 1.0)
        ybuf[slot, :rows] = (jnp.dot(hid.astype(BF16), wd_bf[...], preferred_element_type=F32)
                             + bd_ref[0])

    @pl.when(e == 0)
    def _():
        start_first_load(0)

    @pl.when(nt > 0)
    def _():
        wg_bf[...] = wg_ref[0].astype(BF16)
        wl_bf[...] = wl_ref[0].astype(BF16)
        wd_bf[...] = wd_ref[0].astype(BF16)

    def full_tile(j, carry):
        slot = j % 2

        @pl.when(j + 1 < nf)
        def _():
            x_load(r0 + (j + 1) * tm, 1 - slot, tm).start()

        @pl.when((j + 1 == nf) & (tail > 0))
        def _():
            x_load(r0 + (j + 1) * tm, 1 - slot, half).start()

        x_load(r0 + j * tm, slot, tm).wait()

        @pl.when(j >= 2)
        def _():
            y_store(r0 + (j - 2) * tm, slot, tm).wait()

        compute(slot, tm)
        y_store(r0 + j * tm, slot, tm).start()
        return carry

    lax.fori_loop(0, nf, full_tile, 0)

    @pl.when(tail > 0)
    def _():
        slot = nf % 2
        x_load(r0 + nf * tm, slot, half).wait()

        @pl.when(nf >= 2)
        def _():
            y_store(r0 + (nf - 2) * tm, slot, tm).wait()

        compute(slot, half)
        y_store(r0 + nf * tm, slot, half).start()

    @pl.when(e + 1 < pl.num_programs(0))
    def _():
        start_first_load(e + 1)

    @pl.when(nt >= 2)
    def _():
        y_store(r0 + (nt - 2) * tm, nt % 2, tm).wait()

    @pl.when((nt >= 1) & (tail == 0))
    def _():
        y_store(r0 + (nt - 1) * tm, (nt - 1) % 2, tm).wait()

    @pl.when(tail > 0)
    def _():
        y_store(r0 + nf * tm, nf % 2, half).wait()

    @pl.when(e == pl.num_programs(0) - 1)
    def _():
        ybuf[0] = jnp.zeros((tm, D_MODEL), F32)
        first = (r0 + nf * tm + tail * half) // half
        n_pieces = ys_ref.shape[0] // half

        def fill_issue(t, carry):
            y_store(t * half, 0, half).start()
            return carry

        def fill_wait(t, carry):
            y_store(t * half, 0, half).wait()
            return carry

        lax.fori_loop(first, n_pieces, fill_issue, 0)
        lax.fori_loop(first, n_pieces, fill_wait, 0)


def _experts(row0, nfull, tail, xs, w_glu, b_glu, w_lin, b_lin, w_down, b_down):
    tm = TM_EXPERT
    any_spec = pl.BlockSpec(memory_space=pl.ANY)
    wspec = lambda: pl.BlockSpec((1, D_MODEL, D_FF), lambda e, *_: (e, 0, 0))
    bspec = lambda: pl.BlockSpec((1, 1, D_FF), lambda e, *_: (e, 0, 0))
    return pl.pallas_call(
        _experts_kernel,
        grid_spec=pltpu.PrefetchScalarGridSpec(
            num_scalar_prefetch=3,
            grid=(N_EXPERTS,),
            in_specs=[any_spec, wspec(), bspec(), wspec(), bspec(), wspec(), bspec()],
            out_specs=any_spec,
            scratch_shapes=[pltpu.VMEM((D_MODEL, D_FF), BF16)] * 3
            + [pltpu.VMEM((2, tm, D_MODEL), F32), pltpu.VMEM((2, tm, D_MODEL), F32),
               pltpu.SemaphoreType.DMA((2,)), pltpu.SemaphoreType.DMA((2,))]),
        out_shape=jax.ShapeDtypeStruct(xs.shape, F32),
        compiler_params=pltpu.CompilerParams(
            dimension_semantics=("arbitrary",), vmem_limit_bytes=VMEM_LIMIT,
            has_side_effects=True),
        name="experts",
    )(row0, nfull, tail, xs, w_glu, b_glu.reshape(N_EXPERTS, 1, D_FF),
      w_lin, b_lin.reshape(N_EXPERTS, 1, D_FF), w_down, b_down.reshape(N_EXPERTS, 1, D_MODEL))


def _combine_kernel(dest_ref, gate_ref, ys_ref, h1_ref, g_ref, out_ref, buf, obuf, sems, osems):
    s = pl.program_id(0)
    tm = TM_COMBINE
    n = out_ref.shape[0] // tm
    n_tok = dest_ref.shape[0] // TOP_K

    def out_store(tile, slot):
        return _SlabTileCopy(out_ref.at[pl.ds(tile * tm, tm)], obuf.at[slot], osems.at[slot],
                             to_slab=False)

    def issue_row(slot, r):
        for k in range(TOP_K):
            _row_copy(ys_ref, dest_ref[s * tm + k * n_tok + r], buf.at[slot, k], r,
                      sems.at[slot]).start(priority=k % 2)

    def reduce_row(slot, r):
        acc = h1_ref[r]
        for k in range(TOP_K):
            acc = acc + gate_ref[(s - 1) * tm + k * n_tok + r] * buf[slot, k, r]
        ss = jnp.sum(jnp.sum(acc * acc, axis=1, keepdims=True), axis=0, keepdims=True)
        obuf[slot, r] = acc * lax.rsqrt(ss * (1.0 / D_MODEL) + NORM_EPS) * g_ref[0]

    def step_body(gather_slot, reduce_slot):
        for r in range(tm):
            if gather_slot is not None:
                issue_row(gather_slot, r)
            if reduce_slot is not None:
                reduce_row(reduce_slot, r)
        if reduce_slot is not None:
            out_store(s - 1, reduce_slot).start()

    for parity in range(2):
        other = 1 - parity

        @pl.when(s % 2 == parity)
        def _():
            @pl.when(s < n)
            def _():
                step_body(parity, None)

            @pl.when(s >= 3)
            def _():
                out_store(s - 3, other).wait()

            @pl.when(s > 0)
            def _():
                for k in range(TOP_K):
                    pltpu.make_async_copy(ys_ref.at[pl.ds(0, tm)], buf.at[other, k],
                                          sems.at[other]).wait()
                step_body(None, other)

            if n % 2 == parity:
                @pl.when(s == n)
                def _():
                    if n >= 2:
                        out_store(n - 2, parity).wait()
                    out_store(n - 1, other).wait()


def _combine(dest, ys3, h13, gate, fg):
    T = h13.shape[0]
    tm = TM_COMBINE
    slab = (SLAB_SUBLANES, LANES)
    prev_tile = lambda s, *_: (jnp.maximum(s - 1, 0), 0, 0)
    return pl.pallas_call(
        _combine_kernel,
        grid_spec=pltpu.PrefetchScalarGridSpec(
            num_scalar_prefetch=2,
            grid=(T // tm + 1,),
            in_specs=[pl.BlockSpec(memory_space=pl.ANY),
                      pl.BlockSpec((tm,) + slab, prev_tile),
                      pl.BlockSpec((1,) + slab, lambda s, *_: (0, 0, 0))],
            out_specs=pl.BlockSpec(memory_space=pl.ANY),
            scratch_shapes=[pltpu.VMEM((2, TOP_K, tm) + slab, F32),
                            pltpu.VMEM((2, tm) + slab, F32),
                            pltpu.SemaphoreType.DMA((2,)),
                            pltpu.SemaphoreType.DMA((2,))]),
        out_shape=jax.ShapeDtypeStruct((T, D_MODEL), F32),
        compiler_params=pltpu.CompilerParams(
            dimension_semantics=("arbitrary",), vmem_limit_bytes=VMEM_LIMIT,
            has_side_effects=True),
        name="combine",
    )(dest, gate, ys3, h13, fg.reshape((1,) + slab))


def _swap_halves(w):
    h = w.shape[-1] // 2
    return jnp.concatenate([w[..., h:], w[..., :h]], axis=-1)


def _layer(x2, batch, seq, norm1_g, w_in, w_alpha_up, b_alpha, gla_norm_g, swa_sinks, swa_norm_g,
           w_out, norm2_g, w_router, b_router, w_glu, b_glu, w_lin, b_lin, w_down, b_down):
    T = x2.shape[0]
    kb_w, vb_w = w_in[:, 2064:2192], w_in[:, 2192:2320]
    w_z = w_in[:, 1536:1552]
    w_cat = jnp.concatenate([
        w_in[:, 0:1536], w_in[:, 1552:2064],
        kb_w, _swap_halves(kb_w), vb_w, _swap_halves(vb_w),
        jnp.pad(jnp.tile(w_z, (1, Z_PIECES)), [(0, 0), (0, LANES - Z_PIECES * GLA_RANK)]),
    ], axis=1).astype(BF16)
    wup_hi = w_alpha_up.astype(BF16)
    wup_lo = (w_alpha_up - wup_hi.astype(F32)).astype(BF16)
    wup_cat = jnp.pad(jnp.concatenate([wup_hi, wup_hi, wup_hi, wup_lo, wup_lo], axis=0),
                      [(0, LANES - Z_PIECES * GLA_RANK), (0, 0)])

    qk, vr, la, qb, kv = _in_proj(
        x2, norm1_g.reshape(1, -1), w_cat, wup_cat, b_alpha.reshape(1, -1))
    oa = _gla(qk, vr, la, gla_norm_g.reshape(1, -1), batch, seq)
    ob = _swa(qb, kv, swa_sinks, jnp.tile(swa_norm_g, 2).reshape(1, -1), batch, seq)

    wo = w_out.astype(BF16)
    wr_t = w_router.T
    wr_hi = wr_t.astype(BF16)
    wr_lo = (wr_t - wr_hi.astype(F32)).astype(BF16)
    h1, hn, route, cnt = _out_route(oa, ob, x2, wo, norm2_g.reshape(1, -1),
                                    wr_hi, wr_lo, b_router.reshape(-1, 1))

    tm, pad = TM_EXPERT, EXPERT_PAD
    n_rows = T * TOP_K + N_EXPERTS * pad
    counts = cnt[:, 0].astype(jnp.int32)
    padded = (counts + pad - 1) // pad * pad
    pends = jnp.cumsum(padded)
    pstarts = pends - padded
    top_idx = route[0:TOP_K].astype(jnp.int32)
    gate = route[TOP_K:2 * TOP_K].reshape(-1)
    rank = route[2 * TOP_K:3 * TOP_K].astype(jnp.int32)
    experts = jnp.arange(N_EXPERTS)[:, None, None]
    seg_start = jnp.sum(jnp.where(top_idx[None] == experts, pstarts[:, None, None], 0), axis=0)
    dest = (seg_start + rank).reshape(-1)
    n_used = (pends[-1] // pad).reshape(1)
    xs = _dispatch(dest, pstarts + counts, padded - counts, n_used, hn, n_rows)
    ys = _experts(pstarts, padded // tm, padded % tm // pad, xs,
                  w_glu, b_glu, w_lin, b_lin, w_down, b_down)
    return dest, ys, h1, gate


def kernel(x, norm1_g, w_in, w_alpha_up, b_alpha, gla_norm_g, swa_sinks, swa_norm_g, w_out,
           norm2_g, w_router, b_router, w_glu, b_glu, w_lin, b_lin, w_down, b_down, final_g):
    batch, seq, d = x.shape
    assert norm1_g.shape[0] == 1, "single-layer problem"
    x2 = x.reshape(batch * seq, d)
    dest, ys, h1, gate = _layer(
        x2, batch, seq, norm1_g[0], w_in[0], w_alpha_up[0], b_alpha[0], gla_norm_g[0],
        swa_sinks[0], swa_norm_g[0], w_out[0], norm2_g[0], w_router[0], b_router[0],
        w_glu[0], b_glu[0], w_lin[0], b_lin[0], w_down[0], b_down[0])
    out = _combine(dest, ys, h1, gate, final_g.reshape(1, -1))
    return out.reshape(batch, seq, d)
```

```python
import numpy as np
import jax
import jax.numpy as jnp
from jax import lax
from jax.experimental import pallas as pl
from jax.experimental.pallas import tpu as pltpu

F32 = jnp.float32
BF16 = jnp.bfloat16

D_MODEL = 1024
GLA_HEADS = 4
GLA_DK = 64
GLA_DV = 128
GLA_RANK = 16
GLA_GATE_TAU = 16.0
GLA_CHUNK = 64
SWA_HEADS = 8
SWA_KV_HEADS = 2
SWA_HEAD_DIM = 64
SWA_WINDOW = 128
N_EXPERTS = 32
TOP_K = 4
D_FF = 1024
SWIGLU_LIMIT = 7.0
SWIGLU_ALPHA = 1.702
NORM_EPS = 1e-5

LANES = 128
SLAB_SUBLANES = D_MODEL // LANES
VMEM_LIMIT = 56 * 1024 * 1024

TM_PROJ = 1024
TL_GLA = 1024
GLA_CUM_ROWS = 256
TM_ROUTE = 1024
TM_EXPERT = 512
EXPERT_PAD = 256
TM_COMBINE = 256
SWA_SUB = 8
TM_DISPATCH = 256

NEG_BIG = -1e30
ROUTE_ROWS = 16

C_QA, C_KA, C_VA, C_RA, C_QB, C_KB, C_KBS, C_VB, C_VBS, C_Z, C_END = (
    0, 256, 512, 1024, 1536, 2048, 2176, 2304, 2432, 2560, 2688)
Z_PIECES = 5


def _split3(x):
    hi = x.astype(BF16).astype(F32)
    r = x - hi
    mid = r.astype(BF16).astype(F32)
    lo = (r - mid).astype(BF16).astype(F32)
    return hi, mid, lo


class _SlabTileCopy:
    def __init__(self, flat, slab, sem, to_slab):
        self.copies = []
        for c in range(SLAB_SUBLANES):
            pair = (flat.at[:, pl.ds(c * LANES, LANES)], slab.at[:, c, :])
            src, dst = pair if to_slab else pair[::-1]
            self.copies.append(pltpu.make_async_copy(src, dst, sem))

    def start(self):
        for cp in self.copies:
            cp.start()

    def wait(self):
        for cp in self.copies:
            cp.wait()


def _rms(x, g):
    return x * lax.rsqrt(jnp.mean(x * x, axis=-1, keepdims=True) + NORM_EPS) * g


def _in_proj_kernel(x_ref, g_ref, w_ref, wup_ref, ba_ref,
                    qk_ref, vr_ref, la_ref, qb_ref, kv_ref):
    u = _rms(x_ref[...], g_ref[...]).astype(BF16)

    def proj(c0, c1):
        return jnp.dot(u, w_ref[:, c0:c1], preferred_element_type=F32)

    qk_ref[...] = proj(C_QA, C_VA).astype(BF16)
    vr_ref[:, :C_RA - C_VA] = proj(C_VA, C_RA).astype(BF16)
    vr_ref[:, C_RA - C_VA:] = proj(C_RA, C_QB).astype(BF16)
    qb_ref[...] = proj(C_QB, C_KB).astype(BF16)
    kv_ref[...] = proj(C_KB, C_Z).astype(BF16)
    z = proj(C_Z, C_END)
    hi, mid, lo = _split3(z)
    piece = lax.broadcasted_iota(jnp.int32, z.shape, 1) // GLA_RANK
    zc = jnp.where((piece == 0) | (piece == 3), hi, jnp.where(piece == 2, lo, mid)).astype(BF16)
    y = jnp.dot(zc, wup_ref[...], preferred_element_type=F32) + ba_ref[...]
    log_sig = jnp.minimum(y, 0.0) - jnp.log1p(jnp.exp(-jnp.abs(y)))
    la_ref[...] = log_sig * (1.0 / GLA_GATE_TAU)


def _in_proj(x2, g1, w_cat, wup_p, ba_p):
    T = x2.shape[0]
    tm = TM_PROJ
    row = lambda w: pl.BlockSpec((tm, w), lambda i: (i, 0))
    full = lambda a: pl.BlockSpec(a.shape, lambda i: (0,) * a.ndim)
    outs = [(512, BF16), (1024, BF16), (256, F32), (512, BF16), (512, BF16)]
    return pl.pallas_call(
        _in_proj_kernel,
        grid=(T // tm,),
        in_specs=[row(D_MODEL), full(g1), full(w_cat), full(wup_p), full(ba_p)],
        out_specs=[row(w) for w, _ in outs],
        out_shape=[jax.ShapeDtypeStruct((T, w), dt) for w, dt in outs],
        compiler_params=pltpu.CompilerParams(
            dimension_semantics=("arbitrary",), vmem_limit_bytes=VMEM_LIMIT),
        name="in_proj",
    )(x2, g1, w_cat, wup_p, ba_p)


def _gla_kernel(qk_ref, vr_ref, la_ref, g_ref, cum_ref, o_ref, st_ref):
    @pl.when(pl.program_id(1) == 0)
    def _():
        st_ref[...] = jnp.zeros_like(st_ref)

    tl = TL_GLA
    c = GLA_CHUNK
    kw = GLA_HEADS * GLA_DK
    vw = GLA_HEADS * GLA_DV
    causal = (lax.broadcasted_iota(jnp.int32, (c, c), 0)
              >= lax.broadcasted_iota(jnp.int32, (c, c), 1))
    low_half = lax.broadcasted_iota(jnp.int32, (c, LANES), 1) < GLA_DK
    g = g_ref[...]
    b_groups = []
    for grp in range(tl // GLA_CUM_ROWS):
        la = la_ref[grp * GLA_CUM_ROWS:(grp + 1) * GLA_CUM_ROWS, :]
        pieces = jnp.concatenate([p.astype(BF16) for p in _split3(la)], axis=1)
        b3 = jnp.dot(cum_ref[...], pieces, preferred_element_type=F32)
        b_groups.append(b3[:, :kw] + b3[:, kw:2 * kw] + b3[:, 2 * kw:])
    for ch in range(tl // c):
        rows = slice(ch * c, (ch + 1) * c)
        in_grp = (ch * c) % GLA_CUM_ROWS
        b = b_groups[(ch * c) // GLA_CUM_ROWS][in_grp:in_grp + c]
        b_last = b[c - 1:c]
        qf = qk_ref[rows, :kw].astype(F32)
        kf = qk_ref[rows, kw:].astype(F32)
        q_e = (qf * jnp.exp(b) * (GLA_DK ** -0.5)).astype(BF16)
        k_e = (kf * jnp.exp(-b)).astype(BF16)
        k_t = (kf * jnp.exp(b_last - b)).astype(BF16)
        decay = jnp.exp(b_last)
        for h in range(GLA_HEADS):
            ps = slice((h // 2) * LANES, (h // 2 + 1) * LANES)
            mine = low_half if h % 2 == 0 else ~low_half
            qp, kp = q_e[:, ps], k_e[:, ps]
            qh = jnp.where(mine, qp, jnp.zeros_like(qp))
            kth = jnp.where(mine, k_t[:, ps], jnp.zeros_like(qp))
            vs = slice(h * GLA_DV, (h + 1) * GLA_DV)
            vh = vr_ref[rows, vs]
            a = pl.dot(qh, kp, trans_b=True)
            a = jnp.where(causal, a, 0.0).astype(BF16)
            st = st_ref[h]
            o = (jnp.dot(a, vh, preferred_element_type=F32)
                 + pl.dot(qh, st.astype(BF16), trans_b=True))
            st_ref[h] = st * decay[:, ps] + pl.dot(vh, kth, trans_a=True)
            rh = vr_ref[rows, slice(vw + h * GLA_DV, vw + (h + 1) * GLA_DV)].astype(F32)
            o = _rms(o, g) * (rh * jax.nn.sigmoid(rh))
            o_ref[rows, vs] = o.astype(BF16)


def _gla(qk, vr, la, g, batch, seq):
    tl = TL_GLA
    nl = seq // tl
    row = lambda w: pl.BlockSpec((tl, w), lambda b, i: (b * nl + i, 0))
    r = np.arange(GLA_CUM_ROWS)
    cum = jnp.asarray((r[None, :] <= r[:, None])
                      & (r[None, :] // GLA_CHUNK == r[:, None] // GLA_CHUNK), BF16)
    return pl.pallas_call(
        _gla_kernel,
        grid=(batch, nl),
        in_specs=[row(512), row(1024), row(256),
                  pl.BlockSpec((1, GLA_DV), lambda b, i: (0, 0)),
                  pl.BlockSpec((GLA_CUM_ROWS, GLA_CUM_ROWS), lambda b, i: (0, 0))],
        out_specs=row(512),
        out_shape=jax.ShapeDtypeStruct((batch * seq, 512), BF16),
        scratch_shapes=[pltpu.VMEM((GLA_HEADS, GLA_DV, LANES), F32)],
        compiler_params=pltpu.CompilerParams(
            dimension_semantics=("arbitrary", "arbitrary"), vmem_limit_bytes=VMEM_LIMIT),
        name="gla",
    )(qk, vr, la, g, cum)


def _swa_kernel(sink_ref, bias0_ref, bias_ref, q_ref, kvc_ref, kvp_ref, g_ref, o_ref):
    w = SWA_WINDOW
    scale = jnp.asarray(SWA_HEAD_DIM ** -0.5, BF16)
    kv_lane = lax.broadcasted_iota(jnp.int32, (2 * w, LANES), 1)
    lane_lo = kv_lane < SWA_HEAD_DIM
    out_lo = lax.broadcasted_iota(jnp.int32, (w, LANES), 1) < SWA_HEAD_DIM
    ones_hi = jnp.where(kv_lane == SWA_HEAD_DIM, 1.0, 0.0).astype(BF16)
    ones_lo = jnp.where(kv_lane == 0, 1.0, 0.0).astype(BF16)
    g = g_ref[...]

    for sb in range(SWA_SUB):
        rows = slice(sb * w, (sb + 1) * w)
        if sb == 0:
            kv = jnp.concatenate([kvp_ref[...], kvc_ref[0:w, :]], axis=0)
            bias = bias0_ref
        else:
            kv = kvc_ref[(sb - 1) * w:(sb + 1) * w, :]
            bias = bias_ref
        k = kv[:, 0:LANES] * scale
        ks = kv[:, LANES:2 * LANES] * scale
        v, vs = kv[:, 2 * LANES:3 * LANES], kv[:, 3 * LANES:]
        zero = jnp.zeros_like(k)
        k_low = [jnp.where(lane_lo, k, zero), jnp.where(lane_lo, ks, zero)]
        k_high = [jnp.where(lane_lo, zero, ks), jnp.where(lane_lo, zero, k)]
        v_low = [jnp.where(lane_lo, v, zero) + ones_hi, jnp.where(lane_lo, vs, zero) + ones_hi]
        v_high = [jnp.where(lane_lo, zero, vs) + ones_lo, jnp.where(lane_lo, zero, v) + ones_lo]

        def weights(s, head):
            s = s + bias[0, head]
            sink = sink_ref[head]
            m = jnp.maximum(jnp.max(s, axis=-1, keepdims=True), sink)
            return jnp.exp(s - m).astype(BF16), jnp.exp(sink - m)

        for pair in range(SWA_HEADS // 2):
            j = (2 * pair) // (SWA_HEADS // SWA_KV_HEADS)
            cols = slice(pair * LANES, (pair + 1) * LANES)
            qp = q_ref[rows, cols]
            e0, sink0 = weights(pl.dot(qp, k_low[j], trans_b=True), 2 * pair)
            e1, sink1 = weights(pl.dot(qp, k_high[j], trans_b=True), 2 * pair + 1)
            oa = jnp.dot(e0, v_low[j], preferred_element_type=F32)
            ob = jnp.dot(e1, v_high[j], preferred_element_type=F32)
            inv0 = 1.0 / (oa[:, SWA_HEAD_DIM:SWA_HEAD_DIM + 1] + sink0)
            inv1 = 1.0 / (ob[:, 0:1] + sink1)
            o = jnp.where(out_lo, oa * inv0, ob * inv1)
            sq = o * o
            ms_lo = jnp.sum(jnp.where(out_lo, sq, 0.0), axis=-1, keepdims=True)
            ms_hi = jnp.sum(jnp.where(out_lo, 0.0, sq), axis=-1, keepdims=True)
            ms = jnp.where(out_lo, ms_lo, ms_hi) * (1.0 / SWA_HEAD_DIM)
            o = o * lax.rsqrt(ms + NORM_EPS) * g
            o_ref[rows, cols] = o.astype(BF16)


def _swa_bias():
    w = SWA_WINDOW
    slopes = 2.0 ** (-8.0 * np.arange(1, SWA_HEADS + 1, dtype=np.float64) / SWA_HEADS)
    rel = np.arange(w)[:, None] + w - np.arange(2 * w)[None, :]
    in_window = (rel >= 0) & (rel < w)
    exists = np.stack([np.arange(2 * w) >= w, np.ones(2 * w, bool)])
    valid = in_window[None] & exists[:, None, :]
    bias = -slopes[None, :, None, None] * rel[None, None].astype(np.float64)
    return np.where(valid[:, None], bias, NEG_BIG).astype(np.float32)


def _swa(qb, kv, sinks, g2, batch, seq):
    w = SWA_WINDOW
    rows = SWA_SUB * w
    nb = seq // rows
    cur = lambda: pl.BlockSpec((rows, 512), lambda b, n: (b * nb + n, 0))
    prev = pl.BlockSpec(
        (w, 512), lambda b, n: (jnp.maximum((b * nb + n) * SWA_SUB - 1, b * nb * SWA_SUB), 0))
    table = lambda index: pl.BlockSpec((1, SWA_HEADS, w, 2 * w), index)
    bias = jnp.asarray(_swa_bias())
    return pl.pallas_call(
        _swa_kernel,
        grid=(batch, nb),
        in_specs=[pl.BlockSpec(memory_space=pltpu.SMEM),
                  table(lambda b, n: (jnp.minimum(n, 1), 0, 0, 0)),
                  table(lambda b, n: (1, 0, 0, 0)),
                  cur(), cur(), prev,
                  pl.BlockSpec((1, LANES), lambda b, n: (0, 0))],
        out_specs=cur(),
        out_shape=jax.ShapeDtypeStruct((batch * seq, 512), BF16),
        compiler_params=pltpu.CompilerParams(
            dimension_semantics=("arbitrary", "arbitrary"), vmem_limit_bytes=VMEM_LIMIT),
        name="swa",
    )(sinks, bias, bias, qb, kv, kv, g2)


def _out_route_kernel(oa_ref, ob_ref, x_ref, wo_ref, g_ref, wrh_ref, wrl_ref, br_ref,
                      earlier_ref, h13_ref, hn3_ref, route_ref, cnt_ref, hbuf, hsem):
    i = pl.program_id(0)
    tm = TM_ROUTE

    @pl.when(i == 0)
    def _():
        cnt_ref[...] = jnp.zeros_like(cnt_ref)

    def stores(step, slot):
        rows = pl.ds(step * tm, tm)
        return [_SlabTileCopy(hbuf.at[slot, which], out.at[rows], hsem.at[slot], to_slab=True)
                for which, out in enumerate((h13_ref, hn3_ref))]

    slot = i % 2

    @pl.when(i >= 2)
    def _():
        for st in stores(i - 2, slot):
            st.wait()

    h1 = (x_ref[...]
          + jnp.dot(oa_ref[...], wo_ref[:oa_ref.shape[1], :], preferred_element_type=F32)
          + jnp.dot(ob_ref[...], wo_ref[oa_ref.shape[1]:, :], preferred_element_type=F32))
    hn = _rms(h1, g_ref[...])
    hbuf[slot, 0] = h1
    hbuf[slot, 1] = hn
    for st in stores(i, slot):
        st.start()

    @pl.when(i == pl.num_programs(0) - 1)
    def _():
        @pl.when(i >= 1)
        def _():
            for st in stores(i - 1, 1 - slot):
                st.wait()
        for st in stores(i, slot):
            st.wait()

    hn_hi = hn.astype(BF16)
    hn_lo = (hn - hn_hi.astype(F32)).astype(BF16)
    logits = (pl.dot(wrh_ref[...], hn_hi, trans_b=True)
              + pl.dot(wrh_ref[...], hn_lo, trans_b=True)
              + pl.dot(wrl_ref[...], hn_hi, trans_b=True)) + br_ref[...]

    eid = lax.broadcasted_iota(jnp.int32, (N_EXPERTS, tm), 0)
    work = logits
    vals, idxs, sels = [], [], []
    for _ in range(TOP_K):
        m = jnp.max(work, axis=0, keepdims=True)
        idx = jnp.min(jnp.where(work == m, eid, N_EXPERTS), axis=0, keepdims=True)
        sel = eid == idx
        vals.append(m)
        idxs.append(idx)
        sels.append(sel)
        work = jnp.where(sel, -3e38, work)
    exps = [jnp.exp(v - vals[0]) for v in vals]
    inv_den = 1.0 / (exps[0] + exps[1] + exps[2] + exps[3])

    multihot = jnp.where(sels[0] | sels[1] | sels[2] | sels[3], 1.0, 0.0)
    before = (jnp.dot(multihot.astype(BF16), earlier_ref[...], preferred_element_type=F32)
              + cnt_ref[:, 0:1])
    ranks = [jnp.sum(jnp.where(s, before, 0.0), axis=0, keepdims=True) for s in sels]
    route_ref[...] = jnp.concatenate(
        [i.astype(F32) for i in idxs] + [e * inv_den for e in exps] + ranks
        + [jnp.zeros((ROUTE_ROWS - 3 * TOP_K, tm), F32)], axis=0)
    cnt_ref[...] += jnp.sum(multihot, axis=1, keepdims=True)


def _out_route(oa, ob, x2, wo, g2, wr_hi, wr_lo, br):
    T = x2.shape[0]
    tm = TM_ROUTE
    row = lambda w: pl.BlockSpec((tm, w), lambda i: (i, 0))
    full = lambda a: pl.BlockSpec(a.shape, lambda i: (0,) * a.ndim)
    earlier = jnp.asarray(np.triu(np.ones((tm, tm), np.float32), k=1), BF16)
    return pl.pallas_call(
        _out_route_kernel,
        grid=(T // tm,),
        in_specs=[row(512), row(512), row(D_MODEL), full(wo), full(g2),
                  full(wr_hi), full(wr_lo), full(br), full(earlier)],
        out_specs=[pl.BlockSpec(memory_space=pl.ANY), pl.BlockSpec(memory_space=pl.ANY),
                   pl.BlockSpec((ROUTE_ROWS, tm), lambda i: (0, i)),
                   pl.BlockSpec((N_EXPERTS, LANES), lambda i: (0, 0))],
        out_shape=[jax.ShapeDtypeStruct((T, SLAB_SUBLANES, LANES), F32),
                   jax.ShapeDtypeStruct((T, SLAB_SUBLANES, LANES), F32),
                   jax.ShapeDtypeStruct((ROUTE_ROWS, T), F32),
                   jax.ShapeDtypeStruct((N_EXPERTS, LANES), F32)],
        scratch_shapes=[pltpu.VMEM((2, 2, tm, D_MODEL), F32), pltpu.SemaphoreType.DMA((2,))],
        compiler_params=pltpu.CompilerParams(
            dimension_semantics=("arbitrary",), vmem_limit_bytes=VMEM_LIMIT,
            has_side_effects=True),
        name="out_route",
    )(oa, ob, x2, wo, g2, wr_hi, wr_lo, br, earlier)


def _row_copy(src, s, dst, d, sem):
    return pltpu.make_async_copy(src.at[pl.ds(s, 1)], dst.at[pl.ds(d, 1)], sem)


def _dispatch_kernel(dest_ref, pad_start_ref, pad_cnt_ref, nused_ref, hn_ref, xs_ref,
                     zero_ref, sem, zsem):
    i = pl.program_id(0)
    n = pl.num_programs(0)
    tm = EXPERT_PAD
    n_tok = dest_ref.shape[0] // TOP_K
    base = i * TM_DISPATCH
    for p in range(TM_DISPATCH * TOP_K):
        r, k = divmod(p, TOP_K)
        _row_copy(hn_ref, r, xs_ref, dest_ref[base + k * n_tok + r], sem).start(priority=p % 2)
    for _ in range(TOP_K):
        pltpu.make_async_copy(hn_ref, xs_ref.at[pl.ds(0, TM_DISPATCH)], sem).wait()

    @pl.when(i == n - 1)
    def _():
        zero_ref[...] = jnp.zeros_like(zero_ref)
        sub = 8
        for e in range(N_EXPERTS):
            start, cnt = pad_start_ref[e], pad_cnt_ref[e]
            head = cnt & (sub - 1)
            body0 = start + head
            nbody = cnt // sub

            def head_copy(r):
                return _row_copy(zero_ref, 0, xs_ref, start + r, zsem)

            def body_copy(c):
                return pltpu.make_async_copy(zero_ref.at[pl.ds(0, sub)],
                                             xs_ref.at[pl.ds(body0 + c * sub, sub)], zsem)

            for copy, count in ((head_copy, head), (body_copy, nbody)):
                def zissue(r, carry, copy=copy):
                    copy(r).start()
                    return carry

                def zwait(r, carry, copy=copy):
                    copy(r).wait()
                    return carry

                lax.fori_loop(0, count, zissue, 0)
                lax.fori_loop(0, count, zwait, 0)

        def tail_copy(t):
            return pltpu.make_async_copy(zero_ref, xs_ref.at[pl.ds(t * tm, tm)], zsem)

        def tissue(t, carry):
            tail_copy(t).start()
            return carry

        def twait(t, carry):
            tail_copy(t).wait()
            return carry

        n_tiles = xs_ref.shape[0] // tm
        lax.fori_loop(nused_ref[0], n_tiles, tissue, 0)
        lax.fori_loop(nused_ref[0], n_tiles, twait, 0)


def _dispatch(dest, pad_start, pad_cnt, n_used, hn, n_rows):
    T = hn.shape[0]
    any_spec = pl.BlockSpec(memory_space=pl.ANY)
    return pl.pallas_call(
        _dispatch_kernel,
        grid_spec=pltpu.PrefetchScalarGridSpec(
            num_scalar_prefetch=4,
            grid=(T // TM_DISPATCH,),
            in_specs=[pl.BlockSpec((TM_DISPATCH, SLAB_SUBLANES, LANES), lambda i, *_: (i, 0, 0))],
            out_specs=any_spec,
            scratch_shapes=[pltpu.VMEM((EXPERT_PAD, SLAB_SUBLANES, LANES), F32),
                            pltpu.SemaphoreType.DMA(()),
                            pltpu.SemaphoreType.DMA(())]),
        out_shape=jax.ShapeDtypeStruct((n_rows, SLAB_SUBLANES, LANES), F32),
        compiler_params=pltpu.CompilerParams(
            dimension_semantics=("arbitrary",), has_side_effects=True,
            vmem_limit_bytes=VMEM_LIMIT),
        name="dispatch",
    )(dest, pad_start, pad_cnt, n_used, hn)


def _experts_kernel(row0_ref, nfull_ref, tail_ref, xs_ref, wg_ref, bg_ref, wl_ref, bl_ref, wd_ref,
                    bd_ref, ys_ref, wg_bf, wl_bf, wd_bf, xbuf, ybuf, xsem, ysem, wf32, wsem):
    e = pl.program_id(0)
    tm, half = TM_EXPERT, EXPERT_PAD
    r0, nf, tail = row0_ref[e], nfull_ref[e], tail_ref[e]
    nt = nf + tail

    def x_load(row, slot, rows):
        return _SlabTileCopy(xbuf.at[slot, pl.ds(0, rows)], xs_ref.at[pl.ds(row, rows)],
                             xsem.at[slot], to_slab=False)

    def y_store(row, slot, rows):
        return _SlabTileCopy(ybuf.at[slot, pl.ds(0, rows)], ys_ref.at[pl.ds(row, rows)],
                             ysem.at[slot], to_slab=True)

    def start_first_load(expert):
        @pl.when(nfull_ref[expert] > 0)
        def _():
            x_load(row0_ref[expert], 0, tm).start()

        @pl.when((nfull_ref[expert] == 0) & (tail_ref[expert] > 0))
        def _():
            x_load(row0_ref[expert], 0, half).start()

    def compute(slot, rows):
        x = xbuf[slot, :rows].astype(BF16)
        glu = jnp.minimum(jnp.dot(x, wg_bf[...], preferred_element_type=F32) + bg_ref[0],
                          SWIGLU_LIMIT)
        lin = jnp.clip(jnp.dot(x, wl_bf[...], preferred_element_type=F32) + bl_ref[0],
                       -SWIGLU_LIMIT, SWIGLU_LIMIT)
        hid = glu * jax.nn.sigmoid(SWIGLU_ALPHA * glu) * (lin + 1.0)
        ybuf[slot, :rows] = (jnp.dot(hid.astype(BF16), wd_bf[...], preferred_element_type=F32)
                             + bd_ref[0])

    def w_fetch(expert):
        return [pltpu.make_async_copy(w.at[expert], wf32.at[i], wsem.at[i])
                for i, w in enumerate((wg_ref, wl_ref, wd_ref))]

    @pl.when(e == 0)
    def _():
        start_first_load(0)
        for cp in w_fetch(0):
            cp.start()

    for cp, w_bf, i in zip(w_fetch(e), (wg_bf, wl_bf, wd_bf), range(3)):
        cp.wait()
        w_bf[...] = wf32[i].astype(BF16)

    @pl.when(e + 1 < pl.num_programs(0))
    def _():
        for cp in w_fetch(e + 1):
            cp.start()

    def full_tile(j, carry):
        slot = j % 2

        @pl.when(j + 1 < nf)
        def _():
            x_load(r0 + (j + 1) * tm, 1 - slot, tm).start()

        @pl.when((j + 1 == nf) & (tail > 0))
        def _():
            x_load(r0 + (j + 1) * tm, 1 - slot, half).start()

        x_load(r0 + j * tm, slot, tm).wait()

        @pl.when(j >= 2)
        def _():
            y_store(r0 + (j - 2) * tm, slot, tm).wait()

        compute(slot, tm)
        y_store(r0 + j * tm, slot, tm).start()
        return carry

    lax.fori_loop(0, nf, full_tile, 0)

    @pl.when(tail > 0)
    def _():
        slot = nf % 2
        x_load(r0 + nf * tm, slot, half).wait()

        @pl.when(nf >= 2)
        def _():
            y_store(r0 + (nf - 2) * tm, slot, tm).wait()

        compute(slot, half)
        y_store(r0 + nf * tm, slot, half).start()

    @pl.when(e + 1 < pl.num_programs(0))
    def _():
        start_first_load(e + 1)

    @pl.when(nt >= 2)
    def _():
        y_store(r0 + (nt - 2) * tm, nt % 2, tm).wait()

    @pl.when((nt >= 1) & (tail == 0))
    def _():
        y_store(r0 + (nt - 1) * tm, (nt - 1) % 2, tm).wait()

    @pl.when(tail > 0)
    def _():
        y_store(r0 + nf * tm, nf % 2, half).wait()

    @pl.when(e == pl.num_programs(0) - 1)
    def _():
        ybuf[0] = jnp.zeros((tm, D_MODEL), F32)
        first = (r0 + nf * tm + tail * half) // half
        n_pieces = ys_ref.shape[0] // half

        def fill_issue(t, carry):
            y_store(t * half, 0, half).start()
            return carry

        def fill_wait(t, carry):
            y_store(t * half, 0, half).wait()
            return carry

        lax.fori_loop(first, n_pieces, fill_issue, 0)
        lax.fori_loop(first, n_pieces, fill_wait, 0)


def _experts(row0, nfull, tail, xs, w_glu, b_glu, w_lin, b_lin, w_down, b_down):
    tm = TM_EXPERT
    any_spec = pl.BlockSpec(memory_space=pl.ANY)
    bspec = lambda: pl.BlockSpec((1, 1, D_FF), lambda e, *_: (e, 0, 0))
    return pl.pallas_call(
        _experts_kernel,
        grid_spec=pltpu.PrefetchScalarGridSpec(
            num_scalar_prefetch=3,
            grid=(N_EXPERTS,),
            in_specs=[any_spec, any_spec, bspec(), any_spec, bspec(), any_spec, bspec()],
            out_specs=any_spec,
            scratch_shapes=[pltpu.VMEM((D_MODEL, D_FF), BF16)] * 3
            + [pltpu.VMEM((2, tm, D_MODEL), F32), pltpu.VMEM((2, tm, D_MODEL), F32),
               pltpu.SemaphoreType.DMA((2,)), pltpu.SemaphoreType.DMA((2,)),
               pltpu.VMEM((3, D_MODEL, D_FF), F32), pltpu.SemaphoreType.DMA((3,))]),
        out_shape=jax.ShapeDtypeStruct(xs.shape, F32),
        compiler_params=pltpu.CompilerParams(
            dimension_semantics=("arbitrary",), vmem_limit_bytes=VMEM_LIMIT,
            has_side_effects=True),
        name="experts",
    )(row0, nfull, tail, xs, w_glu, b_glu.reshape(N_EXPERTS, 1, D_FF),
      w_lin, b_lin.reshape(N_EXPERTS, 1, D_FF), w_down, b_down.reshape(N_EXPERTS, 1, D_MODEL))


def _combine_kernel(dest_ref, gate_ref, ys_ref, h1_ref, g_ref, out_ref, buf, obuf, sems, osems):
    s = pl.program_id(0)
    tm = TM_COMBINE
    n = out_ref.shape[0] // tm
    n_tok = dest_ref.shape[0] // TOP_K

    def out_store(tile, slot):
        return _SlabTileCopy(out_ref.at[pl.ds(tile * tm, tm)], obuf.at[slot], osems.at[slot],
                             to_slab=False)

    def issue_row(slot, r):
        for k in range(TOP_K):
            _row_copy(ys_ref, dest_ref[s * tm + k * n_tok + r], buf.at[slot, k], r,
                      sems.at[slot]).start(priority=k % 2)

    def reduce_row(slot, r):
        acc = h1_ref[r]
        for k in range(TOP_K):
            acc = acc + gate_ref[(s - 1) * tm + k * n_tok + r] * buf[slot, k, r]
        ss = jnp.sum(jnp.sum(acc * acc, axis=1, keepdims=True), axis=0, keepdims=True)
        obuf[slot, r] = acc * lax.rsqrt(ss * (1.0 / D_MODEL) + NORM_EPS) * g_ref[0]

    def step_body(gather_slot, reduce_slot):
        for r in range(tm):
            if gather_slot is not None:
                issue_row(gather_slot, r)
            if reduce_slot is not None:
                reduce_row(reduce_slot, r)
        if reduce_slot is not None:
            out_store(s - 1, reduce_slot).start()

    for parity in range(2):
        other = 1 - parity

        @pl.when(s % 2 == parity)
        def _():
            @pl.when(s < n)
            def _():
                step_body(parity, None)

            @pl.when(s >= 3)
            def _():
                out_store(s - 3, other).wait()

            @pl.when(s > 0)
            def _():
                for k in range(TOP_K):
                    pltpu.make_async_copy(ys_ref.at[pl.ds(0, tm)], buf.at[other, k],
                                          sems.at[other]).wait()
                step_body(None, other)

            if n % 2 == parity:
                @pl.when(s == n)
                def _():
                    if n >= 2:
                        out_store(n - 2, parity).wait()
                    out_store(n - 1, other).wait()


def _combine(dest, ys3, h13, gate, fg):
    T = h13.shape[0]
    tm = TM_COMBINE
    slab = (SLAB_SUBLANES, LANES)
    prev_tile = lambda s, *_: (jnp.maximum(s - 1, 0), 0, 0)
    return pl.pallas_call(
        _combine_kernel,
        grid_spec=pltpu.PrefetchScalarGridSpec(
            num_scalar_prefetch=2,
            grid=(T // tm + 1,),
            in_specs=[pl.BlockSpec(memory_space=pl.ANY),
                      pl.BlockSpec((tm,) + slab, prev_tile),
                      pl.BlockSpec((1,) + slab, lambda s, *_: (0, 0, 0))],
            out_specs=pl.BlockSpec(memory_space=pl.ANY),
            scratch_shapes=[pltpu.VMEM((2, TOP_K, tm) + slab, F32),
                            pltpu.VMEM((2, tm) + slab, F32),
                            pltpu.SemaphoreType.DMA((2,)),
                            pltpu.SemaphoreType.DMA((2,))]),
        out_shape=jax.ShapeDtypeStruct((T, D_MODEL), F32),
        compiler_params=pltpu.CompilerParams(
            dimension_semantics=("arbitrary",), vmem_limit_bytes=VMEM_LIMIT,
            has_side_effects=True),
        name="combine",
    )(dest, gate, ys3, h13, fg.reshape((1,) + slab))


def _swap_halves(w):
    h = w.shape[-1] // 2
    return jnp.concatenate([w[..., h:], w[..., :h]], axis=-1)


def _layer(x2, batch, seq, norm1_g, w_in, w_alpha_up, b_alpha, gla_norm_g, swa_sinks, swa_norm_g,
           w_out, norm2_g, w_router, b_router, w_glu, b_glu, w_lin, b_lin, w_down, b_down):
    T = x2.shape[0]
    kb_w, vb_w = w_in[:, 2064:2192], w_in[:, 2192:2320]
    w_z = w_in[:, 1536:1552]
    w_cat = jnp.concatenate([
        w_in[:, 0:1536], w_in[:, 1552:2064],
        kb_w, _swap_halves(kb_w), vb_w, _swap_halves(vb_w),
        jnp.pad(jnp.tile(w_z, (1, Z_PIECES)), [(0, 0), (0, LANES - Z_PIECES * GLA_RANK)]),
    ], axis=1).astype(BF16)
    wup_hi = w_alpha_up.astype(BF16)
    wup_lo = (w_alpha_up - wup_hi.astype(F32)).astype(BF16)
    wup_cat = jnp.pad(jnp.concatenate([wup_hi, wup_hi, wup_hi, wup_lo, wup_lo], axis=0),
                      [(0, LANES - Z_PIECES * GLA_RANK), (0, 0)])

    qk, vr, la, qb, kv = _in_proj(
        x2, norm1_g.reshape(1, -1), w_cat, wup_cat, b_alpha.reshape(1, -1))
    oa = _gla(qk, vr, la, gla_norm_g.reshape(1, -1), batch, seq)
    ob = _swa(qb, kv, swa_sinks, jnp.tile(swa_norm_g, 2).reshape(1, -1), batch, seq)

    wo = w_out.astype(BF16)
    wr_t = w_router.T
    wr_hi = wr_t.astype(BF16)
    wr_lo = (wr_t - wr_hi.astype(F32)).astype(BF16)
    h1, hn, route, cnt = _out_route(oa, ob, x2, wo, norm2_g.reshape(1, -1),
                                    wr_hi, wr_lo, b_router.reshape(-1, 1))

    tm, pad = TM_EXPERT, EXPERT_PAD
    n_rows = T * TOP_K + N_EXPERTS * pad
    counts = cnt[:, 0].astype(jnp.int32)
    padded = (counts + pad - 1) // pad * pad
    pends = jnp.cumsum(padded)
    pstarts = pends - padded
    top_idx = route[0:TOP_K].astype(jnp.int32)
    gate = route[TOP_K:2 * TOP_K].reshape(-1)
    rank = route[2 * TOP_K:3 * TOP_K].astype(jnp.int32)
    experts = jnp.arange(N_EXPERTS)[:, None, None]
    seg_start = jnp.sum(jnp.where(top_idx[None] == experts, pstarts[:, None, None], 0), axis=0)
    dest = (seg_start + rank).reshape(-1)
    n_used = (pends[-1] // pad).reshape(1)
    xs = _dispatch(dest, pstarts + counts, padded - counts, n_used, hn, n_rows)
    ys = _experts(pstarts, padded // tm, padded % tm // pad, xs,
                  w_glu, b_glu, w_lin, b_lin, w_down, b_down)
    return dest, ys, h1, gate


def kernel(x, norm1_g, w_in, w_alpha_up, b_alpha, gla_norm_g, swa_sinks, swa_norm_g, w_out,
           norm2_g, w_router, b_router, w_glu, b_glu, w_lin, b_lin, w_down, b_down, final_g):
    batch, seq, d = x.shape
    assert norm1_g.shape[0] == 1, "single-layer problem"
    x2 = x.reshape(batch * seq, d)
    dest, ys, h1, gate = _layer(
        x2, batch, seq, norm1_g[0], w_in[0], w_alpha_up[0], b_alpha[0], gla_norm_g[0],
        swa_sinks[0], swa_norm_g[0], w_out[0], norm2_g[0], w_router[0], b_router[0],
        w_glu[0], b_glu[0], w_lin[0], b_lin[0], w_down[0], b_down[0])
    out = _combine(dest, ys, h1, gate, final_g.reshape(1, -1))
    return out.reshape(batch, seq, d)
```

```python
import numpy as np
import jax
import jax.numpy as jnp
from jax import lax
from jax.experimental import pallas as pl
from jax.experimental.pallas import tpu as pltpu

F32 = jnp.float32
BF16 = jnp.bfloat16

D_MODEL = 1024
GLA_HEADS = 4
GLA_DK = 64
GLA_DV = 128
GLA_RANK = 16
GLA_GATE_TAU = 16.0
GLA_CHUNK = 64
SWA_HEADS = 8
SWA_KV_HEADS = 2
SWA_HEAD_DIM = 64
SWA_WINDOW = 128
N_EXPERTS = 32
TOP_K = 4
D_FF = 1024
SWIGLU_LIMIT = 7.0
SWIGLU_ALPHA = 1.702
NORM_EPS = 1e-5

LANES = 128
SLAB_SUBLANES = D_MODEL // LANES
VMEM_LIMIT = 56 * 1024 * 1024

TM_PROJ = 1024
TL_GLA = 1024
GLA_CUM_ROWS = 256
TM_ROUTE = 1024
TM_EXPERT = 512
EXPERT_PAD = 256
TM_COMBINE = 256
SWA_SUB = 8
TM_DISPATCH = 256

NEG_BIG = -1e30
ROUTE_ROWS = 16

C_QA, C_KA, C_VA, C_RA, C_QB, C_KB, C_KBS, C_VB, C_VBS, C_Z, C_END = (
    0, 256, 512, 1024, 1536, 2048, 2176, 2304, 2432, 2560, 2688)
Z_PIECES = 5


def _split3(x):
    hi = x.astype(BF16).astype(F32)
    r = x - hi
    mid = r.astype(BF16).astype(F32)
    lo = (r - mid).astype(BF16).astype(F32)
    return hi, mid, lo


class _SlabTileCopy:
    def __init__(self, flat, slab, sem, to_slab):
        self.copies = []
        for c in range(SLAB_SUBLANES):
            pair = (flat.at[:, pl.ds(c * LANES, LANES)], slab.at[:, c, :])
            src, dst = pair if to_slab else pair[::-1]
            self.copies.append(pltpu.make_async_copy(src, dst, sem))

    def start(self):
        for cp in self.copies:
            cp.start()

    def wait(self):
        for cp in self.copies:
            cp.wait()


def _rms(x, g):
    return x * lax.rsqrt(jnp.mean(x * x, axis=-1, keepdims=True) + NORM_EPS) * g


def _in_proj_kernel(x_ref, g_ref, w_ref, wup_ref, ba_ref,
                    qk_ref, vr_ref, la_ref, qb_ref, kv_ref):
    u = _rms(x_ref[...], g_ref[...]).astype(BF16)

    def proj(c0, c1):
        return jnp.dot(u, w_ref[:, c0:c1], preferred_element_type=F32)

    qk_ref[...] = proj(C_QA, C_VA).astype(BF16)
    vr_ref[:, :C_RA - C_VA] = proj(C_VA, C_RA).astype(BF16)
    vr_ref[:, C_RA - C_VA:] = proj(C_RA, C_QB).astype(BF16)
    qb_ref[...] = proj(C_QB, C_KB).astype(BF16)
    kv_ref[...] = proj(C_KB, C_Z).astype(BF16)
    z = proj(C_Z, C_END)
    hi, mid, lo = _split3(z)
    piece = lax.broadcasted_iota(jnp.int32, z.shape, 1) // GLA_RANK
    zc = jnp.where((piece == 0) | (piece == 3), hi, jnp.where(piece == 2, lo, mid)).astype(BF16)
    y = jnp.dot(zc, wup_ref[...], preferred_element_type=F32) + ba_ref[...]
    log_sig = jnp.minimum(y, 0.0) - jnp.log1p(jnp.exp(-jnp.abs(y)))
    la_ref[...] = log_sig * (1.0 / GLA_GATE_TAU)


def _in_proj(x2, g1, w_cat, wup_p, ba_p):
    T = x2.shape[0]
    tm = TM_PROJ
    row = lambda w: pl.BlockSpec((tm, w), lambda i: (i, 0))
    full = lambda a: pl.BlockSpec(a.shape, lambda i: (0,) * a.ndim)
    outs = [(512, BF16), (1024, BF16), (256, F32), (512, BF16), (512, BF16)]
    return pl.pallas_call(
        _in_proj_kernel,
        grid=(T // tm,),
        in_specs=[row(D_MODEL), full(g1), full(w_cat), full(wup_p), full(ba_p)],
        out_specs=[row(w) for w, _ in outs],
        out_shape=[jax.ShapeDtypeStruct((T, w), dt) for w, dt in outs],
        compiler_params=pltpu.CompilerParams(
            dimension_semantics=("arbitrary",), vmem_limit_bytes=VMEM_LIMIT),
        name="in_proj",
    )(x2, g1, w_cat, wup_p, ba_p)


def _gla_kernel(qk_ref, vr_ref, la_ref, g_ref, cum_ref, o_ref, st_ref):
    @pl.when(pl.program_id(1) == 0)
    def _():
        st_ref[...] = jnp.zeros_like(st_ref)

    tl = TL_GLA
    c = GLA_CHUNK
    kw = GLA_HEADS * GLA_DK
    vw = GLA_HEADS * GLA_DV
    causal = (lax.broadcasted_iota(jnp.int32, (c, c), 0)
              >= lax.broadcasted_iota(jnp.int32, (c, c), 1))
    low_half = lax.broadcasted_iota(jnp.int32, (c, LANES), 1) < GLA_DK
    g = g_ref[...]
    b_groups = []
    for grp in range(tl // GLA_CUM_ROWS):
        la = la_ref[grp * GLA_CUM_ROWS:(grp + 1) * GLA_CUM_ROWS, :]
        pieces = jnp.concatenate([p.astype(BF16) for p in _split3(la)], axis=1)
        b3 = jnp.dot(cum_ref[...], pieces, preferred_element_type=F32)
        b_groups.append(b3[:, :kw] + b3[:, kw:2 * kw] + b3[:, 2 * kw:])
    for ch in range(tl // c):
        rows = slice(ch * c, (ch + 1) * c)
        in_grp = (ch * c) % GLA_CUM_ROWS
        b = b_groups[(ch * c) // GLA_CUM_ROWS][in_grp:in_grp + c]
        b_last = b[c - 1:c]
        qf = qk_ref[rows, :kw].astype(F32)
        kf = qk_ref[rows, kw:].astype(F32)
        q_e = (qf * jnp.exp(b) * (GLA_DK ** -0.5)).astype(BF16)
        k_e = (kf * jnp.exp(-b)).astype(BF16)
        k_t = (kf * jnp.exp(b_last - b)).astype(BF16)
        decay = jnp.exp(b_last)
        for h in range(GLA_HEADS):
            ps = slice((h // 2) * LANES, (h // 2 + 1) * LANES)
            mine = low_half if h % 2 == 0 else ~low_half
            qp, kp = q_e[:, ps], k_e[:, ps]
            qh = jnp.where(mine, qp, jnp.zeros_like(qp))
            kth = jnp.where(mine, k_t[:, ps], jnp.zeros_like(qp))
            vs = slice(h * GLA_DV, (h + 1) * GLA_DV)
            vh = vr_ref[rows, vs]
            a = pl.dot(qh, kp, trans_b=True)
            a = jnp.where(causal, a, 0.0).astype(BF16)
            st = st_ref[h]
            o = (jnp.dot(a, vh, preferred_element_type=F32)
                 + pl.dot(qh, st.astype(BF16), trans_b=True))
            st_ref[h] = st * decay[:, ps] + pl.dot(vh, kth, trans_a=True)
            rh = vr_ref[rows, slice(vw + h * GLA_DV, vw + (h + 1) * GLA_DV)].astype(F32)
            o = _rms(o, g) * (rh * jax.nn.sigmoid(rh))
            o_ref[rows, vs] = o.astype(BF16)


def _gla(qk, vr, la, g, batch, seq):
    tl = TL_GLA
    nl = seq // tl
    row = lambda w: pl.BlockSpec((tl, w), lambda b, i: (b * nl + i, 0))
    r = np.arange(GLA_CUM_ROWS)
    cum = jnp.asarray((r[None, :] <= r[:, None])
                      & (r[None, :] // GLA_CHUNK == r[:, None] // GLA_CHUNK), BF16)
    return pl.pallas_call(
        _gla_kernel,
        grid=(batch, nl),
        in_specs=[row(512), row(1024), row(256),
                  pl.BlockSpec((1, GLA_DV), lambda b, i: (0, 0)),
                  pl.BlockSpec((GLA_CUM_ROWS, GLA_CUM_ROWS), lambda b, i: (0, 0))],
        out_specs=row(512),
        out_shape=jax.ShapeDtypeStruct((batch * seq, 512), BF16),
        scratch_shapes=[pltpu.VMEM((GLA_HEADS, GLA_DV, LANES), F32)],
        compiler_params=pltpu.CompilerParams(
            dimension_semantics=("arbitrary", "arbitrary"), vmem_limit_bytes=VMEM_LIMIT),
        name="gla",
    )(qk, vr, la, g, cum)


def _swa_kernel(sink_ref, bias0_ref, bias_ref, q_ref, kvc_ref, kvp_ref, g_ref, o_ref):
    w = SWA_WINDOW
    scale = jnp.asarray(SWA_HEAD_DIM ** -0.5, BF16)
    kv_lane = lax.broadcasted_iota(jnp.int32, (2 * w, LANES), 1)
    lane_lo = kv_lane < SWA_HEAD_DIM
    out_lo = lax.broadcasted_iota(jnp.int32, (w, LANES), 1) < SWA_HEAD_DIM
    ones_hi = jnp.where(kv_lane == SWA_HEAD_DIM, 1.0, 0.0).astype(BF16)
    ones_lo = jnp.where(kv_lane == 0, 1.0, 0.0).astype(BF16)
    g = g_ref[...]

    for sb in range(SWA_SUB):
        rows = slice(sb * w, (sb + 1) * w)
        if sb == 0:
            kv = jnp.concatenate([kvp_ref[...], kvc_ref[0:w, :]], axis=0)
            bias = bias0_ref
        else:
            kv = kvc_ref[(sb - 1) * w:(sb + 1) * w, :]
            bias = bias_ref
        k = kv[:, 0:LANES] * scale
        ks = kv[:, LANES:2 * LANES] * scale
        v, vs = kv[:, 2 * LANES:3 * LANES], kv[:, 3 * LANES:]
        zero = jnp.zeros_like(k)
        k_low = [jnp.where(lane_lo, k, zero), jnp.where(lane_lo, ks, zero)]
        k_high = [jnp.where(lane_lo, zero, ks), jnp.where(lane_lo, zero, k)]
        v_low = [jnp.where(lane_lo, v, zero) + ones_hi, jnp.where(lane_lo, vs, zero) + ones_hi]
        v_high = [jnp.where(lane_lo, zero, vs) + ones_lo, jnp.where(lane_lo, zero, v) + ones_lo]

        def weights(s, head):
            s = s + bias[0, head]
            sink = sink_ref[head]
            m = jnp.maximum(jnp.max(s, axis=-1, keepdims=True), sink)
            return jnp.exp(s - m).astype(BF16), jnp.exp(sink - m)

        for pair in range(SWA_HEADS // 2):
            j = (2 * pair) // (SWA_HEADS // SWA_KV_HEADS)
            cols = slice(pair * LANES, (pair + 1) * LANES)
            qp = q_ref[rows, cols]
            e0, sink0 = weights(pl.dot(qp, k_low[j], trans_b=True), 2 * pair)
            e1, sink1 = weights(pl.dot(qp, k_high[j], trans_b=True), 2 * pair + 1)
            oa = jnp.dot(e0, v_low[j], preferred_element_type=F32)
            ob = jnp.dot(e1, v_high[j], preferred_element_type=F32)
            inv0 = 1.0 / (oa[:, SWA_HEAD_DIM:SWA_HEAD_DIM + 1] + sink0)
            inv1 = 1.0 / (ob[:, 0:1] + sink1)
            o = jnp.where(out_lo, oa * inv0, ob * inv1)
            sq = o * o
            ms_lo = jnp.sum(jnp.where(out_lo, sq, 0.0), axis=-1, keepdims=True)
            ms_hi = jnp.sum(jnp.where(out_lo, 0.0, sq), axis=-1, keepdims=True)
            ms = jnp.where(out_lo, ms_lo, ms_hi) * (1.0 / SWA_HEAD_DIM)
            o = o * lax.rsqrt(ms + NORM_EPS) * g
            o_ref[rows, cols] = o.astype(BF16)


def _swa_bias():
    w = SWA_WINDOW
    slopes = 2.0 ** (-8.0 * np.arange(1, SWA_HEADS + 1, dtype=np.float64) / SWA_HEADS)
    rel = np.arange(w)[:, None] + w - np.arange(2 * w)[None, :]
    in_window = (rel >= 0) & (rel < w)
    exists = np.stack([np.arange(2 * w) >= w, np.ones(2 * w, bool)])
    valid = in_window[None] & exists[:, None, :]
    bias = -slopes[None, :, None, None] * rel[None, None].astype(np.float64)
    return np.where(valid[:, None], bias, NEG_BIG).astype(np.float32)


def _swa(qb, kv, sinks, g2, batch, seq):
    w = SWA_WINDOW
    rows = SWA_SUB * w
    nb = seq // rows
    cur = lambda: pl.BlockSpec((rows, 512), lambda b, n: (b * nb + n, 0))
    prev = pl.BlockSpec(
        (w, 512), lambda b, n: (jnp.maximum((b * nb + n) * SWA_SUB - 1, b * nb * SWA_SUB), 0))
    table = lambda index: pl.BlockSpec((1, SWA_HEADS, w, 2 * w), index)
    bias = jnp.asarray(_swa_bias())
    return pl.pallas_call(
        _swa_kernel,
        grid=(batch, nb),
        in_specs=[pl.BlockSpec(memory_space=pltpu.SMEM),
                  table(lambda b, n: (jnp.minimum(n, 1), 0, 0, 0)),
                  table(lambda b, n: (1, 0, 0, 0)),
                  cur(), cur(), prev,
                  pl.BlockSpec((1, LANES), lambda b, n: (0, 0))],
        out_specs=cur(),
        out_shape=jax.ShapeDtypeStruct((batch * seq, 512), BF16),
        compiler_params=pltpu.CompilerParams(
            dimension_semantics=("arbitrary", "arbitrary"), vmem_limit_bytes=VMEM_LIMIT),
        name="swa",
    )(sinks, bias, bias, qb, kv, kv, g2)


def _out_route_kernel(oa_ref, ob_ref, x_ref, wo_ref, g_ref, wrh_ref, wrl_ref, br_ref,
                      earlier_ref, h13_ref, hn3_ref, route_ref, cnt_ref, hbuf, hsem):
    i = pl.program_id(0)
    tm = TM_ROUTE

    @pl.when(i == 0)
    def _():
        cnt_ref[...] = jnp.zeros_like(cnt_ref)

    def stores(step, slot):
        rows = pl.ds(step * tm, tm)
        return [_SlabTileCopy(hbuf.at[slot, which], out.at[rows], hsem.at[slot], to_slab=True)
                for which, out in enumerate((h13_ref, hn3_ref))]

    slot = i % 2

    @pl.when(i >= 2)
    def _():
        for st in stores(i - 2, slot):
            st.wait()

    h1 = (x_ref[...]
          + jnp.dot(oa_ref[...], wo_ref[:oa_ref.shape[1], :], preferred_element_type=F32)
          + jnp.dot(ob_ref[...], wo_ref[oa_ref.shape[1]:, :], preferred_element_type=F32))
    hn = _rms(h1, g_ref[...])
    hbuf[slot, 0] = h1
    hbuf[slot, 1] = hn
    for st in stores(i, slot):
        st.start()

    @pl.when(i == pl.num_programs(0) - 1)
    def _():
        @pl.when(i >= 1)
        def _():
            for st in stores(i - 1, 1 - slot):
                st.wait()
        for st in stores(i, slot):
            st.wait()

    hn_hi = hn.astype(BF16)
    hn_lo = (hn - hn_hi.astype(F32)).astype(BF16)
    logits = (pl.dot(wrh_ref[...], hn_hi, trans_b=True)
              + pl.dot(wrh_ref[...], hn_lo, trans_b=True)
              + pl.dot(wrl_ref[...], hn_hi, trans_b=True)) + br_ref[...]

    eid = lax.broadcasted_iota(jnp.int32, (N_EXPERTS, tm), 0)
    work = logits
    vals, idxs, sels = [], [], []
    for _ in range(TOP_K):
        m = jnp.max(work, axis=0, keepdims=True)
        idx = jnp.min(jnp.where(work == m, eid, N_EXPERTS), axis=0, keepdims=True)
        sel = eid == idx
        vals.append(m)
        idxs.append(idx)
        sels.append(sel)
        work = jnp.where(sel, -3e38, work)
    exps = [jnp.exp(v - vals[0]) for v in vals]
    inv_den = 1.0 / (exps[0] + exps[1] + exps[2] + exps[3])

    multihot = jnp.where(sels[0] | sels[1] | sels[2] | sels[3], 1.0, 0.0)
    before = (jnp.dot(multihot.astype(BF16), earlier_ref[...], preferred_element_type=F32)
              + cnt_ref[:, 0:1])
    ranks = [jnp.sum(jnp.where(s, before, 0.0), axis=0, keepdims=True) for s in sels]
    route_ref[...] = jnp.concatenate(
        [i.astype(F32) for i in idxs] + [e * inv_den for e in exps] + ranks
        + [jnp.zeros((ROUTE_ROWS - 3 * TOP_K, tm), F32)], axis=0)
    cnt_ref[...] += jnp.sum(multihot, axis=1, keepdims=True)


def _out_route(oa, ob, x2, wo, g2, wr_hi, wr_lo, br):
    T = x2.shape[0]
    tm = TM_ROUTE
    row = lambda w: pl.BlockSpec((tm, w), lambda i: (i, 0))
    full = lambda a: pl.BlockSpec(a.shape, lambda i: (0,) * a.ndim)
    earlier = jnp.asarray(np.triu(np.ones((tm, tm), np.float32), k=1), BF16)
    return pl.pallas_call(
        _out_route_kernel,
        grid=(T // tm,),
        in_specs=[row(512), row(512), row(D_MODEL), full(wo), full(g2),
                  full(wr_hi), full(wr_lo), full(br), full(earlier)],
        out_specs=[pl.BlockSpec(memory_space=pl.ANY), pl.BlockSpec(memory_space=pl.ANY),
                   pl.BlockSpec((ROUTE_ROWS, tm), lambda i: (0, i)),
                   pl.BlockSpec((N_EXPERTS, LANES), lambda i: (0, 0))],
        out_shape=[jax.ShapeDtypeStruct((T, SLAB_SUBLANES, LANES), F32),
                   jax.ShapeDtypeStruct((T, SLAB_SUBLANES, LANES), F32),
                   jax.ShapeDtypeStruct((ROUTE_ROWS, T), F32),
                   jax.ShapeDtypeStruct((N_EXPERTS, LANES), F32)],
        scratch_shapes=[pltpu.VMEM((2, 2, tm, D_MODEL), F32), pltpu.SemaphoreType.DMA((2,))],
        compiler_params=pltpu.CompilerParams(
            dimension_semantics=("arbitrary",), vmem_limit_bytes=VMEM_LIMIT,
            has_side_effects=True),
        name="out_route",
    )(oa, ob, x2, wo, g2, wr_hi, wr_lo, br, earlier)


def _row_copy(src, s, dst, d, sem):
    return pltpu.make_async_copy(src.at[pl.ds(s, 1)], dst.at[pl.ds(d, 1)], sem)


def _dispatch_kernel(dest_ref, pad_start_ref, pad_cnt_ref, nused_ref, hn_ref, xs_ref,
                     zero_ref, sem, zsem):
    i = pl.program_id(0)
    n = pl.num_programs(0)
    tm = EXPERT_PAD
    n_tok = dest_ref.shape[0] // TOP_K
    base = i * TM_DISPATCH
    for p in range(TM_DISPATCH * TOP_K):
        r, k = divmod(p, TOP_K)
        _row_copy(hn_ref, r, xs_ref, dest_ref[base + k * n_tok + r], sem).start(priority=p % 2)
    for _ in range(TOP_K):
        pltpu.make_async_copy(hn_ref, xs_ref.at[pl.ds(0, TM_DISPATCH)], sem).wait()

    @pl.when(i == n - 1)
    def _():
        zero_ref[...] = jnp.zeros_like(zero_ref)
        sub = 8
        for e in range(N_EXPERTS):
            start, cnt = pad_start_ref[e], pad_cnt_ref[e]
            head = cnt & (sub - 1)
            body0 = start + head
            nbody = cnt // sub

            def head_copy(r):
                return _row_copy(zero_ref, 0, xs_ref, start + r, zsem)

            def body_copy(c):
                return pltpu.make_async_copy(zero_ref.at[pl.ds(0, sub)],
                                             xs_ref.at[pl.ds(body0 + c * sub, sub)], zsem)

            for copy, count in ((head_copy, head), (body_copy, nbody)):
                def zissue(r, carry, copy=copy):
                    copy(r).start()
                    return carry

                def zwait(r, carry, copy=copy):
                    copy(r).wait()
                    return carry

                lax.fori_loop(0, count, zissue, 0)
                lax.fori_loop(0, count, zwait, 0)

        def tail_copy(t):
            return pltpu.make_async_copy(zero_ref, xs_ref.at[pl.ds(t * tm, tm)], zsem)

        def tissue(t, carry):
            tail_copy(t).start()
            return carry

        def twait(t, carry):
            tail_copy(t).wait()
            return carry

        n_tiles = xs_ref.shape[0] // tm
        lax.fori_loop(nused_ref[0], n_tiles, tissue, 0)
        lax.fori_loop(nused_ref[0], n_tiles, twait, 0)


def _dispatch(dest, pad_start, pad_cnt, n_used, hn, n_rows):
    T = hn.shape[0]
    any_spec = pl.BlockSpec(memory_space=pl.ANY)
    return pl.pallas_call(
        _dispatch_kernel,
        grid_spec=pltpu.PrefetchScalarGridSpec(
            num_scalar_prefetch=4,
            grid=(T // TM_DISPATCH,),
            in_specs=[pl.BlockSpec((TM_DISPATCH, SLAB_SUBLANES, LANES), lambda i, *_: (i, 0, 0))],
            out_specs=any_spec,
            scratch_shapes=[pltpu.VMEM((EXPERT_PAD, SLAB_SUBLANES, LANES), F32),
                            pltpu.SemaphoreType.DMA(()),
                            pltpu.SemaphoreType.DMA(())]),
        out_shape=jax.ShapeDtypeStruct((n_rows, SLAB_SUBLANES, LANES), F32),
        compiler_params=pltpu.CompilerParams(
            dimension_semantics=("arbitrary",), has_side_effects=True,
            vmem_limit_bytes=VMEM_LIMIT),
        name="dispatch",
    )(dest, pad_start, pad_cnt, n_used, hn)


def _experts_kernel(row0_ref, nfull_ref, tail_ref, xs_ref, wg_ref, bg_ref, wl_ref, bl_ref, wd_ref,
                    bd_ref, ys_ref, wg_bf, wl_bf, wd_bf, xbuf, ybuf, xsem, ysem, wf32, wsem):
    e = pl.program_id(0)
    tm, half = TM_EXPERT, EXPERT_PAD
    r0, nf, tail = row0_ref[e], nfull_ref[e], tail_ref[e]
    nt = nf + tail

    def x_load(row, slot, rows):
        return _SlabTileCopy(xbuf.at[slot, pl.ds(0, rows)], xs_ref.at[pl.ds(row, rows)],
                             xsem.at[slot], to_slab=False)

    def y_store(row, slot, rows):
        return _SlabTileCopy(ybuf.at[slot, pl.ds(0, rows)], ys_ref.at[pl.ds(row, rows)],
                             ysem.at[slot], to_slab=True)

    def start_first_load(expert):
        @pl.when(nfull_ref[expert] > 0)
        def _():
            x_load(row0_ref[expert], 0, tm).start()

        @pl.when((nfull_ref[expert] == 0) & (tail_ref[expert] > 0))
        def _():
            x_load(row0_ref[expert], 0, half).start()

    def compute(slot, rows):
        x = xbuf[slot, :rows].astype(BF16)
        glu = jnp.minimum(jnp.dot(x, wg_bf[...], preferred_element_type=F32) + bg_ref[0],
                          SWIGLU_LIMIT)
        lin = jnp.clip(jnp.dot(x, wl_bf[...], preferred_element_type=F32) + bl_ref[0],
                       -SWIGLU_LIMIT, SWIGLU_LIMIT)
        hid = glu * jax.nn.sigmoid(SWIGLU_ALPHA * glu) * (lin + 1.0)
        ybuf[slot, :rows] = (jnp.dot(hid.astype(BF16), wd_bf[...], preferred_element_type=F32)
                             + bd_ref[0])

    def w_fetch(expert):
        return [pltpu.make_async_copy(w.at[expert], wf32.at[i], wsem.at[i])
                for i, w in enumerate((wg_ref, wl_ref, wd_ref))]

    @pl.when(e == 0)
    def _():
        start_first_load(0)
        for cp in w_fetch(0):
            cp.start()

    for cp, w_bf, i in zip(w_fetch(e), (wg_bf, wl_bf, wd_bf), range(3)):
        cp.wait()
        w_bf[...] = wf32[i].astype(BF16)

    has_next = e + 1 < pl.num_programs(0)

    def fetch_next(i):
        @pl.when(has_next)
        def _():
            w_fetch(e + 1)[i].start()

    fetch_next(0)

    def full_tile(j, carry):
        slot = j % 2

        @pl.when(j + 1 < nf)
        def _():
            x_load(r0 + (j + 1) * tm, 1 - slot, tm).start()

        @pl.when((j + 1 == nf) & (tail > 0))
        def _():
            x_load(r0 + (j + 1) * tm, 1 - slot, half).start()

        x_load(r0 + j * tm, slot, tm).wait()

        @pl.when(j >= 2)
        def _():
            y_store(r0 + (j - 2) * tm, slot, tm).wait()

        compute(slot, tm)
        y_store(r0 + j * tm, slot, tm).start()
        for i in (1, 2):
            @pl.when(j == i - 1)
            def _():
                fetch_next(i)
        return carry

    lax.fori_loop(0, nf, full_tile, 0)
    for i in (1, 2):
        @pl.when(nf < i)
        def _():
            fetch_next(i)

    @pl.when(tail > 0)
    def _():
        slot = nf % 2
        x_load(r0 + nf * tm, slot, half).wait()

        @pl.when(nf >= 2)
        def _():
            y_store(r0 + (nf - 2) * tm, slot, tm).wait()

        compute(slot, half)
        y_store(r0 + nf * tm, slot, half).start()

    @pl.when(e + 1 < pl.num_programs(0))
    def _():
        start_first_load(e + 1)

    @pl.when(nt >= 2)
    def _():
        y_store(r0 + (nt - 2) * tm, nt % 2, tm).wait()

    @pl.when((nt >= 1) & (tail == 0))
    def _():
        y_store(r0 + (nt - 1) * tm, (nt - 1) % 2, tm).wait()

    @pl.when(tail > 0)
    def _():
        y_store(r0 + nf * tm, nf % 2, half).wait()

    @pl.when(e == pl.num_programs(0) - 1)
    def _():
        ybuf[0] = jnp.zeros((tm, D_MODEL), F32)
        first = (r0 + nf * tm + tail * half) // half
        n_pieces = ys_ref.shape[0] // half

        def fill_issue(t, carry):
            y_store(t * half, 0, half).start()
            return carry

        def fill_wait(t, carry):
            y_store(t * half, 0, half).wait()
            return carry

        lax.fori_loop(first, n_pieces, fill_issue, 0)
        lax.fori_loop(first, n_pieces, fill_wait, 0)


def _experts(row0, nfull, tail, xs, w_glu, b_glu, w_lin, b_lin, w_down, b_down):
    tm = TM_EXPERT
    any_spec = pl.BlockSpec(memory_space=pl.ANY)
    bspec = lambda: pl.BlockSpec((1, 1, D_FF), lambda e, *_: (e, 0, 0))
    return pl.pallas_call(
        _experts_kernel,
        grid_spec=pltpu.PrefetchScalarGridSpec(
            num_scalar_prefetch=3,
            grid=(N_EXPERTS,),
            in_specs=[any_spec, any_spec, bspec(), any_spec, bspec(), any_spec, bspec()],
            out_specs=any_spec,
            scratch_shapes=[pltpu.VMEM((D_MODEL, D_FF), BF16)] * 3
            + [pltpu.VMEM((2, tm, D_MODEL), F32), pltpu.VMEM((2, tm, D_MODEL), F32),
               pltpu.SemaphoreType.DMA((2,)), pltpu.SemaphoreType.DMA((2,)),
               pltpu.VMEM((3, D_MODEL, D_FF), F32), pltpu.SemaphoreType.DMA((3,))]),
        out_shape=jax.ShapeDtypeStruct(xs.shape, F32),
        compiler_params=pltpu.CompilerParams(
            dimension_semantics=("arbitrary",), vmem_limit_bytes=VMEM_LIMIT,
            has_side_effects=True),
        name="experts",
    )(row0, nfull, tail, xs, w_glu, b_glu.reshape(N_EXPERTS, 1, D_FF),
      w_lin, b_lin.reshape(N_EXPERTS, 1, D_FF), w_down, b_down.reshape(N_EXPERTS, 1, D_MODEL))


def _combine_kernel(dest_ref, gate_ref, ys_ref, h1_ref, g_ref, out_ref, buf, obuf, sems, osems):
    s = pl.program_id(0)
    tm = TM_COMBINE
    n = out_ref.shape[0] // tm
    n_tok = dest_ref.shape[0] // TOP_K

    def out_store(tile, slot):
        return _SlabTileCopy(out_ref.at[pl.ds(tile * tm, tm)], obuf.at[slot], osems.at[slot],
                             to_slab=False)

    def issue_row(slot, r):
        for k in range(TOP_K):
            _row_copy(ys_ref, dest_ref[s * tm + k * n_tok + r], buf.at[slot, k], r,
                      sems.at[slot]).start(priority=k % 2)

    def reduce_row(slot, r):
        acc = h1_ref[r]
        for k in range(TOP_K):
            acc = acc + gate_ref[(s - 1) * tm + k * n_tok + r] * buf[slot, k, r]
        ss = jnp.sum(jnp.sum(acc * acc, axis=1, keepdims=True), axis=0, keepdims=True)
        obuf[slot, r] = acc * lax.rsqrt(ss * (1.0 / D_MODEL) + NORM_EPS) * g_ref[0]

    def step_body(gather_slot, reduce_slot):
        for r in range(tm):
            if gather_slot is not None:
                issue_row(gather_slot, r)
            if reduce_slot is not None:
                reduce_row(reduce_slot, r)
        if reduce_slot is not None:
            out_store(s - 1, reduce_slot).start()

    for parity in range(2):
        other = 1 - parity

        @pl.when(s % 2 == parity)
        def _():
            @pl.when(s < n)
            def _():
                step_body(parity, None)

            @pl.when(s >= 3)
            def _():
                out_store(s - 3, other).wait()

            @pl.when(s > 0)
            def _():
                for k in range(TOP_K):
                    pltpu.make_async_copy(ys_ref.at[pl.ds(0, tm)], buf.at[other, k],
                                          sems.at[other]).wait()
                step_body(None, other)

            if n % 2 == parity:
                @pl.when(s == n)
                def _():
                    if n >= 2:
                        out_store(n - 2, parity).wait()
                    out_store(n - 1, other).wait()


def _combine(dest, ys3, h13, gate, fg):
    T = h13.shape[0]
    tm = TM_COMBINE
    slab = (SLAB_SUBLANES, LANES)
    prev_tile = lambda s, *_: (jnp.maximum(s - 1, 0), 0, 0)
    return pl.pallas_call(
        _combine_kernel,
        grid_spec=pltpu.PrefetchScalarGridSpec(
            num_scalar_prefetch=2,
            grid=(T // tm + 1,),
            in_specs=[pl.BlockSpec(memory_space=pl.ANY),
                      pl.BlockSpec((tm,) + slab, prev_tile),
                      pl.BlockSpec((1,) + slab, lambda s, *_: (0, 0, 0))],
            out_specs=pl.BlockSpec(memory_space=pl.ANY),
            scratch_shapes=[pltpu.VMEM((2, TOP_K, tm) + slab, F32),
                            pltpu.VMEM((2, tm) + slab, F32),
                            pltpu.SemaphoreType.DMA((2,)),
                            pltpu.SemaphoreType.DMA((2,))]),
        out_shape=jax.ShapeDtypeStruct((T, D_MODEL), F32),
        compiler_params=pltpu.CompilerParams(
            dimension_semantics=("arbitrary",), vmem_limit_bytes=VMEM_LIMIT,
            has_side_effects=True),
        name="combine",
    )(dest, gate, ys3, h13, fg.reshape((1,) + slab))


def _swap_halves(w):
    h = w.shape[-1] // 2
    return jnp.concatenate([w[..., h:], w[..., :h]], axis=-1)


def _layer(x2, batch, seq, norm1_g, w_in, w_alpha_up, b_alpha, gla_norm_g, swa_sinks, swa_norm_g,
           w_out, norm2_g, w_router, b_router, w_glu, b_glu, w_lin, b_lin, w_down, b_down):
    T = x2.shape[0]
    kb_w, vb_w = w_in[:, 2064:2192], w_in[:, 2192:2320]
    w_z = w_in[:, 1536:1552]
    w_cat = jnp.concatenate([
        w_in[:, 0:1536], w_in[:, 1552:2064],
        kb_w, _swap_halves(kb_w), vb_w, _swap_halves(vb_w),
        jnp.pad(jnp.tile(w_z, (1, Z_PIECES)), [(0, 0), (0, LANES - Z_PIECES * GLA_RANK)]),
    ], axis=1).astype(BF16)
    wup_hi = w_alpha_up.astype(BF16)
    wup_lo = (w_alpha_up - wup_hi.astype(F32)).astype(BF16)
    wup_cat = jnp.pad(jnp.concatenate([wup_hi, wup_hi, wup_hi, wup_lo, wup_lo], axis=0),
                      [(0, LANES - Z_PIECES * GLA_RANK), (0, 0)])

    qk, vr, la, qb, kv = _in_proj(
        x2, norm1_g.reshape(1, -1), w_cat, wup_cat, b_alpha.reshape(1, -1))
    oa = _gla(qk, vr, la, gla_norm_g.reshape(1, -1), batch, seq)
    ob = _swa(qb, kv, swa_sinks, jnp.tile(swa_norm_g, 2).reshape(1, -1), batch, seq)

    wo = w_out.astype(BF16)
    wr_t = w_router.T
    wr_hi = wr_t.astype(BF16)
    wr_lo = (wr_t - wr_hi.astype(F32)).astype(BF16)
    h1, hn, route, cnt = _out_route(oa, ob, x2, wo, norm2_g.reshape(1, -1),
                                    wr_hi, wr_lo, b_router.reshape(-1, 1))

    tm, pad = TM_EXPERT, EXPERT_PAD
    n_rows = T * TOP_K + N_EXPERTS * pad
    counts = cnt[:, 0].astype(jnp.int32)
    padded = (counts + pad - 1) // pad * pad
    pends = jnp.cumsum(padded)
    pstarts = pends - padded
    top_idx = route[0:TOP_K].astype(jnp.int32)
    gate = route[TOP_K:2 * TOP_K].reshape(-1)
    rank = route[2 * TOP_K:3 * TOP_K].astype(jnp.int32)
    experts = jnp.arange(N_EXPERTS)[:, None, None]
    seg_start = jnp.sum(jnp.where(top_idx[None] == experts, pstarts[:, None, None], 0), axis=0)
    dest = (seg_start + rank).reshape(-1)
    n_used = (pends[-1] // pad).reshape(1)
    xs = _dispatch(dest, pstarts + counts, padded - counts, n_used, hn, n_rows)
    ys = _experts(pstarts, padded // tm, padded % tm // pad, xs,
                  w_glu, b_glu, w_lin, b_lin, w_down, b_down)
    return dest, ys, h1, gate


def kernel(x, norm1_g, w_in, w_alpha_up, b_alpha, gla_norm_g, swa_sinks, swa_norm_g, w_out,
           norm2_g, w_router, b_router, w_glu, b_glu, w_lin, b_lin, w_down, b_down, final_g):
    batch, seq, d = x.shape
    assert norm1_g.shape[0] == 1, "single-layer problem"
    x2 = x.reshape(batch * seq, d)
    dest, ys, h1, gate = _layer(
        x2, batch, seq, norm1_g[0], w_in[0], w_alpha_up[0], b_alpha[0], gla_norm_g[0],
        swa_sinks[0], swa_norm_g[0], w_out[0], norm2_g[0], w_router[0], b_router[0],
        w_glu[0], b_glu[0], w_lin[0], b_lin[0], w_down[0], b_down[0])
    out = _combine(dest, ys, h1, gate, final_g.reshape(1, -1))
    return out.reshape(batch, seq, d)
```

```python
import numpy as np
import jax
import jax.numpy as jnp
from jax import lax
from jax.experimental import pallas as pl
from jax.experimental.pallas import tpu as pltpu

F32 = jnp.float32
BF16 = jnp.bfloat16

D_MODEL = 1024
GLA_HEADS = 4
GLA_DK = 64
GLA_DV = 128
GLA_RANK = 16
GLA_GATE_TAU = 16.0
GLA_CHUNK = 64
SWA_HEADS = 8
SWA_KV_HEADS = 2
SWA_HEAD_DIM = 64
SWA_WINDOW = 128
N_EXPERTS = 32
TOP_K = 4
D_FF = 1024
SWIGLU_LIMIT = 7.0
SWIGLU_ALPHA = 1.702
NORM_EPS = 1e-5

LANES = 128
SLAB_SUBLANES = D_MODEL // LANES
VMEM_LIMIT = 56 * 1024 * 1024

TM_PROJ = 1024
TL_GLA = 1024
GLA_CUM_ROWS = 256
TM_ROUTE = 1024
TM_EXPERT = 512
EXPERT_PAD = 256
W_PIECES_PER_MATRIX = 2
TM_COMBINE = 256
SWA_SUB = 8
TM_DISPATCH = 256

NEG_BIG = -1e30
ROUTE_ROWS = 16

C_QA, C_KA, C_VA, C_RA, C_QB, C_KB, C_KBS, C_VB, C_VBS, C_Z, C_END = (
    0, 256, 512, 1024, 1536, 2048, 2176, 2304, 2432, 2560, 2688)
Z_PIECES = 5


def _split3(x):
    hi = x.astype(BF16).astype(F32)
    r = x - hi
    mid = r.astype(BF16).astype(F32)
    lo = (r - mid).astype(BF16).astype(F32)
    return hi, mid, lo


class _SlabTileCopy:
    def __init__(self, flat, slab, sem, to_slab):
        self.copies = []
        for c in range(SLAB_SUBLANES):
            pair = (flat.at[:, pl.ds(c * LANES, LANES)], slab.at[:, c, :])
            src, dst = pair if to_slab else pair[::-1]
            self.copies.append(pltpu.make_async_copy(src, dst, sem))

    def start(self):
        for cp in self.copies:
            cp.start()

    def wait(self):
        for cp in self.copies:
            cp.wait()


def _rms(x, g):
    return x * lax.rsqrt(jnp.mean(x * x, axis=-1, keepdims=True) + NORM_EPS) * g


def _in_proj_kernel(x_ref, g_ref, w_ref, wup_ref, ba_ref,
                    qk_ref, vr_ref, la_ref, qb_ref, kv_ref):
    u = _rms(x_ref[...], g_ref[...]).astype(BF16)

    def proj(c0, c1):
        return jnp.dot(u, w_ref[:, c0:c1], preferred_element_type=F32)

    qk_ref[...] = proj(C_QA, C_VA).astype(BF16)
    vr_ref[:, :C_RA - C_VA] = proj(C_VA, C_RA).astype(BF16)
    vr_ref[:, C_RA - C_VA:] = proj(C_RA, C_QB).astype(BF16)
    qb_ref[...] = proj(C_QB, C_KB).astype(BF16)
    kv_ref[...] = proj(C_KB, C_Z).astype(BF16)
    z = proj(C_Z, C_END)
    hi, mid, lo = _split3(z)
    piece = lax.broadcasted_iota(jnp.int32, z.shape, 1) // GLA_RANK
    zc = jnp.where((piece == 0) | (piece == 3), hi, jnp.where(piece == 2, lo, mid)).astype(BF16)
    y = jnp.dot(zc, wup_ref[...], preferred_element_type=F32) + ba_ref[...]
    log_sig = jnp.minimum(y, 0.0) - jnp.log1p(jnp.exp(-jnp.abs(y)))
    la_ref[...] = log_sig * (1.0 / GLA_GATE_TAU)


def _in_proj(x2, g1, w_cat, wup_p, ba_p):
    T = x2.shape[0]
    tm = TM_PROJ
    row = lambda w: pl.BlockSpec((tm, w), lambda i: (i, 0))
    full = lambda a: pl.BlockSpec(a.shape, lambda i: (0,) * a.ndim)
    outs = [(512, BF16), (1024, BF16), (256, F32), (512, BF16), (512, BF16)]
    return pl.pallas_call(
        _in_proj_kernel,
        grid=(T // tm,),
        in_specs=[row(D_MODEL), full(g1), full(w_cat), full(wup_p), full(ba_p)],
        out_specs=[row(w) for w, _ in outs],
        out_shape=[jax.ShapeDtypeStruct((T, w), dt) for w, dt in outs],
        compiler_params=pltpu.CompilerParams(
            dimension_semantics=("arbitrary",), vmem_limit_bytes=VMEM_LIMIT),
        name="in_proj",
    )(x2, g1, w_cat, wup_p, ba_p)


def _gla_kernel(qk_ref, vr_ref, la_ref, g_ref, cum_ref, o_ref, st_ref):
    @pl.when(pl.program_id(1) == 0)
    def _():
        st_ref[...] = jnp.zeros_like(st_ref)

    tl = TL_GLA
    c = GLA_CHUNK
    kw = GLA_HEADS * GLA_DK
    vw = GLA_HEADS * GLA_DV
    causal = (lax.broadcasted_iota(jnp.int32, (c, c), 0)
              >= lax.broadcasted_iota(jnp.int32, (c, c), 1))
    low_half = lax.broadcasted_iota(jnp.int32, (c, LANES), 1) < GLA_DK
    g = g_ref[...]
    b_groups = []
    for grp in range(tl // GLA_CUM_ROWS):
        la = la_ref[grp * GLA_CUM_ROWS:(grp + 1) * GLA_CUM_ROWS, :]
        pieces = jnp.concatenate([p.astype(BF16) for p in _split3(la)], axis=1)
        b3 = jnp.dot(cum_ref[...], pieces, preferred_element_type=F32)
        b_groups.append(b3[:, :kw] + b3[:, kw:2 * kw] + b3[:, 2 * kw:])
    for ch in range(tl // c):
        rows = slice(ch * c, (ch + 1) * c)
        in_grp = (ch * c) % GLA_CUM_ROWS
        b = b_groups[(ch * c) // GLA_CUM_ROWS][in_grp:in_grp + c]
        b_last = b[c - 1:c]
        qf = qk_ref[rows, :kw].astype(F32)
        kf = qk_ref[rows, kw:].astype(F32)
        q_e = (qf * jnp.exp(b) * (GLA_DK ** -0.5)).astype(BF16)
        k_e = (kf * jnp.exp(-b)).astype(BF16)
        k_t = (kf * jnp.exp(b_last - b)).astype(BF16)
        decay = jnp.exp(b_last)
        for h in range(GLA_HEADS):
            ps = slice((h // 2) * LANES, (h // 2 + 1) * LANES)
            mine = low_half if h % 2 == 0 else ~low_half
            qp, kp = q_e[:, ps], k_e[:, ps]
            qh = jnp.where(mine, qp, jnp.zeros_like(qp))
            kth = jnp.where(mine, k_t[:, ps], jnp.zeros_like(qp))
            vs = slice(h * GLA_DV, (h + 1) * GLA_DV)
            vh = vr_ref[rows, vs]
            a = pl.dot(qh, kp, trans_b=True)
            a = jnp.where(causal, a, 0.0).astype(BF16)
            st = st_ref[h]
            o = (jnp.dot(a, vh, preferred_element_type=F32)
                 + pl.dot(qh, st.astype(BF16), trans_b=True))
            st_ref[h] = st * decay[:, ps] + pl.dot(vh, kth, trans_a=True)
            rh = vr_ref[rows, slice(vw + h * GLA_DV, vw + (h + 1) * GLA_DV)].astype(F32)
            o = _rms(o, g) * (rh * jax.nn.sigmoid(rh))
            o_ref[rows, vs] = o.astype(BF16)


def _gla(qk, vr, la, g, batch, seq):
    tl = TL_GLA
    nl = seq // tl
    row = lambda w: pl.BlockSpec((tl, w), lambda b, i: (b * nl + i, 0))
    r = np.arange(GLA_CUM_ROWS)
    cum = jnp.asarray((r[None, :] <= r[:, None])
                      & (r[None, :] // GLA_CHUNK == r[:, None] // GLA_CHUNK), BF16)
    return pl.pallas_call(
        _gla_kernel,
        grid=(batch, nl),
        in_specs=[row(512), row(1024), row(256),
                  pl.BlockSpec((1, GLA_DV), lambda b, i: (0, 0)),
                  pl.BlockSpec((GLA_CUM_ROWS, GLA_CUM_ROWS), lambda b, i: (0, 0))],
        out_specs=row(512),
        out_shape=jax.ShapeDtypeStruct((batch * seq, 512), BF16),
        scratch_shapes=[pltpu.VMEM((GLA_HEADS, GLA_DV, LANES), F32)],
        compiler_params=pltpu.CompilerParams(
            dimension_semantics=("arbitrary", "arbitrary"), vmem_limit_bytes=VMEM_LIMIT),
        name="gla",
    )(qk, vr, la, g, cum)


def _swa_kernel(sink_ref, bias0_ref, bias_ref, q_ref, kvc_ref, kvp_ref, g_ref, o_ref):
    w = SWA_WINDOW
    scale = jnp.asarray(SWA_HEAD_DIM ** -0.5, BF16)
    kv_lane = lax.broadcasted_iota(jnp.int32, (2 * w, LANES), 1)
    lane_lo = kv_lane < SWA_HEAD_DIM
    out_lo = lax.broadcasted_iota(jnp.int32, (w, LANES), 1) < SWA_HEAD_DIM
    ones_hi = jnp.where(kv_lane == SWA_HEAD_DIM, 1.0, 0.0).astype(BF16)
    ones_lo = jnp.where(kv_lane == 0, 1.0, 0.0).astype(BF16)
    g = g_ref[...]

    for sb in range(SWA_SUB):
        rows = slice(sb * w, (sb + 1) * w)
        if sb == 0:
            kv = jnp.concatenate([kvp_ref[...], kvc_ref[0:w, :]], axis=0)
            bias = bias0_ref
        else:
            kv = kvc_ref[(sb - 1) * w:(sb + 1) * w, :]
            bias = bias_ref
        k = kv[:, 0:LANES] * scale
        ks = kv[:, LANES:2 * LANES] * scale
        v, vs = kv[:, 2 * LANES:3 * LANES], kv[:, 3 * LANES:]
        zero = jnp.zeros_like(k)
        k_low = [jnp.where(lane_lo, k, zero), jnp.where(lane_lo, ks, zero)]
        k_high = [jnp.where(lane_lo, zero, ks), jnp.where(lane_lo, zero, k)]
        v_low = [jnp.where(lane_lo, v, zero) + ones_hi, jnp.where(lane_lo, vs, zero) + ones_hi]
        v_high = [jnp.where(lane_lo, zero, vs) + ones_lo, jnp.where(lane_lo, zero, v) + ones_lo]

        def weights(s, head):
            s = s + bias[0, head]
            sink = sink_ref[head]
            m = jnp.maximum(jnp.max(s, axis=-1, keepdims=True), sink)
            return jnp.exp(s - m).astype(BF16), jnp.exp(sink - m)

        for pair in range(SWA_HEADS // 2):
            j = (2 * pair) // (SWA_HEADS // SWA_KV_HEADS)
            cols = slice(pair * LANES, (pair + 1) * LANES)
            qp = q_ref[rows, cols]
            e0, sink0 = weights(pl.dot(qp, k_low[j], trans_b=True), 2 * pair)
            e1, sink1 = weights(pl.dot(qp, k_high[j], trans_b=True), 2 * pair + 1)
            oa = jnp.dot(e0, v_low[j], preferred_element_type=F32)
            ob = jnp.dot(e1, v_high[j], preferred_element_type=F32)
            inv0 = 1.0 / (oa[:, SWA_HEAD_DIM:SWA_HEAD_DIM + 1] + sink0)
            inv1 = 1.0 / (ob[:, 0:1] + sink1)
            o = jnp.where(out_lo, oa * inv0, ob * inv1)
            sq = o * o
            ms_lo = jnp.sum(jnp.where(out_lo, sq, 0.0), axis=-1, keepdims=True)
            ms_hi = jnp.sum(jnp.where(out_lo, 0.0, sq), axis=-1, keepdims=True)
            ms = jnp.where(out_lo, ms_lo, ms_hi) * (1.0 / SWA_HEAD_DIM)
            o = o * lax.rsqrt(ms + NORM_EPS) * g
            o_ref[rows, cols] = o.astype(BF16)


def _swa_bias():
    w = SWA_WINDOW
    slopes = 2.0 ** (-8.0 * np.arange(1, SWA_HEADS + 1, dtype=np.float64) / SWA_HEADS)
    rel = np.arange(w)[:, None] + w - np.arange(2 * w)[None, :]
    in_window = (rel >= 0) & (rel < w)
    exists = np.stack([np.arange(2 * w) >= w, np.ones(2 * w, bool)])
    valid = in_window[None] & exists[:, None, :]
    bias = -slopes[None, :, None, None] * rel[None, None].astype(np.float64)
    return np.where(valid[:, None], bias, NEG_BIG).astype(np.float32)


def _swa(qb, kv, sinks, g2, batch, seq):
    w = SWA_WINDOW
    rows = SWA_SUB * w
    nb = seq // rows
    cur = lambda: pl.BlockSpec((rows, 512), lambda b, n: (b * nb + n, 0))
    prev = pl.BlockSpec(
        (w, 512), lambda b, n: (jnp.maximum((b * nb + n) * SWA_SUB - 1, b * nb * SWA_SUB), 0))
    table = lambda index: pl.BlockSpec((1, SWA_HEADS, w, 2 * w), index)
    bias = jnp.asarray(_swa_bias())
    return pl.pallas_call(
        _swa_kernel,
        grid=(batch, nb),
        in_specs=[pl.BlockSpec(memory_space=pltpu.SMEM),
                  table(lambda b, n: (jnp.minimum(n, 1), 0, 0, 0)),
                  table(lambda b, n: (1, 0, 0, 0)),
                  cur(), cur(), prev,
                  pl.BlockSpec((1, LANES), lambda b, n: (0, 0))],
        out_specs=cur(),
        out_shape=jax.ShapeDtypeStruct((batch * seq, 512), BF16),
        compiler_params=pltpu.CompilerParams(
            dimension_semantics=("arbitrary", "arbitrary"), vmem_limit_bytes=VMEM_LIMIT),
        name="swa",
    )(sinks, bias, bias, qb, kv, kv, g2)


def _out_route_kernel(oa_ref, ob_ref, x_ref, wo_ref, g_ref, wrh_ref, wrl_ref, br_ref,
                      earlier_ref, h13_ref, hn3_ref, route_ref, cnt_ref, hbuf, hsem):
    i = pl.program_id(0)
    tm = TM_ROUTE

    @pl.when(i == 0)
    def _():
        cnt_ref[...] = jnp.zeros_like(cnt_ref)

    def stores(step, slot):
        rows = pl.ds(step * tm, tm)
        return [_SlabTileCopy(hbuf.at[slot, which], out.at[rows], hsem.at[slot], to_slab=True)
                for which, out in enumerate((h13_ref, hn3_ref))]

    slot = i % 2

    @pl.when(i >= 2)
    def _():
        for st in stores(i - 2, slot):
            st.wait()

    h1 = (x_ref[...]
          + jnp.dot(oa_ref[...], wo_ref[:oa_ref.shape[1], :], preferred_element_type=F32)
          + jnp.dot(ob_ref[...], wo_ref[oa_ref.shape[1]:, :], preferred_element_type=F32))
    hn = _rms(h1, g_ref[...])
    hbuf[slot, 0] = h1
    hbuf[slot, 1] = hn
    for st in stores(i, slot):
        st.start()

    @pl.when(i == pl.num_programs(0) - 1)
    def _():
        @pl.when(i >= 1)
        def _():
            for st in stores(i - 1, 1 - slot):
                st.wait()
        for st in stores(i, slot):
            st.wait()

    hn_hi = hn.astype(BF16)
    hn_lo = (hn - hn_hi.astype(F32)).astype(BF16)
    logits = (pl.dot(wrh_ref[...], hn_hi, trans_b=True)
              + pl.dot(wrh_ref[...], hn_lo, trans_b=True)
              + pl.dot(wrl_ref[...], hn_hi, trans_b=True)) + br_ref[...]

    eid = lax.broadcasted_iota(jnp.int32, (N_EXPERTS, tm), 0)
    work = logits
    vals, idxs, sels = [], [], []
    for _ in range(TOP_K):
        m = jnp.max(work, axis=0, keepdims=True)
        idx = jnp.min(jnp.where(work == m, eid, N_EXPERTS), axis=0, keepdims=True)
        sel = eid == idx
        vals.append(m)
        idxs.append(idx)
        sels.append(sel)
        work = jnp.where(sel, -3e38, work)
    exps = [jnp.exp(v - vals[0]) for v in vals]
    inv_den = 1.0 / (exps[0] + exps[1] + exps[2] + exps[3])

    multihot = jnp.where(sels[0] | sels[1] | sels[2] | sels[3], 1.0, 0.0)
    before = (jnp.dot(multihot.astype(BF16), earlier_ref[...], preferred_element_type=F32)
              + cnt_ref[:, 0:1])
    ranks = [jnp.sum(jnp.where(s, before, 0.0), axis=0, keepdims=True) for s in sels]
    route_ref[...] = jnp.concatenate(
        [i.astype(F32) for i in idxs] + [e * inv_den for e in exps] + ranks
        + [jnp.zeros((ROUTE_ROWS - 3 * TOP_K, tm), F32)], axis=0)
    cnt_ref[...] += jnp.sum(multihot, axis=1, keepdims=True)


def _out_route(oa, ob, x2, wo, g2, wr_hi, wr_lo, br):
    T = x2.shape[0]
    tm = TM_ROUTE
    row = lambda w: pl.BlockSpec((tm, w), lambda i: (i, 0))
    full = lambda a: pl.BlockSpec(a.shape, lambda i: (0,) * a.ndim)
    earlier = jnp.asarray(np.triu(np.ones((tm, tm), np.float32), k=1), BF16)
    return pl.pallas_call(
        _out_route_kernel,
        grid=(T // tm,),
        in_specs=[row(512), row(512), row(D_MODEL), full(wo), full(g2),
                  full(wr_hi), full(wr_lo), full(br), full(earlier)],
        out_specs=[pl.BlockSpec(memory_space=pl.ANY), pl.BlockSpec(memory_space=pl.ANY),
                   pl.BlockSpec((ROUTE_ROWS, tm), lambda i: (0, i)),
                   pl.BlockSpec((N_EXPERTS, LANES), lambda i: (0, 0))],
        out_shape=[jax.ShapeDtypeStruct((T, SLAB_SUBLANES, LANES), F32),
                   jax.ShapeDtypeStruct((T, SLAB_SUBLANES, LANES), F32),
                   jax.ShapeDtypeStruct((ROUTE_ROWS, T), F32),
                   jax.ShapeDtypeStruct((N_EXPERTS, LANES), F32)],
        scratch_shapes=[pltpu.VMEM((2, 2, tm, D_MODEL), F32), pltpu.SemaphoreType.DMA((2,))],
        compiler_params=pltpu.CompilerParams(
            dimension_semantics=("arbitrary",), vmem_limit_bytes=VMEM_LIMIT,
            has_side_effects=True),
        name="out_route",
    )(oa, ob, x2, wo, g2, wr_hi, wr_lo, br, earlier)


def _row_copy(src, s, dst, d, sem):
    return pltpu.make_async_copy(src.at[pl.ds(s, 1)], dst.at[pl.ds(d, 1)], sem)


def _dispatch_kernel(dest_ref, pad_start_ref, pad_cnt_ref, nused_ref, hn_ref, xs_ref,
                     zero_ref, sem, zsem):
    i = pl.program_id(0)
    n = pl.num_programs(0)
    tm = EXPERT_PAD
    n_tok = dest_ref.shape[0] // TOP_K
    base = i * TM_DISPATCH
    for p in range(TM_DISPATCH * TOP_K):
        r, k = divmod(p, TOP_K)
        _row_copy(hn_ref, r, xs_ref, dest_ref[base + k * n_tok + r], sem).start(priority=p % 2)
    for _ in range(TOP_K):
        pltpu.make_async_copy(hn_ref, xs_ref.at[pl.ds(0, TM_DISPATCH)], sem).wait()

    @pl.when(i == n - 1)
    def _():
        zero_ref[...] = jnp.zeros_like(zero_ref)
        sub = 8
        for e in range(N_EXPERTS):
            start, cnt = pad_start_ref[e], pad_cnt_ref[e]
            head = cnt & (sub - 1)
            body0 = start + head
            nbody = cnt // sub

            def head_copy(r):
                return _row_copy(zero_ref, 0, xs_ref, start + r, zsem)

            def body_copy(c):
                return pltpu.make_async_copy(zero_ref.at[pl.ds(0, sub)],
                                             xs_ref.at[pl.ds(body0 + c * sub, sub)], zsem)

            for copy, count in ((head_copy, head), (body_copy, nbody)):
                def zissue(r, carry, copy=copy):
                    copy(r).start()
                    return carry

                def zwait(r, carry, copy=copy):
                    copy(r).wait()
                    return carry

                lax.fori_loop(0, count, zissue, 0)
                lax.fori_loop(0, count, zwait, 0)

        def tail_copy(t):
            return pltpu.make_async_copy(zero_ref, xs_ref.at[pl.ds(t * tm, tm)], zsem)

        def tissue(t, carry):
            tail_copy(t).start()
            return carry

        def twait(t, carry):
            tail_copy(t).wait()
            return carry

        n_tiles = xs_ref.shape[0] // tm
        lax.fori_loop(nused_ref[0], n_tiles, tissue, 0)
        lax.fori_loop(nused_ref[0], n_tiles, twait, 0)


def _dispatch(dest, pad_start, pad_cnt, n_used, hn, n_rows):
    T = hn.shape[0]
    any_spec = pl.BlockSpec(memory_space=pl.ANY)
    return pl.pallas_call(
        _dispatch_kernel,
        grid_spec=pltpu.PrefetchScalarGridSpec(
            num_scalar_prefetch=4,
            grid=(T // TM_DISPATCH,),
            in_specs=[pl.BlockSpec((TM_DISPATCH, SLAB_SUBLANES, LANES), lambda i, *_: (i, 0, 0))],
            out_specs=any_spec,
            scratch_shapes=[pltpu.VMEM((EXPERT_PAD, SLAB_SUBLANES, LANES), F32),
                            pltpu.SemaphoreType.DMA(()),
                            pltpu.SemaphoreType.DMA(())]),
        out_shape=jax.ShapeDtypeStruct((n_rows, SLAB_SUBLANES, LANES), F32),
        compiler_params=pltpu.CompilerParams(
            dimension_semantics=("arbitrary",), has_side_effects=True,
            vmem_limit_bytes=VMEM_LIMIT),
        name="dispatch",
    )(dest, pad_start, pad_cnt, n_used, hn)


def _experts_kernel(row0_ref, nfull_ref, tail_ref, xs_ref, wg_ref, bg_ref, wl_ref, bl_ref, wd_ref,
                    bd_ref, ys_ref, wg_bf, wl_bf, wd_bf, xbuf, ybuf, xsem, ysem, wf32, wsem):
    e = pl.program_id(0)
    tm, half = TM_EXPERT, EXPERT_PAD
    r0, nf, tail = row0_ref[e], nfull_ref[e], tail_ref[e]
    nt = nf + tail

    def x_load(row, slot, rows):
        return _SlabTileCopy(xbuf.at[slot, pl.ds(0, rows)], xs_ref.at[pl.ds(row, rows)],
                             xsem.at[slot], to_slab=False)

    def y_store(row, slot, rows):
        return _SlabTileCopy(ybuf.at[slot, pl.ds(0, rows)], ys_ref.at[pl.ds(row, rows)],
                             ysem.at[slot], to_slab=True)

    def start_first_load(expert):
        @pl.when(nfull_ref[expert] > 0)
        def _():
            x_load(row0_ref[expert], 0, tm).start()

        @pl.when((nfull_ref[expert] == 0) & (tail_ref[expert] > 0))
        def _():
            x_load(row0_ref[expert], 0, half).start()

    def compute(slot, rows):
        x = xbuf[slot, :rows].astype(BF16)
        glu = jnp.minimum(jnp.dot(x, wg_bf[...], preferred_element_type=F32) + bg_ref[0],
                          SWIGLU_LIMIT)
        lin = jnp.clip(jnp.dot(x, wl_bf[...], preferred_element_type=F32) + bl_ref[0],
                       -SWIGLU_LIMIT, SWIGLU_LIMIT)
        hid = glu * jax.nn.sigmoid(SWIGLU_ALPHA * glu) * (lin + 1.0)
        ybuf[slot, :rows] = (jnp.dot(hid.astype(BF16), wd_bf[...], preferred_element_type=F32)
                             + bd_ref[0])

    piece_rows = D_MODEL // W_PIECES_PER_MATRIX

    def w_fetch(expert):
        return [pltpu.make_async_copy(w.at[expert, pl.ds(p * piece_rows, piece_rows)],
                                      wf32.at[i, pl.ds(p * piece_rows, piece_rows)], wsem.at[i])
                for i, w in enumerate((wg_ref, wl_ref, wd_ref))
                for p in range(W_PIECES_PER_MATRIX)]

    n_pieces = 3 * W_PIECES_PER_MATRIX

    @pl.when(e == 0)
    def _():
        start_first_load(0)
        for cp in w_fetch(0):
            cp.start()

    pieces = w_fetch(e)
    for i, w_bf in enumerate((wg_bf, wl_bf, wd_bf)):
        for cp in pieces[i * W_PIECES_PER_MATRIX:(i + 1) * W_PIECES_PER_MATRIX]:
            cp.wait()
        w_bf[...] = wf32[i].astype(BF16)

    has_next = e + 1 < pl.num_programs(0)

    def fetch_next(c):
        @pl.when(has_next)
        def _():
            w_fetch(e + 1)[c].start()

    fetch_next(0)

    def full_tile(j, carry):
        slot = j % 2

        @pl.when(j + 1 < nf)
        def _():
            x_load(r0 + (j + 1) * tm, 1 - slot, tm).start()

        @pl.when((j + 1 == nf) & (tail > 0))
        def _():
            x_load(r0 + (j + 1) * tm, 1 - slot, half).start()

        x_load(r0 + j * tm, slot, tm).wait()

        @pl.when(j >= 2)
        def _():
            y_store(r0 + (j - 2) * tm, slot, tm).wait()

        compute(slot, tm)
        y_store(r0 + j * tm, slot, tm).start()
        for c in range(1, n_pieces):
            @pl.when(j == c - 1)
            def _():
                fetch_next(c)
        return carry

    lax.fori_loop(0, nf, full_tile, 0)
    for c in range(1, n_pieces):
        @pl.when(nf < c)
        def _():
            fetch_next(c)

    @pl.when(tail > 0)
    def _():
        slot = nf % 2
        x_load(r0 + nf * tm, slot, half).wait()

        @pl.when(nf >= 2)
        def _():
            y_store(r0 + (nf - 2) * tm, slot, tm).wait()

        compute(slot, half)
        y_store(r0 + nf * tm, slot, half).start()

    @pl.when(e + 1 < pl.num_programs(0))
    def _():
        start_first_load(e + 1)

    @pl.when(nt >= 2)
    def _():
        y_store(r0 + (nt - 2) * tm, nt % 2, tm).wait()

    @pl.when((nt >= 1) & (tail == 0))
    def _():
        y_store(r0 + (nt - 1) * tm, (nt - 1) % 2, tm).wait()

    @pl.when(tail > 0)
    def _():
        y_store(r0 + nf * tm, nf % 2, half).wait()

    @pl.when(e == pl.num_programs(0) - 1)
    def _():
        ybuf[0] = jnp.zeros((tm, D_MODEL), F32)
        first = (r0 + nf * tm + tail * half) // half
        n_pieces = ys_ref.shape[0] // half

        def fill_issue(t, carry):
            y_store(t * half, 0, half).start()
            return carry

        def fill_wait(t, carry):
            y_store(t * half, 0, half).wait()
            return carry

        lax.fori_loop(first, n_pieces, fill_issue, 0)
        lax.fori_loop(first, n_pieces, fill_wait, 0)


def _experts(row0, nfull, tail, xs, w_glu, b_glu, w_lin, b_lin, w_down, b_down):
    tm = TM_EXPERT
    any_spec = pl.BlockSpec(memory_space=pl.ANY)
    bspec = lambda: pl.BlockSpec((1, 1, D_FF), lambda e, *_: (e, 0, 0))
    return pl.pallas_call(
        _experts_kernel,
        grid_spec=pltpu.PrefetchScalarGridSpec(
            num_scalar_prefetch=3,
            grid=(N_EXPERTS,),
            in_specs=[any_spec, any_spec, bspec(), any_spec, bspec(), any_spec, bspec()],
            out_specs=any_spec,
            scratch_shapes=[pltpu.VMEM((D_MODEL, D_FF), BF16)] * 3
            + [pltpu.VMEM((2, tm, D_MODEL), F32), pltpu.VMEM((2, tm, D_MODEL), F32),
               pltpu.SemaphoreType.DMA((2,)), pltpu.SemaphoreType.DMA((2,)),
               pltpu.VMEM((3, D_MODEL, D_FF), F32), pltpu.SemaphoreType.DMA((3,))]),
        out_shape=jax.ShapeDtypeStruct(xs.shape, F32),
        compiler_params=pltpu.CompilerParams(
            dimension_semantics=("arbitrary",), vmem_limit_bytes=VMEM_LIMIT,
            has_side_effects=True),
        name="experts",
    )(row0, nfull, tail, xs, w_glu, b_glu.reshape(N_EXPERTS, 1, D_FF),
      w_lin, b_lin.reshape(N_EXPERTS, 1, D_FF), w_down, b_down.reshape(N_EXPERTS, 1, D_MODEL))


def _combine_kernel(dest_ref, gate_ref, ys_ref, h1_ref, g_ref, out_ref, buf, obuf, sems, osems):
    s = pl.program_id(0)
    tm = TM_COMBINE
    n = out_ref.shape[0] // tm
    n_tok = dest_ref.shape[0] // TOP_K

    def out_store(tile, slot):
        return _SlabTileCopy(out_ref.at[pl.ds(tile * tm, tm)], obuf.at[slot], osems.at[slot],
                             to_slab=False)

    def issue_row(slot, r):
        for k in range(TOP_K):
            _row_copy(ys_ref, dest_ref[s * tm + k * n_tok + r], buf.at[slot, k], r,
                      sems.at[slot]).start(priority=k % 2)

    def reduce_row(slot, r):
        acc = h1_ref[r]
        for k in range(TOP_K):
            acc = acc + gate_ref[(s - 1) * tm + k * n_tok + r] * buf[slot, k, r]
        ss = jnp.sum(jnp.sum(acc * acc, axis=1, keepdims=True), axis=0, keepdims=True)
        obuf[slot, r] = acc * lax.rsqrt(ss * (1.0 / D_MODEL) + NORM_EPS) * g_ref[0]

    def step_body(gather_slot, reduce_slot):
        for r in range(tm):
            if gather_slot is not None:
                issue_row(gather_slot, r)
            if reduce_slot is not None:
                reduce_row(reduce_slot, r)
        if reduce_slot is not None:
            out_store(s - 1, reduce_slot).start()

    for parity in range(2):
        other = 1 - parity

        @pl.when(s % 2 == parity)
        def _():
            @pl.when(s < n)
            def _():
                step_body(parity, None)

            @pl.when(s >= 3)
            def _():
                out_store(s - 3, other).wait()

            @pl.when(s > 0)
            def _():
                for k in range(TOP_K):
                    pltpu.make_async_copy(ys_ref.at[pl.ds(0, tm)], buf.at[other, k],
                                          sems.at[other]).wait()
                step_body(None, other)

            if n % 2 == parity:
                @pl.when(s == n)
                def _():
                    if n >= 2:
                        out_store(n - 2, parity).wait()
                    out_store(n - 1, other).wait()


def _combine(dest, ys3, h13, gate, fg):
    T = h13.shape[0]
    tm = TM_COMBINE
    slab = (SLAB_SUBLANES, LANES)
    prev_tile = lambda s, *_: (jnp.maximum(s - 1, 0), 0, 0)
    return pl.pallas_call(
        _combine_kernel,
        grid_spec=pltpu.PrefetchScalarGridSpec(
            num_scalar_prefetch=2,
            grid=(T // tm + 1,),
            in_specs=[pl.BlockSpec(memory_space=pl.ANY),
                      pl.BlockSpec((tm,) + slab, prev_tile),
                      pl.BlockSpec((1,) + slab, lambda s, *_: (0, 0, 0))],
            out_specs=pl.BlockSpec(memory_space=pl.ANY),
            scratch_shapes=[pltpu.VMEM((2, TOP_K, tm) + slab, F32),
                            pltpu.VMEM((2, tm) + slab, F32),
                            pltpu.SemaphoreType.DMA((2,)),
                            pltpu.SemaphoreType.DMA((2,))]),
        out_shape=jax.ShapeDtypeStruct((T, D_MODEL), F32),
        compiler_params=pltpu.CompilerParams(
            dimension_semantics=("arbitrary",), vmem_limit_bytes=VMEM_LIMIT,
            has_side_effects=True),
        name="combine",
    )(dest, gate, ys3, h13, fg.reshape((1,) + slab))


def _swap_halves(w):
    h = w.shape[-1] // 2
    return jnp.concatenate([w[..., h:], w[..., :h]], axis=-1)


def _layer(x2, batch, seq, norm1_g, w_in, w_alpha_up, b_alpha, gla_norm_g, swa_sinks, swa_norm_g,
           w_out, norm2_g, w_router, b_router, w_glu, b_glu, w_lin, b_lin, w_down, b_down):
    T = x2.shape[0]
    kb_w, vb_w = w_in[:, 2064:2192], w_in[:, 2192:2320]
    w_z = w_in[:, 1536:1552]
    w_cat = jnp.concatenate([
        w_in[:, 0:1536], w_in[:, 1552:2064],
        kb_w, _swap_halves(kb_w), vb_w, _swap_halves(vb_w),
        jnp.pad(jnp.tile(w_z, (1, Z_PIECES)), [(0, 0), (0, LANES - Z_PIECES * GLA_RANK)]),
    ], axis=1).astype(BF16)
    wup_hi = w_alpha_up.astype(BF16)
    wup_lo = (w_alpha_up - wup_hi.astype(F32)).astype(BF16)
    wup_cat = jnp.pad(jnp.concatenate([wup_hi, wup_hi, wup_hi, wup_lo, wup_lo], axis=0),
                      [(0, LANES - Z_PIECES * GLA_RANK), (0, 0)])

    qk, vr, la, qb, kv = _in_proj(
        x2, norm1_g.reshape(1, -1), w_cat, wup_cat, b_alpha.reshape(1, -1))
    oa = _gla(qk, vr, la, gla_norm_g.reshape(1, -1), batch, seq)
    ob = _swa(qb, kv, swa_sinks, jnp.tile(swa_norm_g, 2).reshape(1, -1), batch, seq)

    wo = w_out.astype(BF16)
    wr_t = w_router.T
    wr_hi = wr_t.astype(BF16)
    wr_lo = (wr_t - wr_hi.astype(F32)).astype(BF16)
    h1, hn, route, cnt = _out_route(oa, ob, x2, wo, norm2_g.reshape(1, -1),
                                    wr_hi, wr_lo, b_router.reshape(-1, 1))

    tm, pad = TM_EXPERT, EXPERT_PAD
    n_rows = T * TOP_K + N_EXPERTS * pad
    counts = cnt[:, 0].astype(jnp.int32)
    padded = (counts + pad - 1) // pad * pad
    pends = jnp.cumsum(padded)
    pstarts = pends - padded
    top_idx = route[0:TOP_K].astype(jnp.int32)
    gate = route[TOP_K:2 * TOP_K].reshape(-1)
    rank = route[2 * TOP_K:3 * TOP_K].astype(jnp.int32)
    experts = jnp.arange(N_EXPERTS)[:, None, None]
    seg_start = jnp.sum(jnp.where(top_idx[None] == experts, pstarts[:, None, None], 0), axis=0)
    dest = (seg_start + rank).reshape(-1)
    n_used = (pends[-1] // pad).reshape(1)
    xs = _dispatch(dest, pstarts + counts, padded - counts, n_used, hn, n_rows)
    ys = _experts(pstarts, padded // tm, padded % tm // pad, xs,
                  w_glu, b_glu, w_lin, b_lin, w_down, b_down)
    return dest, ys, h1, gate


def kernel(x, norm1_g, w_in, w_alpha_up, b_alpha, gla_norm_g, swa_sinks, swa_norm_g, w_out,
           norm2_g, w_router, b_router, w_glu, b_glu, w_lin, b_lin, w_down, b_down, final_g):
    batch, seq, d = x.shape
    assert norm1_g.shape[0] == 1, "single-layer problem"
    x2 = x.reshape(batch * seq, d)
    dest, ys, h1, gate = _layer(
        x2, batch, seq, norm1_g[0], w_in[0], w_alpha_up[0], b_alpha[0], gla_norm_g[0],
        swa_sinks[0], swa_norm_g[0], w_out[0], norm2_g[0], w_router[0], b_router[0],
        w_glu[0], b_glu[0], w_lin[0], b_lin[0], w_down[0], b_down[0])
    out = _combine(dest, ys, h1, gate, final_g.reshape(1, -1))
    return out.reshape(batch, seq, d)
```

```python
import numpy as np
import jax
import jax.numpy as jnp
from jax import lax
from jax.experimental import pallas as pl
from jax.experimental.pallas import tpu as pltpu

F32 = jnp.float32
BF16 = jnp.bfloat16

D_MODEL = 1024
GLA_HEADS = 4
GLA_DK = 64
GLA_DV = 128
GLA_RANK = 16
GLA_GATE_TAU = 16.0
GLA_CHUNK = 64
SWA_HEADS = 8
SWA_KV_HEADS = 2
SWA_HEAD_DIM = 64
SWA_WINDOW = 128
N_EXPERTS = 32
TOP_K = 4
D_FF = 1024
SWIGLU_LIMIT = 7.0
SWIGLU_ALPHA = 1.702
NORM_EPS = 1e-5

LANES = 128
SLAB_SUBLANES = D_MODEL // LANES
VMEM_LIMIT = 56 * 1024 * 1024

TM_PROJ = 1024
TL_GLA = 1024
GLA_CUM_ROWS = 256
TM_ROUTE = 1024
TM_EXPERT = 512
EXPERT_PAD = 256
BACKGROUND_DMA = 1
TM_COMBINE = 256
SWA_SUB = 8
TM_DISPATCH = 256

NEG_BIG = -1e30
ROUTE_ROWS = 16

C_QA, C_KA, C_VA, C_RA, C_QB, C_KB, C_KBS, C_VB, C_VBS, C_Z, C_END = (
    0, 256, 512, 1024, 1536, 2048, 2176, 2304, 2432, 2560, 2688)
Z_PIECES = 5


def _split3(x):
    hi = x.astype(BF16).astype(F32)
    r = x - hi
    mid = r.astype(BF16).astype(F32)
    lo = (r - mid).astype(BF16).astype(F32)
    return hi, mid, lo


class _SlabTileCopy:
    def __init__(self, flat, slab, sem, to_slab):
        self.copies = []
        for c in range(SLAB_SUBLANES):
            pair = (flat.at[:, pl.ds(c * LANES, LANES)], slab.at[:, c, :])
            src, dst = pair if to_slab else pair[::-1]
            self.copies.append(pltpu.make_async_copy(src, dst, sem))

    def start(self, priority=0):
        for cp in self.copies:
            cp.start(priority=priority)

    def wait(self):
        for cp in self.copies:
            cp.wait()


def _rms(x, g):
    return x * lax.rsqrt(jnp.mean(x * x, axis=-1, keepdims=True) + NORM_EPS) * g


def _in_proj_kernel(x_ref, g_ref, w_ref, wup_ref, ba_ref,
                    qk_ref, vr_ref, la_ref, qb_ref, kv_ref):
    u = _rms(x_ref[...], g_ref[...]).astype(BF16)

    def proj(c0, c1):
        return jnp.dot(u, w_ref[:, c0:c1], preferred_element_type=F32)

    qk_ref[...] = proj(C_QA, C_VA).astype(BF16)
    vr_ref[:, :C_RA - C_VA] = proj(C_VA, C_RA).astype(BF16)
    vr_ref[:, C_RA - C_VA:] = proj(C_RA, C_QB).astype(BF16)
    qb_ref[...] = proj(C_QB, C_KB).astype(BF16)
    kv_ref[...] = proj(C_KB, C_Z).astype(BF16)
    z = proj(C_Z, C_END)
    hi, mid, lo = _split3(z)
    piece = lax.broadcasted_iota(jnp.int32, z.shape, 1) // GLA_RANK
    zc = jnp.where((piece == 0) | (piece == 3), hi, jnp.where(piece == 2, lo, mid)).astype(BF16)
    y = jnp.dot(zc, wup_ref[...], preferred_element_type=F32) + ba_ref[...]
    log_sig = jnp.minimum(y, 0.0) - jnp.log1p(jnp.exp(-jnp.abs(y)))
    la_ref[...] = log_sig * (1.0 / GLA_GATE_TAU)


def _in_proj(x2, g1, w_cat, wup_p, ba_p):
    T = x2.shape[0]
    tm = TM_PROJ
    row = lambda w: pl.BlockSpec((tm, w), lambda i: (i, 0))
    full = lambda a: pl.BlockSpec(a.shape, lambda i: (0,) * a.ndim)
    outs = [(512, BF16), (1024, BF16), (256, F32), (512, BF16), (512, BF16)]
    return pl.pallas_call(
        _in_proj_kernel,
        grid=(T // tm,),
        in_specs=[row(D_MODEL), full(g1), full(w_cat), full(wup_p), full(ba_p)],
        out_specs=[row(w) for w, _ in outs],
        out_shape=[jax.ShapeDtypeStruct((T, w), dt) for w, dt in outs],
        compiler_params=pltpu.CompilerParams(
            dimension_semantics=("arbitrary",), vmem_limit_bytes=VMEM_LIMIT),
        name="in_proj",
    )(x2, g1, w_cat, wup_p, ba_p)


def _gla_kernel(qk_ref, vr_ref, la_ref, g_ref, cum_ref, o_ref, st_ref):
    @pl.when(pl.program_id(1) == 0)
    def _():
        st_ref[...] = jnp.zeros_like(st_ref)

    tl = TL_GLA
    c = GLA_CHUNK
    kw = GLA_HEADS * GLA_DK
    vw = GLA_HEADS * GLA_DV
    causal = (lax.broadcasted_iota(jnp.int32, (c, c), 0)
              >= lax.broadcasted_iota(jnp.int32, (c, c), 1))
    low_half = lax.broadcasted_iota(jnp.int32, (c, LANES), 1) < GLA_DK
    g = g_ref[...]
    b_groups = []
    for grp in range(tl // GLA_CUM_ROWS):
        la = la_ref[grp * GLA_CUM_ROWS:(grp + 1) * GLA_CUM_ROWS, :]
        pieces = jnp.concatenate([p.astype(BF16) for p in _split3(la)], axis=1)
        b3 = jnp.dot(cum_ref[...], pieces, preferred_element_type=F32)
        b_groups.append(b3[:, :kw] + b3[:, kw:2 * kw] + b3[:, 2 * kw:])
    for ch in range(tl // c):
        rows = slice(ch * c, (ch + 1) * c)
        in_grp = (ch * c) % GLA_CUM_ROWS
        b = b_groups[(ch * c) // GLA_CUM_ROWS][in_grp:in_grp + c]
        b_last = b[c - 1:c]
        qf = qk_ref[rows, :kw].astype(F32)
        kf = qk_ref[rows, kw:].astype(F32)
        q_e = (qf * jnp.exp(b) * (GLA_DK ** -0.5)).astype(BF16)
        k_e = (kf * jnp.exp(-b)).astype(BF16)
        k_t = (kf * jnp.exp(b_last - b)).astype(BF16)
        decay = jnp.exp(b_last)
        for h in range(GLA_HEADS):
            ps = slice((h // 2) * LANES, (h // 2 + 1) * LANES)
            mine = low_half if h % 2 == 0 else ~low_half
            qp, kp = q_e[:, ps], k_e[:, ps]
            qh = jnp.where(mine, qp, jnp.zeros_like(qp))
            kth = jnp.where(mine, k_t[:, ps], jnp.zeros_like(qp))
            vs = slice(h * GLA_DV, (h + 1) * GLA_DV)
            vh = vr_ref[rows, vs]
            a = pl.dot(qh, kp, trans_b=True)
            a = jnp.where(causal, a, 0.0).astype(BF16)
            st = st_ref[h]
            o = (jnp.dot(a, vh, preferred_element_type=F32)
                 + pl.dot(qh, st.astype(BF16), trans_b=True))
            st_ref[h] = st * decay[:, ps] + pl.dot(vh, kth, trans_a=True)
            rh = vr_ref[rows, slice(vw + h * GLA_DV, vw + (h + 1) * GLA_DV)].astype(F32)
            o = _rms(o, g) * (rh * jax.nn.sigmoid(rh))
            o_ref[rows, vs] = o.astype(BF16)


def _gla(qk, vr, la, g, batch, seq):
    tl = TL_GLA
    nl = seq // tl
    row = lambda w: pl.BlockSpec((tl, w), lambda b, i: (b * nl + i, 0))
    r = np.arange(GLA_CUM_ROWS)
    cum = jnp.asarray((r[None, :] <= r[:, None])
                      & (r[None, :] // GLA_CHUNK == r[:, None] // GLA_CHUNK), BF16)
    return pl.pallas_call(
        _gla_kernel,
        grid=(batch, nl),
        in_specs=[row(512), row(1024), row(256),
                  pl.BlockSpec((1, GLA_DV), lambda b, i: (0, 0)),
                  pl.BlockSpec((GLA_CUM_ROWS, GLA_CUM_ROWS), lambda b, i: (0, 0))],
        out_specs=row(512),
        out_shape=jax.ShapeDtypeStruct((batch * seq, 512), BF16),
        scratch_shapes=[pltpu.VMEM((GLA_HEADS, GLA_DV, LANES), F32)],
        compiler_params=pltpu.CompilerParams(
            dimension_semantics=("arbitrary", "arbitrary"), vmem_limit_bytes=VMEM_LIMIT),
        name="gla",
    )(qk, vr, la, g, cum)


def _swa_kernel(sink_ref, bias0_ref, bias_ref, q_ref, kvc_ref, kvp_ref, g_ref, o_ref):
    w = SWA_WINDOW
    scale = jnp.asarray(SWA_HEAD_DIM ** -0.5, BF16)
    kv_lane = lax.broadcasted_iota(jnp.int32, (2 * w, LANES), 1)
    lane_lo = kv_lane < SWA_HEAD_DIM
    out_lo = lax.broadcasted_iota(jnp.int32, (w, LANES), 1) < SWA_HEAD_DIM
    ones_hi = jnp.where(kv_lane == SWA_HEAD_DIM, 1.0, 0.0).astype(BF16)
    ones_lo = jnp.where(kv_lane == 0, 1.0, 0.0).astype(BF16)
    g = g_ref[...]

    for sb in range(SWA_SUB):
        rows = slice(sb * w, (sb + 1) * w)
        if sb == 0:
            kv = jnp.concatenate([kvp_ref[...], kvc_ref[0:w, :]], axis=0)
            bias = bias0_ref
        else:
            kv = kvc_ref[(sb - 1) * w:(sb + 1) * w, :]
            bias = bias_ref
        k = kv[:, 0:LANES] * scale
        ks = kv[:, LANES:2 * LANES] * scale
        v, vs = kv[:, 2 * LANES:3 * LANES], kv[:, 3 * LANES:]
        zero = jnp.zeros_like(k)
        k_low = [jnp.where(lane_lo, k, zero), jnp.where(lane_lo, ks, zero)]
        k_high = [jnp.where(lane_lo, zero, ks), jnp.where(lane_lo, zero, k)]
        v_low = [jnp.where(lane_lo, v, zero) + ones_hi, jnp.where(lane_lo, vs, zero) + ones_hi]
        v_high = [jnp.where(lane_lo, zero, vs) + ones_lo, jnp.where(lane_lo, zero, v) + ones_lo]

        def weights(s, head):
            s = s + bias[0, head]
            sink = sink_ref[head]
            m = jnp.maximum(jnp.max(s, axis=-1, keepdims=True), sink)
            return jnp.exp(s - m).astype(BF16), jnp.exp(sink - m)

        for pair in range(SWA_HEADS // 2):
            j = (2 * pair) // (SWA_HEADS // SWA_KV_HEADS)
            cols = slice(pair * LANES, (pair + 1) * LANES)
            qp = q_ref[rows, cols]
            e0, sink0 = weights(pl.dot(qp, k_low[j], trans_b=True), 2 * pair)
            e1, sink1 = weights(pl.dot(qp, k_high[j], trans_b=True), 2 * pair + 1)
            oa = jnp.dot(e0, v_low[j], preferred_element_type=F32)
            ob = jnp.dot(e1, v_high[j], preferred_element_type=F32)
            inv0 = 1.0 / (oa[:, SWA_HEAD_DIM:SWA_HEAD_DIM + 1] + sink0)
            inv1 = 1.0 / (ob[:, 0:1] + sink1)
            o = jnp.where(out_lo, oa * inv0, ob * inv1)
            sq = o * o
            ms_lo = jnp.sum(jnp.where(out_lo, sq, 0.0), axis=-1, keepdims=True)
            ms_hi = jnp.sum(jnp.where(out_lo, 0.0, sq), axis=-1, keepdims=True)
            ms = jnp.where(out_lo, ms_lo, ms_hi) * (1.0 / SWA_HEAD_DIM)
            o = o * lax.rsqrt(ms + NORM_EPS) * g
            o_ref[rows, cols] = o.astype(BF16)


def _swa_bias():
    w = SWA_WINDOW
    slopes = 2.0 ** (-8.0 * np.arange(1, SWA_HEADS + 1, dtype=np.float64) / SWA_HEADS)
    rel = np.arange(w)[:, None] + w - np.arange(2 * w)[None, :]
    in_window = (rel >= 0) & (rel < w)
    exists = np.stack([np.arange(2 * w) >= w, np.ones(2 * w, bool)])
    valid = in_window[None] & exists[:, None, :]
    bias = -slopes[None, :, None, None] * rel[None, None].astype(np.float64)
    return np.where(valid[:, None], bias, NEG_BIG).astype(np.float32)


def _swa(qb, kv, sinks, g2, batch, seq):
    w = SWA_WINDOW
    rows = SWA_SUB * w
    nb = seq // rows
    cur = lambda: pl.BlockSpec((rows, 512), lambda b, n: (b * nb + n, 0))
    prev = pl.BlockSpec(
        (w, 512), lambda b, n: (jnp.maximum((b * nb + n) * SWA_SUB - 1, b * nb * SWA_SUB), 0))
    table = lambda index: pl.BlockSpec((1, SWA_HEADS, w, 2 * w), index)
    bias = jnp.asarray(_swa_bias())
    return pl.pallas_call(
        _swa_kernel,
        grid=(batch, nb),
        in_specs=[pl.BlockSpec(memory_space=pltpu.SMEM),
                  table(lambda b, n: (jnp.minimum(n, 1), 0, 0, 0)),
                  table(lambda b, n: (1, 0, 0, 0)),
                  cur(), cur(), prev,
                  pl.BlockSpec((1, LANES), lambda b, n: (0, 0))],
        out_specs=cur(),
        out_shape=jax.ShapeDtypeStruct((batch * seq, 512), BF16),
        compiler_params=pltpu.CompilerParams(
            dimension_semantics=("arbitrary", "arbitrary"), vmem_limit_bytes=VMEM_LIMIT),
        name="swa",
    )(sinks, bias, bias, qb, kv, kv, g2)


def _out_route_kernel(oa_ref, ob_ref, x_ref, wo_ref, g_ref, wrh_ref, wrl_ref, br_ref,
                      earlier_ref, h13_ref, hn3_ref, route_ref, cnt_ref, hbuf, hsem):
    i = pl.program_id(0)
    tm = TM_ROUTE

    @pl.when(i == 0)
    def _():
        cnt_ref[...] = jnp.zeros_like(cnt_ref)

    def stores(step, slot):
        rows = pl.ds(step * tm, tm)
        return [_SlabTileCopy(hbuf.at[slot, which], out.at[rows], hsem.at[slot], to_slab=True)
                for which, out in enumerate((h13_ref, hn3_ref))]

    slot = i % 2

    @pl.when(i >= 2)
    def _():
        for st in stores(i - 2, slot):
            st.wait()

    h1 = (x_ref[...]
          + jnp.dot(oa_ref[...], wo_ref[:oa_ref.shape[1], :], preferred_element_type=F32)
          + jnp.dot(ob_ref[...], wo_ref[oa_ref.shape[1]:, :], preferred_element_type=F32))
    hn = _rms(h1, g_ref[...])
    hbuf[slot, 0] = h1
    hbuf[slot, 1] = hn
    for st in stores(i, slot):
        st.start()

    @pl.when(i == pl.num_programs(0) - 1)
    def _():
        @pl.when(i >= 1)
        def _():
            for st in stores(i - 1, 1 - slot):
                st.wait()
        for st in stores(i, slot):
            st.wait()

    hn_hi = hn.astype(BF16)
    hn_lo = (hn - hn_hi.astype(F32)).astype(BF16)
    logits = (pl.dot(wrh_ref[...], hn_hi, trans_b=True)
              + pl.dot(wrh_ref[...], hn_lo, trans_b=True)
              + pl.dot(wrl_ref[...], hn_hi, trans_b=True)) + br_ref[...]

    eid = lax.broadcasted_iota(jnp.int32, (N_EXPERTS, tm), 0)
    work = logits
    vals, idxs, sels = [], [], []
    for _ in range(TOP_K):
        m = jnp.max(work, axis=0, keepdims=True)
        idx = jnp.min(jnp.where(work == m, eid, N_EXPERTS), axis=0, keepdims=True)
        sel = eid == idx
        vals.append(m)
        idxs.append(idx)
        sels.append(sel)
        work = jnp.where(sel, -3e38, work)
    exps = [jnp.exp(v - vals[0]) for v in vals]
    inv_den = 1.0 / (exps[0] + exps[1] + exps[2] + exps[3])

    multihot = jnp.where(sels[0] | sels[1] | sels[2] | sels[3], 1.0, 0.0)
    before = (jnp.dot(multihot.astype(BF16), earlier_ref[...], preferred_element_type=F32)
              + cnt_ref[:, 0:1])
    ranks = [jnp.sum(jnp.where(s, before, 0.0), axis=0, keepdims=True) for s in sels]
    route_ref[...] = jnp.concatenate(
        [i.astype(F32) for i in idxs] + [e * inv_den for e in exps] + ranks
        + [jnp.zeros((ROUTE_ROWS - 3 * TOP_K, tm), F32)], axis=0)
    cnt_ref[...] += jnp.sum(multihot, axis=1, keepdims=True)


def _out_route(oa, ob, x2, wo, g2, wr_hi, wr_lo, br):
    T = x2.shape[0]
    tm = TM_ROUTE
    row = lambda w: pl.BlockSpec((tm, w), lambda i: (i, 0))
    full = lambda a: pl.BlockSpec(a.shape, lambda i: (0,) * a.ndim)
    earlier = jnp.asarray(np.triu(np.ones((tm, tm), np.float32), k=1), BF16)
    return pl.pallas_call(
        _out_route_kernel,
        grid=(T // tm,),
        in_specs=[row(512), row(512), row(D_MODEL), full(wo), full(g2),
                  full(wr_hi), full(wr_lo), full(br), full(earlier)],
        out_specs=[pl.BlockSpec(memory_space=pl.ANY), pl.BlockSpec(memory_space=pl.ANY),
                   pl.BlockSpec((ROUTE_ROWS, tm), lambda i: (0, i)),
                   pl.BlockSpec((N_EXPERTS, LANES), lambda i: (0, 0))],
        out_shape=[jax.ShapeDtypeStruct((T, SLAB_SUBLANES, LANES), F32),
                   jax.ShapeDtypeStruct((T, SLAB_SUBLANES, LANES), F32),
                   jax.ShapeDtypeStruct((ROUTE_ROWS, T), F32),
                   jax.ShapeDtypeStruct((N_EXPERTS, LANES), F32)],
        scratch_shapes=[pltpu.VMEM((2, 2, tm, D_MODEL), F32), pltpu.SemaphoreType.DMA((2,))],
        compiler_params=pltpu.CompilerParams(
            dimension_semantics=("arbitrary",), vmem_limit_bytes=VMEM_LIMIT,
            has_side_effects=True),
        name="out_route",
    )(oa, ob, x2, wo, g2, wr_hi, wr_lo, br, earlier)


def _row_copy(src, s, dst, d, sem):
    return pltpu.make_async_copy(src.at[pl.ds(s, 1)], dst.at[pl.ds(d, 1)], sem)


def _dispatch_kernel(dest_ref, pad_start_ref, pad_cnt_ref, nused_ref, hn_ref, xs_ref,
                     zero_ref, sem, zsem):
    i = pl.program_id(0)
    n = pl.num_programs(0)
    tm = EXPERT_PAD
    n_tok = dest_ref.shape[0] // TOP_K
    base = i * TM_DISPATCH
    for p in range(TM_DISPATCH * TOP_K):
        r, k = divmod(p, TOP_K)
        _row_copy(hn_ref, r, xs_ref, dest_ref[base + k * n_tok + r], sem).start(priority=p % 2)
    for _ in range(TOP_K):
        pltpu.make_async_copy(hn_ref, xs_ref.at[pl.ds(0, TM_DISPATCH)], sem).wait()

    @pl.when(i == n - 1)
    def _():
        zero_ref[...] = jnp.zeros_like(zero_ref)
        sub = 8
        for e in range(N_EXPERTS):
            start, cnt = pad_start_ref[e], pad_cnt_ref[e]
            head = cnt & (sub - 1)
            body0 = start + head
            nbody = cnt // sub

            def head_copy(r):
                return _row_copy(zero_ref, 0, xs_ref, start + r, zsem)

            def body_copy(c):
                return pltpu.make_async_copy(zero_ref.at[pl.ds(0, sub)],
                                             xs_ref.at[pl.ds(body0 + c * sub, sub)], zsem)

            for copy, count in ((head_copy, head), (body_copy, nbody)):
                def zissue(r, carry, copy=copy):
                    copy(r).start()
                    return carry

                def zwait(r, carry, copy=copy):
                    copy(r).wait()
                    return carry

                lax.fori_loop(0, count, zissue, 0)
                lax.fori_loop(0, count, zwait, 0)

        def tail_copy(t):
            return pltpu.make_async_copy(zero_ref, xs_ref.at[pl.ds(t * tm, tm)], zsem)

        def tissue(t, carry):
            tail_copy(t).start()
            return carry

        def twait(t, carry):
            tail_copy(t).wait()
            return carry

        n_tiles = xs_ref.shape[0] // tm
        lax.fori_loop(nused_ref[0], n_tiles, tissue, 0)
        lax.fori_loop(nused_ref[0], n_tiles, twait, 0)


def _dispatch(dest, pad_start, pad_cnt, n_used, hn, n_rows):
    T = hn.shape[0]
    any_spec = pl.BlockSpec(memory_space=pl.ANY)
    return pl.pallas_call(
        _dispatch_kernel,
        grid_spec=pltpu.PrefetchScalarGridSpec(
            num_scalar_prefetch=4,
            grid=(T // TM_DISPATCH,),
            in_specs=[pl.BlockSpec((TM_DISPATCH, SLAB_SUBLANES, LANES), lambda i, *_: (i, 0, 0))],
            out_specs=any_spec,
            scratch_shapes=[pltpu.VMEM((EXPERT_PAD, SLAB_SUBLANES, LANES), F32),
                            pltpu.SemaphoreType.DMA(()),
                            pltpu.SemaphoreType.DMA(())]),
        out_shape=jax.ShapeDtypeStruct((n_rows, SLAB_SUBLANES, LANES), F32),
        compiler_params=pltpu.CompilerParams(
            dimension_semantics=("arbitrary",), has_side_effects=True,
            vmem_limit_bytes=VMEM_LIMIT),
        name="dispatch",
    )(dest, pad_start, pad_cnt, n_used, hn)


def _experts_kernel(row0_ref, nfull_ref, tail_ref, xs_ref, wg_ref, bg_ref, wl_ref, bl_ref, wd_ref,
                    bd_ref, ys_ref, wg_bf, wl_bf, wd_bf, xbuf, ybuf, xsem, ysem, wf32, wsem):
    e = pl.program_id(0)
    tm, half = TM_EXPERT, EXPERT_PAD
    r0, nf, tail = row0_ref[e], nfull_ref[e], tail_ref[e]
    nt = nf + tail

    def x_load(row, slot, rows):
        return _SlabTileCopy(xbuf.at[slot, pl.ds(0, rows)], xs_ref.at[pl.ds(row, rows)],
                             xsem.at[slot], to_slab=False)

    def y_store(row, slot, rows):
        return _SlabTileCopy(ybuf.at[slot, pl.ds(0, rows)], ys_ref.at[pl.ds(row, rows)],
                             ysem.at[slot], to_slab=True)

    def start_first_load(expert):
        @pl.when(nfull_ref[expert] > 0)
        def _():
            x_load(row0_ref[expert], 0, tm).start()

        @pl.when((nfull_ref[expert] == 0) & (tail_ref[expert] > 0))
        def _():
            x_load(row0_ref[expert], 0, half).start()

    def compute(slot, rows):
        x = xbuf[slot, :rows].astype(BF16)
        glu = jnp.minimum(jnp.dot(x, wg_bf[...], preferred_element_type=F32) + bg_ref[0],
                          SWIGLU_LIMIT)
        lin = jnp.clip(jnp.dot(x, wl_bf[...], preferred_element_type=F32) + bl_ref[0],
                       -SWIGLU_LIMIT, SWIGLU_LIMIT)
        hid = glu * jax.nn.sigmoid(SWIGLU_ALPHA * glu) * (lin + 1.0)
        ybuf[slot, :rows] = (jnp.dot(hid.astype(BF16), wd_bf[...], preferred_element_type=F32)
                             + bd_ref[0])

    def w_fetch(expert):
        return [pltpu.make_async_copy(w.at[expert], wf32.at[i], wsem.at[i])
                for i, w in enumerate((wg_ref, wl_ref, wd_ref))]

    @pl.when(e == 0)
    def _():
        start_first_load(0)
        for cp in w_fetch(0):
            cp.start()

    for cp, w_bf, i in zip(w_fetch(e), (wg_bf, wl_bf, wd_bf), range(3)):
        cp.wait()
        w_bf[...] = wf32[i].astype(BF16)

    has_next = e + 1 < pl.num_programs(0)

    def fetch_next(i):
        @pl.when(has_next)
        def _():
            w_fetch(e + 1)[i].start(priority=BACKGROUND_DMA)

    fetch_next(0)

    def full_tile(j, carry):
        slot = j % 2

        @pl.when(j + 1 < nf)
        def _():
            x_load(r0 + (j + 1) * tm, 1 - slot, tm).start()

        @pl.when((j + 1 == nf) & (tail > 0))
        def _():
            x_load(r0 + (j + 1) * tm, 1 - slot, half).start()

        x_load(r0 + j * tm, slot, tm).wait()

        @pl.when(j >= 2)
        def _():
            y_store(r0 + (j - 2) * tm, slot, tm).wait()

        compute(slot, tm)
        y_store(r0 + j * tm, slot, tm).start(priority=BACKGROUND_DMA)
        for i in (1, 2):
            @pl.when(j == i - 1)
            def _():
                fetch_next(i)
        return carry

    lax.fori_loop(0, nf, full_tile, 0)
    for i in (1, 2):
        @pl.when(nf < i)
        def _():
            fetch_next(i)

    @pl.when(tail > 0)
    def _():
        slot = nf % 2
        x_load(r0 + nf * tm, slot, half).wait()

        @pl.when(nf >= 2)
        def _():
            y_store(r0 + (nf - 2) * tm, slot, tm).wait()

        compute(slot, half)
        y_store(r0 + nf * tm, slot, half).start(priority=BACKGROUND_DMA)

    @pl.when(e + 1 < pl.num_programs(0))
    def _():
        start_first_load(e + 1)

    @pl.when(nt >= 2)
    def _():
        y_store(r0 + (nt - 2) * tm, nt % 2, tm).wait()

    @pl.when((nt >= 1) & (tail == 0))
    def _():
        y_store(r0 + (nt - 1) * tm, (nt - 1) % 2, tm).wait()

    @pl.when(tail > 0)
    def _():
        y_store(r0 + nf * tm, nf % 2, half).wait()

    @pl.when(e == pl.num_programs(0) - 1)
    def _():
        ybuf[0] = jnp.zeros((tm, D_MODEL), F32)
        first = (r0 + nf * tm + tail * half) // half
        n_pieces = ys_ref.shape[0] // half

        def fill_issue(t, carry):
            y_store(t * half, 0, half).start()
            return carry

        def fill_wait(t, carry):
            y_store(t * half, 0, half).wait()
            return carry

        lax.fori_loop(first, n_pieces, fill_issue, 0)
        lax.fori_loop(first, n_pieces, fill_wait, 0)


def _experts(row0, nfull, tail, xs, w_glu, b_glu, w_lin, b_lin, w_down, b_down):
    tm = TM_EXPERT
    any_spec = pl.BlockSpec(memory_space=pl.ANY)
    bspec = lambda: pl.BlockSpec((1, 1, D_FF), lambda e, *_: (e, 0, 0))
    return pl.pallas_call(
        _experts_kernel,
        grid_spec=pltpu.PrefetchScalarGridSpec(
            num_scalar_prefetch=3,
            grid=(N_EXPERTS,),
            in_specs=[any_spec, any_spec, bspec(), any_spec, bspec(), any_spec, bspec()],
            out_specs=any_spec,
            scratch_shapes=[pltpu.VMEM((D_MODEL, D_FF), BF16)] * 3
            + [pltpu.VMEM((2, tm, D_MODEL), F32), pltpu.VMEM((2, tm, D_MODEL), F32),
               pltpu.SemaphoreType.DMA((2,)), pltpu.SemaphoreType.DMA((2,)),
               pltpu.VMEM((3, D_MODEL, D_FF), F32), pltpu.SemaphoreType.DMA((3,))]),
        out_shape=jax.ShapeDtypeStruct(xs.shape, F32),
        compiler_params=pltpu.CompilerParams(
            dimension_semantics=("arbitrary",), vmem_limit_bytes=VMEM_LIMIT,
            has_side_effects=True),
        name="experts",
    )(row0, nfull, tail, xs, w_glu, b_glu.reshape(N_EXPERTS, 1, D_FF),
      w_lin, b_lin.reshape(N_EXPERTS, 1, D_FF), w_down, b_down.reshape(N_EXPERTS, 1, D_MODEL))


def _combine_kernel(dest_ref, gate_ref, ys_ref, h1_ref, g_ref, out_ref, buf, obuf, sems, osems):
    s = pl.program_id(0)
    tm = TM_COMBINE
    n = out_ref.shape[0] // tm
    n_tok = dest_ref.shape[0] // TOP_K

    def out_store(tile, slot):
        return _SlabTileCopy(out_ref.at[pl.ds(tile * tm, tm)], obuf.at[slot], osems.at[slot],
                             to_slab=False)

    def issue_row(slot, r):
        for k in range(TOP_K):
            _row_copy(ys_ref, dest_ref[s * tm + k * n_tok + r], buf.at[slot, k], r,
                      sems.at[slot]).start(priority=k % 2)

    def reduce_row(slot, r):
        acc = h1_ref[r]
        for k in range(TOP_K):
            acc = acc + gate_ref[(s - 1) * tm + k * n_tok + r] * buf[slot, k, r]
        ss = jnp.sum(jnp.sum(acc * acc, axis=1, keepdims=True), axis=0, keepdims=True)
        obuf[slot, r] = acc * lax.rsqrt(ss * (1.0 / D_MODEL) + NORM_EPS) * g_ref[0]

    def step_body(gather_slot, reduce_slot):
        for r in range(tm):
            if gather_slot is not None:
                issue_row(gather_slot, r)
            if reduce_slot is not None:
                reduce_row(reduce_slot, r)
        if reduce_slot is not None:
            out_store(s - 1, reduce_slot).start()

    for parity in range(2):
        other = 1 - parity

        @pl.when(s % 2 == parity)
        def _():
            @pl.when(s < n)
            def _():
                step_body(parity, None)

            @pl.when(s >= 3)
            def _():
                out_store(s - 3, other).wait()

            @pl.when(s > 0)
            def _():
                for k in range(TOP_K):
                    pltpu.make_async_copy(ys_ref.at[pl.ds(0, tm)], buf.at[other, k],
                                          sems.at[other]).wait()
                step_body(None, other)

            if n % 2 == parity:
                @pl.when(s == n)
                def _():
                    if n >= 2:
                        out_store(n - 2, parity).wait()
                    out_store(n - 1, other).wait()


def _combine(dest, ys3, h13, gate, fg):
    T = h13.shape[0]
    tm = TM_COMBINE
    slab = (SLAB_SUBLANES, LANES)
    prev_tile = lambda s, *_: (jnp.maximum(s - 1, 0), 0, 0)
    return pl.pallas_call(
        _combine_kernel,
        grid_spec=pltpu.PrefetchScalarGridSpec(
            num_scalar_prefetch=2,
            grid=(T // tm + 1,),
            in_specs=[pl.BlockSpec(memory_space=pl.ANY),
                      pl.BlockSpec((tm,) + slab, prev_tile),
                      pl.BlockSpec((1,) + slab, lambda s, *_: (0, 0, 0))],
            out_specs=pl.BlockSpec(memory_space=pl.ANY),
            scratch_shapes=[pltpu.VMEM((2, TOP_K, tm) + slab, F32),
                            pltpu.VMEM((2, tm) + slab, F32),
                            pltpu.SemaphoreType.DMA((2,)),
                            pltpu.SemaphoreType.DMA((2,))]),
        out_shape=jax.ShapeDtypeStruct((T, D_MODEL), F32),
        compiler_params=pltpu.CompilerParams(
            dimension_semantics=("arbitrary",), vmem_limit_bytes=VMEM_LIMIT,
            has_side_effects=True),
        name="combine",
    )(dest, gate, ys3, h13, fg.reshape((1,) + slab))


def _swap_halves(w):
    h = w.shape[-1] // 2
    return jnp.concatenate([w[..., h:], w[..., :h]], axis=-1)


def _layer(x2, batch, seq, norm1_g, w_in, w_alpha_up, b_alpha, gla_norm_g, swa_sinks, swa_norm_g,
           w_out, norm2_g, w_router, b_router, w_glu, b_glu, w_lin, b_lin, w_down, b_down):
    T = x2.shape[0]
    kb_w, vb_w = w_in[:, 2064:2192], w_in[:, 2192:2320]
    w_z = w_in[:, 1536:1552]
    w_cat = jnp.concatenate([
        w_in[:, 0:1536], w_in[:, 1552:2064],
        kb_w, _swap_halves(kb_w), vb_w, _swap_halves(vb_w),
        jnp.pad(jnp.tile(w_z, (1, Z_PIECES)), [(0, 0), (0, LANES - Z_PIECES * GLA_RANK)]),
    ], axis=1).astype(BF16)
    wup_hi = w_alpha_up.astype(BF16)
    wup_lo = (w_alpha_up - wup_hi.astype(F32)).astype(BF16)
    wup_cat = jnp.pad(jnp.concatenate([wup_hi, wup_hi, wup_hi, wup_lo, wup_lo], axis=0),
                      [(0, LANES - Z_PIECES * GLA_RANK), (0, 0)])

    qk, vr, la, qb, kv = _in_proj(
        x2, norm1_g.reshape(1, -1), w_cat, wup_cat, b_alpha.reshape(1, -1))
    oa = _gla(qk, vr, la, gla_norm_g.reshape(1, -1), batch, seq)
    ob = _swa(qb, kv, swa_sinks, jnp.tile(swa_norm_g, 2).reshape(1, -1), batch, seq)

    wo = w_out.astype(BF16)
    wr_t = w_router.T
    wr_hi = wr_t.astype(BF16)
    wr_lo = (wr_t - wr_hi.astype(F32)).astype(BF16)
    h1, hn, route, cnt = _out_route(oa, ob, x2, wo, norm2_g.reshape(1, -1),
                                    wr_hi, wr_lo, b_router.reshape(-1, 1))

    tm, pad = TM_EXPERT, EXPERT_PAD
    n_rows = T * TOP_K + N_EXPERTS * pad
    counts = cnt[:, 0].astype(jnp.int32)
    padded = (counts + pad - 1) // pad * pad
    pends = jnp.cumsum(padded)
    pstarts = pends - padded
    top_idx = route[0:TOP_K].astype(jnp.int32)
    gate = route[TOP_K:2 * TOP_K].reshape(-1)
    rank = route[2 * TOP_K:3 * TOP_K].astype(jnp.int32)
    experts = jnp.arange(N_EXPERTS)[:, None, None]
    seg_start = jnp.sum(jnp.where(top_idx[None] == experts, pstarts[:, None, None], 0), axis=0)
    dest = (seg_start + rank).reshape(-1)
    n_used = (pends[-1] // pad).reshape(1)
    xs = _dispatch(dest, pstarts + counts, padded - counts, n_used, hn, n_rows)
    ys = _experts(pstarts, padded // tm, padded % tm // pad, xs,
                  w_glu, b_glu, w_lin, b_lin, w_down, b_down)
    return dest, ys, h1, gate


def kernel(x, norm1_g, w_in, w_alpha_up, b_alpha, gla_norm_g, swa_sinks, swa_norm_g, w_out,
           norm2_g, w_router, b_router, w_glu, b_glu, w_lin, b_lin, w_down, b_down, final_g):
    batch, seq, d = x.shape
    assert norm1_g.shape[0] == 1, "single-layer problem"
    x2 = x.reshape(batch * seq, d)
    dest, ys, h1, gate = _layer(
        x2, batch, seq, norm1_g[0], w_in[0], w_alpha_up[0], b_alpha[0], gla_norm_g[0],
        swa_sinks[0], swa_norm_g[0], w_out[0], norm2_g[0], w_router[0], b_router[0],
        w_glu[0], b_glu[0], w_lin[0], b_lin[0], w_down[0], b_down[0])
    out = _combine(dest, ys, h1, gate, final_g.reshape(1, -1))
    return out.reshape(batch, seq, d)
```

```python
import numpy as np
import jax
import jax.numpy as jnp
from jax import lax
from jax.experimental import pallas as pl
from jax.experimental.pallas import tpu as pltpu

F32 = jnp.float32
BF16 = jnp.bfloat16

D_MODEL = 1024
GLA_HEADS = 4
GLA_DK = 64
GLA_DV = 128
GLA_RANK = 16
GLA_GATE_TAU = 16.0
GLA_CHUNK = 64
SWA_HEADS = 8
SWA_KV_HEADS = 2
SWA_HEAD_DIM = 64
SWA_WINDOW = 128
N_EXPERTS = 32
TOP_K = 4
D_FF = 1024
SWIGLU_LIMIT = 7.0
SWIGLU_ALPHA = 1.702
NORM_EPS = 1e-5

LANES = 128
SLAB_SUBLANES = D_MODEL // LANES
VMEM_LIMIT = 56 * 1024 * 1024

TM_PROJ = 1024
TL_GLA = 1024
GLA_CUM_ROWS = 256
TM_ROUTE = 1024
TM_EXPERT = 512
EXPERT_PAD = 128
TM_COMBINE = 256
SWA_SUB = 8
TM_DISPATCH = 256

NEG_BIG = -1e30
ROUTE_ROWS = 16

C_QA, C_KA, C_VA, C_RA, C_QB, C_KB, C_KBS, C_VB, C_VBS, C_Z, C_END = (
    0, 256, 512, 1024, 1536, 2048, 2176, 2304, 2432, 2560, 2688)
Z_PIECES = 5


def _split3(x):
    hi = x.astype(BF16).astype(F32)
    r = x - hi
    mid = r.astype(BF16).astype(F32)
    lo = (r - mid).astype(BF16).astype(F32)
    return hi, mid, lo


class _SlabTileCopy:
    def __init__(self, flat, slab, sem, to_slab):
        self.copies = []
        for c in range(SLAB_SUBLANES):
            pair = (flat.at[:, pl.ds(c * LANES, LANES)], slab.at[:, c, :])
            src, dst = pair if to_slab else pair[::-1]
            self.copies.append(pltpu.make_async_copy(src, dst, sem))

    def start(self):
        for cp in self.copies:
            cp.start()

    def wait(self):
        for cp in self.copies:
            cp.wait()


def _rms(x, g):
    return x * lax.rsqrt(jnp.mean(x * x, axis=-1, keepdims=True) + NORM_EPS) * g


def _in_proj_kernel(x_ref, g_ref, w_ref, wup_ref, ba_ref,
                    qk_ref, vr_ref, la_ref, qb_ref, kv_ref):
    u = _rms(x_ref[...], g_ref[...]).astype(BF16)

    def proj(c0, c1):
        return jnp.dot(u, w_ref[:, c0:c1], preferred_element_type=F32)

    qk_ref[...] = proj(C_QA, C_VA).astype(BF16)
    vr_ref[:, :C_RA - C_VA] = proj(C_VA, C_RA).astype(BF16)
    vr_ref[:, C_RA - C_VA:] = proj(C_RA, C_QB).astype(BF16)
    qb_ref[...] = proj(C_QB, C_KB).astype(BF16)
    kv_ref[...] = proj(C_KB, C_Z).astype(BF16)
    z = proj(C_Z, C_END)
    hi, mid, lo = _split3(z)
    piece = lax.broadcasted_iota(jnp.int32, z.shape, 1) // GLA_RANK
    zc = jnp.where((piece == 0) | (piece == 3), hi, jnp.where(piece == 2, lo, mid)).astype(BF16)
    y = jnp.dot(zc, wup_ref[...], preferred_element_type=F32) + ba_ref[...]
    log_sig = jnp.minimum(y, 0.0) - jnp.log1p(jnp.exp(-jnp.abs(y)))
    la_ref[...] = log_sig * (1.0 / GLA_GATE_TAU)


def _in_proj(x2, g1, w_cat, wup_p, ba_p):
    T = x2.shape[0]
    tm = TM_PROJ
    row = lambda w: pl.BlockSpec((tm, w), lambda i: (i, 0))
    full = lambda a: pl.BlockSpec(a.shape, lambda i: (0,) * a.ndim)
    outs = [(512, BF16), (1024, BF16), (256, F32), (512, BF16), (512, BF16)]
    return pl.pallas_call(
        _in_proj_kernel,
        grid=(T // tm,),
        in_specs=[row(D_MODEL), full(g1), full(w_cat), full(wup_p), full(ba_p)],
        out_specs=[row(w) for w, _ in outs],
        out_shape=[jax.ShapeDtypeStruct((T, w), dt) for w, dt in outs],
        compiler_params=pltpu.CompilerParams(
            dimension_semantics=("arbitrary",), vmem_limit_bytes=VMEM_LIMIT),
        name="in_proj",
    )(x2, g1, w_cat, wup_p, ba_p)


def _gla_kernel(qk_ref, vr_ref, la_ref, g_ref, cum_ref, o_ref, st_ref):
    @pl.when(pl.program_id(1) == 0)
    def _():
        st_ref[...] = jnp.zeros_like(st_ref)

    tl = TL_GLA
    c = GLA_CHUNK
    kw = GLA_HEADS * GLA_DK
    vw = GLA_HEADS * GLA_DV
    causal = (lax.broadcasted_iota(jnp.int32, (c, c), 0)
              >= lax.broadcasted_iota(jnp.int32, (c, c), 1))
    low_half = lax.broadcasted_iota(jnp.int32, (c, LANES), 1) < GLA_DK
    g = g_ref[...]
    b_groups = []
    for grp in range(tl // GLA_CUM_ROWS):
        la = la_ref[grp * GLA_CUM_ROWS:(grp + 1) * GLA_CUM_ROWS, :]
        pieces = jnp.concatenate([p.astype(BF16) for p in _split3(la)], axis=1)
        b3 = jnp.dot(cum_ref[...], pieces, preferred_element_type=F32)
        b_groups.append(b3[:, :kw] + b3[:, kw:2 * kw] + b3[:, 2 * kw:])
    for ch in range(tl // c):
        rows = slice(ch * c, (ch + 1) * c)
        in_grp = (ch * c) % GLA_CUM_ROWS
        b = b_groups[(ch * c) // GLA_CUM_ROWS][in_grp:in_grp + c]
        b_last = b[c - 1:c]
        qf = qk_ref[rows, :kw].astype(F32)
        kf = qk_ref[rows, kw:].astype(F32)
        q_e = (qf * jnp.exp(b) * (GLA_DK ** -0.5)).astype(BF16)
        k_e = (kf * jnp.exp(-b)).astype(BF16)
        k_t = (kf * jnp.exp(b_last - b)).astype(BF16)
        decay = jnp.exp(b_last)
        for h in range(GLA_HEADS):
            ps = slice((h // 2) * LANES, (h // 2 + 1) * LANES)
            mine = low_half if h % 2 == 0 else ~low_half
            qp, kp = q_e[:, ps], k_e[:, ps]
            qh = jnp.where(mine, qp, jnp.zeros_like(qp))
            kth = jnp.where(mine, k_t[:, ps], jnp.zeros_like(qp))
            vs = slice(h * GLA_DV, (h + 1) * GLA_DV)
            vh = vr_ref[rows, vs]
            a = pl.dot(qh, kp, trans_b=True)
            a = jnp.where(causal, a, 0.0).astype(BF16)
            st = st_ref[h]
            o = (jnp.dot(a, vh, preferred_element_type=F32)
                 + pl.dot(qh, st.astype(BF16), trans_b=True))
            st_ref[h] = st * decay[:, ps] + pl.dot(vh, kth, trans_a=True)
            rh = vr_ref[rows, slice(vw + h * GLA_DV, vw + (h + 1) * GLA_DV)].astype(F32)
            o = _rms(o, g) * (rh * jax.nn.sigmoid(rh))
            o_ref[rows, vs] = o.astype(BF16)


def _gla(qk, vr, la, g, batch, seq):
    tl = TL_GLA
    nl = seq // tl
    row = lambda w: pl.BlockSpec((tl, w), lambda b, i: (b * nl + i, 0))
    r = np.arange(GLA_CUM_ROWS)
    cum = jnp.asarray((r[None, :] <= r[:, None])
                      & (r[None, :] // GLA_CHUNK == r[:, None] // GLA_CHUNK), BF16)
    return pl.pallas_call(
        _gla_kernel,
        grid=(batch, nl),
        in_specs=[row(512), row(1024), row(256),
                  pl.BlockSpec((1, GLA_DV), lambda b, i: (0, 0)),
                  pl.BlockSpec((GLA_CUM_ROWS, GLA_CUM_ROWS), lambda b, i: (0, 0))],
        out_specs=row(512),
        out_shape=jax.ShapeDtypeStruct((batch * seq, 512), BF16),
        scratch_shapes=[pltpu.VMEM((GLA_HEADS, GLA_DV, LANES), F32)],
        compiler_params=pltpu.CompilerParams(
            dimension_semantics=("arbitrary", "arbitrary"), vmem_limit_bytes=VMEM_LIMIT),
        name="gla",
    )(qk, vr, la, g, cum)


def _swa_kernel(sink_ref, bias0_ref, bias_ref, q_ref, kvc_ref, kvp_ref, g_ref, o_ref):
    w = SWA_WINDOW
    scale = jnp.asarray(SWA_HEAD_DIM ** -0.5, BF16)
    kv_lane = lax.broadcasted_iota(jnp.int32, (2 * w, LANES), 1)
    lane_lo = kv_lane < SWA_HEAD_DIM
    out_lo = lax.broadcasted_iota(jnp.int32, (w, LANES), 1) < SWA_HEAD_DIM
    ones_hi = jnp.where(kv_lane == SWA_HEAD_DIM, 1.0, 0.0).astype(BF16)
    ones_lo = jnp.where(kv_lane == 0, 1.0, 0.0).astype(BF16)
    g = g_ref[...]

    for sb in range(SWA_SUB):
        rows = slice(sb * w, (sb + 1) * w)
        if sb == 0:
            kv = jnp.concatenate([kvp_ref[...], kvc_ref[0:w, :]], axis=0)
            bias = bias0_ref
        else:
            kv = kvc_ref[(sb - 1) * w:(sb + 1) * w, :]
            bias = bias_ref
        k = kv[:, 0:LANES] * scale
        ks = kv[:, LANES:2 * LANES] * scale
        v, vs = kv[:, 2 * LANES:3 * LANES], kv[:, 3 * LANES:]
        zero = jnp.zeros_like(k)
        k_low = [jnp.where(lane_lo, k, zero), jnp.where(lane_lo, ks, zero)]
        k_high = [jnp.where(lane_lo, zero, ks), jnp.where(lane_lo, zero, k)]
        v_low = [jnp.where(lane_lo, v, zero) + ones_hi, jnp.where(lane_lo, vs, zero) + ones_hi]
        v_high = [jnp.where(lane_lo, zero, vs) + ones_lo, jnp.where(lane_lo, zero, v) + ones_lo]

        def weights(s, head):
            s = s + bias[0, head]
            sink = sink_ref[head]
            m = jnp.maximum(jnp.max(s, axis=-1, keepdims=True), sink)
            return jnp.exp(s - m).astype(BF16), jnp.exp(sink - m)

        for pair in range(SWA_HEADS // 2):
            j = (2 * pair) // (SWA_HEADS // SWA_KV_HEADS)
            cols = slice(pair * LANES, (pair + 1) * LANES)
            qp = q_ref[rows, cols]
            e0, sink0 = weights(pl.dot(qp, k_low[j], trans_b=True), 2 * pair)
            e1, sink1 = weights(pl.dot(qp, k_high[j], trans_b=True), 2 * pair + 1)
            oa = jnp.dot(e0, v_low[j], preferred_element_type=F32)
            ob = jnp.dot(e1, v_high[j], preferred_element_type=F32)
            inv0 = 1.0 / (oa[:, SWA_HEAD_DIM:SWA_HEAD_DIM + 1] + sink0)
            inv1 = 1.0 / (ob[:, 0:1] + sink1)
            o = jnp.where(out_lo, oa * inv0, ob * inv1)
            sq = o * o
            ms_lo = jnp.sum(jnp.where(out_lo, sq, 0.0), axis=-1, keepdims=True)
            ms_hi = jnp.sum(jnp.where(out_lo, 0.0, sq), axis=-1, keepdims=True)
            ms = jnp.where(out_lo, ms_lo, ms_hi) * (1.0 / SWA_HEAD_DIM)
            o = o * lax.rsqrt(ms + NORM_EPS) * g
            o_ref[rows, cols] = o.astype(BF16)


def _swa_bias():
    w = SWA_WINDOW
    slopes = 2.0 ** (-8.0 * np.arange(1, SWA_HEADS + 1, dtype=np.float64) / SWA_HEADS)
    rel = np.arange(w)[:, None] + w - np.arange(2 * w)[None, :]
    in_window = (rel >= 0) & (rel < w)
    exists = np.stack([np.arange(2 * w) >= w, np.ones(2 * w, bool)])
    valid = in_window[None] & exists[:, None, :]
    bias = -slopes[None, :, None, None] * rel[None, None].astype(np.float64)
    return np.where(valid[:, None], bias, NEG_BIG).astype(np.float32)


def _swa(qb, kv, sinks, g2, batch, seq):
    w = SWA_WINDOW
    rows = SWA_SUB * w
    nb = seq // rows
    cur = lambda: pl.BlockSpec((rows, 512), lambda b, n: (b * nb + n, 0))
    prev = pl.BlockSpec(
        (w, 512), lambda b, n: (jnp.maximum((b * nb + n) * SWA_SUB - 1, b * nb * SWA_SUB), 0))
    table = lambda index: pl.BlockSpec((1, SWA_HEADS, w, 2 * w), index)
    bias = jnp.asarray(_swa_bias())
    return pl.pallas_call(
        _swa_kernel,
        grid=(batch, nb),
        in_specs=[pl.BlockSpec(memory_space=pltpu.SMEM),
                  table(lambda b, n: (jnp.minimum(n, 1), 0, 0, 0)),
                  table(lambda b, n: (1, 0, 0, 0)),
                  cur(), cur(), prev,
                  pl.BlockSpec((1, LANES), lambda b, n: (0, 0))],
        out_specs=cur(),
        out_shape=jax.ShapeDtypeStruct((batch * seq, 512), BF16),
        compiler_params=pltpu.CompilerParams(
            dimension_semantics=("arbitrary", "arbitrary"), vmem_limit_bytes=VMEM_LIMIT),
        name="swa",
    )(sinks, bias, bias, qb, kv, kv, g2)


def _out_route_kernel(oa_ref, ob_ref, x_ref, wo_ref, g_ref, wrh_ref, wrl_ref, br_ref,
                      earlier_ref, h13_ref, hn3_ref, route_ref, cnt_ref, hbuf, hsem):
    i = pl.program_id(0)
    tm = TM_ROUTE

    @pl.when(i == 0)
    def _():
        cnt_ref[...] = jnp.zeros_like(cnt_ref)

    def stores(step, slot):
        rows = pl.ds(step * tm, tm)
        return [_SlabTileCopy(hbuf.at[slot, which], out.at[rows], hsem.at[slot], to_slab=True)
                for which, out in enumerate((h13_ref, hn3_ref))]

    slot = i % 2

    @pl.when(i >= 2)
    def _():
        for st in stores(i - 2, slot):
            st.wait()

    h1 = (x_ref[...]
          + jnp.dot(oa_ref[...], wo_ref[:oa_ref.shape[1], :], preferred_element_type=F32)
          + jnp.dot(ob_ref[...], wo_ref[oa_ref.shape[1]:, :], preferred_element_type=F32))
    hn = _rms(h1, g_ref[...])
    hbuf[slot, 0] = h1
    hbuf[slot, 1] = hn
    for st in stores(i, slot):
        st.start()

    @pl.when(i == pl.num_programs(0) - 1)
    def _():
        @pl.when(i >= 1)
        def _():
            for st in stores(i - 1, 1 - slot):
                st.wait()
        for st in stores(i, slot):
            st.wait()

    hn_hi = hn.astype(BF16)
    hn_lo = (hn - hn_hi.astype(F32)).astype(BF16)
    logits = (pl.dot(wrh_ref[...], hn_hi, trans_b=True)
              + pl.dot(wrh_ref[...], hn_lo, trans_b=True)
              + pl.dot(wrl_ref[...], hn_hi, trans_b=True)) + br_ref[...]

    eid = lax.broadcasted_iota(jnp.int32, (N_EXPERTS, tm), 0)
    work = logits
    vals, idxs, sels = [], [], []
    for _ in range(TOP_K):
        m = jnp.max(work, axis=0, keepdims=True)
        idx = jnp.min(jnp.where(work == m, eid, N_EXPERTS), axis=0, keepdims=True)
        sel = eid == idx
        vals.append(m)
        idxs.append(idx)
        sels.append(sel)
        work = jnp.where(sel, -3e38, work)
    exps = [jnp.exp(v - vals[0]) for v in vals]
    inv_den = 1.0 / (exps[0] + exps[1] + exps[2] + exps[3])

    multihot = jnp.where(sels[0] | sels[1] | sels[2] | sels[3], 1.0, 0.0)
    before = (jnp.dot(multihot.astype(BF16), earlier_ref[...], preferred_element_type=F32)
              + cnt_ref[:, 0:1])
    ranks = [jnp.sum(jnp.where(s, before, 0.0), axis=0, keepdims=True) for s in sels]
    route_ref[...] = jnp.concatenate(
        [i.astype(F32) for i in idxs] + [e * inv_den for e in exps] + ranks
        + [jnp.zeros((ROUTE_ROWS - 3 * TOP_K, tm), F32)], axis=0)
    cnt_ref[...] += jnp.sum(multihot, axis=1, keepdims=True)


def _out_route(oa, ob, x2, wo, g2, wr_hi, wr_lo, br):
    T = x2.shape[0]
    tm = TM_ROUTE
    row = lambda w: pl.BlockSpec((tm, w), lambda i: (i, 0))
    full = lambda a: pl.BlockSpec(a.shape, lambda i: (0,) * a.ndim)
    earlier = jnp.asarray(np.triu(np.ones((tm, tm), np.float32), k=1), BF16)
    return pl.pallas_call(
        _out_route_kernel,
        grid=(T // tm,),
        in_specs=[row(512), row(512), row(D_MODEL), full(wo), full(g2),
                  full(wr_hi), full(wr_lo), full(br), full(earlier)],
        out_specs=[pl.BlockSpec(memory_space=pl.ANY), pl.BlockSpec(memory_space=pl.ANY),
                   pl.BlockSpec((ROUTE_ROWS, tm), lambda i: (0, i)),
                   pl.BlockSpec((N_EXPERTS, LANES), lambda i: (0, 0))],
        out_shape=[jax.ShapeDtypeStruct((T, SLAB_SUBLANES, LANES), F32),
                   jax.ShapeDtypeStruct((T, SLAB_SUBLANES, LANES), F32),
                   jax.ShapeDtypeStruct((ROUTE_ROWS, T), F32),
                   jax.ShapeDtypeStruct((N_EXPERTS, LANES), F32)],
        scratch_shapes=[pltpu.VMEM((2, 2, tm, D_MODEL), F32), pltpu.SemaphoreType.DMA((2,))],
        compiler_params=pltpu.CompilerParams(
            dimension_semantics=("arbitrary",), vmem_limit_bytes=VMEM_LIMIT,
            has_side_effects=True),
        name="out_route",
    )(oa, ob, x2, wo, g2, wr_hi, wr_lo, br, earlier)


def _row_copy(src, s, dst, d, sem):
    return pltpu.make_async_copy(src.at[pl.ds(s, 1)], dst.at[pl.ds(d, 1)], sem)


def _dispatch_kernel(dest_ref, pad_start_ref, pad_cnt_ref, nused_ref, hn_ref, xs_ref,
                     zero_ref, sem, zsem):
    i = pl.program_id(0)
    n = pl.num_programs(0)
    tm = EXPERT_PAD
    n_tok = dest_ref.shape[0] // TOP_K
    base = i * TM_DISPATCH
    for p in range(TM_DISPATCH * TOP_K):
        r, k = divmod(p, TOP_K)
        _row_copy(hn_ref, r, xs_ref, dest_ref[base + k * n_tok + r], sem).start(priority=p % 2)
    for _ in range(TOP_K):
        pltpu.make_async_copy(hn_ref, xs_ref.at[pl.ds(0, TM_DISPATCH)], sem).wait()

    @pl.when(i == n - 1)
    def _():
        zero_ref[...] = jnp.zeros_like(zero_ref)
        sub = 8
        for e in range(N_EXPERTS):
            start, cnt = pad_start_ref[e], pad_cnt_ref[e]
            head = cnt & (sub - 1)
            body0 = start + head
            nbody = cnt // sub

            def head_copy(r):
                return _row_copy(zero_ref, 0, xs_ref, start + r, zsem)

            def body_copy(c):
                return pltpu.make_async_copy(zero_ref.at[pl.ds(0, sub)],
                                             xs_ref.at[pl.ds(body0 + c * sub, sub)], zsem)

            for copy, count in ((head_copy, head), (body_copy, nbody)):
                def zissue(r, carry, copy=copy):
                    copy(r).start()
                    return carry

                def zwait(r, carry, copy=copy):
                    copy(r).wait()
                    return carry

                lax.fori_loop(0, count, zissue, 0)
                lax.fori_loop(0, count, zwait, 0)

        def tail_copy(t):
            return pltpu.make_async_copy(zero_ref, xs_ref.at[pl.ds(t * tm, tm)], zsem)

        def tissue(t, carry):
            tail_copy(t).start()
            return carry

        def twait(t, carry):
            tail_copy(t).wait()
            return carry

        n_tiles = xs_ref.shape[0] // tm
        lax.fori_loop(nused_ref[0], n_tiles, tissue, 0)
        lax.fori_loop(nused_ref[0], n_tiles, twait, 0)


def _dispatch(dest, pad_start, pad_cnt, n_used, hn, n_rows):
    T = hn.shape[0]
    any_spec = pl.BlockSpec(memory_space=pl.ANY)
    return pl.pallas_call(
        _dispatch_kernel,
        grid_spec=pltpu.PrefetchScalarGridSpec(
            num_scalar_prefetch=4,
            grid=(T // TM_DISPATCH,),
            in_specs=[pl.BlockSpec((TM_DISPATCH, SLAB_SUBLANES, LANES), lambda i, *_: (i, 0, 0))],
            out_specs=any_spec,
            scratch_shapes=[pltpu.VMEM((EXPERT_PAD, SLAB_SUBLANES, LANES), F32),
                            pltpu.SemaphoreType.DMA(()),
                            pltpu.SemaphoreType.DMA(())]),
        out_shape=jax.ShapeDtypeStruct((n_rows, SLAB_SUBLANES, LANES), F32),
        compiler_params=pltpu.CompilerParams(
            dimension_semantics=("arbitrary",), has_side_effects=True,
            vmem_limit_bytes=VMEM_LIMIT),
        name="dispatch",
    )(dest, pad_start, pad_cnt, n_used, hn)


def _experts_kernel(row0_ref, nfull_ref, tail_ref, xs_ref, wg_ref, bg_ref, wl_ref, bl_ref, wd_ref,
                    bd_ref, ys_ref, wg_bf, wl_bf, wd_bf, xbuf, ybuf, xsem, ysem, wf32, wsem):
    e = pl.program_id(0)
    tm, pad = TM_EXPERT, EXPERT_PAD
    r0, nf, tail = row0_ref[e], nfull_ref[e], tail_ref[e]
    nt = nf + jnp.minimum(tail, 1)

    def for_short_tile(pieces, fn):
        for q in range(1, tm // pad):
            @pl.when(pieces == q)
            def _():
                fn(q * pad)

    def x_load(row, slot, rows):
        return _SlabTileCopy(xbuf.at[slot, pl.ds(0, rows)], xs_ref.at[pl.ds(row, rows)],
                             xsem.at[slot], to_slab=False)

    def y_store(row, slot, rows):
        return _SlabTileCopy(ybuf.at[slot, pl.ds(0, rows)], ys_ref.at[pl.ds(row, rows)],
                             ysem.at[slot], to_slab=True)

    def start_first_load(expert):
        @pl.when(nfull_ref[expert] > 0)
        def _():
            x_load(row0_ref[expert], 0, tm).start()

        @pl.when(nfull_ref[expert] == 0)
        def _():
            for_short_tile(tail_ref[expert], lambda rows: x_load(row0_ref[expert], 0, rows).start())

    def compute(slot, rows):
        x = xbuf[slot, :rows].astype(BF16)
        glu = jnp.minimum(jnp.dot(x, wg_bf[...], preferred_element_type=F32) + bg_ref[0],
                          SWIGLU_LIMIT)
        lin = jnp.clip(jnp.dot(x, wl_bf[...], preferred_element_type=F32) + bl_ref[0],
                       -SWIGLU_LIMIT, SWIGLU_LIMIT)
        hid = glu * jax.nn.sigmoid(SWIGLU_ALPHA * glu) * (lin + 1.0)
        ybuf[slot, :rows] = (jnp.dot(hid.astype(BF16), wd_bf[...], preferred_element_type=F32)
                             + bd_ref[0])

    def w_fetch(expert):
        return [pltpu.make_async_copy(w.at[expert], wf32.at[i], wsem.at[i])
                for i, w in enumerate((wg_ref, wl_ref, wd_ref))]

    @pl.when(e == 0)
    def _():
        start_first_load(0)
        for cp in w_fetch(0):
            cp.start()

    for cp, w_bf, i in zip(w_fetch(e), (wg_bf, wl_bf, wd_bf), range(3)):
        cp.wait()
        w_bf[...] = wf32[i].astype(BF16)

    has_next = e + 1 < pl.num_programs(0)

    def fetch_next(i):
        @pl.when(has_next)
        def _():
            w_fetch(e + 1)[i].start()

    fetch_next(0)

    def full_tile(j, carry):
        slot = j % 2

        @pl.when(j + 1 < nf)
        def _():
            x_load(r0 + (j + 1) * tm, 1 - slot, tm).start()

        @pl.when(j + 1 == nf)
        def _():
            for_short_tile(tail, lambda rows: x_load(r0 + (j + 1) * tm, 1 - slot, rows).start())

        x_load(r0 + j * tm, slot, tm).wait()

        @pl.when(j >= 2)
        def _():
            y_store(r0 + (j - 2) * tm, slot, tm).wait()

        compute(slot, tm)
        y_store(r0 + j * tm, slot, tm).start()
        for i in (1, 2):
            @pl.when(j == i - 1)
            def _():
                fetch_next(i)
        return carry

    lax.fori_loop(0, nf, full_tile, 0)
    for i in (1, 2):
        @pl.when(nf < i)
        def _():
            fetch_next(i)

    def short_tile(rows):
        slot = nf % 2
        x_load(r0 + nf * tm, slot, rows).wait()

        @pl.when(nf >= 2)
        def _():
            y_store(r0 + (nf - 2) * tm, slot, tm).wait()

        compute(slot, rows)
        y_store(r0 + nf * tm, slot, rows).start()

    for_short_tile(tail, short_tile)

    @pl.when(e + 1 < pl.num_programs(0))
    def _():
        start_first_load(e + 1)

    @pl.when(nt >= 2)
    def _():
        y_store(r0 + (nt - 2) * tm, nt % 2, tm).wait()

    @pl.when((nt >= 1) & (tail == 0))
    def _():
        y_store(r0 + (nt - 1) * tm, (nt - 1) % 2, tm).wait()

    for_short_tile(tail, lambda rows: y_store(r0 + nf * tm, nf % 2, rows).wait())

    @pl.when(e == pl.num_programs(0) - 1)
    def _():
        ybuf[0] = jnp.zeros((tm, D_MODEL), F32)
        first = (r0 + nf * tm + tail * pad) // pad
        n_pieces = ys_ref.shape[0] // pad

        def fill_issue(t, carry):
            y_store(t * pad, 0, pad).start()
            return carry

        def fill_wait(t, carry):
            y_store(t * pad, 0, pad).wait()
            return carry

        lax.fori_loop(first, n_pieces, fill_issue, 0)
        lax.fori_loop(first, n_pieces, fill_wait, 0)


def _experts(row0, nfull, tail, xs, w_glu, b_glu, w_lin, b_lin, w_down, b_down):
    tm = TM_EXPERT
    any_spec = pl.BlockSpec(memory_space=pl.ANY)
    bspec = lambda: pl.BlockSpec((1, 1, D_FF), lambda e, *_: (e, 0, 0))
    return pl.pallas_call(
        _experts_kernel,
        grid_spec=pltpu.PrefetchScalarGridSpec(
            num_scalar_prefetch=3,
            grid=(N_EXPERTS,),
            in_specs=[any_spec, any_spec, bspec(), any_spec, bspec(), any_spec, bspec()],
            out_specs=any_spec,
            scratch_shapes=[pltpu.VMEM((D_MODEL, D_FF), BF16)] * 3
            + [pltpu.VMEM((2, tm, D_MODEL), F32), pltpu.VMEM((2, tm, D_MODEL), F32),
               pltpu.SemaphoreType.DMA((2,)), pltpu.SemaphoreType.DMA((2,)),
               pltpu.VMEM((3, D_MODEL, D_FF), F32), pltpu.SemaphoreType.DMA((3,))]),
        out_shape=jax.ShapeDtypeStruct(xs.shape, F32),
        compiler_params=pltpu.CompilerParams(
            dimension_semantics=("arbitrary",), vmem_limit_bytes=VMEM_LIMIT,
            has_side_effects=True),
        name="experts",
    )(row0, nfull, tail, xs, w_glu, b_glu.reshape(N_EXPERTS, 1, D_FF),
      w_lin, b_lin.reshape(N_EXPERTS, 1, D_FF), w_down, b_down.reshape(N_EXPERTS, 1, D_MODEL))


def _combine_kernel(dest_ref, gate_ref, ys_ref, h1_ref, g_ref, out_ref, buf, obuf, sems, osems):
    s = pl.program_id(0)
    tm = TM_COMBINE
    n = out_ref.shape[0] // tm
    n_tok = dest_ref.shape[0] // TOP_K

    def out_store(tile, slot):
        return _SlabTileCopy(out_ref.at[pl.ds(tile * tm, tm)], obuf.at[slot], osems.at[slot],
                             to_slab=False)

    def issue_row(slot, r):
        for k in range(TOP_K):
            _row_copy(ys_ref, dest_ref[s * tm + k * n_tok + r], buf.at[slot, k], r,
                      sems.at[slot]).start(priority=k % 2)

    def reduce_row(slot, r):
        acc = h1_ref[r]
        for k in range(TOP_K):
            acc = acc + gate_ref[(s - 1) * tm + k * n_tok + r] * buf[slot, k, r]
        ss = jnp.sum(jnp.sum(acc * acc, axis=1, keepdims=True), axis=0, keepdims=True)
        obuf[slot, r] = acc * lax.rsqrt(ss * (1.0 / D_MODEL) + NORM_EPS) * g_ref[0]

    def step_body(gather_slot, reduce_slot):
        for r in range(tm):
            if gather_slot is not None:
                issue_row(gather_slot, r)
            if reduce_slot is not None:
                reduce_row(reduce_slot, r)
        if reduce_slot is not None:
            out_store(s - 1, reduce_slot).start()

    for parity in range(2):
        other = 1 - parity

        @pl.when(s % 2 == parity)
        def _():
            @pl.when(s < n)
            def _():
                step_body(parity, None)

            @pl.when(s >= 3)
            def _():
                out_store(s - 3, other).wait()

            @pl.when(s > 0)
            def _():
                for k in range(TOP_K):
                    pltpu.make_async_copy(ys_ref.at[pl.ds(0, tm)], buf.at[other, k],
                                          sems.at[other]).wait()
                step_body(None, other)

            if n % 2 == parity:
                @pl.when(s == n)
                def _():
                    if n >= 2:
                        out_store(n - 2, parity).wait()
                    out_store(n - 1, other).wait()


def _combine(dest, ys3, h13, gate, fg):
    T = h13.shape[0]
    tm = TM_COMBINE
    slab = (SLAB_SUBLANES, LANES)
    prev_tile = lambda s, *_: (jnp.maximum(s - 1, 0), 0, 0)
    return pl.pallas_call(
        _combine_kernel,
        grid_spec=pltpu.PrefetchScalarGridSpec(
            num_scalar_prefetch=2,
            grid=(T // tm + 1,),
            in_specs=[pl.BlockSpec(memory_space=pl.ANY),
                      pl.BlockSpec((tm,) + slab, prev_tile),
                      pl.BlockSpec((1,) + slab, lambda s, *_: (0, 0, 0))],
            out_specs=pl.BlockSpec(memory_space=pl.ANY),
            scratch_shapes=[pltpu.VMEM((2, TOP_K, tm) + slab, F32),
                            pltpu.VMEM((2, tm) + slab, F32),
                            pltpu.SemaphoreType.DMA((2,)),
                            pltpu.SemaphoreType.DMA((2,))]),
        out_shape=jax.ShapeDtypeStruct((T, D_MODEL), F32),
        compiler_params=pltpu.CompilerParams(
            dimension_semantics=("arbitrary",), vmem_limit_bytes=VMEM_LIMIT,
            has_side_effects=True),
        name="combine",
    )(dest, gate, ys3, h13, fg.reshape((1,) + slab))


def _swap_halves(w):
    h = w.shape[-1] // 2
    return jnp.concatenate([w[..., h:], w[..., :h]], axis=-1)


def _layer(x2, batch, seq, norm1_g, w_in, w_alpha_up, b_alpha, gla_norm_g, swa_sinks, swa_norm_g,
           w_out, norm2_g, w_router, b_router, w_glu, b_glu, w_lin, b_lin, w_down, b_down):
    T = x2.shape[0]
    kb_w, vb_w = w_in[:, 2064:2192], w_in[:, 2192:2320]
    w_z = w_in[:, 1536:1552]
    w_cat = jnp.concatenate([
        w_in[:, 0:1536], w_in[:, 1552:2064],
        kb_w, _swap_halves(kb_w), vb_w, _swap_halves(vb_w),
        jnp.pad(jnp.tile(w_z, (1, Z_PIECES)), [(0, 0), (0, LANES - Z_PIECES * GLA_RANK)]),
    ], axis=1).astype(BF16)
    wup_hi = w_alpha_up.astype(BF16)
    wup_lo = (w_alpha_up - wup_hi.astype(F32)).astype(BF16)
    wup_cat = jnp.pad(jnp.concatenate([wup_hi, wup_hi, wup_hi, wup_lo, wup_lo], axis=0),
                      [(0, LANES - Z_PIECES * GLA_RANK), (0, 0)])

    qk, vr, la, qb, kv = _in_proj(
        x2, norm1_g.reshape(1, -1), w_cat, wup_cat, b_alpha.reshape(1, -1))
    oa = _gla(qk, vr, la, gla_norm_g.reshape(1, -1), batch, seq)
    ob = _swa(qb, kv, swa_sinks, jnp.tile(swa_norm_g, 2).reshape(1, -1), batch, seq)

    wo = w_out.astype(BF16)
    wr_t = w_router.T
    wr_hi = wr_t.astype(BF16)
    wr_lo = (wr_t - wr_hi.astype(F32)).astype(BF16)
    h1, hn, route, cnt = _out_route(oa, ob, x2, wo, norm2_g.reshape(1, -1),
                                    wr_hi, wr_lo, b_router.reshape(-1, 1))

    tm, pad = TM_EXPERT, EXPERT_PAD
    n_rows = T * TOP_K + N_EXPERTS * pad
    counts = cnt[:, 0].astype(jnp.int32)
    padded = (counts + pad - 1) // pad * pad
    pends = jnp.cumsum(padded)
    pstarts = pends - padded
    top_idx = route[0:TOP_K].astype(jnp.int32)
    gate = route[TOP_K:2 * TOP_K].reshape(-1)
    rank = route[2 * TOP_K:3 * TOP_K].astype(jnp.int32)
    experts = jnp.arange(N_EXPERTS)[:, None, None]
    seg_start = jnp.sum(jnp.where(top_idx[None] == experts, pstarts[:, None, None], 0), axis=0)
    dest = (seg_start + rank).reshape(-1)
    n_used = (pends[-1] // pad).reshape(1)
    xs = _dispatch(dest, pstarts + counts, padded - counts, n_used, hn, n_rows)
    ys = _experts(pstarts, padded // tm, padded % tm // pad, xs,
                  w_glu, b_glu, w_lin, b_lin, w_down, b_down)
    return dest, ys, h1, gate


def kernel(x, norm1_g, w_in, w_alpha_up, b_alpha, gla_norm_g, swa_sinks, swa_norm_g, w_out,
           norm2_g, w_router, b_router, w_glu, b_glu, w_lin, b_lin, w_down, b_down, final_g):
    batch, seq, d = x.shape
    assert norm1_g.shape[0] == 1, "single-layer problem"
    x2 = x.reshape(batch * seq, d)
    dest, ys, h1, gate = _layer(
        x2, batch, seq, norm1_g[0], w_in[0], w_alpha_up[0], b_alpha[0], gla_norm_g[0],
        swa_sinks[0], swa_norm_g[0], w_out[0], norm2_g[0], w_router[0], b_router[0],
        w_glu[0], b_glu[0], w_lin[0], b_lin[0], w_down[0], b_down[0])
    out = _combine(dest, ys, h1, gate, final_g.reshape(1, -1))
    return out.reshape(batch, seq, d)
```

```python
import numpy as np
import jax
import jax.numpy as jnp
from jax import lax
from jax.experimental import pallas as pl
from jax.experimental.pallas import tpu as pltpu

F32 = jnp.float32
BF16 = jnp.bfloat16

D_MODEL = 1024
GLA_HEADS = 4
GLA_DK = 64
GLA_DV = 128
GLA_RANK = 16
GLA_GATE_TAU = 16.0
GLA_CHUNK = 64
SWA_HEADS = 8
SWA_KV_HEADS = 2
SWA_HEAD_DIM = 64
SWA_WINDOW = 128
N_EXPERTS = 32
TOP_K = 4
D_FF = 1024
SWIGLU_LIMIT = 7.0
SWIGLU_ALPHA = 1.702
NORM_EPS = 1e-5

LANES = 128
SLAB_SUBLANES = D_MODEL // LANES
VMEM_LIMIT = 56 * 1024 * 1024

TM_PROJ = 1024
TL_GLA = 1024
GLA_CUM_ROWS = 256
TM_ROUTE = 1024
TM_EXPERT = 512
EXPERT_PAD = 128
TM_COMBINE = 512
SWA_SUB = 8
TM_DISPATCH = 512

NEG_BIG = -1e30
ROUTE_ROWS = 16

C_QA, C_KA, C_VA, C_RA, C_QB, C_KB, C_KBS, C_VB, C_VBS, C_Z, C_END = (
    0, 256, 512, 1024, 1536, 2048, 2176, 2304, 2432, 2560, 2688)
Z_PIECES = 5


def _split3(x):
    hi = x.astype(BF16).astype(F32)
    r = x - hi
    mid = r.astype(BF16).astype(F32)
    lo = (r - mid).astype(BF16).astype(F32)
    return hi, mid, lo


class _SlabTileCopy:
    def __init__(self, flat, slab, sem, to_slab):
        self.copies = []
        for c in range(SLAB_SUBLANES):
            pair = (flat.at[:, pl.ds(c * LANES, LANES)], slab.at[:, c, :])
            src, dst = pair if to_slab else pair[::-1]
            self.copies.append(pltpu.make_async_copy(src, dst, sem))

    def start(self):
        for cp in self.copies:
            cp.start()

    def wait(self):
        for cp in self.copies:
            cp.wait()


def _rms(x, g):
    return x * lax.rsqrt(jnp.mean(x * x, axis=-1, keepdims=True) + NORM_EPS) * g


def _in_proj_kernel(x_ref, g_ref, w_ref, wup_ref, ba_ref,
                    qk_ref, vr_ref, la_ref, qb_ref, kv_ref):
    u = _rms(x_ref[...], g_ref[...]).astype(BF16)

    def proj(c0, c1):
        return jnp.dot(u, w_ref[:, c0:c1], preferred_element_type=F32)

    qk_ref[...] = proj(C_QA, C_VA).astype(BF16)
    vr_ref[:, :C_RA - C_VA] = proj(C_VA, C_RA).astype(BF16)
    vr_ref[:, C_RA - C_VA:] = proj(C_RA, C_QB).astype(BF16)
    qb_ref[...] = proj(C_QB, C_KB).astype(BF16)
    kv_ref[...] = proj(C_KB, C_Z).astype(BF16)
    z = proj(C_Z, C_END)
    hi, mid, lo = _split3(z)
    piece = lax.broadcasted_iota(jnp.int32, z.shape, 1) // GLA_RANK
    zc = jnp.where((piece == 0) | (piece == 3), hi, jnp.where(piece == 2, lo, mid)).astype(BF16)
    y = jnp.dot(zc, wup_ref[...], preferred_element_type=F32) + ba_ref[...]
    log_sig = jnp.minimum(y, 0.0) - jnp.log1p(jnp.exp(-jnp.abs(y)))
    la_ref[...] = log_sig * (1.0 / GLA_GATE_TAU)


def _in_proj(x2, g1, w_cat, wup_p, ba_p):
    T = x2.shape[0]
    tm = TM_PROJ
    row = lambda w: pl.BlockSpec((tm, w), lambda i: (i, 0))
    full = lambda a: pl.BlockSpec(a.shape, lambda i: (0,) * a.ndim)
    outs = [(512, BF16), (1024, BF16), (256, F32), (512, BF16), (512, BF16)]
    return pl.pallas_call(
        _in_proj_kernel,
        grid=(T // tm,),
        in_specs=[row(D_MODEL), full(g1), full(w_cat), full(wup_p), full(ba_p)],
        out_specs=[row(w) for w, _ in outs],
        out_shape=[jax.ShapeDtypeStruct((T, w), dt) for w, dt in outs],
        compiler_params=pltpu.CompilerParams(
            dimension_semantics=("arbitrary",), vmem_limit_bytes=VMEM_LIMIT),
        name="in_proj",
    )(x2, g1, w_cat, wup_p, ba_p)


def _gla_kernel(qk_ref, vr_ref, la_ref, g_ref, cum_ref, o_ref, st_ref):
    @pl.when(pl.program_id(1) == 0)
    def _():
        st_ref[...] = jnp.zeros_like(st_ref)

    tl = TL_GLA
    c = GLA_CHUNK
    kw = GLA_HEADS * GLA_DK
    vw = GLA_HEADS * GLA_DV
    causal = (lax.broadcasted_iota(jnp.int32, (c, c), 0)
              >= lax.broadcasted_iota(jnp.int32, (c, c), 1))
    low_half = lax.broadcasted_iota(jnp.int32, (c, LANES), 1) < GLA_DK
    g = g_ref[...]
    b_groups = []
    for grp in range(tl // GLA_CUM_ROWS):
        la = la_ref[grp * GLA_CUM_ROWS:(grp + 1) * GLA_CUM_ROWS, :]
        pieces = jnp.concatenate([p.astype(BF16) for p in _split3(la)], axis=1)
        b3 = jnp.dot(cum_ref[...], pieces, preferred_element_type=F32)
        b_groups.append(b3[:, :kw] + b3[:, kw:2 * kw] + b3[:, 2 * kw:])
    for ch in range(tl // c):
        rows = slice(ch * c, (ch + 1) * c)
        in_grp = (ch * c) % GLA_CUM_ROWS
        b = b_groups[(ch * c) // GLA_CUM_ROWS][in_grp:in_grp + c]
        b_last = b[c - 1:c]
        qf = qk_ref[rows, :kw].astype(F32)
        kf = qk_ref[rows, kw:].astype(F32)
        q_e = (qf * jnp.exp(b) * (GLA_DK ** -0.5)).astype(BF16)
        k_e = (kf * jnp.exp(-b)).astype(BF16)
        k_t = (kf * jnp.exp(b_last - b)).astype(BF16)
        decay = jnp.exp(b_last)
        for h in range(GLA_HEADS):
            ps = slice((h // 2) * LANES, (h // 2 + 1) * LANES)
            mine = low_half if h % 2 == 0 else ~low_half
            qp, kp = q_e[:, ps], k_e[:, ps]
            qh = jnp.where(mine, qp, jnp.zeros_like(qp))
            kth = jnp.where(mine, k_t[:, ps], jnp.zeros_like(qp))
            vs = slice(h * GLA_DV, (h + 1) * GLA_DV)
            vh = vr_ref[rows, vs]
            a = pl.dot(qh, kp, trans_b=True)
            a = jnp.where(causal, a, 0.0).astype(BF16)
            st = st_ref[h]
            o = (jnp.dot(a, vh, preferred_element_type=F32)
                 + pl.dot(qh, st.astype(BF16), trans_b=True))
            st_ref[h] = st * decay[:, ps] + pl.dot(vh, kth, trans_a=True)
            rh = vr_ref[rows, slice(vw + h * GLA_DV, vw + (h + 1) * GLA_DV)].astype(F32)
            o = _rms(o, g) * (rh * jax.nn.sigmoid(rh))
            o_ref[rows, vs] = o.astype(BF16)


def _gla(qk, vr, la, g, batch, seq):
    tl = TL_GLA
    nl = seq // tl
    row = lambda w: pl.BlockSpec((tl, w), lambda b, i: (b * nl + i, 0))
    r = np.arange(GLA_CUM_ROWS)
    cum = jnp.asarray((r[None, :] <= r[:, None])
                      & (r[None, :] // GLA_CHUNK == r[:, None] // GLA_CHUNK), BF16)
    return pl.pallas_call(
        _gla_kernel,
        grid=(batch, nl),
        in_specs=[row(512), row(1024), row(256),
                  pl.BlockSpec((1, GLA_DV), lambda b, i: (0, 0)),
                  pl.BlockSpec((GLA_CUM_ROWS, GLA_CUM_ROWS), lambda b, i: (0, 0))],
        out_specs=row(512),
        out_shape=jax.ShapeDtypeStruct((batch * seq, 512), BF16),
        scratch_shapes=[pltpu.VMEM((GLA_HEADS, GLA_DV, LANES), F32)],
        compiler_params=pltpu.CompilerParams(
            dimension_semantics=("arbitrary", "arbitrary"), vmem_limit_bytes=VMEM_LIMIT),
        name="gla",
    )(qk, vr, la, g, cum)


def _swa_kernel(sink_ref, bias0_ref, bias_ref, q_ref, kvc_ref, kvp_ref, g_ref, o_ref):
    w = SWA_WINDOW
    scale = jnp.asarray(SWA_HEAD_DIM ** -0.5, BF16)
    kv_lane = lax.broadcasted_iota(jnp.int32, (2 * w, LANES), 1)
    lane_lo = kv_lane < SWA_HEAD_DIM
    out_lo = lax.broadcasted_iota(jnp.int32, (w, LANES), 1) < SWA_HEAD_DIM
    ones_hi = jnp.where(kv_lane == SWA_HEAD_DIM, 1.0, 0.0).astype(BF16)
    ones_lo = jnp.where(kv_lane == 0, 1.0, 0.0).astype(BF16)
    g = g_ref[...]

    for sb in range(SWA_SUB):
        rows = slice(sb * w, (sb + 1) * w)
        if sb == 0:
            kv = jnp.concatenate([kvp_ref[...], kvc_ref[0:w, :]], axis=0)
            bias = bias0_ref
        else:
            kv = kvc_ref[(sb - 1) * w:(sb + 1) * w, :]
            bias = bias_ref
        k = kv[:, 0:LANES] * scale
        ks = kv[:, LANES:2 * LANES] * scale
        v, vs = kv[:, 2 * LANES:3 * LANES], kv[:, 3 * LANES:]
        zero = jnp.zeros_like(k)
        k_low = [jnp.where(lane_lo, k, zero), jnp.where(lane_lo, ks, zero)]
        k_high = [jnp.where(lane_lo, zero, ks), jnp.where(lane_lo, zero, k)]
        v_low = [jnp.where(lane_lo, v, zero) + ones_hi, jnp.where(lane_lo, vs, zero) + ones_hi]
        v_high = [jnp.where(lane_lo, zero, vs) + ones_lo, jnp.where(lane_lo, zero, v) + ones_lo]

        def weights(s, head):
            s = s + bias[0, head]
            sink = sink_ref[head]
            m = jnp.maximum(jnp.max(s, axis=-1, keepdims=True), sink)
            return jnp.exp(s - m).astype(BF16), jnp.exp(sink - m)

        for pair in range(SWA_HEADS // 2):
            j = (2 * pair) // (SWA_HEADS // SWA_KV_HEADS)
            cols = slice(pair * LANES, (pair + 1) * LANES)
            qp = q_ref[rows, cols]
            e0, sink0 = weights(pl.dot(qp, k_low[j], trans_b=True), 2 * pair)
            e1, sink1 = weights(pl.dot(qp, k_high[j], trans_b=True), 2 * pair + 1)
            oa = jnp.dot(e0, v_low[j], preferred_element_type=F32)
            ob = jnp.dot(e1, v_high[j], preferred_element_type=F32)
            inv0 = 1.0 / (oa[:, SWA_HEAD_DIM:SWA_HEAD_DIM + 1] + sink0)
            inv1 = 1.0 / (ob[:, 0:1] + sink1)
            o = jnp.where(out_lo, oa * inv0, ob * inv1)
            sq = o * o
            ms_lo = jnp.sum(jnp.where(out_lo, sq, 0.0), axis=-1, keepdims=True)
            ms_hi = jnp.sum(jnp.where(out_lo, 0.0, sq), axis=-1, keepdims=True)
            ms = jnp.where(out_lo, ms_lo, ms_hi) * (1.0 / SWA_HEAD_DIM)
            o = o * lax.rsqrt(ms + NORM_EPS) * g
            o_ref[rows, cols] = o.astype(BF16)


def _swa_bias():
    w = SWA_WINDOW
    slopes = 2.0 ** (-8.0 * np.arange(1, SWA_HEADS + 1, dtype=np.float64) / SWA_HEADS)
    rel = np.arange(w)[:, None] + w - np.arange(2 * w)[None, :]
    in_window = (rel >= 0) & (rel < w)
    exists = np.stack([np.arange(2 * w) >= w, np.ones(2 * w, bool)])
    valid = in_window[None] & exists[:, None, :]
    bias = -slopes[None, :, None, None] * rel[None, None].astype(np.float64)
    return np.where(valid[:, None], bias, NEG_BIG).astype(np.float32)


def _swa(qb, kv, sinks, g2, batch, seq):
    w = SWA_WINDOW
    rows = SWA_SUB * w
    nb = seq // rows
    cur = lambda: pl.BlockSpec((rows, 512), lambda b, n: (b * nb + n, 0))
    prev = pl.BlockSpec(
        (w, 512), lambda b, n: (jnp.maximum((b * nb + n) * SWA_SUB - 1, b * nb * SWA_SUB), 0))
    table = lambda index: pl.BlockSpec((1, SWA_HEADS, w, 2 * w), index)
    bias = jnp.asarray(_swa_bias())
    return pl.pallas_call(
        _swa_kernel,
        grid=(batch, nb),
        in_specs=[pl.BlockSpec(memory_space=pltpu.SMEM),
                  table(lambda b, n: (jnp.minimum(n, 1), 0, 0, 0)),
                  table(lambda b, n: (1, 0, 0, 0)),
                  cur(), cur(), prev,
                  pl.BlockSpec((1, LANES), lambda b, n: (0, 0))],
        out_specs=cur(),
        out_shape=jax.ShapeDtypeStruct((batch * seq, 512), BF16),
        compiler_params=pltpu.CompilerParams(
            dimension_semantics=("arbitrary", "arbitrary"), vmem_limit_bytes=VMEM_LIMIT),
        name="swa",
    )(sinks, bias, bias, qb, kv, kv, g2)


def _out_route_kernel(oa_ref, ob_ref, x_ref, wo_ref, g_ref, wrh_ref, wrl_ref, br_ref,
                      earlier_ref, h13_ref, hn3_ref, route_ref, cnt_ref, hbuf, hsem):
    i = pl.program_id(0)
    tm = TM_ROUTE

    @pl.when(i == 0)
    def _():
        cnt_ref[...] = jnp.zeros_like(cnt_ref)

    def stores(step, slot):
        rows = pl.ds(step * tm, tm)
        return [_SlabTileCopy(hbuf.at[slot, which], out.at[rows], hsem.at[slot], to_slab=True)
                for which, out in enumerate((h13_ref, hn3_ref))]

    slot = i % 2

    @pl.when(i >= 2)
    def _():
        for st in stores(i - 2, slot):
            st.wait()

    h1 = (x_ref[...]
          + jnp.dot(oa_ref[...], wo_ref[:oa_ref.shape[1], :], preferred_element_type=F32)
          + jnp.dot(ob_ref[...], wo_ref[oa_ref.shape[1]:, :], preferred_element_type=F32))
    hn = _rms(h1, g_ref[...])
    hbuf[slot, 0] = h1
    hbuf[slot, 1] = hn
    for st in stores(i, slot):
        st.start()

    @pl.when(i == pl.num_programs(0) - 1)
    def _():
        @pl.when(i >= 1)
        def _():
            for st in stores(i - 1, 1 - slot):
                st.wait()
        for st in stores(i, slot):
            st.wait()

    hn_hi = hn.astype(BF16)
    hn_lo = (hn - hn_hi.astype(F32)).astype(BF16)
    logits = (pl.dot(wrh_ref[...], hn_hi, trans_b=True)
              + pl.dot(wrh_ref[...], hn_lo, trans_b=True)
              + pl.dot(wrl_ref[...], hn_hi, trans_b=True)) + br_ref[...]

    eid = lax.broadcasted_iota(jnp.int32, (N_EXPERTS, tm), 0)
    work = logits
    vals, idxs, sels = [], [], []
    for _ in range(TOP_K):
        m = jnp.max(work, axis=0, keepdims=True)
        idx = jnp.min(jnp.where(work == m, eid, N_EXPERTS), axis=0, keepdims=True)
        sel = eid == idx
        vals.append(m)
        idxs.append(idx)
        sels.append(sel)
        work = jnp.where(sel, -3e38, work)
    exps = [jnp.exp(v - vals[0]) for v in vals]
    inv_den = 1.0 / (exps[0] + exps[1] + exps[2] + exps[3])

    multihot = jnp.where(sels[0] | sels[1] | sels[2] | sels[3], 1.0, 0.0)
    before = (jnp.dot(multihot.astype(BF16), earlier_ref[...], preferred_element_type=F32)
              + cnt_ref[:, 0:1])
    ranks = [jnp.sum(jnp.where(s, before, 0.0), axis=0, keepdims=True) for s in sels]
    route_ref[...] = jnp.concatenate(
        [i.astype(F32) for i in idxs] + [e * inv_den for e in exps] + ranks
        + [jnp.zeros((ROUTE_ROWS - 3 * TOP_K, tm), F32)], axis=0)
    cnt_ref[...] += jnp.sum(multihot, axis=1, keepdims=True)


def _out_route(oa, ob, x2, wo, g2, wr_hi, wr_lo, br):
    T = x2.shape[0]
    tm = TM_ROUTE
    row = lambda w: pl.BlockSpec((tm, w), lambda i: (i, 0))
    full = lambda a: pl.BlockSpec(a.shape, lambda i: (0,) * a.ndim)
    earlier = jnp.asarray(np.triu(np.ones((tm, tm), np.float32), k=1), BF16)
    return pl.pallas_call(
        _out_route_kernel,
        grid=(T // tm,),
        in_specs=[row(512), row(512), row(D_MODEL), full(wo), full(g2),
                  full(wr_hi), full(wr_lo), full(br), full(earlier)],
        out_specs=[pl.BlockSpec(memory_space=pl.ANY), pl.BlockSpec(memory_space=pl.ANY),
                   pl.BlockSpec((ROUTE_ROWS, tm), lambda i: (0, i)),
                   pl.BlockSpec((N_EXPERTS, LANES), lambda i: (0, 0))],
        out_shape=[jax.ShapeDtypeStruct((T, SLAB_SUBLANES, LANES), F32),
                   jax.ShapeDtypeStruct((T, SLAB_SUBLANES, LANES), F32),
                   jax.ShapeDtypeStruct((ROUTE_ROWS, T), F32),
                   jax.ShapeDtypeStruct((N_EXPERTS, LANES), F32)],
        scratch_shapes=[pltpu.VMEM((2, 2, tm, D_MODEL), F32), pltpu.SemaphoreType.DMA((2,))],
        compiler_params=pltpu.CompilerParams(
            dimension_semantics=("arbitrary",), vmem_limit_bytes=VMEM_LIMIT,
            has_side_effects=True),
        name="out_route",
    )(oa, ob, x2, wo, g2, wr_hi, wr_lo, br, earlier)


def _row_copy(src, s, dst, d, sem):
    return pltpu.make_async_copy(src.at[pl.ds(s, 1)], dst.at[pl.ds(d, 1)], sem)


def _dispatch_kernel(dest_ref, pad_start_ref, pad_cnt_ref, nused_ref, hn_ref, xs_ref,
                     zero_ref, sem, zsem):
    i = pl.program_id(0)
    n = pl.num_programs(0)
    tm = EXPERT_PAD
    n_tok = dest_ref.shape[0] // TOP_K
    base = i * TM_DISPATCH
    for p in range(TM_DISPATCH * TOP_K):
        r, k = divmod(p, TOP_K)
        _row_copy(hn_ref, r, xs_ref, dest_ref[base + k * n_tok + r], sem).start(priority=p % 2)
    for _ in range(TOP_K):
        pltpu.make_async_copy(hn_ref, xs_ref.at[pl.ds(0, TM_DISPATCH)], sem).wait()

    @pl.when(i == n - 1)
    def _():
        zero_ref[...] = jnp.zeros_like(zero_ref)
        sub = 8
        for e in range(N_EXPERTS):
            start, cnt = pad_start_ref[e], pad_cnt_ref[e]
            head = cnt & (sub - 1)
            body0 = start + head
            nbody = cnt // sub

            def head_copy(r):
                return _row_copy(zero_ref, 0, xs_ref, start + r, zsem)

            def body_copy(c):
                return pltpu.make_async_copy(zero_ref.at[pl.ds(0, sub)],
                                             xs_ref.at[pl.ds(body0 + c * sub, sub)], zsem)

            for copy, count in ((head_copy, head), (body_copy, nbody)):
                def zissue(r, carry, copy=copy):
                    copy(r).start()
                    return carry

                def zwait(r, carry, copy=copy):
                    copy(r).wait()
                    return carry

                lax.fori_loop(0, count, zissue, 0)
                lax.fori_loop(0, count, zwait, 0)

        def tail_copy(t):
            return pltpu.make_async_copy(zero_ref, xs_ref.at[pl.ds(t * tm, tm)], zsem)

        def tissue(t, carry):
            tail_copy(t).start()
            return carry

        def twait(t, carry):
            tail_copy(t).wait()
            return carry

        n_tiles = xs_ref.shape[0] // tm
        lax.fori_loop(nused_ref[0], n_tiles, tissue, 0)
        lax.fori_loop(nused_ref[0], n_tiles, twait, 0)


def _dispatch(dest, pad_start, pad_cnt, n_used, hn, n_rows):
    T = hn.shape[0]
    any_spec = pl.BlockSpec(memory_space=pl.ANY)
    return pl.pallas_call(
        _dispatch_kernel,
        grid_spec=pltpu.PrefetchScalarGridSpec(
            num_scalar_prefetch=4,
            grid=(T // TM_DISPATCH,),
            in_specs=[pl.BlockSpec((TM_DISPATCH, SLAB_SUBLANES, LANES), lambda i, *_: (i, 0, 0))],
            out_specs=any_spec,
            scratch_shapes=[pltpu.VMEM((EXPERT_PAD, SLAB_SUBLANES, LANES), F32),
                            pltpu.SemaphoreType.DMA(()),
                            pltpu.SemaphoreType.DMA(())]),
        out_shape=jax.ShapeDtypeStruct((n_rows, SLAB_SUBLANES, LANES), F32),
        compiler_params=pltpu.CompilerParams(
            dimension_semantics=("arbitrary",), has_side_effects=True,
            vmem_limit_bytes=VMEM_LIMIT),
        name="dispatch",
    )(dest, pad_start, pad_cnt, n_used, hn)


def _experts_kernel(row0_ref, nfull_ref, tail_ref, xs_ref, wg_ref, bg_ref, wl_ref, bl_ref, wd_ref,
                    bd_ref, ys_ref, wg_bf, wl_bf, wd_bf, xbuf, ybuf, xsem, ysem, wf32, wsem):
    e = pl.program_id(0)
    tm, pad = TM_EXPERT, EXPERT_PAD
    r0, nf, tail = row0_ref[e], nfull_ref[e], tail_ref[e]
    nt = nf + jnp.minimum(tail, 1)

    def for_short_tile(pieces, fn):
        for q in range(1, tm // pad):
            @pl.when(pieces == q)
            def _():
                fn(q * pad)

    def x_load(row, slot, rows):
        return _SlabTileCopy(xbuf.at[slot, pl.ds(0, rows)], xs_ref.at[pl.ds(row, rows)],
                             xsem.at[slot], to_slab=False)

    def y_store(row, slot, rows):
        return _SlabTileCopy(ybuf.at[slot, pl.ds(0, rows)], ys_ref.at[pl.ds(row, rows)],
                             ysem.at[slot], to_slab=True)

    def start_first_load(expert):
        @pl.when(nfull_ref[expert] > 0)
        def _():
            x_load(row0_ref[expert], 0, tm).start()

        @pl.when(nfull_ref[expert] == 0)
        def _():
            for_short_tile(tail_ref[expert], lambda rows: x_load(row0_ref[expert], 0, rows).start())

    def compute(slot, rows):
        x = xbuf[slot, :rows].astype(BF16)
        glu = jnp.minimum(jnp.dot(x, wg_bf[...], preferred_element_type=F32) + bg_ref[0],
                          SWIGLU_LIMIT)
        lin = jnp.clip(jnp.dot(x, wl_bf[...], preferred_element_type=F32) + bl_ref[0],
                       -SWIGLU_LIMIT, SWIGLU_LIMIT)
        hid = glu * jax.nn.sigmoid(SWIGLU_ALPHA * glu) * (lin + 1.0)
        ybuf[slot, :rows] = (jnp.dot(hid.astype(BF16), wd_bf[...], preferred_element_type=F32)
                             + bd_ref[0])

    def w_fetch(expert):
        return [pltpu.make_async_copy(w.at[expert], wf32.at[i], wsem.at[i])
                for i, w in enumerate((wg_ref, wl_ref, wd_ref))]

    @pl.when(e == 0)
    def _():
        start_first_load(0)
        for cp in w_fetch(0):
            cp.start()

    for cp, w_bf, i in zip(w_fetch(e), (wg_bf, wl_bf, wd_bf), range(3)):
        cp.wait()
        w_bf[...] = wf32[i].astype(BF16)

    has_next = e + 1 < pl.num_programs(0)

    def fetch_next(i):
        @pl.when(has_next)
        def _():
            w_fetch(e + 1)[i].start()

    fetch_next(0)

    def full_tile(j, carry):
        slot = j % 2

        @pl.when(j + 1 < nf)
        def _():
            x_load(r0 + (j + 1) * tm, 1 - slot, tm).start()

        @pl.when(j + 1 == nf)
        def _():
            for_short_tile(tail, lambda rows: x_load(r0 + (j + 1) * tm, 1 - slot, rows).start())

        x_load(r0 + j * tm, slot, tm).wait()

        @pl.when(j >= 2)
        def _():
            y_store(r0 + (j - 2) * tm, slot, tm).wait()

        compute(slot, tm)
        y_store(r0 + j * tm, slot, tm).start()
        for i in (1, 2):
            @pl.when(j == i - 1)
            def _():
                fetch_next(i)
        return carry

    lax.fori_loop(0, nf, full_tile, 0)
    for i in (1, 2):
        @pl.when(nf < i)
        def _():
            fetch_next(i)

    def short_tile(rows):
        slot = nf % 2
        x_load(r0 + nf * tm, slot, rows).wait()

        @pl.when(nf >= 2)
        def _():
            y_store(r0 + (nf - 2) * tm, slot, tm).wait()

        compute(slot, rows)
        y_store(r0 + nf * tm, slot, rows).start()

    for_short_tile(tail, short_tile)

    @pl.when(e + 1 < pl.num_programs(0))
    def _():
        start_first_load(e + 1)

    @pl.when(nt >= 2)
    def _():
        y_store(r0 + (nt - 2) * tm, nt % 2, tm).wait()

    @pl.when((nt >= 1) & (tail == 0))
    def _():
        y_store(r0 + (nt - 1) * tm, (nt - 1) % 2, tm).wait()

    for_short_tile(tail, lambda rows: y_store(r0 + nf * tm, nf % 2, rows).wait())

    @pl.when(e == pl.num_programs(0) - 1)
    def _():
        ybuf[0] = jnp.zeros((tm, D_MODEL), F32)
        first = (r0 + nf * tm + tail * pad) // pad
        n_pieces = ys_ref.shape[0] // pad

        def fill_issue(t, carry):
            y_store(t * pad, 0, pad).start()
            return carry

        def fill_wait(t, carry):
            y_store(t * pad, 0, pad).wait()
            return carry

        lax.fori_loop(first, n_pieces, fill_issue, 0)
        lax.fori_loop(first, n_pieces, fill_wait, 0)


def _experts(row0, nfull, tail, xs, w_glu, b_glu, w_lin, b_lin, w_down, b_down):
    tm = TM_EXPERT
    any_spec = pl.BlockSpec(memory_space=pl.ANY)
    bspec = lambda: pl.BlockSpec((1, 1, D_FF), lambda e, *_: (e, 0, 0))
    return pl.pallas_call(
        _experts_kernel,
        grid_spec=pltpu.PrefetchScalarGridSpec(
            num_scalar_prefetch=3,
            grid=(N_EXPERTS,),
            in_specs=[any_spec, any_spec, bspec(), any_spec, bspec(), any_spec, bspec()],
            out_specs=any_spec,
            scratch_shapes=[pltpu.VMEM((D_MODEL, D_FF), BF16)] * 3
            + [pltpu.VMEM((2, tm, D_MODEL), F32), pltpu.VMEM((2, tm, D_MODEL), F32),
               pltpu.SemaphoreType.DMA((2,)), pltpu.SemaphoreType.DMA((2,)),
               pltpu.VMEM((3, D_MODEL, D_FF), F32), pltpu.SemaphoreType.DMA((3,))]),
        out_shape=jax.ShapeDtypeStruct(xs.shape, F32),
        compiler_params=pltpu.CompilerParams(
            dimension_semantics=("arbitrary",), vmem_limit_bytes=VMEM_LIMIT,
            has_side_effects=True),
        name="experts",
    )(row0, nfull, tail, xs, w_glu, b_glu.reshape(N_EXPERTS, 1, D_FF),
      w_lin, b_lin.reshape(N_EXPERTS, 1, D_FF), w_down, b_down.reshape(N_EXPERTS, 1, D_MODEL))


def _combine_kernel(dest_ref, gate_ref, ys_ref, h1_ref, g_ref, out_ref, buf, obuf, sems, osems):
    s = pl.program_id(0)
    tm = TM_COMBINE
    n = out_ref.shape[0] // tm
    n_tok = dest_ref.shape[0] // TOP_K

    def out_store(tile, slot):
        return _SlabTileCopy(out_ref.at[pl.ds(tile * tm, tm)], obuf.at[slot], osems.at[slot],
                             to_slab=False)

    def issue_row(slot, r):
        for k in range(TOP_K):
            _row_copy(ys_ref, dest_ref[s * tm + k * n_tok + r], buf.at[slot, k], r,
                      sems.at[slot]).start(priority=k % 2)

    def reduce_row(slot, r):
        acc = h1_ref[r]
        for k in range(TOP_K):
            acc = acc + gate_ref[(s - 1) * tm + k * n_tok + r] * buf[slot, k, r]
        ss = jnp.sum(jnp.sum(acc * acc, axis=1, keepdims=True), axis=0, keepdims=True)
        obuf[slot, r] = acc * lax.rsqrt(ss * (1.0 / D_MODEL) + NORM_EPS) * g_ref[0]

    def step_body(gather_slot, reduce_slot):
        for r in range(tm):
            if gather_slot is not None:
                issue_row(gather_slot, r)
            if reduce_slot is not None:
                reduce_row(reduce_slot, r)
        if reduce_slot is not None:
            out_store(s - 1, reduce_slot).start()

    for parity in range(2):
        other = 1 - parity

        @pl.when(s % 2 == parity)
        def _():
            @pl.when(s < n)
            def _():
                step_body(parity, None)

            @pl.when(s >= 3)
            def _():
                out_store(s - 3, other).wait()

            @pl.when(s > 0)
            def _():
                for k in range(TOP_K):
                    pltpu.make_async_copy(ys_ref.at[pl.ds(0, tm)], buf.at[other, k],
                                          sems.at[other]).wait()
                step_body(None, other)

            if n % 2 == parity:
                @pl.when(s == n)
                def _():
                    if n >= 2:
                        out_store(n - 2, parity).wait()
                    out_store(n - 1, other).wait()


def _combine(dest, ys3, h13, gate, fg):
    T = h13.shape[0]
    tm = TM_COMBINE
    slab = (SLAB_SUBLANES, LANES)
    prev_tile = lambda s, *_: (jnp.maximum(s - 1, 0), 0, 0)
    return pl.pallas_call(
        _combine_kernel,
        grid_spec=pltpu.PrefetchScalarGridSpec(
            num_scalar_prefetch=2,
            grid=(T // tm + 1,),
            in_specs=[pl.BlockSpec(memory_space=pl.ANY),
                      pl.BlockSpec((tm,) + slab, prev_tile),
                      pl.BlockSpec((1,) + slab, lambda s, *_: (0, 0, 0))],
            out_specs=pl.BlockSpec(memory_space=pl.ANY),
            scratch_shapes=[pltpu.VMEM((2, TOP_K, tm) + slab, F32),
                            pltpu.VMEM((2, tm) + slab, F32),
                            pltpu.SemaphoreType.DMA((2,)),
                            pltpu.SemaphoreType.DMA((2,))]),
        out_shape=jax.ShapeDtypeStruct((T, D_MODEL), F32),
        compiler_params=pltpu.CompilerParams(
            dimension_semantics=("arbitrary",), vmem_limit_bytes=VMEM_LIMIT,
            has_side_effects=True),
        name="combine",
    )(dest, gate, ys3, h13, fg.reshape((1,) + slab))


def _swap_halves(w):
    h = w.shape[-1] // 2
    return jnp.concatenate([w[..., h:], w[..., :h]], axis=-1)


def _layer(x2, batch, seq, norm1_g, w_in, w_alpha_up, b_alpha, gla_norm_g, swa_sinks, swa_norm_g,
           w_out, norm2_g, w_router, b_router, w_glu, b_glu, w_lin, b_lin, w_down, b_down):
    T = x2.shape[0]
    kb_w, vb_w = w_in[:, 2064:2192], w_in[:, 2192:2320]
    w_z = w_in[:, 1536:1552]
    w_cat = jnp.concatenate([
        w_in[:, 0:1536], w_in[:, 1552:2064],
        kb_w, _swap_halves(kb_w), vb_w, _swap_halves(vb_w),
        jnp.pad(jnp.tile(w_z, (1, Z_PIECES)), [(0, 0), (0, LANES - Z_PIECES * GLA_RANK)]),
    ], axis=1).astype(BF16)
    wup_hi = w_alpha_up.astype(BF16)
    wup_lo = (w_alpha_up - wup_hi.astype(F32)).astype(BF16)
    wup_cat = jnp.pad(jnp.concatenate([wup_hi, wup_hi, wup_hi, wup_lo, wup_lo], axis=0),
                      [(0, LANES - Z_PIECES * GLA_RANK), (0, 0)])

    qk, vr, la, qb, kv = _in_proj(
        x2, norm1_g.reshape(1, -1), w_cat, wup_cat, b_alpha.reshape(1, -1))
    oa = _gla(qk, vr, la, gla_norm_g.reshape(1, -1), batch, seq)
    ob = _swa(qb, kv, swa_sinks, jnp.tile(swa_norm_g, 2).reshape(1, -1), batch, seq)

    wo = w_out.astype(BF16)
    wr_t = w_router.T
    wr_hi = wr_t.astype(BF16)
    wr_lo = (wr_t - wr_hi.astype(F32)).astype(BF16)
    h1, hn, route, cnt = _out_route(oa, ob, x2, wo, norm2_g.reshape(1, -1),
                                    wr_hi, wr_lo, b_router.reshape(-1, 1))

    tm, pad = TM_EXPERT, EXPERT_PAD
    n_rows = T * TOP_K + N_EXPERTS * pad
    counts = cnt[:, 0].astype(jnp.int32)
    padded = (counts + pad - 1) // pad * pad
    pends = jnp.cumsum(padded)
    pstarts = pends - padded
    top_idx = route[0:TOP_K].astype(jnp.int32)
    gate = route[TOP_K:2 * TOP_K].reshape(-1)
    rank = route[2 * TOP_K:3 * TOP_K].astype(jnp.int32)
    experts = jnp.arange(N_EXPERTS)[:, None, None]
    seg_start = jnp.sum(jnp.where(top_idx[None] == experts, pstarts[:, None, None], 0), axis=0)
    dest = (seg_start + rank).reshape(-1)
    n_used = (pends[-1] // pad).reshape(1)
    xs = _dispatch(dest, pstarts + counts, padded - counts, n_used, hn, n_rows)
    ys = _experts(pstarts, padded // tm, padded % tm // pad, xs,
                  w_glu, b_glu, w_lin, b_lin, w_down, b_down)
    return dest, ys, h1, gate


def kernel(x, norm1_g, w_in, w_alpha_up, b_alpha, gla_norm_g, swa_sinks, swa_norm_g, w_out,
           norm2_g, w_router, b_router, w_glu, b_glu, w_lin, b_lin, w_down, b_down, final_g):
    batch, seq, d = x.shape
    assert norm1_g.shape[0] == 1, "single-layer problem"
    x2 = x.reshape(batch * seq, d)
    dest, ys, h1, gate = _layer(
        x2, batch, seq, norm1_g[0], w_in[0], w_alpha_up[0], b_alpha[0], gla_norm_g[0],
        swa_sinks[0], swa_norm_g[0], w_out[0], norm2_g[0], w_router[0], b_router[0],
        w_glu[0], b_glu[0], w_lin[0], b_lin[0], w_down[0], b_down[0])
    out = _combine(dest, ys, h1, gate, final_g.reshape(1, -1))
    return out.reshape(batch, seq, d)
```

```python
import numpy as np
import jax
import jax.numpy as jnp
from jax import lax
from jax.experimental import pallas as pl
from jax.experimental.pallas import tpu as pltpu

F32 = jnp.float32
BF16 = jnp.bfloat16

D_MODEL = 1024
GLA_HEADS = 4
GLA_DK = 64
GLA_DV = 128
GLA_RANK = 16
GLA_GATE_TAU = 16.0
GLA_CHUNK = 64
SWA_HEADS = 8
SWA_KV_HEADS = 2
SWA_HEAD_DIM = 64
SWA_WINDOW = 128
N_EXPERTS = 32
TOP_K = 4
D_FF = 1024
SWIGLU_LIMIT = 7.0
SWIGLU_ALPHA = 1.702
NORM_EPS = 1e-5

LANES = 128
SLAB_SUBLANES = D_MODEL // LANES
VMEM_LIMIT = 56 * 1024 * 1024

TM_PROJ = 1024
TL_GLA = 1024
GLA_CUM_ROWS = 256
TM_ROUTE = 1024
TM_EXPERT = 512
EXPERT_PAD = 128
TM_COMBINE = 256
SWA_SUB = 8
TM_DISPATCH = 512

NEG_BIG = -1e30
ROUTE_ROWS = 16

C_QA, C_KA, C_VA, C_RA, C_QB, C_KB, C_KBS, C_VB, C_VBS, C_Z, C_END = (
    0, 256, 512, 1024, 1536, 2048, 2176, 2304, 2432, 2560, 2688)
Z_PIECES = 5


def _split3(x):
    hi = x.astype(BF16).astype(F32)
    r = x - hi
    mid = r.astype(BF16).astype(F32)
    lo = (r - mid).astype(BF16).astype(F32)
    return hi, mid, lo


class _SlabTileCopy:
    def __init__(self, flat, slab, sem, to_slab):
        self.copies = []
        for c in range(SLAB_SUBLANES):
            pair = (flat.at[:, pl.ds(c * LANES, LANES)], slab.at[:, c, :])
            src, dst = pair if to_slab else pair[::-1]
            self.copies.append(pltpu.make_async_copy(src, dst, sem))

    def start(self):
        for cp in self.copies:
            cp.start()

    def wait(self):
        for cp in self.copies:
            cp.wait()


def _rms(x, g):
    return x * lax.rsqrt(jnp.mean(x * x, axis=-1, keepdims=True) + NORM_EPS) * g


def _in_proj_kernel(x_ref, g_ref, w_ref, wup_ref, ba_ref,
                    qk_ref, vr_ref, la_ref, qb_ref, kv_ref):
    u = _rms(x_ref[...], g_ref[...]).astype(BF16)

    def proj(c0, c1):
        return jnp.dot(u, w_ref[:, c0:c1], preferred_element_type=F32)

    qk_ref[...] = proj(C_QA, C_VA).astype(BF16)
    vr_ref[:, :C_RA - C_VA] = proj(C_VA, C_RA).astype(BF16)
    vr_ref[:, C_RA - C_VA:] = proj(C_RA, C_QB).astype(BF16)
    qb_ref[...] = proj(C_QB, C_KB).astype(BF16)
    kv_ref[...] = proj(C_KB, C_Z).astype(BF16)
    z = proj(C_Z, C_END)
    hi, mid, lo = _split3(z)
    piece = lax.broadcasted_iota(jnp.int32, z.shape, 1) // GLA_RANK
    zc = jnp.where((piece == 0) | (piece == 3), hi, jnp.where(piece == 2, lo, mid)).astype(BF16)
    y = jnp.dot(zc, wup_ref[...], preferred_element_type=F32) + ba_ref[...]
    log_sig = jnp.minimum(y, 0.0) - jnp.log1p(jnp.exp(-jnp.abs(y)))
    la_ref[...] = log_sig * (1.0 / GLA_GATE_TAU)


def _in_proj(x2, g1, w_cat, wup_p, ba_p):
    T = x2.shape[0]
    tm = TM_PROJ
    row = lambda w: pl.BlockSpec((tm, w), lambda i: (i, 0))
    full = lambda a: pl.BlockSpec(a.shape, lambda i: (0,) * a.ndim)
    outs = [(512, BF16), (1024, BF16), (256, F32), (512, BF16), (512, BF16)]
    return pl.pallas_call(
        _in_proj_kernel,
        grid=(T // tm,),
        in_specs=[row(D_MODEL), full(g1), full(w_cat), full(wup_p), full(ba_p)],
        out_specs=[row(w) for w, _ in outs],
        out_shape=[jax.ShapeDtypeStruct((T, w), dt) for w, dt in outs],
        compiler_params=pltpu.CompilerParams(
            dimension_semantics=("arbitrary",), vmem_limit_bytes=VMEM_LIMIT),
        name="in_proj",
    )(x2, g1, w_cat, wup_p, ba_p)


def _gla_kernel(qk_ref, vr_ref, la_ref, g_ref, cum_ref, o_ref, st_ref):
    @pl.when(pl.program_id(1) == 0)
    def _():
        st_ref[...] = jnp.zeros_like(st_ref)

    tl = TL_GLA
    c = GLA_CHUNK
    kw = GLA_HEADS * GLA_DK
    vw = GLA_HEADS * GLA_DV
    causal = (lax.broadcasted_iota(jnp.int32, (c, c), 0)
              >= lax.broadcasted_iota(jnp.int32, (c, c), 1))
    low_half = lax.broadcasted_iota(jnp.int32, (c, LANES), 1) < GLA_DK
    g = g_ref[...]
    b_groups = []
    for grp in range(tl // GLA_CUM_ROWS):
        la = la_ref[grp * GLA_CUM_ROWS:(grp + 1) * GLA_CUM_ROWS, :]
        pieces = jnp.concatenate([p.astype(BF16) for p in _split3(la)], axis=1)
        b3 = jnp.dot(cum_ref[...], pieces, preferred_element_type=F32)
        b_groups.append(b3[:, :kw] + b3[:, kw:2 * kw] + b3[:, 2 * kw:])
    for ch in range(tl // c):
        rows = slice(ch * c, (ch + 1) * c)
        in_grp = (ch * c) % GLA_CUM_ROWS
        b = b_groups[(ch * c) // GLA_CUM_ROWS][in_grp:in_grp + c]
        b_last = b[c - 1:c]
        qf = qk_ref[rows, :kw].astype(F32)
        kf = qk_ref[rows, kw:].astype(F32)
        q_e = (qf * jnp.exp(b) * (GLA_DK ** -0.5)).astype(BF16)
        k_e = (kf * jnp.exp(-b)).astype(BF16)
        k_t = (kf * jnp.exp(b_last - b)).astype(BF16)
        decay = jnp.exp(b_last)
        for h in range(GLA_HEADS):
            ps = slice((h // 2) * LANES, (h // 2 + 1) * LANES)
            mine = low_half if h % 2 == 0 else ~low_half
            qp, kp = q_e[:, ps], k_e[:, ps]
            qh = jnp.where(mine, qp, jnp.zeros_like(qp))
            kth = jnp.where(mine, k_t[:, ps], jnp.zeros_like(qp))
            vs = slice(h * GLA_DV, (h + 1) * GLA_DV)
            vh = vr_ref[rows, vs]
            a = pl.dot(qh, kp, trans_b=True)
            a = jnp.where(causal, a, 0.0).astype(BF16)
            st = st_ref[h]
            o = (jnp.dot(a, vh, preferred_element_type=F32)
                 + pl.dot(qh, st.astype(BF16), trans_b=True))
            st_ref[h] = st * decay[:, ps] + pl.dot(vh, kth, trans_a=True)
            rh = vr_ref[rows, slice(vw + h * GLA_DV, vw + (h + 1) * GLA_DV)].astype(F32)
            o = _rms(o, g) * (rh * jax.nn.sigmoid(rh))
            o_ref[rows, vs] = o.astype(BF16)


def _gla(qk, vr, la, g, batch, seq):
    tl = TL_GLA
    nl = seq // tl
    row = lambda w: pl.BlockSpec((tl, w), lambda b, i: (b * nl + i, 0))
    r = np.arange(GLA_CUM_ROWS)
    cum = jnp.asarray((r[None, :] <= r[:, None])
                      & (r[None, :] // GLA_CHUNK == r[:, None] // GLA_CHUNK), BF16)
    return pl.pallas_call(
        _gla_kernel,
        grid=(batch, nl),
        in_specs=[row(512), row(1024), row(256),
                  pl.BlockSpec((1, GLA_DV), lambda b, i: (0, 0)),
                  pl.BlockSpec((GLA_CUM_ROWS, GLA_CUM_ROWS), lambda b, i: (0, 0))],
        out_specs=row(512),
        out_shape=jax.ShapeDtypeStruct((batch * seq, 512), BF16),
        scratch_shapes=[pltpu.VMEM((GLA_HEADS, GLA_DV, LANES), F32)],
        compiler_params=pltpu.CompilerParams(
            dimension_semantics=("arbitrary", "arbitrary"), vmem_limit_bytes=VMEM_LIMIT),
        name="gla",
    )(qk, vr, la, g, cum)


def _swa_kernel(sink_ref, bias0_ref, bias_ref, q_ref, kvc_ref, kvp_ref, g_ref, o_ref):
    w = SWA_WINDOW
    scale = jnp.asarray(SWA_HEAD_DIM ** -0.5, BF16)
    kv_lane = lax.broadcasted_iota(jnp.int32, (2 * w, LANES), 1)
    lane_lo = kv_lane < SWA_HEAD_DIM
    out_lo = lax.broadcasted_iota(jnp.int32, (w, LANES), 1) < SWA_HEAD_DIM
    ones_hi = jnp.where(kv_lane == SWA_HEAD_DIM, 1.0, 0.0).astype(BF16)
    ones_lo = jnp.where(kv_lane == 0, 1.0, 0.0).astype(BF16)
    g = g_ref[...]

    for sb in range(SWA_SUB):
        rows = slice(sb * w, (sb + 1) * w)
        if sb == 0:
            kv = jnp.concatenate([kvp_ref[...], kvc_ref[0:w, :]], axis=0)
            bias = bias0_ref
        else:
            kv = kvc_ref[(sb - 1) * w:(sb + 1) * w, :]
            bias = bias_ref
        k = kv[:, 0:LANES] * scale
        ks = kv[:, LANES:2 * LANES] * scale
        v, vs = kv[:, 2 * LANES:3 * LANES], kv[:, 3 * LANES:]
        zero = jnp.zeros_like(k)
        k_low = [jnp.where(lane_lo, k, zero), jnp.where(lane_lo, ks, zero)]
        k_high = [jnp.where(lane_lo, zero, ks), jnp.where(lane_lo, zero, k)]
        v_low = [jnp.where(lane_lo, v, zero) + ones_hi, jnp.where(lane_lo, vs, zero) + ones_hi]
        v_high = [jnp.where(lane_lo, zero, vs) + ones_lo, jnp.where(lane_lo, zero, v) + ones_lo]

        def weights(s, head):
            s = s + bias[0, head]
            sink = sink_ref[head]
            m = jnp.maximum(jnp.max(s, axis=-1, keepdims=True), sink)
            return jnp.exp(s - m).astype(BF16), jnp.exp(sink - m)

        for pair in range(SWA_HEADS // 2):
            j = (2 * pair) // (SWA_HEADS // SWA_KV_HEADS)
            cols = slice(pair * LANES, (pair + 1) * LANES)
            qp = q_ref[rows, cols]
            e0, sink0 = weights(pl.dot(qp, k_low[j], trans_b=True), 2 * pair)
            e1, sink1 = weights(pl.dot(qp, k_high[j], trans_b=True), 2 * pair + 1)
            oa = jnp.dot(e0, v_low[j], preferred_element_type=F32)
            ob = jnp.dot(e1, v_high[j], preferred_element_type=F32)
            inv0 = 1.0 / (oa[:, SWA_HEAD_DIM:SWA_HEAD_DIM + 1] + sink0)
            inv1 = 1.0 / (ob[:, 0:1] + sink1)
            o = jnp.where(out_lo, oa * inv0, ob * inv1)
            sq = o * o
            ms_lo = jnp.sum(jnp.where(out_lo, sq, 0.0), axis=-1, keepdims=True)
            ms_hi = jnp.sum(jnp.where(out_lo, 0.0, sq), axis=-1, keepdims=True)
            ms = jnp.where(out_lo, ms_lo, ms_hi) * (1.0 / SWA_HEAD_DIM)
            o = o * lax.rsqrt(ms + NORM_EPS) * g
            o_ref[rows, cols] = o.astype(BF16)


def _swa_bias():
    w = SWA_WINDOW
    slopes = 2.0 ** (-8.0 * np.arange(1, SWA_HEADS + 1, dtype=np.float64) / SWA_HEADS)
    rel = np.arange(w)[:, None] + w - np.arange(2 * w)[None, :]
    in_window = (rel >= 0) & (rel < w)
    exists = np.stack([np.arange(2 * w) >= w, np.ones(2 * w, bool)])
    valid = in_window[None] & exists[:, None, :]
    bias = -slopes[None, :, None, None] * rel[None, None].astype(np.float64)
    return np.where(valid[:, None], bias, NEG_BIG).astype(np.float32)


def _swa(qb, kv, sinks, g2, batch, seq):
    w = SWA_WINDOW
    rows = SWA_SUB * w
    nb = seq // rows
    cur = lambda: pl.BlockSpec((rows, 512), lambda b, n: (b * nb + n, 0))
    prev = pl.BlockSpec(
        (w, 512), lambda b, n: (jnp.maximum((b * nb + n) * SWA_SUB - 1, b * nb * SWA_SUB), 0))
    table = lambda index: pl.BlockSpec((1, SWA_HEADS, w, 2 * w), index)
    bias = jnp.asarray(_swa_bias())
    return pl.pallas_call(
        _swa_kernel,
        grid=(batch, nb),
        in_specs=[pl.BlockSpec(memory_space=pltpu.SMEM),
                  table(lambda b, n: (jnp.minimum(n, 1), 0, 0, 0)),
                  table(lambda b, n: (1, 0, 0, 0)),
                  cur(), cur(), prev,
                  pl.BlockSpec((1, LANES), lambda b, n: (0, 0))],
        out_specs=cur(),
        out_shape=jax.ShapeDtypeStruct((batch * seq, 512), BF16),
        compiler_params=pltpu.CompilerParams(
            dimension_semantics=("arbitrary", "arbitrary"), vmem_limit_bytes=VMEM_LIMIT),
        name="swa",
    )(sinks, bias, bias, qb, kv, kv, g2)


def _out_route_kernel(oa_ref, ob_ref, x_ref, wo_ref, g_ref, wrh_ref, wrl_ref, br_ref,
                      earlier_ref, h13_ref, hn3_ref, route_ref, cnt_ref, hbuf, hsem):
    i = pl.program_id(0)
    tm = TM_ROUTE

    @pl.when(i == 0)
    def _():
        cnt_ref[...] = jnp.zeros_like(cnt_ref)

    def stores(step, slot):
        rows = pl.ds(step * tm, tm)
        return [_SlabTileCopy(hbuf.at[slot, which], out.at[rows], hsem.at[slot], to_slab=True)
                for which, out in enumerate((h13_ref, hn3_ref))]

    slot = i % 2

    @pl.when(i >= 2)
    def _():
        for st in stores(i - 2, slot):
            st.wait()

    h1 = (x_ref[...]
          + jnp.dot(oa_ref[...], wo_ref[:oa_ref.shape[1], :], preferred_element_type=F32)
          + jnp.dot(ob_ref[...], wo_ref[oa_ref.shape[1]:, :], preferred_element_type=F32))
    hn = _rms(h1, g_ref[...])
    hbuf[slot, 0] = h1
    hbuf[slot, 1] = hn
    for st in stores(i, slot):
        st.start()

    @pl.when(i == pl.num_programs(0) - 1)
    def _():
        @pl.when(i >= 1)
        def _():
            for st in stores(i - 1, 1 - slot):
                st.wait()
        for st in stores(i, slot):
            st.wait()

    hn_hi = hn.astype(BF16)
    hn_lo = (hn - hn_hi.astype(F32)).astype(BF16)
    logits = (pl.dot(wrh_ref[...], hn_hi, trans_b=True)
              + pl.dot(wrh_ref[...], hn_lo, trans_b=True)
              + pl.dot(wrl_ref[...], hn_hi, trans_b=True)) + br_ref[...]

    eid = lax.broadcasted_iota(jnp.int32, (N_EXPERTS, tm), 0)
    work = logits
    vals, idxs, sels = [], [], []
    for _ in range(TOP_K):
        m = jnp.max(work, axis=0, keepdims=True)
        idx = jnp.min(jnp.where(work == m, eid, N_EXPERTS), axis=0, keepdims=True)
        sel = eid == idx
        vals.append(m)
        idxs.append(idx)
        sels.append(sel)
        work = jnp.where(sel, -3e38, work)
    exps = [jnp.exp(v - vals[0]) for v in vals]
    inv_den = 1.0 / (exps[0] + exps[1] + exps[2] + exps[3])

    multihot = jnp.where(sels[0] | sels[1] | sels[2] | sels[3], 1.0, 0.0)
    before = (jnp.dot(multihot.astype(BF16), earlier_ref[...], preferred_element_type=F32)
              + cnt_ref[:, 0:1])
    ranks = [jnp.sum(jnp.where(s, before, 0.0), axis=0, keepdims=True) for s in sels]
    route_ref[...] = jnp.concatenate(
        [i.astype(F32) for i in idxs] + [e * inv_den for e in exps] + ranks
        + [jnp.zeros((ROUTE_ROWS - 3 * TOP_K, tm), F32)], axis=0)
    cnt_ref[...] += jnp.sum(multihot, axis=1, keepdims=True)


def _out_route(oa, ob, x2, wo, g2, wr_hi, wr_lo, br):
    T = x2.shape[0]
    tm = TM_ROUTE
    row = lambda w: pl.BlockSpec((tm, w), lambda i: (i, 0))
    full = lambda a: pl.BlockSpec(a.shape, lambda i: (0,) * a.ndim)
    earlier = jnp.asarray(np.triu(np.ones((tm, tm), np.float32), k=1), BF16)
    return pl.pallas_call(
        _out_route_kernel,
        grid=(T // tm,),
        in_specs=[row(512), row(512), row(D_MODEL), full(wo), full(g2),
                  full(wr_hi), full(wr_lo), full(br), full(earlier)],
        out_specs=[pl.BlockSpec(memory_space=pl.ANY), pl.BlockSpec(memory_space=pl.ANY),
                   pl.BlockSpec((ROUTE_ROWS, tm), lambda i: (0, i)),
                   pl.BlockSpec((N_EXPERTS, LANES), lambda i: (0, 0))],
        out_shape=[jax.ShapeDtypeStruct((T, SLAB_SUBLANES, LANES), F32),
                   jax.ShapeDtypeStruct((T, SLAB_SUBLANES, LANES), F32),
                   jax.ShapeDtypeStruct((ROUTE_ROWS, T), F32),
                   jax.ShapeDtypeStruct((N_EXPERTS, LANES), F32)],
        scratch_shapes=[pltpu.VMEM((2, 2, tm, D_MODEL), F32), pltpu.SemaphoreType.DMA((2,))],
        compiler_params=pltpu.CompilerParams(
            dimension_semantics=("arbitrary",), vmem_limit_bytes=VMEM_LIMIT,
            has_side_effects=True),
        name="out_route",
    )(oa, ob, x2, wo, g2, wr_hi, wr_lo, br, earlier)


def _row_copy(src, s, dst, d, sem):
    return pltpu.make_async_copy(src.at[pl.ds(s, 1)], dst.at[pl.ds(d, 1)], sem)


def _dispatch_kernel(dest_ref, pad_start_ref, pad_cnt_ref, nused_ref, hn_ref, xs_ref,
                     zero_ref, sem, zsem):
    i = pl.program_id(0)
    n = pl.num_programs(0)
    tm = EXPERT_PAD
    n_tok = dest_ref.shape[0] // TOP_K
    base = i * TM_DISPATCH
    for p in range(TM_DISPATCH * TOP_K):
        r, k = divmod(p, TOP_K)
        _row_copy(hn_ref, r, xs_ref, dest_ref[base + k * n_tok + r], sem).start(priority=p % 2)
    for _ in range(TOP_K):
        pltpu.make_async_copy(hn_ref, xs_ref.at[pl.ds(0, TM_DISPATCH)], sem).wait()

    @pl.when(i == n - 1)
    def _():
        zero_ref[...] = jnp.zeros_like(zero_ref)
        sub = 8
        for e in range(N_EXPERTS):
            start, cnt = pad_start_ref[e], pad_cnt_ref[e]
            head = cnt & (sub - 1)
            body0 = start + head
            nbody = cnt // sub

            def head_copy(r):
                return _row_copy(zero_ref, 0, xs_ref, start + r, zsem)

            def body_copy(c):
                return pltpu.make_async_copy(zero_ref.at[pl.ds(0, sub)],
                                             xs_ref.at[pl.ds(body0 + c * sub, sub)], zsem)

            for copy, count in ((head_copy, head), (body_copy, nbody)):
                def zissue(r, carry, copy=copy):
                    copy(r).start()
                    return carry

                def zwait(r, carry, copy=copy):
                    copy(r).wait()
                    return carry

                lax.fori_loop(0, count, zissue, 0)
                lax.fori_loop(0, count, zwait, 0)

        def tail_copy(t):
            return pltpu.make_async_copy(zero_ref, xs_ref.at[pl.ds(t * tm, tm)], zsem)

        def tissue(t, carry):
            tail_copy(t).start()
            return carry

        def twait(t, carry):
            tail_copy(t).wait()
            return carry

        n_tiles = xs_ref.shape[0] // tm
        lax.fori_loop(nused_ref[0], n_tiles, tissue, 0)
        lax.fori_loop(nused_ref[0], n_tiles, twait, 0)


def _dispatch(dest, pad_start, pad_cnt, n_used, hn, n_rows):
    T = hn.shape[0]
    any_spec = pl.BlockSpec(memory_space=pl.ANY)
    return pl.pallas_call(
        _dispatch_kernel,
        grid_spec=pltpu.PrefetchScalarGridSpec(
            num_scalar_prefetch=4,
            grid=(T // TM_DISPATCH,),
            in_specs=[pl.BlockSpec((TM_DISPATCH, SLAB_SUBLANES, LANES), lambda i, *_: (i, 0, 0))],
            out_specs=any_spec,
            scratch_shapes=[pltpu.VMEM((EXPERT_PAD, SLAB_SUBLANES, LANES), F32),
                            pltpu.SemaphoreType.DMA(()),
                            pltpu.SemaphoreType.DMA(())]),
        out_shape=jax.ShapeDtypeStruct((n_rows, SLAB_SUBLANES, LANES), F32),
        compiler_params=pltpu.CompilerParams(
            dimension_semantics=("arbitrary",), has_side_effects=True,
            vmem_limit_bytes=VMEM_LIMIT),
        name="dispatch",
    )(dest, pad_start, pad_cnt, n_used, hn)


def _experts_kernel(row0_ref, nfull_ref, tail_ref, xs_ref, wg_ref, bg_ref, wl_ref, bl_ref, wd_ref,
                    bd_ref, ys_ref, wg_bf, wl_bf, wd_bf, xbuf, ybuf, xsem, ysem, wf32, wsem):
    e = pl.program_id(0)
    tm, pad = TM_EXPERT, EXPERT_PAD
    r0, nf, tail = row0_ref[e], nfull_ref[e], tail_ref[e]
    nt = nf + jnp.minimum(tail, 1)

    def for_short_tile(pieces, fn):
        for q in range(1, tm // pad):
            @pl.when(pieces == q)
            def _():
                fn(q * pad)

    def x_load(row, slot, rows):
        return _SlabTileCopy(xbuf.at[slot, pl.ds(0, rows)], xs_ref.at[pl.ds(row, rows)],
                             xsem.at[slot], to_slab=False)

    def y_store(row, slot, rows):
        return _SlabTileCopy(ybuf.at[slot, pl.ds(0, rows)], ys_ref.at[pl.ds(row, rows)],
                             ysem.at[slot], to_slab=True)

    def start_first_load(expert):
        @pl.when(nfull_ref[expert] > 0)
        def _():
            x_load(row0_ref[expert], 0, tm).start()

        @pl.when(nfull_ref[expert] == 0)
        def _():
            for_short_tile(tail_ref[expert], lambda rows: x_load(row0_ref[expert], 0, rows).start())

    def compute(slot, rows):
        x = xbuf[slot, :rows].astype(BF16)
        glu = jnp.minimum(jnp.dot(x, wg_bf[...], preferred_element_type=F32) + bg_ref[0],
                          SWIGLU_LIMIT)
        lin = jnp.clip(jnp.dot(x, wl_bf[...], preferred_element_type=F32) + bl_ref[0],
                       -SWIGLU_LIMIT, SWIGLU_LIMIT)
        hid = glu * jax.nn.sigmoid(SWIGLU_ALPHA * glu) * (lin + 1.0)
        ybuf[slot, :rows] = (jnp.dot(hid.astype(BF16), wd_bf[...], preferred_element_type=F32)
                             + bd_ref[0])

    def w_fetch(expert):
        return [pltpu.make_async_copy(w.at[expert], wf32.at[i], wsem.at[i])
                for i, w in enumerate((wg_ref, wl_ref, wd_ref))]

    @pl.when(e == 0)
    def _():
        start_first_load(0)
        for cp in w_fetch(0):
            cp.start()

    for cp, w_bf, i in zip(w_fetch(e), (wg_bf, wl_bf, wd_bf), range(3)):
        cp.wait()
        w_bf[...] = wf32[i].astype(BF16)

    has_next = e + 1 < pl.num_programs(0)

    def fetch_next(i):
        @pl.when(has_next)
        def _():
            w_fetch(e + 1)[i].start()

    fetch_next(0)

    def full_tile(j, carry):
        slot = j % 2

        @pl.when(j + 1 < nf)
        def _():
            x_load(r0 + (j + 1) * tm, 1 - slot, tm).start()

        @pl.when(j + 1 == nf)
        def _():
            for_short_tile(tail, lambda rows: x_load(r0 + (j + 1) * tm, 1 - slot, rows).start())

        x_load(r0 + j * tm, slot, tm).wait()

        @pl.when(j >= 2)
        def _():
            y_store(r0 + (j - 2) * tm, slot, tm).wait()

        compute(slot, tm)
        y_store(r0 + j * tm, slot, tm).start()
        for i in (1, 2):
            @pl.when(j == i - 1)
            def _():
                fetch_next(i)
        return carry

    lax.fori_loop(0, nf, full_tile, 0)
    for i in (1, 2):
        @pl.when(nf < i)
        def _():
            fetch_next(i)

    def short_tile(rows):
        slot = nf % 2
        x_load(r0 + nf * tm, slot, rows).wait()

        @pl.when(nf >= 2)
        def _():
            y_store(r0 + (nf - 2) * tm, slot, tm).wait()

        compute(slot, rows)
        y_store(r0 + nf * tm, slot, rows).start()

    for_short_tile(tail, short_tile)

    @pl.when(e + 1 < pl.num_programs(0))
    def _():
        start_first_load(e + 1)

    @pl.when(nt >= 2)
    def _():
        y_store(r0 + (nt - 2) * tm, nt % 2, tm).wait()

    @pl.when((nt >= 1) & (tail == 0))
    def _():
        y_store(r0 + (nt - 1) * tm, (nt - 1) % 2, tm).wait()

    for_short_tile(tail, lambda rows: y_store(r0 + nf * tm, nf % 2, rows).wait())

    @pl.when(e == pl.num_programs(0) - 1)
    def _():
        ybuf[0] = jnp.zeros((tm, D_MODEL), F32)
        first = (r0 + nf * tm + tail * pad) // pad
        n_pieces = ys_ref.shape[0] // pad

        def fill_issue(t, carry):
            y_store(t * pad, 0, pad).start()
            return carry

        def fill_wait(t, carry):
            y_store(t * pad, 0, pad).wait()
            return carry

        lax.fori_loop(first, n_pieces, fill_issue, 0)
        lax.fori_loop(first, n_pieces, fill_wait, 0)


def _experts(row0, nfull, tail, xs, w_glu, b_glu, w_lin, b_lin, w_down, b_down):
    tm = TM_EXPERT
    any_spec = pl.BlockSpec(memory_space=pl.ANY)
    bspec = lambda: pl.BlockSpec((1, 1, D_FF), lambda e, *_: (e, 0, 0))
    return pl.pallas_call(
        _experts_kernel,
        grid_spec=pltpu.PrefetchScalarGridSpec(
            num_scalar_prefetch=3,
            grid=(N_EXPERTS,),
            in_specs=[any_spec, any_spec, bspec(), any_spec, bspec(), any_spec, bspec()],
            out_specs=any_spec,
            scratch_shapes=[pltpu.VMEM((D_MODEL, D_FF), BF16)] * 3
            + [pltpu.VMEM((2, tm, D_MODEL), F32), pltpu.VMEM((2, tm, D_MODEL), F32),
               pltpu.SemaphoreType.DMA((2,)), pltpu.SemaphoreType.DMA((2,)),
               pltpu.VMEM((3, D_MODEL, D_FF), F32), pltpu.SemaphoreType.DMA((3,))]),
        out_shape=jax.ShapeDtypeStruct(xs.shape, F32),
        compiler_params=pltpu.CompilerParams(
            dimension_semantics=("arbitrary",), vmem_limit_bytes=VMEM_LIMIT,
            has_side_effects=True),
        name="experts",
    )(row0, nfull, tail, xs, w_glu, b_glu.reshape(N_EXPERTS, 1, D_FF),
      w_lin, b_lin.reshape(N_EXPERTS, 1, D_FF), w_down, b_down.reshape(N_EXPERTS, 1, D_MODEL))


def _combine_kernel(dest_ref, gate_ref, ys_ref, h1_ref, g_ref, out_ref, buf, obuf, sems, osems):
    s = pl.program_id(0)
    tm = TM_COMBINE
    n = out_ref.shape[0] // tm
    n_tok = dest_ref.shape[0] // TOP_K

    def out_store(tile, slot):
        return _SlabTileCopy(out_ref.at[pl.ds(tile * tm, tm)], obuf.at[slot], osems.at[slot],
                             to_slab=False)

    def issue_row(slot, r):
        for k in range(TOP_K):
            _row_copy(ys_ref, dest_ref[s * tm + k * n_tok + r], buf.at[slot, k], r,
                      sems.at[slot]).start(priority=k % 2)

    def reduce_row(slot, r):
        acc = h1_ref[r]
        for k in range(TOP_K):
            acc = acc + gate_ref[(s - 1) * tm + k * n_tok + r] * buf[slot, k, r]
        ss = jnp.sum(jnp.sum(acc * acc, axis=1, keepdims=True), axis=0, keepdims=True)
        obuf[slot, r] = acc * lax.rsqrt(ss * (1.0 / D_MODEL) + NORM_EPS) * g_ref[0]

    def step_body(gather_slot, reduce_slot):
        for r in range(tm):
            if gather_slot is not None:
                issue_row(gather_slot, r)
            if reduce_slot is not None:
                reduce_row(reduce_slot, r)
        if reduce_slot is not None:
            out_store(s - 1, reduce_slot).start()

    for parity in range(2):
        other = 1 - parity

        @pl.when(s % 2 == parity)
        def _():
            @pl.when(s < n)
            def _():
                step_body(parity, None)

            @pl.when(s >= 3)
            def _():
                out_store(s - 3, other).wait()

            @pl.when(s > 0)
            def _():
                for k in range(TOP_K):
                    pltpu.make_async_copy(ys_ref.at[pl.ds(0, tm)], buf.at[other, k],
                                          sems.at[other]).wait()
                step_body(None, other)

            if n % 2 == parity:
                @pl.when(s == n)
                def _():
                    if n >= 2:
                        out_store(n - 2, parity).wait()
                    out_store(n - 1, other).wait()


def _combine(dest, ys3, h13, gate, fg):
    T = h13.shape[0]
    tm = TM_COMBINE
    slab = (SLAB_SUBLANES, LANES)
    prev_tile = lambda s, *_: (jnp.maximum(s - 1, 0), 0, 0)
    return pl.pallas_call(
        _combine_kernel,
        grid_spec=pltpu.PrefetchScalarGridSpec(
            num_scalar_prefetch=2,
            grid=(T // tm + 1,),
            in_specs=[pl.BlockSpec(memory_space=pl.ANY),
                      pl.BlockSpec((tm,) + slab, prev_tile),
                      pl.BlockSpec((1,) + slab, lambda s, *_: (0, 0, 0))],
            out_specs=pl.BlockSpec(memory_space=pl.ANY),
            scratch_shapes=[pltpu.VMEM((2, TOP_K, tm) + slab, F32),
                            pltpu.VMEM((2, tm) + slab, F32),
                            pltpu.SemaphoreType.DMA((2,)),
                            pltpu.SemaphoreType.DMA((2,))]),
        out_shape=jax.ShapeDtypeStruct((T, D_MODEL), F32),
        compiler_params=pltpu.CompilerParams(
            dimension_semantics=("arbitrary",), vmem_limit_bytes=VMEM_LIMIT,
            has_side_effects=True),
        name="combine",
    )(dest, gate, ys3, h13, fg.reshape((1,) + slab))


def _swap_halves(w):
    h = w.shape[-1] // 2
    return jnp.concatenate([w[..., h:], w[..., :h]], axis=-1)


def _layer(x2, batch, seq, norm1_g, w_in, w_alpha_up, b_alpha, gla_norm_g, swa_sinks, swa_norm_g,
           w_out, norm2_g, w_router, b_router, w_glu, b_glu, w_lin, b_lin, w_down, b_down):
    T = x2.shape[0]
    kb_w, vb_w = w_in[:, 2064:2192], w_in[:, 2192:2320]
    w_z = w_in[:, 1536:1552]
    w_cat = jnp.concatenate([
        w_in[:, 0:1536], w_in[:, 1552:2064],
        kb_w, _swap_halves(kb_w), vb_w, _swap_halves(vb_w),
        jnp.pad(jnp.tile(w_z, (1, Z_PIECES)), [(0, 0), (0, LANES - Z_PIECES * GLA_RANK)]),
    ], axis=1).astype(BF16)
    wup_hi = w_alpha_up.astype(BF16)
    wup_lo = (w_alpha_up - wup_hi.astype(F32)).astype(BF16)
    wup_cat = jnp.pad(jnp.concatenate([wup_hi, wup_hi, wup_hi, wup_lo, wup_lo], axis=0),
                      [(0, LANES - Z_PIECES * GLA_RANK), (0, 0)])

    qk, vr, la, qb, kv = _in_proj(
        x2, norm1_g.reshape(1, -1), w_cat, wup_cat, b_alpha.reshape(1, -1))
    oa = _gla(qk, vr, la, gla_norm_g.reshape(1, -1), batch, seq)
    ob = _swa(qb, kv, swa_sinks, jnp.tile(swa_norm_g, 2).reshape(1, -1), batch, seq)

    wo = w_out.astype(BF16)
    wr_t = w_router.T
    wr_hi = wr_t.astype(BF16)
    wr_lo = (wr_t - wr_hi.astype(F32)).astype(BF16)
    h1, hn, route, cnt = _out_route(oa, ob, x2, wo, norm2_g.reshape(1, -1),
                                    wr_hi, wr_lo, b_router.reshape(-1, 1))

    tm, pad = TM_EXPERT, EXPERT_PAD
    n_rows = T * TOP_K + N_EXPERTS * pad
    counts = cnt[:, 0].astype(jnp.int32)
    padded = (counts + pad - 1) // pad * pad
    pends = jnp.cumsum(padded)
    pstarts = pends - padded
    top_idx = route[0:TOP_K].astype(jnp.int32)
    gate = route[TOP_K:2 * TOP_K].reshape(-1)
    rank = route[2 * TOP_K:3 * TOP_K].astype(jnp.int32)
    experts = jnp.arange(N_EXPERTS)[:, None, None]
    seg_start = jnp.sum(jnp.where(top_idx[None] == experts, pstarts[:, None, None], 0), axis=0)
    dest = (seg_start + rank).reshape(-1)
    n_used = (pends[-1] // pad).reshape(1)
    xs = _dispatch(dest, pstarts + counts, padded - counts, n_used, hn, n_rows)
    ys = _experts(pstarts, padded // tm, padded % tm // pad, xs,
                  w_glu, b_glu, w_lin, b_lin, w_down, b_down)
    return dest, ys, h1, gate


def kernel(x, norm1_g, w_in, w_alpha_up, b_alpha, gla_norm_g, swa_sinks, swa_norm_g, w_out,
           norm2_g, w_router, b_router, w_glu, b_glu, w_lin, b_lin, w_down, b_down, final_g):
    batch, seq, d = x.shape
    assert norm1_g.shape[0] == 1, "single-layer problem"
    x2 = x.reshape(batch * seq, d)
    dest, ys, h1, gate = _layer(
        x2, batch, seq, norm1_g[0], w_in[0], w_alpha_up[0], b_alpha[0], gla_norm_g[0],
        swa_sinks[0], swa_norm_g[0], w_out[0], norm2_g[0], w_router[0], b_router[0],
        w_glu[0], b_glu[0], w_lin[0], b_lin[0], w_down[0], b_down[0])
    out = _combine(dest, ys, h1, gate, final_g.reshape(1, -1))
    return out.reshape(batch, seq, d)
```

```python
import numpy as np
import jax
import jax.numpy as jnp
from jax import lax
from jax.experimental import pallas as pl
from jax.experimental.pallas import tpu as pltpu

F32 = jnp.float32
BF16 = jnp.bfloat16

D_MODEL = 1024
GLA_HEADS = 4
GLA_DK = 64
GLA_DV = 128
GLA_RANK = 16
GLA_GATE_TAU = 16.0
GLA_CHUNK = 64
SWA_HEADS = 8
SWA_KV_HEADS = 2
SWA_HEAD_DIM = 64
SWA_WINDOW = 128
N_EXPERTS = 32
TOP_K = 4
D_FF = 1024
SWIGLU_LIMIT = 7.0
SWIGLU_ALPHA = 1.702
NORM_EPS = 1e-5

LANES = 128
SLAB_SUBLANES = D_MODEL // LANES
VMEM_LIMIT = 56 * 1024 * 1024

TM_PROJ = 1024
TL_GLA = 1024
GLA_CUM_ROWS = 256
TM_ROUTE = 1024
TM_EXPERT = 512
EXPERT_PAD = 128
TM_COMBINE = 128
SWA_SUB = 8
TM_DISPATCH = 1024

NEG_BIG = -1e30
ROUTE_ROWS = 16

C_QA, C_KA, C_VA, C_RA, C_QB, C_KB, C_KBS, C_VB, C_VBS, C_Z, C_END = (
    0, 256, 512, 1024, 1536, 2048, 2176, 2304, 2432, 2560, 2688)
Z_PIECES = 5


def _split3(x):
    hi = x.astype(BF16).astype(F32)
    r = x - hi
    mid = r.astype(BF16).astype(F32)
    lo = (r - mid).astype(BF16).astype(F32)
    return hi, mid, lo


class _SlabTileCopy:
    def __init__(self, flat, slab, sem, to_slab):
        self.copies = []
        for c in range(SLAB_SUBLANES):
            pair = (flat.at[:, pl.ds(c * LANES, LANES)], slab.at[:, c, :])
            src, dst = pair if to_slab else pair[::-1]
            self.copies.append(pltpu.make_async_copy(src, dst, sem))

    def start(self):
        for cp in self.copies:
            cp.start()

    def wait(self):
        for cp in self.copies:
            cp.wait()


def _rms(x, g):
    return x * lax.rsqrt(jnp.mean(x * x, axis=-1, keepdims=True) + NORM_EPS) * g


def _in_proj_kernel(x_ref, g_ref, w_ref, wup_ref, ba_ref,
                    qk_ref, vr_ref, la_ref, qb_ref, kv_ref):
    u = _rms(x_ref[...], g_ref[...]).astype(BF16)

    def proj(c0, c1):
        return jnp.dot(u, w_ref[:, c0:c1], preferred_element_type=F32)

    qk_ref[...] = proj(C_QA, C_VA).astype(BF16)
    vr_ref[:, :C_RA - C_VA] = proj(C_VA, C_RA).astype(BF16)
    vr_ref[:, C_RA - C_VA:] = proj(C_RA, C_QB).astype(BF16)
    qb_ref[...] = proj(C_QB, C_KB).astype(BF16)
    kv_ref[...] = proj(C_KB, C_Z).astype(BF16)
    z = proj(C_Z, C_END)
    hi, mid, lo = _split3(z)
    piece = lax.broadcasted_iota(jnp.int32, z.shape, 1) // GLA_RANK
    zc = jnp.where((piece == 0) | (piece == 3), hi, jnp.where(piece == 2, lo, mid)).astype(BF16)
    y = jnp.dot(zc, wup_ref[...], preferred_element_type=F32) + ba_ref[...]
    log_sig = jnp.minimum(y, 0.0) - jnp.log1p(jnp.exp(-jnp.abs(y)))
    la_ref[...] = log_sig * (1.0 / GLA_GATE_TAU)


def _in_proj(x2, g1, w_cat, wup_p, ba_p):
    T = x2.shape[0]
    tm = TM_PROJ
    row = lambda w: pl.BlockSpec((tm, w), lambda i: (i, 0))
    full = lambda a: pl.BlockSpec(a.shape, lambda i: (0,) * a.ndim)
    outs = [(512, BF16), (1024, BF16), (256, F32), (512, BF16), (512, BF16)]
    return pl.pallas_call(
        _in_proj_kernel,
        grid=(T // tm,),
        in_specs=[row(D_MODEL), full(g1), full(w_cat), full(wup_p), full(ba_p)],
        out_specs=[row(w) for w, _ in outs],
        out_shape=[jax.ShapeDtypeStruct((T, w), dt) for w, dt in outs],
        compiler_params=pltpu.CompilerParams(
            dimension_semantics=("arbitrary",), vmem_limit_bytes=VMEM_LIMIT),
        name="in_proj",
    )(x2, g1, w_cat, wup_p, ba_p)


def _gla_kernel(qk_ref, vr_ref, la_ref, g_ref, cum_ref, o_ref, st_ref):
    @pl.when(pl.program_id(1) == 0)
    def _():
        st_ref[...] = jnp.zeros_like(st_ref)

    tl = TL_GLA
    c = GLA_CHUNK
    kw = GLA_HEADS * GLA_DK
    vw = GLA_HEADS * GLA_DV
    causal = (lax.broadcasted_iota(jnp.int32, (c, c), 0)
              >= lax.broadcasted_iota(jnp.int32, (c, c), 1))
    low_half = lax.broadcasted_iota(jnp.int32, (c, LANES), 1) < GLA_DK
    g = g_ref[...]
    b_groups = []
    for grp in range(tl // GLA_CUM_ROWS):
        la = la_ref[grp * GLA_CUM_ROWS:(grp + 1) * GLA_CUM_ROWS, :]
        pieces = jnp.concatenate([p.astype(BF16) for p in _split3(la)], axis=1)
        b3 = jnp.dot(cum_ref[...], pieces, preferred_element_type=F32)
        b_groups.append(b3[:, :kw] + b3[:, kw:2 * kw] + b3[:, 2 * kw:])
    for ch in range(tl // c):
        rows = slice(ch * c, (ch + 1) * c)
        in_grp = (ch * c) % GLA_CUM_ROWS
        b = b_groups[(ch * c) // GLA_CUM_ROWS][in_grp:in_grp + c]
        b_last = b[c - 1:c]
        qf = qk_ref[rows, :kw].astype(F32)
        kf = qk_ref[rows, kw:].astype(F32)
        q_e = (qf * jnp.exp(b) * (GLA_DK ** -0.5)).astype(BF16)
        k_e = (kf * jnp.exp(-b)).astype(BF16)
        k_t = (kf * jnp.exp(b_last - b)).astype(BF16)
        decay = jnp.exp(b_last)
        for h in range(GLA_HEADS):
            ps = slice((h // 2) * LANES, (h // 2 + 1) * LANES)
            mine = low_half if h % 2 == 0 else ~low_half
            qp, kp = q_e[:, ps], k_e[:, ps]
            qh = jnp.where(mine, qp, jnp.zeros_like(qp))
            kth = jnp.where(mine, k_t[:, ps], jnp.zeros_like(qp))
            vs = slice(h * GLA_DV, (h + 1) * GLA_DV)
            vh = vr_ref[rows, vs]
            a = pl.dot(qh, kp, trans_b=True)
            a = jnp.where(causal, a, 0.0).astype(BF16)
            st = st_ref[h]
            o = (jnp.dot(a, vh, preferred_element_type=F32)
                 + pl.dot(qh, st.astype(BF16), trans_b=True))
            st_ref[h] = st * decay[:, ps] + pl.dot(vh, kth, trans_a=True)
            rh = vr_ref[rows, slice(vw + h * GLA_DV, vw + (h + 1) * GLA_DV)].astype(F32)
            o = _rms(o, g) * (rh * jax.nn.sigmoid(rh))
            o_ref[rows, vs] = o.astype(BF16)


def _gla(qk, vr, la, g, batch, seq):
    tl = TL_GLA
    nl = seq // tl
    row = lambda w: pl.BlockSpec((tl, w), lambda b, i: (b * nl + i, 0))
    r = np.arange(GLA_CUM_ROWS)
    cum = jnp.asarray((r[None, :] <= r[:, None])
                      & (r[None, :] // GLA_CHUNK == r[:, None] // GLA_CHUNK), BF16)
    return pl.pallas_call(
        _gla_kernel,
        grid=(batch, nl),
        in_specs=[row(512), row(1024), row(256),
                  pl.BlockSpec((1, GLA_DV), lambda b, i: (0, 0)),
                  pl.BlockSpec((GLA_CUM_ROWS, GLA_CUM_ROWS), lambda b, i: (0, 0))],
        out_specs=row(512),
        out_shape=jax.ShapeDtypeStruct((batch * seq, 512), BF16),
        scratch_shapes=[pltpu.VMEM((GLA_HEADS, GLA_DV, LANES), F32)],
        compiler_params=pltpu.CompilerParams(
            dimension_semantics=("arbitrary", "arbitrary"), vmem_limit_bytes=VMEM_LIMIT),
        name="gla",
    )(qk, vr, la, g, cum)


def _swa_kernel(sink_ref, bias0_ref, bias_ref, q_ref, kvc_ref, kvp_ref, g_ref, o_ref):
    w = SWA_WINDOW
    scale = jnp.asarray(SWA_HEAD_DIM ** -0.5, BF16)
    kv_lane = lax.broadcasted_iota(jnp.int32, (2 * w, LANES), 1)
    lane_lo = kv_lane < SWA_HEAD_DIM
    out_lo = lax.broadcasted_iota(jnp.int32, (w, LANES), 1) < SWA_HEAD_DIM
    ones_hi = jnp.where(kv_lane == SWA_HEAD_DIM, 1.0, 0.0).astype(BF16)
    ones_lo = jnp.where(kv_lane == 0, 1.0, 0.0).astype(BF16)
    g = g_ref[...]

    for sb in range(SWA_SUB):
        rows = slice(sb * w, (sb + 1) * w)
        if sb == 0:
            kv = jnp.concatenate([kvp_ref[...], kvc_ref[0:w, :]], axis=0)
            bias = bias0_ref
        else:
            kv = kvc_ref[(sb - 1) * w:(sb + 1) * w, :]
            bias = bias_ref
        k = kv[:, 0:LANES] * scale
        ks = kv[:, LANES:2 * LANES] * scale
        v, vs = kv[:, 2 * LANES:3 * LANES], kv[:, 3 * LANES:]
        zero = jnp.zeros_like(k)
        k_low = [jnp.where(lane_lo, k, zero), jnp.where(lane_lo, ks, zero)]
        k_high = [jnp.where(lane_lo, zero, ks), jnp.where(lane_lo, zero, k)]
        v_low = [jnp.where(lane_lo, v, zero) + ones_hi, jnp.where(lane_lo, vs, zero) + ones_hi]
        v_high = [jnp.where(lane_lo, zero, vs) + ones_lo, jnp.where(lane_lo, zero, v) + ones_lo]

        def weights(s, head):
            s = s + bias[0, head]
            sink = sink_ref[head]
            m = jnp.maximum(jnp.max(s, axis=-1, keepdims=True), sink)
            return jnp.exp(s - m).astype(BF16), jnp.exp(sink - m)

        for pair in range(SWA_HEADS // 2):
            j = (2 * pair) // (SWA_HEADS // SWA_KV_HEADS)
            cols = slice(pair * LANES, (pair + 1) * LANES)
            qp = q_ref[rows, cols]
            e0, sink0 = weights(pl.dot(qp, k_low[j], trans_b=True), 2 * pair)
            e1, sink1 = weights(pl.dot(qp, k_high[j], trans_b=True), 2 * pair + 1)
            oa = jnp.dot(e0, v_low[j], preferred_element_type=F32)
            ob = jnp.dot(e1, v_high[j], preferred_element_type=F32)
            inv0 = 1.0 / (oa[:, SWA_HEAD_DIM:SWA_HEAD_DIM + 1] + sink0)
            inv1 = 1.0 / (ob[:, 0:1] + sink1)
            o = jnp.where(out_lo, oa * inv0, ob * inv1)
            sq = o * o
            ms_lo = jnp.sum(jnp.where(out_lo, sq, 0.0), axis=-1, keepdims=True)
            ms_hi = jnp.sum(jnp.where(out_lo, 0.0, sq), axis=-1, keepdims=True)
            ms = jnp.where(out_lo, ms_lo, ms_hi) * (1.0 / SWA_HEAD_DIM)
            o = o * lax.rsqrt(ms + NORM_EPS) * g
            o_ref[rows, cols] = o.astype(BF16)


def _swa_bias():
    w = SWA_WINDOW
    slopes = 2.0 ** (-8.0 * np.arange(1, SWA_HEADS + 1, dtype=np.float64) / SWA_HEADS)
    rel = np.arange(w)[:, None] + w - np.arange(2 * w)[None, :]
    in_window = (rel >= 0) & (rel < w)
    exists = np.stack([np.arange(2 * w) >= w, np.ones(2 * w, bool)])
    valid = in_window[None] & exists[:, None, :]
    bias = -slopes[None, :, None, None] * rel[None, None].astype(np.float64)
    return np.where(valid[:, None], bias, NEG_BIG).astype(np.float32)


def _swa(qb, kv, sinks, g2, batch, seq):
    w = SWA_WINDOW
    rows = SWA_SUB * w
    nb = seq // rows
    cur = lambda: pl.BlockSpec((rows, 512), lambda b, n: (b * nb + n, 0))
    prev = pl.BlockSpec(
        (w, 512), lambda b, n: (jnp.maximum((b * nb + n) * SWA_SUB - 1, b * nb * SWA_SUB), 0))
    table = lambda index: pl.BlockSpec((1, SWA_HEADS, w, 2 * w), index)
    bias = jnp.asarray(_swa_bias())
    return pl.pallas_call(
        _swa_kernel,
        grid=(batch, nb),
        in_specs=[pl.BlockSpec(memory_space=pltpu.SMEM),
                  table(lambda b, n: (jnp.minimum(n, 1), 0, 0, 0)),
                  table(lambda b, n: (1, 0, 0, 0)),
                  cur(), cur(), prev,
                  pl.BlockSpec((1, LANES), lambda b, n: (0, 0))],
        out_specs=cur(),
        out_shape=jax.ShapeDtypeStruct((batch * seq, 512), BF16),
        compiler_params=pltpu.CompilerParams(
            dimension_semantics=("arbitrary", "arbitrary"), vmem_limit_bytes=VMEM_LIMIT),
        name="swa",
    )(sinks, bias, bias, qb, kv, kv, g2)


def _out_route_kernel(oa_ref, ob_ref, x_ref, wo_ref, g_ref, wrh_ref, wrl_ref, br_ref,
                      earlier_ref, h13_ref, hn3_ref, route_ref, cnt_ref, hbuf, hsem):
    i = pl.program_id(0)
    tm = TM_ROUTE

    @pl.when(i == 0)
    def _():
        cnt_ref[...] = jnp.zeros_like(cnt_ref)

    def stores(step, slot):
        rows = pl.ds(step * tm, tm)
        return [_SlabTileCopy(hbuf.at[slot, which], out.at[rows], hsem.at[slot], to_slab=True)
                for which, out in enumerate((h13_ref, hn3_ref))]

    slot = i % 2

    @pl.when(i >= 2)
    def _():
        for st in stores(i - 2, slot):
            st.wait()

    h1 = (x_ref[...]
          + jnp.dot(oa_ref[...], wo_ref[:oa_ref.shape[1], :], preferred_element_type=F32)
          + jnp.dot(ob_ref[...], wo_ref[oa_ref.shape[1]:, :], preferred_element_type=F32))
    hn = _rms(h1, g_ref[...])
    hbuf[slot, 0] = h1
    hbuf[slot, 1] = hn
    for st in stores(i, slot):
        st.start()

    @pl.when(i == pl.num_programs(0) - 1)
    def _():
        @pl.when(i >= 1)
        def _():
            for st in stores(i - 1, 1 - slot):
                st.wait()
        for st in stores(i, slot):
            st.wait()

    hn_hi = hn.astype(BF16)
    hn_lo = (hn - hn_hi.astype(F32)).astype(BF16)
    logits = (pl.dot(wrh_ref[...], hn_hi, trans_b=True)
              + pl.dot(wrh_ref[...], hn_lo, trans_b=True)
              + pl.dot(wrl_ref[...], hn_hi, trans_b=True)) + br_ref[...]

    eid = lax.broadcasted_iota(jnp.int32, (N_EXPERTS, tm), 0)
    work = logits
    vals, idxs, sels = [], [], []
    for _ in range(TOP_K):
        m = jnp.max(work, axis=0, keepdims=True)
        idx = jnp.min(jnp.where(work == m, eid, N_EXPERTS), axis=0, keepdims=True)
        sel = eid == idx
        vals.append(m)
        idxs.append(idx)
        sels.append(sel)
        work = jnp.where(sel, -3e38, work)
    exps = [jnp.exp(v - vals[0]) for v in vals]
    inv_den = 1.0 / (exps[0] + exps[1] + exps[2] + exps[3])

    multihot = jnp.where(sels[0] | sels[1] | sels[2] | sels[3], 1.0, 0.0)
    before = (jnp.dot(multihot.astype(BF16), earlier_ref[...], preferred_element_type=F32)
              + cnt_ref[:, 0:1])
    ranks = [jnp.sum(jnp.where(s, before, 0.0), axis=0, keepdims=True) for s in sels]
    route_ref[...] = jnp.concatenate(
        [i.astype(F32) for i in idxs] + [e * inv_den for e in exps] + ranks
        + [jnp.zeros((ROUTE_ROWS - 3 * TOP_K, tm), F32)], axis=0)
    cnt_ref[...] += jnp.sum(multihot, axis=1, keepdims=True)


def _out_route(oa, ob, x2, wo, g2, wr_hi, wr_lo, br):
    T = x2.shape[0]
    tm = TM_ROUTE
    row = lambda w: pl.BlockSpec((tm, w), lambda i: (i, 0))
    full = lambda a: pl.BlockSpec(a.shape, lambda i: (0,) * a.ndim)
    earlier = jnp.asarray(np.triu(np.ones((tm, tm), np.float32), k=1), BF16)
    return pl.pallas_call(
        _out_route_kernel,
        grid=(T // tm,),
        in_specs=[row(512), row(512), row(D_MODEL), full(wo), full(g2),
                  full(wr_hi), full(wr_lo), full(br), full(earlier)],
        out_specs=[pl.BlockSpec(memory_space=pl.ANY), pl.BlockSpec(memory_space=pl.ANY),
                   pl.BlockSpec((ROUTE_ROWS, tm), lambda i: (0, i)),
                   pl.BlockSpec((N_EXPERTS, LANES), lambda i: (0, 0))],
        out_shape=[jax.ShapeDtypeStruct((T, SLAB_SUBLANES, LANES), F32),
                   jax.ShapeDtypeStruct((T, SLAB_SUBLANES, LANES), F32),
                   jax.ShapeDtypeStruct((ROUTE_ROWS, T), F32),
                   jax.ShapeDtypeStruct((N_EXPERTS, LANES), F32)],
        scratch_shapes=[pltpu.VMEM((2, 2, tm, D_MODEL), F32), pltpu.SemaphoreType.DMA((2,))],
        compiler_params=pltpu.CompilerParams(
            dimension_semantics=("arbitrary",), vmem_limit_bytes=VMEM_LIMIT,
            has_side_effects=True),
        name="out_route",
    )(oa, ob, x2, wo, g2, wr_hi, wr_lo, br, earlier)


def _row_copy(src, s, dst, d, sem):
    return pltpu.make_async_copy(src.at[pl.ds(s, 1)], dst.at[pl.ds(d, 1)], sem)


def _dispatch_kernel(dest_ref, pad_start_ref, pad_cnt_ref, nused_ref, hn_ref, xs_ref,
                     zero_ref, sem, zsem):
    i = pl.program_id(0)
    n = pl.num_programs(0)
    tm = EXPERT_PAD
    n_tok = dest_ref.shape[0] // TOP_K
    base = i * TM_DISPATCH
    for p in range(TM_DISPATCH * TOP_K):
        r, k = divmod(p, TOP_K)
        _row_copy(hn_ref, r, xs_ref, dest_ref[base + k * n_tok + r], sem).start(priority=p % 2)
    for _ in range(TOP_K):
        pltpu.make_async_copy(hn_ref, xs_ref.at[pl.ds(0, TM_DISPATCH)], sem).wait()

    @pl.when(i == n - 1)
    def _():
        zero_ref[...] = jnp.zeros_like(zero_ref)
        sub = 8
        for e in range(N_EXPERTS):
            start, cnt = pad_start_ref[e], pad_cnt_ref[e]
            head = cnt & (sub - 1)
            body0 = start + head
            nbody = cnt // sub

            def head_copy(r):
                return _row_copy(zero_ref, 0, xs_ref, start + r, zsem)

            def body_copy(c):
                return pltpu.make_async_copy(zero_ref.at[pl.ds(0, sub)],
                                             xs_ref.at[pl.ds(body0 + c * sub, sub)], zsem)

            for copy, count in ((head_copy, head), (body_copy, nbody)):
                def zissue(r, carry, copy=copy):
                    copy(r).start()
                    return carry

                def zwait(r, carry, copy=copy):
                    copy(r).wait()
                    return carry

                lax.fori_loop(0, count, zissue, 0)
                lax.fori_loop(0, count, zwait, 0)

        def tail_copy(t):
            return pltpu.make_async_copy(zero_ref, xs_ref.at[pl.ds(t * tm, tm)], zsem)

        def tissue(t, carry):
            tail_copy(t).start()
            return carry

        def twait(t, carry):
            tail_copy(t).wait()
            return carry

        n_tiles = xs_ref.shape[0] // tm
        lax.fori_loop(nused_ref[0], n_tiles, tissue, 0)
        lax.fori_loop(nused_ref[0], n_tiles, twait, 0)


def _dispatch(dest, pad_start, pad_cnt, n_used, hn, n_rows):
    T = hn.shape[0]
    any_spec = pl.BlockSpec(memory_space=pl.ANY)
    return pl.pallas_call(
        _dispatch_kernel,
        grid_spec=pltpu.PrefetchScalarGridSpec(
            num_scalar_prefetch=4,
            grid=(T // TM_DISPATCH,),
            in_specs=[pl.BlockSpec((TM_DISPATCH, SLAB_SUBLANES, LANES), lambda i, *_: (i, 0, 0))],
            out_specs=any_spec,
            scratch_shapes=[pltpu.VMEM((EXPERT_PAD, SLAB_SUBLANES, LANES), F32),
                            pltpu.SemaphoreType.DMA(()),
                            pltpu.SemaphoreType.DMA(())]),
        out_shape=jax.ShapeDtypeStruct((n_rows, SLAB_SUBLANES, LANES), F32),
        compiler_params=pltpu.CompilerParams(
            dimension_semantics=("arbitrary",), has_side_effects=True,
            vmem_limit_bytes=VMEM_LIMIT),
        name="dispatch",
    )(dest, pad_start, pad_cnt, n_used, hn)


def _experts_kernel(row0_ref, nfull_ref, tail_ref, xs_ref, wg_ref, bg_ref, wl_ref, bl_ref, wd_ref,
                    bd_ref, ys_ref, wg_bf, wl_bf, wd_bf, xbuf, ybuf, xsem, ysem, wf32, wsem):
    e = pl.program_id(0)
    tm, pad = TM_EXPERT, EXPERT_PAD
    r0, nf, tail = row0_ref[e], nfull_ref[e], tail_ref[e]
    nt = nf + jnp.minimum(tail, 1)

    def for_short_tile(pieces, fn):
        for q in range(1, tm // pad):
            @pl.when(pieces == q)
            def _():
                fn(q * pad)

    def x_load(row, slot, rows):
        return _SlabTileCopy(xbuf.at[slot, pl.ds(0, rows)], xs_ref.at[pl.ds(row, rows)],
                             xsem.at[slot], to_slab=False)

    def y_store(row, slot, rows):
        return _SlabTileCopy(ybuf.at[slot, pl.ds(0, rows)], ys_ref.at[pl.ds(row, rows)],
                             ysem.at[slot], to_slab=True)

    def start_first_load(expert):
        @pl.when(nfull_ref[expert] > 0)
        def _():
            x_load(row0_ref[expert], 0, tm).start()

        @pl.when(nfull_ref[expert] == 0)
        def _():
            for_short_tile(tail_ref[expert], lambda rows: x_load(row0_ref[expert], 0, rows).start())

    def compute(slot, rows):
        x = xbuf[slot, :rows].astype(BF16)
        glu = jnp.minimum(jnp.dot(x, wg_bf[...], preferred_element_type=F32) + bg_ref[0],
                          SWIGLU_LIMIT)
        lin = jnp.clip(jnp.dot(x, wl_bf[...], preferred_element_type=F32) + bl_ref[0],
                       -SWIGLU_LIMIT, SWIGLU_LIMIT)
        hid = glu * jax.nn.sigmoid(SWIGLU_ALPHA * glu) * (lin + 1.0)
        ybuf[slot, :rows] = (jnp.dot(hid.astype(BF16), wd_bf[...], preferred_element_type=F32)
                             + bd_ref[0])

    def w_fetch(expert):
        return [pltpu.make_async_copy(w.at[expert], wf32.at[i], wsem.at[i])
                for i, w in enumerate((wg_ref, wl_ref, wd_ref))]

    @pl.when(e == 0)
    def _():
        start_first_load(0)
        for cp in w_fetch(0):
            cp.start()

    for cp, w_bf, i in zip(w_fetch(e), (wg_bf, wl_bf, wd_bf), range(3)):
        cp.wait()
        w_bf[...] = wf32[i].astype(BF16)

    has_next = e + 1 < pl.num_programs(0)

    def fetch_next(i):
        @pl.when(has_next)
        def _():
            w_fetch(e + 1)[i].start()

    fetch_next(0)

    def full_tile(j, carry):
        slot = j % 2

        @pl.when(j + 1 < nf)
        def _():
            x_load(r0 + (j + 1) * tm, 1 - slot, tm).start()

        @pl.when(j + 1 == nf)
        def _():
            for_short_tile(tail, lambda rows: x_load(r0 + (j + 1) * tm, 1 - slot, rows).start())

        x_load(r0 + j * tm, slot, tm).wait()

        @pl.when(j >= 2)
        def _():
            y_store(r0 + (j - 2) * tm, slot, tm).wait()

        compute(slot, tm)
        y_store(r0 + j * tm, slot, tm).start()
        for i in (1, 2):
            @pl.when(j == i - 1)
            def _():
                fetch_next(i)
        return carry

    lax.fori_loop(0, nf, full_tile, 0)
    for i in (1, 2):
        @pl.when(nf < i)
        def _():
            fetch_next(i)

    def short_tile(rows):
        slot = nf % 2
        x_load(r0 + nf * tm, slot, rows).wait()

        @pl.when(nf >= 2)
        def _():
            y_store(r0 + (nf - 2) * tm, slot, tm).wait()

        compute(slot, rows)
        y_store(r0 + nf * tm, slot, rows).start()

    for_short_tile(tail, short_tile)

    @pl.when(e + 1 < pl.num_programs(0))
    def _():
        start_first_load(e + 1)

    @pl.when(nt >= 2)
    def _():
        y_store(r0 + (nt - 2) * tm, nt % 2, tm).wait()

    @pl.when((nt >= 1) & (tail == 0))
    def _():
        y_store(r0 + (nt - 1) * tm, (nt - 1) % 2, tm).wait()

    for_short_tile(tail, lambda rows: y_store(r0 + nf * tm, nf % 2, rows).wait())

    @pl.when(e == pl.num_programs(0) - 1)
    def _():
        ybuf[0] = jnp.zeros((tm, D_MODEL), F32)
        first = (r0 + nf * tm + tail * pad) // pad
        n_pieces = ys_ref.shape[0] // pad

        def fill_issue(t, carry):
            y_store(t * pad, 0, pad).start()
            return carry

        def fill_wait(t, carry):
            y_store(t * pad, 0, pad).wait()
            return carry

        lax.fori_loop(first, n_pieces, fill_issue, 0)
        lax.fori_loop(first, n_pieces, fill_wait, 0)


def _experts(row0, nfull, tail, xs, w_glu, b_glu, w_lin, b_lin, w_down, b_down):
    tm = TM_EXPERT
    any_spec = pl.BlockSpec(memory_space=pl.ANY)
    bspec = lambda: pl.BlockSpec((1, 1, D_FF), lambda e, *_: (e, 0, 0))
    return pl.pallas_call(
        _experts_kernel,
        grid_spec=pltpu.PrefetchScalarGridSpec(
            num_scalar_prefetch=3,
            grid=(N_EXPERTS,),
            in_specs=[any_spec, any_spec, bspec(), any_spec, bspec(), any_spec, bspec()],
            out_specs=any_spec,
            scratch_shapes=[pltpu.VMEM((D_MODEL, D_FF), BF16)] * 3
            + [pltpu.VMEM((2, tm, D_MODEL), F32), pltpu.VMEM((2, tm, D_MODEL), F32),
               pltpu.SemaphoreType.DMA((2,)), pltpu.SemaphoreType.DMA((2,)),
               pltpu.VMEM((3, D_MODEL, D_FF), F32), pltpu.SemaphoreType.DMA((3,))]),
        out_shape=jax.ShapeDtypeStruct(xs.shape, F32),
        compiler_params=pltpu.CompilerParams(
            dimension_semantics=("arbitrary",), vmem_limit_bytes=VMEM_LIMIT,
            has_side_effects=True),
        name="experts",
    )(row0, nfull, tail, xs, w_glu, b_glu.reshape(N_EXPERTS, 1, D_FF),
      w_lin, b_lin.reshape(N_EXPERTS, 1, D_FF), w_down, b_down.reshape(N_EXPERTS, 1, D_MODEL))


def _combine_kernel(dest_ref, gate_ref, ys_ref, h1_ref, g_ref, out_ref, buf, obuf, sems, osems):
    s = pl.program_id(0)
    tm = TM_COMBINE
    n = out_ref.shape[0] // tm
    n_tok = dest_ref.shape[0] // TOP_K

    def out_store(tile, slot):
        return _SlabTileCopy(out_ref.at[pl.ds(tile * tm, tm)], obuf.at[slot], osems.at[slot],
                             to_slab=False)

    def issue_row(slot, r):
        for k in range(TOP_K):
            _row_copy(ys_ref, dest_ref[s * tm + k * n_tok + r], buf.at[slot, k], r,
                      sems.at[slot]).start(priority=k % 2)

    def reduce_row(slot, r):
        acc = h1_ref[r]
        for k in range(TOP_K):
            acc = acc + gate_ref[(s - 1) * tm + k * n_tok + r] * buf[slot, k, r]
        ss = jnp.sum(jnp.sum(acc * acc, axis=1, keepdims=True), axis=0, keepdims=True)
        obuf[slot, r] = acc * lax.rsqrt(ss * (1.0 / D_MODEL) + NORM_EPS) * g_ref[0]

    def step_body(gather_slot, reduce_slot):
        for r in range(tm):
            if gather_slot is not None:
                issue_row(gather_slot, r)
            if reduce_slot is not None:
                reduce_row(reduce_slot, r)
        if reduce_slot is not None:
            out_store(s - 1, reduce_slot).start()

    for parity in range(2):
        other = 1 - parity

        @pl.when(s % 2 == parity)
        def _():
            @pl.when(s < n)
            def _():
                step_body(parity, None)

            @pl.when(s >= 3)
            def _():
                out_store(s - 3, other).wait()

            @pl.when(s > 0)
            def _():
                for k in range(TOP_K):
                    pltpu.make_async_copy(ys_ref.at[pl.ds(0, tm)], buf.at[other, k],
                                          sems.at[other]).wait()
                step_body(None, other)

            if n % 2 == parity:
                @pl.when(s == n)
                def _():
                    if n >= 2:
                        out_store(n - 2, parity).wait()
                    out_store(n - 1, other).wait()


def _combine(dest, ys3, h13, gate, fg):
    T = h13.shape[0]
    tm = TM_COMBINE
    slab = (SLAB_SUBLANES, LANES)
    prev_tile = lambda s, *_: (jnp.maximum(s - 1, 0), 0, 0)
    return pl.pallas_call(
        _combine_kernel,
        grid_spec=pltpu.PrefetchScalarGridSpec(
            num_scalar_prefetch=2,
            grid=(T // tm + 1,),
            in_specs=[pl.BlockSpec(memory_space=pl.ANY),
                      pl.BlockSpec((tm,) + slab, prev_tile),
                      pl.BlockSpec((1,) + slab, lambda s, *_: (0, 0, 0))],
            out_specs=pl.BlockSpec(memory_space=pl.ANY),
            scratch_shapes=[pltpu.VMEM((2, TOP_K, tm) + slab, F32),
                            pltpu.VMEM((2, tm) + slab, F32),
                            pltpu.SemaphoreType.DMA((2,)),
                            pltpu.SemaphoreType.DMA((2,))]),
        out_shape=jax.ShapeDtypeStruct((T, D_MODEL), F32),
        compiler_params=pltpu.CompilerParams(
            dimension_semantics=("arbitrary",), vmem_limit_bytes=VMEM_LIMIT,
            has_side_effects=True),
        name="combine",
    )(dest, gate, ys3, h13, fg.reshape((1,) + slab))


def _swap_halves(w):
    h = w.shape[-1] // 2
    return jnp.concatenate([w[..., h:], w[..., :h]], axis=-1)


def _layer(x2, batch, seq, norm1_g, w_in, w_alpha_up, b_alpha, gla_norm_g, swa_sinks, swa_norm_g,
           w_out, norm2_g, w_router, b_router, w_glu, b_glu, w_lin, b_lin, w_down, b_down):
    T = x2.shape[0]
    kb_w, vb_w = w_in[:, 2064:2192], w_in[:, 2192:2320]
    w_z = w_in[:, 1536:1552]
    w_cat = jnp.concatenate([
        w_in[:, 0:1536], w_in[:, 1552:2064],
        kb_w, _swap_halves(kb_w), vb_w, _swap_halves(vb_w),
        jnp.pad(jnp.tile(w_z, (1, Z_PIECES)), [(0, 0), (0, LANES - Z_PIECES * GLA_RANK)]),
    ], axis=1).astype(BF16)
    wup_hi = w_alpha_up.astype(BF16)
    wup_lo = (w_alpha_up - wup_hi.astype(F32)).astype(BF16)
    wup_cat = jnp.pad(jnp.concatenate([wup_hi, wup_hi, wup_hi, wup_lo, wup_lo], axis=0),
                      [(0, LANES - Z_PIECES * GLA_RANK), (0, 0)])

    qk, vr, la, qb, kv = _in_proj(
        x2, norm1_g.reshape(1, -1), w_cat, wup_cat, b_alpha.reshape(1, -1))
    oa = _gla(qk, vr, la, gla_norm_g.reshape(1, -1), batch, seq)
    ob = _swa(qb, kv, swa_sinks, jnp.tile(swa_norm_g, 2).reshape(1, -1), batch, seq)

    wo = w_out.astype(BF16)
    wr_t = w_router.T
    wr_hi = wr_t.astype(BF16)
    wr_lo = (wr_t - wr_hi.astype(F32)).astype(BF16)
    h1, hn, route, cnt = _out_route(oa, ob, x2, wo, norm2_g.reshape(1, -1),
                                    wr_hi, wr_lo, b_router.reshape(-1, 1))

    tm, pad = TM_EXPERT, EXPERT_PAD
    n_rows = T * TOP_K + N_EXPERTS * pad
    counts = cnt[:, 0].astype(jnp.int32)
    padded = (counts + pad - 1) // pad * pad
    pends = jnp.cumsum(padded)
    pstarts = pends - padded
    top_idx = route[0:TOP_K].astype(jnp.int32)
    gate = route[TOP_K:2 * TOP_K].reshape(-1)
    rank = route[2 * TOP_K:3 * TOP_K].astype(jnp.int32)
    experts = jnp.arange(N_EXPERTS)[:, None, None]
    seg_start = jnp.sum(jnp.where(top_idx[None] == experts, pstarts[:, None, None], 0), axis=0)
    dest = (seg_start + rank).reshape(-1)
    n_used = (pends[-1] // pad).reshape(1)
    xs = _dispatch(dest, pstarts + counts, padded - counts, n_used, hn, n_rows)
    ys = _experts(pstarts, padded // tm, padded % tm // pad, xs,
                  w_glu, b_glu, w_lin, b_lin, w_down, b_down)
    return dest, ys, h1, gate


def kernel(x, norm1_g, w_in, w_alpha_up, b_alpha, gla_norm_g, swa_sinks, swa_norm_g, w_out,
           norm2_g, w_router, b_router, w_glu, b_glu, w_lin, b_lin, w_down, b_down, final_g):
    batch, seq, d = x.shape
    assert norm1_g.shape[0] == 1, "single-layer problem"
    x2 = x.reshape(batch * seq, d)
    dest, ys, h1, gate = _layer(
        x2, batch, seq, norm1_g[0], w_in[0], w_alpha_up[0], b_alpha[0], gla_norm_g[0],
        swa_sinks[0], swa_norm_g[0], w_out[0], norm2_g[0], w_router[0], b_router[0],
        w_glu[0], b_glu[0], w_lin[0], b_lin[0], w_down[0], b_down[0])
    out = _combine(dest, ys, h1, gate, final_g.reshape(1, -1))
    return out.reshape(batch, seq, d)
```

```python
import numpy as np
import jax
import jax.numpy as jnp
from jax import lax
from jax.experimental import pallas as pl
from jax.experimental.pallas import tpu as pltpu

F32 = jnp.float32
BF16 = jnp.bfloat16

D_MODEL = 1024
GLA_HEADS = 4
GLA_DK = 64
GLA_DV = 128
GLA_RANK = 16
GLA_GATE_TAU = 16.0
GLA_CHUNK = 64
SWA_HEADS = 8
SWA_KV_HEADS = 2
SWA_HEAD_DIM = 64
SWA_WINDOW = 128
N_EXPERTS = 32
TOP_K = 4
D_FF = 1024
SWIGLU_LIMIT = 7.0
SWIGLU_ALPHA = 1.702
NORM_EPS = 1e-5

LANES = 128
SLAB_SUBLANES = D_MODEL // LANES
VMEM_LIMIT = 56 * 1024 * 1024

TM_PROJ = 1024
TL_GLA = 1024
GLA_CUM_ROWS = 256
TM_ROUTE = 1024
TM_EXPERT = 512
EXPERT_PAD = 128
TM_COMBINE = 256
SWA_SUB = 8
TM_DISPATCH = 1024

NEG_BIG = -1e30
ROUTE_ROWS = 16

C_QA, C_KA, C_VA, C_RA, C_QB, C_KB, C_KBS, C_VB, C_VBS, C_Z, C_END = (
    0, 256, 512, 1024, 1536, 2048, 2176, 2304, 2432, 2560, 2688)
Z_PIECES = 5


def _split3(x):
    hi = x.astype(BF16).astype(F32)
    r = x - hi
    mid = r.astype(BF16).astype(F32)
    lo = (r - mid).astype(BF16).astype(F32)
    return hi, mid, lo


class _SlabTileCopy:
    def __init__(self, flat, slab, sem, to_slab):
        self.copies = []
        for c in range(SLAB_SUBLANES):
            pair = (flat.at[:, pl.ds(c * LANES, LANES)], slab.at[:, c, :])
            src, dst = pair if to_slab else pair[::-1]
            self.copies.append(pltpu.make_async_copy(src, dst, sem))

    def start(self):
        for cp in self.copies:
            cp.start()

    def wait(self):
        for cp in self.copies:
            cp.wait()


def _rms(x, g):
    return x * lax.rsqrt(jnp.mean(x * x, axis=-1, keepdims=True) + NORM_EPS) * g


def _in_proj_kernel(x_ref, g_ref, w_ref, wup_ref, ba_ref,
                    qk_ref, vr_ref, la_ref, qb_ref, kv_ref):
    u = _rms(x_ref[...], g_ref[...]).astype(BF16)

    def proj(c0, c1):
        return jnp.dot(u, w_ref[:, c0:c1], preferred_element_type=F32)

    qk_ref[...] = proj(C_QA, C_VA).astype(BF16)
    vr_ref[:, :C_RA - C_VA] = proj(C_VA, C_RA).astype(BF16)
    vr_ref[:, C_RA - C_VA:] = proj(C_RA, C_QB).astype(BF16)
    qb_ref[...] = proj(C_QB, C_KB).astype(BF16)
    kv_ref[...] = proj(C_KB, C_Z).astype(BF16)
    z = proj(C_Z, C_END)
    hi, mid, lo = _split3(z)
    piece = lax.broadcasted_iota(jnp.int32, z.shape, 1) // GLA_RANK
    zc = jnp.where((piece == 0) | (piece == 3), hi, jnp.where(piece == 2, lo, mid)).astype(BF16)
    y = jnp.dot(zc, wup_ref[...], preferred_element_type=F32) + ba_ref[...]
    log_sig = jnp.minimum(y, 0.0) - jnp.log1p(jnp.exp(-jnp.abs(y)))
    la_ref[...] = log_sig * (1.0 / GLA_GATE_TAU)


def _in_proj(x2, g1, w_cat, wup_p, ba_p):
    T = x2.shape[0]
    tm = TM_PROJ
    row = lambda w: pl.BlockSpec((tm, w), lambda i: (i, 0))
    full = lambda a: pl.BlockSpec(a.shape, lambda i: (0,) * a.ndim)
    outs = [(512, BF16), (1024, BF16), (256, F32), (512, BF16), (512, BF16)]
    return pl.pallas_call(
        _in_proj_kernel,
        grid=(T // tm,),
        in_specs=[row(D_MODEL), full(g1), full(w_cat), full(wup_p), full(ba_p)],
        out_specs=[row(w) for w, _ in outs],
        out_shape=[jax.ShapeDtypeStruct((T, w), dt) for w, dt in outs],
        compiler_params=pltpu.CompilerParams(
            dimension_semantics=("arbitrary",), vmem_limit_bytes=VMEM_LIMIT),
        name="in_proj",
    )(x2, g1, w_cat, wup_p, ba_p)


def _gla_kernel(qk_ref, vr_ref, la_ref, g_ref, cum_ref, o_ref, st_ref):
    @pl.when(pl.program_id(1) == 0)
    def _():
        st_ref[...] = jnp.zeros_like(st_ref)

    tl = TL_GLA
    c = GLA_CHUNK
    kw = GLA_HEADS * GLA_DK
    vw = GLA_HEADS * GLA_DV
    causal = (lax.broadcasted_iota(jnp.int32, (c, c), 0)
              >= lax.broadcasted_iota(jnp.int32, (c, c), 1))
    low_half = lax.broadcasted_iota(jnp.int32, (c, LANES), 1) < GLA_DK
    g = g_ref[...]
    b_groups = []
    for grp in range(tl // GLA_CUM_ROWS):
        la = la_ref[grp * GLA_CUM_ROWS:(grp + 1) * GLA_CUM_ROWS, :]
        pieces = jnp.concatenate([p.astype(BF16) for p in _split3(la)], axis=1)
        b3 = jnp.dot(cum_ref[...], pieces, preferred_element_type=F32)
        b_groups.append(b3[:, :kw] + b3[:, kw:2 * kw] + b3[:, 2 * kw:])
    for ch in range(tl // c):
        rows = slice(ch * c, (ch + 1) * c)
        in_grp = (ch * c) % GLA_CUM_ROWS
        b = b_groups[(ch * c) // GLA_CUM_ROWS][in_grp:in_grp + c]
        b_last = b[c - 1:c]
        qf = qk_ref[rows, :kw].astype(F32)
        kf = qk_ref[rows, kw:].astype(F32)
        q_e = (qf * jnp.exp(b) * (GLA_DK ** -0.5)).astype(BF16)
        k_e = (kf * jnp.exp(-b)).astype(BF16)
        k_t = (kf * jnp.exp(b_last - b)).astype(BF16)
        decay = jnp.exp(b_last)
        for h in range(GLA_HEADS):
            ps = slice((h // 2) * LANES, (h // 2 + 1) * LANES)
            mine = low_half if h % 2 == 0 else ~low_half
            qp, kp = q_e[:, ps], k_e[:, ps]
            qh = jnp.where(mine, qp, jnp.zeros_like(qp))
            kth = jnp.where(mine, k_t[:, ps], jnp.zeros_like(qp))
            vs = slice(h * GLA_DV, (h + 1) * GLA_DV)
            vh = vr_ref[rows, vs]
            a = pl.dot(qh, kp, trans_b=True)
            a = jnp.where(causal, a, 0.0).astype(BF16)
            st = st_ref[h]
            o = (jnp.dot(a, vh, preferred_element_type=F32)
                 + pl.dot(qh, st.astype(BF16), trans_b=True))
            st_ref[h] = st * decay[:, ps] + pl.dot(vh, kth, trans_a=True)
            rh = vr_ref[rows, slice(vw + h * GLA_DV, vw + (h + 1) * GLA_DV)].astype(F32)
            o = _rms(o, g) * (rh * jax.nn.sigmoid(rh))
            o_ref[rows, vs] = o.astype(BF16)


def _gla(qk, vr, la, g, batch, seq):
    tl = TL_GLA
    nl = seq // tl
    row = lambda w: pl.BlockSpec((tl, w), lambda b, i: (b * nl + i, 0))
    r = np.arange(GLA_CUM_ROWS)
    cum = jnp.asarray((r[None, :] <= r[:, None])
                      & (r[None, :] // GLA_CHUNK == r[:, None] // GLA_CHUNK), BF16)
    return pl.pallas_call(
        _gla_kernel,
        grid=(batch, nl),
        in_specs=[row(512), row(1024), row(256),
                  pl.BlockSpec((1, GLA_DV), lambda b, i: (0, 0)),
                  pl.BlockSpec((GLA_CUM_ROWS, GLA_CUM_ROWS), lambda b, i: (0, 0))],
        out_specs=row(512),
        out_shape=jax.ShapeDtypeStruct((batch * seq, 512), BF16),
        scratch_shapes=[pltpu.VMEM((GLA_HEADS, GLA_DV, LANES), F32)],
        compiler_params=pltpu.CompilerParams(
            dimension_semantics=("arbitrary", "arbitrary"), vmem_limit_bytes=VMEM_LIMIT),
        name="gla",
    )(qk, vr, la, g, cum)


def _swa_kernel(sink_ref, bias0_ref, bias_ref, q_ref, kvc_ref, kvp_ref, g_ref, o_ref):
    w = SWA_WINDOW
    scale = jnp.asarray(SWA_HEAD_DIM ** -0.5, BF16)
    kv_lane = lax.broadcasted_iota(jnp.int32, (2 * w, LANES), 1)
    lane_lo = kv_lane < SWA_HEAD_DIM
    out_lo = lax.broadcasted_iota(jnp.int32, (w, LANES), 1) < SWA_HEAD_DIM
    ones_hi = jnp.where(kv_lane == SWA_HEAD_DIM, 1.0, 0.0).astype(BF16)
    ones_lo = jnp.where(kv_lane == 0, 1.0, 0.0).astype(BF16)
    g = g_ref[...]

    for sb in range(SWA_SUB):
        rows = slice(sb * w, (sb + 1) * w)
        if sb == 0:
            kv = jnp.concatenate([kvp_ref[...], kvc_ref[0:w, :]], axis=0)
            bias = bias0_ref
        else:
            kv = kvc_ref[(sb - 1) * w:(sb + 1) * w, :]
            bias = bias_ref
        k = kv[:, 0:LANES] * scale
        ks = kv[:, LANES:2 * LANES] * scale
        v, vs = kv[:, 2 * LANES:3 * LANES], kv[:, 3 * LANES:]
        zero = jnp.zeros_like(k)
        k_low = [jnp.where(lane_lo, k, zero), jnp.where(lane_lo, ks, zero)]
        k_high = [jnp.where(lane_lo, zero, ks), jnp.where(lane_lo, zero, k)]
        v_low = [jnp.where(lane_lo, v, zero) + ones_hi, jnp.where(lane_lo, vs, zero) + ones_hi]
        v_high = [jnp.where(lane_lo, zero, vs) + ones_lo, jnp.where(lane_lo, zero, v) + ones_lo]

        def weights(s, head):
            s = s + bias[0, head]
            sink = sink_ref[head]
            m = jnp.maximum(jnp.max(s, axis=-1, keepdims=True), sink)
            return jnp.exp(s - m).astype(BF16), jnp.exp(sink - m)

        for pair in range(SWA_HEADS // 2):
            j = (2 * pair) // (SWA_HEADS // SWA_KV_HEADS)
            cols = slice(pair * LANES, (pair + 1) * LANES)
            qp = q_ref[rows, cols]
            e0, sink0 = weights(pl.dot(qp, k_low[j], trans_b=True), 2 * pair)
            e1, sink1 = weights(pl.dot(qp, k_high[j], trans_b=True), 2 * pair + 1)
            oa = jnp.dot(e0, v_low[j], preferred_element_type=F32)
            ob = jnp.dot(e1, v_high[j], preferred_element_type=F32)
            inv0 = 1.0 / (oa[:, SWA_HEAD_DIM:SWA_HEAD_DIM + 1] + sink0)
            inv1 = 1.0 / (ob[:, 0:1] + sink1)
            o = jnp.where(out_lo, oa * inv0, ob * inv1)
            sq = o * o
            ms_lo = jnp.sum(jnp.where(out_lo, sq, 0.0), axis=-1, keepdims=True)
            ms_hi = jnp.sum(jnp.where(out_lo, 0.0, sq), axis=-1, keepdims=True)
            ms = jnp.where(out_lo, ms_lo, ms_hi) * (1.0 / SWA_HEAD_DIM)
            o = o * lax.rsqrt(ms + NORM_EPS) * g
            o_ref[rows, cols] = o.astype(BF16)


def _swa_bias():
    w = SWA_WINDOW
    slopes = 2.0 ** (-8.0 * np.arange(1, SWA_HEADS + 1, dtype=np.float64) / SWA_HEADS)
    rel = np.arange(w)[:, None] + w - np.arange(2 * w)[None, :]
    in_window = (rel >= 0) & (rel < w)
    exists = np.stack([np.arange(2 * w) >= w, np.ones(2 * w, bool)])
    valid = in_window[None] & exists[:, None, :]
    bias = -slopes[None, :, None, None] * rel[None, None].astype(np.float64)
    return np.where(valid[:, None], bias, NEG_BIG).astype(np.float32)


def _swa(qb, kv, sinks, g2, batch, seq):
    w = SWA_WINDOW
    rows = SWA_SUB * w
    nb = seq // rows
    cur = lambda: pl.BlockSpec((rows, 512), lambda b, n: (b * nb + n, 0))
    prev = pl.BlockSpec(
        (w, 512), lambda b, n: (jnp.maximum((b * nb + n) * SWA_SUB - 1, b * nb * SWA_SUB), 0))
    table = lambda index: pl.BlockSpec((1, SWA_HEADS, w, 2 * w), index)
    bias = jnp.asarray(_swa_bias())
    return pl.pallas_call(
        _swa_kernel,
        grid=(batch, nb),
        in_specs=[pl.BlockSpec(memory_space=pltpu.SMEM),
                  table(lambda b, n: (jnp.minimum(n, 1), 0, 0, 0)),
                  table(lambda b, n: (1, 0, 0, 0)),
                  cur(), cur(), prev,
                  pl.BlockSpec((1, LANES), lambda b, n: (0, 0))],
        out_specs=cur(),
        out_shape=jax.ShapeDtypeStruct((batch * seq, 512), BF16),
        compiler_params=pltpu.CompilerParams(
            dimension_semantics=("arbitrary", "arbitrary"), vmem_limit_bytes=VMEM_LIMIT),
        name="swa",
    )(sinks, bias, bias, qb, kv, kv, g2)


def _out_route_kernel(oa_ref, ob_ref, x_ref, wo_ref, g_ref, wrh_ref, wrl_ref, br_ref,
                      earlier_ref, h13_ref, hn3_ref, route_ref, cnt_ref, hbuf, hsem):
    i = pl.program_id(0)
    tm = TM_ROUTE

    @pl.when(i == 0)
    def _():
        cnt_ref[...] = jnp.zeros_like(cnt_ref)

    def stores(step, slot):
        rows = pl.ds(step * tm, tm)
        return [_SlabTileCopy(hbuf.at[slot, which], out.at[rows], hsem.at[slot], to_slab=True)
                for which, out in enumerate((h13_ref, hn3_ref))]

    slot = i % 2

    @pl.when(i >= 2)
    def _():
        for st in stores(i - 2, slot):
            st.wait()

    h1 = (x_ref[...]
          + jnp.dot(oa_ref[...], wo_ref[:oa_ref.shape[1], :], preferred_element_type=F32)
          + jnp.dot(ob_ref[...], wo_ref[oa_ref.shape[1]:, :], preferred_element_type=F32))
    hn = _rms(h1, g_ref[...])
    hbuf[slot, 0] = h1
    hbuf[slot, 1] = hn
    for st in stores(i, slot):
        st.start()

    @pl.when(i == pl.num_programs(0) - 1)
    def _():
        @pl.when(i >= 1)
        def _():
            for st in stores(i - 1, 1 - slot):
                st.wait()
        for st in stores(i, slot):
            st.wait()

    hn_hi = hn.astype(BF16)
    hn_lo = (hn - hn_hi.astype(F32)).astype(BF16)
    logits = (pl.dot(wrh_ref[...], hn_hi, trans_b=True)
              + pl.dot(wrh_ref[...], hn_lo, trans_b=True)
              + pl.dot(wrl_ref[...], hn_hi, trans_b=True)) + br_ref[...]

    eid = lax.broadcasted_iota(jnp.int32, (N_EXPERTS, tm), 0)
    work = logits
    vals, idxs, sels = [], [], []
    for _ in range(TOP_K):
        m = jnp.max(work, axis=0, keepdims=True)
        idx = jnp.min(jnp.where(work == m, eid, N_EXPERTS), axis=0, keepdims=True)
        sel = eid == idx
        vals.append(m)
        idxs.append(idx)
        sels.append(sel)
        work = jnp.where(sel, -3e38, work)
    exps = [jnp.exp(v - vals[0]) for v in vals]
    inv_den = 1.0 / (exps[0] + exps[1] + exps[2] + exps[3])

    multihot = jnp.where(sels[0] | sels[1] | sels[2] | sels[3], 1.0, 0.0)
    before = (jnp.dot(multihot.astype(BF16), earlier_ref[...], preferred_element_type=F32)
              + cnt_ref[:, 0:1])
    ranks = [jnp.sum(jnp.where(s, before, 0.0), axis=0, keepdims=True) for s in sels]
    route_ref[...] = jnp.concatenate(
        [i.astype(F32) for i in idxs] + [e * inv_den for e in exps] + ranks
        + [jnp.zeros((ROUTE_ROWS - 3 * TOP_K, tm), F32)], axis=0)
    cnt_ref[...] += jnp.sum(multihot, axis=1, keepdims=True)


def _out_route(oa, ob, x2, wo, g2, wr_hi, wr_lo, br):
    T = x2.shape[0]
    tm = TM_ROUTE
    row = lambda w: pl.BlockSpec((tm, w), lambda i: (i, 0))
    full = lambda a: pl.BlockSpec(a.shape, lambda i: (0,) * a.ndim)
    earlier = jnp.asarray(np.triu(np.ones((tm, tm), np.float32), k=1), BF16)
    return pl.pallas_call(
        _out_route_kernel,
        grid=(T // tm,),
        in_specs=[row(512), row(512), row(D_MODEL), full(wo), full(g2),
                  full(wr_hi), full(wr_lo), full(br), full(earlier)],
        out_specs=[pl.BlockSpec(memory_space=pl.ANY), pl.BlockSpec(memory_space=pl.ANY),
                   pl.BlockSpec((ROUTE_ROWS, tm), lambda i: (0, i)),
                   pl.BlockSpec((N_EXPERTS, LANES), lambda i: (0, 0))],
        out_shape=[jax.ShapeDtypeStruct((T, SLAB_SUBLANES, LANES), F32),
                   jax.ShapeDtypeStruct((T, SLAB_SUBLANES, LANES), F32),
                   jax.ShapeDtypeStruct((ROUTE_ROWS, T), F32),
                   jax.ShapeDtypeStruct((N_EXPERTS, LANES), F32)],
        scratch_shapes=[pltpu.VMEM((2, 2, tm, D_MODEL), F32), pltpu.SemaphoreType.DMA((2,))],
        compiler_params=pltpu.CompilerParams(
            dimension_semantics=("arbitrary",), vmem_limit_bytes=VMEM_LIMIT,
            has_side_effects=True),
        name="out_route",
    )(oa, ob, x2, wo, g2, wr_hi, wr_lo, br, earlier)


def _row_copy(src, s, dst, d, sem):
    return pltpu.make_async_copy(src.at[pl.ds(s, 1)], dst.at[pl.ds(d, 1)], sem)


def _dispatch_kernel(dest_ref, pad_start_ref, pad_cnt_ref, nused_ref, hn_ref, xs_ref,
                     zero_ref, sem, zsem):
    i = pl.program_id(0)
    n = pl.num_programs(0)
    tm = EXPERT_PAD
    n_tok = dest_ref.shape[0] // TOP_K
    base = i * TM_DISPATCH
    for p in range(TM_DISPATCH * TOP_K):
        r, k = divmod(p, TOP_K)
        _row_copy(hn_ref, r, xs_ref, dest_ref[base + k * n_tok + r], sem).start(priority=p % 2)
    for _ in range(TOP_K):
        pltpu.make_async_copy(hn_ref, xs_ref.at[pl.ds(0, TM_DISPATCH)], sem).wait()

    @pl.when(i == n - 1)
    def _():
        zero_ref[...] = jnp.zeros_like(zero_ref)
        sub = 8
        for e in range(N_EXPERTS):
            start, cnt = pad_start_ref[e], pad_cnt_ref[e]
            head = cnt & (sub - 1)
            body0 = start + head
            nbody = cnt // sub

            def head_copy(r):
                return _row_copy(zero_ref, 0, xs_ref, start + r, zsem)

            def body_copy(c):
                return pltpu.make_async_copy(zero_ref.at[pl.ds(0, sub)],
                                             xs_ref.at[pl.ds(body0 + c * sub, sub)], zsem)

            for copy, count in ((head_copy, head), (body_copy, nbody)):
                def zissue(r, carry, copy=copy):
                    copy(r).start()
                    return carry

                def zwait(r, carry, copy=copy):
                    copy(r).wait()
                    return carry

                lax.fori_loop(0, count, zissue, 0)
                lax.fori_loop(0, count, zwait, 0)

        def tail_copy(t):
            return pltpu.make_async_copy(zero_ref, xs_ref.at[pl.ds(t * tm, tm)], zsem)

        def tissue(t, carry):
            tail_copy(t).start()
            return carry

        def twait(t, carry):
            tail_copy(t).wait()
            return carry

        n_tiles = xs_ref.shape[0] // tm
        lax.fori_loop(nused_ref[0], n_tiles, tissue, 0)
        lax.fori_loop(nused_ref[0], n_tiles, twait, 0)


def _dispatch(dest, pad_start, pad_cnt, n_used, hn, n_rows):
    T = hn.shape[0]
    any_spec = pl.BlockSpec(memory_space=pl.ANY)
    return pl.pallas_call(
        _dispatch_kernel,
        grid_spec=pltpu.PrefetchScalarGridSpec(
            num_scalar_prefetch=4,
            grid=(T // TM_DISPATCH,),
            in_specs=[pl.BlockSpec((TM_DISPATCH, SLAB_SUBLANES, LANES), lambda i, *_: (i, 0, 0))],
            out_specs=any_spec,
            scratch_shapes=[pltpu.VMEM((EXPERT_PAD, SLAB_SUBLANES, LANES), F32),
                            pltpu.SemaphoreType.DMA(()),
                            pltpu.SemaphoreType.DMA(())]),
        out_shape=jax.ShapeDtypeStruct((n_rows, SLAB_SUBLANES, LANES), F32),
        compiler_params=pltpu.CompilerParams(
            dimension_semantics=("arbitrary",), has_side_effects=True,
            vmem_limit_bytes=VMEM_LIMIT),
        name="dispatch",
    )(dest, pad_start, pad_cnt, n_used, hn)


def _experts_kernel(row0_ref, nfull_ref, tail_ref, xs_ref, wg_ref, bg_ref, wl_ref, bl_ref, wd_ref,
                    bd_ref, ys_ref, wg_bf, wl_bf, wd_bf, xbuf, ybuf, xsem, ysem, wf32, wsem):
    e = pl.program_id(0)
    tm, pad = TM_EXPERT, EXPERT_PAD
    r0, nf, tail = row0_ref[e], nfull_ref[e], tail_ref[e]
    nt = nf + jnp.minimum(tail, 1)

    def for_short_tile(pieces, fn):
        for q in range(1, tm // pad):
            @pl.when(pieces == q)
            def _():
                fn(q * pad)

    def x_load(row, slot, rows):
        return _SlabTileCopy(xbuf.at[slot, pl.ds(0, rows)], xs_ref.at[pl.ds(row, rows)],
                             xsem.at[slot], to_slab=False)

    def y_store(row, slot, rows):
        return _SlabTileCopy(ybuf.at[slot, pl.ds(0, rows)], ys_ref.at[pl.ds(row, rows)],
                             ysem.at[slot], to_slab=True)

    def start_first_load(expert):
        @pl.when(nfull_ref[expert] > 0)
        def _():
            x_load(row0_ref[expert], 0, tm).start()

        @pl.when(nfull_ref[expert] == 0)
        def _():
            for_short_tile(tail_ref[expert], lambda rows: x_load(row0_ref[expert], 0, rows).start())

    def compute(slot, rows):
        x = xbuf[slot, :rows].astype(BF16)
        glu = jnp.minimum(jnp.dot(x, wg_bf[...], preferred_element_type=F32) + bg_ref[0],
                          SWIGLU_LIMIT)
        lin = jnp.clip(jnp.dot(x, wl_bf[...], preferred_element_type=F32) + bl_ref[0],
                       -SWIGLU_LIMIT, SWIGLU_LIMIT)
        hid = glu * jax.nn.sigmoid(SWIGLU_ALPHA * glu) * (lin + 1.0)
        ybuf[slot, :rows] = (jnp.dot(hid.astype(BF16), wd_bf[...], preferred_element_type=F32)
                             + bd_ref[0])

    def w_fetch(expert):
        return [pltpu.make_async_copy(w.at[expert], wf32.at[i], wsem.at[i])
                for i, w in enumerate((wg_ref, wl_ref, wd_ref))]

    @pl.when(e == 0)
    def _():
        start_first_load(0)
        for cp in w_fetch(0):
            cp.start()

    for cp, w_bf, i in zip(w_fetch(e), (wg_bf, wl_bf, wd_bf), range(3)):
        cp.wait()
        w_bf[...] = wf32[i].astype(BF16)

    has_next = e + 1 < pl.num_programs(0)

    def fetch_next(i):
        @pl.when(has_next)
        def _():
            w_fetch(e + 1)[i].start()

    fetch_next(0)

    def full_tile(j, carry):
        slot = j % 2

        @pl.when(j + 1 < nf)
        def _():
            x_load(r0 + (j + 1) * tm, 1 - slot, tm).start()

        @pl.when(j + 1 == nf)
        def _():
            for_short_tile(tail, lambda rows: x_load(r0 + (j + 1) * tm, 1 - slot, rows).start())

        x_load(r0 + j * tm, slot, tm).wait()

        @pl.when(j >= 2)
        def _():
            y_store(r0 + (j - 2) * tm, slot, tm).wait()

        compute(slot, tm)
        y_store(r0 + j * tm, slot, tm).start()
        for i in (1, 2):
            @pl.when(j == i - 1)
            def _():
                fetch_next(i)
        return carry

    lax.fori_loop(0, nf, full_tile, 0)
    for i in (1, 2):
        @pl.when(nf < i)
        def _():
            fetch_next(i)

    def short_tile(rows):
        slot = nf % 2
        x_load(r0 + nf * tm, slot, rows).wait()

        @pl.when(nf >= 2)
        def _():
            y_store(r0 + (nf - 2) * tm, slot, tm).wait()

        compute(slot, rows)
        y_store(r0 + nf * tm, slot, rows).start()

    for_short_tile(tail, short_tile)

    @pl.when(e + 1 < pl.num_programs(0))
    def _():
        start_first_load(e + 1)

    @pl.when(nt >= 2)
    def _():
        y_store(r0 + (nt - 2) * tm, nt % 2, tm).wait()

    @pl.when((nt >= 1) & (tail == 0))
    def _():
        y_store(r0 + (nt - 1) * tm, (nt - 1) % 2, tm).wait()

    for_short_tile(tail, lambda rows: y_store(r0 + nf * tm, nf % 2, rows).wait())

    @pl.when(e == pl.num_programs(0) - 1)
    def _():
        ybuf[0] = jnp.zeros((tm, D_MODEL), F32)
        first = (r0 + nf * tm + tail * pad) // pad
        n_pieces = ys_ref.shape[0] // pad

        def fill_issue(t, carry):
            y_store(t * pad, 0, pad).start()
            return carry

        def fill_wait(t, carry):
            y_store(t * pad, 0, pad).wait()
            return carry

        lax.fori_loop(first, n_pieces, fill_issue, 0)
        lax.fori_loop(first, n_pieces, fill_wait, 0)


def _experts(row0, nfull, tail, xs, w_glu, b_glu, w_lin, b_lin, w_down, b_down):
    tm = TM_EXPERT
    any_spec = pl.BlockSpec(memory_space=pl.ANY)
    bspec = lambda: pl.BlockSpec((1, 1, D_FF), lambda e, *_: (e, 0, 0))
    return pl.pallas_call(
        _experts_kernel,
        grid_spec=pltpu.PrefetchScalarGridSpec(
            num_scalar_prefetch=3,
            grid=(N_EXPERTS,),
            in_specs=[any_spec, any_spec, bspec(), any_spec, bspec(), any_spec, bspec()],
            out_specs=any_spec,
            scratch_shapes=[pltpu.VMEM((D_MODEL, D_FF), BF16)] * 3
            + [pltpu.VMEM((2, tm, D_MODEL), F32), pltpu.VMEM((2, tm, D_MODEL), F32),
               pltpu.SemaphoreType.DMA((2,)), pltpu.SemaphoreType.DMA((2,)),
               pltpu.VMEM((3, D_MODEL, D_FF), F32), pltpu.SemaphoreType.DMA((3,))]),
        out_shape=jax.ShapeDtypeStruct(xs.shape, F32),
        compiler_params=pltpu.CompilerParams(
            dimension_semantics=("arbitrary",), vmem_limit_bytes=VMEM_LIMIT,
            has_side_effects=True),
        name="experts",
    )(row0, nfull, tail, xs, w_glu, b_glu.reshape(N_EXPERTS, 1, D_FF),
      w_lin, b_lin.reshape(N_EXPERTS, 1, D_FF), w_down, b_down.reshape(N_EXPERTS, 1, D_MODEL))


def _combine_kernel(dest_ref, gate_ref, ys_ref, h1_ref, g_ref, out_ref, buf, obuf, sems, osems):
    s = pl.program_id(0)
    tm = TM_COMBINE
    n = out_ref.shape[0] // tm
    n_tok = dest_ref.shape[0] // TOP_K

    def out_store(tile, slot):
        return _SlabTileCopy(out_ref.at[pl.ds(tile * tm, tm)], obuf.at[slot], osems.at[slot],
                             to_slab=False)

    def issue_row(slot, r):
        for k in range(TOP_K):
            _row_copy(ys_ref, dest_ref[s * tm + k * n_tok + r], buf.at[slot, k], r,
                      sems.at[slot]).start(priority=k % 2)

    def reduce_row(slot, r):
        acc = h1_ref[r]
        for k in range(TOP_K):
            acc = acc + gate_ref[(s - 1) * tm + k * n_tok + r] * buf[slot, k, r]
        ss = jnp.sum(jnp.sum(acc * acc, axis=1, keepdims=True), axis=0, keepdims=True)
        obuf[slot, r] = acc * lax.rsqrt(ss * (1.0 / D_MODEL) + NORM_EPS) * g_ref[0]

    def step_body(gather_slot, reduce_slot):
        for r in range(tm):
            if gather_slot is not None:
                issue_row(gather_slot, r)
            if reduce_slot is not None:
                reduce_row(reduce_slot, r)
        if reduce_slot is not None:
            out_store(s - 1, reduce_slot).start()

    for parity in range(2):
        other = 1 - parity

        @pl.when(s % 2 == parity)
        def _():
            @pl.when(s < n)
            def _():
                step_body(parity, None)

            @pl.when(s >= 3)
            def _():
                out_store(s - 3, other).wait()

            @pl.when(s > 0)
            def _():
                for k in range(TOP_K):
                    pltpu.make_async_copy(ys_ref.at[pl.ds(0, tm)], buf.at[other, k],
                                          sems.at[other]).wait()
                step_body(None, other)

            if n % 2 == parity:
                @pl.when(s == n)
                def _():
                    if n >= 2:
                        out_store(n - 2, parity).wait()
                    out_store(n - 1, other).wait()


def _combine(dest, ys3, h13, gate, fg):
    T = h13.shape[0]
    tm = TM_COMBINE
    slab = (SLAB_SUBLANES, LANES)
    prev_tile = lambda s, *_: (jnp.maximum(s - 1, 0), 0, 0)
    return pl.pallas_call(
        _combine_kernel,
        grid_spec=pltpu.PrefetchScalarGridSpec(
            num_scalar_prefetch=2,
            grid=(T // tm + 1,),
            in_specs=[pl.BlockSpec(memory_space=pl.ANY),
                      pl.BlockSpec((tm,) + slab, prev_tile),
                      pl.BlockSpec((1,) + slab, lambda s, *_: (0, 0, 0))],
            out_specs=pl.BlockSpec(memory_space=pl.ANY),
            scratch_shapes=[pltpu.VMEM((2, TOP_K, tm) + slab, F32),
                            pltpu.VMEM((2, tm) + slab, F32),
                            pltpu.SemaphoreType.DMA((2,)),
                            pltpu.SemaphoreType.DMA((2,))]),
        out_shape=jax.ShapeDtypeStruct((T, D_MODEL), F32),
        compiler_params=pltpu.CompilerParams(
            dimension_semantics=("arbitrary",), vmem_limit_bytes=VMEM_LIMIT,
            has_side_effects=True),
        name="combine",
    )(dest, gate, ys3, h13, fg.reshape((1,) + slab))


def _swap_halves(w):
    h = w.shape[-1] // 2
    return jnp.concatenate([w[..., h:], w[..., :h]], axis=-1)


def _layer(x2, batch, seq, norm1_g, w_in, w_alpha_up, b_alpha, gla_norm_g, swa_sinks, swa_norm_g,
           w_out, norm2_g, w_router, b_router, w_glu, b_glu, w_lin, b_lin, w_down, b_down):
    T = x2.shape[0]
    kb_w, vb_w = w_in[:, 2064:2192], w_in[:, 2192:2320]
    w_z = w_in[:, 1536:1552]
    w_cat = jnp.concatenate([
        w_in[:, 0:1536], w_in[:, 1552:2064],
        kb_w, _swap_halves(kb_w), vb_w, _swap_halves(vb_w),
        jnp.pad(jnp.tile(w_z, (1, Z_PIECES)), [(0, 0), (0, LANES - Z_PIECES * GLA_RANK)]),
    ], axis=1).astype(BF16)
    wup_hi = w_alpha_up.astype(BF16)
    wup_lo = (w_alpha_up - wup_hi.astype(F32)).astype(BF16)
    wup_cat = jnp.pad(jnp.concatenate([wup_hi, wup_hi, wup_hi, wup_lo, wup_lo], axis=0),
                      [(0, LANES - Z_PIECES * GLA_RANK), (0, 0)])

    qk, vr, la, qb, kv = _in_proj(
        x2, norm1_g.reshape(1, -1), w_cat, wup_cat, b_alpha.reshape(1, -1))
    oa = _gla(qk, vr, la, gla_norm_g.reshape(1, -1), batch, seq)
    ob = _swa(qb, kv, swa_sinks, jnp.tile(swa_norm_g, 2).reshape(1, -1), batch, seq)

    wo = w_out.astype(BF16)
    wr_t = w_router.T
    wr_hi = wr_t.astype(BF16)
    wr_lo = (wr_t - wr_hi.astype(F32)).astype(BF16)
    h1, hn, route, cnt = _out_route(oa, ob, x2, wo, norm2_g.reshape(1, -1),
                                    wr_hi, wr_lo, b_router.reshape(-1, 1))

    tm, pad = TM_EXPERT, EXPERT_PAD
    n_rows = T * TOP_K + N_EXPERTS * pad
    counts = cnt[:, 0].astype(jnp.int32)
    padded = (counts + pad - 1) // pad * pad
    pends = jnp.cumsum(padded)
    pstarts = pends - padded
    top_idx = route[0:TOP_K].astype(jnp.int32)
    gate = route[TOP_K:2 * TOP_K].reshape(-1)
    rank = route[2 * TOP_K:3 * TOP_K].astype(jnp.int32)
    experts = jnp.arange(N_EXPERTS)[:, None, None]
    seg_start = jnp.sum(jnp.where(top_idx[None] == experts, pstarts[:, None, None], 0), axis=0)
    dest = (seg_start + rank).reshape(-1)
    n_used = (pends[-1] // pad).reshape(1)
    xs = _dispatch(dest, pstarts + counts, padded - counts, n_used, hn, n_rows)
    ys = _experts(pstarts, padded // tm, padded % tm // pad, xs,
                  w_glu, b_glu, w_lin, b_lin, w_down, b_down)
    return dest, ys, h1, gate


def kernel(x, norm1_g, w_in, w_alpha_up, b_alpha, gla_norm_g, swa_sinks, swa_norm_g, w_out,
           norm2_g, w_router, b_router, w_glu, b_glu, w_lin, b_lin, w_down, b_down, final_g):
    batch, seq, d = x.shape
    assert norm1_g.shape[0] == 1, "single-layer problem"
    x2 = x.reshape(batch * seq, d)
    dest, ys, h1, gate = _layer(
        x2, batch, seq, norm1_g[0], w_in[0], w_alpha_up[0], b_alpha[0], gla_norm_g[0],
        swa_sinks[0], swa_norm_g[0], w_out[0], norm2_g[0], w_router[0], b_router[0],
        w_glu[0], b_glu[0], w_lin[0], b_lin[0], w_down[0], b_down[0])
    out = _combine(dest, ys, h1, gate, final_g.reshape(1, -1))
    return out.reshape(batch, seq, d)
```

```python
import numpy as np
import jax
import jax.numpy as jnp
from jax import lax
from jax.experimental import pallas as pl
from jax.experimental.pallas import tpu as pltpu

F32 = jnp.float32
BF16 = jnp.bfloat16

D_MODEL = 1024
GLA_HEADS = 4
GLA_DK = 64
GLA_DV = 128
GLA_RANK = 16
GLA_GATE_TAU = 16.0
GLA_CHUNK = 64
SWA_HEADS = 8
SWA_KV_HEADS = 2
SWA_HEAD_DIM = 64
SWA_WINDOW = 128
N_EXPERTS = 32
TOP_K = 4
D_FF = 1024
SWIGLU_LIMIT = 7.0
SWIGLU_ALPHA = 1.702
NORM_EPS = 1e-5

LANES = 128
SLAB_SUBLANES = D_MODEL // LANES
VMEM_LIMIT = 56 * 1024 * 1024

TM_PROJ = 1024
TL_GLA = 1024
GLA_CUM_ROWS = 256
TM_ROUTE = 1024
TM_EXPERT = 512
EXPERT_PAD = 64
TM_COMBINE = 256
SWA_SUB = 8
TM_DISPATCH = 1024

NEG_BIG = -1e30
ROUTE_ROWS = 16

C_QA, C_KA, C_VA, C_RA, C_QB, C_KB, C_KBS, C_VB, C_VBS, C_Z, C_END = (
    0, 256, 512, 1024, 1536, 2048, 2176, 2304, 2432, 2560, 2688)
Z_PIECES = 5


def _split3(x):
    hi = x.astype(BF16).astype(F32)
    r = x - hi
    mid = r.astype(BF16).astype(F32)
    lo = (r - mid).astype(BF16).astype(F32)
    return hi, mid, lo


class _SlabTileCopy:
    def __init__(self, flat, slab, sem, to_slab):
        self.copies = []
        for c in range(SLAB_SUBLANES):
            pair = (flat.at[:, pl.ds(c * LANES, LANES)], slab.at[:, c, :])
            src, dst = pair if to_slab else pair[::-1]
            self.copies.append(pltpu.make_async_copy(src, dst, sem))

    def start(self):
        for cp in self.copies:
            cp.start()

    def wait(self):
        for cp in self.copies:
            cp.wait()


def _rms(x, g):
    return x * lax.rsqrt(jnp.mean(x * x, axis=-1, keepdims=True) + NORM_EPS) * g


def _in_proj_kernel(x_ref, g_ref, w_ref, wup_ref, ba_ref,
                    qk_ref, vr_ref, la_ref, qb_ref, kv_ref):
    u = _rms(x_ref[...], g_ref[...]).astype(BF16)

    def proj(c0, c1):
        return jnp.dot(u, w_ref[:, c0:c1], preferred_element_type=F32)

    qk_ref[...] = proj(C_QA, C_VA).astype(BF16)
    vr_ref[:, :C_RA - C_VA] = proj(C_VA, C_RA).astype(BF16)
    vr_ref[:, C_RA - C_VA:] = proj(C_RA, C_QB).astype(BF16)
    qb_ref[...] = proj(C_QB, C_KB).astype(BF16)
    kv_ref[...] = proj(C_KB, C_Z).astype(BF16)
    z = proj(C_Z, C_END)
    hi, mid, lo = _split3(z)
    piece = lax.broadcasted_iota(jnp.int32, z.shape, 1) // GLA_RANK
    zc = jnp.where((piece == 0) | (piece == 3), hi, jnp.where(piece == 2, lo, mid)).astype(BF16)
    y = jnp.dot(zc, wup_ref[...], preferred_element_type=F32) + ba_ref[...]
    log_sig = jnp.minimum(y, 0.0) - jnp.log1p(jnp.exp(-jnp.abs(y)))
    la_ref[...] = log_sig * (1.0 / GLA_GATE_TAU)


def _in_proj(x2, g1, w_cat, wup_p, ba_p):
    T = x2.shape[0]
    tm = TM_PROJ
    row = lambda w: pl.BlockSpec((tm, w), lambda i: (i, 0))
    full = lambda a: pl.BlockSpec(a.shape, lambda i: (0,) * a.ndim)
    outs = [(512, BF16), (1024, BF16), (256, F32), (512, BF16), (512, BF16)]
    return pl.pallas_call(
        _in_proj_kernel,
        grid=(T // tm,),
        in_specs=[row(D_MODEL), full(g1), full(w_cat), full(wup_p), full(ba_p)],
        out_specs=[row(w) for w, _ in outs],
        out_shape=[jax.ShapeDtypeStruct((T, w), dt) for w, dt in outs],
        compiler_params=pltpu.CompilerParams(
            dimension_semantics=("arbitrary",), vmem_limit_bytes=VMEM_LIMIT),
        name="in_proj",
    )(x2, g1, w_cat, wup_p, ba_p)


def _gla_kernel(qk_ref, vr_ref, la_ref, g_ref, cum_ref, o_ref, st_ref):
    @pl.when(pl.program_id(1) == 0)
    def _():
        st_ref[...] = jnp.zeros_like(st_ref)

    tl = TL_GLA
    c = GLA_CHUNK
    kw = GLA_HEADS * GLA_DK
    vw = GLA_HEADS * GLA_DV
    causal = (lax.broadcasted_iota(jnp.int32, (c, c), 0)
              >= lax.broadcasted_iota(jnp.int32, (c, c), 1))
    low_half = lax.broadcasted_iota(jnp.int32, (c, LANES), 1) < GLA_DK
    g = g_ref[...]
    b_groups = []
    for grp in range(tl // GLA_CUM_ROWS):
        la = la_ref[grp * GLA_CUM_ROWS:(grp + 1) * GLA_CUM_ROWS, :]
        pieces = jnp.concatenate([p.astype(BF16) for p in _split3(la)], axis=1)
        b3 = jnp.dot(cum_ref[...], pieces, preferred_element_type=F32)
        b_groups.append(b3[:, :kw] + b3[:, kw:2 * kw] + b3[:, 2 * kw:])
    for ch in range(tl // c):
        rows = slice(ch * c, (ch + 1) * c)
        in_grp = (ch * c) % GLA_CUM_ROWS
        b = b_groups[(ch * c) // GLA_CUM_ROWS][in_grp:in_grp + c]
        b_last = b[c - 1:c]
        qf = qk_ref[rows, :kw].astype(F32)
        kf = qk_ref[rows, kw:].astype(F32)
        q_e = (qf * jnp.exp(b) * (GLA_DK ** -0.5)).astype(BF16)
        k_e = (kf * jnp.exp(-b)).astype(BF16)
        k_t = (kf * jnp.exp(b_last - b)).astype(BF16)
        decay = jnp.exp(b_last)
        for h in range(GLA_HEADS):
            ps = slice((h // 2) * LANES, (h // 2 + 1) * LANES)
            mine = low_half if h % 2 == 0 else ~low_half
            qp, kp = q_e[:, ps], k_e[:, ps]
            qh = jnp.where(mine, qp, jnp.zeros_like(qp))
            kth = jnp.where(mine, k_t[:, ps], jnp.zeros_like(qp))
            vs = slice(h * GLA_DV, (h + 1) * GLA_DV)
            vh = vr_ref[rows, vs]
            a = pl.dot(qh, kp, trans_b=True)
            a = jnp.where(causal, a, 0.0).astype(BF16)
            st = st_ref[h]
            o = (jnp.dot(a, vh, preferred_element_type=F32)
                 + pl.dot(qh, st.astype(BF16), trans_b=True))
            st_ref[h] = st * decay[:, ps] + pl.dot(vh, kth, trans_a=True)
            rh = vr_ref[rows, slice(vw + h * GLA_DV, vw + (h + 1) * GLA_DV)].astype(F32)
            o = _rms(o, g) * (rh * jax.nn.sigmoid(rh))
            o_ref[rows, vs] = o.astype(BF16)


def _gla(qk, vr, la, g, batch, seq):
    tl = TL_GLA
    nl = seq // tl
    row = lambda w: pl.BlockSpec((tl, w), lambda b, i: (b * nl + i, 0))
    r = np.arange(GLA_CUM_ROWS)
    cum = jnp.asarray((r[None, :] <= r[:, None])
                      & (r[None, :] // GLA_CHUNK == r[:, None] // GLA_CHUNK), BF16)
    return pl.pallas_call(
        _gla_kernel,
        grid=(batch, nl),
        in_specs=[row(512), row(1024), row(256),
                  pl.BlockSpec((1, GLA_DV), lambda b, i: (0, 0)),
                  pl.BlockSpec((GLA_CUM_ROWS, GLA_CUM_ROWS), lambda b, i: (0, 0))],
        out_specs=row(512),
        out_shape=jax.ShapeDtypeStruct((batch * seq, 512), BF16),
        scratch_shapes=[pltpu.VMEM((GLA_HEADS, GLA_DV, LANES), F32)],
        compiler_params=pltpu.CompilerParams(
            dimension_semantics=("arbitrary", "arbitrary"), vmem_limit_bytes=VMEM_LIMIT),
        name="gla",
    )(qk, vr, la, g, cum)


def _swa_kernel(sink_ref, bias0_ref, bias_ref, q_ref, kvc_ref, kvp_ref, g_ref, o_ref):
    w = SWA_WINDOW
    scale = jnp.asarray(SWA_HEAD_DIM ** -0.5, BF16)
    kv_lane = lax.broadcasted_iota(jnp.int32, (2 * w, LANES), 1)
    lane_lo = kv_lane < SWA_HEAD_DIM
    out_lo = lax.broadcasted_iota(jnp.int32, (w, LANES), 1) < SWA_HEAD_DIM
    ones_hi = jnp.where(kv_lane == SWA_HEAD_DIM, 1.0, 0.0).astype(BF16)
    ones_lo = jnp.where(kv_lane == 0, 1.0, 0.0).astype(BF16)
    g = g_ref[...]

    for sb in range(SWA_SUB):
        rows = slice(sb * w, (sb + 1) * w)
        if sb == 0:
            kv = jnp.concatenate([kvp_ref[...], kvc_ref[0:w, :]], axis=0)
            bias = bias0_ref
        else:
            kv = kvc_ref[(sb - 1) * w:(sb + 1) * w, :]
            bias = bias_ref
        k = kv[:, 0:LANES] * scale
        ks = kv[:, LANES:2 * LANES] * scale
        v, vs = kv[:, 2 * LANES:3 * LANES], kv[:, 3 * LANES:]
        zero = jnp.zeros_like(k)
        k_low = [jnp.where(lane_lo, k, zero), jnp.where(lane_lo, ks, zero)]
        k_high = [jnp.where(lane_lo, zero, ks), jnp.where(lane_lo, zero, k)]
        v_low = [jnp.where(lane_lo, v, zero) + ones_hi, jnp.where(lane_lo, vs, zero) + ones_hi]
        v_high = [jnp.where(lane_lo, zero, vs) + ones_lo, jnp.where(lane_lo, zero, v) + ones_lo]

        def weights(s, head):
            s = s + bias[0, head]
            sink = sink_ref[head]
            m = jnp.maximum(jnp.max(s, axis=-1, keepdims=True), sink)
            return jnp.exp(s - m).astype(BF16), jnp.exp(sink - m)

        for pair in range(SWA_HEADS // 2):
            j = (2 * pair) // (SWA_HEADS // SWA_KV_HEADS)
            cols = slice(pair * LANES, (pair + 1) * LANES)
            qp = q_ref[rows, cols]
            e0, sink0 = weights(pl.dot(qp, k_low[j], trans_b=True), 2 * pair)
            e1, sink1 = weights(pl.dot(qp, k_high[j], trans_b=True), 2 * pair + 1)
            oa = jnp.dot(e0, v_low[j], preferred_element_type=F32)
            ob = jnp.dot(e1, v_high[j], preferred_element_type=F32)
            inv0 = 1.0 / (oa[:, SWA_HEAD_DIM:SWA_HEAD_DIM + 1] + sink0)
            inv1 = 1.0 / (ob[:, 0:1] + sink1)
            o = jnp.where(out_lo, oa * inv0, ob * inv1)
            sq = o * o
            ms_lo = jnp.sum(jnp.where(out_lo, sq, 0.0), axis=-1, keepdims=True)
            ms_hi = jnp.sum(jnp.where(out_lo, 0.0, sq), axis=-1, keepdims=True)
            ms = jnp.where(out_lo, ms_lo, ms_hi) * (1.0 / SWA_HEAD_DIM)
            o = o * lax.rsqrt(ms + NORM_EPS) * g
            o_ref[rows, cols] = o.astype(BF16)


def _swa_bias():
    w = SWA_WINDOW
    slopes = 2.0 ** (-8.0 * np.arange(1, SWA_HEADS + 1, dtype=np.float64) / SWA_HEADS)
    rel = np.arange(w)[:, None] + w - np.arange(2 * w)[None, :]
    in_window = (rel >= 0) & (rel < w)
    exists = np.stack([np.arange(2 * w) >= w, np.ones(2 * w, bool)])
    valid = in_window[None] & exists[:, None, :]
    bias = -slopes[None, :, None, None] * rel[None, None].astype(np.float64)
    return np.where(valid[:, None], bias, NEG_BIG).astype(np.float32)


def _swa(qb, kv, sinks, g2, batch, seq):
    w = SWA_WINDOW
    rows = SWA_SUB * w
    nb = seq // rows
    cur = lambda: pl.BlockSpec((rows, 512), lambda b, n: (b * nb + n, 0))
    prev = pl.BlockSpec(
        (w, 512), lambda b, n: (jnp.maximum((b * nb + n) * SWA_SUB - 1, b * nb * SWA_SUB), 0))
    table = lambda index: pl.BlockSpec((1, SWA_HEADS, w, 2 * w), index)
    bias = jnp.asarray(_swa_bias())
    return pl.pallas_call(
        _swa_kernel,
        grid=(batch, nb),
        in_specs=[pl.BlockSpec(memory_space=pltpu.SMEM),
                  table(lambda b, n: (jnp.minimum(n, 1), 0, 0, 0)),
                  table(lambda b, n: (1, 0, 0, 0)),
                  cur(), cur(), prev,
                  pl.BlockSpec((1, LANES), lambda b, n: (0, 0))],
        out_specs=cur(),
        out_shape=jax.ShapeDtypeStruct((batch * seq, 512), BF16),
        compiler_params=pltpu.CompilerParams(
            dimension_semantics=("arbitrary", "arbitrary"), vmem_limit_bytes=VMEM_LIMIT),
        name="swa",
    )(sinks, bias, bias, qb, kv, kv, g2)


def _out_route_kernel(oa_ref, ob_ref, x_ref, wo_ref, g_ref, wrh_ref, wrl_ref, br_ref,
                      earlier_ref, h13_ref, hn3_ref, route_ref, cnt_ref, hbuf, hsem):
    i = pl.program_id(0)
    tm = TM_ROUTE

    @pl.when(i == 0)
    def _():
        cnt_ref[...] = jnp.zeros_like(cnt_ref)

    def stores(step, slot):
        rows = pl.ds(step * tm, tm)
        return [_SlabTileCopy(hbuf.at[slot, which], out.at[rows], hsem.at[slot], to_slab=True)
                for which, out in enumerate((h13_ref, hn3_ref))]

    slot = i % 2

    @pl.when(i >= 2)
    def _():
        for st in stores(i - 2, slot):
            st.wait()

    h1 = (x_ref[...]
          + jnp.dot(oa_ref[...], wo_ref[:oa_ref.shape[1], :], preferred_element_type=F32)
          + jnp.dot(ob_ref[...], wo_ref[oa_ref.shape[1]:, :], preferred_element_type=F32))
    hn = _rms(h1, g_ref[...])
    hbuf[slot, 0] = h1
    hbuf[slot, 1] = hn
    for st in stores(i, slot):
        st.start()

    @pl.when(i == pl.num_programs(0) - 1)
    def _():
        @pl.when(i >= 1)
        def _():
            for st in stores(i - 1, 1 - slot):
                st.wait()
        for st in stores(i, slot):
            st.wait()

    hn_hi = hn.astype(BF16)
    hn_lo = (hn - hn_hi.astype(F32)).astype(BF16)
    logits = (pl.dot(wrh_ref[...], hn_hi, trans_b=True)
              + pl.dot(wrh_ref[...], hn_lo, trans_b=True)
              + pl.dot(wrl_ref[...], hn_hi, trans_b=True)) + br_ref[...]

    eid = lax.broadcasted_iota(jnp.int32, (N_EXPERTS, tm), 0)
    work = logits
    vals, idxs, sels = [], [], []
    for _ in range(TOP_K):
        m = jnp.max(work, axis=0, keepdims=True)
        idx = jnp.min(jnp.where(work == m, eid, N_EXPERTS), axis=0, keepdims=True)
        sel = eid == idx
        vals.append(m)
        idxs.append(idx)
        sels.append(sel)
        work = jnp.where(sel, -3e38, work)
    exps = [jnp.exp(v - vals[0]) for v in vals]
    inv_den = 1.0 / (exps[0] + exps[1] + exps[2] + exps[3])

    multihot = jnp.where(sels[0] | sels[1] | sels[2] | sels[3], 1.0, 0.0)
    before = (jnp.dot(multihot.astype(BF16), earlier_ref[...], preferred_element_type=F32)
              + cnt_ref[:, 0:1])
    ranks = [jnp.sum(jnp.where(s, before, 0.0), axis=0, keepdims=True) for s in sels]
    route_ref[...] = jnp.concatenate(
        [i.astype(F32) for i in idxs] + [e * inv_den for e in exps] + ranks
        + [jnp.zeros((ROUTE_ROWS - 3 * TOP_K, tm), F32)], axis=0)
    cnt_ref[...] += jnp.sum(multihot, axis=1, keepdims=True)


def _out_route(oa, ob, x2, wo, g2, wr_hi, wr_lo, br):
    T = x2.shape[0]
    tm = TM_ROUTE
    row = lambda w: pl.BlockSpec((tm, w), lambda i: (i, 0))
    full = lambda a: pl.BlockSpec(a.shape, lambda i: (0,) * a.ndim)
    earlier = jnp.asarray(np.triu(np.ones((tm, tm), np.float32), k=1), BF16)
    return pl.pallas_call(
        _out_route_kernel,
        grid=(T // tm,),
        in_specs=[row(512), row(512), row(D_MODEL), full(wo), full(g2),
                  full(wr_hi), full(wr_lo), full(br), full(earlier)],
        out_specs=[pl.BlockSpec(memory_space=pl.ANY), pl.BlockSpec(memory_space=pl.ANY),
                   pl.BlockSpec((ROUTE_ROWS, tm), lambda i: (0, i)),
                   pl.BlockSpec((N_EXPERTS, LANES), lambda i: (0, 0))],
        out_shape=[jax.ShapeDtypeStruct((T, SLAB_SUBLANES, LANES), F32),
                   jax.ShapeDtypeStruct((T, SLAB_SUBLANES, LANES), F32),
                   jax.ShapeDtypeStruct((ROUTE_ROWS, T), F32),
                   jax.ShapeDtypeStruct((N_EXPERTS, LANES), F32)],
        scratch_shapes=[pltpu.VMEM((2, 2, tm, D_MODEL), F32), pltpu.SemaphoreType.DMA((2,))],
        compiler_params=pltpu.CompilerParams(
            dimension_semantics=("arbitrary",), vmem_limit_bytes=VMEM_LIMIT,
            has_side_effects=True),
        name="out_route",
    )(oa, ob, x2, wo, g2, wr_hi, wr_lo, br, earlier)


def _row_copy(src, s, dst, d, sem):
    return pltpu.make_async_copy(src.at[pl.ds(s, 1)], dst.at[pl.ds(d, 1)], sem)


def _dispatch_kernel(dest_ref, pad_start_ref, pad_cnt_ref, nused_ref, hn_ref, xs_ref,
                     zero_ref, sem, zsem):
    i = pl.program_id(0)
    n = pl.num_programs(0)
    tm = EXPERT_PAD
    n_tok = dest_ref.shape[0] // TOP_K
    base = i * TM_DISPATCH
    for p in range(TM_DISPATCH * TOP_K):
        r, k = divmod(p, TOP_K)
        _row_copy(hn_ref, r, xs_ref, dest_ref[base + k * n_tok + r], sem).start(priority=p % 2)
    for _ in range(TOP_K):
        pltpu.make_async_copy(hn_ref, xs_ref.at[pl.ds(0, TM_DISPATCH)], sem).wait()

    @pl.when(i == n - 1)
    def _():
        zero_ref[...] = jnp.zeros_like(zero_ref)
        sub = 8
        for e in range(N_EXPERTS):
            start, cnt = pad_start_ref[e], pad_cnt_ref[e]
            head = cnt & (sub - 1)
            body0 = start + head
            nbody = cnt // sub

            def head_copy(r):
                return _row_copy(zero_ref, 0, xs_ref, start + r, zsem)

            def body_copy(c):
                return pltpu.make_async_copy(zero_ref.at[pl.ds(0, sub)],
                                             xs_ref.at[pl.ds(body0 + c * sub, sub)], zsem)

            for copy, count in ((head_copy, head), (body_copy, nbody)):
                def zissue(r, carry, copy=copy):
                    copy(r).start()
                    return carry

                def zwait(r, carry, copy=copy):
                    copy(r).wait()
                    return carry

                lax.fori_loop(0, count, zissue, 0)
                lax.fori_loop(0, count, zwait, 0)

        def tail_copy(t):
            return pltpu.make_async_copy(zero_ref, xs_ref.at[pl.ds(t * tm, tm)], zsem)

        def tissue(t, carry):
            tail_copy(t).start()
            return carry

        def twait(t, carry):
            tail_copy(t).wait()
            return carry

        n_tiles = xs_ref.shape[0] // tm
        lax.fori_loop(nused_ref[0], n_tiles, tissue, 0)
        lax.fori_loop(nused_ref[0], n_tiles, twait, 0)


def _dispatch(dest, pad_start, pad_cnt, n_used, hn, n_rows):
    T = hn.shape[0]
    any_spec = pl.BlockSpec(memory_space=pl.ANY)
    return pl.pallas_call(
        _dispatch_kernel,
        grid_spec=pltpu.PrefetchScalarGridSpec(
            num_scalar_prefetch=4,
            grid=(T // TM_DISPATCH,),
            in_specs=[pl.BlockSpec((TM_DISPATCH, SLAB_SUBLANES, LANES), lambda i, *_: (i, 0, 0))],
            out_specs=any_spec,
            scratch_shapes=[pltpu.VMEM((EXPERT_PAD, SLAB_SUBLANES, LANES), F32),
                            pltpu.SemaphoreType.DMA(()),
                            pltpu.SemaphoreType.DMA(())]),
        out_shape=jax.ShapeDtypeStruct((n_rows, SLAB_SUBLANES, LANES), F32),
        compiler_params=pltpu.CompilerParams(
            dimension_semantics=("arbitrary",), has_side_effects=True,
            vmem_limit_bytes=VMEM_LIMIT),
        name="dispatch",
    )(dest, pad_start, pad_cnt, n_used, hn)


def _experts_kernel(row0_ref, nfull_ref, tail_ref, xs_ref, wg_ref, bg_ref, wl_ref, bl_ref, wd_ref,
                    bd_ref, ys_ref, wg_bf, wl_bf, wd_bf, xbuf, ybuf, xsem, ysem, wf32, wsem):
    e = pl.program_id(0)
    tm, pad = TM_EXPERT, EXPERT_PAD
    r0, nf, tail = row0_ref[e], nfull_ref[e], tail_ref[e]
    nt = nf + jnp.minimum(tail, 1)

    def for_short_tile(pieces, fn):
        for q in range(1, tm // pad):
            @pl.when(pieces == q)
            def _():
                fn(q * pad)

    def x_load(row, slot, rows):
        return _SlabTileCopy(xbuf.at[slot, pl.ds(0, rows)], xs_ref.at[pl.ds(row, rows)],
                             xsem.at[slot], to_slab=False)

    def y_store(row, slot, rows):
        return _SlabTileCopy(ybuf.at[slot, pl.ds(0, rows)], ys_ref.at[pl.ds(row, rows)],
                             ysem.at[slot], to_slab=True)

    def start_first_load(expert):
        @pl.when(nfull_ref[expert] > 0)
        def _():
            x_load(row0_ref[expert], 0, tm).start()

        @pl.when(nfull_ref[expert] == 0)
        def _():
            for_short_tile(tail_ref[expert], lambda rows: x_load(row0_ref[expert], 0, rows).start())

    def compute(slot, rows):
        x = xbuf[slot, :rows].astype(BF16)
        glu = jnp.minimum(jnp.dot(x, wg_bf[...], preferred_element_type=F32) + bg_ref[0],
                          SWIGLU_LIMIT)
        lin = jnp.clip(jnp.dot(x, wl_bf[...], preferred_element_type=F32) + bl_ref[0],
                       -SWIGLU_LIMIT, SWIGLU_LIMIT)
        hid = glu * jax.nn.sigmoid(SWIGLU_ALPHA * glu) * (lin + 1.0)
        ybuf[slot, :rows] = (jnp.dot(hid.astype(BF16), wd_bf[...], preferred_element_type=F32)
                             + bd_ref[0])

    def w_fetch(expert):
        return [pltpu.make_async_copy(w.at[expert], wf32.at[i], wsem.at[i])
                for i, w in enumerate((wg_ref, wl_ref, wd_ref))]

    @pl.when(e == 0)
    def _():
        start_first_load(0)
        for cp in w_fetch(0):
            cp.start()

    for cp, w_bf, i in zip(w_fetch(e), (wg_bf, wl_bf, wd_bf), range(3)):
        cp.wait()
        w_bf[...] = wf32[i].astype(BF16)

    has_next = e + 1 < pl.num_programs(0)

    def fetch_next(i):
        @pl.when(has_next)
        def _():
            w_fetch(e + 1)[i].start()

    fetch_next(0)

    def full_tile(j, carry):
        slot = j % 2

        @pl.when(j + 1 < nf)
        def _():
            x_load(r0 + (j + 1) * tm, 1 - slot, tm).start()

        @pl.when(j + 1 == nf)
        def _():
            for_short_tile(tail, lambda rows: x_load(r0 + (j + 1) * tm, 1 - slot, rows).start())

        x_load(r0 + j * tm, slot, tm).wait()

        @pl.when(j >= 2)
        def _():
            y_store(r0 + (j - 2) * tm, slot, tm).wait()

        compute(slot, tm)
        y_store(r0 + j * tm, slot, tm).start()
        for i in (1, 2):
            @pl.when(j == i - 1)
            def _():
                fetch_next(i)
        return carry

    lax.fori_loop(0, nf, full_tile, 0)
    for i in (1, 2):
        @pl.when(nf < i)
        def _():
            fetch_next(i)

    def short_tile(rows):
        slot = nf % 2
        x_load(r0 + nf * tm, slot, rows).wait()

        @pl.when(nf >= 2)
        def _():
            y_store(r0 + (nf - 2) * tm, slot, tm).wait()

        compute(slot, rows)
        y_store(r0 + nf * tm, slot, rows).start()

    for_short_tile(tail, short_tile)

    @pl.when(e + 1 < pl.num_programs(0))
    def _():
        start_first_load(e + 1)

    @pl.when(nt >= 2)
    def _():
        y_store(r0 + (nt - 2) * tm, nt % 2, tm).wait()

    @pl.when((nt >= 1) & (tail == 0))
    def _():
        y_store(r0 + (nt - 1) * tm, (nt - 1) % 2, tm).wait()

    for_short_tile(tail, lambda rows: y_store(r0 + nf * tm, nf % 2, rows).wait())

    @pl.when(e == pl.num_programs(0) - 1)
    def _():
        ybuf[0] = jnp.zeros((tm, D_MODEL), F32)
        first = (r0 + nf * tm + tail * pad) // pad
        n_pieces = ys_ref.shape[0] // pad

        def fill_issue(t, carry):
            y_store(t * pad, 0, pad).start()
            return carry

        def fill_wait(t, carry):
            y_store(t * pad, 0, pad).wait()
            return carry

        lax.fori_loop(first, n_pieces, fill_issue, 0)
        lax.fori_loop(first, n_pieces, fill_wait, 0)


def _experts(row0, nfull, tail, xs, w_glu, b_glu, w_lin, b_lin, w_down, b_down):
    tm = TM_EXPERT
    any_spec = pl.BlockSpec(memory_space=pl.ANY)
    bspec = lambda: pl.BlockSpec((1, 1, D_FF), lambda e, *_: (e, 0, 0))
    return pl.pallas_call(
        _experts_kernel,
        grid_spec=pltpu.PrefetchScalarGridSpec(
            num_scalar_prefetch=3,
            grid=(N_EXPERTS,),
            in_specs=[any_spec, any_spec, bspec(), any_spec, bspec(), any_spec, bspec()],
            out_specs=any_spec,
            scratch_shapes=[pltpu.VMEM((D_MODEL, D_FF), BF16)] * 3
            + [pltpu.VMEM((2, tm, D_MODEL), F32), pltpu.VMEM((2, tm, D_MODEL), F32),
               pltpu.SemaphoreType.DMA((2,)), pltpu.SemaphoreType.DMA((2,)),
               pltpu.VMEM((3, D_MODEL, D_FF), F32), pltpu.SemaphoreType.DMA((3,))]),
        out_shape=jax.ShapeDtypeStruct(xs.shape, F32),
        compiler_params=pltpu.CompilerParams(
            dimension_semantics=("arbitrary",), vmem_limit_bytes=VMEM_LIMIT,
            has_side_effects=True),
        name="experts",
    )(row0, nfull, tail, xs, w_glu, b_glu.reshape(N_EXPERTS, 1, D_FF),
      w_lin, b_lin.reshape(N_EXPERTS, 1, D_FF), w_down, b_down.reshape(N_EXPERTS, 1, D_MODEL))


def _combine_kernel(dest_ref, gate_ref, ys_ref, h1_ref, g_ref, out_ref, buf, obuf, sems, osems):
    s = pl.program_id(0)
    tm = TM_COMBINE
    n = out_ref.shape[0] // tm
    n_tok = dest_ref.shape[0] // TOP_K

    def out_store(tile, slot):
        return _SlabTileCopy(out_ref.at[pl.ds(tile * tm, tm)], obuf.at[slot], osems.at[slot],
                             to_slab=False)

    def issue_row(slot, r):
        for k in range(TOP_K):
            _row_copy(ys_ref, dest_ref[s * tm + k * n_tok + r], buf.at[slot, k], r,
                      sems.at[slot]).start(priority=k % 2)

    def reduce_row(slot, r):
        acc = h1_ref[r]
        for k in range(TOP_K):
            acc = acc + gate_ref[(s - 1) * tm + k * n_tok + r] * buf[slot, k, r]
        ss = jnp.sum(jnp.sum(acc * acc, axis=1, keepdims=True), axis=0, keepdims=True)
        obuf[slot, r] = acc * lax.rsqrt(ss * (1.0 / D_MODEL) + NORM_EPS) * g_ref[0]

    def step_body(gather_slot, reduce_slot):
        for r in range(tm):
            if gather_slot is not None:
                issue_row(gather_slot, r)
            if reduce_slot is not None:
                reduce_row(reduce_slot, r)
        if reduce_slot is not None:
            out_store(s - 1, reduce_slot).start()

    for parity in range(2):
        other = 1 - parity

        @pl.when(s % 2 == parity)
        def _():
            @pl.when(s < n)
            def _():
                step_body(parity, None)

            @pl.when(s >= 3)
            def _():
                out_store(s - 3, other).wait()

            @pl.when(s > 0)
            def _():
                for k in range(TOP_K):
                    pltpu.make_async_copy(ys_ref.at[pl.ds(0, tm)], buf.at[other, k],
                                          sems.at[other]).wait()
                step_body(None, other)

            if n % 2 == parity:
                @pl.when(s == n)
                def _():
                    if n >= 2:
                        out_store(n - 2, parity).wait()
                    out_store(n - 1, other).wait()


def _combine(dest, ys3, h13, gate, fg):
    T = h13.shape[0]
    tm = TM_COMBINE
    slab = (SLAB_SUBLANES, LANES)
    prev_tile = lambda s, *_: (jnp.maximum(s - 1, 0), 0, 0)
    return pl.pallas_call(
        _combine_kernel,
        grid_spec=pltpu.PrefetchScalarGridSpec(
            num_scalar_prefetch=2,
            grid=(T // tm + 1,),
            in_specs=[pl.BlockSpec(memory_space=pl.ANY),
                      pl.BlockSpec((tm,) + slab, prev_tile),
                      pl.BlockSpec((1,) + slab, lambda s, *_: (0, 0, 0))],
            out_specs=pl.BlockSpec(memory_space=pl.ANY),
            scratch_shapes=[pltpu.VMEM((2, TOP_K, tm) + slab, F32),
                            pltpu.VMEM((2, tm) + slab, F32),
                            pltpu.SemaphoreType.DMA((2,)),
                            pltpu.SemaphoreType.DMA((2,))]),
        out_shape=jax.ShapeDtypeStruct((T, D_MODEL), F32),
        compiler_params=pltpu.CompilerParams(
            dimension_semantics=("arbitrary",), vmem_limit_bytes=VMEM_LIMIT,
            has_side_effects=True),
        name="combine",
    )(dest, gate, ys3, h13, fg.reshape((1,) + slab))


def _swap_halves(w):
    h = w.shape[-1] // 2
    return jnp.concatenate([w[..., h:], w[..., :h]], axis=-1)


def _layer(x2, batch, seq, norm1_g, w_in, w_alpha_up, b_alpha, gla_norm_g, swa_sinks, swa_norm_g,
           w_out, norm2_g, w_router, b_router, w_glu, b_glu, w_lin, b_lin, w_down, b_down):
    T = x2.shape[0]
    kb_w, vb_w = w_in[:, 2064:2192], w_in[:, 2192:2320]
    w_z = w_in[:, 1536:1552]
    w_cat = jnp.concatenate([
        w_in[:, 0:1536], w_in[:, 1552:2064],
        kb_w, _swap_halves(kb_w), vb_w, _swap_halves(vb_w),
        jnp.pad(jnp.tile(w_z, (1, Z_PIECES)), [(0, 0), (0, LANES - Z_PIECES * GLA_RANK)]),
    ], axis=1).astype(BF16)
    wup_hi = w_alpha_up.astype(BF16)
    wup_lo = (w_alpha_up - wup_hi.astype(F32)).astype(BF16)
    wup_cat = jnp.pad(jnp.concatenate([wup_hi, wup_hi, wup_hi, wup_lo, wup_lo], axis=0),
                      [(0, LANES - Z_PIECES * GLA_RANK), (0, 0)])

    qk, vr, la, qb, kv = _in_proj(
        x2, norm1_g.reshape(1, -1), w_cat, wup_cat, b_alpha.reshape(1, -1))
    oa = _gla(qk, vr, la, gla_norm_g.reshape(1, -1), batch, seq)
    ob = _swa(qb, kv, swa_sinks, jnp.tile(swa_norm_g, 2).reshape(1, -1), batch, seq)

    wo = w_out.astype(BF16)
    wr_t = w_router.T
    wr_hi = wr_t.astype(BF16)
    wr_lo = (wr_t - wr_hi.astype(F32)).astype(BF16)
    h1, hn, route, cnt = _out_route(oa, ob, x2, wo, norm2_g.reshape(1, -1),
                                    wr_hi, wr_lo, b_router.reshape(-1, 1))

    tm, pad = TM_EXPERT, EXPERT_PAD
    n_rows = T * TOP_K + N_EXPERTS * pad
    counts = cnt[:, 0].astype(jnp.int32)
    padded = (counts + pad - 1) // pad * pad
    pends = jnp.cumsum(padded)
    pstarts = pends - padded
    top_idx = route[0:TOP_K].astype(jnp.int32)
    gate = route[TOP_K:2 * TOP_K].reshape(-1)
    rank = route[2 * TOP_K:3 * TOP_K].astype(jnp.int32)
    experts = jnp.arange(N_EXPERTS)[:, None, None]
    seg_start = jnp.sum(jnp.where(top_idx[None] == experts, pstarts[:, None, None], 0), axis=0)
    dest = (seg_start + rank).reshape(-1)
    n_used = (pends[-1] // pad).reshape(1)
    xs = _dispatch(dest, pstarts + counts, padded - counts, n_used, hn, n_rows)
    ys = _experts(pstarts, padded // tm, padded % tm // pad, xs,
                  w_glu, b_glu, w_lin, b_lin, w_down, b_down)
    return dest, ys, h1, gate


def kernel(x, norm1_g, w_in, w_alpha_up, b_alpha, gla_norm_g, swa_sinks, swa_norm_g, w_out,
           norm2_g, w_router, b_router, w_glu, b_glu, w_lin, b_lin, w_down, b_down, final_g):
    batch, seq, d = x.shape
    assert norm1_g.shape[0] == 1, "single-layer problem"
    x2 = x.reshape(batch * seq, d)
    dest, ys, h1, gate = _layer(
        x2, batch, seq, norm1_g[0], w_in[0], w_alpha_up[0], b_alpha[0], gla_norm_g[0],
        swa_sinks[0], swa_norm_g[0], w_out[0], norm2_g[0], w_router[0], b_router[0],
        w_glu[0], b_glu[0], w_lin[0], b_lin[0], w_down[0], b_down[0])
    out = _combine(dest, ys, h1, gate, final_g.reshape(1, -1))
    return out.reshape(batch, seq, d)
```

```python
import numpy as np
import jax
import jax.numpy as jnp
from jax import lax
from jax.experimental import pallas as pl
from jax.experimental.pallas import tpu as pltpu

F32 = jnp.float32
BF16 = jnp.bfloat16

D_MODEL = 1024
GLA_HEADS = 4
GLA_DK = 64
GLA_DV = 128
GLA_RANK = 16
GLA_GATE_TAU = 16.0
GLA_CHUNK = 64
SWA_HEADS = 8
SWA_KV_HEADS = 2
SWA_HEAD_DIM = 64
SWA_WINDOW = 128
N_EXPERTS = 32
TOP_K = 4
D_FF = 1024
SWIGLU_LIMIT = 7.0
SWIGLU_ALPHA = 1.702
NORM_EPS = 1e-5

LANES = 128
SLAB_SUBLANES = D_MODEL // LANES
VMEM_LIMIT = 56 * 1024 * 1024

TM_PROJ = 1024
TL_GLA = 1024
GLA_CUM_ROWS = 256
TM_ROUTE = 1024
TM_EXPERT = 512
EXPERT_PAD = 128
TM_COMBINE = 256
SWA_SUB = 8
TM_DISPATCH = 1024

NEG_BIG = -1e30
ROUTE_ROWS = 16

C_QA, C_KA, C_VA, C_RA, C_QB, C_KB, C_KBS, C_VB, C_VBS, C_Z, C_END = (
    0, 256, 512, 1024, 1536, 2048, 2176, 2304, 2432, 2560, 2688)
Z_PIECES = 5


def _split3(x):
    hi = x.astype(BF16).astype(F32)
    r = x - hi
    mid = r.astype(BF16).astype(F32)
    lo = (r - mid).astype(BF16).astype(F32)
    return hi, mid, lo


class _SlabTileCopy:
    def __init__(self, flat, slab, sem, to_slab):
        self.copies = []
        for c in range(SLAB_SUBLANES):
            pair = (flat.at[:, pl.ds(c * LANES, LANES)], slab.at[:, c, :])
            src, dst = pair if to_slab else pair[::-1]
            self.copies.append(pltpu.make_async_copy(src, dst, sem))

    def start(self):
        for cp in self.copies:
            cp.start()

    def wait(self):
        for cp in self.copies:
            cp.wait()


def _rms(x, g):
    return x * lax.rsqrt(jnp.mean(x * x, axis=-1, keepdims=True) + NORM_EPS) * g


def _in_proj_kernel(x_ref, g_ref, w_ref, wup_ref, ba_ref,
                    qk_ref, vr_ref, la_ref, qb_ref, kv_ref):
    u = _rms(x_ref[...], g_ref[...]).astype(BF16)

    def proj(c0, c1):
        return jnp.dot(u, w_ref[:, c0:c1], preferred_element_type=F32)

    qk_ref[...] = proj(C_QA, C_VA).astype(BF16)
    vr_ref[:, :C_RA - C_VA] = proj(C_VA, C_RA).astype(BF16)
    vr_ref[:, C_RA - C_VA:] = proj(C_RA, C_QB).astype(BF16)
    qb_ref[...] = proj(C_QB, C_KB).astype(BF16)
    kv_ref[...] = proj(C_KB, C_Z).astype(BF16)
    z = proj(C_Z, C_END)
    hi, mid, lo = _split3(z)
    piece = lax.broadcasted_iota(jnp.int32, z.shape, 1) // GLA_RANK
    zc = jnp.where((piece == 0) | (piece == 3), hi, jnp.where(piece == 2, lo, mid)).astype(BF16)
    y = jnp.dot(zc, wup_ref[...], preferred_element_type=F32) + ba_ref[...]
    log_sig = jnp.minimum(y, 0.0) - jnp.log1p(jnp.exp(-jnp.abs(y)))
    la_ref[...] = log_sig * (1.0 / GLA_GATE_TAU)


def _in_proj(x2, g1, w_cat, wup_p, ba_p):
    T = x2.shape[0]
    tm = TM_PROJ
    row = lambda w: pl.BlockSpec((tm, w), lambda i: (i, 0))
    full = lambda a: pl.BlockSpec(a.shape, lambda i: (0,) * a.ndim)
    outs = [(512, BF16), (1024, BF16), (256, F32), (512, BF16), (512, BF16)]
    return pl.pallas_call(
        _in_proj_kernel,
        grid=(T // tm,),
        in_specs=[row(D_MODEL), full(g1), full(w_cat), full(wup_p), full(ba_p)],
        out_specs=[row(w) for w, _ in outs],
        out_shape=[jax.ShapeDtypeStruct((T, w), dt) for w, dt in outs],
        compiler_params=pltpu.CompilerParams(
            dimension_semantics=("arbitrary",), vmem_limit_bytes=VMEM_LIMIT),
        name="in_proj",
    )(x2, g1, w_cat, wup_p, ba_p)


def _gla_kernel(qk_ref, vr_ref, la_ref, g_ref, cum_ref, o_ref, st_ref):
    @pl.when(pl.program_id(1) == 0)
    def _():
        st_ref[...] = jnp.zeros_like(st_ref)

    tl = TL_GLA
    c = GLA_CHUNK
    kw = GLA_HEADS * GLA_DK
    vw = GLA_HEADS * GLA_DV
    causal = (lax.broadcasted_iota(jnp.int32, (c, c), 0)
              >= lax.broadcasted_iota(jnp.int32, (c, c), 1))
    low_half = lax.broadcasted_iota(jnp.int32, (c, LANES), 1) < GLA_DK
    g = g_ref[...]
    b_groups = []
    for grp in range(tl // GLA_CUM_ROWS):
        la = la_ref[grp * GLA_CUM_ROWS:(grp + 1) * GLA_CUM_ROWS, :]
        pieces = jnp.concatenate([p.astype(BF16) for p in _split3(la)], axis=1)
        b3 = jnp.dot(cum_ref[...], pieces, preferred_element_type=F32)
        b_groups.append(b3[:, :kw] + b3[:, kw:2 * kw] + b3[:, 2 * kw:])
    for ch in range(tl // c):
        rows = slice(ch * c, (ch + 1) * c)
        in_grp = (ch * c) % GLA_CUM_ROWS
        b = b_groups[(ch * c) // GLA_CUM_ROWS][in_grp:in_grp + c]
        b_last = b[c - 1:c]
        qf = qk_ref[rows, :kw].astype(F32)
        kf = qk_ref[rows, kw:].astype(F32)
        q_e = (qf * jnp.exp(b) * (GLA_DK ** -0.5)).astype(BF16)
        k_e = (kf * jnp.exp(-b)).astype(BF16)
        k_t = (kf * jnp.exp(b_last - b)).astype(BF16)
        decay = jnp.exp(b_last)
        for h in range(GLA_HEADS):
            ps = slice((h // 2) * LANES, (h // 2 + 1) * LANES)
            mine = low_half if h % 2 == 0 else ~low_half
            qp, kp = q_e[:, ps], k_e[:, ps]
            qh = jnp.where(mine, qp, jnp.zeros_like(qp))
            kth = jnp.where(mine, k_t[:, ps], jnp.zeros_like(qp))
            vs = slice(h * GLA_DV, (h + 1) * GLA_DV)
            vh = vr_ref[rows, vs]
            a = pl.dot(qh, kp, trans_b=True)
            a = jnp.where(causal, a, 0.0).astype(BF16)
            st = st_ref[h]
            o = (jnp.dot(a, vh, preferred_element_type=F32)
                 + pl.dot(qh, st.astype(BF16), trans_b=True))
            st_ref[h] = st * decay[:, ps] + pl.dot(vh, kth, trans_a=True)
            rh = vr_ref[rows, slice(vw + h * GLA_DV, vw + (h + 1) * GLA_DV)].astype(F32)
            o = _rms(o, g) * (rh * jax.nn.sigmoid(rh))
            o_ref[rows, vs] = o.astype(BF16)


def _gla(qk, vr, la, g, batch, seq):
    tl = TL_GLA
    nl = seq // tl
    row = lambda w: pl.BlockSpec((tl, w), lambda b, i: (b * nl + i, 0))
    r = np.arange(GLA_CUM_ROWS)
    cum = jnp.asarray((r[None, :] <= r[:, None])
                      & (r[None, :] // GLA_CHUNK == r[:, None] // GLA_CHUNK), BF16)
    return pl.pallas_call(
        _gla_kernel,
        grid=(batch, nl),
        in_specs=[row(512), row(1024), row(256),
                  pl.BlockSpec((1, GLA_DV), lambda b, i: (0, 0)),
                  pl.BlockSpec((GLA_CUM_ROWS, GLA_CUM_ROWS), lambda b, i: (0, 0))],
        out_specs=row(512),
        out_shape=jax.ShapeDtypeStruct((batch * seq, 512), BF16),
        scratch_shapes=[pltpu.VMEM((GLA_HEADS, GLA_DV, LANES), F32)],
        compiler_params=pltpu.CompilerParams(
            dimension_semantics=("arbitrary", "arbitrary"), vmem_limit_bytes=VMEM_LIMIT),
        name="gla",
    )(qk, vr, la, g, cum)


def _swa_kernel(sink_ref, bias0_ref, bias_ref, q_ref, kvc_ref, kvp_ref, g_ref, o_ref):
    w = SWA_WINDOW
    scale = jnp.asarray(SWA_HEAD_DIM ** -0.5, BF16)
    kv_lane = lax.broadcasted_iota(jnp.int32, (2 * w, LANES), 1)
    lane_lo = kv_lane < SWA_HEAD_DIM
    out_lo = lax.broadcasted_iota(jnp.int32, (w, LANES), 1) < SWA_HEAD_DIM
    ones_hi = jnp.where(kv_lane == SWA_HEAD_DIM, 1.0, 0.0).astype(BF16)
    ones_lo = jnp.where(kv_lane == 0, 1.0, 0.0).astype(BF16)
    g = g_ref[...]

    for sb in range(SWA_SUB):
        rows = slice(sb * w, (sb + 1) * w)
        if sb == 0:
            kv = jnp.concatenate([kvp_ref[...], kvc_ref[0:w, :]], axis=0)
            bias = bias0_ref
        else:
            kv = kvc_ref[(sb - 1) * w:(sb + 1) * w, :]
            bias = bias_ref
        k = kv[:, 0:LANES] * scale
        ks = kv[:, LANES:2 * LANES] * scale
        v, vs = kv[:, 2 * LANES:3 * LANES], kv[:, 3 * LANES:]
        zero = jnp.zeros_like(k)
        k_low = [jnp.where(lane_lo, k, zero), jnp.where(lane_lo, ks, zero)]
        k_high = [jnp.where(lane_lo, zero, ks), jnp.where(lane_lo, zero, k)]
        v_low = [jnp.where(lane_lo, v, zero) + ones_hi, jnp.where(lane_lo, vs, zero) + ones_hi]
        v_high = [jnp.where(lane_lo, zero, vs) + ones_lo, jnp.where(lane_lo, zero, v) + ones_lo]

        def weights(s, head):
            s = s + bias[0, head]
            sink = sink_ref[head]
            m = jnp.maximum(jnp.max(s, axis=-1, keepdims=True), sink)
            return jnp.exp(s - m).astype(BF16), jnp.exp(sink - m)

        for pair in range(SWA_HEADS // 2):
            j = (2 * pair) // (SWA_HEADS // SWA_KV_HEADS)
            cols = slice(pair * LANES, (pair + 1) * LANES)
            qp = q_ref[rows, cols]
            e0, sink0 = weights(pl.dot(qp, k_low[j], trans_b=True), 2 * pair)
            e1, sink1 = weights(pl.dot(qp, k_high[j], trans_b=True), 2 * pair + 1)
            oa = jnp.dot(e0, v_low[j], preferred_element_type=F32)
            ob = jnp.dot(e1, v_high[j], preferred_element_type=F32)
            inv0 = 1.0 / (oa[:, SWA_HEAD_DIM:SWA_HEAD_DIM + 1] + sink0)
            inv1 = 1.0 / (ob[:, 0:1] + sink1)
            o = jnp.where(out_lo, oa * inv0, ob * inv1)
            sq = o * o
            ms_lo = jnp.sum(jnp.where(out_lo, sq, 0.0), axis=-1, keepdims=True)
            ms_hi = jnp.sum(jnp.where(out_lo, 0.0, sq), axis=-1, keepdims=True)
            ms = jnp.where(out_lo, ms_lo, ms_hi) * (1.0 / SWA_HEAD_DIM)
            o = o * lax.rsqrt(ms + NORM_EPS) * g
            o_ref[rows, cols] = o.astype(BF16)


def _swa_bias():
    w = SWA_WINDOW
    slopes = 2.0 ** (-8.0 * np.arange(1, SWA_HEADS + 1, dtype=np.float64) / SWA_HEADS)
    rel = np.arange(w)[:, None] + w - np.arange(2 * w)[None, :]
    in_window = (rel >= 0) & (rel < w)
    exists = np.stack([np.arange(2 * w) >= w, np.ones(2 * w, bool)])
    valid = in_window[None] & exists[:, None, :]
    bias = -slopes[None, :, None, None] * rel[None, None].astype(np.float64)
    return np.where(valid[:, None], bias, NEG_BIG).astype(np.float32)


def _swa(qb, kv, sinks, g2, batch, seq):
    w = SWA_WINDOW
    rows = SWA_SUB * w
    nb = seq // rows
    cur = lambda: pl.BlockSpec((rows, 512), lambda b, n: (b * nb + n, 0))
    prev = pl.BlockSpec(
        (w, 512), lambda b, n: (jnp.maximum((b * nb + n) * SWA_SUB - 1, b * nb * SWA_SUB), 0))
    table = lambda index: pl.BlockSpec((1, SWA_HEADS, w, 2 * w), index)
    bias = jnp.asarray(_swa_bias())
    return pl.pallas_call(
        _swa_kernel,
        grid=(batch, nb),
        in_specs=[pl.BlockSpec(memory_space=pltpu.SMEM),
                  table(lambda b, n: (jnp.minimum(n, 1), 0, 0, 0)),
                  table(lambda b, n: (1, 0, 0, 0)),
                  cur(), cur(), prev,
                  pl.BlockSpec((1, LANES), lambda b, n: (0, 0))],
        out_specs=cur(),
        out_shape=jax.ShapeDtypeStruct((batch * seq, 512), BF16),
        compiler_params=pltpu.CompilerParams(
            dimension_semantics=("arbitrary", "arbitrary"), vmem_limit_bytes=VMEM_LIMIT),
        name="swa",
    )(sinks, bias, bias, qb, kv, kv, g2)


def _out_route_kernel(oa_ref, ob_ref, x_ref, wo_ref, g_ref, wrh_ref, wrl_ref, br_ref,
                      earlier_ref, h13_ref, hn3_ref, route_ref, cnt_ref, hbuf, hsem):
    i = pl.program_id(0)
    tm = TM_ROUTE

    @pl.when(i == 0)
    def _():
        cnt_ref[...] = jnp.zeros_like(cnt_ref)

    def stores(step, slot):
        rows = pl.ds(step * tm, tm)
        return [_SlabTileCopy(hbuf.at[slot, which], out.at[rows], hsem.at[slot], to_slab=True)
                for which, out in enumerate((h13_ref, hn3_ref))]

    slot = i % 2

    @pl.when(i >= 2)
    def _():
        for st in stores(i - 2, slot):
            st.wait()

    h1 = (x_ref[...]
          + jnp.dot(oa_ref[...], wo_ref[:oa_ref.shape[1], :], preferred_element_type=F32)
          + jnp.dot(ob_ref[...], wo_ref[oa_ref.shape[1]:, :], preferred_element_type=F32))
    hn = _rms(h1, g_ref[...])
    hbuf[slot, 0] = h1
    hbuf[slot, 1] = hn
    for st in stores(i, slot):
        st.start()

    @pl.when(i == pl.num_programs(0) - 1)
    def _():
        @pl.when(i >= 1)
        def _():
            for st in stores(i - 1, 1 - slot):
                st.wait()
        for st in stores(i, slot):
            st.wait()

    hn_hi = hn.astype(BF16)
    hn_lo = (hn - hn_hi.astype(F32)).astype(BF16)
    logits = (pl.dot(wrh_ref[...], hn_hi, trans_b=True)
              + pl.dot(wrh_ref[...], hn_lo, trans_b=True)
              + pl.dot(wrl_ref[...], hn_hi, trans_b=True)) + br_ref[...]

    eid = lax.broadcasted_iota(jnp.int32, (N_EXPERTS, tm), 0)
    work = logits
    vals, idxs, sels = [], [], []
    for _ in range(TOP_K):
        m = jnp.max(work, axis=0, keepdims=True)
        idx = jnp.min(jnp.where(work == m, eid, N_EXPERTS), axis=0, keepdims=True)
        sel = eid == idx
        vals.append(m)
        idxs.append(idx)
        sels.append(sel)
        work = jnp.where(sel, -3e38, work)
    exps = [jnp.exp(v - vals[0]) for v in vals]
    inv_den = 1.0 / (exps[0] + exps[1] + exps[2] + exps[3])

    multihot = jnp.where(sels[0] | sels[1] | sels[2] | sels[3], 1.0, 0.0)
    before = (jnp.dot(multihot.astype(BF16), earlier_ref[...], preferred_element_type=F32)
              + cnt_ref[:, 0:1])
    ranks = [jnp.sum(jnp.where(s, before, 0.0), axis=0, keepdims=True) for s in sels]
    route_ref[...] = jnp.concatenate(
        [i.astype(F32) for i in idxs] + [e * inv_den for e in exps] + ranks
        + [jnp.zeros((ROUTE_ROWS - 3 * TOP_K, tm), F32)], axis=0)
    cnt_ref[...] += jnp.sum(multihot, axis=1, keepdims=True)


def _out_route(oa, ob, x2, wo, g2, wr_hi, wr_lo, br):
    T = x2.shape[0]
    tm = TM_ROUTE
    row = lambda w: pl.BlockSpec((tm, w), lambda i: (i, 0))
    full = lambda a: pl.BlockSpec(a.shape, lambda i: (0,) * a.ndim)
    earlier = jnp.asarray(np.triu(np.ones((tm, tm), np.float32), k=1), BF16)
    return pl.pallas_call(
        _out_route_kernel,
        grid=(T // tm,),
        in_specs=[row(512), row(512), row(D_MODEL), full(wo), full(g2),
                  full(wr_hi), full(wr_lo), full(br), full(earlier)],
        out_specs=[pl.BlockSpec(memory_space=pl.ANY), pl.BlockSpec(memory_space=pl.ANY),
                   pl.BlockSpec((ROUTE_ROWS, tm), lambda i: (0, i)),
                   pl.BlockSpec((N_EXPERTS, LANES), lambda i: (0, 0))],
        out_shape=[jax.ShapeDtypeStruct((T, SLAB_SUBLANES, LANES), F32),
                   jax.ShapeDtypeStruct((T, SLAB_SUBLANES, LANES), F32),
                   jax.ShapeDtypeStruct((ROUTE_ROWS, T), F32),
                   jax.ShapeDtypeStruct((N_EXPERTS, LANES), F32)],
        scratch_shapes=[pltpu.VMEM((2, 2, tm, D_MODEL), F32), pltpu.SemaphoreType.DMA((2,))],
        compiler_params=pltpu.CompilerParams(
            dimension_semantics=("arbitrary",), vmem_limit_bytes=VMEM_LIMIT,
            has_side_effects=True),
        name="out_route",
    )(oa, ob, x2, wo, g2, wr_hi, wr_lo, br, earlier)


def _row_copy(src, s, dst, d, sem):
    return pltpu.make_async_copy(src.at[pl.ds(s, 1)], dst.at[pl.ds(d, 1)], sem)


def _dispatch_kernel(dest_ref, pad_start_ref, pad_cnt_ref, nused_ref, hn_ref, xs_ref,
                     zero_ref, sem, zsem):
    i = pl.program_id(0)
    n = pl.num_programs(0)
    tm = EXPERT_PAD
    n_tok = dest_ref.shape[0] // TOP_K
    base = i * TM_DISPATCH
    for p in range(TM_DISPATCH * TOP_K):
        r, k = divmod(p, TOP_K)
        _row_copy(hn_ref, r, xs_ref, dest_ref[base + k * n_tok + r], sem).start(priority=p % 2)
    for _ in range(TOP_K):
        pltpu.make_async_copy(hn_ref, xs_ref.at[pl.ds(0, TM_DISPATCH)], sem).wait()

    @pl.when(i == n - 1)
    def _():
        zero_ref[...] = jnp.zeros_like(zero_ref)
        sub = 8
        for e in range(N_EXPERTS):
            start, cnt = pad_start_ref[e], pad_cnt_ref[e]
            head = cnt & (sub - 1)
            body0 = start + head
            nbody = cnt // sub

            def head_copy(r):
                return _row_copy(zero_ref, 0, xs_ref, start + r, zsem)

            def body_copy(c):
                return pltpu.make_async_copy(zero_ref.at[pl.ds(0, sub)],
                                             xs_ref.at[pl.ds(body0 + c * sub, sub)], zsem)

            for copy, count in ((head_copy, head), (body_copy, nbody)):
                def zissue(r, carry, copy=copy):
                    copy(r).start()
                    return carry

                def zwait(r, carry, copy=copy):
                    copy(r).wait()
                    return carry

                lax.fori_loop(0, count, zissue, 0)
                lax.fori_loop(0, count, zwait, 0)

        def tail_copy(t):
            return pltpu.make_async_copy(zero_ref, xs_ref.at[pl.ds(t * tm, tm)], zsem)

        def tissue(t, carry):
            tail_copy(t).start()
            return carry

        def twait(t, carry):
            tail_copy(t).wait()
            return carry

        n_tiles = xs_ref.shape[0] // tm
        lax.fori_loop(nused_ref[0], n_tiles, tissue, 0)
        lax.fori_loop(nused_ref[0], n_tiles, twait, 0)


def _dispatch(dest, pad_start, pad_cnt, n_used, hn, n_rows):
    T = hn.shape[0]
    any_spec = pl.BlockSpec(memory_space=pl.ANY)
    return pl.pallas_call(
        _dispatch_kernel,
        grid_spec=pltpu.PrefetchScalarGridSpec(
            num_scalar_prefetch=4,
            grid=(T // TM_DISPATCH,),
            in_specs=[pl.BlockSpec((TM_DISPATCH, SLAB_SUBLANES, LANES), lambda i, *_: (i, 0, 0))],
            out_specs=any_spec,
            scratch_shapes=[pltpu.VMEM((EXPERT_PAD, SLAB_SUBLANES, LANES), F32),
                            pltpu.SemaphoreType.DMA(()),
                            pltpu.SemaphoreType.DMA(())]),
        out_shape=jax.ShapeDtypeStruct((n_rows, SLAB_SUBLANES, LANES), F32),
        compiler_params=pltpu.CompilerParams(
            dimension_semantics=("arbitrary",), has_side_effects=True,
            vmem_limit_bytes=VMEM_LIMIT),
        name="dispatch",
    )(dest, pad_start, pad_cnt, n_used, hn)


def _experts_kernel(row0_ref, nfull_ref, tail_ref, xs_ref, wg_ref, bg_ref, wl_ref, bl_ref, wd_ref,
                    bd_ref, ys_ref, wg_bf, wl_bf, wd_bf, xbuf, ybuf, xsem, ysem, wf32, wsem):
    e = pl.program_id(0)
    tm, pad = TM_EXPERT, EXPERT_PAD
    r0, nf, tail = row0_ref[e], nfull_ref[e], tail_ref[e]
    nt = nf + jnp.minimum(tail, 1)

    def for_short_tile(pieces, fn):
        for q in range(1, tm // pad):
            @pl.when(pieces == q)
            def _():
                fn(q * pad)

    def x_load(row, slot, rows):
        return _SlabTileCopy(xbuf.at[slot, pl.ds(0, rows)], xs_ref.at[pl.ds(row, rows)],
                             xsem.at[slot], to_slab=False)

    def y_store(row, slot, rows):
        return _SlabTileCopy(ybuf.at[slot, pl.ds(0, rows)], ys_ref.at[pl.ds(row, rows)],
                             ysem.at[slot], to_slab=True)

    def start_first_load(expert):
        @pl.when(nfull_ref[expert] > 0)
        def _():
            x_load(row0_ref[expert], 0, tm).start()

        @pl.when(nfull_ref[expert] == 0)
        def _():
            for_short_tile(tail_ref[expert], lambda rows: x_load(row0_ref[expert], 0, rows).start())

    def compute(slot, rows):
        x = xbuf[slot, :rows].astype(BF16)
        glu = jnp.minimum(jnp.dot(x, wg_bf[...], preferred_element_type=F32) + bg_ref[0],
                          SWIGLU_LIMIT)
        lin = jnp.clip(jnp.dot(x, wl_bf[...], preferred_element_type=F32) + bl_ref[0],
                       -SWIGLU_LIMIT, SWIGLU_LIMIT)
        hid = glu * jax.nn.sigmoid(SWIGLU_ALPHA * glu) * (lin + 1.0)
        ybuf[slot, :rows] = (jnp.dot(hid.astype(BF16), wd_bf[...], preferred_element_type=F32)
                             + bd_ref[0])

    def w_fetch(expert, i):
        w = (wg_ref, wl_ref, wd_ref)[i]
        return pltpu.make_async_copy(w.at[expert], wf32.at[i], wsem.at[i])

    @pl.when(e == 0)
    def _():
        start_first_load(0)
        for i in range(3):
            w_fetch(0, i).start()

    for i, w_bf in enumerate((wg_bf, wl_bf, wd_bf)):
        w_fetch(e, i).wait()
        w_bf[...] = wf32[i].astype(BF16)

    has_next = e + 1 < pl.num_programs(0)

    def fetch_next(i):
        @pl.when(has_next)
        def _():
            w_fetch(e + 1, i).start()

    fetch_next(0)

    def full_tile(j, carry):
        slot = j % 2

        @pl.when(j + 1 < nf)
        def _():
            x_load(r0 + (j + 1) * tm, 1 - slot, tm).start()

        @pl.when(j + 1 == nf)
        def _():
            for_short_tile(tail, lambda rows: x_load(r0 + (j + 1) * tm, 1 - slot, rows).start())

        x_load(r0 + j * tm, slot, tm).wait()

        @pl.when(j >= 2)
        def _():
            y_store(r0 + (j - 2) * tm, slot, tm).wait()

        compute(slot, tm)
        y_store(r0 + j * tm, slot, tm).start()
        for i in (1, 2):
            @pl.when(j == i - 1)
            def _():
                fetch_next(i)
        return carry

    lax.fori_loop(0, nf, full_tile, 0)
    for i in (1, 2):
        @pl.when(nf < i)
        def _():
            fetch_next(i)

    def short_tile(rows):
        slot = nf % 2
        x_load(r0 + nf * tm, slot, rows).wait()

        @pl.when(nf >= 2)
        def _():
            y_store(r0 + (nf - 2) * tm, slot, tm).wait()

        compute(slot, rows)
        y_store(r0 + nf * tm, slot, rows).start()

    for_short_tile(tail, short_tile)

    @pl.when(e + 1 < pl.num_programs(0))
    def _():
        start_first_load(e + 1)

    @pl.when(nt >= 2)
    def _():
        y_store(r0 + (nt - 2) * tm, nt % 2, tm).wait()

    @pl.when((nt >= 1) & (tail == 0))
    def _():
        y_store(r0 + (nt - 1) * tm, (nt - 1) % 2, tm).wait()

    for_short_tile(tail, lambda rows: y_store(r0 + nf * tm, nf % 2, rows).wait())

    @pl.when(e == pl.num_programs(0) - 1)
    def _():
        ybuf[0] = jnp.zeros((tm, D_MODEL), F32)
        first = (r0 + nf * tm + tail * pad) // pad
        n_pieces = ys_ref.shape[0] // pad

        def fill_issue(t, carry):
            y_store(t * pad, 0, pad).start()
            return carry

        def fill_wait(t, carry):
            y_store(t * pad, 0, pad).wait()
            return carry

        lax.fori_loop(first, n_pieces, fill_issue, 0)
        lax.fori_loop(first, n_pieces, fill_wait, 0)


def _experts(row0, nfull, tail, xs, w_glu, b_glu, w_lin, b_lin, w_down, b_down):
    tm = TM_EXPERT
    any_spec = pl.BlockSpec(memory_space=pl.ANY)
    bspec = lambda: pl.BlockSpec((1, 1, D_FF), lambda e, *_: (e, 0, 0))
    return pl.pallas_call(
        _experts_kernel,
        grid_spec=pltpu.PrefetchScalarGridSpec(
            num_scalar_prefetch=3,
            grid=(N_EXPERTS,),
            in_specs=[any_spec, any_spec, bspec(), any_spec, bspec(), any_spec, bspec()],
            out_specs=any_spec,
            scratch_shapes=[pltpu.VMEM((D_MODEL, D_FF), BF16)] * 3
            + [pltpu.VMEM((2, tm, D_MODEL), F32), pltpu.VMEM((2, tm, D_MODEL), F32),
               pltpu.SemaphoreType.DMA((2,)), pltpu.SemaphoreType.DMA((2,)),
               pltpu.VMEM((3, D_MODEL, D_FF), F32), pltpu.SemaphoreType.DMA((3,))]),
        out_shape=jax.ShapeDtypeStruct(xs.shape, F32),
        compiler_params=pltpu.CompilerParams(
            dimension_semantics=("arbitrary",), vmem_limit_bytes=VMEM_LIMIT,
            has_side_effects=True),
        name="experts",
    )(row0, nfull, tail, xs, w_glu, b_glu.reshape(N_EXPERTS, 1, D_FF),
      w_lin, b_lin.reshape(N_EXPERTS, 1, D_FF), w_down, b_down.reshape(N_EXPERTS, 1, D_MODEL))


def _combine_kernel(dest_ref, gate_ref, ys_ref, h1_ref, g_ref, out_ref, buf, obuf, sems, osems):
    s = pl.program_id(0)
    tm = TM_COMBINE
    n = out_ref.shape[0] // tm
    n_tok = dest_ref.shape[0] // TOP_K

    def out_store(tile, slot):
        return _SlabTileCopy(out_ref.at[pl.ds(tile * tm, tm)], obuf.at[slot], osems.at[slot],
                             to_slab=False)

    def issue_row(slot, r):
        for k in range(TOP_K):
            _row_copy(ys_ref, dest_ref[s * tm + k * n_tok + r], buf.at[slot, k], r,
                      sems.at[slot]).start(priority=k % 2)

    def reduce_row(slot, r):
        acc = h1_ref[r]
        for k in range(TOP_K):
            acc = acc + gate_ref[(s - 1) * tm + k * n_tok + r] * buf[slot, k, r]
        ss = jnp.sum(jnp.sum(acc * acc, axis=1, keepdims=True), axis=0, keepdims=True)
        obuf[slot, r] = acc * lax.rsqrt(ss * (1.0 / D_MODEL) + NORM_EPS) * g_ref[0]

    def step_body(gather_slot, reduce_slot):
        for r in range(tm):
            if gather_slot is not None:
                issue_row(gather_slot, r)
            if reduce_slot is not None:
                reduce_row(reduce_slot, r)
        if reduce_slot is not None:
            out_store(s - 1, reduce_slot).start()

    for parity in range(2):
        other = 1 - parity

        @pl.when(s % 2 == parity)
        def _():
            @pl.when(s < n)
            def _():
                step_body(parity, None)

            @pl.when(s >= 3)
            def _():
                out_store(s - 3, other).wait()

            @pl.when(s > 0)
            def _():
                for k in range(TOP_K):
                    pltpu.make_async_copy(ys_ref.at[pl.ds(0, tm)], buf.at[other, k],
                                          sems.at[other]).wait()
                step_body(None, other)

            if n % 2 == parity:
                @pl.when(s == n)
                def _():
                    if n >= 2:
                        out_store(n - 2, parity).wait()
                    out_store(n - 1, other).wait()


def _combine(dest, ys3, h13, gate, fg):
    T = h13.shape[0]
    tm = TM_COMBINE
    slab = (SLAB_SUBLANES, LANES)
    prev_tile = lambda s, *_: (jnp.maximum(s - 1, 0), 0, 0)
    return pl.pallas_call(
        _combine_kernel,
        grid_spec=pltpu.PrefetchScalarGridSpec(
            num_scalar_prefetch=2,
            grid=(T // tm + 1,),
            in_specs=[pl.BlockSpec(memory_space=pl.ANY),
                      pl.BlockSpec((tm,) + slab, prev_tile),
                      pl.BlockSpec((1,) + slab, lambda s, *_: (0, 0, 0))],
            out_specs=pl.BlockSpec(memory_space=pl.ANY),
            scratch_shapes=[pltpu.VMEM((2, TOP_K, tm) + slab, F32),
                            pltpu.VMEM((2, tm) + slab, F32),
                            pltpu.SemaphoreType.DMA((2,)),
                            pltpu.SemaphoreType.DMA((2,))]),
        out_shape=jax.ShapeDtypeStruct((T, D_MODEL), F32),
        compiler_params=pltpu.CompilerParams(
            dimension_semantics=("arbitrary",), vmem_limit_bytes=VMEM_LIMIT,
            has_side_effects=True),
        name="combine",
    )(dest, gate, ys3, h13, fg.reshape((1,) + slab))


def _swap_halves(w):
    h = w.shape[-1] // 2
    return jnp.concatenate([w[..., h:], w[..., :h]], axis=-1)


def _layer(x2, batch, seq, norm1_g, w_in, w_alpha_up, b_alpha, gla_norm_g, swa_sinks, swa_norm_g,
           w_out, norm2_g, w_router, b_router, w_glu, b_glu, w_lin, b_lin, w_down, b_down):
    T = x2.shape[0]
    kb_w, vb_w = w_in[:, 2064:2192], w_in[:, 2192:2320]
    w_z = w_in[:, 1536:1552]
    w_cat = jnp.concatenate([
        w_in[:, 0:1536], w_in[:, 1552:2064],
        kb_w, _swap_halves(kb_w), vb_w, _swap_halves(vb_w),
        jnp.pad(jnp.tile(w_z, (1, Z_PIECES)), [(0, 0), (0, LANES - Z_PIECES * GLA_RANK)]),
    ], axis=1).astype(BF16)
    wup_hi = w_alpha_up.astype(BF16)
    wup_lo = (w_alpha_up - wup_hi.astype(F32)).astype(BF16)
    wup_cat = jnp.pad(jnp.concatenate([wup_hi, wup_hi, wup_hi, wup_lo, wup_lo], axis=0),
                      [(0, LANES - Z_PIECES * GLA_RANK), (0, 0)])

    qk, vr, la, qb, kv = _in_proj(
        x2, norm1_g.reshape(1, -1), w_cat, wup_cat, b_alpha.reshape(1, -1))
    oa = _gla(qk, vr, la, gla_norm_g.reshape(1, -1), batch, seq)
    ob = _swa(qb, kv, swa_sinks, jnp.tile(swa_norm_g, 2).reshape(1, -1), batch, seq)

    wo = w_out.astype(BF16)
    wr_t = w_router.T
    wr_hi = wr_t.astype(BF16)
    wr_lo = (wr_t - wr_hi.astype(F32)).astype(BF16)
    h1, hn, route, cnt = _out_route(oa, ob, x2, wo, norm2_g.reshape(1, -1),
                                    wr_hi, wr_lo, b_router.reshape(-1, 1))

    tm, pad = TM_EXPERT, EXPERT_PAD
    n_rows = T * TOP_K + N_EXPERTS * pad
    counts = cnt[:, 0].astype(jnp.int32)
    padded = (counts + pad - 1) // pad * pad
    pends = jnp.cumsum(padded)
    pstarts = pends - padded
    top_idx = route[0:TOP_K].astype(jnp.int32)
    gate = route[TOP_K:2 * TOP_K].reshape(-1)
    rank = route[2 * TOP_K:3 * TOP_K].astype(jnp.int32)
    experts = jnp.arange(N_EXPERTS)[:, None, None]
    seg_start = jnp.sum(jnp.where(top_idx[None] == experts, pstarts[:, None, None], 0), axis=0)
    dest = (seg_start + rank).reshape(-1)
    n_used = (pends[-1] // pad).reshape(1)
    xs = _dispatch(dest, pstarts + counts, padded - counts, n_used, hn, n_rows)
    ys = _experts(pstarts, padded // tm, padded % tm // pad, xs,
                  w_glu, b_glu, w_lin, b_lin, w_down, b_down)
    return dest, ys, h1, gate


def kernel(x, norm1_g, w_in, w_alpha_up, b_alpha, gla_norm_g, swa_sinks, swa_norm_g, w_out,
           norm2_g, w_router, b_router, w_glu, b_glu, w_lin, b_lin, w_down, b_down, final_g):
    batch, seq, d = x.shape
    assert norm1_g.shape[0] == 1, "single-layer problem"
    x2 = x.reshape(batch * seq, d)
    dest, ys, h1, gate = _layer(
        x2, batch, seq, norm1_g[0], w_in[0], w_alpha_up[0], b_alpha[0], gla_norm_g[0],
        swa_sinks[0], swa_norm_g[0], w_out[0], norm2_g[0], w_router[0], b_router[0],
        w_glu[0], b_glu[0], w_lin[0], b_lin[0], w_down[0], b_down[0])
    out = _combine(dest, ys, h1, gate, final_g.reshape(1, -1))
    return out.reshape(batch, seq, d)
```
